```python
import jax, jax.numpy as jnp
from jax import lax
import numpy as np

D_MODEL = 2048
BATCH = 8
SEQ = 4096
DEPTH = 1

ATTN_WIDTH = D_MODEL // 2
HEAD_DIM = 64
N_Q_HEADS = ATTN_WIDTH // HEAD_DIM
N_KV_HEADS = 2
GROUP = N_Q_HEADS // N_KV_HEADS
KV_WIDTH = N_KV_HEADS * HEAD_DIM
WINDOW = 128
BLOCK = 128
RNN_WIDTH = D_MODEL - ATTN_WIDTH
RNN_HEAD_DIM = 128
N_RNN_HEADS = RNN_WIDTH // RNN_HEAD_DIM
CHUNK = 64
MIX_WIDTH = ATTN_WIDTH + RNN_WIDTH
D_FF = 4 * D_MODEL
IN_WIDTH = ATTN_WIDTH + 2 * KV_WIDTH + 4 * RNN_WIDTH
SPLITS = tuple(int(s) for s in np.cumsum([ATTN_WIDTH, KV_WIDTH, KV_WIDTH,
                                           RNN_WIDTH, RNN_WIDTH, RNN_WIDTH]))
EPS = 1e-6

kernel_name = "hymba_swa_sink_hgrn2_sqrelu_sandwich"


def rmsnorm(x, gain):
    xf = x.astype(jnp.float32)
    y = xf * lax.rsqrt(jnp.mean(xf * xf, axis=-1, keepdims=True) + EPS)
    return (y * gain.astype(jnp.float32)).astype(x.dtype)


def alibi_slopes(n_heads):
    return jnp.exp2(-8.0 * jnp.arange(1, n_heads + 1, dtype=jnp.float32) / n_heads)


def sliding_window_attention(q, k, v, sinks):
    B, S, _ = q.shape
    nb = S // BLOCK
    qb = q.reshape(B, nb, BLOCK, N_KV_HEADS, GROUP, HEAD_DIM)
    kb = k.reshape(B, nb, BLOCK, N_KV_HEADS, HEAD_DIM)
    vb = v.reshape(B, nb, BLOCK, N_KV_HEADS, HEAD_DIM)
    pad = ((0, 0), (1, 0), (0, 0), (0, 0), (0, 0))
    kcat = jnp.concatenate([jnp.pad(kb, pad)[:, :-1], kb], axis=2)
    vcat = jnp.concatenate([jnp.pad(vb, pad)[:, :-1], vb], axis=2)
    scores = jnp.einsum('bnqhgd,bnkhd->bnhgqk', qb, kcat,
                        preferred_element_type=jnp.float32) * (HEAD_DIM ** -0.5)
    q_pos = jnp.arange(BLOCK) + BLOCK
    k_pos = jnp.arange(2 * BLOCK)
    dist = (q_pos[:, None] - k_pos[None, :]).astype(jnp.float32)
    band = (dist >= 0) & (dist < WINDOW)
    abs_k = jnp.arange(nb)[:, None] * BLOCK - BLOCK + k_pos[None, :]
    valid = band[None] & (abs_k >= 0)[:, None, :]
    slopes = alibi_slopes(N_Q_HEADS).reshape(N_KV_HEADS, GROUP, 1, 1)
    scores = scores - slopes * dist
    scores = jnp.where(valid[None, :, None, None], scores, -jnp.inf)
    sink = sinks.astype(jnp.float32).reshape(N_KV_HEADS, GROUP, 1, 1)
    m = jnp.maximum(jnp.max(scores, axis=-1, keepdims=True), sink)
    p = jnp.exp(scores - m)
    probs = p / (jnp.sum(p, axis=-1, keepdims=True) + jnp.exp(sink - m))
    out = jnp.einsum('bnhgqk,bnkhd->bnqhgd', probs.astype(v.dtype), vcat)
    return out.reshape(B, S, ATTN_WIDTH)


def hgrn2_chunkwise(q, f_logit, i, g, lb, norm_gain):
    B, S, _ = q.shape
    nc = S // CHUNK
    f32 = jnp.float32
    f = lb + (1.0 - lb) * jax.nn.sigmoid(f_logit.astype(f32))
    log_f = jnp.log(f)
    key = 1.0 - f
    qf = jax.nn.silu(q.astype(f32))
    vf = i.astype(f32)

    def to_chunks(t):
        return t.reshape(B, nc, CHUNK, N_RNN_HEADS, RNN_HEAD_DIM).transpose(1, 0, 3, 2, 4)

    causal = jnp.tril(jnp.ones((CHUNK, CHUNK), dtype=bool))

    def step(state, inp):
        qc, kc, vc, lfc = inp
        b = jnp.cumsum(lfc, axis=-2)
        o_inter = jnp.einsum('bhtk,bhkv->bhtv', qc * jnp.exp(b), state)
        diff = b[:, :, :, None, :] - b[:, :, None, :, :]
        decay = jnp.exp(jnp.where(causal[:, :, None], diff, -jnp.inf))
        att = jnp.einsum('bhtk,bhsk,bhtsk->bhts', qc, kc, decay)
        o_intra = jnp.einsum('bhts,bhsv->bhtv', att, vc)
        b_last = b[:, :, -1:, :]
        new_state = (jnp.exp(b_last[:, :, 0, :])[..., None] * state
                     + jnp.einsum('bhsk,bhsv->bhkv', kc * jnp.exp(b_last - b), vc))
        return new_state, o_inter + o_intra

    s0 = jnp.zeros((B, N_RNN_HEADS, RNN_HEAD_DIM, RNN_HEAD_DIM), f32)
    _, o = lax.scan(step, s0, (to_chunks(qf), to_chunks(key), to_chunks(vf), to_chunks(log_f)))
    o = o.transpose(1, 0, 3, 2, 4).reshape(B, S, N_RNN_HEADS, RNN_HEAD_DIM)
    o = o * lax.rsqrt(jnp.mean(o * o, axis=-1, keepdims=True) + EPS) * norm_gain.astype(f32)
    gate = jax.nn.silu(g.astype(f32)).reshape(B, S, N_RNN_HEADS, RNN_HEAD_DIM)
    return (o * gate).reshape(B, S, RNN_WIDTH).astype(q.dtype)


def _fwd_setup_inputs(seed: int = 0) -> dict:
    key = jax.random.key(seed)
    ks = jax.random.split(key, 14)
    f32 = jnp.float32

    def gain(k, shape):
        return 1.0 + 0.05 * jax.random.normal(k, shape, f32)

    return {
        "x": jax.random.normal(ks[0], (BATCH, SEQ, D_MODEL), f32),
        "w_in": jax.random.normal(ks[1], (DEPTH, D_MODEL, IN_WIDTH), f32) * D_MODEL ** -0.5,
        "attn_sinks": 0.5 * jax.random.normal(ks[2], (DEPTH, N_Q_HEADS), f32),
        "attn_out_gain": gain(ks[3], (DEPTH, ATTN_WIDTH)),
        "rnn_lb_logits": 0.1 * jax.random.normal(ks[4], (DEPTH + 1, RNN_WIDTH), f32),
        "rnn_norm_gain": gain(ks[5], (DEPTH, RNN_HEAD_DIM)),
        "w_out": jax.random.normal(ks[6], (DEPTH, MIX_WIDTH, D_MODEL), f32) * MIX_WIDTH ** -0.5,
        "mix_pre_gain": gain(ks[7], (DEPTH, D_MODEL)),
        "mix_post_gain": gain(ks[8], (DEPTH, D_MODEL)),
        "mlp_pre_gain": gain(ks[9], (DEPTH, D_MODEL)),
        "mlp_post_gain": gain(ks[10], (DEPTH, D_MODEL)),
        "w_up": jax.random.normal(ks[11], (DEPTH, D_MODEL, D_FF), f32) * D_MODEL ** -0.5,
        "w_down": jax.random.normal(ks[12], (DEPTH, D_FF, D_MODEL), f32) * D_FF ** -0.5,
    }


def _fwd_reference(x, w_in, attn_sinks, attn_out_gain, rnn_lb_logits, rnn_norm_gain, w_out,
              mix_pre_gain, mix_post_gain, mlp_pre_gain, mlp_post_gain, w_up, w_down):
    lb_all = jnp.cumsum(jax.nn.softmax(rnn_lb_logits.astype(jnp.float32), axis=0), axis=0)
    for layer in range(DEPTH):
        h = rmsnorm(x, mix_pre_gain[layer])
        proj = jnp.einsum('bsd,de->bse', h, w_in[layer])
        q_a, k_a, v_a, q_r, f_r, i_r, g_r = jnp.split(proj, SPLITS, axis=-1)
        attn = sliding_window_attention(q_a, k_a, v_a, attn_sinks[layer])
        attn = rmsnorm(attn, attn_out_gain[layer])
        rnn = hgrn2_chunkwise(q_r, f_r, i_r, g_r, lb_all[layer], rnn_norm_gain[layer])
        mixed = jnp.einsum('bse,ed->bsd', jnp.concatenate([attn, rnn], axis=-1), w_out[layer])
        x = x + rmsnorm(mixed, mix_post_gain[layer])
        h = rmsnorm(x, mlp_pre_gain[layer])
        u = jax.nn.relu(jnp.einsum('bsd,df->bsf', h, w_up[layer]))
        y = jnp.einsum('bsf,fd->bsd', u * u, w_down[layer])
        x = x + rmsnorm(y, mlp_post_gain[layer])
    return x


import jax as _jax
import jax.numpy as _jnp

TWIN_FORMAT = 'train_step'
FWD_PARAMS = ['x', 'w_in', 'attn_sinks', 'attn_out_gain', 'rnn_lb_logits', 'rnn_norm_gain', 'w_out', 'mix_pre_gain', 'mix_post_gain', 'mlp_pre_gain', 'mlp_post_gain', 'w_up', 'w_down']
TWIN_WEIGHTS = ['w_in', 'attn_sinks', 'attn_out_gain', 'rnn_lb_logits', 'rnn_norm_gain', 'w_out', 'mix_pre_gain', 'mix_post_gain', 'mlp_pre_gain', 'mlp_post_gain', 'w_up', 'w_down']
TWIN_DIFF_INPUT = 'x'
TWIN_INPUTS = ['x', 'w_in', 'attn_sinks', 'attn_out_gain', 'rnn_lb_logits', 'rnn_norm_gain', 'w_out', 'mix_pre_gain', 'mix_post_gain', 'mlp_pre_gain', 'mlp_post_gain', 'w_up', 'w_down', 'loss_target', 'm_w_in', 'm_attn_sinks', 'm_attn_out_gain', 'm_rnn_lb_logits', 'm_rnn_norm_gain', 'm_w_out', 'm_mix_pre_gain', 'm_mix_post_gain', 'm_mlp_pre_gain', 'm_mlp_post_gain', 'm_w_up', 'm_w_down', 'v_w_in', 'v_attn_sinks', 'v_attn_out_gain', 'v_rnn_lb_logits', 'v_rnn_norm_gain', 'v_w_out', 'v_mix_pre_gain', 'v_mix_post_gain', 'v_mlp_pre_gain', 'v_mlp_post_gain', 'v_w_up', 'v_w_down']
TWIN_OUTPUTS = ['loss', 'grad_x', 'grad_w_in', 'grad_attn_sinks', 'grad_attn_out_gain', 'grad_rnn_lb_logits', 'grad_rnn_norm_gain', 'grad_w_out', 'grad_mix_pre_gain', 'grad_mix_post_gain', 'grad_mlp_pre_gain', 'grad_mlp_post_gain', 'grad_w_up', 'grad_w_down', 'delta_w_in', 'delta_attn_sinks', 'delta_attn_out_gain', 'delta_rnn_lb_logits', 'delta_rnn_norm_gain', 'delta_w_out', 'delta_mix_pre_gain', 'delta_mix_post_gain', 'delta_mlp_pre_gain', 'delta_mlp_post_gain', 'delta_w_up', 'delta_w_down', 'new_m_w_in', 'new_m_attn_sinks', 'new_m_attn_out_gain', 'new_m_rnn_lb_logits', 'new_m_rnn_norm_gain', 'new_m_w_out', 'new_m_mix_pre_gain', 'new_m_mix_post_gain', 'new_m_mlp_pre_gain', 'new_m_mlp_post_gain', 'new_m_w_up', 'new_m_w_down', 'new_v_w_in', 'new_v_attn_sinks', 'new_v_attn_out_gain', 'new_v_rnn_lb_logits', 'new_v_rnn_norm_gain', 'new_v_w_out', 'new_v_mix_pre_gain', 'new_v_mix_post_gain', 'new_v_mlp_pre_gain', 'new_v_mlp_post_gain', 'new_v_w_up', 'new_v_w_down']
TWIN_LEAF_KINDS = {'loss': 'loss', 'grad_x': 'grad_x', 'grad_w_in': 'grad_w', 'grad_attn_sinks': 'grad_w', 'grad_attn_out_gain': 'grad_w', 'grad_rnn_lb_logits': 'grad_w', 'grad_rnn_norm_gain': 'grad_w', 'grad_w_out': 'grad_w', 'grad_mix_pre_gain': 'grad_w', 'grad_mix_post_gain': 'grad_w', 'grad_mlp_pre_gain': 'grad_w', 'grad_mlp_post_gain': 'grad_w', 'grad_w_up': 'grad_w', 'grad_w_down': 'grad_w', 'delta_w_in': 'delta_w', 'delta_attn_sinks': 'delta_w', 'delta_attn_out_gain': 'delta_w', 'delta_rnn_lb_logits': 'delta_w', 'delta_rnn_norm_gain': 'delta_w', 'delta_w_out': 'delta_w', 'delta_mix_pre_gain': 'delta_w', 'delta_mix_post_gain': 'delta_w', 'delta_mlp_pre_gain': 'delta_w', 'delta_mlp_post_gain': 'delta_w', 'delta_w_up': 'delta_w', 'delta_w_down': 'delta_w', 'new_m_w_in': 'new_m', 'new_m_attn_sinks': 'new_m', 'new_m_attn_out_gain': 'new_m', 'new_m_rnn_lb_logits': 'new_m', 'new_m_rnn_norm_gain': 'new_m', 'new_m_w_out': 'new_m', 'new_m_mix_pre_gain': 'new_m', 'new_m_mix_post_gain': 'new_m', 'new_m_mlp_pre_gain': 'new_m', 'new_m_mlp_post_gain': 'new_m', 'new_m_w_up': 'new_m', 'new_m_w_down': 'new_m', 'new_v_w_in': 'new_v', 'new_v_attn_sinks': 'new_v', 'new_v_attn_out_gain': 'new_v', 'new_v_rnn_lb_logits': 'new_v', 'new_v_rnn_norm_gain': 'new_v', 'new_v_w_out': 'new_v', 'new_v_mix_pre_gain': 'new_v', 'new_v_mix_post_gain': 'new_v', 'new_v_mlp_pre_gain': 'new_v', 'new_v_mlp_post_gain': 'new_v', 'new_v_w_up': 'new_v', 'new_v_w_down': 'new_v'}


def _forward(args):
    return _fwd_reference(*[args[k] for k in FWD_PARAMS])


def _output_shape():
    def fwd():
        inp = _fwd_setup_inputs(0)
        return _fwd_reference(*[inp[k] for k in FWD_PARAMS])
    out = _jax.eval_shape(fwd)
    return out.shape, out.dtype

N_MICROBATCH = 1
ADAM_LR = 0.001
ADAM_B1 = 0.9
ADAM_B2 = 0.999
ADAM_EPS = 1e-08
ADAM_WD = 0.01
ADAM_STEP = 10
PER_EXAMPLE_BATCH_AXIS = {'x': 0, 'loss_target': 0}
SHARED_INPUTS = []
_WEIGHT_DTYPES = {'w_in': _jnp.float32, 'attn_sinks': _jnp.float32, 'attn_out_gain': _jnp.float32, 'rnn_lb_logits': _jnp.float32, 'rnn_norm_gain': _jnp.float32, 'w_out': _jnp.float32, 'mix_pre_gain': _jnp.float32, 'mix_post_gain': _jnp.float32, 'mlp_pre_gain': _jnp.float32, 'mlp_post_gain': _jnp.float32, 'w_up': _jnp.float32, 'w_down': _jnp.float32}
MOMENT_SCALE = {'w_in': 2.123717e-01, 'attn_sinks': 1.020234e+00, 'attn_out_gain': 7.403545e-01, 'rnn_lb_logits': 1.371286e-02, 'rnn_norm_gain': 5.344090e-01, 'w_out': 4.584359e-01, 'mix_pre_gain': 3.661285e-01, 'mix_post_gain': 1.605616e+01, 'mlp_pre_gain': 3.993710e-01, 'mlp_post_gain': 1.650171e+01, 'w_up': 2.101431e-01, 'w_down': 4.801319e-01}


def _to_microbatches(a, axis):
    t = _jnp.moveaxis(a, axis, 0)
    t = t.reshape((N_MICROBATCH, t.shape[0] // N_MICROBATCH) + t.shape[1:])
    return _jnp.moveaxis(t, 1, axis + 1)


def setup_inputs(seed: int = 0) -> dict:
    inp = _fwd_setup_inputs(seed)
    key = _jax.random.fold_in(_jax.random.key(seed), 7919)
    shape, _ = _output_shape()
    out = dict(inp)
    out["loss_target"] = _jax.random.normal(_jax.random.fold_in(key, 0), shape, _jnp.float32)
    for i, name in enumerate(TWIN_WEIGHTS):
        w = inp[name].astype(_jnp.float32)
        if MOMENT_SCALE is None:
            s = _jnp.sqrt(_jnp.mean(_jnp.square(w)) + 1e-30)
        else:
            s = MOMENT_SCALE[name]
        km, kv = _jax.random.split(_jax.random.fold_in(key, i + 1))
        out[name] = w
        out["m_" + name] = s * _jax.random.normal(km, w.shape, _jnp.float32)
        out["v_" + name] = (s * s) * _jax.random.uniform(kv, w.shape, _jnp.float32, 0.5, 1.5)
    if N_MICROBATCH > 1:
        for name, axis in PER_EXAMPLE_BATCH_AXIS.items():
            out[name] = _to_microbatches(out[name], axis)
    return {'x': out['x'], 'w_in': out['w_in'], 'attn_sinks': out['attn_sinks'], 'attn_out_gain': out['attn_out_gain'], 'rnn_lb_logits': out['rnn_lb_logits'], 'rnn_norm_gain': out['rnn_norm_gain'], 'w_out': out['w_out'], 'mix_pre_gain': out['mix_pre_gain'], 'mix_post_gain': out['mix_post_gain'], 'mlp_pre_gain': out['mlp_pre_gain'], 'mlp_post_gain': out['mlp_post_gain'], 'w_up': out['w_up'], 'w_down': out['w_down'], 'loss_target': out['loss_target'], 'm_w_in': out['m_w_in'], 'm_attn_sinks': out['m_attn_sinks'], 'm_attn_out_gain': out['m_attn_out_gain'], 'm_rnn_lb_logits': out['m_rnn_lb_logits'], 'm_rnn_norm_gain': out['m_rnn_norm_gain'], 'm_w_out': out['m_w_out'], 'm_mix_pre_gain': out['m_mix_pre_gain'], 'm_mix_post_gain': out['m_mix_post_gain'], 'm_mlp_pre_gain': out['m_mlp_pre_gain'], 'm_mlp_post_gain': out['m_mlp_post_gain'], 'm_w_up': out['m_w_up'], 'm_w_down': out['m_w_down'], 'v_w_in': out['v_w_in'], 'v_attn_sinks': out['v_attn_sinks'], 'v_attn_out_gain': out['v_attn_out_gain'], 'v_rnn_lb_logits': out['v_rnn_lb_logits'], 'v_rnn_norm_gain': out['v_rnn_norm_gain'], 'v_w_out': out['v_w_out'], 'v_mix_pre_gain': out['v_mix_pre_gain'], 'v_mix_post_gain': out['v_mix_post_gain'], 'v_mlp_pre_gain': out['v_mlp_pre_gain'], 'v_mlp_post_gain': out['v_mlp_post_gain'], 'v_w_up': out['v_w_up'], 'v_w_down': out['v_w_down']}


def _loss(weights, diff, rest, loss_target):
    with _jax.named_scope("forward"):
        args = {**rest, TWIN_DIFF_INPUT: diff, **{k: w.astype(_WEIGHT_DTYPES[k]) for k, w in weights.items()}}
        y = _forward(args)
    with _jax.named_scope("loss_head"):
        err = _jnp.square(y.astype(_jnp.float32) - loss_target)
        return 0.5 * _jnp.sum(_jnp.mean(err, axis=-1)) if err.ndim else 0.5 * err


def _adamw(w, g, m, v):
    m = ADAM_B1 * m + (1.0 - ADAM_B1) * g
    v = ADAM_B2 * v + (1.0 - ADAM_B2) * _jnp.square(g)
    m_hat = m / (1.0 - ADAM_B1 ** ADAM_STEP)
    v_hat = v / (1.0 - ADAM_B2 ** ADAM_STEP)
    delta = -ADAM_LR * (m_hat / (_jnp.sqrt(v_hat) + ADAM_EPS) + ADAM_WD * w)
    return delta, m, v


def reference(x, w_in, attn_sinks, attn_out_gain, rnn_lb_logits, rnn_norm_gain, w_out, mix_pre_gain, mix_post_gain, mlp_pre_gain, mlp_post_gain, w_up, w_down, loss_target, m_w_in, m_attn_sinks, m_attn_out_gain, m_rnn_lb_logits, m_rnn_norm_gain, m_w_out, m_mix_pre_gain, m_mix_post_gain, m_mlp_pre_gain, m_mlp_post_gain, m_w_up, m_w_down, v_w_in, v_attn_sinks, v_attn_out_gain, v_rnn_lb_logits, v_rnn_norm_gain, v_w_out, v_mix_pre_gain, v_mix_post_gain, v_mlp_pre_gain, v_mlp_post_gain, v_w_up, v_w_down):
    given = dict(x=x, w_in=w_in, attn_sinks=attn_sinks, attn_out_gain=attn_out_gain, rnn_lb_logits=rnn_lb_logits, rnn_norm_gain=rnn_norm_gain, w_out=w_out, mix_pre_gain=mix_pre_gain, mix_post_gain=mix_post_gain, mlp_pre_gain=mlp_pre_gain, mlp_post_gain=mlp_post_gain, w_up=w_up, w_down=w_down, loss_target=loss_target, m_w_in=m_w_in, m_attn_sinks=m_attn_sinks, m_attn_out_gain=m_attn_out_gain, m_rnn_lb_logits=m_rnn_lb_logits, m_rnn_norm_gain=m_rnn_norm_gain, m_w_out=m_w_out, m_mix_pre_gain=m_mix_pre_gain, m_mix_post_gain=m_mix_post_gain, m_mlp_pre_gain=m_mlp_pre_gain, m_mlp_post_gain=m_mlp_post_gain, m_w_up=m_w_up, m_w_down=m_w_down, v_w_in=v_w_in, v_attn_sinks=v_attn_sinks, v_attn_out_gain=v_attn_out_gain, v_rnn_lb_logits=v_rnn_lb_logits, v_rnn_norm_gain=v_rnn_norm_gain, v_w_out=v_w_out, v_mix_pre_gain=v_mix_pre_gain, v_mix_post_gain=v_mix_post_gain, v_mlp_pre_gain=v_mlp_pre_gain, v_mlp_post_gain=v_mlp_post_gain, v_w_up=v_w_up, v_w_down=v_w_down)
    weights = {n: given[n] for n in TWIN_WEIGHTS}
    shared = {n: given[n] for n in SHARED_INPUTS}
    per_example = {n: given[n] for n in ['x']}
    grad_fn = _jax.value_and_grad(_loss, argnums=(0, 1))

    def one_microbatch(ex, loss_target):
        ex = dict(ex)
        diff = ex.pop(TWIN_DIFF_INPUT)
        return grad_fn(weights, diff, {**shared, **ex}, loss_target)

    if N_MICROBATCH == 1:
        loss, (grad_w, grad_x) = one_microbatch(per_example, given["loss_target"])
    else:
        def body(carry, xs):
            loss_sum, grad_sum = carry
            l_k, (gw_k, gx_k) = one_microbatch(xs[0], xs[1])
            with _jax.named_scope("update"):
                return (loss_sum + l_k, _jax.tree.map(_jnp.add, grad_sum, gw_k)), gx_k

        init = (_jnp.zeros((), _jnp.float32), _jax.tree.map(_jnp.zeros_like, weights))
        (loss, grad_w), grad_x = _jax.lax.scan(body, init, (per_example, given["loss_target"]))
    with _jax.named_scope("update"):
        delta_w, new_m, new_v = {}, {}, {}
        for n in TWIN_WEIGHTS:
            delta_w[n], new_m[n], new_v[n] = _adamw(weights[n], grad_w[n], given["m_" + n], given["v_" + n])
    return (loss, grad_x, *[grad_w[n] for n in TWIN_WEIGHTS], *[delta_w[n] for n in TWIN_WEIGHTS],
            *[new_m[n] for n in TWIN_WEIGHTS], *[new_v[n] for n in TWIN_WEIGHTS])
```

```python
import functools
import math

import jax
import jax.numpy as jnp
from jax import lax
from jax.experimental import pallas as pl
from jax.experimental.pallas import tpu as pltpu

F32 = jnp.float32
BF16 = jnp.bfloat16

HEAD_DIM = 64
N_KV_HEADS = 2
BLOCK = 128
RNN_HEAD_DIM = 128
CHUNK = 64
SUB = 16
EPS = 1e-6

ADAM_LR = 0.001
ADAM_B1 = 0.9
ADAM_B2 = 0.999
ADAM_EPS = 1e-08
ADAM_WD = 0.01
ADAM_STEP = 10

N_DEV = 8
LANES = 128
V7X_VMEM_LIMIT = 56 * 1024 * 1024
MESH = pl.DeviceIdType.MESH
HI = lax.Precision.HIGHEST


def _cparams(sem=None, **kw):
    return pltpu.CompilerParams(dimension_semantics=sem, vmem_limit_bytes=V7X_VMEM_LIMIT, **kw)


def _dot(a, b, dims):
    return lax.dot_general(a.astype(BF16), b.astype(BF16), (dims, ((), ())), preferred_element_type=F32)


NN = ((1,), (0,))
NT = ((1,), (1,))
TN = ((0,), (0,))


def _pick(n, pref):
    t = min(n, pref)
    while n % t:
        t //= 2
    return t


def _matmul(name, a, b, dims, grid, a_spec, b_spec, acc_shape, out_shapes, out_specs, epilogue,
            extras=(), extra_specs=()):
    nk = grid[2]
    n_extra = len(extras)

    def body(a_ref, b_ref, *rest):
        extra_refs = rest[:n_extra]
        out_refs = rest[n_extra:-1]
        acc = rest[-1]
        k = pl.program_id(2)

        @pl.when(k == 0)
        def _():
            acc[...] = jnp.zeros_like(acc)

        acc[...] += _dot(a_ref[...], b_ref[...], dims)

        @pl.when(k == nk - 1)
        def _():
            epilogue(acc[...], extra_refs, out_refs)

    return pl.pallas_call(
        body, name=name, grid=grid,
        in_specs=[a_spec, b_spec, *extra_specs],
        out_specs=out_specs, out_shape=out_shapes,
        scratch_shapes=[pltpu.VMEM(acc_shape, F32)],
        compiler_params=_cparams(("parallel", "parallel", "arbitrary")),
    )(a, b, *extras)


def _store_as(acc, extra_refs, out_refs):
    out_refs[0][...] = acc.astype(out_refs[0].dtype)


def mm_nn(name, a, b, out_dtype, tm=512, tn=1024, tk=512):
    (m, kk), n = a.shape, b.shape[1]
    tm, tn, tk = _pick(m, tm), _pick(n, tn), _pick(kk, tk)
    return _matmul(name, a, b, NN, (m // tm, n // tn, kk // tk),
                   pl.BlockSpec((tm, tk), lambda i, j, k: (i, k)),
                   pl.BlockSpec((tk, tn), lambda i, j, k: (k, j)),
                   (tm, tn), jax.ShapeDtypeStruct((m, n), out_dtype),
                   pl.BlockSpec((tm, tn), lambda i, j, k: (i, j)), _store_as)


def mm_nt(name, a, b, out_dtype, tm=512, tn=1024, tk=512):
    (m, kk), n = a.shape, b.shape[0]
    tm, tn, tk = _pick(m, tm), _pick(n, tn), _pick(kk, tk)
    return _matmul(name, a, b, NT, (m // tm, n // tn, kk // tk),
                   pl.BlockSpec((tm, tk), lambda i, j, k: (i, k)),
                   pl.BlockSpec((tn, tk), lambda i, j, k: (j, k)),
                   (tm, tn), jax.ShapeDtypeStruct((m, n), out_dtype),
                   pl.BlockSpec((tm, tn), lambda i, j, k: (i, j)), _store_as)


def mm_tn(name, a, b, out_dtype, tm=512, tn=1024, tk=512):
    (kk, m), n = a.shape, b.shape[1]
    tm, tn, tk = _pick(m, tm), _pick(n, tn), _pick(kk, tk)
    return _matmul(name, a, b, TN, (m // tm, n // tn, kk // tk),
                   pl.BlockSpec((tk, tm), lambda i, j, k: (k, i)),
                   pl.BlockSpec((tk, tn), lambda i, j, k: (k, j)),
                   (tm, tn), jax.ShapeDtypeStruct((m, n), out_dtype),
                   pl.BlockSpec((tm, tn), lambda i, j, k: (i, j)), _store_as)


def up_proj(h2, wup_slabs, tm=512, tn=1024, tk=512):
    (m, kk), (_, _, ns) = h2.shape, wup_slabs.shape
    tm, tn, tk = _pick(m, tm), _pick(ns, tn), _pick(kk, tk)
    r = ns // tn
    n = N_DEV * ns

    def epi(acc, extra_refs, out_refs):
        out_refs[0][...] = jnp.maximum(acc, 0.0).astype(BF16)

    return _matmul("up_proj", h2, wup_slabs, NN, (m // tm, n // tn, kk // tk),
                   pl.BlockSpec((tm, tk), lambda i, j, k: (i, k)),
                   pl.BlockSpec((None, tk, tn), lambda i, j, k: (j // r, k, j % r)),
                   (tm, tn), jax.ShapeDtypeStruct((m, n), BF16),
                   pl.BlockSpec((tm, tn), lambda i, j, k: (i, j)), epi)


def down_proj(u, wdown, tm=512, tn=1024, tk=512):
    (m, kk), n = u.shape, wdown.shape[1]
    tm, tn, tk = _pick(m, tm), _pick(n, tn), _pick(kk, tk)
    nk = kk // tk

    def body(u_ref, w_ref, o_ref, acc):
        k = pl.program_id(2)

        @pl.when(k == 0)
        def _():
            acc[...] = jnp.zeros_like(acc)

        uu = u_ref[...]
        acc[...] += _dot(uu * uu, w_ref[...], NN)

        @pl.when(k == nk - 1)
        def _():
            o_ref[...] = acc[...]

    return pl.pallas_call(
        body, name="down_proj", grid=(m // tm, n // tn, nk),
        in_specs=[pl.BlockSpec((tm, tk), lambda i, j, k: (i, k)),
                  pl.BlockSpec((tk, tn), lambda i, j, k: (k, j))],
        out_specs=pl.BlockSpec((tm, tn), lambda i, j, k: (i, j)),
        out_shape=jax.ShapeDtypeStruct((m, n), F32),
        scratch_shapes=[pltpu.VMEM((tm, tn), F32)],
        compiler_params=_cparams(("parallel", "parallel", "arbitrary")),
    )(u, wdown)


def down_bwd_act(dy, wdown, u, tm=512, tn=1024, tk=512):
    (m, kk), n = dy.shape, wdown.shape[0]
    tm, tn, tk = _pick(m, tm), _pick(n, tn), _pick(kk, tk)

    def epi(acc, extra_refs, out_refs):
        out_refs[0][...] = (acc * (2.0 * extra_refs[0][...].astype(F32))).astype(BF16)

    return _matmul("down_bwd_act", dy, wdown, NT, (m // tm, n // tn, kk // tk),
                   pl.BlockSpec((tm, tk), lambda i, j, k: (i, k)),
                   pl.BlockSpec((tn, tk), lambda i, j, k: (j, k)),
                   (tm, tn), jax.ShapeDtypeStruct((m, n), BF16),
                   pl.BlockSpec((tm, tn), lambda i, j, k: (i, j)), epi,
                   extras=(u,), extra_specs=(pl.BlockSpec((tm, tn), lambda i, j, k: (i, j)),))


def down_wgrad(u, dy, tm=512, tn=1024, tk=512):
    (kk, m), n = u.shape, dy.shape[1]
    tm, tn, tk = _pick(m, tm), _pick(n, tn), _pick(kk, tk)
    nk = kk // tk

    def body(u_ref, d_ref, o_ref, acc):
        k = pl.program_id(2)

        @pl.when(k == 0)
        def _():
            acc[...] = jnp.zeros_like(acc)

        uu = u_ref[...]
        acc[...] += _dot(uu * uu, d_ref[...], TN)

        @pl.when(k == nk - 1)
        def _():
            o_ref[...] = acc[...]

    return pl.pallas_call(
        body, name="down_wgrad", grid=(m // tm, n // tn, nk),
        in_specs=[pl.BlockSpec((tk, tm), lambda i, j, k: (k, i)),
                  pl.BlockSpec((tk, tn), lambda i, j, k: (k, j))],
        out_specs=pl.BlockSpec((tm, tn), lambda i, j, k: (i, j)),
        out_shape=jax.ShapeDtypeStruct((m, n), F32),
        scratch_shapes=[pltpu.VMEM((tm, tn), F32)],
        compiler_params=_cparams(("parallel", "parallel", "arbitrary")),
    )(u, dy)


def up_bwd_x(du, wup_slabs, tm=512, tn=1024, tk=512):
    (m, kk), (_, n, ns) = du.shape, wup_slabs.shape
    tm, tn, tk = _pick(m, tm), _pick(n, tn), _pick(ns, tk)
    r = ns // tk
    return _matmul("up_bwd_x", du, wup_slabs, NT, (m // tm, n // tn, kk // tk),
                   pl.BlockSpec((tm, tk), lambda i, j, k: (i, k)),
                   pl.BlockSpec((None, tn, tk), lambda i, j, k: (k // r, j, k % r)),
                   (tm, tn), jax.ShapeDtypeStruct((m, n), F32),
                   pl.BlockSpec((tm, tn), lambda i, j, k: (i, j)), _store_as)


def up_wgrad(h2, du, tm=512, tn=1024, tk=512):
    (kk, m), n = h2.shape, du.shape[1]
    ns = n // N_DEV
    tm, tn, tk = _pick(m, tm), _pick(ns, tn), _pick(kk, tk)
    r = ns // tn
    return _matmul("up_wgrad", h2, du, TN, (m // tm, n // tn, kk // tk),
                   pl.BlockSpec((tk, tm), lambda i, j, k: (k, i)),
                   pl.BlockSpec((tk, tn), lambda i, j, k: (k, j)),
                   (tm, tn), jax.ShapeDtypeStruct((N_DEV, m, ns), F32),
                   pl.BlockSpec((None, tm, tn), lambda i, j, k: (j // r, i, j % r)), _store_as)


def _rstd(x):
    return lax.rsqrt(jnp.mean(x * x, axis=-1, keepdims=True) + EPS)


def _norm_bwd(x, g, dy):
    r = _rstd(x)
    xh = x * r
    dyg = dy * g
    dx = r * (dyg - xh * jnp.mean(dyg * xh, axis=-1, keepdims=True))
    return dx, jnp.sum(dy * xh, axis=0, keepdims=True)


def _row_spec(tr, d):
    return pl.BlockSpec((tr, d), lambda i: (i, 0))


def _vec_spec(d):
    return pl.BlockSpec((1, d), lambda i: (0, 0))


def _accum(ref, val):
    @pl.when(pl.program_id(0) == 0)
    def _():
        ref[...] = jnp.zeros_like(ref)

    ref[...] += val


def pre_norm(x, g, tr=256):
    t, d = x.shape
    tr = _pick(t, tr)

    def body(x_ref, g_ref, h_ref):
        xx = x_ref[...]
        h_ref[...] = (xx * _rstd(xx) * g_ref[...]).astype(BF16)

    return pl.pallas_call(
        body, name="pre_norm", grid=(t // tr,),
        in_specs=[_row_spec(tr, d), _vec_spec(d)], out_specs=_row_spec(tr, d),
        out_shape=jax.ShapeDtypeStruct((t, d), BF16), compiler_params=_cparams(("parallel",)),
    )(x, g)


def mid_fwd(mixed, g_post, x, g_pre2, tr=256):
    t, d = x.shape
    tr = _pick(t, tr)

    def body(m_ref, gp_ref, x_ref, g2_ref, x1_ref, h2_ref):
        mm = m_ref[...]
        x1 = x_ref[...] + mm * _rstd(mm) * gp_ref[...]
        x1_ref[...] = x1
        h2_ref[...] = (x1 * _rstd(x1) * g2_ref[...]).astype(BF16)

    return pl.pallas_call(
        body, name="mid_fwd", grid=(t // tr,),
        in_specs=[_row_spec(tr, d), _vec_spec(d), _row_spec(tr, d), _vec_spec(d)],
        out_specs=[_row_spec(tr, d), _row_spec(tr, d)],
        out_shape=[jax.ShapeDtypeStruct((t, d), F32), jax.ShapeDtypeStruct((t, d), BF16)],
        compiler_params=_cparams(("parallel",)),
    )(mixed, g_post, x, g_pre2)


def loss_bwd(y, g_post2, x1, target, tr=256):
    t, d = y.shape
    tr = _pick(t, tr)

    def body(y_ref, g_ref, x1_ref, t_ref, sse_ref, dout_ref, dy_ref, dg_ref):
        yy = y_ref[...]
        g = g_ref[...]
        err = x1_ref[...] + yy * _rstd(yy) * g - t_ref[...]
        _accum(sse_ref, jnp.sum(jnp.sum(err * err, axis=1, keepdims=True), axis=0, keepdims=True))
        dout = err * (1.0 / d)
        dout_ref[...] = dout
        dy, dg = _norm_bwd(yy, g, dout)
        dy_ref[...] = dy.astype(BF16)
        _accum(dg_ref, dg)

    return pl.pallas_call(
        body, name="loss_bwd", grid=(t // tr,),
        in_specs=[_row_spec(tr, d), _vec_spec(d), _row_spec(tr, d), _row_spec(tr, d)],
        out_specs=[pl.BlockSpec((1, 1), lambda i: (0, 0)), _row_spec(tr, d), _row_spec(tr, d), _vec_spec(d)],
        out_shape=[jax.ShapeDtypeStruct((1, 1), F32), jax.ShapeDtypeStruct((t, d), F32),
                   jax.ShapeDtypeStruct((t, d), BF16), jax.ShapeDtypeStruct((1, d), F32)],
        compiler_params=_cparams(("arbitrary",)),
    )(y, g_post2, x1, target)


def mid_bwd(dh2, x1, g_pre2, dout, mixed, g_post, tr=256):
    t, d = x1.shape
    tr = _pick(t, tr)

    def body(dh_ref, x1_ref, g2_ref, do_ref, m_ref, gp_ref, dx1_ref, dm_ref, dg2_ref, dgp_ref):
        d1, dg2 = _norm_bwd(x1_ref[...], g2_ref[...], dh_ref[...])
        dx1 = do_ref[...] + d1
        dx1_ref[...] = dx1
        dm, dgp = _norm_bwd(m_ref[...], gp_ref[...], dx1)
        dm_ref[...] = dm.astype(BF16)
        _accum(dg2_ref, dg2)
        _accum(dgp_ref, dgp)

    return pl.pallas_call(
        body, name="mid_bwd", grid=(t // tr,),
        in_specs=[_row_spec(tr, d), _row_spec(tr, d), _vec_spec(d), _row_spec(tr, d), _row_spec(tr, d), _vec_spec(d)],
        out_specs=[_row_spec(tr, d), _row_spec(tr, d), _vec_spec(d), _vec_spec(d)],
        out_shape=[jax.ShapeDtypeStruct((t, d), F32), jax.ShapeDtypeStruct((t, d), BF16),
                   jax.ShapeDtypeStruct((1, d), F32), jax.ShapeDtypeStruct((1, d), F32)],
        compiler_params=_cparams(("arbitrary",)),
    )(dh2, x1, g_pre2, dout, mixed, g_post)


def first_bwd(dh1, x, g_pre, dx1, tr=256):
    t, d = x.shape
    tr = _pick(t, tr)

    def body(dh_ref, x_ref, g_ref, dx1_ref, gx_ref, dg_ref):
        d0, dg = _norm_bwd(x_ref[...], g_ref[...], dh_ref[...])
        gx_ref[...] = dx1_ref[...] + d0
        _accum(dg_ref, dg)

    return pl.pallas_call(
        body, name="first_bwd", grid=(t // tr,),
        in_specs=[_row_spec(tr, d), _row_spec(tr, d), _vec_spec(d), _row_spec(tr, d)],
        out_specs=[_row_spec(tr, d), _vec_spec(d)],
        out_shape=[jax.ShapeDtypeStruct((t, d), F32), jax.ShapeDtypeStruct((1, d), F32)],
        compiler_params=_cparams(("arbitrary",)),
    )(dh1, x, g_pre, dx1)


def _attn_geometry():
    r = lax.broadcasted_iota(jnp.int32, (BLOCK, BLOCK), 0)
    c = lax.broadcasted_iota(jnp.int32, (BLOCK, BLOCK), 1)
    dist_cur = (r - c).astype(F32)
    return dist_cur, dist_cur + float(BLOCK), r >= c, c > r


def _attn_probs(qh, kp, kc, slope, sink, geo, has_prev):
    dist_cur, dist_prev, mask_cur, mask_prev = geo
    s_cur = _dot(qh, kc, NT) - slope * dist_cur
    s_prev = _dot(qh, kp, NT) - slope * dist_prev
    s_cur = jnp.where(mask_cur, s_cur, -jnp.inf)
    s_prev = jnp.where(jnp.logical_and(mask_prev, has_prev), s_prev, -jnp.inf)
    m = jnp.maximum(jnp.maximum(jnp.max(s_cur, axis=-1, keepdims=True),
                                jnp.max(s_prev, axis=-1, keepdims=True)), sink)
    p_cur = jnp.exp(s_cur - m)
    p_prev = jnp.exp(s_prev - m)
    p_sink = jnp.exp(sink - m)
    inv = 1.0 / (jnp.sum(p_cur, axis=-1, keepdims=True) + jnp.sum(p_prev, axis=-1, keepdims=True) + p_sink)
    return p_prev * inv, p_cur * inv, p_sink * inv


def attn_fwd(proj, sinks, gain, aw):
    t = proj.shape[0]
    kw = N_KV_HEADS * HEAD_DIM
    n_heads = aw // HEAD_DIM
    group = n_heads // N_KV_HEADS
    nb = t // BLOCK
    scale = HEAD_DIM ** -0.5

    def body(sink_ref, q_ref, k_ref, v_ref, g_ref, o_ref, on_ref):
        n = pl.program_id(0)
        cur = pl.multiple_of(n * BLOCK, BLOCK)
        prev = pl.multiple_of(jnp.maximum(n - 1, 0) * BLOCK, BLOCK)
        has_prev = n > 0
        geo = _attn_geometry()
        kc, kp = k_ref[pl.ds(cur, BLOCK), :], k_ref[pl.ds(prev, BLOCK), :]
        vc, vp = v_ref[pl.ds(cur, BLOCK), :], v_ref[pl.ds(prev, BLOCK), :]
        for h in range(n_heads):
            kv = h // group
            ks = slice(kv * HEAD_DIM, (kv + 1) * HEAD_DIM)
            hs = slice(h * HEAD_DIM, (h + 1) * HEAD_DIM)
            slope = 2.0 ** (-8.0 * (h + 1) / n_heads)
            qh = q_ref[:, hs] * scale
            p_prev, p_cur, _ = _attn_probs(qh, kp[:, ks], kc[:, ks], slope, sink_ref[0, h], geo, has_prev)
            o_ref[:, hs] = _dot(p_prev, vp[:, ks], NN) + _dot(p_cur, vc[:, ks], NN)
        o = o_ref[...]
        on_ref[...] = (o * _rstd(o) * g_ref[...]).astype(BF16)

    return pl.pallas_call(
        body, name="attn_fwd", grid=(nb,),
        in_specs=[pl.BlockSpec(memory_space=pltpu.SMEM),
                  pl.BlockSpec((BLOCK, aw), lambda n: (n, 0)),
                  pl.BlockSpec((t, kw), lambda n: (0, aw // kw)),
                  pl.BlockSpec((t, kw), lambda n: (0, aw // kw + 1)),
                  pl.BlockSpec((1, aw), lambda n: (0, 0))],
        out_specs=[pl.BlockSpec((BLOCK, aw), lambda n: (n, 0)), pl.BlockSpec((BLOCK, aw), lambda n: (n, 0))],
        out_shape=[jax.ShapeDtypeStruct((t, aw), F32), jax.ShapeDtypeStruct((t, aw), BF16)],
        compiler_params=_cparams(("parallel",)),
    )(sinks, proj, proj, proj, gain)


def attn_bwd(proj, sinks, gain, attn_o, dcat, aw):
    t = proj.shape[0]
    kw = N_KV_HEADS * HEAD_DIM
    n_heads = aw // HEAD_DIM
    group = n_heads // N_KV_HEADS
    nb = t // BLOCK
    scale = HEAD_DIM ** -0.5

    def body(sink_ref, q_ref, k_ref, v_ref, g_ref, o_ref, dn_ref, dq_ref, dk_ref, dv_ref, dsink_ref, dg_ref, do_ref):
        n = pl.program_id(0)
        cur = pl.multiple_of(n * BLOCK, BLOCK)
        prev = pl.multiple_of(jnp.maximum(n - 1, 0) * BLOCK, BLOCK)
        has_prev = n > 0
        geo = _attn_geometry()

        @pl.when(n == 0)
        def _():
            dk_ref[...] = jnp.zeros_like(dk_ref)
            dv_ref[...] = jnp.zeros_like(dv_ref)
            dsink_ref[...] = jnp.zeros_like(dsink_ref)

        o = o_ref[...]
        do_all, dg = _norm_bwd(o, g_ref[...], dn_ref[...])
        _accum(dg_ref, dg)
        do_ref[...] = do_all
        kc, kp = k_ref[pl.ds(cur, BLOCK), :], k_ref[pl.ds(prev, BLOCK), :]
        vc, vp = v_ref[pl.ds(cur, BLOCK), :], v_ref[pl.ds(prev, BLOCK), :]
        lane = lax.broadcasted_iota(jnp.int32, (1, LANES), 1)
        dsink = jnp.zeros((1, LANES), F32)
        for kv in range(N_KV_HEADS):
            ks = slice(kv * HEAD_DIM, (kv + 1) * HEAD_DIM)
            dkc = jnp.zeros((BLOCK, HEAD_DIM), F32)
            dkp = jnp.zeros((BLOCK, HEAD_DIM), F32)
            dvc = jnp.zeros((BLOCK, HEAD_DIM), F32)
            dvp = jnp.zeros((BLOCK, HEAD_DIM), F32)
            for gidx in range(group):
                h = kv * group + gidx
                hs = slice(h * HEAD_DIM, (h + 1) * HEAD_DIM)
                slope = 2.0 ** (-8.0 * (h + 1) / n_heads)
                qh = q_ref[:, hs] * scale
                p_prev, p_cur, p_sink = _attn_probs(qh, kp[:, ks], kc[:, ks], slope, sink_ref[0, h], geo, has_prev)
                doh = do_ref[:, hs]
                delta = jnp.sum(doh * o_ref[:, hs], axis=-1, keepdims=True)
                ds_cur = p_cur * (_dot(doh, vc[:, ks], NT) - delta)
                ds_prev = p_prev * (_dot(doh, vp[:, ks], NT) - delta)
                dsink = dsink + jnp.where(lane == h, -jnp.sum(p_sink * delta, axis=0, keepdims=True), 0.0)
                dq_ref[:, hs] = ((_dot(ds_cur, kc[:, ks], NN) + _dot(ds_prev, kp[:, ks], NN)) * scale).astype(BF16)
                dkc = dkc + _dot(ds_cur, qh, TN)
                dkp = dkp + _dot(ds_prev, qh, TN)
                dvc = dvc + _dot(p_cur, doh, TN)
                dvp = dvp + _dot(p_prev, doh, TN)
            dk_ref[pl.ds(cur, BLOCK), ks] += dkc
            dv_ref[pl.ds(cur, BLOCK), ks] += dvc

            @pl.when(has_prev)
            def _():
                dk_ref[pl.ds(prev, BLOCK), ks] += dkp
                dv_ref[pl.ds(prev, BLOCK), ks] += dvp
        dsink_ref[...] += dsink

    return pl.pallas_call(
        body, name="attn_bwd", grid=(nb,),
        in_specs=[pl.BlockSpec(memory_space=pltpu.SMEM),
                  pl.BlockSpec((BLOCK, aw), lambda n: (n, 0)),
                  pl.BlockSpec((t, kw), lambda n: (0, aw // kw)),
                  pl.BlockSpec((t, kw), lambda n: (0, aw // kw + 1)),
                  pl.BlockSpec((1, aw), lambda n: (0, 0)),
                  pl.BlockSpec((BLOCK, aw), lambda n: (n, 0)),
                  pl.BlockSpec((BLOCK, aw), lambda n: (n, 0))],
        out_specs=[pl.BlockSpec((BLOCK, aw), lambda n: (n, 0)),
                   pl.BlockSpec((t, kw), lambda n: (0, 0)), pl.BlockSpec((t, kw), lambda n: (0, 0)),
                   pl.BlockSpec((1, LANES), lambda n: (0, 0)), pl.BlockSpec((1, aw), lambda n: (0, 0))],
        out_shape=[jax.ShapeDtypeStruct((t, aw), BF16), jax.ShapeDtypeStruct((t, kw), F32),
                   jax.ShapeDtypeStruct((t, kw), F32), jax.ShapeDtypeStruct((1, LANES), F32),
                   jax.ShapeDtypeStruct((1, aw), F32)],
        scratch_shapes=[pltpu.VMEM((BLOCK, aw), F32)],
        compiler_params=_cparams(("arbitrary",)),
    )(sinks, proj, proj, proj, gain, attn_o, dcat)


def _sigmoid(x):
    return 1.0 / (1.0 + jnp.exp(-x))


def _chunk_geometry():
    row = lax.broadcasted_iota(jnp.int32, (CHUNK, CHUNK), 0)
    col = lax.broadcasted_iota(jnp.int32, (CHUNK, CHUNK), 1)
    return row, col


def _cumsum_rows(x, reverse=False):
    row, col = _chunk_geometry()
    tri = (col >= row) if reverse else (col <= row)
    return lax.dot_general(tri.astype(F32), x, ((NN), ((), ())), precision=HI, preferred_element_type=F32)


def _rep_sub(x4):
    k = x4.shape[-1]
    return jnp.broadcast_to(x4[:, None, :], (CHUNK // SUB, SUB, k)).reshape(CHUNK, k)


def _gates(q_r, f_r, lb):
    sg = _sigmoid(f_r)
    f = lb + (1.0 - lb) * sg
    sq = _sigmoid(q_r)
    return sg, f, sq, q_r * sq


def _offdiag_terms(b, b_ref, j):
    c = b_ref[pl.ds(j * SUB + SUB - 1, 1), :]
    return jnp.exp(jnp.minimum(b - c, 0.0)), jnp.exp(jnp.minimum(c - b, 0.0))


def _offdiag_mask(j):
    row, col = _chunk_geometry()
    return jnp.logical_and(row >= (j + 1) * SUB, (col // SUB) == j)


def _diag_mask():
    row, col = _chunk_geometry()
    return jnp.logical_and((row // SUB) == (col // SUB), row >= col)


def hgrn_fwd(proj, lb, norm_gain, col0, rw):
    t = proj.shape[0]
    nh = rw // RNN_HEAD_DIM
    nc = t // CHUNK
    kd = RNN_HEAD_DIM
    cb = col0 // kd
    nsub = CHUNK // SUB

    def body(q_ref, f_ref, i_ref, g_ref, lb_ref, ng_ref, rnn_ref, o_ref, att_ref, st_ref, state, b_ref, k_ref):
        c = pl.program_id(1)

        @pl.when(c == 0)
        def _():
            state[...] = jnp.zeros_like(state)

        st_ref[...] = state[...]
        _, f, _, q = _gates(q_ref[...], f_ref[...], lb_ref[...])
        k = 1.0 - f
        v = i_ref[...]
        b = _cumsum_rows(jnp.log(f))
        b_ref[...] = b
        k_ref[...] = k
        row, col = _chunk_geometry()
        att = jnp.zeros((CHUNK, CHUNK), F32)
        for j in range(nsub - 1):
            e_row, e_col = _offdiag_terms(b, b_ref, j)
            att = att + jnp.where(_offdiag_mask(j), _dot(q * e_row, k * e_col, NT), 0.0)
        rloc = lax.broadcasted_iota(jnp.int32, (CHUNK, kd), 0) % SUB
        for r in range(SUB):
            bs = _rep_sub(b_ref[pl.ds(r, nsub, stride=SUB), :])
            ks = _rep_sub(k_ref[pl.ds(r, nsub, stride=SUB), :])
            e = jnp.exp(jnp.where(rloc >= r, b - bs, -jnp.inf))
            colsum = jnp.sum(q * e * ks, axis=-1, keepdims=True)
            att = jnp.where(jnp.logical_and((col % SUB) == r, (row // SUB) == (col // SUB)), colsum, att)
        att_ref[...] = att
        b_last = b_ref[pl.ds(CHUNK - 1, 1), :]
        o = _dot(q * jnp.exp(b), state[...], NT) + _dot(att, v, NN)
        state[...] = state[...] * jnp.exp(b_last) + _dot(v, k * jnp.exp(b_last - b), TN)
        o_ref[...] = o
        gate = g_ref[...]
        gate = gate * _sigmoid(gate)
        rnn_ref[...] = (o * _rstd(o) * ng_ref[...] * gate).astype(BF16)

    def col(kidx):
        return pl.BlockSpec((CHUNK, kd), lambda h, c: (c, cb + kidx * nh + h))

    rnn, o, att, st = pl.pallas_call(
        body, name="hgrn_fwd", grid=(nh, nc),
        in_specs=[col(0), col(1), col(2), col(3),
                  pl.BlockSpec((1, kd), lambda h, c: (0, h)), pl.BlockSpec((1, kd), lambda h, c: (0, 0))],
        out_specs=[pl.BlockSpec((CHUNK, kd), lambda h, c: (c, h)), pl.BlockSpec((CHUNK, kd), lambda h, c: (c, h)),
                   pl.BlockSpec((None, CHUNK, CHUNK), lambda h, c: (h, c, 0)),
                   pl.BlockSpec((None, None, kd, kd), lambda h, c: (c, h, 0, 0))],
        out_shape=[jax.ShapeDtypeStruct((t, rw), BF16), jax.ShapeDtypeStruct((t, rw), F32),
                   jax.ShapeDtypeStruct((nh, t, CHUNK), F32), jax.ShapeDtypeStruct((nc, nh, kd, kd), F32)],
        scratch_shapes=[pltpu.VMEM((kd, kd), F32), pltpu.VMEM((CHUNK, kd), F32), pltpu.VMEM((CHUNK, kd), F32)],
        compiler_params=_cparams(("parallel", "arbitrary")),
    )(proj, proj, proj, proj, lb, norm_gain)
    return rnn, o, att, st


def hgrn_bwd(proj, lb, norm_gain, o_all, att_all, st_all, dcat, col0, rw):
    t = proj.shape[0]
    nh = rw // RNN_HEAD_DIM
    nc = t // CHUNK
    kd = RNN_HEAD_DIM
    cb = col0 // kd
    nsub = CHUNK // SUB
    dcb = (dcat.shape[1] - rw) // kd

    def body(q_ref, f_ref, i_ref, g_ref, lb_ref, ng_ref, o_ref, att_ref, st0_ref, st1_ref, d_ref,
             dq_ref, df_ref, di_ref, dg_ref, dlb_ref, dng_ref, dstate, b_ref, k_ref, dks_ref):
        ci = pl.program_id(1)

        @pl.when(ci == 0)
        def _():
            dstate[...] = jnp.zeros_like(dstate)
            dlb_ref[...] = jnp.zeros_like(dlb_ref)
            dng_ref[...] = jnp.zeros_like(dng_ref)

        lbv = lb_ref[...]
        q_r, g_r = q_ref[...], g_ref[...]
        sg, f, sq, q = _gates(q_r, f_ref[...], lbv)
        k = 1.0 - f
        v = i_ref[...]
        b = _cumsum_rows(jnp.log(f))
        b_ref[...] = b
        k_ref[...] = k
        row, col = _chunk_geometry()

        o = o_ref[...]
        ng = ng_ref[...]
        sgg = _sigmoid(g_r)
        gate = g_r * sgg
        d_rnn = d_ref[...]
        r = _rstd(o)
        oh = o * r
        dg_ref[...] = (d_rnn * oh * ng * (sgg * (1.0 + g_r * (1.0 - sgg)))).astype(BF16)
        d_on = d_rnn * gate
        dng_ref[...] += jnp.sum(d_on * oh, axis=0, keepdims=True)
        dyg = d_on * ng
        do = r * (dyg - oh * jnp.mean(dyg * oh, axis=-1, keepdims=True))

        st0 = st0_ref[...]
        dst = dstate[...]
        b_last = b_ref[pl.ds(CHUNK - 1, 1), :]
        eb = jnp.exp(b)
        kdec = k * jnp.exp(b_last - b)
        att = att_ref[...]
        da = jnp.where(row >= col, _dot(do, v, NT), 0.0)

        dq = _dot(do, st0, NN) * eb
        dk = _dot(v, dst, NN) * jnp.exp(b_last - b)
        dv = _dot(att, do, TN) + _dot(kdec, dst, NT)
        for j in range(nsub - 1):
            e_row, e_col = _offdiag_terms(b, b_ref, j)
            daj = jnp.where(_offdiag_mask(j), da, 0.0)
            dq = dq + e_row * _dot(daj, k * e_col, NN)
            dk = dk + e_col * _dot(daj, q * e_row, TN)
        rloc = lax.broadcasted_iota(jnp.int32, (CHUNK, kd), 0) % SUB
        dad = jnp.where(_diag_mask(), da, 0.0)
        for rr in range(SUB):
            bs = _rep_sub(b_ref[pl.ds(rr, nsub, stride=SUB), :])
            ks = _rep_sub(k_ref[pl.ds(rr, nsub, stride=SUB), :])
            e = jnp.exp(jnp.where(rloc >= rr, b - bs, -jnp.inf))
            dacol = jnp.sum(jnp.where((col % SUB) == rr, dad, 0.0), axis=-1, keepdims=True)
            w = dacol * e
            dq = dq + w * ks
            dks_ref[pl.ds(rr, nsub, stride=SUB), :] = jnp.sum((w * q).reshape(nsub, SUB, kd), axis=1)
        dk = dk + dks_ref[...]

        gsum = jnp.sum(dst * st1_ref[...], axis=0, keepdims=True)
        dlf = _cumsum_rows(q * dq - k * dk, reverse=True) + gsum
        dfv = dlf / f - dk
        df_ref[...] = (dfv * (1.0 - lbv) * sg * (1.0 - sg)).astype(BF16)
        dlb_ref[...] += jnp.sum(dfv * (1.0 - sg), axis=0, keepdims=True)
        dq_ref[...] = (dq * (sq * (1.0 + q_r * (1.0 - sq)))).astype(BF16)
        di_ref[...] = dv.astype(BF16)
        dstate[...] = dst * jnp.exp(b_last) + _dot(do, q * eb, TN)

    def rev(c):
        return nc - 1 - c

    def col_in(kidx):
        return pl.BlockSpec((CHUNK, kd), lambda h, c: (rev(c), cb + kidx * nh + h))

    tile = pl.BlockSpec((CHUNK, kd), lambda h, c: (rev(c), h))
    outs = pl.pallas_call(
        body, name="hgrn_bwd", grid=(nh, nc),
        in_specs=[col_in(0), col_in(1), col_in(2), col_in(3),
                  pl.BlockSpec((1, kd), lambda h, c: (0, h)), pl.BlockSpec((1, kd), lambda h, c: (0, 0)),
                  tile,
                  pl.BlockSpec((None, CHUNK, CHUNK), lambda h, c: (h, rev(c), 0)),
                  pl.BlockSpec((None, None, kd, kd), lambda h, c: (rev(c), h, 0, 0)),
                  pl.BlockSpec((None, None, kd, kd), lambda h, c: (jnp.minimum(rev(c) + 1, nc - 1), h, 0, 0)),
                  pl.BlockSpec((CHUNK, kd), lambda h, c: (rev(c), dcb + h))],
        out_specs=[tile, tile, tile, tile,
                   pl.BlockSpec((1, kd), lambda h, c: (0, h)), pl.BlockSpec((None, 1, kd), lambda h, c: (h, 0, 0))],
        out_shape=[jax.ShapeDtypeStruct((t, rw), BF16)] * 4 + [jax.ShapeDtypeStruct((1, rw), F32),
                                                               jax.ShapeDtypeStruct((nh, 1, kd), F32)],
        scratch_shapes=[pltpu.VMEM((kd, kd), F32), pltpu.VMEM((CHUNK, kd), F32), pltpu.VMEM((CHUNK, kd), F32),
                        pltpu.VMEM((CHUNK, kd), F32)],
        compiler_params=_cparams(("parallel", "arbitrary")),
    )(proj, proj, proj, proj, lb, norm_gain, o_all, att_all, st_all, st_all, dcat)
    return outs


def _local_step(x, target, wint, wout, wup, wdown, sinks, aog, lb, rng, g_mixpre, g_mixpost, g_mlppre, g_mlppost):
    t, d = x.shape
    aw = d // 2
    rw = d - aw
    col0 = aw + 2 * N_KV_HEADS * HEAD_DIM
    h1 = pre_norm(x, g_mixpre)
    proj = mm_nt("in_proj", h1, wint, F32)
    attn_o, attn_n = attn_fwd(proj, sinks, aog, aw)
    rnn, o_r, att, st = hgrn_fwd(proj, lb, rng, col0, rw)
    cat = jnp.concatenate([attn_n, rnn], axis=1)
    mixed = mm_nn("out_proj", cat, wout, F32)
    x1, h2 = mid_fwd(mixed, g_mixpost, x, g_mlppre)
    u = up_proj(h2, wup)
    y = down_proj(u, wdown)
    sse, dout, dy, dg_mlppost = loss_bwd(y, g_mlppost, x1, target)
    du = down_bwd_act(dy, wdown, u)
    dwdown = down_wgrad(u, dy)
    dh2 = up_bwd_x(du, wup)
    dwup = up_wgrad(h2, du)
    dx1, dmixed, dg_mlppre, dg_mixpost = mid_bwd(dh2, x1, g_mlppre, dout, mixed, g_mixpost)
    dcat = mm_nt("out_bwd_x", dmixed, wout, F32)
    dwout = mm_tn("out_wgrad", cat, dmixed, F32)
    dq_r, df_r, di_r, dg_r, dlb, dng = hgrn_bwd(proj, lb, rng, o_r, att, st, dcat, col0, rw)
    dq_a, dk_a, dv_a, dsinks, daog = attn_bwd(proj, sinks, aog, attn_o, dcat, aw)
    dproj = jnp.concatenate([dq_a, dk_a.astype(BF16), dv_a.astype(BF16), dq_r, df_r, di_r, dg_r], axis=1)
    dh1 = mm_nn("in_bwd_x", dproj, wint, F32)
    dwint = mm_tn("in_wgrad", dproj, h1, F32)
    grad_x, dg_mixpre = first_bwd(dh1, x, g_mixpre, dx1)
    small = dict(sse=sse, sinks=dsinks, aog=daog, lb=dlb, rng=dng, mixpre=dg_mixpre, mixpost=dg_mixpost,
                 mlppre=dg_mlppre, mlppost=dg_mlppost)
    return grad_x, dwint, dwout, dwup, dwdown, small


ANY = pl.BlockSpec(memory_space=pl.ANY)


def _coords():
    return lax.axis_index("x"), lax.axis_index("y"), lax.axis_index("c")


def _slab_index(dev):
    return 4 * dev[0] + 2 * dev[1] + dev[2]


def all_gather_slabs(shards):
    n = len(shards)

    def body(*refs):
        ins, outs = refs[:n], refs[n:2 * n]
        send_sems, recv_sems, local_sems = refs[2 * n:]
        x, y, c = _coords()
        me, sibling = (x, y, c), (x, y, 1 - c)
        chips = [(1 - x, y), (x, 1 - y), (1 - x, 1 - y)]

        def copy(a, k, block, to, src=None):
            slab = outs[a].at[_slab_index(block)]
            return pltpu.make_async_remote_copy(
                src_ref=slab if src is None else src, dst_ref=slab,
                send_sem=send_sems.at[a, k], recv_sem=recv_sems.at[a, k],
                device_id=to, device_id_type=MESH)

        mine = [pltpu.make_async_copy(ins[a], outs[a].at[_slab_index(me)], local_sems.at[a]) for a in range(n)]
        for cp in mine:
            cp.start()
        first = []
        for a in range(n):
            first.append(copy(a, 0, me, sibling, src=ins[a]))
            first += [copy(a, 1 + j, me, (*chip, c), src=ins[a]) for j, chip in enumerate(chips)]
        for cp in first:
            cp.start()
        passed = []
        for j, chip in enumerate(chips):
            for a in range(n):
                copy(a, 1 + j, (*chip, c), me).wait_recv()
                fwd = copy(a, 4 + j, (*chip, c), sibling)
                fwd.start()
                passed.append(fwd)
        for a in range(n):
            copy(a, 0, sibling, me).wait_recv()
            for j, chip in enumerate(chips):
                copy(a, 4 + j, (*chip, 1 - c), me).wait_recv()
        for cp in first + passed:
            cp.wait_send()
        for cp in mine:
            cp.wait()

    return pl.pallas_call(
        body, name="all_gather_weights",
        in_specs=[ANY] * n, out_specs=[ANY] * n,
        out_shape=[jax.ShapeDtypeStruct((N_DEV, *s.shape), s.dtype) for s in shards],
        scratch_shapes=[pltpu.SemaphoreType.DMA((n, 7)), pltpu.SemaphoreType.DMA((n, 7)),
                        pltpu.SemaphoreType.DMA((n,))],
        compiler_params=pltpu.CompilerParams(has_side_effects=True),
    )(*shards)


_AXES = ("x", "y", "c")


def exchange_halves(name, arrays, axis):
    n = len(arrays)
    minor = axis == "c"
    pieces = arrays[0].shape[0] if minor else arrays[0].shape[1]

    def body(*refs):
        ins, outs = refs[:n], refs[n:2 * n]
        send_sems, recv_sems = refs[2 * n:]
        coords = list(_coords())
        ai = _AXES.index(axis)
        mine = coords[ai]
        peer = list(coords)
        peer[ai] = 1 - mine
        copies = []
        for a in range(n):
            for p in range(pieces):
                src = ins[a].at[p, 1 - mine] if minor else ins[a].at[1 - mine, p]
                copies.append(pltpu.make_async_remote_copy(
                    src_ref=src, dst_ref=outs[a].at[p],
                    send_sem=send_sems.at[a, p], recv_sem=recv_sems.at[a, p],
                    device_id=tuple(peer), device_id_type=MESH))
        for cp in copies:
            cp.start()
        for cp in copies:
            cp.wait()

    return pl.pallas_call(
        body, name=name,
        in_specs=[ANY] * n, out_specs=[ANY] * n,
        out_shape=[jax.ShapeDtypeStruct((pieces, *a.shape[2:]), a.dtype) for a in arrays],
        scratch_shapes=[pltpu.SemaphoreType.DMA((n, pieces)), pltpu.SemaphoreType.DMA((n, pieces))],
        compiler_params=pltpu.CompilerParams(has_side_effects=True),
    )(*arrays)


def add_kept_half(name, kept, got, sel, axis, tr=256):
    minor = axis == "c"
    pieces, rows, cols = got.shape
    tr = _pick(rows, tr)

    def body(sel_ref, k_ref, g_ref, o_ref):
        o_ref[...] = k_ref[...] + g_ref[...]

    kept_spec = (pl.BlockSpec((None, None, tr, cols), lambda p, i, s: (p, s[0], i, 0)) if minor else
                 pl.BlockSpec((None, None, tr, cols), lambda p, i, s: (s[0], p, i, 0)))
    return pl.pallas_call(
        body, name=name,
        grid_spec=pltpu.PrefetchScalarGridSpec(
            num_scalar_prefetch=1, grid=(pieces, rows // tr),
            in_specs=[kept_spec, pl.BlockSpec((None, tr, cols), lambda p, i, s: (p, i, 0))],
            out_specs=pl.BlockSpec((None, tr, cols), lambda p, i, s: (p, i, 0))),
        out_shape=jax.ShapeDtypeStruct(got.shape, got.dtype),
        compiler_params=_cparams(("parallel", "parallel")),
    )(sel, kept, got)


def _adamw(w, g, m, v):
    m = ADAM_B1 * m + (1.0 - ADAM_B1) * g
    v = ADAM_B2 * v + (1.0 - ADAM_B2) * (g * g)
    m_hat = m / (1.0 - ADAM_B1 ** ADAM_STEP)
    v_hat = v / (1.0 - ADAM_B2 ** ADAM_STEP)
    delta = -ADAM_LR * (m_hat / (jnp.sqrt(v_hat) + ADAM_EPS) + ADAM_WD * w)
    return delta, m, v


def add_adamw(name, kept, got, sel, w, m, v, tr=128):
    rows, cols = w.shape
    tr = _pick(rows, tr)

    def body(sel_ref, k_ref, g_ref, w_ref, m_ref, v_ref, go_ref, d_ref, mo_ref, vo_ref):
        g = k_ref[...] + g_ref[...]
        go_ref[...] = g
        d_ref[...], mo_ref[...], vo_ref[...] = _adamw(w_ref[...], g, m_ref[...], v_ref[...])

    tile = pl.BlockSpec((tr, cols), lambda i, s: (i, 0))
    return pl.pallas_call(
        body, name=name,
        grid_spec=pltpu.PrefetchScalarGridSpec(
            num_scalar_prefetch=1, grid=(rows // tr,),
            in_specs=[pl.BlockSpec((None, None, tr, cols), lambda i, s: (s[0], 0, i, 0)),
                      pl.BlockSpec((None, tr, cols), lambda i, s: (0, i, 0)), tile, tile, tile],
            out_specs=[tile] * 4),
        out_shape=[jax.ShapeDtypeStruct((rows, cols), F32)] * 4,
        compiler_params=_cparams(("parallel",)),
    )(sel, kept, got, w, m, v)


def small_allreduce_adamw(partial, scale, w, m, v):
    rows = partial.shape[0]

    def body(p_ref, s_ref, w_ref, m_ref, v_ref, g_ref, d_ref, mo_ref, vo_ref, slots, send_sems, recv_sems):
        x, y, c = _coords()
        my_slot = _slab_index((x, y, c))
        slots[my_slot] = p_ref[...]
        copies = []
        for mask in range(1, N_DEV):
            to = tuple(1 - v_ if (mask >> s_) & 1 else v_ for v_, s_ in ((x, 2), (y, 1), (c, 0)))
            copies.append(pltpu.make_async_remote_copy(
                src_ref=p_ref, dst_ref=slots.at[my_slot],
                send_sem=send_sems.at[mask - 1], recv_sem=recv_sems.at[mask - 1],
                device_id=to, device_id_type=MESH))
        for cp in copies:
            cp.start()
        for cp in copies:
            cp.wait()
        total = slots[0]
        for b in range(1, N_DEV):
            total = total + slots[b]
        g = total * s_ref[...]
        g_ref[...] = g
        d_ref[...], mo_ref[...], vo_ref[...] = _adamw(w_ref[...], g, m_ref[...], v_ref[...])

    vm = pl.BlockSpec(memory_space=pltpu.VMEM)
    return pl.pallas_call(
        body, name="small_allreduce_adamw",
        in_specs=[vm] * 5, out_specs=[vm] * 4,
        out_shape=[jax.ShapeDtypeStruct((rows, LANES), F32)] * 4,
        scratch_shapes=[pltpu.VMEM((N_DEV, rows, LANES), F32),
                        pltpu.SemaphoreType.DMA((N_DEV - 1,)), pltpu.SemaphoreType.DMA((N_DEV - 1,))],
        compiler_params=pltpu.CompilerParams(has_side_effects=True),
    )(partial, scale, w, m, v)


def reduce_scatter_adamw(grads, ws, ms, vs):
    x, y, c = _coords()
    sel_x, sel_y, sel_c = (jnp.reshape(v_, (1,)).astype(jnp.int32) for v_ in (x, y, c))
    n = len(grads)
    shapes = [g.shape[1:] for g in grads]
    by_c = [g.reshape(4, 2, *s) for g, s in zip(grads, shapes)]
    got = exchange_halves("rs_exchange_c", by_c, "c")
    s1 = [add_kept_half("rs_add_c_%d" % a, by_c[a], got[a], sel_c, "c") for a in range(n)]
    by_x = [g.reshape(2, 2, *s) for g, s in zip(s1, shapes)]
    got = exchange_halves("rs_exchange_x", by_x, "x")
    s2 = [add_kept_half("rs_add_x_%d" % a, by_x[a], got[a], sel_x, "x") for a in range(n)]
    by_y = [g.reshape(2, 1, *s) for g, s in zip(s2, shapes)]
    got = exchange_halves("rs_exchange_y", by_y, "y")
    return [add_adamw("rs_add_y_adamw_%d" % a, by_y[a], got[a], sel_y, ws[a], ms[a], vs[a]) for a in range(n)]


_SMALL = ("attn_sinks", "attn_out_gain", "rnn_lb_logits", "rnn_norm_gain", "mix_pre_gain", "mix_post_gain",
          "mlp_pre_gain", "mlp_post_gain")


def _pack(parts):
    rows = []
    for p in parts:
        flat = p.reshape(-1).astype(F32)
        pad = (-flat.shape[0]) % LANES
        rows.append(jnp.pad(flat, (0, pad)).reshape(-1, LANES))
    packed = jnp.concatenate(rows, axis=0)
    pad_rows = (-packed.shape[0]) % 8
    return jnp.pad(packed, ((0, pad_rows), (0, 0)))


def _unpack(packed, shapes):
    out, r = [], 0
    for s in shapes:
        size = math.prod(s)
        nrows = -(-size // LANES)
        out.append(packed[r:r + nrows].reshape(-1)[:size].reshape(s))
        r += nrows
    return out


def kernel(x, w_in, attn_sinks, attn_out_gain, rnn_lb_logits, rnn_norm_gain, w_out, mix_pre_gain, mix_post_gain, mlp_pre_gain, mlp_post_gain, w_up, w_down, loss_target, m_w_in, m_attn_sinks, m_attn_out_gain, m_rnn_lb_logits, m_rnn_norm_gain, m_w_out, m_mix_pre_gain, m_mix_post_gain, m_mlp_pre_gain, m_mlp_post_gain, m_w_up, m_w_down, v_w_in, v_attn_sinks, v_attn_out_gain, v_rnn_lb_logits, v_rnn_norm_gain, v_w_out, v_mix_pre_gain, v_mix_post_gain, v_mlp_pre_gain, v_mlp_post_gain, v_w_up, v_w_down):
    xs, target = x[0], loss_target[0]
    d = xs.shape[1]
    small_w = dict(attn_sinks=attn_sinks, attn_out_gain=attn_out_gain, rnn_lb_logits=rnn_lb_logits,
                   rnn_norm_gain=rnn_norm_gain, mix_pre_gain=mix_pre_gain, mix_post_gain=mix_post_gain,
                   mlp_pre_gain=mlp_pre_gain, mlp_post_gain=mlp_post_gain)
    small_m = dict(attn_sinks=m_attn_sinks, attn_out_gain=m_attn_out_gain, rnn_lb_logits=m_rnn_lb_logits,
                   rnn_norm_gain=m_rnn_norm_gain, mix_pre_gain=m_mix_pre_gain, mix_post_gain=m_mix_post_gain,
                   mlp_pre_gain=m_mlp_pre_gain, mlp_post_gain=m_mlp_post_gain)
    small_v = dict(attn_sinks=v_attn_sinks, attn_out_gain=v_attn_out_gain, rnn_lb_logits=v_rnn_lb_logits,
                   rnn_norm_gain=v_rnn_norm_gain, mix_pre_gain=v_mix_pre_gain, mix_post_gain=v_mix_post_gain,
                   mlp_pre_gain=v_mlp_pre_gain, mlp_post_gain=v_mlp_post_gain)

    big_w = [w_in[0].T, w_out[0], w_up[0], w_down[0]]
    big_m = [m_w_in[0].T, m_w_out[0], m_w_up[0], m_w_down[0]]
    big_v = [v_w_in[0].T, v_w_out[0], v_w_up[0], v_w_down[0]]
    wint, wout, wup, wdown = all_gather_slabs([w.astype(BF16) for w in big_w])
    wint = wint.reshape(-1, d)
    wout = wout.reshape(-1, d)
    wdown = wdown.reshape(-1, d)

    probs = jax.nn.softmax(rnn_lb_logits.astype(F32), axis=0)
    lb = probs[0:1]
    grad_x, dwint, dwout, dwup, dwdown, sg = _local_step(
        xs, target, wint, wout, wup, wdown, attn_sinks, attn_out_gain, lb, rnn_norm_gain,
        mix_pre_gain, mix_post_gain, mlp_pre_gain, mlp_post_gain)

    big_g = [dwint.reshape(N_DEV, -1, d), dwout.reshape(N_DEV, -1, d), dwup, dwdown.reshape(N_DEV, -1, d)]
    big_out = reduce_scatter_adamw(big_g, big_w, big_m, big_v)

    n_heads = attn_sinks.shape[1]
    jac = probs[0] * probs[1]
    partial = _pack([sg["sse"], sg["sinks"][0, :n_heads], sg["aog"], jnp.stack([sg["lb"][0], sg["lb"][0]]),
                     jnp.sum(sg["rng"], axis=0), sg["mixpre"], sg["mixpost"], sg["mlppre"], sg["mlppost"]])
    ones = [jnp.ones(small_w[k].shape, F32) for k in _SMALL]
    ones[2] = jnp.stack([jac, -jac])
    scale = _pack([jnp.full((1,), 0.5 / d, F32)] + ones)
    zero = jnp.zeros((1,), F32)
    outs = small_allreduce_adamw(partial, scale, _pack([zero] + [small_w[k] for k in _SMALL]),
                                 _pack([zero] + [small_m[k] for k in _SMALL]),
                                 _pack([jnp.ones((1,), F32)] + [small_v[k] for k in _SMALL]))
    shapes = [(1,)] + [small_w[k].shape for k in _SMALL]
    sgrad, sdelta, snm, snv = (_unpack(o, shapes) for o in outs)
    loss = sgrad[0][0]

    def big(i, j):
        o = big_out[i][j]
        return (o.T if i == 0 else o)[None]

    def ordered(j, smalls):
        s = dict(zip(_SMALL, smalls[1:]))
        return [big(0, j), s["attn_sinks"], s["attn_out_gain"], s["rnn_lb_logits"], s["rnn_norm_gain"], big(1, j),
                s["mix_pre_gain"], s["mix_post_gain"], s["mlp_pre_gain"], s["mlp_post_gain"], big(2, j), big(3, j)]

    return (loss, grad_x[None], *ordered(0, sgrad), *ordered(1, sdelta), *ordered(2, snm), *ordered(3, snv))
```

```python
import functools
import math

import jax
import jax.numpy as jnp
from jax import lax
from jax.experimental import pallas as pl
from jax.experimental.pallas import tpu as pltpu

F32 = jnp.float32
BF16 = jnp.bfloat16

HEAD_DIM = 64
N_KV_HEADS = 2
BLOCK = 128
RNN_HEAD_DIM = 128
CHUNK = 64
SUB = 16
EPS = 1e-6

ADAM_LR = 0.001
ADAM_B1 = 0.9
ADAM_B2 = 0.999
ADAM_EPS = 1e-08
ADAM_WD = 0.01
ADAM_STEP = 10

N_DEV = 8
LANES = 128
V7X_VMEM_LIMIT = 56 * 1024 * 1024
MESH = pl.DeviceIdType.MESH
HI = lax.Precision.HIGHEST


def _cparams(sem=None, **kw):
    return pltpu.CompilerParams(dimension_semantics=sem, vmem_limit_bytes=V7X_VMEM_LIMIT, **kw)


def _dot(a, b, dims):
    return lax.dot_general(a.astype(BF16), b.astype(BF16), (dims, ((), ())), preferred_element_type=F32)


NN = ((1,), (0,))
NT = ((1,), (1,))
TN = ((0,), (0,))


def _pick(n, pref):
    t = min(n, pref)
    while n % t:
        t //= 2
    return t


def _tile(n, pref, mult=LANES):
    if n <= pref:
        return n
    t = pref - pref % mult
    while n % t:
        t -= mult
    return t


MM_TILE = 1024
MM_K_TILE = 2048


def _matmul(name, a, b, dims, grid, a_spec, b_spec, out_shapes, out_specs, epilogue,
            extras=(), extra_specs=(), prologue=None):
    nk = grid[2]
    n_extra = len(extras)
    acc_shape = out_specs.block_shape[-2:]

    def lhs(a_ref):
        return a_ref[...] if prologue is None else prologue(a_ref[...])

    def body_one(a_ref, b_ref, *rest):
        epilogue(_dot(lhs(a_ref), b_ref[...], dims), rest[:n_extra], rest[n_extra:])

    def body_acc(a_ref, b_ref, *rest):
        acc = rest[-1]
        k = pl.program_id(2)
        part = _dot(lhs(a_ref), b_ref[...], dims)

        @pl.when(k == 0)
        def _():
            acc[...] = part

        @pl.when(k > 0)
        def _():
            acc[...] += part

        @pl.when(k == nk - 1)
        def _():
            epilogue(acc[...], rest[:n_extra], rest[n_extra:-1])

    return pl.pallas_call(
        body_one if nk == 1 else body_acc, name=name, grid=grid,
        in_specs=[a_spec, b_spec, *extra_specs],
        out_specs=out_specs, out_shape=out_shapes,
        scratch_shapes=[] if nk == 1 else [pltpu.VMEM(acc_shape, F32)],
        compiler_params=_cparams(("parallel", "parallel", "arbitrary")),
    )(a, b, *extras)


def _store_as(acc, extra_refs, out_refs):
    out_refs[0][...] = acc.astype(out_refs[0].dtype)


def _square(u):
    return u * u


def mm_nn(name, a, b, out_dtype, tk=None, prologue=None):
    (m, kk), n = a.shape, b.shape[1]
    tm, tn = _tile(m, MM_TILE), _tile(n, MM_TILE)
    tk = kk if tk is None else _tile(kk, tk)
    return _matmul(name, a, b, NN, (m // tm, n // tn, kk // tk),
                   pl.BlockSpec((tm, tk), lambda i, j, k: (i, k)),
                   pl.BlockSpec((tk, tn), lambda i, j, k: (k, j)),
                   jax.ShapeDtypeStruct((m, n), out_dtype),
                   pl.BlockSpec((tm, tn), lambda i, j, k: (i, j)), _store_as, prologue=prologue)


def mm_nt(name, a, b, out_dtype, tn=MM_TILE, epilogue=_store_as, extras=(), extra_specs=()):
    (m, kk), n = a.shape, b.shape[0]
    tm, tn = _tile(m, MM_TILE), _tile(n, tn)
    return _matmul(name, a, b, NT, (m // tm, n // tn, 1),
                   pl.BlockSpec((tm, kk), lambda i, j, k: (i, 0)),
                   pl.BlockSpec((tn, kk), lambda i, j, k: (j, 0)),
                   jax.ShapeDtypeStruct((m, n), out_dtype),
                   pl.BlockSpec((tm, tn), lambda i, j, k: (i, j)), epilogue,
                   extras=extras, extra_specs=extra_specs)


def mm_tn(name, a, b, out_dtype, tm=MM_TILE, prologue=None):
    (kk, m), n = a.shape, b.shape[1]
    tm, tn, tk = _tile(m, tm), _tile(n, MM_TILE), _tile(kk, MM_K_TILE)
    return _matmul(name, a, b, TN, (m // tm, n // tn, kk // tk),
                   pl.BlockSpec((tk, tm), lambda i, j, k: (k, i)),
                   pl.BlockSpec((tk, tn), lambda i, j, k: (k, j)),
                   jax.ShapeDtypeStruct((m, n), out_dtype),
                   pl.BlockSpec((tm, tn), lambda i, j, k: (i, j)), _store_as, prologue=prologue)


def up_proj(h2, wup_slabs):
    (m, kk), (_, _, ns) = h2.shape, wup_slabs.shape
    tm, tn = _tile(m, MM_TILE), _tile(ns, MM_TILE)
    r = ns // tn
    n = N_DEV * ns

    def epi(acc, extra_refs, out_refs):
        out_refs[0][...] = jnp.maximum(acc, 0.0).astype(BF16)

    return _matmul("up_proj", h2, wup_slabs, NN, (m // tm, n // tn, 1),
                   pl.BlockSpec((tm, kk), lambda i, j, k: (i, 0)),
                   pl.BlockSpec((None, kk, tn), lambda i, j, k: (j // r, 0, j % r)),
                   jax.ShapeDtypeStruct((m, n), BF16),
                   pl.BlockSpec((tm, tn), lambda i, j, k: (i, j)), epi)


def down_proj(u, wdown):
    return mm_nn("down_proj", u, wdown, F32, tk=MM_K_TILE, prologue=_square)


def down_bwd_act(dy, wdown, u):
    tm, tn = _tile(dy.shape[0], MM_TILE), _tile(wdown.shape[0], MM_TILE)

    def epi(acc, extra_refs, out_refs):
        out_refs[0][...] = (acc * (2.0 * extra_refs[0][...].astype(F32))).astype(BF16)

    return mm_nt("down_bwd_act", dy, wdown, BF16, epilogue=epi, extras=(u,),
                 extra_specs=(pl.BlockSpec((tm, tn), lambda i, j, k: (i, j)),))


def down_wgrad(u, dy):
    return mm_tn("down_wgrad", u, dy, F32, prologue=_square)


def up_bwd_x(du, wup_slabs):
    (m, kk), (_, n, ns) = du.shape, wup_slabs.shape
    tm, tk = _tile(m, MM_TILE), _tile(ns, MM_TILE)
    r = ns // tk
    return _matmul("up_bwd_x", du, wup_slabs, NT, (m // tm, 1, kk // tk),
                   pl.BlockSpec((tm, tk), lambda i, j, k: (i, k)),
                   pl.BlockSpec((None, n, tk), lambda i, j, k: (k // r, 0, k % r)),
                   jax.ShapeDtypeStruct((m, n), F32),
                   pl.BlockSpec((tm, n), lambda i, j, k: (i, 0)), _store_as)


def up_wgrad(h2, du):
    (kk, m), n = h2.shape, du.shape[1]
    ns = n // N_DEV
    tm, tn, tk = _tile(m, MM_TILE), _tile(ns, MM_TILE), _tile(kk, MM_K_TILE)
    r = ns // tn
    return _matmul("up_wgrad", h2, du, TN, (m // tm, n // tn, kk // tk),
                   pl.BlockSpec((tk, tm), lambda i, j, k: (k, i)),
                   pl.BlockSpec((tk, tn), lambda i, j, k: (k, j)),
                   jax.ShapeDtypeStruct((N_DEV, m, ns), F32),
                   pl.BlockSpec((None, tm, tn), lambda i, j, k: (j // r, i, j % r)), _store_as)


def _rstd(x):
    return lax.rsqrt(jnp.mean(x * x, axis=-1, keepdims=True) + EPS)


def _norm_bwd(x, g, dy):
    r = _rstd(x)
    xh = x * r
    dyg = dy * g
    dx = r * (dyg - xh * jnp.mean(dyg * xh, axis=-1, keepdims=True))
    return dx, jnp.sum(dy * xh, axis=0, keepdims=True)


def _row_spec(tr, d):
    return pl.BlockSpec((tr, d), lambda i: (i, 0))


def _vec_spec(d):
    return pl.BlockSpec((1, d), lambda i: (0, 0))


def _accum(ref, val):
    @pl.when(pl.program_id(0) == 0)
    def _():
        ref[...] = jnp.zeros_like(ref)

    ref[...] += val


def pre_norm(x, g, tr=256):
    t, d = x.shape
    tr = _pick(t, tr)

    def body(x_ref, g_ref, h_ref):
        xx = x_ref[...]
        h_ref[...] = (xx * _rstd(xx) * g_ref[...]).astype(BF16)

    return pl.pallas_call(
        body, name="pre_norm", grid=(t // tr,),
        in_specs=[_row_spec(tr, d), _vec_spec(d)], out_specs=_row_spec(tr, d),
        out_shape=jax.ShapeDtypeStruct((t, d), BF16), compiler_params=_cparams(("parallel",)),
    )(x, g)


def mid_fwd(mixed, g_post, x, g_pre2, tr=256):
    t, d = x.shape
    tr = _pick(t, tr)

    def body(m_ref, gp_ref, x_ref, g2_ref, x1_ref, h2_ref):
        mm = m_ref[...]
        x1 = x_ref[...] + mm * _rstd(mm) * gp_ref[...]
        x1_ref[...] = x1
        h2_ref[...] = (x1 * _rstd(x1) * g2_ref[...]).astype(BF16)

    return pl.pallas_call(
        body, name="mid_fwd", grid=(t // tr,),
        in_specs=[_row_spec(tr, d), _vec_spec(d), _row_spec(tr, d), _vec_spec(d)],
        out_specs=[_row_spec(tr, d), _row_spec(tr, d)],
        out_shape=[jax.ShapeDtypeStruct((t, d), F32), jax.ShapeDtypeStruct((t, d), BF16)],
        compiler_params=_cparams(("parallel",)),
    )(mixed, g_post, x, g_pre2)


def loss_bwd(y, g_post2, x1, target, tr=256):
    t, d = y.shape
    tr = _pick(t, tr)

    def body(y_ref, g_ref, x1_ref, t_ref, sse_ref, dout_ref, dy_ref, dg_ref):
        yy = y_ref[...]
        g = g_ref[...]
        err = x1_ref[...] + yy * _rstd(yy) * g - t_ref[...]
        _accum(sse_ref, jnp.sum(jnp.sum(err * err, axis=1, keepdims=True), axis=0, keepdims=True))
        dout = err * (1.0 / d)
        dout_ref[...] = dout
        dy, dg = _norm_bwd(yy, g, dout)
        dy_ref[...] = dy.astype(BF16)
        _accum(dg_ref, dg)

    return pl.pallas_call(
        body, name="loss_bwd", grid=(t // tr,),
        in_specs=[_row_spec(tr, d), _vec_spec(d), _row_spec(tr, d), _row_spec(tr, d)],
        out_specs=[pl.BlockSpec((1, 1), lambda i: (0, 0)), _row_spec(tr, d), _row_spec(tr, d), _vec_spec(d)],
        out_shape=[jax.ShapeDtypeStruct((1, 1), F32), jax.ShapeDtypeStruct((t, d), F32),
                   jax.ShapeDtypeStruct((t, d), BF16), jax.ShapeDtypeStruct((1, d), F32)],
        compiler_params=_cparams(("arbitrary",)),
    )(y, g_post2, x1, target)


def mid_bwd(dh2, x1, g_pre2, dout, mixed, g_post, tr=256):
    t, d = x1.shape
    tr = _pick(t, tr)

    def body(dh_ref, x1_ref, g2_ref, do_ref, m_ref, gp_ref, dx1_ref, dm_ref, dg2_ref, dgp_ref):
        d1, dg2 = _norm_bwd(x1_ref[...], g2_ref[...], dh_ref[...])
        dx1 = do_ref[...] + d1
        dx1_ref[...] = dx1
        dm, dgp = _norm_bwd(m_ref[...], gp_ref[...], dx1)
        dm_ref[...] = dm.astype(BF16)
        _accum(dg2_ref, dg2)
        _accum(dgp_ref, dgp)

    return pl.pallas_call(
        body, name="mid_bwd", grid=(t // tr,),
        in_specs=[_row_spec(tr, d), _row_spec(tr, d), _vec_spec(d), _row_spec(tr, d), _row_spec(tr, d), _vec_spec(d)],
        out_specs=[_row_spec(tr, d), _row_spec(tr, d), _vec_spec(d), _vec_spec(d)],
        out_shape=[jax.ShapeDtypeStruct((t, d), F32), jax.ShapeDtypeStruct((t, d), BF16),
                   jax.ShapeDtypeStruct((1, d), F32), jax.ShapeDtypeStruct((1, d), F32)],
        compiler_params=_cparams(("arbitrary",)),
    )(dh2, x1, g_pre2, dout, mixed, g_post)


def first_bwd(dh1, x, g_pre, dx1, tr=256):
    t, d = x.shape
    tr = _pick(t, tr)

    def body(dh_ref, x_ref, g_ref, dx1_ref, gx_ref, dg_ref):
        d0, dg = _norm_bwd(x_ref[...], g_ref[...], dh_ref[...])
        gx_ref[...] = dx1_ref[...] + d0
        _accum(dg_ref, dg)

    return pl.pallas_call(
        body, name="first_bwd", grid=(t // tr,),
        in_specs=[_row_spec(tr, d), _row_spec(tr, d), _vec_spec(d), _row_spec(tr, d)],
        out_specs=[_row_spec(tr, d), _vec_spec(d)],
        out_shape=[jax.ShapeDtypeStruct((t, d), F32), jax.ShapeDtypeStruct((1, d), F32)],
        compiler_params=_cparams(("arbitrary",)),
    )(dh1, x, g_pre, dx1)


def _attn_geometry():
    r = lax.broadcasted_iota(jnp.int32, (BLOCK, BLOCK), 0)
    c = lax.broadcasted_iota(jnp.int32, (BLOCK, BLOCK), 1)
    dist_cur = (r - c).astype(F32)
    return dist_cur, dist_cur + float(BLOCK), r >= c, c > r


def _attn_probs(qh, kp, kc, slope, sink, geo, has_prev):
    dist_cur, dist_prev, mask_cur, mask_prev = geo
    s_cur = _dot(qh, kc, NT) - slope * dist_cur
    s_prev = _dot(qh, kp, NT) - slope * dist_prev
    s_cur = jnp.where(mask_cur, s_cur, -jnp.inf)
    s_prev = jnp.where(jnp.logical_and(mask_prev, has_prev), s_prev, -jnp.inf)
    m = jnp.maximum(jnp.maximum(jnp.max(s_cur, axis=-1, keepdims=True),
                                jnp.max(s_prev, axis=-1, keepdims=True)), sink)
    p_cur = jnp.exp(s_cur - m)
    p_prev = jnp.exp(s_prev - m)
    p_sink = jnp.exp(sink - m)
    inv = 1.0 / (jnp.sum(p_cur, axis=-1, keepdims=True) + jnp.sum(p_prev, axis=-1, keepdims=True) + p_sink)
    return p_prev * inv, p_cur * inv, p_sink * inv


def attn_fwd(proj, sinks, gain, aw):
    t = proj.shape[0]
    kw = N_KV_HEADS * HEAD_DIM
    n_heads = aw // HEAD_DIM
    group = n_heads // N_KV_HEADS
    nb = t // BLOCK
    scale = HEAD_DIM ** -0.5

    def body(sink_ref, q_ref, k_ref, v_ref, g_ref, o_ref, on_ref):
        n = pl.program_id(0)
        cur = pl.multiple_of(n * BLOCK, BLOCK)
        prev = pl.multiple_of(jnp.maximum(n - 1, 0) * BLOCK, BLOCK)
        has_prev = n > 0
        geo = _attn_geometry()
        kc, kp = k_ref[pl.ds(cur, BLOCK), :], k_ref[pl.ds(prev, BLOCK), :]
        vc, vp = v_ref[pl.ds(cur, BLOCK), :], v_ref[pl.ds(prev, BLOCK), :]
        for h in range(n_heads):
            kv = h // group
            ks = slice(kv * HEAD_DIM, (kv + 1) * HEAD_DIM)
            hs = slice(h * HEAD_DIM, (h + 1) * HEAD_DIM)
            slope = 2.0 ** (-8.0 * (h + 1) / n_heads)
            qh = q_ref[:, hs] * scale
            p_prev, p_cur, _ = _attn_probs(qh, kp[:, ks], kc[:, ks], slope, sink_ref[0, h], geo, has_prev)
            o_ref[:, hs] = _dot(p_prev, vp[:, ks], NN) + _dot(p_cur, vc[:, ks], NN)
        o = o_ref[...]
        on_ref[...] = (o * _rstd(o) * g_ref[...]).astype(BF16)

    return pl.pallas_call(
        body, name="attn_fwd", grid=(nb,),
        in_specs=[pl.BlockSpec(memory_space=pltpu.SMEM),
                  pl.BlockSpec((BLOCK, aw), lambda n: (n, 0)),
                  pl.BlockSpec((t, kw), lambda n: (0, aw // kw)),
                  pl.BlockSpec((t, kw), lambda n: (0, aw // kw + 1)),
                  pl.BlockSpec((1, aw), lambda n: (0, 0))],
        out_specs=[pl.BlockSpec((BLOCK, aw), lambda n: (n, 0)), pl.BlockSpec((BLOCK, aw), lambda n: (n, 0))],
        out_shape=[jax.ShapeDtypeStruct((t, aw), F32), jax.ShapeDtypeStruct((t, aw), BF16)],
        compiler_params=_cparams(("parallel",)),
    )(sinks, proj, proj, proj, gain)


def attn_bwd(proj, sinks, gain, attn_o, dcat, aw):
    t = proj.shape[0]
    kw = N_KV_HEADS * HEAD_DIM
    n_heads = aw // HEAD_DIM
    group = n_heads // N_KV_HEADS
    nb = t // BLOCK
    scale = HEAD_DIM ** -0.5

    def body(sink_ref, q_ref, k_ref, v_ref, g_ref, o_ref, dn_ref, dq_ref, dk_ref, dv_ref, dsink_ref, dg_ref, do_ref):
        n = pl.program_id(0)
        cur = pl.multiple_of(n * BLOCK, BLOCK)
        prev = pl.multiple_of(jnp.maximum(n - 1, 0) * BLOCK, BLOCK)
        has_prev = n > 0
        geo = _attn_geometry()

        @pl.when(n == 0)
        def _():
            dk_ref[...] = jnp.zeros_like(dk_ref)
            dv_ref[...] = jnp.zeros_like(dv_ref)
            dsink_ref[...] = jnp.zeros_like(dsink_ref)

        o = o_ref[...]
        do_all, dg = _norm_bwd(o, g_ref[...], dn_ref[...])
        _accum(dg_ref, dg)
        do_ref[...] = do_all
        kc, kp = k_ref[pl.ds(cur, BLOCK), :], k_ref[pl.ds(prev, BLOCK), :]
        vc, vp = v_ref[pl.ds(cur, BLOCK), :], v_ref[pl.ds(prev, BLOCK), :]
        lane = lax.broadcasted_iota(jnp.int32, (1, LANES), 1)
        dsink = jnp.zeros((1, LANES), F32)
        for kv in range(N_KV_HEADS):
            ks = slice(kv * HEAD_DIM, (kv + 1) * HEAD_DIM)
            dkc = jnp.zeros((BLOCK, HEAD_DIM), F32)
            dkp = jnp.zeros((BLOCK, HEAD_DIM), F32)
            dvc = jnp.zeros((BLOCK, HEAD_DIM), F32)
            dvp = jnp.zeros((BLOCK, HEAD_DIM), F32)
            for gidx in range(group):
                h = kv * group + gidx
                hs = slice(h * HEAD_DIM, (h + 1) * HEAD_DIM)
                slope = 2.0 ** (-8.0 * (h + 1) / n_heads)
                qh = q_ref[:, hs] * scale
                p_prev, p_cur, p_sink = _attn_probs(qh, kp[:, ks], kc[:, ks], slope, sink_ref[0, h], geo, has_prev)
                doh = do_ref[:, hs]
                delta = jnp.sum(doh * o_ref[:, hs], axis=-1, keepdims=True)
                ds_cur = p_cur * (_dot(doh, vc[:, ks], NT) - delta)
                ds_prev = p_prev * (_dot(doh, vp[:, ks], NT) - delta)
                dsink = dsink + jnp.where(lane == h, -jnp.sum(p_sink * delta, axis=0, keepdims=True), 0.0)
                dq_ref[:, hs] = ((_dot(ds_cur, kc[:, ks], NN) + _dot(ds_prev, kp[:, ks], NN)) * scale).astype(BF16)
                dkc = dkc + _dot(ds_cur, qh, TN)
                dkp = dkp + _dot(ds_prev, qh, TN)
                dvc = dvc + _dot(p_cur, doh, TN)
                dvp = dvp + _dot(p_prev, doh, TN)
            dk_ref[pl.ds(cur, BLOCK), ks] += dkc
            dv_ref[pl.ds(cur, BLOCK), ks] += dvc

            @pl.when(has_prev)
            def _():
                dk_ref[pl.ds(prev, BLOCK), ks] += dkp
                dv_ref[pl.ds(prev, BLOCK), ks] += dvp
        dsink_ref[...] += dsink

    return pl.pallas_call(
        body, name="attn_bwd", grid=(nb,),
        in_specs=[pl.BlockSpec(memory_space=pltpu.SMEM),
                  pl.BlockSpec((BLOCK, aw), lambda n: (n, 0)),
                  pl.BlockSpec((t, kw), lambda n: (0, aw // kw)),
                  pl.BlockSpec((t, kw), lambda n: (0, aw // kw + 1)),
                  pl.BlockSpec((1, aw), lambda n: (0, 0)),
                  pl.BlockSpec((BLOCK, aw), lambda n: (n, 0)),
                  pl.BlockSpec((BLOCK, aw), lambda n: (n, 0))],
        out_specs=[pl.BlockSpec((BLOCK, aw), lambda n: (n, 0)),
                   pl.BlockSpec((t, kw), lambda n: (0, 0)), pl.BlockSpec((t, kw), lambda n: (0, 0)),
                   pl.BlockSpec((1, LANES), lambda n: (0, 0)), pl.BlockSpec((1, aw), lambda n: (0, 0))],
        out_shape=[jax.ShapeDtypeStruct((t, aw), BF16), jax.ShapeDtypeStruct((t, kw), F32),
                   jax.ShapeDtypeStruct((t, kw), F32), jax.ShapeDtypeStruct((1, LANES), F32),
                   jax.ShapeDtypeStruct((1, aw), F32)],
        scratch_shapes=[pltpu.VMEM((BLOCK, aw), F32)],
        compiler_params=_cparams(("arbitrary",)),
    )(sinks, proj, proj, proj, gain, attn_o, dcat)


def _sigmoid(x):
    return 1.0 / (1.0 + jnp.exp(-x))


def _chunk_geometry():
    row = lax.broadcasted_iota(jnp.int32, (CHUNK, CHUNK), 0)
    col = lax.broadcasted_iota(jnp.int32, (CHUNK, CHUNK), 1)
    return row, col


def _cumsum_rows(x, reverse=False):
    row, col = _chunk_geometry()
    tri = (col >= row) if reverse else (col <= row)
    return lax.dot_general(tri.astype(F32), x, ((NN), ((), ())), precision=HI, preferred_element_type=F32)


def _rep_sub(x4):
    k = x4.shape[-1]
    return jnp.broadcast_to(x4[:, None, :], (CHUNK // SUB, SUB, k)).reshape(CHUNK, k)


def _gates(q_r, f_r, lb):
    sg = _sigmoid(f_r)
    f = lb + (1.0 - lb) * sg
    sq = _sigmoid(q_r)
    return sg, f, sq, q_r * sq


def _offdiag_terms(b, b_ref, j):
    c = b_ref[pl.ds(j * SUB + SUB - 1, 1), :]
    return jnp.exp(jnp.minimum(b - c, 0.0)), jnp.exp(jnp.minimum(c - b, 0.0))


def _offdiag_mask(j):
    row, col = _chunk_geometry()
    return jnp.logical_and(row >= (j + 1) * SUB, (col // SUB) == j)


def _diag_mask():
    row, col = _chunk_geometry()
    return jnp.logical_and((row // SUB) == (col // SUB), row >= col)


def hgrn_fwd(proj, lb, norm_gain, col0, rw):
    t = proj.shape[0]
    nh = rw // RNN_HEAD_DIM
    nc = t // CHUNK
    kd = RNN_HEAD_DIM
    cb = col0 // kd
    nsub = CHUNK // SUB

    def body(q_ref, f_ref, i_ref, g_ref, lb_ref, ng_ref, rnn_ref, o_ref, att_ref, st_ref, state, b_ref, k_ref):
        c = pl.program_id(1)

        @pl.when(c == 0)
        def _():
            state[...] = jnp.zeros_like(state)

        st_ref[...] = state[...]
        _, f, _, q = _gates(q_ref[...], f_ref[...], lb_ref[...])
        k = 1.0 - f
        v = i_ref[...]
        b = _cumsum_rows(jnp.log(f))
        b_ref[...] = b
        k_ref[...] = k
        row, col = _chunk_geometry()
        att = jnp.zeros((CHUNK, CHUNK), F32)
        for j in range(nsub - 1):
            e_row, e_col = _offdiag_terms(b, b_ref, j)
            att = att + jnp.where(_offdiag_mask(j), _dot(q * e_row, k * e_col, NT), 0.0)
        rloc = lax.broadcasted_iota(jnp.int32, (CHUNK, kd), 0) % SUB
        for r in range(SUB):
            bs = _rep_sub(b_ref[pl.ds(r, nsub, stride=SUB), :])
            ks = _rep_sub(k_ref[pl.ds(r, nsub, stride=SUB), :])
            e = jnp.exp(jnp.where(rloc >= r, b - bs, -jnp.inf))
            colsum = jnp.sum(q * e * ks, axis=-1, keepdims=True)
            att = jnp.where(jnp.logical_and((col % SUB) == r, (row // SUB) == (col // SUB)), colsum, att)
        att_ref[...] = att
        b_last = b_ref[pl.ds(CHUNK - 1, 1), :]
        o = _dot(q * jnp.exp(b), state[...], NT) + _dot(att, v, NN)
        state[...] = state[...] * jnp.exp(b_last) + _dot(v, k * jnp.exp(b_last - b), TN)
        o_ref[...] = o
        gate = g_ref[...]
        gate = gate * _sigmoid(gate)
        rnn_ref[...] = (o * _rstd(o) * ng_ref[...] * gate).astype(BF16)

    def col(kidx):
        return pl.BlockSpec((CHUNK, kd), lambda h, c: (c, cb + kidx * nh + h))

    rnn, o, att, st = pl.pallas_call(
        body, name="hgrn_fwd", grid=(nh, nc),
        in_specs=[col(0), col(1), col(2), col(3),
                  pl.BlockSpec((1, kd), lambda h, c: (0, h)), pl.BlockSpec((1, kd), lambda h, c: (0, 0))],
        out_specs=[pl.BlockSpec((CHUNK, kd), lambda h, c: (c, h)), pl.BlockSpec((CHUNK, kd), lambda h, c: (c, h)),
                   pl.BlockSpec((None, CHUNK, CHUNK), lambda h, c: (h, c, 0)),
                   pl.BlockSpec((None, None, kd, kd), lambda h, c: (c, h, 0, 0))],
        out_shape=[jax.ShapeDtypeStruct((t, rw), BF16), jax.ShapeDtypeStruct((t, rw), F32),
                   jax.ShapeDtypeStruct((nh, t, CHUNK), F32), jax.ShapeDtypeStruct((nc, nh, kd, kd), F32)],
        scratch_shapes=[pltpu.VMEM((kd, kd), F32), pltpu.VMEM((CHUNK, kd), F32), pltpu.VMEM((CHUNK, kd), F32)],
        compiler_params=_cparams(("parallel", "arbitrary")),
    )(proj, proj, proj, proj, lb, norm_gain)
    return rnn, o, att, st


def hgrn_bwd(proj, lb, norm_gain, o_all, att_all, st_all, dcat, col0, rw):
    t = proj.shape[0]
    nh = rw // RNN_HEAD_DIM
    nc = t // CHUNK
    kd = RNN_HEAD_DIM
    cb = col0 // kd
    nsub = CHUNK // SUB
    dcb = (dcat.shape[1] - rw) // kd

    def body(q_ref, f_ref, i_ref, g_ref, lb_ref, ng_ref, o_ref, att_ref, st0_ref, st1_ref, d_ref,
             dq_ref, df_ref, di_ref, dg_ref, dlb_ref, dng_ref, dstate, b_ref, k_ref, dks_ref):
        ci = pl.program_id(1)

        @pl.when(ci == 0)
        def _():
            dstate[...] = jnp.zeros_like(dstate)
            dlb_ref[...] = jnp.zeros_like(dlb_ref)
            dng_ref[...] = jnp.zeros_like(dng_ref)

        lbv = lb_ref[...]
        q_r, g_r = q_ref[...], g_ref[...]
        sg, f, sq, q = _gates(q_r, f_ref[...], lbv)
        k = 1.0 - f
        v = i_ref[...]
        b = _cumsum_rows(jnp.log(f))
        b_ref[...] = b
        k_ref[...] = k
        row, col = _chunk_geometry()

        o = o_ref[...]
        ng = ng_ref[...]
        sgg = _sigmoid(g_r)
        gate = g_r * sgg
        d_rnn = d_ref[...]
        r = _rstd(o)
        oh = o * r
        dg_ref[...] = (d_rnn * oh * ng * (sgg * (1.0 + g_r * (1.0 - sgg)))).astype(BF16)
        d_on = d_rnn * gate
        dng_ref[...] += jnp.sum(d_on * oh, axis=0, keepdims=True)
        dyg = d_on * ng
        do = r * (dyg - oh * jnp.mean(dyg * oh, axis=-1, keepdims=True))

        st0 = st0_ref[...]
        dst = dstate[...]
        b_last = b_ref[pl.ds(CHUNK - 1, 1), :]
        eb = jnp.exp(b)
        kdec = k * jnp.exp(b_last - b)
        att = att_ref[...]
        da = jnp.where(row >= col, _dot(do, v, NT), 0.0)

        dq = _dot(do, st0, NN) * eb
        dk = _dot(v, dst, NN) * jnp.exp(b_last - b)
        dv = _dot(att, do, TN) + _dot(kdec, dst, NT)
        for j in range(nsub - 1):
            e_row, e_col = _offdiag_terms(b, b_ref, j)
            daj = jnp.where(_offdiag_mask(j), da, 0.0)
            dq = dq + e_row * _dot(daj, k * e_col, NN)
            dk = dk + e_col * _dot(daj, q * e_row, TN)
        rloc = lax.broadcasted_iota(jnp.int32, (CHUNK, kd), 0) % SUB
        dad = jnp.where(_diag_mask(), da, 0.0)
        for rr in range(SUB):
            bs = _rep_sub(b_ref[pl.ds(rr, nsub, stride=SUB), :])
            ks = _rep_sub(k_ref[pl.ds(rr, nsub, stride=SUB), :])
            e = jnp.exp(jnp.where(rloc >= rr, b - bs, -jnp.inf))
            dacol = jnp.sum(jnp.where((col % SUB) == rr, dad, 0.0), axis=-1, keepdims=True)
            w = dacol * e
            dq = dq + w * ks
            dks_ref[pl.ds(rr, nsub, stride=SUB), :] = jnp.sum((w * q).reshape(nsub, SUB, kd), axis=1)
        dk = dk + dks_ref[...]

        gsum = jnp.sum(dst * st1_ref[...], axis=0, keepdims=True)
        dlf = _cumsum_rows(q * dq - k * dk, reverse=True) + gsum
        dfv = dlf / f - dk
        df_ref[...] = (dfv * (1.0 - lbv) * sg * (1.0 - sg)).astype(BF16)
        dlb_ref[...] += jnp.sum(dfv * (1.0 - sg), axis=0, keepdims=True)
        dq_ref[...] = (dq * (sq * (1.0 + q_r * (1.0 - sq)))).astype(BF16)
        di_ref[...] = dv.astype(BF16)
        dstate[...] = dst * jnp.exp(b_last) + _dot(do, q * eb, TN)

    def rev(c):
        return nc - 1 - c

    def col_in(kidx):
        return pl.BlockSpec((CHUNK, kd), lambda h, c: (rev(c), cb + kidx * nh + h))

    tile = pl.BlockSpec((CHUNK, kd), lambda h, c: (rev(c), h))
    outs = pl.pallas_call(
        body, name="hgrn_bwd", grid=(nh, nc),
        in_specs=[col_in(0), col_in(1), col_in(2), col_in(3),
                  pl.BlockSpec((1, kd), lambda h, c: (0, h)), pl.BlockSpec((1, kd), lambda h, c: (0, 0)),
                  tile,
                  pl.BlockSpec((None, CHUNK, CHUNK), lambda h, c: (h, rev(c), 0)),
                  pl.BlockSpec((None, None, kd, kd), lambda h, c: (rev(c), h, 0, 0)),
                  pl.BlockSpec((None, None, kd, kd), lambda h, c: (jnp.minimum(rev(c) + 1, nc - 1), h, 0, 0)),
                  pl.BlockSpec((CHUNK, kd), lambda h, c: (rev(c), dcb + h))],
        out_specs=[tile, tile, tile, tile,
                   pl.BlockSpec((1, kd), lambda h, c: (0, h)), pl.BlockSpec((None, 1, kd), lambda h, c: (h, 0, 0))],
        out_shape=[jax.ShapeDtypeStruct((t, rw), BF16)] * 4 + [jax.ShapeDtypeStruct((1, rw), F32),
                                                               jax.ShapeDtypeStruct((nh, 1, kd), F32)],
        scratch_shapes=[pltpu.VMEM((kd, kd), F32), pltpu.VMEM((CHUNK, kd), F32), pltpu.VMEM((CHUNK, kd), F32),
                        pltpu.VMEM((CHUNK, kd), F32)],
        compiler_params=_cparams(("parallel", "arbitrary")),
    )(proj, proj, proj, proj, lb, norm_gain, o_all, att_all, st_all, st_all, dcat)
    return outs


def _local_step(x, target, wint, wout, wup, wdown, sinks, aog, lb, rng, g_mixpre, g_mixpost, g_mlppre, g_mlppost):
    t, d = x.shape
    aw = d // 2
    rw = d - aw
    col0 = aw + 2 * N_KV_HEADS * HEAD_DIM
    h1 = pre_norm(x, g_mixpre)
    proj = mm_nt("in_proj", h1, wint, F32)
    attn_o, attn_n = attn_fwd(proj, sinks, aog, aw)
    rnn, o_r, att, st = hgrn_fwd(proj, lb, rng, col0, rw)
    cat = jnp.concatenate([attn_n, rnn], axis=1)
    mixed = mm_nn("out_proj", cat, wout, F32)
    x1, h2 = mid_fwd(mixed, g_mixpost, x, g_mlppre)
    u = up_proj(h2, wup)
    y = down_proj(u, wdown)
    sse, dout, dy, dg_mlppost = loss_bwd(y, g_mlppost, x1, target)
    du = down_bwd_act(dy, wdown, u)
    dwdown = down_wgrad(u, dy)
    dh2 = up_bwd_x(du, wup)
    dwup = up_wgrad(h2, du)
    dx1, dmixed, dg_mlppre, dg_mixpost = mid_bwd(dh2, x1, g_mlppre, dout, mixed, g_mixpost)
    dcat = mm_nt("out_bwd_x", dmixed, wout, F32)
    dwout = mm_tn("out_wgrad", cat, dmixed, F32)
    dq_r, df_r, di_r, dg_r, dlb, dng = hgrn_bwd(proj, lb, rng, o_r, att, st, dcat, col0, rw)
    dq_a, dk_a, dv_a, dsinks, daog = attn_bwd(proj, sinks, aog, attn_o, dcat, aw)
    dproj = jnp.concatenate([dq_a, dk_a.astype(BF16), dv_a.astype(BF16), dq_r, df_r, di_r, dg_r], axis=1)
    dh1 = mm_nn("in_bwd_x", dproj, wint, F32, tk=MM_K_TILE)
    dwint = mm_tn("in_wgrad", dproj, h1, F32)
    grad_x, dg_mixpre = first_bwd(dh1, x, g_mixpre, dx1)
    small = dict(sse=sse, sinks=dsinks, aog=daog, lb=dlb, rng=dng, mixpre=dg_mixpre, mixpost=dg_mixpost,
                 mlppre=dg_mlppre, mlppost=dg_mlppost)
    return grad_x, dwint, dwout, dwup, dwdown, small


ANY = pl.BlockSpec(memory_space=pl.ANY)


def _coords():
    return lax.axis_index("x"), lax.axis_index("y"), lax.axis_index("c")


def _slab_index(dev):
    return 4 * dev[0] + 2 * dev[1] + dev[2]


def all_gather_slabs(shards):
    n = len(shards)

    def body(*refs):
        ins, outs = refs[:n], refs[n:2 * n]
        send_sems, recv_sems, local_sems = refs[2 * n:]
        x, y, c = _coords()
        me, sibling = (x, y, c), (x, y, 1 - c)
        chips = [(1 - x, y), (x, 1 - y), (1 - x, 1 - y)]

        def copy(a, k, block, to, src=None):
            slab = outs[a].at[_slab_index(block)]
            return pltpu.make_async_remote_copy(
                src_ref=slab if src is None else src, dst_ref=slab,
                send_sem=send_sems.at[a, k], recv_sem=recv_sems.at[a, k],
                device_id=to, device_id_type=MESH)

        mine = [pltpu.make_async_copy(ins[a], outs[a].at[_slab_index(me)], local_sems.at[a]) for a in range(n)]
        for cp in mine:
            cp.start()
        first = []
        for a in range(n):
            first.append(copy(a, 0, me, sibling, src=ins[a]))
            first += [copy(a, 1 + j, me, (*chip, c), src=ins[a]) for j, chip in enumerate(chips)]
        for cp in first:
            cp.start()
        passed = []
        for j, chip in enumerate(chips):
            for a in range(n):
                copy(a, 1 + j, (*chip, c), me).wait_recv()
                fwd = copy(a, 4 + j, (*chip, c), sibling)
                fwd.start()
                passed.append(fwd)
        for a in range(n):
            copy(a, 0, sibling, me).wait_recv()
            for j, chip in enumerate(chips):
                copy(a, 4 + j, (*chip, 1 - c), me).wait_recv()
        for cp in first + passed:
            cp.wait_send()
        for cp in mine:
            cp.wait()

    return pl.pallas_call(
        body, name="all_gather_weights",
        in_specs=[ANY] * n, out_specs=[ANY] * n,
        out_shape=[jax.ShapeDtypeStruct((N_DEV, *s.shape), s.dtype) for s in shards],
        scratch_shapes=[pltpu.SemaphoreType.DMA((n, 7)), pltpu.SemaphoreType.DMA((n, 7)),
                        pltpu.SemaphoreType.DMA((n,))],
        compiler_params=pltpu.CompilerParams(has_side_effects=True),
    )(*shards)


_AXES = ("x", "y", "c")


def exchange_halves(name, arrays, axis):
    n = len(arrays)
    minor = axis == "c"
    pieces = arrays[0].shape[0] if minor else arrays[0].shape[1]

    def body(*refs):
        ins, outs = refs[:n], refs[n:2 * n]
        send_sems, recv_sems = refs[2 * n:]
        coords = list(_coords())
        ai = _AXES.index(axis)
        mine = coords[ai]
        peer = list(coords)
        peer[ai] = 1 - mine
        copies = []
        for a in range(n):
            for p in range(pieces):
                src = ins[a].at[p, 1 - mine] if minor else ins[a].at[1 - mine, p]
                copies.append(pltpu.make_async_remote_copy(
                    src_ref=src, dst_ref=outs[a].at[p],
                    send_sem=send_sems.at[a, p], recv_sem=recv_sems.at[a, p],
                    device_id=tuple(peer), device_id_type=MESH))
        for cp in copies:
            cp.start()
        for cp in copies:
            cp.wait()

    return pl.pallas_call(
        body, name=name,
        in_specs=[ANY] * n, out_specs=[ANY] * n,
        out_shape=[jax.ShapeDtypeStruct((pieces, *a.shape[2:]), a.dtype) for a in arrays],
        scratch_shapes=[pltpu.SemaphoreType.DMA((n, pieces)), pltpu.SemaphoreType.DMA((n, pieces))],
        compiler_params=pltpu.CompilerParams(has_side_effects=True),
    )(*arrays)


def add_kept_half(name, kept, got, sel, axis, tr=256):
    minor = axis == "c"
    pieces, rows, cols = got.shape
    tr = _pick(rows, tr)

    def body(sel_ref, k_ref, g_ref, o_ref):
        o_ref[...] = k_ref[...] + g_ref[...]

    kept_spec = (pl.BlockSpec((None, None, tr, cols), lambda p, i, s: (p, s[0], i, 0)) if minor else
                 pl.BlockSpec((None, None, tr, cols), lambda p, i, s: (s[0], p, i, 0)))
    return pl.pallas_call(
        body, name=name,
        grid_spec=pltpu.PrefetchScalarGridSpec(
            num_scalar_prefetch=1, grid=(pieces, rows // tr),
            in_specs=[kept_spec, pl.BlockSpec((None, tr, cols), lambda p, i, s: (p, i, 0))],
            out_specs=pl.BlockSpec((None, tr, cols), lambda p, i, s: (p, i, 0))),
        out_shape=jax.ShapeDtypeStruct(got.shape, got.dtype),
        compiler_params=_cparams(("parallel", "parallel")),
    )(sel, kept, got)


def _adamw(w, g, m, v):
    m = ADAM_B1 * m + (1.0 - ADAM_B1) * g
    v = ADAM_B2 * v + (1.0 - ADAM_B2) * (g * g)
    m_hat = m / (1.0 - ADAM_B1 ** ADAM_STEP)
    v_hat = v / (1.0 - ADAM_B2 ** ADAM_STEP)
    delta = -ADAM_LR * (m_hat / (jnp.sqrt(v_hat) + ADAM_EPS) + ADAM_WD * w)
    return delta, m, v


def add_adamw(name, kept, got, sel, w, m, v, tr=128):
    rows, cols = w.shape
    tr = _pick(rows, tr)

    def body(sel_ref, k_ref, g_ref, w_ref, m_ref, v_ref, go_ref, d_ref, mo_ref, vo_ref):
        g = k_ref[...] + g_ref[...]
        go_ref[...] = g
        d_ref[...], mo_ref[...], vo_ref[...] = _adamw(w_ref[...], g, m_ref[...], v_ref[...])

    tile = pl.BlockSpec((tr, cols), lambda i, s: (i, 0))
    return pl.pallas_call(
        body, name=name,
        grid_spec=pltpu.PrefetchScalarGridSpec(
            num_scalar_prefetch=1, grid=(rows // tr,),
            in_specs=[pl.BlockSpec((None, None, tr, cols), lambda i, s: (s[0], 0, i, 0)),
                      pl.BlockSpec((None, tr, cols), lambda i, s: (0, i, 0)), tile, tile, tile],
            out_specs=[tile] * 4),
        out_shape=[jax.ShapeDtypeStruct((rows, cols), F32)] * 4,
        compiler_params=_cparams(("parallel",)),
    )(sel, kept, got, w, m, v)


def small_allreduce_adamw(partial, scale, w, m, v):
    rows = partial.shape[0]

    def body(p_ref, s_ref, w_ref, m_ref, v_ref, g_ref, d_ref, mo_ref, vo_ref, slots, send_sems, recv_sems):
        x, y, c = _coords()
        my_slot = _slab_index((x, y, c))
        slots[my_slot] = p_ref[...]
        copies = []
        for mask in range(1, N_DEV):
            to = tuple(1 - v_ if (mask >> s_) & 1 else v_ for v_, s_ in ((x, 2), (y, 1), (c, 0)))
            copies.append(pltpu.make_async_remote_copy(
                src_ref=p_ref, dst_ref=slots.at[my_slot],
                send_sem=send_sems.at[mask - 1], recv_sem=recv_sems.at[mask - 1],
                device_id=to, device_id_type=MESH))
        for cp in copies:
            cp.start()
        for cp in copies:
            cp.wait()
        total = slots[0]
        for b in range(1, N_DEV):
            total = total + slots[b]
        g = total * s_ref[...]
        g_ref[...] = g
        d_ref[...], mo_ref[...], vo_ref[...] = _adamw(w_ref[...], g, m_ref[...], v_ref[...])

    vm = pl.BlockSpec(memory_space=pltpu.VMEM)
    return pl.pallas_call(
        body, name="small_allreduce_adamw",
        in_specs=[vm] * 5, out_specs=[vm] * 4,
        out_shape=[jax.ShapeDtypeStruct((rows, LANES), F32)] * 4,
        scratch_shapes=[pltpu.VMEM((N_DEV, rows, LANES), F32),
                        pltpu.SemaphoreType.DMA((N_DEV - 1,)), pltpu.SemaphoreType.DMA((N_DEV - 1,))],
        compiler_params=pltpu.CompilerParams(has_side_effects=True),
    )(partial, scale, w, m, v)


def reduce_scatter_adamw(grads, ws, ms, vs):
    x, y, c = _coords()
    sel_x, sel_y, sel_c = (jnp.reshape(v_, (1,)).astype(jnp.int32) for v_ in (x, y, c))
    n = len(grads)
    shapes = [g.shape[1:] for g in grads]
    by_c = [g.reshape(4, 2, *s) for g, s in zip(grads, shapes)]
    got = exchange_halves("rs_exchange_c", by_c, "c")
    s1 = [add_kept_half("rs_add_c_%d" % a, by_c[a], got[a], sel_c, "c") for a in range(n)]
    by_x = [g.reshape(2, 2, *s) for g, s in zip(s1, shapes)]
    got = exchange_halves("rs_exchange_x", by_x, "x")
    s2 = [add_kept_half("rs_add_x_%d" % a, by_x[a], got[a], sel_x, "x") for a in range(n)]
    by_y = [g.reshape(2, 1, *s) for g, s in zip(s2, shapes)]
    got = exchange_halves("rs_exchange_y", by_y, "y")
    return [add_adamw("rs_add_y_adamw_%d" % a, by_y[a], got[a], sel_y, ws[a], ms[a], vs[a]) for a in range(n)]


_SMALL = ("attn_sinks", "attn_out_gain", "rnn_lb_logits", "rnn_norm_gain", "mix_pre_gain", "mix_post_gain",
          "mlp_pre_gain", "mlp_post_gain")


def _pack(parts):
    rows = []
    for p in parts:
        flat = p.reshape(-1).astype(F32)
        pad = (-flat.shape[0]) % LANES
        rows.append(jnp.pad(flat, (0, pad)).reshape(-1, LANES))
    packed = jnp.concatenate(rows, axis=0)
    pad_rows = (-packed.shape[0]) % 8
    return jnp.pad(packed, ((0, pad_rows), (0, 0)))


def _unpack(packed, shapes):
    out, r = [], 0
    for s in shapes:
        size = math.prod(s)
        nrows = -(-size // LANES)
        out.append(packed[r:r + nrows].reshape(-1)[:size].reshape(s))
        r += nrows
    return out


def kernel(x, w_in, attn_sinks, attn_out_gain, rnn_lb_logits, rnn_norm_gain, w_out, mix_pre_gain, mix_post_gain, mlp_pre_gain, mlp_post_gain, w_up, w_down, loss_target, m_w_in, m_attn_sinks, m_attn_out_gain, m_rnn_lb_logits, m_rnn_norm_gain, m_w_out, m_mix_pre_gain, m_mix_post_gain, m_mlp_pre_gain, m_mlp_post_gain, m_w_up, m_w_down, v_w_in, v_attn_sinks, v_attn_out_gain, v_rnn_lb_logits, v_rnn_norm_gain, v_w_out, v_mix_pre_gain, v_mix_post_gain, v_mlp_pre_gain, v_mlp_post_gain, v_w_up, v_w_down):
    xs, target = x[0], loss_target[0]
    d = xs.shape[1]
    small_w = dict(attn_sinks=attn_sinks, attn_out_gain=attn_out_gain, rnn_lb_logits=rnn_lb_logits,
                   rnn_norm_gain=rnn_norm_gain, mix_pre_gain=mix_pre_gain, mix_post_gain=mix_post_gain,
                   mlp_pre_gain=mlp_pre_gain, mlp_post_gain=mlp_post_gain)
    small_m = dict(attn_sinks=m_attn_sinks, attn_out_gain=m_attn_out_gain, rnn_lb_logits=m_rnn_lb_logits,
                   rnn_norm_gain=m_rnn_norm_gain, mix_pre_gain=m_mix_pre_gain, mix_post_gain=m_mix_post_gain,
                   mlp_pre_gain=m_mlp_pre_gain, mlp_post_gain=m_mlp_post_gain)
    small_v = dict(attn_sinks=v_attn_sinks, attn_out_gain=v_attn_out_gain, rnn_lb_logits=v_rnn_lb_logits,
                   rnn_norm_gain=v_rnn_norm_gain, mix_pre_gain=v_mix_pre_gain, mix_post_gain=v_mix_post_gain,
                   mlp_pre_gain=v_mlp_pre_gain, mlp_post_gain=v_mlp_post_gain)

    big_w = [w_in[0].T, w_out[0], w_up[0], w_down[0]]
    big_m = [m_w_in[0].T, m_w_out[0], m_w_up[0], m_w_down[0]]
    big_v = [v_w_in[0].T, v_w_out[0], v_w_up[0], v_w_down[0]]
    wint, wout, wup, wdown = all_gather_slabs([w.astype(BF16) for w in big_w])
    wint = wint.reshape(-1, d)
    wout = wout.reshape(-1, d)
    wdown = wdown.reshape(-1, d)

    probs = jax.nn.softmax(rnn_lb_logits.astype(F32), axis=0)
    lb = probs[0:1]
    grad_x, dwint, dwout, dwup, dwdown, sg = _local_step(
        xs, target, wint, wout, wup, wdown, attn_sinks, attn_out_gain, lb, rnn_norm_gain,
        mix_pre_gain, mix_post_gain, mlp_pre_gain, mlp_post_gain)

    big_g = [dwint.reshape(N_DEV, -1, d), dwout.reshape(N_DEV, -1, d), dwup, dwdown.reshape(N_DEV, -1, d)]
    big_out = reduce_scatter_adamw(big_g, big_w, big_m, big_v)

    n_heads = attn_sinks.shape[1]
    jac = probs[0] * probs[1]
    partial = _pack([sg["sse"], sg["sinks"][0, :n_heads], sg["aog"], jnp.stack([sg["lb"][0], sg["lb"][0]]),
                     jnp.sum(sg["rng"], axis=0), sg["mixpre"], sg["mixpost"], sg["mlppre"], sg["mlppost"]])
    ones = [jnp.ones(small_w[k].shape, F32) for k in _SMALL]
    ones[2] = jnp.stack([jac, -jac])
    scale = _pack([jnp.full((1,), 0.5 / d, F32)] + ones)
    zero = jnp.zeros((1,), F32)
    outs = small_allreduce_adamw(partial, scale, _pack([zero] + [small_w[k] for k in _SMALL]),
                                 _pack([zero] + [small_m[k] for k in _SMALL]),
                                 _pack([jnp.ones((1,), F32)] + [small_v[k] for k in _SMALL]))
    shapes = [(1,)] + [small_w[k].shape for k in _SMALL]
    sgrad, sdelta, snm, snv = (_unpack(o, shapes) for o in outs)
    loss = sgrad[0][0]

    def big(i, j):
        o = big_out[i][j]
        return (o.T if i == 0 else o)[None]

    def ordered(j, smalls):
        s = dict(zip(_SMALL, smalls[1:]))
        return [big(0, j), s["attn_sinks"], s["attn_out_gain"], s["rnn_lb_logits"], s["rnn_norm_gain"], big(1, j),
                s["mix_pre_gain"], s["mix_post_gain"], s["mlp_pre_gain"], s["mlp_post_gain"], big(2, j), big(3, j)]

    return (loss, grad_x[None], *ordered(0, sgrad), *ordered(1, sdelta), *ordered(2, snm), *ordered(3, snv))
```

```python
import math

import jax
import jax.numpy as jnp
from jax import lax
from jax.experimental import pallas as pl
from jax.experimental.pallas import tpu as pltpu

F32 = jnp.float32
BF16 = jnp.bfloat16

HEAD_DIM = 64
N_KV_HEADS = 2
BLOCK = 128
RNN_HEAD_DIM = 128
CHUNK = 64
SUB = 16
EPS = 1e-6

ADAM_LR = 0.001
ADAM_B1 = 0.9
ADAM_B2 = 0.999
ADAM_EPS = 1e-08
ADAM_WD = 0.01
ADAM_STEP = 10

N_DEV = 8
LANES = 128
V7X_VMEM_LIMIT = 56 * 1024 * 1024
MESH = pl.DeviceIdType.MESH
HI = lax.Precision.HIGHEST
ANY = pl.BlockSpec(memory_space=pl.ANY)
_AXES = ("x", "y", "c")


def _cparams(sem=None, **kw):
    return pltpu.CompilerParams(dimension_semantics=sem, vmem_limit_bytes=V7X_VMEM_LIMIT, **kw)


def _dot(a, b, dims):
    return lax.dot_general(a.astype(BF16), b.astype(BF16), (dims, ((), ())), preferred_element_type=F32)


NN = ((1,), (0,))
NT = ((1,), (1,))
TN = ((0,), (0,))


def _pick(n, pref):
    t = min(n, pref)
    while n % t:
        t //= 2
    return t


def _tile(n, pref, mult=LANES):
    if n <= pref:
        return n
    t = pref - pref % mult
    while n % t:
        t -= mult
    return t


def _coords():
    return lax.axis_index("x"), lax.axis_index("y"), lax.axis_index("c")


def _slab_index(dev):
    return 4 * dev[0] + 2 * dev[1] + dev[2]


class _Part:
    def __init__(self, operands, landings, aliases, n_sems, plan):
        self.operands, self.landings, self.aliases, self.n_sems, self.plan = operands, landings, aliases, n_sems, plan


def _merge(*parts):
    operands, landings, aliases, plans = [], [], {}, []
    s0 = 0
    for p in parts:
        o0, l0 = len(operands), len(landings)
        aliases.update({o0 + i: l0 + j for i, j in p.aliases.items()})
        plans.append((p.plan, o0, len(p.operands), l0, len(p.landings), s0))
        operands += p.operands
        landings += p.landings
        s0 += p.n_sems

    def plan(ops, lands, sem):
        starts, waits = [], []
        for f, o0, no, l0, nl, off in plans:
            s, w = f(ops[o0:o0 + no], lands[l0:l0 + nl], lambda kind, k, off=off: sem(kind, off + k))
            starts += s
            waits += w
        return starts, waits

    return _Part(operands, landings, aliases, s0, plan)


def _gather_peers(x, y, c):
    return [(x, y, 1 - c), (1 - x, y, c), (x, 1 - y, c), (1 - x, 1 - y, c)]


def _gather_first(shard):
    def plan(ops, lands, sem):
        x, y, c = _coords()
        me, peers = (x, y, c), _gather_peers(x, y, c)

        def cp(k, block, to):
            return pltpu.make_async_remote_copy(
                src_ref=ops[0], dst_ref=lands[0].at[_slab_index(block)],
                send_sem=sem(0, k), recv_sem=sem(1, k), device_id=to, device_id_type=MESH)

        local = pltpu.make_async_copy(ops[0], lands[0].at[_slab_index(me)], sem(2, 0))
        sends = [cp(k, me, to) for k, to in enumerate(peers)]
        recvs = [cp(k, frm, me) for k, frm in enumerate(peers)]
        return ([local.start] + [s.start for s in sends],
                [local.wait] + [s.wait_send for s in sends] + [r.wait_recv for r in recvs])

    return _Part([shard], [jax.ShapeDtypeStruct((N_DEV, *shard.shape), shard.dtype)], {}, 4, plan)


def _gather_second(gathered):
    def plan(ops, lands, sem):
        x, y, c = _coords()
        sibling = (x, y, 1 - c)
        chips = [(1 - x, y), (x, 1 - y), (1 - x, 1 - y)]

        def cp(k, block):
            slab = lands[0].at[_slab_index(block)]
            return pltpu.make_async_remote_copy(
                src_ref=slab, dst_ref=slab, send_sem=sem(0, k), recv_sem=sem(1, k),
                device_id=sibling, device_id_type=MESH)

        sends = [cp(k, (*chip, c)) for k, chip in enumerate(chips)]
        recvs = [cp(k, (*chip, 1 - c)) for k, chip in enumerate(chips)]
        return [s.start for s in sends], [s.wait_send for s in sends] + [r.wait_recv for r in recvs]

    return _Part([gathered], [jax.ShapeDtypeStruct(gathered.shape, gathered.dtype)], {0: 0}, 3, plan)


def _scatter_step(array, axis):
    minor = axis == "c"
    pieces = array.shape[0] if minor else array.shape[1]

    def plan(ops, lands, sem):
        coords = list(_coords())
        ai = _AXES.index(axis)
        mine = coords[ai]
        peer = list(coords)
        peer[ai] = 1 - mine
        cps = []
        for p in range(pieces):
            src = ops[0].at[p, 1 - mine] if minor else ops[0].at[1 - mine, p]
            cps.append(pltpu.make_async_remote_copy(
                src_ref=src, dst_ref=lands[0].at[p], send_sem=sem(0, p), recv_sem=sem(1, p),
                device_id=tuple(peer), device_id_type=MESH))
        return [cp.start for cp in cps], [cp.wait for cp in cps]

    return _Part([array], [jax.ShapeDtypeStruct((pieces, *array.shape[2:]), array.dtype)], {}, pieces, plan)


def _grid_edges(grid):
    first = last = None
    for ax, n in enumerate(grid):
        p = pl.program_id(ax)
        f, l = p == 0, p == n - 1
        first = f if first is None else jnp.logical_and(first, f)
        last = l if last is None else jnp.logical_and(last, l)
    return first, last


def _call(body, *, name, grid, in_specs, out_specs, out_shape, args, scratch_shapes=(), sem=None, carry=None):
    if carry is None:
        return pl.pallas_call(
            body, name=name, grid=grid, in_specs=list(in_specs), out_specs=list(out_specs),
            out_shape=list(out_shape), scratch_shapes=list(scratch_shapes), compiler_params=_cparams(sem),
        )(*args)
    n_in, n_out, n_scr = len(in_specs), len(out_specs), len(scratch_shapes)
    n_cin, n_cout = len(carry.operands), len(carry.landings)

    def wrapped(*refs):
        ins, cins = refs[:n_in], refs[n_in:n_in + n_cin]
        o0 = n_in + n_cin
        outs, couts = refs[o0:o0 + n_out], refs[o0 + n_out:o0 + n_out + n_cout]
        s0 = o0 + n_out + n_cout
        scr, sems = refs[s0:s0 + n_scr], refs[s0 + n_scr:]
        first, last = _grid_edges(grid)

        def plan():
            return carry.plan(cins, couts, lambda kind, k: sems[kind].at[k])

        def start_all():
            for start in plan()[0]:
                start()

        def wait_all():
            for wait in plan()[1]:
                wait()

        if grid:
            pl.when(first)(start_all)
            body(*ins, *outs, *scr)
            pl.when(last)(wait_all)
        else:
            start_all()
            body(*ins, *outs, *scr)
            wait_all()

    sem_arrays = [pltpu.SemaphoreType.DMA((carry.n_sems,))] * 3
    res = pl.pallas_call(
        wrapped, name=name, grid=grid,
        in_specs=[*in_specs, *[ANY] * n_cin], out_specs=[*out_specs, *[ANY] * n_cout],
        out_shape=[*out_shape, *carry.landings],
        scratch_shapes=[*scratch_shapes, *sem_arrays],
        input_output_aliases={n_in + i: n_out + j for i, j in carry.aliases.items()},
        compiler_params=_cparams(("arbitrary",) * len(grid) if grid else None, has_side_effects=True),
    )(*args, *carry.operands)
    return res[:n_out], res[n_out:]


MM_TILE = 1024
MM_K_TILE = 2048


def _matmul(name, a, b, dims, grid, a_spec, b_spec, out_shape, out_spec, epilogue,
            extras=(), extra_specs=(), prologue=None, carry=None):
    nk = grid[2]
    n_extra = len(extras)
    acc_shape = out_spec.block_shape[-2:]

    def lhs(a_ref):
        return a_ref[...] if prologue is None else prologue(a_ref[...])

    def body_one(a_ref, b_ref, *rest):
        epilogue(_dot(lhs(a_ref), b_ref[...], dims), rest[:n_extra], rest[n_extra:])

    def body_acc(a_ref, b_ref, *rest):
        acc = rest[-1]
        k = pl.program_id(2)
        part = _dot(lhs(a_ref), b_ref[...], dims)

        @pl.when(k == 0)
        def _():
            acc[...] = part

        @pl.when(k > 0)
        def _():
            acc[...] += part

        @pl.when(k == nk - 1)
        def _():
            epilogue(acc[...], rest[:n_extra], rest[n_extra:-1])

    res = _call(body_one if nk == 1 else body_acc, name=name, grid=grid,
                in_specs=[a_spec, b_spec, *extra_specs], out_specs=[out_spec], out_shape=[out_shape],
                args=(a, b, *extras), scratch_shapes=[] if nk == 1 else [pltpu.VMEM(acc_shape, F32)],
                sem=("parallel", "parallel", "arbitrary"), carry=carry)
    return res[0] if carry is None else (res[0][0], res[1])


def _store_as(acc, extra_refs, out_refs):
    out_refs[0][...] = acc.astype(out_refs[0].dtype)


def _square(u):
    return u * u


def mm_nn(name, a, b, out_dtype, tk=None, prologue=None, carry=None):
    (m, kk), n = a.shape, b.shape[1]
    tm, tn = _tile(m, MM_TILE), _tile(n, MM_TILE)
    tk = kk if tk is None else _tile(kk, tk)
    return _matmul(name, a, b, NN, (m // tm, n // tn, kk // tk),
                   pl.BlockSpec((tm, tk), lambda i, j, k: (i, k)),
                   pl.BlockSpec((tk, tn), lambda i, j, k: (k, j)),
                   jax.ShapeDtypeStruct((m, n), out_dtype),
                   pl.BlockSpec((tm, tn), lambda i, j, k: (i, j)), _store_as, prologue=prologue, carry=carry)


def mm_nt(name, a, b, out_dtype, epilogue=_store_as, extras=(), extra_specs=(), carry=None):
    (m, kk), n = a.shape, b.shape[0]
    tm, tn = _tile(m, MM_TILE), _tile(n, MM_TILE)
    return _matmul(name, a, b, NT, (m // tm, n // tn, 1),
                   pl.BlockSpec((tm, kk), lambda i, j, k: (i, 0)),
                   pl.BlockSpec((tn, kk), lambda i, j, k: (j, 0)),
                   jax.ShapeDtypeStruct((m, n), out_dtype),
                   pl.BlockSpec((tm, tn), lambda i, j, k: (i, j)), epilogue,
                   extras=extras, extra_specs=extra_specs, carry=carry)


def mm_tn(name, a, b, out_dtype, prologue=None, carry=None):
    (kk, m), n = a.shape, b.shape[1]
    tm, tn, tk = _tile(m, MM_TILE), _tile(n, MM_TILE), _tile(kk, MM_K_TILE)
    return _matmul(name, a, b, TN, (m // tm, n // tn, kk // tk),
                   pl.BlockSpec((tk, tm), lambda i, j, k: (k, i)),
                   pl.BlockSpec((tk, tn), lambda i, j, k: (k, j)),
                   jax.ShapeDtypeStruct((m, n), out_dtype),
                   pl.BlockSpec((tm, tn), lambda i, j, k: (i, j)), _store_as, prologue=prologue, carry=carry)


def up_proj(h2, wup_slabs):
    (m, kk), (_, _, ns) = h2.shape, wup_slabs.shape
    tm, tn = _tile(m, MM_TILE), _tile(ns, MM_TILE)
    r = ns // tn
    n = N_DEV * ns

    def epi(acc, extra_refs, out_refs):
        out_refs[0][...] = jnp.maximum(acc, 0.0).astype(BF16)

    return _matmul("up_proj", h2, wup_slabs, NN, (m // tm, n // tn, 1),
                   pl.BlockSpec((tm, kk), lambda i, j, k: (i, 0)),
                   pl.BlockSpec((None, kk, tn), lambda i, j, k: (j // r, 0, j % r)),
                   jax.ShapeDtypeStruct((m, n), BF16),
                   pl.BlockSpec((tm, tn), lambda i, j, k: (i, j)), epi)


def down_proj(u, wdown):
    return mm_nn("down_proj", u, wdown, F32, tk=MM_K_TILE, prologue=_square)


def down_bwd_act(dy, wdown, u):
    tm, tn = _tile(dy.shape[0], MM_TILE), _tile(wdown.shape[0], MM_TILE)

    def epi(acc, extra_refs, out_refs):
        out_refs[0][...] = (acc * (2.0 * extra_refs[0][...].astype(F32))).astype(BF16)

    return mm_nt("down_bwd_act", dy, wdown, BF16, epilogue=epi, extras=(u,),
                 extra_specs=(pl.BlockSpec((tm, tn), lambda i, j, k: (i, j)),))


def down_wgrad(u, dy):
    return mm_tn("down_wgrad", u, dy, BF16, prologue=_square)


def up_bwd_x(du, wup_slabs, carry=None):
    (m, kk), (_, n, ns) = du.shape, wup_slabs.shape
    tm, tk = _tile(m, MM_TILE), _tile(ns, MM_TILE)
    r = ns // tk
    return _matmul("up_bwd_x", du, wup_slabs, NT, (m // tm, 1, kk // tk),
                   pl.BlockSpec((tm, tk), lambda i, j, k: (i, k)),
                   pl.BlockSpec((None, n, tk), lambda i, j, k: (k // r, 0, k % r)),
                   jax.ShapeDtypeStruct((m, n), F32),
                   pl.BlockSpec((tm, n), lambda i, j, k: (i, 0)), _store_as, carry=carry)


def up_wgrad(h2, du, carry=None):
    (kk, m), n = h2.shape, du.shape[1]
    ns = n // N_DEV
    tm, tn, tk = _tile(m, MM_TILE), _tile(ns, MM_TILE), _tile(kk, MM_K_TILE)
    r = ns // tn
    return _matmul("up_wgrad", h2, du, TN, (m // tm, n // tn, kk // tk),
                   pl.BlockSpec((tk, tm), lambda i, j, k: (k, i)),
                   pl.BlockSpec((tk, tn), lambda i, j, k: (k, j)),
                   jax.ShapeDtypeStruct((N_DEV, m, ns), BF16),
                   pl.BlockSpec((None, tm, tn), lambda i, j, k: (j // r, i, j % r)), _store_as, carry=carry)


def _rstd(x):
    return lax.rsqrt(jnp.mean(x * x, axis=-1, keepdims=True) + EPS)


def _norm_bwd(x, g, dy):
    r = _rstd(x)
    xh = x * r
    dyg = dy * g
    dx = r * (dyg - xh * jnp.mean(dyg * xh, axis=-1, keepdims=True))
    return dx, jnp.sum(dy * xh, axis=0, keepdims=True)


def _row_spec(tr, d):
    return pl.BlockSpec((tr, d), lambda i: (i, 0))


def _vec_spec(d):
    return pl.BlockSpec((1, d), lambda i: (0, 0))


def _accum(ref, val):
    @pl.when(pl.program_id(0) == 0)
    def _():
        ref[...] = jnp.zeros_like(ref)

    ref[...] += val


def pre_norm(x, g, tr=256):
    t, d = x.shape
    tr = _pick(t, tr)

    def body(x_ref, g_ref, h_ref):
        xx = x_ref[...]
        h_ref[...] = (xx * _rstd(xx) * g_ref[...]).astype(BF16)

    return _call(body, name="pre_norm", grid=(t // tr,),
                 in_specs=[_row_spec(tr, d), _vec_spec(d)], out_specs=[_row_spec(tr, d)],
                 out_shape=[jax.ShapeDtypeStruct((t, d), BF16)], args=(x, g), sem=("parallel",))[0]


def mid_fwd(mixed, g_post, x, g_pre2, tr=256):
    t, d = x.shape
    tr = _pick(t, tr)

    def body(m_ref, gp_ref, x_ref, g2_ref, x1_ref, h2_ref):
        mm = m_ref[...]
        x1 = x_ref[...] + mm * _rstd(mm) * gp_ref[...]
        x1_ref[...] = x1
        h2_ref[...] = (x1 * _rstd(x1) * g2_ref[...]).astype(BF16)

    return _call(body, name="mid_fwd", grid=(t // tr,),
                 in_specs=[_row_spec(tr, d), _vec_spec(d), _row_spec(tr, d), _vec_spec(d)],
                 out_specs=[_row_spec(tr, d), _row_spec(tr, d)],
                 out_shape=[jax.ShapeDtypeStruct((t, d), F32), jax.ShapeDtypeStruct((t, d), BF16)],
                 args=(mixed, g_post, x, g_pre2), sem=("parallel",))


def loss_bwd(y, g_post2, x1, target, tr=256):
    t, d = y.shape
    tr = _pick(t, tr)

    def body(y_ref, g_ref, x1_ref, t_ref, sse_ref, dout_ref, dy_ref, dg_ref):
        yy = y_ref[...]
        g = g_ref[...]
        err = x1_ref[...] + yy * _rstd(yy) * g - t_ref[...]
        _accum(sse_ref, jnp.sum(jnp.sum(err * err, axis=1, keepdims=True), axis=0, keepdims=True))
        dout = err * (1.0 / d)
        dout_ref[...] = dout
        dy, dg = _norm_bwd(yy, g, dout)
        dy_ref[...] = dy.astype(BF16)
        _accum(dg_ref, dg)

    return _call(body, name="loss_bwd", grid=(t // tr,),
                 in_specs=[_row_spec(tr, d), _vec_spec(d), _row_spec(tr, d), _row_spec(tr, d)],
                 out_specs=[pl.BlockSpec((1, 1), lambda i: (0, 0)), _row_spec(tr, d), _row_spec(tr, d), _vec_spec(d)],
                 out_shape=[jax.ShapeDtypeStruct((1, 1), F32), jax.ShapeDtypeStruct((t, d), F32),
                            jax.ShapeDtypeStruct((t, d), BF16), jax.ShapeDtypeStruct((1, d), F32)],
                 args=(y, g_post2, x1, target), sem=("arbitrary",))


def mid_bwd(dh2, x1, g_pre2, dout, mixed, g_post, carry=None, tr=256):
    t, d = x1.shape
    tr = _pick(t, tr)

    def body(dh_ref, x1_ref, g2_ref, do_ref, m_ref, gp_ref, dx1_ref, dm_ref, dg2_ref, dgp_ref):
        d1, dg2 = _norm_bwd(x1_ref[...], g2_ref[...], dh_ref[...])
        dx1 = do_ref[...] + d1
        dx1_ref[...] = dx1
        dm, dgp = _norm_bwd(m_ref[...], gp_ref[...], dx1)
        dm_ref[...] = dm.astype(BF16)
        _accum(dg2_ref, dg2)
        _accum(dgp_ref, dgp)

    return _call(body, name="mid_bwd", grid=(t // tr,),
                 in_specs=[_row_spec(tr, d), _row_spec(tr, d), _vec_spec(d), _row_spec(tr, d), _row_spec(tr, d),
                           _vec_spec(d)],
                 out_specs=[_row_spec(tr, d), _row_spec(tr, d), _vec_spec(d), _vec_spec(d)],
                 out_shape=[jax.ShapeDtypeStruct((t, d), F32), jax.ShapeDtypeStruct((t, d), BF16),
                            jax.ShapeDtypeStruct((1, d), F32), jax.ShapeDtypeStruct((1, d), F32)],
                 args=(dh2, x1, g_pre2, dout, mixed, g_post), sem=("arbitrary",), carry=carry)


def first_bwd(dh1, x, g_pre, dx1, carry=None, tr=256):
    t, d = x.shape
    tr = _pick(t, tr)

    def body(dh_ref, x_ref, g_ref, dx1_ref, gx_ref, dg_ref):
        d0, dg = _norm_bwd(x_ref[...], g_ref[...], dh_ref[...])
        gx_ref[...] = dx1_ref[...] + d0
        _accum(dg_ref, dg)

    return _call(body, name="first_bwd", grid=(t // tr,),
                 in_specs=[_row_spec(tr, d), _row_spec(tr, d), _vec_spec(d), _row_spec(tr, d)],
                 out_specs=[_row_spec(tr, d), _vec_spec(d)],
                 out_shape=[jax.ShapeDtypeStruct((t, d), F32), jax.ShapeDtypeStruct((1, d), F32)],
                 args=(dh1, x, g_pre, dx1), sem=("arbitrary",), carry=carry)


def _attn_geometry():
    r = lax.broadcasted_iota(jnp.int32, (BLOCK, BLOCK), 0)
    c = lax.broadcasted_iota(jnp.int32, (BLOCK, BLOCK), 1)
    dist_cur = (r - c).astype(F32)
    return dist_cur, dist_cur + float(BLOCK), r >= c, c > r


def _attn_probs(qh, kp, kc, slope, sink, geo, has_prev):
    dist_cur, dist_prev, mask_cur, mask_prev = geo
    s_cur = _dot(qh, kc, NT) - slope * dist_cur
    s_prev = _dot(qh, kp, NT) - slope * dist_prev
    s_cur = jnp.where(mask_cur, s_cur, -jnp.inf)
    s_prev = jnp.where(jnp.logical_and(mask_prev, has_prev), s_prev, -jnp.inf)
    m = jnp.maximum(jnp.maximum(jnp.max(s_cur, axis=-1, keepdims=True),
                                jnp.max(s_prev, axis=-1, keepdims=True)), sink)
    p_cur = jnp.exp(s_cur - m)
    p_prev = jnp.exp(s_prev - m)
    p_sink = jnp.exp(sink - m)
    inv = 1.0 / (jnp.sum(p_cur, axis=-1, keepdims=True) + jnp.sum(p_prev, axis=-1, keepdims=True) + p_sink)
    return p_prev * inv, p_cur * inv, p_sink * inv


def attn_fwd(proj, sinks, gain, aw, carry=None):
    t = proj.shape[0]
    kw = N_KV_HEADS * HEAD_DIM
    n_heads = aw // HEAD_DIM
    group = n_heads // N_KV_HEADS
    nb = t // BLOCK
    scale = HEAD_DIM ** -0.5

    def body(sink_ref, q_ref, k_ref, v_ref, g_ref, o_ref, on_ref):
        n = pl.program_id(0)
        cur = pl.multiple_of(n * BLOCK, BLOCK)
        prev = pl.multiple_of(jnp.maximum(n - 1, 0) * BLOCK, BLOCK)
        has_prev = n > 0
        geo = _attn_geometry()
        kc, kp = k_ref[pl.ds(cur, BLOCK), :], k_ref[pl.ds(prev, BLOCK), :]
        vc, vp = v_ref[pl.ds(cur, BLOCK), :], v_ref[pl.ds(prev, BLOCK), :]
        for h in range(n_heads):
            kv = h // group
            ks = slice(kv * HEAD_DIM, (kv + 1) * HEAD_DIM)
            hs = slice(h * HEAD_DIM, (h + 1) * HEAD_DIM)
            slope = 2.0 ** (-8.0 * (h + 1) / n_heads)
            qh = q_ref[:, hs] * scale
            p_prev, p_cur, _ = _attn_probs(qh, kp[:, ks], kc[:, ks], slope, sink_ref[0, h], geo, has_prev)
            o_ref[:, hs] = _dot(p_prev, vp[:, ks], NN) + _dot(p_cur, vc[:, ks], NN)
        o = o_ref[...]
        on_ref[...] = (o * _rstd(o) * g_ref[...]).astype(BF16)

    return _call(body, name="attn_fwd", grid=(nb,),
                 in_specs=[pl.BlockSpec(memory_space=pltpu.SMEM),
                           pl.BlockSpec((BLOCK, aw), lambda n: (n, 0)),
                           pl.BlockSpec((t, kw), lambda n: (0, aw // kw)),
                           pl.BlockSpec((t, kw), lambda n: (0, aw // kw + 1)),
                           pl.BlockSpec((1, aw), lambda n: (0, 0))],
                 out_specs=[pl.BlockSpec((BLOCK, aw), lambda n: (n, 0)), pl.BlockSpec((BLOCK, aw), lambda n: (n, 0))],
                 out_shape=[jax.ShapeDtypeStruct((t, aw), F32), jax.ShapeDtypeStruct((t, aw), BF16)],
                 args=(sinks, proj, proj, proj, gain), sem=("parallel",), carry=carry)


def attn_bwd(proj, sinks, gain, attn_o, dcat, aw, carry=None):
    t = proj.shape[0]
    kw = N_KV_HEADS * HEAD_DIM
    n_heads = aw // HEAD_DIM
    group = n_heads // N_KV_HEADS
    nb = t // BLOCK
    scale = HEAD_DIM ** -0.5

    def body(sink_ref, q_ref, k_ref, v_ref, g_ref, o_ref, dn_ref, dq_ref, dk_ref, dv_ref, dsink_ref, dg_ref, do_ref):
        n = pl.program_id(0)
        cur = pl.multiple_of(n * BLOCK, BLOCK)
        prev = pl.multiple_of(jnp.maximum(n - 1, 0) * BLOCK, BLOCK)
        has_prev = n > 0
        geo = _attn_geometry()

        @pl.when(n == 0)
        def _():
            dk_ref[...] = jnp.zeros_like(dk_ref)
            dv_ref[...] = jnp.zeros_like(dv_ref)
            dsink_ref[...] = jnp.zeros_like(dsink_ref)

        o = o_ref[...]
        do_all, dg = _norm_bwd(o, g_ref[...], dn_ref[...])
        _accum(dg_ref, dg)
        do_ref[...] = do_all
        kc, kp = k_ref[pl.ds(cur, BLOCK), :], k_ref[pl.ds(prev, BLOCK), :]
        vc, vp = v_ref[pl.ds(cur, BLOCK), :], v_ref[pl.ds(prev, BLOCK), :]
        lane = lax.broadcasted_iota(jnp.int32, (1, LANES), 1)
        dsink = jnp.zeros((1, LANES), F32)
        for kv in range(N_KV_HEADS):
            ks = slice(kv * HEAD_DIM, (kv + 1) * HEAD_DIM)
            dkc = jnp.zeros((BLOCK, HEAD_DIM), F32)
            dkp = jnp.zeros((BLOCK, HEAD_DIM), F32)
            dvc = jnp.zeros((BLOCK, HEAD_DIM), F32)
            dvp = jnp.zeros((BLOCK, HEAD_DIM), F32)
            for gidx in range(group):
                h = kv * group + gidx
                hs = slice(h * HEAD_DIM, (h + 1) * HEAD_DIM)
                slope = 2.0 ** (-8.0 * (h + 1) / n_heads)
                qh = q_ref[:, hs] * scale
                p_prev, p_cur, p_sink = _attn_probs(qh, kp[:, ks], kc[:, ks], slope, sink_ref[0, h], geo, has_prev)
                doh = do_ref[:, hs]
                delta = jnp.sum(doh * o_ref[:, hs], axis=-1, keepdims=True)
                ds_cur = p_cur * (_dot(doh, vc[:, ks], NT) - delta)
                ds_prev = p_prev * (_dot(doh, vp[:, ks], NT) - delta)
                dsink = dsink + jnp.where(lane == h, -jnp.sum(p_sink * delta, axis=0, keepdims=True), 0.0)
                dq_ref[:, hs] = ((_dot(ds_cur, kc[:, ks], NN) + _dot(ds_prev, kp[:, ks], NN)) * scale).astype(BF16)
                dkc = dkc + _dot(ds_cur, qh, TN)
                dkp = dkp + _dot(ds_prev, qh, TN)
                dvc = dvc + _dot(p_cur, doh, TN)
                dvp = dvp + _dot(p_prev, doh, TN)
            dk_ref[pl.ds(cur, BLOCK), ks] += dkc
            dv_ref[pl.ds(cur, BLOCK), ks] += dvc

            @pl.when(has_prev)
            def _():
                dk_ref[pl.ds(prev, BLOCK), ks] += dkp
                dv_ref[pl.ds(prev, BLOCK), ks] += dvp
        dsink_ref[...] += dsink

    return _call(body, name="attn_bwd", grid=(nb,),
                 in_specs=[pl.BlockSpec(memory_space=pltpu.SMEM),
                           pl.BlockSpec((BLOCK, aw), lambda n: (n, 0)),
                           pl.BlockSpec((t, kw), lambda n: (0, aw // kw)),
                           pl.BlockSpec((t, kw), lambda n: (0, aw // kw + 1)),
                           pl.BlockSpec((1, aw), lambda n: (0, 0)),
                           pl.BlockSpec((BLOCK, aw), lambda n: (n, 0)),
                           pl.BlockSpec((BLOCK, aw), lambda n: (n, 0))],
                 out_specs=[pl.BlockSpec((BLOCK, aw), lambda n: (n, 0)),
                            pl.BlockSpec((t, kw), lambda n: (0, 0)), pl.BlockSpec((t, kw), lambda n: (0, 0)),
                            pl.BlockSpec((1, LANES), lambda n: (0, 0)), pl.BlockSpec((1, aw), lambda n: (0, 0))],
                 out_shape=[jax.ShapeDtypeStruct((t, aw), BF16), jax.ShapeDtypeStruct((t, kw), F32),
                            jax.ShapeDtypeStruct((t, kw), F32), jax.ShapeDtypeStruct((1, LANES), F32),
                            jax.ShapeDtypeStruct((1, aw), F32)],
                 args=(sinks, proj, proj, proj, gain, attn_o, dcat),
                 scratch_shapes=[pltpu.VMEM((BLOCK, aw), F32)], sem=("arbitrary",), carry=carry)


def _sigmoid(x):
    return 1.0 / (1.0 + jnp.exp(-x))


def _chunk_geometry():
    row = lax.broadcasted_iota(jnp.int32, (CHUNK, CHUNK), 0)
    col = lax.broadcasted_iota(jnp.int32, (CHUNK, CHUNK), 1)
    return row, col


def _cumsum_rows(x, reverse=False):
    row, col = _chunk_geometry()
    tri = (col >= row) if reverse else (col <= row)
    return lax.dot_general(tri.astype(F32), x, ((NN), ((), ())), precision=HI, preferred_element_type=F32)


def _rep_sub(x4):
    k = x4.shape[-1]
    return jnp.broadcast_to(x4[:, None, :], (CHUNK // SUB, SUB, k)).reshape(CHUNK, k)


def _gates(q_r, f_r, lb):
    sg = _sigmoid(f_r)
    f = lb + (1.0 - lb) * sg
    sq = _sigmoid(q_r)
    return sg, f, sq, q_r * sq


def _offdiag_terms(b, b_ref, j):
    c = b_ref[pl.ds(j * SUB + SUB - 1, 1), :]
    return jnp.exp(jnp.minimum(b - c, 0.0)), jnp.exp(jnp.minimum(c - b, 0.0))


def _offdiag_mask(j):
    row, col = _chunk_geometry()
    return jnp.logical_and(row >= (j + 1) * SUB, (col // SUB) == j)


def _diag_mask():
    row, col = _chunk_geometry()
    return jnp.logical_and((row // SUB) == (col // SUB), row >= col)


def hgrn_fwd(proj, lb, norm_gain, col0, rw, carry=None):
    t = proj.shape[0]
    nh = rw // RNN_HEAD_DIM
    nc = t // CHUNK
    kd = RNN_HEAD_DIM
    cb = col0 // kd
    nsub = CHUNK // SUB

    def body(q_ref, f_ref, i_ref, g_ref, lb_ref, ng_ref, rnn_ref, o_ref, att_ref, st_ref, state, b_ref, k_ref):
        c = pl.program_id(1)

        @pl.when(c == 0)
        def _():
            state[...] = jnp.zeros_like(state)

        st_ref[...] = state[...]
        _, f, _, q = _gates(q_ref[...], f_ref[...], lb_ref[...])
        k = 1.0 - f
        v = i_ref[...]
        b = _cumsum_rows(jnp.log(f))
        b_ref[...] = b
        k_ref[...] = k
        row, col = _chunk_geometry()
        att = jnp.zeros((CHUNK, CHUNK), F32)
        for j in range(nsub - 1):
            e_row, e_col = _offdiag_terms(b, b_ref, j)
            att = att + jnp.where(_offdiag_mask(j), _dot(q * e_row, k * e_col, NT), 0.0)
        rloc = lax.broadcasted_iota(jnp.int32, (CHUNK, kd), 0) % SUB
        for r in range(SUB):
            bs = _rep_sub(b_ref[pl.ds(r, nsub, stride=SUB), :])
            ks = _rep_sub(k_ref[pl.ds(r, nsub, stride=SUB), :])
            e = jnp.exp(jnp.where(rloc >= r, b - bs, -jnp.inf))
            colsum = jnp.sum(q * e * ks, axis=-1, keepdims=True)
            att = jnp.where(jnp.logical_and((col % SUB) == r, (row // SUB) == (col // SUB)), colsum, att)
        att_ref[...] = att
        b_last = b_ref[pl.ds(CHUNK - 1, 1), :]
        o = _dot(q * jnp.exp(b), state[...], NT) + _dot(att, v, NN)
        state[...] = state[...] * jnp.exp(b_last) + _dot(v, k * jnp.exp(b_last - b), TN)
        o_ref[...] = o
        gate = g_ref[...]
        gate = gate * _sigmoid(gate)
        rnn_ref[...] = (o * _rstd(o) * ng_ref[...] * gate).astype(BF16)

    def col(kidx):
        return pl.BlockSpec((CHUNK, kd), lambda h, c: (c, cb + kidx * nh + h))

    return _call(body, name="hgrn_fwd", grid=(nh, nc),
                 in_specs=[col(0), col(1), col(2), col(3),
                           pl.BlockSpec((1, kd), lambda h, c: (0, h)), pl.BlockSpec((1, kd), lambda h, c: (0, 0))],
                 out_specs=[pl.BlockSpec((CHUNK, kd), lambda h, c: (c, h)),
                            pl.BlockSpec((CHUNK, kd), lambda h, c: (c, h)),
                            pl.BlockSpec((None, CHUNK, CHUNK), lambda h, c: (h, c, 0)),
                            pl.BlockSpec((None, None, kd, kd), lambda h, c: (c, h, 0, 0))],
                 out_shape=[jax.ShapeDtypeStruct((t, rw), BF16), jax.ShapeDtypeStruct((t, rw), F32),
                            jax.ShapeDtypeStruct((nh, t, CHUNK), F32), jax.ShapeDtypeStruct((nc, nh, kd, kd), F32)],
                 args=(proj, proj, proj, proj, lb, norm_gain),
                 scratch_shapes=[pltpu.VMEM((kd, kd), F32), pltpu.VMEM((CHUNK, kd), F32), pltpu.VMEM((CHUNK, kd), F32)],
                 sem=("parallel", "arbitrary"), carry=carry)


def hgrn_bwd(proj, lb, norm_gain, o_all, att_all, st_all, dcat, col0, rw, carry=None):
    t = proj.shape[0]
    nh = rw // RNN_HEAD_DIM
    nc = t // CHUNK
    kd = RNN_HEAD_DIM
    cb = col0 // kd
    nsub = CHUNK // SUB
    dcb = (dcat.shape[1] - rw) // kd

    def body(q_ref, f_ref, i_ref, g_ref, lb_ref, ng_ref, o_ref, att_ref, st0_ref, st1_ref, d_ref,
             dq_ref, df_ref, di_ref, dg_ref, dlb_ref, dng_ref, dstate, b_ref, k_ref, dks_ref):
        ci = pl.program_id(1)

        @pl.when(ci == 0)
        def _():
            dstate[...] = jnp.zeros_like(dstate)
            dlb_ref[...] = jnp.zeros_like(dlb_ref)
            dng_ref[...] = jnp.zeros_like(dng_ref)

        lbv = lb_ref[...]
        q_r, g_r = q_ref[...], g_ref[...]
        sg, f, sq, q = _gates(q_r, f_ref[...], lbv)
        k = 1.0 - f
        v = i_ref[...]
        b = _cumsum_rows(jnp.log(f))
        b_ref[...] = b
        k_ref[...] = k
        row, col = _chunk_geometry()

        o = o_ref[...]
        ng = ng_ref[...]
        sgg = _sigmoid(g_r)
        gate = g_r * sgg
        d_rnn = d_ref[...]
        r = _rstd(o)
        oh = o * r
        dg_ref[...] = (d_rnn * oh * ng * (sgg * (1.0 + g_r * (1.0 - sgg)))).astype(BF16)
        d_on = d_rnn * gate
        dng_ref[...] += jnp.sum(d_on * oh, axis=0, keepdims=True)
        dyg = d_on * ng
        do = r * (dyg - oh * jnp.mean(dyg * oh, axis=-1, keepdims=True))

        st0 = st0_ref[...]
        dst = dstate[...]
        b_last = b_ref[pl.ds(CHUNK - 1, 1), :]
        eb = jnp.exp(b)
        kdec = k * jnp.exp(b_last - b)
        att = att_ref[...]
        da = jnp.where(row >= col, _dot(do, v, NT), 0.0)

        dq = _dot(do, st0, NN) * eb
        dk = _dot(v, dst, NN) * jnp.exp(b_last - b)
        dv = _dot(att, do, TN) + _dot(kdec, dst, NT)
        for j in range(nsub - 1):
            e_row, e_col = _offdiag_terms(b, b_ref, j)
            daj = jnp.where(_offdiag_mask(j), da, 0.0)
            dq = dq + e_row * _dot(daj, k * e_col, NN)
            dk = dk + e_col * _dot(daj, q * e_row, TN)
        rloc = lax.broadcasted_iota(jnp.int32, (CHUNK, kd), 0) % SUB
        dad = jnp.where(_diag_mask(), da, 0.0)
        for rr in range(SUB):
            bs = _rep_sub(b_ref[pl.ds(rr, nsub, stride=SUB), :])
            ks = _rep_sub(k_ref[pl.ds(rr, nsub, stride=SUB), :])
            e = jnp.exp(jnp.where(rloc >= rr, b - bs, -jnp.inf))
            dacol = jnp.sum(jnp.where((col % SUB) == rr, dad, 0.0), axis=-1, keepdims=True)
            w = dacol * e
            dq = dq + w * ks
            dks_ref[pl.ds(rr, nsub, stride=SUB), :] = jnp.sum((w * q).reshape(nsub, SUB, kd), axis=1)
        dk = dk + dks_ref[...]

        gsum = jnp.sum(dst * st1_ref[...], axis=0, keepdims=True)
        dlf = _cumsum_rows(q * dq - k * dk, reverse=True) + gsum
        dfv = dlf / f - dk
        df_ref[...] = (dfv * (1.0 - lbv) * sg * (1.0 - sg)).astype(BF16)
        dlb_ref[...] += jnp.sum(dfv * (1.0 - sg), axis=0, keepdims=True)
        dq_ref[...] = (dq * (sq * (1.0 + q_r * (1.0 - sq)))).astype(BF16)
        di_ref[...] = dv.astype(BF16)
        dstate[...] = dst * jnp.exp(b_last) + _dot(do, q * eb, TN)

    def rev(c):
        return nc - 1 - c

    def col_in(kidx):
        return pl.BlockSpec((CHUNK, kd), lambda h, c: (rev(c), cb + kidx * nh + h))

    tile = pl.BlockSpec((CHUNK, kd), lambda h, c: (rev(c), h))
    return _call(body, name="hgrn_bwd", grid=(nh, nc),
                 in_specs=[col_in(0), col_in(1), col_in(2), col_in(3),
                           pl.BlockSpec((1, kd), lambda h, c: (0, h)), pl.BlockSpec((1, kd), lambda h, c: (0, 0)),
                           tile,
                           pl.BlockSpec((None, CHUNK, CHUNK), lambda h, c: (h, rev(c), 0)),
                           pl.BlockSpec((None, None, kd, kd), lambda h, c: (rev(c), h, 0, 0)),
                           pl.BlockSpec((None, None, kd, kd),
                                        lambda h, c: (jnp.minimum(rev(c) + 1, nc - 1), h, 0, 0)),
                           pl.BlockSpec((CHUNK, kd), lambda h, c: (rev(c), dcb + h))],
                 out_specs=[tile, tile, tile, tile,
                            pl.BlockSpec((1, kd), lambda h, c: (0, h)),
                            pl.BlockSpec((None, 1, kd), lambda h, c: (h, 0, 0))],
                 out_shape=[jax.ShapeDtypeStruct((t, rw), BF16)] * 4 + [jax.ShapeDtypeStruct((1, rw), F32),
                                                                        jax.ShapeDtypeStruct((nh, 1, kd), F32)],
                 args=(proj, proj, proj, proj, lb, norm_gain, o_all, att_all, st_all, st_all, dcat),
                 scratch_shapes=[pltpu.VMEM((kd, kd), F32), pltpu.VMEM((CHUNK, kd), F32),
                                 pltpu.VMEM((CHUNK, kd), F32), pltpu.VMEM((CHUNK, kd), F32)],
                 sem=("parallel", "arbitrary"), carry=carry)


def all_gather_slabs(shards):
    n = len(shards)

    def body(*refs):
        ins, outs = refs[:n], refs[n:2 * n]
        send_sems, recv_sems, local_sems = refs[2 * n:]
        x, y, c = _coords()
        me, sibling = (x, y, c), (x, y, 1 - c)
        chips = [(1 - x, y), (x, 1 - y), (1 - x, 1 - y)]

        def copy(a, k, block, to, src=None):
            slab = outs[a].at[_slab_index(block)]
            return pltpu.make_async_remote_copy(
                src_ref=slab if src is None else src, dst_ref=slab,
                send_sem=send_sems.at[a, k], recv_sem=recv_sems.at[a, k],
                device_id=to, device_id_type=MESH)

        mine = [pltpu.make_async_copy(ins[a], outs[a].at[_slab_index(me)], local_sems.at[a]) for a in range(n)]
        for cp in mine:
            cp.start()
        first = []
        for a in range(n):
            first.append(copy(a, 0, me, sibling, src=ins[a]))
            first += [copy(a, 1 + j, me, (*chip, c), src=ins[a]) for j, chip in enumerate(chips)]
        for cp in first:
            cp.start()
        passed = []
        for j, chip in enumerate(chips):
            for a in range(n):
                copy(a, 1 + j, (*chip, c), me).wait_recv()
                fwd = copy(a, 4 + j, (*chip, c), sibling)
                fwd.start()
                passed.append(fwd)
        for a in range(n):
            copy(a, 0, sibling, me).wait_recv()
            for j, chip in enumerate(chips):
                copy(a, 4 + j, (*chip, 1 - c), me).wait_recv()
        for cp in first + passed:
            cp.wait_send()
        for cp in mine:
            cp.wait()

    return pl.pallas_call(
        body, name="all_gather_weights",
        in_specs=[ANY] * n, out_specs=[ANY] * n,
        out_shape=[jax.ShapeDtypeStruct((N_DEV, *s.shape), s.dtype) for s in shards],
        scratch_shapes=[pltpu.SemaphoreType.DMA((n, 7)), pltpu.SemaphoreType.DMA((n, 7)),
                        pltpu.SemaphoreType.DMA((n,))],
        compiler_params=pltpu.CompilerParams(has_side_effects=True),
    )(*shards)


def exchange_halves(name, array, axis):
    return _call(lambda: None, name=name, grid=(), in_specs=[], out_specs=[], out_shape=[], args=(),
                 carry=_scatter_step(array, axis))[1][0]


def add_kept_half(name, kept, got, sel, axis, tr=256):
    minor = axis == "c"
    pieces, rows, cols = got.shape
    tr = _pick(rows, tr)

    def body(sel_ref, k_ref, g_ref, o_ref):
        o_ref[...] = (k_ref[...].astype(F32) + g_ref[...].astype(F32)).astype(o_ref.dtype)

    kept_spec = (pl.BlockSpec((None, None, tr, cols), lambda p, i, s: (p, s[0], i, 0)) if minor else
                 pl.BlockSpec((None, None, tr, cols), lambda p, i, s: (s[0], p, i, 0)))
    return pl.pallas_call(
        body, name=name,
        grid_spec=pltpu.PrefetchScalarGridSpec(
            num_scalar_prefetch=1, grid=(pieces, rows // tr),
            in_specs=[kept_spec, pl.BlockSpec((None, tr, cols), lambda p, i, s: (p, i, 0))],
            out_specs=pl.BlockSpec((None, tr, cols), lambda p, i, s: (p, i, 0))),
        out_shape=jax.ShapeDtypeStruct(got.shape, got.dtype),
        compiler_params=_cparams(("parallel", "parallel")),
    )(sel, kept, got)


def _adamw(w, g, m, v):
    m = ADAM_B1 * m + (1.0 - ADAM_B1) * g
    v = ADAM_B2 * v + (1.0 - ADAM_B2) * (g * g)
    m_hat = m / (1.0 - ADAM_B1 ** ADAM_STEP)
    v_hat = v / (1.0 - ADAM_B2 ** ADAM_STEP)
    delta = -ADAM_LR * (m_hat / (jnp.sqrt(v_hat) + ADAM_EPS) + ADAM_WD * w)
    return delta, m, v


def add_adamw(name, kept, got, sel, w, m, v, tr=128):
    rows, cols = w.shape
    tr = _pick(rows, tr)

    def body(sel_ref, k_ref, g_ref, w_ref, m_ref, v_ref, go_ref, d_ref, mo_ref, vo_ref):
        g = k_ref[...].astype(F32) + g_ref[...].astype(F32)
        go_ref[...] = g
        d_ref[...], mo_ref[...], vo_ref[...] = _adamw(w_ref[...], g, m_ref[...], v_ref[...])

    tile = pl.BlockSpec((tr, cols), lambda i, s: (i, 0))
    return pl.pallas_call(
        body, name=name,
        grid_spec=pltpu.PrefetchScalarGridSpec(
            num_scalar_prefetch=1, grid=(rows // tr,),
            in_specs=[pl.BlockSpec((None, None, tr, cols), lambda i, s: (s[0], 0, i, 0)),
                      pl.BlockSpec((None, tr, cols), lambda i, s: (0, i, 0)), tile, tile, tile],
            out_specs=[tile] * 4),
        out_shape=[jax.ShapeDtypeStruct((rows, cols), F32)] * 4,
        compiler_params=_cparams(("parallel",)),
    )(sel, kept, got, w, m, v)


def small_allreduce_adamw(partial, scale, w, m, v):
    rows = partial.shape[0]

    def body(p_ref, s_ref, w_ref, m_ref, v_ref, g_ref, d_ref, mo_ref, vo_ref, slots, send_sems, recv_sems):
        x, y, c = _coords()
        my_slot = _slab_index((x, y, c))
        slots[my_slot] = p_ref[...]
        copies = []
        for mask in range(1, N_DEV):
            to = tuple(1 - v_ if (mask >> s_) & 1 else v_ for v_, s_ in ((x, 2), (y, 1), (c, 0)))
            copies.append(pltpu.make_async_remote_copy(
                src_ref=p_ref, dst_ref=slots.at[my_slot],
                send_sem=send_sems.at[mask - 1], recv_sem=recv_sems.at[mask - 1],
                device_id=to, device_id_type=MESH))
        for cp in copies:
            cp.start()
        for cp in copies:
            cp.wait()
        total = slots[0]
        for b in range(1, N_DEV):
            total = total + slots[b]
        g = total * s_ref[...]
        g_ref[...] = g
        d_ref[...], mo_ref[...], vo_ref[...] = _adamw(w_ref[...], g, m_ref[...], v_ref[...])

    vm = pl.BlockSpec(memory_space=pltpu.VMEM)
    return pl.pallas_call(
        body, name="small_allreduce_adamw",
        in_specs=[vm] * 5, out_specs=[vm] * 4,
        out_shape=[jax.ShapeDtypeStruct((rows, LANES), F32)] * 4,
        scratch_shapes=[pltpu.VMEM((N_DEV, rows, LANES), F32),
                        pltpu.SemaphoreType.DMA((N_DEV - 1,)), pltpu.SemaphoreType.DMA((N_DEV - 1,))],
        compiler_params=pltpu.CompilerParams(has_side_effects=True),
    )(partial, scale, w, m, v)


_SMALL = ("attn_sinks", "attn_out_gain", "rnn_lb_logits", "rnn_norm_gain", "mix_pre_gain", "mix_post_gain",
          "mlp_pre_gain", "mlp_post_gain")


def _pack(parts):
    rows = []
    for p in parts:
        flat = p.reshape(-1).astype(F32)
        pad = (-flat.shape[0]) % LANES
        rows.append(jnp.pad(flat, (0, pad)).reshape(-1, LANES))
    packed = jnp.concatenate(rows, axis=0)
    pad_rows = (-packed.shape[0]) % 8
    return jnp.pad(packed, ((0, pad_rows), (0, 0)))


def _unpack(packed, shapes):
    out, r = [], 0
    for s in shapes:
        size = math.prod(s)
        nrows = -(-size // LANES)
        out.append(packed[r:r + nrows].reshape(-1)[:size].reshape(s))
        r += nrows
    return out


class _Scatter:
    def __init__(self, tag, grad, sels):
        self.tag, self.sels = tag, sels
        self.shape = grad.shape[1:]
        self.cur = grad.reshape(4, 2, *self.shape)
        self.stage = 0

    def step(self):
        return _scatter_step(self.cur, "cxy"[self.stage])

    def land(self, got, w=None, m=None, v=None):
        axis = "cxy"[self.stage]
        name = "rs_add_%s_%s" % (axis, self.tag)
        sel = self.sels[axis]
        self.stage += 1
        if axis == "y":
            return add_adamw(name, self.cur, got, sel, w, m, v)
        summed = add_kept_half(name, self.cur, got, sel, axis)
        self.cur = summed.reshape(2, summed.shape[0] // 2, *self.shape)
        return None


def kernel(x, w_in, attn_sinks, attn_out_gain, rnn_lb_logits, rnn_norm_gain, w_out, mix_pre_gain, mix_post_gain, mlp_pre_gain, mlp_post_gain, w_up, w_down, loss_target, m_w_in, m_attn_sinks, m_attn_out_gain, m_rnn_lb_logits, m_rnn_norm_gain, m_w_out, m_mix_pre_gain, m_mix_post_gain, m_mlp_pre_gain, m_mlp_post_gain, m_w_up, m_w_down, v_w_in, v_attn_sinks, v_attn_out_gain, v_rnn_lb_logits, v_rnn_norm_gain, v_w_out, v_mix_pre_gain, v_mix_post_gain, v_mlp_pre_gain, v_mlp_post_gain, v_w_up, v_w_down):
    xs, target = x[0], loss_target[0]
    t, d = xs.shape
    aw = d // 2
    rw = d - aw
    col0 = aw + 2 * N_KV_HEADS * HEAD_DIM
    small_w = dict(attn_sinks=attn_sinks, attn_out_gain=attn_out_gain, rnn_lb_logits=rnn_lb_logits,
                   rnn_norm_gain=rnn_norm_gain, mix_pre_gain=mix_pre_gain, mix_post_gain=mix_post_gain,
                   mlp_pre_gain=mlp_pre_gain, mlp_post_gain=mlp_post_gain)
    small_m = dict(attn_sinks=m_attn_sinks, attn_out_gain=m_attn_out_gain, rnn_lb_logits=m_rnn_lb_logits,
                   rnn_norm_gain=m_rnn_norm_gain, mix_pre_gain=m_mix_pre_gain, mix_post_gain=m_mix_post_gain,
                   mlp_pre_gain=m_mlp_pre_gain, mlp_post_gain=m_mlp_post_gain)
    small_v = dict(attn_sinks=v_attn_sinks, attn_out_gain=v_attn_out_gain, rnn_lb_logits=v_rnn_lb_logits,
                   rnn_norm_gain=v_rnn_norm_gain, mix_pre_gain=v_mix_pre_gain, mix_post_gain=v_mix_post_gain,
                   mlp_pre_gain=v_mlp_pre_gain, mlp_post_gain=v_mlp_post_gain)
    cx, cy, cc = _coords()
    sels = {a: jnp.reshape(v_, (1,)).astype(jnp.int32) for a, v_ in (("x", cx), ("y", cy), ("c", cc))}

    w_in_t, m_in_t, v_in_t = w_in[0].T, m_w_in[0].T, v_w_in[0].T
    s_in, s_out, s_up, s_down = (w.astype(BF16) for w in (w_in_t, w_out[0], w_up[0], w_down[0]))
    probs = jax.nn.softmax(rnn_lb_logits.astype(F32), axis=0)
    lb = probs[0:1]

    wint = all_gather_slabs([s_in])[0].reshape(-1, d)
    h1 = pre_norm(xs, mix_pre_gain)
    proj, (wout_half,) = mm_nt("in_proj", h1, wint, F32, carry=_gather_first(s_out))
    (attn_o, attn_n), (wout, wup_half) = attn_fwd(
        proj, attn_sinks, attn_out_gain, aw, carry=_merge(_gather_second(wout_half), _gather_first(s_up)))
    (rnn, o_r, att, st), (wup, wdown_half) = hgrn_fwd(
        proj, lb, rnn_norm_gain, col0, rw, carry=_merge(_gather_second(wup_half), _gather_first(s_down)))
    wout = wout.reshape(-1, d)
    cat = jnp.concatenate([attn_n, rnn], axis=1)
    mixed, (wdown,) = mm_nn("out_proj", cat, wout, F32, carry=_gather_second(wdown_half))
    wdown = wdown.reshape(-1, d)
    x1, h2 = mid_fwd(mixed, mix_post_gain, xs, mlp_pre_gain)
    u = up_proj(h2, wup)
    y = down_proj(u, wdown)
    sse, dout, dy, dg_mlppost = loss_bwd(y, mlp_post_gain, x1, target)

    du = down_bwd_act(dy, wdown, u)
    rs_down = _Scatter("down", down_wgrad(u, dy).reshape(N_DEV, -1, d), sels)
    dh2, (got,) = up_bwd_x(du, wup, carry=rs_down.step())
    rs_down.land(got)
    dwup, (got,) = up_wgrad(h2, du, carry=rs_down.step())
    rs_down.land(got)
    rs_up = _Scatter("up", dwup, sels)
    (dx1, dmixed, dg_mlppre, dg_mixpost), (got,) = mid_bwd(dh2, x1, mlp_pre_gain, dout, mixed, mix_post_gain,
                                                          carry=rs_up.step())
    rs_up.land(got)
    dcat = mm_nt("out_bwd_x", dmixed, wout, F32)
    rs_out = _Scatter("out", mm_tn("out_wgrad", cat, dmixed, BF16).reshape(N_DEV, -1, d), sels)
    (dq_r, df_r, di_r, dg_r, dlb, dng), (got_d, got_u, got_o) = hgrn_bwd(
        proj, lb, rnn_norm_gain, o_r, att, st, dcat, col0, rw,
        carry=_merge(rs_down.step(), rs_up.step(), rs_out.step()))
    out_down = rs_down.land(got_d, w_down[0], m_w_down[0], v_w_down[0])
    rs_up.land(got_u)
    rs_out.land(got_o)
    (dq_a, dk_a, dv_a, dsinks, daog), (got_u, got_o) = attn_bwd(
        proj, attn_sinks, attn_out_gain, attn_o, dcat, aw, carry=_merge(rs_up.step(), rs_out.step()))
    out_up = rs_up.land(got_u, w_up[0], m_w_up[0], v_w_up[0])
    rs_out.land(got_o)
    dproj = jnp.concatenate([dq_a, dk_a.astype(BF16), dv_a.astype(BF16), dq_r, df_r, di_r, dg_r], axis=1)
    dh1, (got,) = mm_nn("in_bwd_x", dproj, wint, F32, tk=MM_K_TILE, carry=rs_out.step())
    out_out = rs_out.land(got, w_out[0], m_w_out[0], v_w_out[0])
    rs_in = _Scatter("in", mm_tn("in_wgrad", dproj, h1, BF16).reshape(N_DEV, -1, d), sels)
    (grad_x, dg_mixpre), (got,) = first_bwd(dh1, xs, mix_pre_gain, dx1, carry=rs_in.step())
    rs_in.land(got)
    rs_in.land(exchange_halves("rs_exchange_x_in", rs_in.cur, "x"))
    out_in = rs_in.land(exchange_halves("rs_exchange_y_in", rs_in.cur, "y"), w_in_t, m_in_t, v_in_t)
    big_out = [out_in, out_out, out_up, out_down]

    n_heads = attn_sinks.shape[1]
    jac = probs[0] * probs[1]
    partial = _pack([sse, dsinks[0, :n_heads], daog, jnp.stack([dlb[0], dlb[0]]), jnp.sum(dng, axis=0),
                     dg_mixpre, dg_mixpost, dg_mlppre, dg_mlppost])
    ones = [jnp.ones(small_w[k].shape, F32) for k in _SMALL]
    ones[2] = jnp.stack([jac, -jac])
    scale = _pack([jnp.full((1,), 0.5 / d, F32)] + ones)
    zero = jnp.zeros((1,), F32)
    outs = small_allreduce_adamw(partial, scale, _pack([zero] + [small_w[k] for k in _SMALL]),
                                 _pack([zero] + [small_m[k] for k in _SMALL]),
                                 _pack([jnp.ones((1,), F32)] + [small_v[k] for k in _SMALL]))
    shapes = [(1,)] + [small_w[k].shape for k in _SMALL]
    sgrad, sdelta, snm, snv = (_unpack(o, shapes) for o in outs)
    loss = sgrad[0][0]

    def big(i, j):
        o = big_out[i][j]
        return (o.T if i == 0 else o)[None]

    def ordered(j, smalls):
        s = dict(zip(_SMALL, smalls[1:]))
        return [big(0, j), s["attn_sinks"], s["attn_out_gain"], s["rnn_lb_logits"], s["rnn_norm_gain"], big(1, j),
                s["mix_pre_gain"], s["mix_post_gain"], s["mlp_pre_gain"], s["mlp_post_gain"], big(2, j), big(3, j)]

    return (loss, grad_x[None], *ordered(0, sgrad), *ordered(1, sdelta), *ordered(2, snm), *ordered(3, snv))
```

```python
import math

import jax
import jax.numpy as jnp
from jax import lax
from jax.experimental import pallas as pl
from jax.experimental.pallas import tpu as pltpu

F32 = jnp.float32
BF16 = jnp.bfloat16

HEAD_DIM = 64
N_KV_HEADS = 2
BLOCK = 128
RNN_HEAD_DIM = 128
CHUNK = 64
SUB = 16
EPS = 1e-6

ADAM_LR = 0.001
ADAM_B1 = 0.9
ADAM_B2 = 0.999
ADAM_EPS = 1e-08
ADAM_WD = 0.01
ADAM_STEP = 10

N_DEV = 8
LANES = 128
V7X_VMEM_LIMIT = 56 * 1024 * 1024
MESH = pl.DeviceIdType.MESH
HI = lax.Precision.HIGHEST
ANY = pl.BlockSpec(memory_space=pl.ANY)
_AXES = ("x", "y", "c")


def _cparams(sem=None, **kw):
    return pltpu.CompilerParams(dimension_semantics=sem, vmem_limit_bytes=V7X_VMEM_LIMIT, **kw)


def _dot(a, b, dims):
    return lax.dot_general(a.astype(BF16), b.astype(BF16), (dims, ((), ())), preferred_element_type=F32)


NN = ((1,), (0,))
NT = ((1,), (1,))
TN = ((0,), (0,))


def _pick(n, pref):
    t = min(n, pref)
    while n % t:
        t //= 2
    return t


def _tile(n, pref, mult=LANES):
    if n <= pref:
        return n
    t = pref - pref % mult
    while n % t:
        t -= mult
    return t


def _coords():
    return lax.axis_index("x"), lax.axis_index("y"), lax.axis_index("c")


def _slab_index(dev):
    return 4 * dev[0] + 2 * dev[1] + dev[2]


class _Part:
    def __init__(self, operands, landings, aliases, n_sems, plan):
        self.operands, self.landings, self.aliases, self.n_sems, self.plan = operands, landings, aliases, n_sems, plan


def _merge(*parts):
    operands, landings, aliases, plans = [], [], {}, []
    s0 = 0
    for p in parts:
        o0, l0 = len(operands), len(landings)
        aliases.update({o0 + i: l0 + j for i, j in p.aliases.items()})
        plans.append((p.plan, o0, len(p.operands), l0, len(p.landings), s0))
        operands += p.operands
        landings += p.landings
        s0 += p.n_sems

    def plan(ops, lands, sem):
        starts, waits = [], []
        for f, o0, no, l0, nl, off in plans:
            s, w = f(ops[o0:o0 + no], lands[l0:l0 + nl], lambda kind, k, off=off: sem(kind, off + k))
            starts += s
            waits += w
        return starts, waits

    return _Part(operands, landings, aliases, s0, plan)


def _gather_peers(x, y, c):
    return [(x, y, 1 - c), (1 - x, y, c), (x, 1 - y, c), (1 - x, 1 - y, c)]


def _gather_first(shard):
    def plan(ops, lands, sem):
        x, y, c = _coords()
        me, peers = (x, y, c), _gather_peers(x, y, c)

        def cp(k, block, to):
            return pltpu.make_async_remote_copy(
                src_ref=ops[0], dst_ref=lands[0].at[_slab_index(block)],
                send_sem=sem(0, k), recv_sem=sem(1, k), device_id=to, device_id_type=MESH)

        local = pltpu.make_async_copy(ops[0], lands[0].at[_slab_index(me)], sem(2, 0))
        sends = [cp(k, me, to) for k, to in enumerate(peers)]
        recvs = [cp(k, frm, me) for k, frm in enumerate(peers)]
        return ([local.start] + [s.start for s in sends],
                [local.wait] + [s.wait_send for s in sends] + [r.wait_recv for r in recvs])

    return _Part([shard], [jax.ShapeDtypeStruct((N_DEV, *shard.shape), shard.dtype)], {}, 4, plan)


def _gather_second(gathered):
    def plan(ops, lands, sem):
        x, y, c = _coords()
        sibling = (x, y, 1 - c)
        chips = [(1 - x, y), (x, 1 - y), (1 - x, 1 - y)]

        def cp(k, block):
            slab = lands[0].at[_slab_index(block)]
            return pltpu.make_async_remote_copy(
                src_ref=slab, dst_ref=slab, send_sem=sem(0, k), recv_sem=sem(1, k),
                device_id=sibling, device_id_type=MESH)

        sends = [cp(k, (*chip, c)) for k, chip in enumerate(chips)]
        recvs = [cp(k, (*chip, 1 - c)) for k, chip in enumerate(chips)]
        return [s.start for s in sends], [s.wait_send for s in sends] + [r.wait_recv for r in recvs]

    return _Part([gathered], [jax.ShapeDtypeStruct(gathered.shape, gathered.dtype)], {0: 0}, 3, plan)


def _scatter_step(array, axis):
    minor = axis == "c"
    pieces = array.shape[0] if minor else array.shape[1]

    def plan(ops, lands, sem):
        coords = list(_coords())
        ai = _AXES.index(axis)
        mine = coords[ai]
        peer = list(coords)
        peer[ai] = 1 - mine
        cps = []
        for p in range(pieces):
            src = ops[0].at[p, 1 - mine] if minor else ops[0].at[1 - mine, p]
            cps.append(pltpu.make_async_remote_copy(
                src_ref=src, dst_ref=lands[0].at[p], send_sem=sem(0, p), recv_sem=sem(1, p),
                device_id=tuple(peer), device_id_type=MESH))
        return [cp.start for cp in cps], [cp.wait for cp in cps]

    return _Part([array], [jax.ShapeDtypeStruct((pieces, *array.shape[2:]), array.dtype)], {}, pieces, plan)


def _grid_edges(grid):
    first = last = None
    for ax, n in enumerate(grid):
        p = pl.program_id(ax)
        f, l = p == 0, p == n - 1
        first = f if first is None else jnp.logical_and(first, f)
        last = l if last is None else jnp.logical_and(last, l)
    return first, last


def _call(body, *, name, grid, in_specs, out_specs, out_shape, args, scratch_shapes=(), sem=None, carry=None):
    if carry is None:
        return pl.pallas_call(
            body, name=name, grid=grid, in_specs=list(in_specs), out_specs=list(out_specs),
            out_shape=list(out_shape), scratch_shapes=list(scratch_shapes), compiler_params=_cparams(sem),
        )(*args)
    n_in, n_out, n_scr = len(in_specs), len(out_specs), len(scratch_shapes)
    n_cin, n_cout = len(carry.operands), len(carry.landings)

    def wrapped(*refs):
        ins, cins = refs[:n_in], refs[n_in:n_in + n_cin]
        o0 = n_in + n_cin
        outs, couts = refs[o0:o0 + n_out], refs[o0 + n_out:o0 + n_out + n_cout]
        s0 = o0 + n_out + n_cout
        scr, sems = refs[s0:s0 + n_scr], refs[s0 + n_scr:]
        first, last = _grid_edges(grid)

        def plan():
            return carry.plan(cins, couts, lambda kind, k: sems[kind].at[k])

        def start_all():
            for start in plan()[0]:
                start()

        def wait_all():
            for wait in plan()[1]:
                wait()

        if grid:
            pl.when(first)(start_all)
            body(*ins, *outs, *scr)
            pl.when(last)(wait_all)
        else:
            start_all()
            body(*ins, *outs, *scr)
            wait_all()

    sem_arrays = [pltpu.SemaphoreType.DMA((carry.n_sems,))] * 3
    res = pl.pallas_call(
        wrapped, name=name, grid=grid,
        in_specs=[*in_specs, *[ANY] * n_cin], out_specs=[*out_specs, *[ANY] * n_cout],
        out_shape=[*out_shape, *carry.landings],
        scratch_shapes=[*scratch_shapes, *sem_arrays],
        input_output_aliases={n_in + i: n_out + j for i, j in carry.aliases.items()},
        compiler_params=_cparams(("arbitrary",) * len(grid) if grid else None, has_side_effects=True),
    )(*args, *carry.operands)
    return res[:n_out], res[n_out:]


MM_TILE = 1024
MM_K_TILE = 2048


def _matmul(name, a, b, dims, grid, a_spec, b_spec, out_shape, out_spec, epilogue,
            extras=(), extra_specs=(), prologue=None, carry=None):
    nk = grid[2]
    n_extra = len(extras)
    acc_shape = out_spec.block_shape[-2:]

    def lhs(a_ref):
        return a_ref[...] if prologue is None else prologue(a_ref[...])

    def body_one(a_ref, b_ref, *rest):
        epilogue(_dot(lhs(a_ref), b_ref[...], dims), rest[:n_extra], rest[n_extra:])

    def body_acc(a_ref, b_ref, *rest):
        acc = rest[-1]
        k = pl.program_id(2)
        part = _dot(lhs(a_ref), b_ref[...], dims)

        @pl.when(k == 0)
        def _():
            acc[...] = part

        @pl.when(k > 0)
        def _():
            acc[...] += part

        @pl.when(k == nk - 1)
        def _():
            epilogue(acc[...], rest[:n_extra], rest[n_extra:-1])

    res = _call(body_one if nk == 1 else body_acc, name=name, grid=grid,
                in_specs=[a_spec, b_spec, *extra_specs], out_specs=[out_spec], out_shape=[out_shape],
                args=(a, b, *extras), scratch_shapes=[] if nk == 1 else [pltpu.VMEM(acc_shape, F32)],
                sem=("parallel", "parallel", "arbitrary"), carry=carry)
    return res[0] if carry is None else (res[0][0], res[1])


def _store_as(acc, extra_refs, out_refs):
    out_refs[0][...] = acc.astype(out_refs[0].dtype)


def _square(u):
    return u * u


def mm_nn(name, a, b, out_dtype, tk=None, prologue=None, carry=None):
    (m, kk), n = a.shape, b.shape[1]
    tm, tn = _tile(m, MM_TILE), _tile(n, MM_TILE)
    tk = kk if tk is None else _tile(kk, tk)
    return _matmul(name, a, b, NN, (m // tm, n // tn, kk // tk),
                   pl.BlockSpec((tm, tk), lambda i, j, k: (i, k)),
                   pl.BlockSpec((tk, tn), lambda i, j, k: (k, j)),
                   jax.ShapeDtypeStruct((m, n), out_dtype),
                   pl.BlockSpec((tm, tn), lambda i, j, k: (i, j)), _store_as, prologue=prologue, carry=carry)


def mm_nt(name, a, b, out_dtype, epilogue=_store_as, extras=(), extra_specs=(), carry=None):
    (m, kk), n = a.shape, b.shape[0]
    tm, tn = _tile(m, MM_TILE), _tile(n, MM_TILE)
    return _matmul(name, a, b, NT, (m // tm, n // tn, 1),
                   pl.BlockSpec((tm, kk), lambda i, j, k: (i, 0)),
                   pl.BlockSpec((tn, kk), lambda i, j, k: (j, 0)),
                   jax.ShapeDtypeStruct((m, n), out_dtype),
                   pl.BlockSpec((tm, tn), lambda i, j, k: (i, j)), epilogue,
                   extras=extras, extra_specs=extra_specs, carry=carry)


def mm_tn(name, a, b, out_dtype, prologue=None, carry=None):
    (kk, m), n = a.shape, b.shape[1]
    tm, tn, tk = _tile(m, MM_TILE), _tile(n, MM_TILE), _tile(kk, MM_K_TILE)
    return _matmul(name, a, b, TN, (m // tm, n // tn, kk // tk),
                   pl.BlockSpec((tk, tm), lambda i, j, k: (k, i)),
                   pl.BlockSpec((tk, tn), lambda i, j, k: (k, j)),
                   jax.ShapeDtypeStruct((m, n), out_dtype),
                   pl.BlockSpec((tm, tn), lambda i, j, k: (i, j)), _store_as, prologue=prologue, carry=carry)


def up_proj(h2, wup_slabs):
    (m, kk), (_, _, ns) = h2.shape, wup_slabs.shape
    tm, tn = _tile(m, MM_TILE), _tile(ns, MM_TILE)
    r = ns // tn
    n = N_DEV * ns

    def epi(acc, extra_refs, out_refs):
        out_refs[0][...] = jnp.maximum(acc, 0.0).astype(BF16)

    return _matmul("up_proj", h2, wup_slabs, NN, (m // tm, n // tn, 1),
                   pl.BlockSpec((tm, kk), lambda i, j, k: (i, 0)),
                   pl.BlockSpec((None, kk, tn), lambda i, j, k: (j // r, 0, j % r)),
                   jax.ShapeDtypeStruct((m, n), BF16),
                   pl.BlockSpec((tm, tn), lambda i, j, k: (i, j)), epi)


def down_proj(u, wdown):
    return mm_nn("down_proj", u, wdown, F32, tk=MM_K_TILE, prologue=_square)


def down_bwd_act(dy, wdown, u):
    tm, tn = _tile(dy.shape[0], MM_TILE), _tile(wdown.shape[0], MM_TILE)

    def epi(acc, extra_refs, out_refs):
        out_refs[0][...] = (acc * (2.0 * extra_refs[0][...].astype(F32))).astype(BF16)

    return mm_nt("down_bwd_act", dy, wdown, BF16, epilogue=epi, extras=(u,),
                 extra_specs=(pl.BlockSpec((tm, tn), lambda i, j, k: (i, j)),))


def down_wgrad(u, dy):
    return mm_tn("down_wgrad", u, dy, BF16, prologue=_square)


def up_bwd_x(du, wup_slabs, carry=None):
    (m, kk), (_, n, ns) = du.shape, wup_slabs.shape
    tm, tk = _tile(m, MM_TILE), _tile(ns, MM_TILE)
    r = ns // tk
    return _matmul("up_bwd_x", du, wup_slabs, NT, (m // tm, 1, kk // tk),
                   pl.BlockSpec((tm, tk), lambda i, j, k: (i, k)),
                   pl.BlockSpec((None, n, tk), lambda i, j, k: (k // r, 0, k % r)),
                   jax.ShapeDtypeStruct((m, n), F32),
                   pl.BlockSpec((tm, n), lambda i, j, k: (i, 0)), _store_as, carry=carry)


def up_wgrad(h2, du, carry=None):
    (kk, m), n = h2.shape, du.shape[1]
    ns = n // N_DEV
    tm, tn, tk = _tile(m, MM_TILE), _tile(ns, MM_TILE), _tile(kk, MM_K_TILE)
    r = ns // tn
    return _matmul("up_wgrad", h2, du, TN, (m // tm, n // tn, kk // tk),
                   pl.BlockSpec((tk, tm), lambda i, j, k: (k, i)),
                   pl.BlockSpec((tk, tn), lambda i, j, k: (k, j)),
                   jax.ShapeDtypeStruct((N_DEV, m, ns), BF16),
                   pl.BlockSpec((None, tm, tn), lambda i, j, k: (j // r, i, j % r)), _store_as, carry=carry)


def _rstd(x):
    return lax.rsqrt(jnp.mean(x * x, axis=-1, keepdims=True) + EPS)


def _norm_bwd(x, g, dy):
    r = _rstd(x)
    xh = x * r
    dyg = dy * g
    dx = r * (dyg - xh * jnp.mean(dyg * xh, axis=-1, keepdims=True))
    return dx, jnp.sum(dy * xh, axis=0, keepdims=True)


def _row_spec(tr, d):
    return pl.BlockSpec((tr, d), lambda i: (i, 0))


def _vec_spec(d):
    return pl.BlockSpec((1, d), lambda i: (0, 0))


def _accum(ref, val):
    @pl.when(pl.program_id(0) == 0)
    def _():
        ref[...] = jnp.zeros_like(ref)

    ref[...] += val


def pre_norm(x, g, tr=256):
    t, d = x.shape
    tr = _pick(t, tr)

    def body(x_ref, g_ref, h_ref):
        xx = x_ref[...]
        h_ref[...] = (xx * _rstd(xx) * g_ref[...]).astype(BF16)

    return _call(body, name="pre_norm", grid=(t // tr,),
                 in_specs=[_row_spec(tr, d), _vec_spec(d)], out_specs=[_row_spec(tr, d)],
                 out_shape=[jax.ShapeDtypeStruct((t, d), BF16)], args=(x, g), sem=("parallel",))[0]


def mid_fwd(mixed, g_post, x, g_pre2, tr=256):
    t, d = x.shape
    tr = _pick(t, tr)

    def body(m_ref, gp_ref, x_ref, g2_ref, x1_ref, h2_ref):
        mm = m_ref[...]
        x1 = x_ref[...] + mm * _rstd(mm) * gp_ref[...]
        x1_ref[...] = x1
        h2_ref[...] = (x1 * _rstd(x1) * g2_ref[...]).astype(BF16)

    return _call(body, name="mid_fwd", grid=(t // tr,),
                 in_specs=[_row_spec(tr, d), _vec_spec(d), _row_spec(tr, d), _vec_spec(d)],
                 out_specs=[_row_spec(tr, d), _row_spec(tr, d)],
                 out_shape=[jax.ShapeDtypeStruct((t, d), F32), jax.ShapeDtypeStruct((t, d), BF16)],
                 args=(mixed, g_post, x, g_pre2), sem=("parallel",))


def loss_bwd(y, g_post2, x1, target, tr=256):
    t, d = y.shape
    tr = _pick(t, tr)

    def body(y_ref, g_ref, x1_ref, t_ref, sse_ref, dout_ref, dy_ref, dg_ref):
        yy = y_ref[...]
        g = g_ref[...]
        err = x1_ref[...] + yy * _rstd(yy) * g - t_ref[...]
        _accum(sse_ref, jnp.sum(jnp.sum(err * err, axis=1, keepdims=True), axis=0, keepdims=True))
        dout = err * (1.0 / d)
        dout_ref[...] = dout
        dy, dg = _norm_bwd(yy, g, dout)
        dy_ref[...] = dy.astype(BF16)
        _accum(dg_ref, dg)

    return _call(body, name="loss_bwd", grid=(t // tr,),
                 in_specs=[_row_spec(tr, d), _vec_spec(d), _row_spec(tr, d), _row_spec(tr, d)],
                 out_specs=[pl.BlockSpec((1, 1), lambda i: (0, 0)), _row_spec(tr, d), _row_spec(tr, d), _vec_spec(d)],
                 out_shape=[jax.ShapeDtypeStruct((1, 1), F32), jax.ShapeDtypeStruct((t, d), F32),
                            jax.ShapeDtypeStruct((t, d), BF16), jax.ShapeDtypeStruct((1, d), F32)],
                 args=(y, g_post2, x1, target), sem=("arbitrary",))


def mid_bwd(dh2, x1, g_pre2, dout, mixed, g_post, carry=None, tr=256):
    t, d = x1.shape
    tr = _pick(t, tr)

    def body(dh_ref, x1_ref, g2_ref, do_ref, m_ref, gp_ref, dx1_ref, dm_ref, dg2_ref, dgp_ref):
        d1, dg2 = _norm_bwd(x1_ref[...], g2_ref[...], dh_ref[...])
        dx1 = do_ref[...] + d1
        dx1_ref[...] = dx1
        dm, dgp = _norm_bwd(m_ref[...], gp_ref[...], dx1)
        dm_ref[...] = dm.astype(BF16)
        _accum(dg2_ref, dg2)
        _accum(dgp_ref, dgp)

    return _call(body, name="mid_bwd", grid=(t // tr,),
                 in_specs=[_row_spec(tr, d), _row_spec(tr, d), _vec_spec(d), _row_spec(tr, d), _row_spec(tr, d),
                           _vec_spec(d)],
                 out_specs=[_row_spec(tr, d), _row_spec(tr, d), _vec_spec(d), _vec_spec(d)],
                 out_shape=[jax.ShapeDtypeStruct((t, d), F32), jax.ShapeDtypeStruct((t, d), BF16),
                            jax.ShapeDtypeStruct((1, d), F32), jax.ShapeDtypeStruct((1, d), F32)],
                 args=(dh2, x1, g_pre2, dout, mixed, g_post), sem=("arbitrary",), carry=carry)


def first_bwd(dh1, x, g_pre, dx1, carry=None, tr=256):
    t, d = x.shape
    tr = _pick(t, tr)

    def body(dh_ref, x_ref, g_ref, dx1_ref, gx_ref, dg_ref):
        d0, dg = _norm_bwd(x_ref[...], g_ref[...], dh_ref[...])
        gx_ref[...] = dx1_ref[...] + d0
        _accum(dg_ref, dg)

    return _call(body, name="first_bwd", grid=(t // tr,),
                 in_specs=[_row_spec(tr, d), _row_spec(tr, d), _vec_spec(d), _row_spec(tr, d)],
                 out_specs=[_row_spec(tr, d), _vec_spec(d)],
                 out_shape=[jax.ShapeDtypeStruct((t, d), F32), jax.ShapeDtypeStruct((1, d), F32)],
                 args=(dh1, x, g_pre, dx1), sem=("arbitrary",), carry=carry)


def _attn_geometry():
    r = lax.broadcasted_iota(jnp.int32, (BLOCK, BLOCK), 0)
    c = lax.broadcasted_iota(jnp.int32, (BLOCK, BLOCK), 1)
    dist_cur = (r - c).astype(F32)
    return dist_cur, dist_cur + float(BLOCK), r >= c, c > r


def _attn_probs(qh, kp, kc, slope, sink, geo, has_prev):
    dist_cur, dist_prev, mask_cur, mask_prev = geo
    s_cur = _dot(qh, kc, NT) - slope * dist_cur
    s_prev = _dot(qh, kp, NT) - slope * dist_prev
    s_cur = jnp.where(mask_cur, s_cur, -jnp.inf)
    s_prev = jnp.where(jnp.logical_and(mask_prev, has_prev), s_prev, -jnp.inf)
    m = jnp.maximum(jnp.maximum(jnp.max(s_cur, axis=-1, keepdims=True),
                                jnp.max(s_prev, axis=-1, keepdims=True)), sink)
    p_cur = jnp.exp(s_cur - m)
    p_prev = jnp.exp(s_prev - m)
    p_sink = jnp.exp(sink - m)
    inv = 1.0 / (jnp.sum(p_cur, axis=-1, keepdims=True) + jnp.sum(p_prev, axis=-1, keepdims=True) + p_sink)
    return p_prev * inv, p_cur * inv, p_sink * inv


def attn_fwd(proj, sinks, gain, aw, carry=None):
    t = proj.shape[0]
    kw = N_KV_HEADS * HEAD_DIM
    n_heads = aw // HEAD_DIM
    group = n_heads // N_KV_HEADS
    nb = t // BLOCK
    scale = HEAD_DIM ** -0.5

    def body(sink_ref, q_ref, k_ref, v_ref, g_ref, o_ref, on_ref):
        n = pl.program_id(0)
        cur = pl.multiple_of(n * BLOCK, BLOCK)
        prev = pl.multiple_of(jnp.maximum(n - 1, 0) * BLOCK, BLOCK)
        has_prev = n > 0
        geo = _attn_geometry()
        kc, kp = k_ref[pl.ds(cur, BLOCK), :], k_ref[pl.ds(prev, BLOCK), :]
        vc, vp = v_ref[pl.ds(cur, BLOCK), :], v_ref[pl.ds(prev, BLOCK), :]
        heads = []
        for h in range(n_heads):
            kv = h // group
            ks = slice(kv * HEAD_DIM, (kv + 1) * HEAD_DIM)
            hs = slice(h * HEAD_DIM, (h + 1) * HEAD_DIM)
            slope = 2.0 ** (-8.0 * (h + 1) / n_heads)
            qh = q_ref[:, hs] * scale
            p_prev, p_cur, _ = _attn_probs(qh, kp[:, ks], kc[:, ks], slope, sink_ref[0, h], geo, has_prev)
            heads.append(_dot(p_prev, vp[:, ks], NN) + _dot(p_cur, vc[:, ks], NN))
        o = jnp.concatenate(heads, axis=1)
        o_ref[...] = o
        on_ref[...] = (o * _rstd(o) * g_ref[...]).astype(BF16)

    return _call(body, name="attn_fwd", grid=(nb,),
                 in_specs=[pl.BlockSpec(memory_space=pltpu.SMEM),
                           pl.BlockSpec((BLOCK, aw), lambda n: (n, 0)),
                           pl.BlockSpec((t, kw), lambda n: (0, aw // kw)),
                           pl.BlockSpec((t, kw), lambda n: (0, aw // kw + 1)),
                           pl.BlockSpec((1, aw), lambda n: (0, 0))],
                 out_specs=[pl.BlockSpec((BLOCK, aw), lambda n: (n, 0)), pl.BlockSpec((BLOCK, aw), lambda n: (n, 0))],
                 out_shape=[jax.ShapeDtypeStruct((t, aw), F32), jax.ShapeDtypeStruct((t, aw), BF16)],
                 args=(sinks, proj, proj, proj, gain), sem=("parallel",), carry=carry)


def attn_bwd(proj, sinks, gain, attn_o, dcat, aw, carry=None):
    t = proj.shape[0]
    kw = N_KV_HEADS * HEAD_DIM
    n_heads = aw // HEAD_DIM
    group = n_heads // N_KV_HEADS
    nb = t // BLOCK
    scale = HEAD_DIM ** -0.5

    def body(sink_ref, q_ref, k_ref, v_ref, g_ref, o_ref, dn_ref, dq_ref, dk_ref, dv_ref, dsink_ref, dg_ref, do_ref):
        n = pl.program_id(0)
        cur = pl.multiple_of(n * BLOCK, BLOCK)
        prev = pl.multiple_of(jnp.maximum(n - 1, 0) * BLOCK, BLOCK)
        has_prev = n > 0
        geo = _attn_geometry()

        @pl.when(n == 0)
        def _():
            dk_ref[...] = jnp.zeros_like(dk_ref)
            dv_ref[...] = jnp.zeros_like(dv_ref)
            dsink_ref[...] = jnp.zeros_like(dsink_ref)

        o = o_ref[...]
        do_all, dg = _norm_bwd(o, g_ref[...], dn_ref[...])
        _accum(dg_ref, dg)
        do_ref[...] = do_all
        kc, kp = k_ref[pl.ds(cur, BLOCK), :], k_ref[pl.ds(prev, BLOCK), :]
        vc, vp = v_ref[pl.ds(cur, BLOCK), :], v_ref[pl.ds(prev, BLOCK), :]
        lane = lax.broadcasted_iota(jnp.int32, (1, LANES), 1)
        dsink = jnp.zeros((1, LANES), F32)
        for kv in range(N_KV_HEADS):
            ks = slice(kv * HEAD_DIM, (kv + 1) * HEAD_DIM)
            dkc = jnp.zeros((BLOCK, HEAD_DIM), F32)
            dkp = jnp.zeros((BLOCK, HEAD_DIM), F32)
            dvc = jnp.zeros((BLOCK, HEAD_DIM), F32)
            dvp = jnp.zeros((BLOCK, HEAD_DIM), F32)
            for gidx in range(group):
                h = kv * group + gidx
                hs = slice(h * HEAD_DIM, (h + 1) * HEAD_DIM)
                slope = 2.0 ** (-8.0 * (h + 1) / n_heads)
                qh = q_ref[:, hs] * scale
                p_prev, p_cur, p_sink = _attn_probs(qh, kp[:, ks], kc[:, ks], slope, sink_ref[0, h], geo, has_prev)
                doh = do_ref[:, hs]
                delta = jnp.sum(doh * o_ref[:, hs], axis=-1, keepdims=True)
                ds_cur = p_cur * (_dot(doh, vc[:, ks], NT) - delta)
                ds_prev = p_prev * (_dot(doh, vp[:, ks], NT) - delta)
                dsink = dsink + jnp.where(lane == h, -jnp.sum(p_sink * delta, axis=0, keepdims=True), 0.0)
                dq_ref[:, hs] = ((_dot(ds_cur, kc[:, ks], NN) + _dot(ds_prev, kp[:, ks], NN)) * scale).astype(BF16)
                dkc = dkc + _dot(ds_cur, qh, TN)
                dkp = dkp + _dot(ds_prev, qh, TN)
                dvc = dvc + _dot(p_cur, doh, TN)
                dvp = dvp + _dot(p_prev, doh, TN)
            dk_ref[pl.ds(cur, BLOCK), ks] += dkc
            dv_ref[pl.ds(cur, BLOCK), ks] += dvc

            @pl.when(has_prev)
            def _():
                dk_ref[pl.ds(prev, BLOCK), ks] += dkp
                dv_ref[pl.ds(prev, BLOCK), ks] += dvp
        dsink_ref[...] += dsink

    return _call(body, name="attn_bwd", grid=(nb,),
                 in_specs=[pl.BlockSpec(memory_space=pltpu.SMEM),
                           pl.BlockSpec((BLOCK, aw), lambda n: (n, 0)),
                           pl.BlockSpec((t, kw), lambda n: (0, aw // kw)),
                           pl.BlockSpec((t, kw), lambda n: (0, aw // kw + 1)),
                           pl.BlockSpec((1, aw), lambda n: (0, 0)),
                           pl.BlockSpec((BLOCK, aw), lambda n: (n, 0)),
                           pl.BlockSpec((BLOCK, aw), lambda n: (n, 0))],
                 out_specs=[pl.BlockSpec((BLOCK, aw), lambda n: (n, 0)),
                            pl.BlockSpec((t, kw), lambda n: (0, 0)), pl.BlockSpec((t, kw), lambda n: (0, 0)),
                            pl.BlockSpec((1, LANES), lambda n: (0, 0)), pl.BlockSpec((1, aw), lambda n: (0, 0))],
                 out_shape=[jax.ShapeDtypeStruct((t, aw), BF16), jax.ShapeDtypeStruct((t, kw), F32),
                            jax.ShapeDtypeStruct((t, kw), F32), jax.ShapeDtypeStruct((1, LANES), F32),
                            jax.ShapeDtypeStruct((1, aw), F32)],
                 args=(sinks, proj, proj, proj, gain, attn_o, dcat),
                 scratch_shapes=[pltpu.VMEM((BLOCK, aw), F32)], sem=("arbitrary",), carry=carry)


def _sigmoid(x):
    return 1.0 / (1.0 + jnp.exp(-x))


def _chunk_geometry():
    row = lax.broadcasted_iota(jnp.int32, (CHUNK, CHUNK), 0)
    col = lax.broadcasted_iota(jnp.int32, (CHUNK, CHUNK), 1)
    return row, col


def _cumsum_rows(x, reverse=False):
    row, col = _chunk_geometry()
    tri = (col >= row) if reverse else (col <= row)
    return lax.dot_general(tri.astype(F32), x, ((NN), ((), ())), precision=HI, preferred_element_type=F32)


def _rep_sub(x4):
    k = x4.shape[-1]
    return jnp.broadcast_to(x4[:, None, :], (CHUNK // SUB, SUB, k)).reshape(CHUNK, k)


def _gates(q_r, f_r, lb):
    sg = _sigmoid(f_r)
    f = lb + (1.0 - lb) * sg
    sq = _sigmoid(q_r)
    return sg, f, sq, q_r * sq


def _offdiag_terms(b, j):
    c = b[j * SUB + SUB - 1:j * SUB + SUB, :]
    return jnp.exp(jnp.minimum(b - c, 0.0)), jnp.exp(jnp.minimum(c - b, 0.0))


def _store_heads(ref, x):
    for j in range(ref.shape[0]):
        ref[j] = x[:, _head(j)]


def _sub_rows(ref, r):
    rows = [ref[j, pl.ds(r, CHUNK // SUB, stride=SUB), :] for j in range(ref.shape[0])]
    return _rep_sub(jnp.concatenate(rows, axis=1))


def _diag_mask():
    row, col = _chunk_geometry()
    return jnp.logical_and((row // SUB) == (col // SUB), row >= col)


HGRN_HEADS_PER_STEP = 8


def _wide(refs):
    return jnp.concatenate([r[...] for r in refs], axis=1)


def _head(j):
    return slice(j * RNN_HEAD_DIM, (j + 1) * RNN_HEAD_DIM)


def _cat_heads(parts, hs):
    return jnp.concatenate([p[:, hs] for p in parts], axis=1)


def _offdiag_factors(q, k, b):
    rowi = lax.broadcasted_iota(jnp.int32, b.shape, 0)
    qs, ks, ers, ecs = [], [], [], []
    for j in range(CHUNK // SUB - 1):
        e_row, e_col = _offdiag_terms(b, j)
        e_row = jnp.where(rowi >= (j + 1) * SUB, e_row, 0.0)
        e_col = jnp.where((rowi // SUB) == j, e_col, 0.0)
        qs.append(q * e_row)
        ks.append(k * e_col)
        ers.append(e_row)
        ecs.append(e_col)
    return qs, ks, ers, ecs


def hgrn_fwd(proj, lb, norm_gain, col0, rw, carry=None):
    t = proj.shape[0]
    nh = rw // RNN_HEAD_DIM
    nc = t // CHUNK
    kd = RNN_HEAD_DIM
    cb = col0 // kd
    nsub = CHUNK // SUB

    hp = _pick(nh, HGRN_HEADS_PER_STEP)
    w = hp * kd

    def body(*refs):
        q_refs, f_refs, i_refs, g_refs = (refs[i * hp:(i + 1) * hp] for i in range(4))
        lb_ref, ng_ref, rnn_ref, o_ref, att_ref, st_ref, state, b_ref, k_ref = refs[4 * hp:]
        c = pl.program_id(1)

        @pl.when(c == 0)
        def _():
            state[...] = jnp.zeros_like(state)

        st_ref[...] = state[...]
        q_r, f_r, v, g_r = (_wide(rs) for rs in (q_refs, f_refs, i_refs, g_refs))
        _, f, _, q = _gates(q_r, f_r, lb_ref[...])
        k = 1.0 - f
        b = _cumsum_rows(jnp.log(f))
        _store_heads(b_ref, b)
        _store_heads(k_ref, k)
        qcat, kcat, _, _ = _offdiag_factors(q, k, b)
        row, col = _chunk_geometry()
        same = (row // SUB) == (col // SUB)
        rloc = lax.broadcasted_iota(jnp.int32, (CHUNK, w), 0) % SUB
        diag = [jnp.zeros((CHUNK, CHUNK), F32)] * hp
        for r in range(SUB):
            bs = _sub_rows(b_ref, r)
            ks = _sub_rows(k_ref, r)
            prod = q * jnp.exp(jnp.where(rloc >= r, b - bs, -jnp.inf)) * ks
            place = jnp.logical_and((col % SUB) == r, same)
            diag = [jnp.where(place, jnp.sum(prod[:, _head(j)], axis=-1, keepdims=True), diag[j]) for j in range(hp)]
        b_last = b[CHUNK - 1:CHUNK, :]
        qe = q * jnp.exp(b)
        kdec = k * jnp.exp(b_last - b)
        decay = jnp.exp(b_last)
        outs, normed, states = [], [], []
        for j in range(hp):
            hs = _head(j)
            att = diag[j] + _dot(_cat_heads(qcat, hs), _cat_heads(kcat, hs), NT)
            att_ref[j] = att
            sj = state[j]
            o = _dot(qe[:, hs], sj, NT) + _dot(att, v[:, hs], NN)
            outs.append(o)
            normed.append(o * _rstd(o))
            states.append(sj * decay[:, hs] + _dot(v[:, hs], kdec[:, hs], TN))
        for j in range(hp):
            state[j] = states[j]
        o_ref[...] = jnp.concatenate(outs, axis=1)
        gate = g_r * _sigmoid(g_r)
        rnn_ref[...] = (jnp.concatenate(normed, axis=1) * jnp.tile(ng_ref[...], (1, hp)) * gate).astype(BF16)

    def col(kidx, j):
        return pl.BlockSpec((CHUNK, kd), lambda hg, c: (c, cb + kidx * nh + hg * hp + j))

    return _call(body, name="hgrn_fwd", grid=(nh // hp, nc),
                 in_specs=[col(kidx, j) for kidx in range(4) for j in range(hp)] +
                          [pl.BlockSpec((1, w), lambda hg, c: (0, hg)), pl.BlockSpec((1, kd), lambda hg, c: (0, 0))],
                 out_specs=[pl.BlockSpec((CHUNK, w), lambda hg, c: (c, hg)),
                            pl.BlockSpec((CHUNK, w), lambda hg, c: (c, hg)),
                            pl.BlockSpec((hp, CHUNK, CHUNK), lambda hg, c: (hg, c, 0)),
                            pl.BlockSpec((None, hp, kd, kd), lambda hg, c: (c, hg, 0, 0))],
                 out_shape=[jax.ShapeDtypeStruct((t, rw), BF16), jax.ShapeDtypeStruct((t, rw), F32),
                            jax.ShapeDtypeStruct((nh, t, CHUNK), F32), jax.ShapeDtypeStruct((nc, nh, kd, kd), F32)],
                 args=(*([proj] * (4 * hp)), lb, norm_gain),
                 scratch_shapes=[pltpu.VMEM((hp, kd, kd), F32), pltpu.VMEM((hp, CHUNK, kd), F32),
                                 pltpu.VMEM((hp, CHUNK, kd), F32)],
                 sem=("parallel", "arbitrary"), carry=carry)


def hgrn_bwd(proj, lb, norm_gain, o_all, att_all, st_all, dcat, col0, rw, carry=None):
    t = proj.shape[0]
    nh = rw // RNN_HEAD_DIM
    nc = t // CHUNK
    kd = RNN_HEAD_DIM
    cb = col0 // kd
    nsub = CHUNK // SUB
    dcb = (dcat.shape[1] - rw) // kd

    hp = _pick(nh, HGRN_HEADS_PER_STEP)
    w = hp * kd

    def per_head(x, fn):
        return jnp.concatenate([jnp.broadcast_to(fn(x[:, _head(j)]), (CHUNK, kd)) for j in range(hp)], axis=1)

    def body(*refs):
        q_refs, f_refs, i_refs, g_refs = (refs[i * hp:(i + 1) * hp] for i in range(4))
        (lb_ref, ng_ref, o_ref, att_ref, st0_ref, st1_ref, d_ref, dq_ref, df_ref, di_ref, dg_ref, dlb_ref, dng_ref,
         dstate, b_ref, k_ref, dks_ref) = refs[4 * hp:]
        ci = pl.program_id(1)

        @pl.when(ci == 0)
        def _():
            dstate[...] = jnp.zeros_like(dstate)
            dlb_ref[...] = jnp.zeros_like(dlb_ref)
            dng_ref[...] = jnp.zeros_like(dng_ref)

        lbv = lb_ref[...]
        q_r, f_r, v, g_r = (_wide(rs) for rs in (q_refs, f_refs, i_refs, g_refs))
        sg, f, sq, q = _gates(q_r, f_r, lbv)
        k = 1.0 - f
        b = _cumsum_rows(jnp.log(f))
        _store_heads(b_ref, b)
        _store_heads(k_ref, k)
        row, col = _chunk_geometry()

        o = o_ref[...]
        ng = jnp.tile(ng_ref[...], (1, hp))
        sgg = _sigmoid(g_r)
        gate = g_r * sgg
        d_rnn = d_ref[...]
        r = per_head(o, _rstd)
        oh = o * r
        dg_ref[...] = (d_rnn * oh * ng * (sgg * (1.0 + g_r * (1.0 - sgg)))).astype(BF16)
        d_on = d_rnn * gate
        dng_rows = jnp.sum(d_on * oh, axis=0, keepdims=True)
        dng = dng_rows[:, _head(0)]
        for j in range(1, hp):
            dng = dng + dng_rows[:, _head(j)]
        dng_ref[...] += dng
        dyg = d_on * ng
        do = r * (dyg - oh * per_head(dyg * oh, lambda x: jnp.mean(x, axis=-1, keepdims=True)))

        b_last = b[CHUNK - 1:CHUNK, :]
        eb = jnp.exp(b)
        tail = jnp.exp(b_last - b)
        kdec = k * tail
        decay = jnp.exp(b_last)
        qe = q * eb
        qcat, kcat, ers, ecs = _offdiag_factors(q, k, b)
        diag_mask = _diag_mask()
        dqs, dks, dvs, dads, gsums, dstates = [], [], [], [], [], []
        for j in range(hp):
            hs = _head(j)
            do_h, v_h, dst = do[:, hs], v[:, hs], dstate[j]
            da = jnp.where(row >= col, _dot(do_h, v_h, NT), 0.0)
            dads.append(jnp.where(diag_mask, da, 0.0))
            dq = _dot(do_h, st0_ref[j], NN) * eb[:, hs]
            dk = _dot(v_h, dst, NN) * tail[:, hs]
            dvs.append(_dot(att_ref[j], do_h, TN) + _dot(kdec[:, hs], dst, NT))
            rq = _dot(da, _cat_heads(kcat, hs), NN)
            rk = _dot(da, _cat_heads(qcat, hs), TN)
            for jj in range(nsub - 1):
                dq = dq + ers[jj][:, hs] * rq[:, _head(jj)]
                dk = dk + ecs[jj][:, hs] * rk[:, _head(jj)]
            dqs.append(dq)
            dks.append(dk)
            gsums.append(jnp.sum(dst * st1_ref[j], axis=0, keepdims=True))
            dstates.append(dst * decay[:, hs] + _dot(do_h, qe[:, hs], TN))
        for j in range(hp):
            dstate[j] = dstates[j]
        dq = jnp.concatenate(dqs, axis=1)
        dk = jnp.concatenate(dks, axis=1)
        rloc = lax.broadcasted_iota(jnp.int32, (CHUNK, w), 0) % SUB
        for rr in range(SUB):
            bs = _sub_rows(b_ref, rr)
            ks = _sub_rows(k_ref, rr)
            e = jnp.exp(jnp.where(rloc >= rr, b - bs, -jnp.inf))
            pick = (col % SUB) == rr
            dacol = jnp.concatenate(
                [jnp.broadcast_to(jnp.sum(jnp.where(pick, dads[j], 0.0), axis=-1, keepdims=True), (CHUNK, kd))
                 for j in range(hp)], axis=1)
            wv = dacol * e
            dq = dq + wv * ks
            sums = jnp.sum((wv * q).reshape(nsub, SUB, w), axis=1)
            for j in range(hp):
                dks_ref[j, pl.ds(rr, nsub, stride=SUB), :] = sums[:, _head(j)]
        dk = dk + jnp.concatenate([dks_ref[j] for j in range(hp)], axis=1)

        dlf = _cumsum_rows(q * dq - k * dk, reverse=True) + jnp.concatenate(gsums, axis=1)
        dfv = dlf / f - dk
        df_ref[...] = (dfv * (1.0 - lbv) * sg * (1.0 - sg)).astype(BF16)
        dlb_ref[...] += jnp.sum(dfv * (1.0 - sg), axis=0, keepdims=True)
        dq_ref[...] = (dq * (sq * (1.0 + q_r * (1.0 - sq)))).astype(BF16)
        di_ref[...] = jnp.concatenate(dvs, axis=1).astype(BF16)

    def rev(c):
        return nc - 1 - c

    def col_in(kidx, j):
        return pl.BlockSpec((CHUNK, kd), lambda hg, c: (rev(c), cb + kidx * nh + hg * hp + j))

    tile = pl.BlockSpec((CHUNK, w), lambda hg, c: (rev(c), hg))
    return _call(body, name="hgrn_bwd", grid=(nh // hp, nc),
                 in_specs=[col_in(kidx, j) for kidx in range(4) for j in range(hp)] +
                          [pl.BlockSpec((1, w), lambda hg, c: (0, hg)), pl.BlockSpec((1, kd), lambda hg, c: (0, 0)),
                           tile,
                           pl.BlockSpec((hp, CHUNK, CHUNK), lambda hg, c: (hg, rev(c), 0)),
                           pl.BlockSpec((None, hp, kd, kd), lambda hg, c: (rev(c), hg, 0, 0)),
                           pl.BlockSpec((None, hp, kd, kd),
                                        lambda hg, c: (jnp.minimum(rev(c) + 1, nc - 1), hg, 0, 0)),
                           pl.BlockSpec((CHUNK, w), lambda hg, c: (rev(c), dcb // hp + hg))],
                 out_specs=[tile, tile, tile, tile,
                            pl.BlockSpec((1, w), lambda hg, c: (0, hg)),
                            pl.BlockSpec((None, 1, kd), lambda hg, c: (hg, 0, 0))],
                 out_shape=[jax.ShapeDtypeStruct((t, rw), BF16)] * 4 + [jax.ShapeDtypeStruct((1, rw), F32),
                                                                        jax.ShapeDtypeStruct((nh // hp, 1, kd), F32)],
                 args=(*([proj] * (4 * hp)), lb, norm_gain, o_all, att_all, st_all, st_all, dcat),
                 scratch_shapes=[pltpu.VMEM((hp, kd, kd), F32), pltpu.VMEM((hp, CHUNK, kd), F32),
                                 pltpu.VMEM((hp, CHUNK, kd), F32), pltpu.VMEM((hp, CHUNK, kd), F32)],
                 sem=("parallel", "arbitrary"), carry=carry)


def all_gather_slabs(shards):
    n = len(shards)

    def body(*refs):
        ins, outs = refs[:n], refs[n:2 * n]
        send_sems, recv_sems, local_sems = refs[2 * n:]
        x, y, c = _coords()
        me, sibling = (x, y, c), (x, y, 1 - c)
        chips = [(1 - x, y), (x, 1 - y), (1 - x, 1 - y)]

        def copy(a, k, block, to, src=None):
            slab = outs[a].at[_slab_index(block)]
            return pltpu.make_async_remote_copy(
                src_ref=slab if src is None else src, dst_ref=slab,
                send_sem=send_sems.at[a, k], recv_sem=recv_sems.at[a, k],
                device_id=to, device_id_type=MESH)

        mine = [pltpu.make_async_copy(ins[a], outs[a].at[_slab_index(me)], local_sems.at[a]) for a in range(n)]
        for cp in mine:
            cp.start()
        first = []
        for a in range(n):
            first.append(copy(a, 0, me, sibling, src=ins[a]))
            first += [copy(a, 1 + j, me, (*chip, c), src=ins[a]) for j, chip in enumerate(chips)]
        for cp in first:
            cp.start()
        passed = []
        for j, chip in enumerate(chips):
            for a in range(n):
                copy(a, 1 + j, (*chip, c), me).wait_recv()
                fwd = copy(a, 4 + j, (*chip, c), sibling)
                fwd.start()
                passed.append(fwd)
        for a in range(n):
            copy(a, 0, sibling, me).wait_recv()
            for j, chip in enumerate(chips):
                copy(a, 4 + j, (*chip, 1 - c), me).wait_recv()
        for cp in first + passed:
            cp.wait_send()
        for cp in mine:
            cp.wait()

    return pl.pallas_call(
        body, name="all_gather_weights",
        in_specs=[ANY] * n, out_specs=[ANY] * n,
        out_shape=[jax.ShapeDtypeStruct((N_DEV, *s.shape), s.dtype) for s in shards],
        scratch_shapes=[pltpu.SemaphoreType.DMA((n, 7)), pltpu.SemaphoreType.DMA((n, 7)),
                        pltpu.SemaphoreType.DMA((n,))],
        compiler_params=pltpu.CompilerParams(has_side_effects=True),
    )(*shards)


def exchange_halves(name, array, axis):
    return _call(lambda: None, name=name, grid=(), in_specs=[], out_specs=[], out_shape=[], args=(),
                 carry=_scatter_step(array, axis))[1][0]


def add_kept_half(name, kept, got, sel, axis, tr=256):
    minor = axis == "c"
    pieces, rows, cols = got.shape
    tr = _tile(rows, tr, mult=16)

    def body(sel_ref, k_ref, g_ref, o_ref):
        o_ref[...] = (k_ref[...].astype(F32) + g_ref[...].astype(F32)).astype(o_ref.dtype)

    kept_spec = (pl.BlockSpec((None, None, tr, cols), lambda p, i, s: (p, s[0], i, 0)) if minor else
                 pl.BlockSpec((None, None, tr, cols), lambda p, i, s: (s[0], p, i, 0)))
    return pl.pallas_call(
        body, name=name,
        grid_spec=pltpu.PrefetchScalarGridSpec(
            num_scalar_prefetch=1, grid=(pieces, rows // tr),
            in_specs=[kept_spec, pl.BlockSpec((None, tr, cols), lambda p, i, s: (p, i, 0))],
            out_specs=pl.BlockSpec((None, tr, cols), lambda p, i, s: (p, i, 0))),
        out_shape=jax.ShapeDtypeStruct(got.shape, got.dtype),
        compiler_params=_cparams(("parallel", "parallel")),
    )(sel, kept, got)


def _adamw(w, g, m, v):
    m = ADAM_B1 * m + (1.0 - ADAM_B1) * g
    v = ADAM_B2 * v + (1.0 - ADAM_B2) * (g * g)
    m_hat = m / (1.0 - ADAM_B1 ** ADAM_STEP)
    v_hat = v / (1.0 - ADAM_B2 ** ADAM_STEP)
    delta = -ADAM_LR * (m_hat / (jnp.sqrt(v_hat) + ADAM_EPS) + ADAM_WD * w)
    return delta, m, v


def add_adamw(name, kept, got, sel, w, m, v, tr=128):
    rows, cols = w.shape
    tr = _tile(rows, tr, mult=16)

    def body(sel_ref, k_ref, g_ref, w_ref, m_ref, v_ref, go_ref, d_ref, mo_ref, vo_ref):
        g = k_ref[...].astype(F32) + g_ref[...].astype(F32)
        go_ref[...] = g
        d_ref[...], mo_ref[...], vo_ref[...] = _adamw(w_ref[...], g, m_ref[...], v_ref[...])

    tile = pl.BlockSpec((tr, cols), lambda i, s: (i, 0))
    return pl.pallas_call(
        body, name=name,
        grid_spec=pltpu.PrefetchScalarGridSpec(
            num_scalar_prefetch=1, grid=(rows // tr,),
            in_specs=[pl.BlockSpec((None, None, tr, cols), lambda i, s: (s[0], 0, i, 0)),
                      pl.BlockSpec((None, tr, cols), lambda i, s: (0, i, 0)), tile, tile, tile],
            out_specs=[tile] * 4),
        out_shape=[jax.ShapeDtypeStruct((rows, cols), F32)] * 4,
        compiler_params=_cparams(("parallel",)),
    )(sel, kept, got, w, m, v)


def small_allreduce_adamw(partial, scale, w, m, v):
    rows = partial.shape[0]

    def body(p_ref, s_ref, w_ref, m_ref, v_ref, g_ref, d_ref, mo_ref, vo_ref, slots, send_sems, recv_sems):
        x, y, c = _coords()
        my_slot = _slab_index((x, y, c))
        slots[my_slot] = p_ref[...]
        copies = []
        for mask in range(1, N_DEV):
            to = tuple(1 - v_ if (mask >> s_) & 1 else v_ for v_, s_ in ((x, 2), (y, 1), (c, 0)))
            copies.append(pltpu.make_async_remote_copy(
                src_ref=p_ref, dst_ref=slots.at[my_slot],
                send_sem=send_sems.at[mask - 1], recv_sem=recv_sems.at[mask - 1],
                device_id=to, device_id_type=MESH))
        for cp in copies:
            cp.start()
        for cp in copies:
            cp.wait()
        total = slots[0]
        for b in range(1, N_DEV):
            total = total + slots[b]
        g = total * s_ref[...]
        g_ref[...] = g
        d_ref[...], mo_ref[...], vo_ref[...] = _adamw(w_ref[...], g, m_ref[...], v_ref[...])

    vm = pl.BlockSpec(memory_space=pltpu.VMEM)
    return pl.pallas_call(
        body, name="small_allreduce_adamw",
        in_specs=[vm] * 5, out_specs=[vm] * 4,
        out_shape=[jax.ShapeDtypeStruct((rows, LANES), F32)] * 4,
        scratch_shapes=[pltpu.VMEM((N_DEV, rows, LANES), F32),
                        pltpu.SemaphoreType.DMA((N_DEV - 1,)), pltpu.SemaphoreType.DMA((N_DEV - 1,))],
        compiler_params=pltpu.CompilerParams(has_side_effects=True),
    )(partial, scale, w, m, v)


_SMALL = ("attn_sinks", "attn_out_gain", "rnn_lb_logits", "rnn_norm_gain", "mix_pre_gain", "mix_post_gain",
          "mlp_pre_gain", "mlp_post_gain")


def _pack(parts):
    rows = []
    for p in parts:
        flat = p.reshape(-1).astype(F32)
        pad = (-flat.shape[0]) % LANES
        rows.append(jnp.pad(flat, (0, pad)).reshape(-1, LANES))
    packed = jnp.concatenate(rows, axis=0)
    pad_rows = (-packed.shape[0]) % 8
    return jnp.pad(packed, ((0, pad_rows), (0, 0)))


def _unpack(packed, shapes):
    out, r = [], 0
    for s in shapes:
        size = math.prod(s)
        nrows = -(-size // LANES)
        out.append(packed[r:r + nrows].reshape(-1)[:size].reshape(s))
        r += nrows
    return out


class _Scatter:
    def __init__(self, tag, grad, sels):
        self.tag, self.sels = tag, sels
        self.shape = grad.shape[1:]
        self.cur = grad.reshape(4, 2, *self.shape)
        self.stage = 0

    def step(self):
        return _scatter_step(self.cur, "cxy"[self.stage])

    def land(self, got, w=None, m=None, v=None):
        axis = "cxy"[self.stage]
        name = "rs_add_%s_%s" % (axis, self.tag)
        sel = self.sels[axis]
        self.stage += 1
        if axis == "y":
            return add_adamw(name, self.cur, got, sel, w, m, v)
        summed = add_kept_half(name, self.cur, got, sel, axis)
        self.cur = summed.reshape(2, summed.shape[0] // 2, *self.shape)
        return None


def kernel(x, w_in, attn_sinks, attn_out_gain, rnn_lb_logits, rnn_norm_gain, w_out, mix_pre_gain, mix_post_gain, mlp_pre_gain, mlp_post_gain, w_up, w_down, loss_target, m_w_in, m_attn_sinks, m_attn_out_gain, m_rnn_lb_logits, m_rnn_norm_gain, m_w_out, m_mix_pre_gain, m_mix_post_gain, m_mlp_pre_gain, m_mlp_post_gain, m_w_up, m_w_down, v_w_in, v_attn_sinks, v_attn_out_gain, v_rnn_lb_logits, v_rnn_norm_gain, v_w_out, v_mix_pre_gain, v_mix_post_gain, v_mlp_pre_gain, v_mlp_post_gain, v_w_up, v_w_down):
    xs, target = x[0], loss_target[0]
    t, d = xs.shape
    aw = d // 2
    rw = d - aw
    col0 = aw + 2 * N_KV_HEADS * HEAD_DIM
    small_w = dict(attn_sinks=attn_sinks, attn_out_gain=attn_out_gain, rnn_lb_logits=rnn_lb_logits,
                   rnn_norm_gain=rnn_norm_gain, mix_pre_gain=mix_pre_gain, mix_post_gain=mix_post_gain,
                   mlp_pre_gain=mlp_pre_gain, mlp_post_gain=mlp_post_gain)
    small_m = dict(attn_sinks=m_attn_sinks, attn_out_gain=m_attn_out_gain, rnn_lb_logits=m_rnn_lb_logits,
                   rnn_norm_gain=m_rnn_norm_gain, mix_pre_gain=m_mix_pre_gain, mix_post_gain=m_mix_post_gain,
                   mlp_pre_gain=m_mlp_pre_gain, mlp_post_gain=m_mlp_post_gain)
    small_v = dict(attn_sinks=v_attn_sinks, attn_out_gain=v_attn_out_gain, rnn_lb_logits=v_rnn_lb_logits,
                   rnn_norm_gain=v_rnn_norm_gain, mix_pre_gain=v_mix_pre_gain, mix_post_gain=v_mix_post_gain,
                   mlp_pre_gain=v_mlp_pre_gain, mlp_post_gain=v_mlp_post_gain)
    cx, cy, cc = _coords()
    sels = {a: jnp.reshape(v_, (1,)).astype(jnp.int32) for a, v_ in (("x", cx), ("y", cy), ("c", cc))}

    w_in_t, m_in_t, v_in_t = w_in[0].T, m_w_in[0].T, v_w_in[0].T
    s_in, s_out, s_up, s_down = (w.astype(BF16) for w in (w_in_t, w_out[0], w_up[0], w_down[0]))
    probs = jax.nn.softmax(rnn_lb_logits.astype(F32), axis=0)
    lb = probs[0:1]

    wint = all_gather_slabs([s_in])[0].reshape(-1, d)
    h1 = pre_norm(xs, mix_pre_gain)
    proj, (wout_half,) = mm_nt("in_proj", h1, wint, F32, carry=_gather_first(s_out))
    (attn_o, attn_n), (wout, wup_half) = attn_fwd(
        proj, attn_sinks, attn_out_gain, aw, carry=_merge(_gather_second(wout_half), _gather_first(s_up)))
    (rnn, o_r, att, st), (wup, wdown_half) = hgrn_fwd(
        proj, lb, rnn_norm_gain, col0, rw, carry=_merge(_gather_second(wup_half), _gather_first(s_down)))
    wout = wout.reshape(-1, d)
    cat = jnp.concatenate([attn_n, rnn], axis=1)
    mixed, (wdown,) = mm_nn("out_proj", cat, wout, F32, carry=_gather_second(wdown_half))
    wdown = wdown.reshape(-1, d)
    x1, h2 = mid_fwd(mixed, mix_post_gain, xs, mlp_pre_gain)
    u = up_proj(h2, wup)
    y = down_proj(u, wdown)
    sse, dout, dy, dg_mlppost = loss_bwd(y, mlp_post_gain, x1, target)

    du = down_bwd_act(dy, wdown, u)
    rs_down = _Scatter("down", down_wgrad(u, dy).reshape(N_DEV, -1, d), sels)
    dh2, (got,) = up_bwd_x(du, wup, carry=rs_down.step())
    rs_down.land(got)
    dwup, (got,) = up_wgrad(h2, du, carry=rs_down.step())
    rs_down.land(got)
    rs_up = _Scatter("up", dwup, sels)
    (dx1, dmixed, dg_mlppre, dg_mixpost), (got,) = mid_bwd(dh2, x1, mlp_pre_gain, dout, mixed, mix_post_gain,
                                                          carry=rs_up.step())
    rs_up.land(got)
    dcat = mm_nt("out_bwd_x", dmixed, wout, F32)
    rs_out = _Scatter("out", mm_tn("out_wgrad", cat, dmixed, BF16).reshape(N_DEV, -1, d), sels)
    (dq_r, df_r, di_r, dg_r, dlb, dng), (got_d, got_u, got_o) = hgrn_bwd(
        proj, lb, rnn_norm_gain, o_r, att, st, dcat, col0, rw,
        carry=_merge(rs_down.step(), rs_up.step(), rs_out.step()))
    out_down = rs_down.land(got_d, w_down[0], m_w_down[0], v_w_down[0])
    rs_up.land(got_u)
    rs_out.land(got_o)
    (dq_a, dk_a, dv_a, dsinks, daog), (got_u, got_o) = attn_bwd(
        proj, attn_sinks, attn_out_gain, attn_o, dcat, aw, carry=_merge(rs_up.step(), rs_out.step()))
    out_up = rs_up.land(got_u, w_up[0], m_w_up[0], v_w_up[0])
    rs_out.land(got_o)
    dproj = jnp.concatenate([dq_a, dk_a.astype(BF16), dv_a.astype(BF16), dq_r, df_r, di_r, dg_r], axis=1)
    dwin, (got,) = mm_tn("in_wgrad", dproj, h1, BF16, carry=rs_out.step())
    out_out = rs_out.land(got, w_out[0], m_w_out[0], v_w_out[0])
    rs_in = _Scatter("in", dwin.reshape(N_DEV, -1, d), sels)
    dh1, (got,) = mm_nn("in_bwd_x", dproj, wint, F32, tk=MM_K_TILE, carry=rs_in.step())
    rs_in.land(got)
    (grad_x, dg_mixpre), (got,) = first_bwd(dh1, xs, mix_pre_gain, dx1, carry=rs_in.step())
    rs_in.land(got)
    out_in = rs_in.land(exchange_halves("rs_exchange_y_in", rs_in.cur, "y"), w_in_t, m_in_t, v_in_t)
    big_out = [out_in, out_out, out_up, out_down]

    n_heads = attn_sinks.shape[1]
    jac = probs[0] * probs[1]
    partial = _pack([sse, dsinks[0, :n_heads], daog, jnp.stack([dlb[0], dlb[0]]), jnp.sum(dng, axis=0),
                     dg_mixpre, dg_mixpost, dg_mlppre, dg_mlppost])
    ones = [jnp.ones(small_w[k].shape, F32) for k in _SMALL]
    ones[2] = jnp.stack([jac, -jac])
    scale = _pack([jnp.full((1,), 0.5 / d, F32)] + ones)
    zero = jnp.zeros((1,), F32)
    outs = small_allreduce_adamw(partial, scale, _pack([zero] + [small_w[k] for k in _SMALL]),
                                 _pack([zero] + [small_m[k] for k in _SMALL]),
                                 _pack([jnp.ones((1,), F32)] + [small_v[k] for k in _SMALL]))
    shapes = [(1,)] + [small_w[k].shape for k in _SMALL]
    sgrad, sdelta, snm, snv = (_unpack(o, shapes) for o in outs)
    loss = sgrad[0][0]

    def big(i, j):
        o = big_out[i][j]
        return (o.T if i == 0 else o)[None]

    def ordered(j, smalls):
        s = dict(zip(_SMALL, smalls[1:]))
        return [big(0, j), s["attn_sinks"], s["attn_out_gain"], s["rnn_lb_logits"], s["rnn_norm_gain"], big(1, j),
                s["mix_pre_gain"], s["mix_post_gain"], s["mlp_pre_gain"], s["mlp_post_gain"], big(2, j), big(3, j)]

    return (loss, grad_x[None], *ordered(0, sgrad), *ordered(1, sdelta), *ordered(2, snm), *ordered(3, snv))
```

```python
import math

import jax
import jax.numpy as jnp
from jax import lax
from jax.experimental import pallas as pl
from jax.experimental.pallas import tpu as pltpu

F32 = jnp.float32
BF16 = jnp.bfloat16

HEAD_DIM = 64
N_KV_HEADS = 2
BLOCK = 128
RNN_HEAD_DIM = 128
CHUNK = 64
SUB = 16
EPS = 1e-6

ADAM_LR = 0.001
ADAM_B1 = 0.9
ADAM_B2 = 0.999
ADAM_EPS = 1e-08
ADAM_WD = 0.01
ADAM_STEP = 10

N_DEV = 8
LANES = 128
V7X_VMEM_LIMIT = 56 * 1024 * 1024
MESH = pl.DeviceIdType.MESH
HI = lax.Precision.HIGHEST
ANY = pl.BlockSpec(memory_space=pl.ANY)
_AXES = ("x", "y", "c")


def _cparams(sem=None, **kw):
    return pltpu.CompilerParams(dimension_semantics=sem, vmem_limit_bytes=V7X_VMEM_LIMIT, **kw)


def _dot(a, b, dims):
    return lax.dot_general(a.astype(BF16), b.astype(BF16), (dims, ((), ())), preferred_element_type=F32)


NN = ((1,), (0,))
NT = ((1,), (1,))
TN = ((0,), (0,))


def _pick(n, pref):
    t = min(n, pref)
    while n % t:
        t //= 2
    return t


def _tile(n, pref, mult=LANES):
    if n <= pref:
        return n
    t = pref - pref % mult
    while n % t:
        t -= mult
    return t


def _coords():
    return lax.axis_index("x"), lax.axis_index("y"), lax.axis_index("c")


def _slab_index(dev):
    return 4 * dev[0] + 2 * dev[1] + dev[2]


class _Part:
    def __init__(self, operands, landings, aliases, n_sems, plan):
        self.operands, self.landings, self.aliases, self.n_sems, self.plan = operands, landings, aliases, n_sems, plan


def _merge(*parts):
    operands, landings, aliases, plans = [], [], {}, []
    s0 = 0
    for p in parts:
        o0, l0 = len(operands), len(landings)
        aliases.update({o0 + i: l0 + j for i, j in p.aliases.items()})
        plans.append((p.plan, o0, len(p.operands), l0, len(p.landings), s0))
        operands += p.operands
        landings += p.landings
        s0 += p.n_sems

    def plan(ops, lands, sem):
        starts, waits = [], []
        for f, o0, no, l0, nl, off in plans:
            s, w = f(ops[o0:o0 + no], lands[l0:l0 + nl], lambda kind, k, off=off: sem(kind, off + k))
            starts += s
            waits += w
        return starts, waits

    return _Part(operands, landings, aliases, s0, plan)


def _gather_peers(x, y, c):
    return [(x, y, 1 - c), (1 - x, y, c), (x, 1 - y, c), (1 - x, 1 - y, c)]


def _gather_first(shard):
    def plan(ops, lands, sem):
        x, y, c = _coords()
        me, peers = (x, y, c), _gather_peers(x, y, c)

        def cp(k, block, to):
            return pltpu.make_async_remote_copy(
                src_ref=ops[0], dst_ref=lands[0].at[_slab_index(block)],
                send_sem=sem(0, k), recv_sem=sem(1, k), device_id=to, device_id_type=MESH)

        local = pltpu.make_async_copy(ops[0], lands[0].at[_slab_index(me)], sem(2, 0))
        sends = [cp(k, me, to) for k, to in enumerate(peers)]
        recvs = [cp(k, frm, me) for k, frm in enumerate(peers)]
        return ([local.start] + [s.start for s in sends],
                [local.wait] + [s.wait_send for s in sends] + [r.wait_recv for r in recvs])

    return _Part([shard], [jax.ShapeDtypeStruct((N_DEV, *shard.shape), shard.dtype)], {}, 4, plan)


def _gather_second(gathered):
    def plan(ops, lands, sem):
        x, y, c = _coords()
        sibling = (x, y, 1 - c)
        chips = [(1 - x, y), (x, 1 - y), (1 - x, 1 - y)]

        def cp(k, block):
            slab = lands[0].at[_slab_index(block)]
            return pltpu.make_async_remote_copy(
                src_ref=slab, dst_ref=slab, send_sem=sem(0, k), recv_sem=sem(1, k),
                device_id=sibling, device_id_type=MESH)

        sends = [cp(k, (*chip, c)) for k, chip in enumerate(chips)]
        recvs = [cp(k, (*chip, 1 - c)) for k, chip in enumerate(chips)]
        return [s.start for s in sends], [s.wait_send for s in sends] + [r.wait_recv for r in recvs]

    return _Part([gathered], [jax.ShapeDtypeStruct(gathered.shape, gathered.dtype)], {0: 0}, 3, plan)


def _scatter_step(array, axis):
    minor = axis == "c"
    pieces = array.shape[0] if minor else array.shape[1]

    def plan(ops, lands, sem):
        coords = list(_coords())
        ai = _AXES.index(axis)
        mine = coords[ai]
        peer = list(coords)
        peer[ai] = 1 - mine
        cps = []
        for p in range(pieces):
            src = ops[0].at[p, 1 - mine] if minor else ops[0].at[1 - mine, p]
            cps.append(pltpu.make_async_remote_copy(
                src_ref=src, dst_ref=lands[0].at[p], send_sem=sem(0, p), recv_sem=sem(1, p),
                device_id=tuple(peer), device_id_type=MESH))
        return [cp.start for cp in cps], [cp.wait for cp in cps]

    return _Part([array], [jax.ShapeDtypeStruct((pieces, *array.shape[2:]), array.dtype)], {}, pieces, plan)


def _grid_edges(grid):
    first = last = None
    for ax, n in enumerate(grid):
        p = pl.program_id(ax)
        f, l = p == 0, p == n - 1
        first = f if first is None else jnp.logical_and(first, f)
        last = l if last is None else jnp.logical_and(last, l)
    return first, last


def _call(body, *, name, grid, in_specs, out_specs, out_shape, args, scratch_shapes=(), sem=None, carry=None):
    if carry is None:
        return pl.pallas_call(
            body, name=name, grid=grid, in_specs=list(in_specs), out_specs=list(out_specs),
            out_shape=list(out_shape), scratch_shapes=list(scratch_shapes), compiler_params=_cparams(sem),
        )(*args)
    n_in, n_out, n_scr = len(in_specs), len(out_specs), len(scratch_shapes)
    n_cin, n_cout = len(carry.operands), len(carry.landings)

    def wrapped(*refs):
        ins, cins = refs[:n_in], refs[n_in:n_in + n_cin]
        o0 = n_in + n_cin
        outs, couts = refs[o0:o0 + n_out], refs[o0 + n_out:o0 + n_out + n_cout]
        s0 = o0 + n_out + n_cout
        scr, sems = refs[s0:s0 + n_scr], refs[s0 + n_scr:]
        first, last = _grid_edges(grid)

        def plan():
            return carry.plan(cins, couts, lambda kind, k: sems[kind].at[k])

        def start_all():
            for start in plan()[0]:
                start()

        def wait_all():
            for wait in plan()[1]:
                wait()

        if grid:
            pl.when(first)(start_all)
            body(*ins, *outs, *scr)
            pl.when(last)(wait_all)
        else:
            start_all()
            body(*ins, *outs, *scr)
            wait_all()

    sem_arrays = [pltpu.SemaphoreType.DMA((carry.n_sems,))] * 3
    res = pl.pallas_call(
        wrapped, name=name, grid=grid,
        in_specs=[*in_specs, *[ANY] * n_cin], out_specs=[*out_specs, *[ANY] * n_cout],
        out_shape=[*out_shape, *carry.landings],
        scratch_shapes=[*scratch_shapes, *sem_arrays],
        input_output_aliases={n_in + i: n_out + j for i, j in carry.aliases.items()},
        compiler_params=_cparams(("arbitrary",) * len(grid) if grid else None, has_side_effects=True),
    )(*args, *carry.operands)
    return res[:n_out], res[n_out:]


MM_TILE = 1024
MM_K_TILE = 2048


def _matmul(name, a, b, dims, grid, a_spec, b_spec, out_shape, out_spec, epilogue,
            extras=(), extra_specs=(), prologue=None, carry=None):
    nk = grid[2]
    n_extra = len(extras)
    acc_shape = out_spec.block_shape[-2:]

    def lhs(a_ref):
        return a_ref[...] if prologue is None else prologue(a_ref[...])

    def body_one(a_ref, b_ref, *rest):
        epilogue(_dot(lhs(a_ref), b_ref[...], dims), rest[:n_extra], rest[n_extra:])

    def body_acc(a_ref, b_ref, *rest):
        acc = rest[-1]
        k = pl.program_id(2)
        part = _dot(lhs(a_ref), b_ref[...], dims)

        @pl.when(k == 0)
        def _():
            acc[...] = part

        @pl.when(k > 0)
        def _():
            acc[...] += part

        @pl.when(k == nk - 1)
        def _():
            epilogue(acc[...], rest[:n_extra], rest[n_extra:-1])

    res = _call(body_one if nk == 1 else body_acc, name=name, grid=grid,
                in_specs=[a_spec, b_spec, *extra_specs], out_specs=[out_spec], out_shape=[out_shape],
                args=(a, b, *extras), scratch_shapes=[] if nk == 1 else [pltpu.VMEM(acc_shape, F32)],
                sem=("parallel", "parallel", "arbitrary"), carry=carry)
    return res[0] if carry is None else (res[0][0], res[1])


def _store_as(acc, extra_refs, out_refs):
    out_refs[0][...] = acc.astype(out_refs[0].dtype)


def _square(u):
    return u * u


def mm_nn(name, a, b, out_dtype, tk=None, prologue=None, carry=None):
    (m, kk), n = a.shape, b.shape[1]
    tm, tn = _tile(m, MM_TILE), _tile(n, MM_TILE)
    tk = kk if tk is None else _tile(kk, tk)
    return _matmul(name, a, b, NN, (m // tm, n // tn, kk // tk),
                   pl.BlockSpec((tm, tk), lambda i, j, k: (i, k)),
                   pl.BlockSpec((tk, tn), lambda i, j, k: (k, j)),
                   jax.ShapeDtypeStruct((m, n), out_dtype),
                   pl.BlockSpec((tm, tn), lambda i, j, k: (i, j)), _store_as, prologue=prologue, carry=carry)


def mm_nt(name, a, b, out_dtype, epilogue=_store_as, extras=(), extra_specs=(), carry=None):
    (m, kk), n = a.shape, b.shape[0]
    tm, tn = _tile(m, MM_TILE), _tile(n, MM_TILE)
    return _matmul(name, a, b, NT, (m // tm, n // tn, 1),
                   pl.BlockSpec((tm, kk), lambda i, j, k: (i, 0)),
                   pl.BlockSpec((tn, kk), lambda i, j, k: (j, 0)),
                   jax.ShapeDtypeStruct((m, n), out_dtype),
                   pl.BlockSpec((tm, tn), lambda i, j, k: (i, j)), epilogue,
                   extras=extras, extra_specs=extra_specs, carry=carry)


def mm_tn(name, a, b, out_dtype, prologue=None, carry=None):
    (kk, m), n = a.shape, b.shape[1]
    tm, tn, tk = _tile(m, MM_TILE), _tile(n, MM_TILE), _tile(kk, MM_K_TILE)
    return _matmul(name, a, b, TN, (m // tm, n // tn, kk // tk),
                   pl.BlockSpec((tk, tm), lambda i, j, k: (k, i)),
                   pl.BlockSpec((tk, tn), lambda i, j, k: (k, j)),
                   jax.ShapeDtypeStruct((m, n), out_dtype),
                   pl.BlockSpec((tm, tn), lambda i, j, k: (i, j)), _store_as, prologue=prologue, carry=carry)


def up_proj(h2, wup_slabs):
    (m, kk), (_, _, ns) = h2.shape, wup_slabs.shape
    tm, tn = _tile(m, MM_TILE), _tile(ns, MM_TILE)
    r = ns // tn
    n = N_DEV * ns

    def epi(acc, extra_refs, out_refs):
        out_refs[0][...] = jnp.maximum(acc, 0.0).astype(BF16)

    return _matmul("up_proj", h2, wup_slabs, NN, (m // tm, n // tn, 1),
                   pl.BlockSpec((tm, kk), lambda i, j, k: (i, 0)),
                   pl.BlockSpec((None, kk, tn), lambda i, j, k: (j // r, 0, j % r)),
                   jax.ShapeDtypeStruct((m, n), BF16),
                   pl.BlockSpec((tm, tn), lambda i, j, k: (i, j)), epi)


def down_proj(u, wdown):
    return mm_nn("down_proj", u, wdown, F32, tk=MM_K_TILE, prologue=_square)


def down_bwd_act(dy, wdown, u):
    tm, tn = _tile(dy.shape[0], MM_TILE), _tile(wdown.shape[0], MM_TILE)

    def epi(acc, extra_refs, out_refs):
        out_refs[0][...] = (acc * (2.0 * extra_refs[0][...].astype(F32))).astype(BF16)

    return mm_nt("down_bwd_act", dy, wdown, BF16, epilogue=epi, extras=(u,),
                 extra_specs=(pl.BlockSpec((tm, tn), lambda i, j, k: (i, j)),))


def down_wgrad(u, dy):
    return mm_tn("down_wgrad", u, dy, BF16, prologue=_square)


def up_bwd_x(du, wup_slabs, carry=None):
    (m, kk), (_, n, ns) = du.shape, wup_slabs.shape
    tm, tk = _tile(m, MM_TILE), _tile(ns, MM_TILE)
    r = ns // tk
    return _matmul("up_bwd_x", du, wup_slabs, NT, (m // tm, 1, kk // tk),
                   pl.BlockSpec((tm, tk), lambda i, j, k: (i, k)),
                   pl.BlockSpec((None, n, tk), lambda i, j, k: (k // r, 0, k % r)),
                   jax.ShapeDtypeStruct((m, n), F32),
                   pl.BlockSpec((tm, n), lambda i, j, k: (i, 0)), _store_as, carry=carry)


def up_wgrad(h2, du, carry=None):
    (kk, m), n = h2.shape, du.shape[1]
    ns = n // N_DEV
    tm, tn, tk = _tile(m, MM_TILE), _tile(ns, MM_TILE), _tile(kk, MM_K_TILE)
    r = ns // tn
    return _matmul("up_wgrad", h2, du, TN, (m // tm, n // tn, kk // tk),
                   pl.BlockSpec((tk, tm), lambda i, j, k: (k, i)),
                   pl.BlockSpec((tk, tn), lambda i, j, k: (k, j)),
                   jax.ShapeDtypeStruct((N_DEV, m, ns), BF16),
                   pl.BlockSpec((None, tm, tn), lambda i, j, k: (j // r, i, j % r)), _store_as, carry=carry)


def _rstd(x):
    return lax.rsqrt(jnp.mean(x * x, axis=-1, keepdims=True) + EPS)


def _norm_bwd(x, g, dy):
    r = _rstd(x)
    xh = x * r
    dyg = dy * g
    dx = r * (dyg - xh * jnp.mean(dyg * xh, axis=-1, keepdims=True))
    return dx, jnp.sum(dy * xh, axis=0, keepdims=True)


def _row_spec(tr, d):
    return pl.BlockSpec((tr, d), lambda i: (i, 0))


def _vec_spec(d):
    return pl.BlockSpec((1, d), lambda i: (0, 0))


def _accum(ref, val):
    @pl.when(pl.program_id(0) == 0)
    def _():
        ref[...] = jnp.zeros_like(ref)

    ref[...] += val


def pre_norm(x, g, tr=256):
    t, d = x.shape
    tr = _pick(t, tr)

    def body(x_ref, g_ref, h_ref):
        xx = x_ref[...]
        h_ref[...] = (xx * _rstd(xx) * g_ref[...]).astype(BF16)

    return _call(body, name="pre_norm", grid=(t // tr,),
                 in_specs=[_row_spec(tr, d), _vec_spec(d)], out_specs=[_row_spec(tr, d)],
                 out_shape=[jax.ShapeDtypeStruct((t, d), BF16)], args=(x, g), sem=("parallel",))[0]


def mid_fwd(mixed, g_post, x, g_pre2, tr=256):
    t, d = x.shape
    tr = _pick(t, tr)

    def body(m_ref, gp_ref, x_ref, g2_ref, x1_ref, h2_ref):
        mm = m_ref[...]
        x1 = x_ref[...] + mm * _rstd(mm) * gp_ref[...]
        x1_ref[...] = x1
        h2_ref[...] = (x1 * _rstd(x1) * g2_ref[...]).astype(BF16)

    return _call(body, name="mid_fwd", grid=(t // tr,),
                 in_specs=[_row_spec(tr, d), _vec_spec(d), _row_spec(tr, d), _vec_spec(d)],
                 out_specs=[_row_spec(tr, d), _row_spec(tr, d)],
                 out_shape=[jax.ShapeDtypeStruct((t, d), F32), jax.ShapeDtypeStruct((t, d), BF16)],
                 args=(mixed, g_post, x, g_pre2), sem=("parallel",))


def loss_bwd(y, g_post2, x1, target, tr=256):
    t, d = y.shape
    tr = _pick(t, tr)

    def body(y_ref, g_ref, x1_ref, t_ref, sse_ref, dout_ref, dy_ref, dg_ref):
        yy = y_ref[...]
        g = g_ref[...]
        err = x1_ref[...] + yy * _rstd(yy) * g - t_ref[...]
        _accum(sse_ref, jnp.sum(jnp.sum(err * err, axis=1, keepdims=True), axis=0, keepdims=True))
        dout = err * (1.0 / d)
        dout_ref[...] = dout
        dy, dg = _norm_bwd(yy, g, dout)
        dy_ref[...] = dy.astype(BF16)
        _accum(dg_ref, dg)

    return _call(body, name="loss_bwd", grid=(t // tr,),
                 in_specs=[_row_spec(tr, d), _vec_spec(d), _row_spec(tr, d), _row_spec(tr, d)],
                 out_specs=[pl.BlockSpec((1, 1), lambda i: (0, 0)), _row_spec(tr, d), _row_spec(tr, d), _vec_spec(d)],
                 out_shape=[jax.ShapeDtypeStruct((1, 1), F32), jax.ShapeDtypeStruct((t, d), F32),
                            jax.ShapeDtypeStruct((t, d), BF16), jax.ShapeDtypeStruct((1, d), F32)],
                 args=(y, g_post2, x1, target), sem=("arbitrary",))


def mid_bwd(dh2, x1, g_pre2, dout, mixed, g_post, carry=None, tr=256):
    t, d = x1.shape
    tr = _pick(t, tr)

    def body(dh_ref, x1_ref, g2_ref, do_ref, m_ref, gp_ref, dx1_ref, dm_ref, dg2_ref, dgp_ref):
        d1, dg2 = _norm_bwd(x1_ref[...], g2_ref[...], dh_ref[...])
        dx1 = do_ref[...] + d1
        dx1_ref[...] = dx1
        dm, dgp = _norm_bwd(m_ref[...], gp_ref[...], dx1)
        dm_ref[...] = dm.astype(BF16)
        _accum(dg2_ref, dg2)
        _accum(dgp_ref, dgp)

    return _call(body, name="mid_bwd", grid=(t // tr,),
                 in_specs=[_row_spec(tr, d), _row_spec(tr, d), _vec_spec(d), _row_spec(tr, d), _row_spec(tr, d),
                           _vec_spec(d)],
                 out_specs=[_row_spec(tr, d), _row_spec(tr, d), _vec_spec(d), _vec_spec(d)],
                 out_shape=[jax.ShapeDtypeStruct((t, d), F32), jax.ShapeDtypeStruct((t, d), BF16),
                            jax.ShapeDtypeStruct((1, d), F32), jax.ShapeDtypeStruct((1, d), F32)],
                 args=(dh2, x1, g_pre2, dout, mixed, g_post), sem=("arbitrary",), carry=carry)


def first_bwd(dh1, x, g_pre, dx1, carry=None, tr=256):
    t, d = x.shape
    tr = _pick(t, tr)

    def body(dh_ref, x_ref, g_ref, dx1_ref, gx_ref, dg_ref):
        d0, dg = _norm_bwd(x_ref[...], g_ref[...], dh_ref[...])
        gx_ref[...] = dx1_ref[...] + d0
        _accum(dg_ref, dg)

    return _call(body, name="first_bwd", grid=(t // tr,),
                 in_specs=[_row_spec(tr, d), _row_spec(tr, d), _vec_spec(d), _row_spec(tr, d)],
                 out_specs=[_row_spec(tr, d), _vec_spec(d)],
                 out_shape=[jax.ShapeDtypeStruct((t, d), F32), jax.ShapeDtypeStruct((1, d), F32)],
                 args=(dh1, x, g_pre, dx1), sem=("arbitrary",), carry=carry)


def _attn_geometry(has_prev):
    r = lax.broadcasted_iota(jnp.int32, (BLOCK, 2 * BLOCK), 0)
    c = lax.broadcasted_iota(jnp.int32, (BLOCK, 2 * BLOCK), 1)
    dist = r + BLOCK - c
    valid = jnp.logical_and(jnp.logical_and(dist >= 0, dist < BLOCK), jnp.logical_or(c >= BLOCK, has_prev))
    return dist.astype(F32), valid


def _stack_pairs(x, g, pairs):
    base = g * pairs * LANES
    return jnp.concatenate([x[:, base + p * LANES:base + (p + 1) * LANES] for p in range(pairs)], axis=0)


def _unstack_pairs(xs, pairs):
    return jnp.concatenate([xs[p * BLOCK:(p + 1) * BLOCK, :] for p in range(pairs)], axis=1)


def _to_half(x, g, odd):
    lane = lax.broadcasted_iota(jnp.int32, x.shape, 1)
    y = x if (g == 1) == odd else pltpu.roll(x, HEAD_DIM, axis=1)
    return jnp.where((lane >= HEAD_DIM) == odd, y, 0.0)


def _from_halves(even, odd, g):
    lane = lax.broadcasted_iota(jnp.int32, even.shape, 1)
    if g == 0:
        return jnp.where(lane < HEAD_DIM, even + pltpu.roll(odd, HEAD_DIM, axis=1), 0.0)
    return jnp.where(lane >= HEAD_DIM, pltpu.roll(even, HEAD_DIM, axis=1) + odd, 0.0)


_PARITIES = [(g, odd) for g in range(N_KV_HEADS) for odd in (False, True)]


def _softmax_sink(s, sink_ref, g, odd, group, n_heads, geo):
    dist, valid = geo
    pairs = group // 2
    heads = [g * group + 2 * p + int(odd) for p in range(pairs)]
    bias = jnp.concatenate([(2.0 ** (-8.0 * (h + 1) / n_heads)) * dist for h in heads], axis=0)
    sink = jnp.concatenate([jnp.full((BLOCK, 1), sink_ref[0, h], F32) for h in heads], axis=0)
    s = jnp.where(jnp.concatenate([valid] * pairs, axis=0), s - bias, -jnp.inf)
    m = jnp.maximum(jnp.max(s, axis=-1, keepdims=True), sink)
    p = jnp.exp(s - m)
    p_sink = jnp.exp(sink - m)
    inv = 1.0 / (jnp.sum(p, axis=-1, keepdims=True) + p_sink)
    return p * inv, p_sink * inv


def attn_fwd(proj, sinks, gain, aw, carry=None):
    t = proj.shape[0]
    kw = N_KV_HEADS * HEAD_DIM
    n_heads = aw // HEAD_DIM
    group = n_heads // N_KV_HEADS
    pairs = group // 2
    assert kw == LANES and group % 2 == 0
    nb = t // BLOCK
    scale = HEAD_DIM ** -0.5

    def body(sink_ref, q_ref, k_ref, v_ref, g_ref, o_ref, on_ref):
        n = pl.program_id(0)
        cur = pl.multiple_of(n * BLOCK, BLOCK)
        prev = pl.multiple_of(jnp.maximum(n - 1, 0) * BLOCK, BLOCK)
        geo = _attn_geometry(n > 0)
        kcat = jnp.concatenate([k_ref[pl.ds(prev, BLOCK), :], k_ref[pl.ds(cur, BLOCK), :]], axis=0)
        vcat = jnp.concatenate([v_ref[pl.ds(prev, BLOCK), :], v_ref[pl.ds(cur, BLOCK), :]], axis=0)
        q = q_ref[...] * scale
        qs = [_stack_pairs(q, g, pairs) for g in range(N_KV_HEADS)]
        scores = [_dot(qs[g], _to_half(kcat, g, odd), NT) for g, odd in _PARITIES]
        probs = [_softmax_sink(s, sink_ref, g, odd, group, n_heads, geo)[0] for s, (g, odd) in zip(scores, _PARITIES)]
        outs = [_dot(p, _to_half(vcat, g, odd), NN) for p, (g, odd) in zip(probs, _PARITIES)]
        o = jnp.concatenate([_unstack_pairs(outs[2 * g] + outs[2 * g + 1], pairs) for g in range(N_KV_HEADS)], axis=1)
        o_ref[...] = o
        on_ref[...] = (o * _rstd(o) * g_ref[...]).astype(BF16)

    return _call(body, name="attn_fwd", grid=(nb,),
                 in_specs=[pl.BlockSpec(memory_space=pltpu.SMEM),
                           pl.BlockSpec((BLOCK, aw), lambda n: (n, 0)),
                           pl.BlockSpec((t, kw), lambda n: (0, aw // kw)),
                           pl.BlockSpec((t, kw), lambda n: (0, aw // kw + 1)),
                           pl.BlockSpec((1, aw), lambda n: (0, 0))],
                 out_specs=[pl.BlockSpec((BLOCK, aw), lambda n: (n, 0)), pl.BlockSpec((BLOCK, aw), lambda n: (n, 0))],
                 out_shape=[jax.ShapeDtypeStruct((t, aw), F32), jax.ShapeDtypeStruct((t, aw), BF16)],
                 args=(sinks, proj, proj, proj, gain), sem=("parallel",), carry=carry)


def attn_bwd(proj, sinks, gain, attn_o, dcat, aw, carry=None):
    t = proj.shape[0]
    kw = N_KV_HEADS * HEAD_DIM
    n_heads = aw // HEAD_DIM
    group = n_heads // N_KV_HEADS
    pairs = group // 2
    assert kw == LANES and group % 2 == 0
    nb = t // BLOCK
    scale = HEAD_DIM ** -0.5

    def body(sink_ref, q_ref, k_ref, v_ref, g_ref, o_ref, dn_ref, dq_ref, dk_ref, dv_ref, dsink_ref, dg_ref):
        n = pl.program_id(0)
        cur = pl.multiple_of(n * BLOCK, BLOCK)
        prev = pl.multiple_of(jnp.maximum(n - 1, 0) * BLOCK, BLOCK)
        geo = _attn_geometry(n > 0)

        @pl.when(n == 0)
        def _():
            dk_ref[...] = jnp.zeros_like(dk_ref)
            dv_ref[...] = jnp.zeros_like(dv_ref)
            dsink_ref[...] = jnp.zeros_like(dsink_ref)

        o = o_ref[...]
        do_all, dg = _norm_bwd(o, g_ref[...], dn_ref[...])
        _accum(dg_ref, dg)
        kcat = jnp.concatenate([k_ref[pl.ds(prev, BLOCK), :], k_ref[pl.ds(cur, BLOCK), :]], axis=0)
        vcat = jnp.concatenate([v_ref[pl.ds(prev, BLOCK), :], v_ref[pl.ds(cur, BLOCK), :]], axis=0)
        q = q_ref[...] * scale
        lane = lax.broadcasted_iota(jnp.int32, (1, LANES), 1)
        lane_s = lax.broadcasted_iota(jnp.int32, (pairs * BLOCK, LANES), 1)
        qs = [_stack_pairs(q, g, pairs) for g in range(N_KV_HEADS)]
        dos = [_stack_pairs(do_all, g, pairs) for g in range(N_KV_HEADS)]
        kxs = [_to_half(kcat, g, odd) for g, odd in _PARITIES]
        scores = [_dot(qs[g], kx, NT) for kx, (g, odd) in zip(kxs, _PARITIES)]
        dps = [_dot(dos[g], _to_half(vcat, g, odd), NT) for g, odd in _PARITIES]
        deltas = []
        for g in range(N_KV_HEADS):
            prod = dos[g] * _stack_pairs(o, g, pairs)
            delta_even = jnp.sum(jnp.where(lane_s < HEAD_DIM, prod, 0.0), axis=-1, keepdims=True)
            deltas += [delta_even, jnp.sum(prod, axis=-1, keepdims=True) - delta_even]
        dsink = jnp.zeros((1, LANES), F32)
        ps, dss = [], []
        for i, (g, odd) in enumerate(_PARITIES):
            p, p_sink = _softmax_sink(scores[i], sink_ref, g, odd, group, n_heads, geo)
            ps.append(p)
            dss.append(p * (dps[i] - deltas[i]))
            sink_rows = p_sink * deltas[i]
            for pr in range(pairs):
                h = g * group + 2 * pr + int(odd)
                dsink = dsink + jnp.where(
                    lane == h, -jnp.sum(sink_rows[pr * BLOCK:(pr + 1) * BLOCK], axis=0, keepdims=True), 0.0)
        dq_pairs = [_dot(ds, kx, NN) for ds, kx in zip(dss, kxs)]
        dk_halves = [_dot(ds, qs[g], TN) for ds, (g, odd) in zip(dss, _PARITIES)]
        dv_halves = [_dot(p, dos[g], TN) for p, (g, odd) in zip(ps, _PARITIES)]
        dq_ref[...] = jnp.concatenate(
            [_unstack_pairs((dq_pairs[2 * g] + dq_pairs[2 * g + 1]) * scale, pairs) for g in range(N_KV_HEADS)],
            axis=1).astype(BF16)
        dk_upd = _from_halves(dk_halves[0], dk_halves[1], 0) + _from_halves(dk_halves[2], dk_halves[3], 1)
        dv_upd = _from_halves(dv_halves[0], dv_halves[1], 0) + _from_halves(dv_halves[2], dv_halves[3], 1)
        dk_ref[pl.ds(prev, BLOCK), :] += dk_upd[:BLOCK]
        dv_ref[pl.ds(prev, BLOCK), :] += dv_upd[:BLOCK]
        dk_ref[pl.ds(cur, BLOCK), :] += dk_upd[BLOCK:]
        dv_ref[pl.ds(cur, BLOCK), :] += dv_upd[BLOCK:]
        dsink_ref[...] += dsink

    return _call(body, name="attn_bwd", grid=(nb,),
                 in_specs=[pl.BlockSpec(memory_space=pltpu.SMEM),
                           pl.BlockSpec((BLOCK, aw), lambda n: (n, 0)),
                           pl.BlockSpec((t, kw), lambda n: (0, aw // kw)),
                           pl.BlockSpec((t, kw), lambda n: (0, aw // kw + 1)),
                           pl.BlockSpec((1, aw), lambda n: (0, 0)),
                           pl.BlockSpec((BLOCK, aw), lambda n: (n, 0)),
                           pl.BlockSpec((BLOCK, aw), lambda n: (n, 0))],
                 out_specs=[pl.BlockSpec((BLOCK, aw), lambda n: (n, 0)),
                            pl.BlockSpec((t, kw), lambda n: (0, 0)), pl.BlockSpec((t, kw), lambda n: (0, 0)),
                            pl.BlockSpec((1, LANES), lambda n: (0, 0)), pl.BlockSpec((1, aw), lambda n: (0, 0))],
                 out_shape=[jax.ShapeDtypeStruct((t, aw), BF16), jax.ShapeDtypeStruct((t, kw), F32),
                            jax.ShapeDtypeStruct((t, kw), F32), jax.ShapeDtypeStruct((1, LANES), F32),
                            jax.ShapeDtypeStruct((1, aw), F32)],
                 args=(sinks, proj, proj, proj, gain, attn_o, dcat), sem=("arbitrary",), carry=carry)


def _sigmoid(x):
    return 1.0 / (1.0 + jnp.exp(-x))


def _chunk_geometry():
    row = lax.broadcasted_iota(jnp.int32, (CHUNK, CHUNK), 0)
    col = lax.broadcasted_iota(jnp.int32, (CHUNK, CHUNK), 1)
    return row, col


def _cumsum_rows(x, reverse=False):
    row, col = _chunk_geometry()
    tri = (col >= row) if reverse else (col <= row)
    return lax.dot_general(tri.astype(F32), x, ((NN), ((), ())), precision=HI, preferred_element_type=F32)


def _rep_sub(x4):
    k = x4.shape[-1]
    return jnp.broadcast_to(x4[:, None, :], (CHUNK // SUB, SUB, k)).reshape(CHUNK, k)


def _gates(q_r, f_r, lb):
    sg = _sigmoid(f_r)
    f = lb + (1.0 - lb) * sg
    sq = _sigmoid(q_r)
    return sg, f, sq, q_r * sq


def _offdiag_terms(b, j):
    c = b[j * SUB + SUB - 1:j * SUB + SUB, :]
    return jnp.exp(jnp.minimum(b - c, 0.0)), jnp.exp(jnp.minimum(c - b, 0.0))


def _store_heads(ref, x):
    for j in range(ref.shape[0]):
        ref[j] = x[:, _head(j)]


def _sub_rows(ref, r):
    rows = [ref[j, pl.ds(r, CHUNK // SUB, stride=SUB), :] for j in range(ref.shape[0])]
    return _rep_sub(jnp.concatenate(rows, axis=1))


def _diag_mask():
    row, col = _chunk_geometry()
    return jnp.logical_and((row // SUB) == (col // SUB), row >= col)


HGRN_HEADS_PER_STEP = 8


def _wide(refs):
    return jnp.concatenate([r[...] for r in refs], axis=1)


def _head(j):
    return slice(j * RNN_HEAD_DIM, (j + 1) * RNN_HEAD_DIM)


def _cat_heads(parts, hs):
    return jnp.concatenate([p[:, hs] for p in parts], axis=1)


def _offdiag_factors(q, k, b):
    rowi = lax.broadcasted_iota(jnp.int32, b.shape, 0)
    qs, ks, ers, ecs = [], [], [], []
    for j in range(CHUNK // SUB - 1):
        e_row, e_col = _offdiag_terms(b, j)
        e_row = jnp.where(rowi >= (j + 1) * SUB, e_row, 0.0)
        e_col = jnp.where((rowi // SUB) == j, e_col, 0.0)
        qs.append(q * e_row)
        ks.append(k * e_col)
        ers.append(e_row)
        ecs.append(e_col)
    return qs, ks, ers, ecs


def hgrn_fwd(proj, lb, norm_gain, col0, rw, carry=None):
    t = proj.shape[0]
    nh = rw // RNN_HEAD_DIM
    nc = t // CHUNK
    kd = RNN_HEAD_DIM
    cb = col0 // kd
    nsub = CHUNK // SUB

    hp = _pick(nh, HGRN_HEADS_PER_STEP)
    w = hp * kd

    def body(*refs):
        q_refs, f_refs, i_refs, g_refs = (refs[i * hp:(i + 1) * hp] for i in range(4))
        lb_ref, ng_ref, rnn_ref, o_ref, att_ref, st_ref, state, b_ref, k_ref = refs[4 * hp:]
        c = pl.program_id(1)

        @pl.when(c == 0)
        def _():
            state[...] = jnp.zeros_like(state)

        st_ref[...] = state[...]
        q_r, f_r, v, g_r = (_wide(rs) for rs in (q_refs, f_refs, i_refs, g_refs))
        _, f, _, q = _gates(q_r, f_r, lb_ref[...])
        k = 1.0 - f
        b = _cumsum_rows(jnp.log(f))
        _store_heads(b_ref, b)
        _store_heads(k_ref, k)
        qcat, kcat, _, _ = _offdiag_factors(q, k, b)
        row, col = _chunk_geometry()
        same = (row // SUB) == (col // SUB)
        rloc = lax.broadcasted_iota(jnp.int32, (CHUNK, w), 0) % SUB
        diag = [jnp.zeros((CHUNK, CHUNK), F32)] * hp
        for r in range(SUB):
            bs = _sub_rows(b_ref, r)
            ks = _sub_rows(k_ref, r)
            prod = q * jnp.exp(jnp.where(rloc >= r, b - bs, -jnp.inf)) * ks
            place = jnp.logical_and((col % SUB) == r, same)
            diag = [jnp.where(place, jnp.sum(prod[:, _head(j)], axis=-1, keepdims=True), diag[j]) for j in range(hp)]
        b_last = b[CHUNK - 1:CHUNK, :]
        qe = q * jnp.exp(b)
        kdec = k * jnp.exp(b_last - b)
        decay = jnp.exp(b_last)
        outs, normed, states = [], [], []
        for j in range(hp):
            hs = _head(j)
            att = diag[j] + _dot(_cat_heads(qcat, hs), _cat_heads(kcat, hs), NT)
            att_ref[j] = att
            sj = state[j]
            o = _dot(qe[:, hs], sj, NT) + _dot(att, v[:, hs], NN)
            outs.append(o)
            normed.append(o * _rstd(o))
            states.append(sj * decay[:, hs] + _dot(v[:, hs], kdec[:, hs], TN))
        for j in range(hp):
            state[j] = states[j]
        o_ref[...] = jnp.concatenate(outs, axis=1)
        gate = g_r * _sigmoid(g_r)
        rnn_ref[...] = (jnp.concatenate(normed, axis=1) * jnp.tile(ng_ref[...], (1, hp)) * gate).astype(BF16)

    def col(kidx, j):
        return pl.BlockSpec((CHUNK, kd), lambda hg, c: (c, cb + kidx * nh + hg * hp + j))

    return _call(body, name="hgrn_fwd", grid=(nh // hp, nc),
                 in_specs=[col(kidx, j) for kidx in range(4) for j in range(hp)] +
                          [pl.BlockSpec((1, w), lambda hg, c: (0, hg)), pl.BlockSpec((1, kd), lambda hg, c: (0, 0))],
                 out_specs=[pl.BlockSpec((CHUNK, w), lambda hg, c: (c, hg)),
                            pl.BlockSpec((CHUNK, w), lambda hg, c: (c, hg)),
                            pl.BlockSpec((hp, CHUNK, CHUNK), lambda hg, c: (hg, c, 0)),
                            pl.BlockSpec((None, hp, kd, kd), lambda hg, c: (c, hg, 0, 0))],
                 out_shape=[jax.ShapeDtypeStruct((t, rw), BF16), jax.ShapeDtypeStruct((t, rw), F32),
                            jax.ShapeDtypeStruct((nh, t, CHUNK), F32), jax.ShapeDtypeStruct((nc, nh, kd, kd), F32)],
                 args=(*([proj] * (4 * hp)), lb, norm_gain),
                 scratch_shapes=[pltpu.VMEM((hp, kd, kd), F32), pltpu.VMEM((hp, CHUNK, kd), F32),
                                 pltpu.VMEM((hp, CHUNK, kd), F32)],
                 sem=("parallel", "arbitrary"), carry=carry)


def hgrn_bwd(proj, lb, norm_gain, o_all, att_all, st_all, dcat, col0, rw, carry=None):
    t = proj.shape[0]
    nh = rw // RNN_HEAD_DIM
    nc = t // CHUNK
    kd = RNN_HEAD_DIM
    cb = col0 // kd
    nsub = CHUNK // SUB
    dcb = (dcat.shape[1] - rw) // kd

    hp = _pick(nh, HGRN_HEADS_PER_STEP)
    w = hp * kd

    def per_head(x, fn):
        return jnp.concatenate([jnp.broadcast_to(fn(x[:, _head(j)]), (CHUNK, kd)) for j in range(hp)], axis=1)

    def body(*refs):
        q_refs, f_refs, i_refs, g_refs = (refs[i * hp:(i + 1) * hp] for i in range(4))
        (lb_ref, ng_ref, o_ref, att_ref, st0_ref, st1_ref, d_ref, dq_ref, df_ref, di_ref, dg_ref, dlb_ref, dng_ref,
         dstate, b_ref, k_ref, dks_ref) = refs[4 * hp:]
        ci = pl.program_id(1)

        @pl.when(ci == 0)
        def _():
            dstate[...] = jnp.zeros_like(dstate)
            dlb_ref[...] = jnp.zeros_like(dlb_ref)
            dng_ref[...] = jnp.zeros_like(dng_ref)

        lbv = lb_ref[...]
        q_r, f_r, v, g_r = (_wide(rs) for rs in (q_refs, f_refs, i_refs, g_refs))
        sg, f, sq, q = _gates(q_r, f_r, lbv)
        k = 1.0 - f
        b = _cumsum_rows(jnp.log(f))
        _store_heads(b_ref, b)
        _store_heads(k_ref, k)
        row, col = _chunk_geometry()

        o = o_ref[...]
        ng = jnp.tile(ng_ref[...], (1, hp))
        sgg = _sigmoid(g_r)
        gate = g_r * sgg
        d_rnn = d_ref[...]
        r = per_head(o, _rstd)
        oh = o * r
        dg_ref[...] = (d_rnn * oh * ng * (sgg * (1.0 + g_r * (1.0 - sgg)))).astype(BF16)
        d_on = d_rnn * gate
        dng_rows = jnp.sum(d_on * oh, axis=0, keepdims=True)
        dng = dng_rows[:, _head(0)]
        for j in range(1, hp):
            dng = dng + dng_rows[:, _head(j)]
        dng_ref[...] += dng
        dyg = d_on * ng
        do = r * (dyg - oh * per_head(dyg * oh, lambda x: jnp.mean(x, axis=-1, keepdims=True)))

        b_last = b[CHUNK - 1:CHUNK, :]
        eb = jnp.exp(b)
        tail = jnp.exp(b_last - b)
        kdec = k * tail
        decay = jnp.exp(b_last)
        qe = q * eb
        qcat, kcat, ers, ecs = _offdiag_factors(q, k, b)
        diag_mask = _diag_mask()
        dqs, dks, dvs, dads, gsums, dstates = [], [], [], [], [], []
        for j in range(hp):
            hs = _head(j)
            do_h, v_h, dst = do[:, hs], v[:, hs], dstate[j]
            da = jnp.where(row >= col, _dot(do_h, v_h, NT), 0.0)
            dads.append(jnp.where(diag_mask, da, 0.0))
            dq = _dot(do_h, st0_ref[j], NN) * eb[:, hs]
            dk = _dot(v_h, dst, NN) * tail[:, hs]
            dvs.append(_dot(att_ref[j], do_h, TN) + _dot(kdec[:, hs], dst, NT))
            rq = _dot(da, _cat_heads(kcat, hs), NN)
            rk = _dot(da, _cat_heads(qcat, hs), TN)
            for jj in range(nsub - 1):
                dq = dq + ers[jj][:, hs] * rq[:, _head(jj)]
                dk = dk + ecs[jj][:, hs] * rk[:, _head(jj)]
            dqs.append(dq)
            dks.append(dk)
            gsums.append(jnp.sum(dst * st1_ref[j], axis=0, keepdims=True))
            dstates.append(dst * decay[:, hs] + _dot(do_h, qe[:, hs], TN))
        for j in range(hp):
            dstate[j] = dstates[j]
        dq = jnp.concatenate(dqs, axis=1)
        dk = jnp.concatenate(dks, axis=1)
        rloc = lax.broadcasted_iota(jnp.int32, (CHUNK, w), 0) % SUB
        for rr in range(SUB):
            bs = _sub_rows(b_ref, rr)
            ks = _sub_rows(k_ref, rr)
            e = jnp.exp(jnp.where(rloc >= rr, b - bs, -jnp.inf))
            pick = (col % SUB) == rr
            dacol = jnp.concatenate(
                [jnp.broadcast_to(jnp.sum(jnp.where(pick, dads[j], 0.0), axis=-1, keepdims=True), (CHUNK, kd))
                 for j in range(hp)], axis=1)
            wv = dacol * e
            dq = dq + wv * ks
            sums = jnp.sum((wv * q).reshape(nsub, SUB, w), axis=1)
            for j in range(hp):
                dks_ref[j, pl.ds(rr, nsub, stride=SUB), :] = sums[:, _head(j)]
        dk = dk + jnp.concatenate([dks_ref[j] for j in range(hp)], axis=1)

        dlf = _cumsum_rows(q * dq - k * dk, reverse=True) + jnp.concatenate(gsums, axis=1)
        dfv = dlf / f - dk
        df_ref[...] = (dfv * (1.0 - lbv) * sg * (1.0 - sg)).astype(BF16)
        dlb_ref[...] += jnp.sum(dfv * (1.0 - sg), axis=0, keepdims=True)
        dq_ref[...] = (dq * (sq * (1.0 + q_r * (1.0 - sq)))).astype(BF16)
        di_ref[...] = jnp.concatenate(dvs, axis=1).astype(BF16)

    def rev(c):
        return nc - 1 - c

    def col_in(kidx, j):
        return pl.BlockSpec((CHUNK, kd), lambda hg, c: (rev(c), cb + kidx * nh + hg * hp + j))

    tile = pl.BlockSpec((CHUNK, w), lambda hg, c: (rev(c), hg))
    return _call(body, name="hgrn_bwd", grid=(nh // hp, nc),
                 in_specs=[col_in(kidx, j) for kidx in range(4) for j in range(hp)] +
                          [pl.BlockSpec((1, w), lambda hg, c: (0, hg)), pl.BlockSpec((1, kd), lambda hg, c: (0, 0)),
                           tile,
                           pl.BlockSpec((hp, CHUNK, CHUNK), lambda hg, c: (hg, rev(c), 0)),
                           pl.BlockSpec((None, hp, kd, kd), lambda hg, c: (rev(c), hg, 0, 0)),
                           pl.BlockSpec((None, hp, kd, kd),
                                        lambda hg, c: (jnp.minimum(rev(c) + 1, nc - 1), hg, 0, 0)),
                           pl.BlockSpec((CHUNK, w), lambda hg, c: (rev(c), dcb // hp + hg))],
                 out_specs=[tile, tile, tile, tile,
                            pl.BlockSpec((1, w), lambda hg, c: (0, hg)),
                            pl.BlockSpec((None, 1, kd), lambda hg, c: (hg, 0, 0))],
                 out_shape=[jax.ShapeDtypeStruct((t, rw), BF16)] * 4 + [jax.ShapeDtypeStruct((1, rw), F32),
                                                                        jax.ShapeDtypeStruct((nh // hp, 1, kd), F32)],
                 args=(*([proj] * (4 * hp)), lb, norm_gain, o_all, att_all, st_all, st_all, dcat),
                 scratch_shapes=[pltpu.VMEM((hp, kd, kd), F32), pltpu.VMEM((hp, CHUNK, kd), F32),
                                 pltpu.VMEM((hp, CHUNK, kd), F32), pltpu.VMEM((hp, CHUNK, kd), F32)],
                 sem=("parallel", "arbitrary"), carry=carry)


def all_gather_slabs(shards):
    n = len(shards)

    def body(*refs):
        ins, outs = refs[:n], refs[n:2 * n]
        send_sems, recv_sems, local_sems = refs[2 * n:]
        x, y, c = _coords()
        me, sibling = (x, y, c), (x, y, 1 - c)
        chips = [(1 - x, y), (x, 1 - y), (1 - x, 1 - y)]

        def copy(a, k, block, to, src=None):
            slab = outs[a].at[_slab_index(block)]
            return pltpu.make_async_remote_copy(
                src_ref=slab if src is None else src, dst_ref=slab,
                send_sem=send_sems.at[a, k], recv_sem=recv_sems.at[a, k],
                device_id=to, device_id_type=MESH)

        mine = [pltpu.make_async_copy(ins[a], outs[a].at[_slab_index(me)], local_sems.at[a]) for a in range(n)]
        for cp in mine:
            cp.start()
        first = []
        for a in range(n):
            first.append(copy(a, 0, me, sibling, src=ins[a]))
            first += [copy(a, 1 + j, me, (*chip, c), src=ins[a]) for j, chip in enumerate(chips)]
        for cp in first:
            cp.start()
        passed = []
        for j, chip in enumerate(chips):
            for a in range(n):
                copy(a, 1 + j, (*chip, c), me).wait_recv()
                fwd = copy(a, 4 + j, (*chip, c), sibling)
                fwd.start()
                passed.append(fwd)
        for a in range(n):
            copy(a, 0, sibling, me).wait_recv()
            for j, chip in enumerate(chips):
                copy(a, 4 + j, (*chip, 1 - c), me).wait_recv()
        for cp in first + passed:
            cp.wait_send()
        for cp in mine:
            cp.wait()

    return pl.pallas_call(
        body, name="all_gather_weights",
        in_specs=[ANY] * n, out_specs=[ANY] * n,
        out_shape=[jax.ShapeDtypeStruct((N_DEV, *s.shape), s.dtype) for s in shards],
        scratch_shapes=[pltpu.SemaphoreType.DMA((n, 7)), pltpu.SemaphoreType.DMA((n, 7)),
                        pltpu.SemaphoreType.DMA((n,))],
        compiler_params=pltpu.CompilerParams(has_side_effects=True),
    )(*shards)


def exchange_halves(name, array, axis):
    return _call(lambda: None, name=name, grid=(), in_specs=[], out_specs=[], out_shape=[], args=(),
                 carry=_scatter_step(array, axis))[1][0]


def add_kept_half(name, kept, got, sel, axis, tr=256):
    minor = axis == "c"
    pieces, rows, cols = got.shape
    tr = _tile(rows, tr, mult=16)

    def body(sel_ref, k_ref, g_ref, o_ref):
        o_ref[...] = (k_ref[...].astype(F32) + g_ref[...].astype(F32)).astype(o_ref.dtype)

    kept_spec = (pl.BlockSpec((None, None, tr, cols), lambda p, i, s: (p, s[0], i, 0)) if minor else
                 pl.BlockSpec((None, None, tr, cols), lambda p, i, s: (s[0], p, i, 0)))
    return pl.pallas_call(
        body, name=name,
        grid_spec=pltpu.PrefetchScalarGridSpec(
            num_scalar_prefetch=1, grid=(pieces, rows // tr),
            in_specs=[kept_spec, pl.BlockSpec((None, tr, cols), lambda p, i, s: (p, i, 0))],
            out_specs=pl.BlockSpec((None, tr, cols), lambda p, i, s: (p, i, 0))),
        out_shape=jax.ShapeDtypeStruct(got.shape, got.dtype),
        compiler_params=_cparams(("parallel", "parallel")),
    )(sel, kept, got)


def _adamw(w, g, m, v):
    m = ADAM_B1 * m + (1.0 - ADAM_B1) * g
    v = ADAM_B2 * v + (1.0 - ADAM_B2) * (g * g)
    m_hat = m / (1.0 - ADAM_B1 ** ADAM_STEP)
    v_hat = v / (1.0 - ADAM_B2 ** ADAM_STEP)
    delta = -ADAM_LR * (m_hat / (jnp.sqrt(v_hat) + ADAM_EPS) + ADAM_WD * w)
    return delta, m, v


def add_adamw(name, kept, got, sel, w, m, v, tr=128):
    rows, cols = w.shape
    tr = _tile(rows, tr, mult=16)

    def body(sel_ref, k_ref, g_ref, w_ref, m_ref, v_ref, go_ref, d_ref, mo_ref, vo_ref):
        g = k_ref[...].astype(F32) + g_ref[...].astype(F32)
        go_ref[...] = g
        d_ref[...], mo_ref[...], vo_ref[...] = _adamw(w_ref[...], g, m_ref[...], v_ref[...])

    tile = pl.BlockSpec((tr, cols), lambda i, s: (i, 0))
    return pl.pallas_call(
        body, name=name,
        grid_spec=pltpu.PrefetchScalarGridSpec(
            num_scalar_prefetch=1, grid=(rows // tr,),
            in_specs=[pl.BlockSpec((None, None, tr, cols), lambda i, s: (s[0], 0, i, 0)),
                      pl.BlockSpec((None, tr, cols), lambda i, s: (0, i, 0)), tile, tile, tile],
            out_specs=[tile] * 4),
        out_shape=[jax.ShapeDtypeStruct((rows, cols), F32)] * 4,
        compiler_params=_cparams(("parallel",)),
    )(sel, kept, got, w, m, v)


def small_allreduce_adamw(partial, scale, w, m, v):
    rows = partial.shape[0]

    def body(p_ref, s_ref, w_ref, m_ref, v_ref, g_ref, d_ref, mo_ref, vo_ref, slots, send_sems, recv_sems):
        x, y, c = _coords()
        my_slot = _slab_index((x, y, c))
        slots[my_slot] = p_ref[...]
        copies = []
        for mask in range(1, N_DEV):
            to = tuple(1 - v_ if (mask >> s_) & 1 else v_ for v_, s_ in ((x, 2), (y, 1), (c, 0)))
            copies.append(pltpu.make_async_remote_copy(
                src_ref=p_ref, dst_ref=slots.at[my_slot],
                send_sem=send_sems.at[mask - 1], recv_sem=recv_sems.at[mask - 1],
                device_id=to, device_id_type=MESH))
        for cp in copies:
            cp.start()
        for cp in copies:
            cp.wait()
        total = slots[0]
        for b in range(1, N_DEV):
            total = total + slots[b]
        g = total * s_ref[...]
        g_ref[...] = g
        d_ref[...], mo_ref[...], vo_ref[...] = _adamw(w_ref[...], g, m_ref[...], v_ref[...])

    vm = pl.BlockSpec(memory_space=pltpu.VMEM)
    return pl.pallas_call(
        body, name="small_allreduce_adamw",
        in_specs=[vm] * 5, out_specs=[vm] * 4,
        out_shape=[jax.ShapeDtypeStruct((rows, LANES), F32)] * 4,
        scratch_shapes=[pltpu.VMEM((N_DEV, rows, LANES), F32),
                        pltpu.SemaphoreType.DMA((N_DEV - 1,)), pltpu.SemaphoreType.DMA((N_DEV - 1,))],
        compiler_params=pltpu.CompilerParams(has_side_effects=True),
    )(partial, scale, w, m, v)


_SMALL = ("attn_sinks", "attn_out_gain", "rnn_lb_logits", "rnn_norm_gain", "mix_pre_gain", "mix_post_gain",
          "mlp_pre_gain", "mlp_post_gain")


def _pack(parts):
    rows = []
    for p in parts:
        flat = p.reshape(-1).astype(F32)
        pad = (-flat.shape[0]) % LANES
        rows.append(jnp.pad(flat, (0, pad)).reshape(-1, LANES))
    packed = jnp.concatenate(rows, axis=0)
    pad_rows = (-packed.shape[0]) % 8
    return jnp.pad(packed, ((0, pad_rows), (0, 0)))


def _unpack(packed, shapes):
    out, r = [], 0
    for s in shapes:
        size = math.prod(s)
        nrows = -(-size // LANES)
        out.append(packed[r:r + nrows].reshape(-1)[:size].reshape(s))
        r += nrows
    return out


class _Scatter:
    def __init__(self, tag, grad, sels):
        self.tag, self.sels = tag, sels
        self.shape = grad.shape[1:]
        self.cur = grad.reshape(4, 2, *self.shape)
        self.stage = 0

    def step(self):
        return _scatter_step(self.cur, "cxy"[self.stage])

    def land(self, got, w=None, m=None, v=None):
        axis = "cxy"[self.stage]
        name = "rs_add_%s_%s" % (axis, self.tag)
        sel = self.sels[axis]
        self.stage += 1
        if axis == "y":
            return add_adamw(name, self.cur, got, sel, w, m, v)
        summed = add_kept_half(name, self.cur, got, sel, axis)
        self.cur = summed.reshape(2, summed.shape[0] // 2, *self.shape)
        return None


def kernel(x, w_in, attn_sinks, attn_out_gain, rnn_lb_logits, rnn_norm_gain, w_out, mix_pre_gain, mix_post_gain, mlp_pre_gain, mlp_post_gain, w_up, w_down, loss_target, m_w_in, m_attn_sinks, m_attn_out_gain, m_rnn_lb_logits, m_rnn_norm_gain, m_w_out, m_mix_pre_gain, m_mix_post_gain, m_mlp_pre_gain, m_mlp_post_gain, m_w_up, m_w_down, v_w_in, v_attn_sinks, v_attn_out_gain, v_rnn_lb_logits, v_rnn_norm_gain, v_w_out, v_mix_pre_gain, v_mix_post_gain, v_mlp_pre_gain, v_mlp_post_gain, v_w_up, v_w_down):
    xs, target = x[0], loss_target[0]
    t, d = xs.shape
    aw = d // 2
    rw = d - aw
    col0 = aw + 2 * N_KV_HEADS * HEAD_DIM
    small_w = dict(attn_sinks=attn_sinks, attn_out_gain=attn_out_gain, rnn_lb_logits=rnn_lb_logits,
                   rnn_norm_gain=rnn_norm_gain, mix_pre_gain=mix_pre_gain, mix_post_gain=mix_post_gain,
                   mlp_pre_gain=mlp_pre_gain, mlp_post_gain=mlp_post_gain)
    small_m = dict(attn_sinks=m_attn_sinks, attn_out_gain=m_attn_out_gain, rnn_lb_logits=m_rnn_lb_logits,
                   rnn_norm_gain=m_rnn_norm_gain, mix_pre_gain=m_mix_pre_gain, mix_post_gain=m_mix_post_gain,
                   mlp_pre_gain=m_mlp_pre_gain, mlp_post_gain=m_mlp_post_gain)
    small_v = dict(attn_sinks=v_attn_sinks, attn_out_gain=v_attn_out_gain, rnn_lb_logits=v_rnn_lb_logits,
                   rnn_norm_gain=v_rnn_norm_gain, mix_pre_gain=v_mix_pre_gain, mix_post_gain=v_mix_post_gain,
                   mlp_pre_gain=v_mlp_pre_gain, mlp_post_gain=v_mlp_post_gain)
    cx, cy, cc = _coords()
    sels = {a: jnp.reshape(v_, (1,)).astype(jnp.int32) for a, v_ in (("x", cx), ("y", cy), ("c", cc))}

    w_in_t, m_in_t, v_in_t = w_in[0].T, m_w_in[0].T, v_w_in[0].T
    s_in, s_out, s_up, s_down = (w.astype(BF16) for w in (w_in_t, w_out[0], w_up[0], w_down[0]))
    probs = jax.nn.softmax(rnn_lb_logits.astype(F32), axis=0)
    lb = probs[0:1]

    wint = all_gather_slabs([s_in])[0].reshape(-1, d)
    h1 = pre_norm(xs, mix_pre_gain)
    proj, (wout_half,) = mm_nt("in_proj", h1, wint, F32, carry=_gather_first(s_out))
    (attn_o, attn_n), (wout, wup_half) = attn_fwd(
        proj, attn_sinks, attn_out_gain, aw, carry=_merge(_gather_second(wout_half), _gather_first(s_up)))
    (rnn, o_r, att, st), (wup, wdown_half) = hgrn_fwd(
        proj, lb, rnn_norm_gain, col0, rw, carry=_merge(_gather_second(wup_half), _gather_first(s_down)))
    wout = wout.reshape(-1, d)
    cat = jnp.concatenate([attn_n, rnn], axis=1)
    mixed, (wdown,) = mm_nn("out_proj", cat, wout, F32, carry=_gather_second(wdown_half))
    wdown = wdown.reshape(-1, d)
    x1, h2 = mid_fwd(mixed, mix_post_gain, xs, mlp_pre_gain)
    u = up_proj(h2, wup)
    y = down_proj(u, wdown)
    sse, dout, dy, dg_mlppost = loss_bwd(y, mlp_post_gain, x1, target)

    du = down_bwd_act(dy, wdown, u)
    rs_down = _Scatter("down", down_wgrad(u, dy).reshape(N_DEV, -1, d), sels)
    dh2, (got,) = up_bwd_x(du, wup, carry=rs_down.step())
    rs_down.land(got)
    dwup, (got,) = up_wgrad(h2, du, carry=rs_down.step())
    rs_down.land(got)
    rs_up = _Scatter("up", dwup, sels)
    (dx1, dmixed, dg_mlppre, dg_mixpost), (got,) = mid_bwd(dh2, x1, mlp_pre_gain, dout, mixed, mix_post_gain,
                                                          carry=rs_up.step())
    rs_up.land(got)
    dcat = mm_nt("out_bwd_x", dmixed, wout, F32)
    rs_out = _Scatter("out", mm_tn("out_wgrad", cat, dmixed, BF16).reshape(N_DEV, -1, d), sels)
    (dq_r, df_r, di_r, dg_r, dlb, dng), (got_d, got_u, got_o) = hgrn_bwd(
        proj, lb, rnn_norm_gain, o_r, att, st, dcat, col0, rw,
        carry=_merge(rs_down.step(), rs_up.step(), rs_out.step()))
    out_down = rs_down.land(got_d, w_down[0], m_w_down[0], v_w_down[0])
    rs_up.land(got_u)
    rs_out.land(got_o)
    (dq_a, dk_a, dv_a, dsinks, daog), (got_u, got_o) = attn_bwd(
        proj, attn_sinks, attn_out_gain, attn_o, dcat, aw, carry=_merge(rs_up.step(), rs_out.step()))
    out_up = rs_up.land(got_u, w_up[0], m_w_up[0], v_w_up[0])
    rs_out.land(got_o)
    dproj = jnp.concatenate([dq_a, dk_a.astype(BF16), dv_a.astype(BF16), dq_r, df_r, di_r, dg_r], axis=1)
    dwin, (got,) = mm_tn("in_wgrad", dproj, h1, BF16, carry=rs_out.step())
    out_out = rs_out.land(got, w_out[0], m_w_out[0], v_w_out[0])
    rs_in = _Scatter("in", dwin.reshape(N_DEV, -1, d), sels)
    dh1, (got,) = mm_nn("in_bwd_x", dproj, wint, F32, tk=MM_K_TILE, carry=rs_in.step())
    rs_in.land(got)
    (grad_x, dg_mixpre), (got,) = first_bwd(dh1, xs, mix_pre_gain, dx1, carry=rs_in.step())
    rs_in.land(got)
    out_in = rs_in.land(exchange_halves("rs_exchange_y_in", rs_in.cur, "y"), w_in_t, m_in_t, v_in_t)
    big_out = [out_in, out_out, out_up, out_down]

    n_heads = attn_sinks.shape[1]
    jac = probs[0] * probs[1]
    partial = _pack([sse, dsinks[0, :n_heads], daog, jnp.stack([dlb[0], dlb[0]]), jnp.sum(dng, axis=0),
                     dg_mixpre, dg_mixpost, dg_mlppre, dg_mlppost])
    ones = [jnp.ones(small_w[k].shape, F32) for k in _SMALL]
    ones[2] = jnp.stack([jac, -jac])
    scale = _pack([jnp.full((1,), 0.5 / d, F32)] + ones)
    zero = jnp.zeros((1,), F32)
    outs = small_allreduce_adamw(partial, scale, _pack([zero] + [small_w[k] for k in _SMALL]),
                                 _pack([zero] + [small_m[k] for k in _SMALL]),
                                 _pack([jnp.ones((1,), F32)] + [small_v[k] for k in _SMALL]))
    shapes = [(1,)] + [small_w[k].shape for k in _SMALL]
    sgrad, sdelta, snm, snv = (_unpack(o, shapes) for o in outs)
    loss = sgrad[0][0]

    def big(i, j):
        o = big_out[i][j]
        return (o.T if i == 0 else o)[None]

    def ordered(j, smalls):
        s = dict(zip(_SMALL, smalls[1:]))
        return [big(0, j), s["attn_sinks"], s["attn_out_gain"], s["rnn_lb_logits"], s["rnn_norm_gain"], big(1, j),
                s["mix_pre_gain"], s["mix_post_gain"], s["mlp_pre_gain"], s["mlp_post_gain"], big(2, j), big(3, j)]

    return (loss, grad_x[None], *ordered(0, sgrad), *ordered(1, sdelta), *ordered(2, snm), *ordered(3, snv))
```

```python
import math

import jax
import jax.numpy as jnp
from jax import lax
from jax.experimental import pallas as pl
from jax.experimental.pallas import tpu as pltpu

F32 = jnp.float32
BF16 = jnp.bfloat16

HEAD_DIM = 64
N_KV_HEADS = 2
BLOCK = 128
RNN_HEAD_DIM = 128
CHUNK = 64
SUB = 16
EPS = 1e-6

ADAM_LR = 0.001
ADAM_B1 = 0.9
ADAM_B2 = 0.999
ADAM_EPS = 1e-08
ADAM_WD = 0.01
ADAM_STEP = 10

N_DEV = 8
LANES = 128
V7X_VMEM_LIMIT = 56 * 1024 * 1024
MESH = pl.DeviceIdType.MESH
HI = lax.Precision.HIGHEST
ANY = pl.BlockSpec(memory_space=pl.ANY)
_AXES = ("x", "y", "c")


def _cparams(sem=None, **kw):
    return pltpu.CompilerParams(dimension_semantics=sem, vmem_limit_bytes=V7X_VMEM_LIMIT, **kw)


def _dot(a, b, dims):
    return lax.dot_general(a.astype(BF16), b.astype(BF16), (dims, ((), ())), preferred_element_type=F32)


NN = ((1,), (0,))
NT = ((1,), (1,))
TN = ((0,), (0,))


def _pick(n, pref):
    t = min(n, pref)
    while n % t:
        t //= 2
    return t


def _tile(n, pref, mult=LANES):
    if n <= pref:
        return n
    t = pref - pref % mult
    while n % t:
        t -= mult
    return t


def _coords():
    return lax.axis_index("x"), lax.axis_index("y"), lax.axis_index("c")


def _slab_index(dev):
    return 4 * dev[0] + 2 * dev[1] + dev[2]


class _Part:
    def __init__(self, operands, landings, aliases, n_sems, plan):
        self.operands, self.landings, self.aliases, self.n_sems, self.plan = operands, landings, aliases, n_sems, plan


def _merge(*parts):
    operands, landings, aliases, plans = [], [], {}, []
    s0 = 0
    for p in parts:
        o0, l0 = len(operands), len(landings)
        aliases.update({o0 + i: l0 + j for i, j in p.aliases.items()})
        plans.append((p.plan, o0, len(p.operands), l0, len(p.landings), s0))
        operands += p.operands
        landings += p.landings
        s0 += p.n_sems

    def plan(ops, lands, sem):
        starts, waits = [], []
        for f, o0, no, l0, nl, off in plans:
            s, w = f(ops[o0:o0 + no], lands[l0:l0 + nl], lambda kind, k, off=off: sem(kind, off + k))
            starts += s
            waits += w
        return starts, waits

    return _Part(operands, landings, aliases, s0, plan)


def _gather_peers(x, y, c):
    return [(x, y, 1 - c), (1 - x, y, c), (x, 1 - y, c), (1 - x, 1 - y, c)]


def _gather_first(shard, rows=None, into=None):
    lo, hi = (0, shard.shape[0]) if rows is None else rows

    def plan(ops, lands, sem):
        x, y, c = _coords()
        me, peers = (x, y, c), _gather_peers(x, y, c)
        src = ops[0].at[pl.ds(lo, hi - lo)]

        def slab(block):
            return lands[0].at[_slab_index(block), pl.ds(lo, hi - lo)]

        def cp(k, block, to):
            return pltpu.make_async_remote_copy(
                src_ref=src, dst_ref=slab(block),
                send_sem=sem(0, k), recv_sem=sem(1, k), device_id=to, device_id_type=MESH)

        local = pltpu.make_async_copy(src, slab(me), sem(2, 0))
        sends = [cp(k, me, to) for k, to in enumerate(peers)]
        recvs = [cp(k, frm, me) for k, frm in enumerate(peers)]
        return ([local.start] + [s.start for s in sends],
                [local.wait] + [s.wait_send for s in sends] + [r.wait_recv for r in recvs])

    landing = jax.ShapeDtypeStruct((N_DEV, *shard.shape), shard.dtype)
    if into is None:
        return _Part([shard], [landing], {}, 4, plan)
    return _Part([shard, into], [landing], {1: 0}, 4, plan)


def _gather_second(gathered):
    def plan(ops, lands, sem):
        x, y, c = _coords()
        sibling = (x, y, 1 - c)
        chips = [(1 - x, y), (x, 1 - y), (1 - x, 1 - y)]

        def cp(k, block):
            slab = lands[0].at[_slab_index(block)]
            return pltpu.make_async_remote_copy(
                src_ref=slab, dst_ref=slab, send_sem=sem(0, k), recv_sem=sem(1, k),
                device_id=sibling, device_id_type=MESH)

        sends = [cp(k, (*chip, c)) for k, chip in enumerate(chips)]
        recvs = [cp(k, (*chip, 1 - c)) for k, chip in enumerate(chips)]
        return [s.start for s in sends], [s.wait_send for s in sends] + [r.wait_recv for r in recvs]

    return _Part([gathered], [jax.ShapeDtypeStruct(gathered.shape, gathered.dtype)], {0: 0}, 3, plan)


def _scatter_step(array, axis):
    minor = axis == "c"
    pieces = array.shape[0] if minor else array.shape[1]

    def plan(ops, lands, sem):
        coords = list(_coords())
        ai = _AXES.index(axis)
        mine = coords[ai]
        peer = list(coords)
        peer[ai] = 1 - mine
        cps = []
        for p in range(pieces):
            src = ops[0].at[p, 1 - mine] if minor else ops[0].at[1 - mine, p]
            cps.append(pltpu.make_async_remote_copy(
                src_ref=src, dst_ref=lands[0].at[p], send_sem=sem(0, p), recv_sem=sem(1, p),
                device_id=tuple(peer), device_id_type=MESH))
        return [cp.start for cp in cps], [cp.wait for cp in cps]

    return _Part([array], [jax.ShapeDtypeStruct((pieces, *array.shape[2:]), array.dtype)], {}, pieces, plan)


def _grid_edges(grid):
    first = last = None
    for ax, n in enumerate(grid):
        p = pl.program_id(ax)
        f, l = p == 0, p == n - 1
        first = f if first is None else jnp.logical_and(first, f)
        last = l if last is None else jnp.logical_and(last, l)
    return first, last


def _call(body, *, name, grid, in_specs, out_specs, out_shape, args, scratch_shapes=(), sem=None, carry=None):
    if carry is None:
        return pl.pallas_call(
            body, name=name, grid=grid, in_specs=list(in_specs), out_specs=list(out_specs),
            out_shape=list(out_shape), scratch_shapes=list(scratch_shapes), compiler_params=_cparams(sem),
        )(*args)
    n_in, n_out, n_scr = len(in_specs), len(out_specs), len(scratch_shapes)
    n_cin, n_cout = len(carry.operands), len(carry.landings)

    def wrapped(*refs):
        ins, cins = refs[:n_in], refs[n_in:n_in + n_cin]
        o0 = n_in + n_cin
        outs, couts = refs[o0:o0 + n_out], refs[o0 + n_out:o0 + n_out + n_cout]
        s0 = o0 + n_out + n_cout
        scr, sems = refs[s0:s0 + n_scr], refs[s0 + n_scr:]
        first, last = _grid_edges(grid)

        def plan():
            return carry.plan(cins, couts, lambda kind, k: sems[kind].at[k])

        def start_all():
            for start in plan()[0]:
                start()

        def wait_all():
            for wait in plan()[1]:
                wait()

        if grid:
            pl.when(first)(start_all)
            body(*ins, *outs, *scr)
            pl.when(last)(wait_all)
        else:
            start_all()
            body(*ins, *outs, *scr)
            wait_all()

    sem_arrays = [pltpu.SemaphoreType.DMA((carry.n_sems,))] * 3
    res = pl.pallas_call(
        wrapped, name=name, grid=grid,
        in_specs=[*in_specs, *[ANY] * n_cin], out_specs=[*out_specs, *[ANY] * n_cout],
        out_shape=[*out_shape, *carry.landings],
        scratch_shapes=[*scratch_shapes, *sem_arrays],
        input_output_aliases={n_in + i: n_out + j for i, j in carry.aliases.items()},
        compiler_params=_cparams(("arbitrary",) * len(grid) if grid else None, has_side_effects=True),
    )(*args, *carry.operands)
    return res[:n_out], res[n_out:]


MM_TILE = 1024
MM_K_TILE = 2048


def _matmul(name, a, b, dims, grid, a_spec, b_spec, out_shape, out_spec, epilogue,
            extras=(), extra_specs=(), prologue=None, carry=None):
    nk = grid[2]
    n_extra = len(extras)
    acc_shape = out_spec.block_shape[-2:]

    def lhs(a_ref):
        return a_ref[...] if prologue is None else prologue(a_ref[...])

    def body_one(a_ref, b_ref, *rest):
        epilogue(_dot(lhs(a_ref), b_ref[...], dims), rest[:n_extra], rest[n_extra:])

    def body_acc(a_ref, b_ref, *rest):
        acc = rest[-1]
        k = pl.program_id(2)
        part = _dot(lhs(a_ref), b_ref[...], dims)

        @pl.when(k == 0)
        def _():
            acc[...] = part

        @pl.when(k > 0)
        def _():
            acc[...] += part

        @pl.when(k == nk - 1)
        def _():
            epilogue(acc[...], rest[:n_extra], rest[n_extra:-1])

    res = _call(body_one if nk == 1 else body_acc, name=name, grid=grid,
                in_specs=[a_spec, b_spec, *extra_specs], out_specs=[out_spec], out_shape=[out_shape],
                args=(a, b, *extras), scratch_shapes=[] if nk == 1 else [pltpu.VMEM(acc_shape, F32)],
                sem=("parallel", "parallel", "arbitrary"), carry=carry)
    return res[0] if carry is None else (res[0][0], res[1])


def _store_as(acc, extra_refs, out_refs):
    out_refs[0][...] = acc.astype(out_refs[0].dtype)


def _square(u):
    return u * u


def mm_nn(name, a, b, out_dtype, tk=None, prologue=None, carry=None):
    (m, kk), n = a.shape, b.shape[1]
    tm, tn = _tile(m, MM_TILE), _tile(n, MM_TILE)
    tk = kk if tk is None else _tile(kk, tk)
    return _matmul(name, a, b, NN, (m // tm, n // tn, kk // tk),
                   pl.BlockSpec((tm, tk), lambda i, j, k: (i, k)),
                   pl.BlockSpec((tk, tn), lambda i, j, k: (k, j)),
                   jax.ShapeDtypeStruct((m, n), out_dtype),
                   pl.BlockSpec((tm, tn), lambda i, j, k: (i, j)), _store_as, prologue=prologue, carry=carry)


def mm_nt(name, a, b, out_dtype, epilogue=_store_as, extras=(), extra_specs=(), carry=None):
    (m, kk), n = a.shape, b.shape[0]
    tm, tn = _tile(m, MM_TILE), _tile(n, MM_TILE)
    return _matmul(name, a, b, NT, (m // tm, n // tn, 1),
                   pl.BlockSpec((tm, kk), lambda i, j, k: (i, 0)),
                   pl.BlockSpec((tn, kk), lambda i, j, k: (j, 0)),
                   jax.ShapeDtypeStruct((m, n), out_dtype),
                   pl.BlockSpec((tm, tn), lambda i, j, k: (i, j)), epilogue,
                   extras=extras, extra_specs=extra_specs, carry=carry)


def mm_tn(name, a, b, out_dtype, prologue=None, carry=None):
    (kk, m), n = a.shape, b.shape[1]
    tm, tn, tk = _tile(m, MM_TILE), _tile(n, MM_TILE), _tile(kk, MM_K_TILE)
    return _matmul(name, a, b, TN, (m // tm, n // tn, kk // tk),
                   pl.BlockSpec((tk, tm), lambda i, j, k: (k, i)),
                   pl.BlockSpec((tk, tn), lambda i, j, k: (k, j)),
                   jax.ShapeDtypeStruct((m, n), out_dtype),
                   pl.BlockSpec((tm, tn), lambda i, j, k: (i, j)), _store_as, prologue=prologue, carry=carry)


def up_proj(h2, wup_slabs):
    (m, kk), (_, _, ns) = h2.shape, wup_slabs.shape
    tm, tn = _tile(m, MM_TILE), _tile(ns, MM_TILE)
    r = ns // tn
    n = N_DEV * ns

    def epi(acc, extra_refs, out_refs):
        out_refs[0][...] = jnp.maximum(acc, 0.0).astype(BF16)

    return _matmul("up_proj", h2, wup_slabs, NN, (m // tm, n // tn, 1),
                   pl.BlockSpec((tm, kk), lambda i, j, k: (i, 0)),
                   pl.BlockSpec((None, kk, tn), lambda i, j, k: (j // r, 0, j % r)),
                   jax.ShapeDtypeStruct((m, n), BF16),
                   pl.BlockSpec((tm, tn), lambda i, j, k: (i, j)), epi)


def down_proj(u, wdown):
    return mm_nn("down_proj", u, wdown, F32, tk=MM_K_TILE, prologue=_square)


def down_bwd_act(dy, wdown, u):
    tm, tn = _tile(dy.shape[0], MM_TILE), _tile(wdown.shape[0], MM_TILE)

    def epi(acc, extra_refs, out_refs):
        out_refs[0][...] = (acc * (2.0 * extra_refs[0][...].astype(F32))).astype(BF16)

    return mm_nt("down_bwd_act", dy, wdown, BF16, epilogue=epi, extras=(u,),
                 extra_specs=(pl.BlockSpec((tm, tn), lambda i, j, k: (i, j)),))


def down_wgrad(u, dy):
    return mm_tn("down_wgrad", u, dy, BF16, prologue=_square)


def up_bwd_x(du, wup_slabs, carry=None):
    (m, kk), (_, n, ns) = du.shape, wup_slabs.shape
    tm, tk = _tile(m, MM_TILE), _tile(ns, MM_TILE)
    r = ns // tk
    return _matmul("up_bwd_x", du, wup_slabs, NT, (m // tm, 1, kk // tk),
                   pl.BlockSpec((tm, tk), lambda i, j, k: (i, k)),
                   pl.BlockSpec((None, n, tk), lambda i, j, k: (k // r, 0, k % r)),
                   jax.ShapeDtypeStruct((m, n), F32),
                   pl.BlockSpec((tm, n), lambda i, j, k: (i, 0)), _store_as, carry=carry)


def up_wgrad(h2, du, carry=None):
    (kk, m), n = h2.shape, du.shape[1]
    ns = n // N_DEV
    tm, tn, tk = _tile(m, MM_TILE), _tile(ns, MM_TILE), _tile(kk, MM_K_TILE)
    r = ns // tn
    return _matmul("up_wgrad", h2, du, TN, (m // tm, n // tn, kk // tk),
                   pl.BlockSpec((tk, tm), lambda i, j, k: (k, i)),
                   pl.BlockSpec((tk, tn), lambda i, j, k: (k, j)),
                   jax.ShapeDtypeStruct((N_DEV, m, ns), BF16),
                   pl.BlockSpec((None, tm, tn), lambda i, j, k: (j // r, i, j % r)), _store_as, carry=carry)


def _rstd(x):
    return lax.rsqrt(jnp.mean(x * x, axis=-1, keepdims=True) + EPS)


def _norm_bwd(x, g, dy):
    r = _rstd(x)
    xh = x * r
    dyg = dy * g
    dx = r * (dyg - xh * jnp.mean(dyg * xh, axis=-1, keepdims=True))
    return dx, jnp.sum(dy * xh, axis=0, keepdims=True)


def _row_spec(tr, d):
    return pl.BlockSpec((tr, d), lambda i: (i, 0))


def _vec_spec(d):
    return pl.BlockSpec((1, d), lambda i: (0, 0))


def _accum(ref, val):
    @pl.when(pl.program_id(0) == 0)
    def _():
        ref[...] = jnp.zeros_like(ref)

    ref[...] += val


def pre_norm(x, g, tr=256):
    t, d = x.shape
    tr = _pick(t, tr)

    def body(x_ref, g_ref, h_ref):
        xx = x_ref[...]
        h_ref[...] = (xx * _rstd(xx) * g_ref[...]).astype(BF16)

    return _call(body, name="pre_norm", grid=(t // tr,),
                 in_specs=[_row_spec(tr, d), _vec_spec(d)], out_specs=[_row_spec(tr, d)],
                 out_shape=[jax.ShapeDtypeStruct((t, d), BF16)], args=(x, g), sem=("parallel",))[0]


def mid_fwd(mixed, g_post, x, g_pre2, tr=256):
    t, d = x.shape
    tr = _pick(t, tr)

    def body(m_ref, gp_ref, x_ref, g2_ref, x1_ref, h2_ref):
        mm = m_ref[...]
        x1 = x_ref[...] + mm * _rstd(mm) * gp_ref[...]
        x1_ref[...] = x1
        h2_ref[...] = (x1 * _rstd(x1) * g2_ref[...]).astype(BF16)

    return _call(body, name="mid_fwd", grid=(t // tr,),
                 in_specs=[_row_spec(tr, d), _vec_spec(d), _row_spec(tr, d), _vec_spec(d)],
                 out_specs=[_row_spec(tr, d), _row_spec(tr, d)],
                 out_shape=[jax.ShapeDtypeStruct((t, d), F32), jax.ShapeDtypeStruct((t, d), BF16)],
                 args=(mixed, g_post, x, g_pre2), sem=("parallel",))


def loss_bwd(y, g_post2, x1, target, tr=256):
    t, d = y.shape
    tr = _pick(t, tr)

    def body(y_ref, g_ref, x1_ref, t_ref, sse_ref, dout_ref, dy_ref, dg_ref):
        yy = y_ref[...]
        g = g_ref[...]
        err = x1_ref[...] + yy * _rstd(yy) * g - t_ref[...]
        _accum(sse_ref, jnp.sum(jnp.sum(err * err, axis=1, keepdims=True), axis=0, keepdims=True))
        dout = err * (1.0 / d)
        dout_ref[...] = dout
        dy, dg = _norm_bwd(yy, g, dout)
        dy_ref[...] = dy.astype(BF16)
        _accum(dg_ref, dg)

    return _call(body, name="loss_bwd", grid=(t // tr,),
                 in_specs=[_row_spec(tr, d), _vec_spec(d), _row_spec(tr, d), _row_spec(tr, d)],
                 out_specs=[pl.BlockSpec((1, 1), lambda i: (0, 0)), _row_spec(tr, d), _row_spec(tr, d), _vec_spec(d)],
                 out_shape=[jax.ShapeDtypeStruct((1, 1), F32), jax.ShapeDtypeStruct((t, d), F32),
                            jax.ShapeDtypeStruct((t, d), BF16), jax.ShapeDtypeStruct((1, d), F32)],
                 args=(y, g_post2, x1, target), sem=("arbitrary",))


def mid_bwd(dh2, x1, g_pre2, dout, mixed, g_post, carry=None, tr=256):
    t, d = x1.shape
    tr = _pick(t, tr)

    def body(dh_ref, x1_ref, g2_ref, do_ref, m_ref, gp_ref, dx1_ref, dm_ref, dg2_ref, dgp_ref):
        d1, dg2 = _norm_bwd(x1_ref[...], g2_ref[...], dh_ref[...])
        dx1 = do_ref[...] + d1
        dx1_ref[...] = dx1
        dm, dgp = _norm_bwd(m_ref[...], gp_ref[...], dx1)
        dm_ref[...] = dm.astype(BF16)
        _accum(dg2_ref, dg2)
        _accum(dgp_ref, dgp)

    return _call(body, name="mid_bwd", grid=(t // tr,),
                 in_specs=[_row_spec(tr, d), _row_spec(tr, d), _vec_spec(d), _row_spec(tr, d), _row_spec(tr, d),
                           _vec_spec(d)],
                 out_specs=[_row_spec(tr, d), _row_spec(tr, d), _vec_spec(d), _vec_spec(d)],
                 out_shape=[jax.ShapeDtypeStruct((t, d), F32), jax.ShapeDtypeStruct((t, d), BF16),
                            jax.ShapeDtypeStruct((1, d), F32), jax.ShapeDtypeStruct((1, d), F32)],
                 args=(dh2, x1, g_pre2, dout, mixed, g_post), sem=("arbitrary",), carry=carry)


def first_bwd(dh1, x, g_pre, dx1, carry=None, tr=256):
    t, d = x.shape
    tr = _pick(t, tr)

    def body(dh_ref, x_ref, g_ref, dx1_ref, gx_ref, dg_ref):
        d0, dg = _norm_bwd(x_ref[...], g_ref[...], dh_ref[...])
        gx_ref[...] = dx1_ref[...] + d0
        _accum(dg_ref, dg)

    return _call(body, name="first_bwd", grid=(t // tr,),
                 in_specs=[_row_spec(tr, d), _row_spec(tr, d), _vec_spec(d), _row_spec(tr, d)],
                 out_specs=[_row_spec(tr, d), _vec_spec(d)],
                 out_shape=[jax.ShapeDtypeStruct((t, d), F32), jax.ShapeDtypeStruct((1, d), F32)],
                 args=(dh1, x, g_pre, dx1), sem=("arbitrary",), carry=carry)


def _attn_geometry(has_prev):
    r = lax.broadcasted_iota(jnp.int32, (BLOCK, 2 * BLOCK), 0)
    c = lax.broadcasted_iota(jnp.int32, (BLOCK, 2 * BLOCK), 1)
    dist = r + BLOCK - c
    valid = jnp.logical_and(jnp.logical_and(dist >= 0, dist < BLOCK), jnp.logical_or(c >= BLOCK, has_prev))
    return dist.astype(F32), valid


def _stack_pairs(x, g, pairs):
    base = g * pairs * LANES
    return jnp.concatenate([x[:, base + p * LANES:base + (p + 1) * LANES] for p in range(pairs)], axis=0)


def _unstack_pairs(xs, pairs):
    return jnp.concatenate([xs[p * BLOCK:(p + 1) * BLOCK, :] for p in range(pairs)], axis=1)


def _to_half(x, g, odd):
    lane = lax.broadcasted_iota(jnp.int32, x.shape, 1)
    y = x if (g == 1) == odd else pltpu.roll(x, HEAD_DIM, axis=1)
    return jnp.where((lane >= HEAD_DIM) == odd, y, 0.0)


def _from_halves(even, odd, g):
    lane = lax.broadcasted_iota(jnp.int32, even.shape, 1)
    if g == 0:
        return jnp.where(lane < HEAD_DIM, even + pltpu.roll(odd, HEAD_DIM, axis=1), 0.0)
    return jnp.where(lane >= HEAD_DIM, pltpu.roll(even, HEAD_DIM, axis=1) + odd, 0.0)


_PARITIES = [(g, odd) for g in range(N_KV_HEADS) for odd in (False, True)]


def _softmax_sink(s, sink_ref, g, odd, group, n_heads, geo):
    dist, valid = geo
    pairs = group // 2
    heads = [g * group + 2 * p + int(odd) for p in range(pairs)]
    bias = jnp.concatenate([(2.0 ** (-8.0 * (h + 1) / n_heads)) * dist for h in heads], axis=0)
    sink = jnp.concatenate([jnp.full((BLOCK, 1), sink_ref[0, h], F32) for h in heads], axis=0)
    s = jnp.where(jnp.concatenate([valid] * pairs, axis=0), s - bias, -jnp.inf)
    m = jnp.maximum(jnp.max(s, axis=-1, keepdims=True), sink)
    p = jnp.exp(s - m)
    p_sink = jnp.exp(sink - m)
    inv = 1.0 / (jnp.sum(p, axis=-1, keepdims=True) + p_sink)
    return p * inv, p_sink * inv


def attn_fwd(proj, sinks, gain, aw, carry=None):
    t = proj.shape[0]
    kw = N_KV_HEADS * HEAD_DIM
    n_heads = aw // HEAD_DIM
    group = n_heads // N_KV_HEADS
    pairs = group // 2
    assert kw == LANES and group % 2 == 0
    nb = t // BLOCK
    scale = HEAD_DIM ** -0.5

    def body(sink_ref, q_ref, k_ref, v_ref, g_ref, o_ref, on_ref):
        n = pl.program_id(0)
        cur = pl.multiple_of(n * BLOCK, BLOCK)
        prev = pl.multiple_of(jnp.maximum(n - 1, 0) * BLOCK, BLOCK)
        geo = _attn_geometry(n > 0)
        kcat = jnp.concatenate([k_ref[pl.ds(prev, BLOCK), :], k_ref[pl.ds(cur, BLOCK), :]], axis=0)
        vcat = jnp.concatenate([v_ref[pl.ds(prev, BLOCK), :], v_ref[pl.ds(cur, BLOCK), :]], axis=0)
        q = q_ref[...] * scale
        qs = [_stack_pairs(q, g, pairs) for g in range(N_KV_HEADS)]
        scores = [_dot(qs[g], _to_half(kcat, g, odd), NT) for g, odd in _PARITIES]
        probs = [_softmax_sink(s, sink_ref, g, odd, group, n_heads, geo)[0] for s, (g, odd) in zip(scores, _PARITIES)]
        outs = [_dot(p, _to_half(vcat, g, odd), NN) for p, (g, odd) in zip(probs, _PARITIES)]
        o = jnp.concatenate([_unstack_pairs(outs[2 * g] + outs[2 * g + 1], pairs) for g in range(N_KV_HEADS)], axis=1)
        o_ref[...] = o
        on_ref[...] = (o * _rstd(o) * g_ref[...]).astype(BF16)

    return _call(body, name="attn_fwd", grid=(nb,),
                 in_specs=[pl.BlockSpec(memory_space=pltpu.SMEM),
                           pl.BlockSpec((BLOCK, aw), lambda n: (n, 0)),
                           pl.BlockSpec((t, kw), lambda n: (0, aw // kw)),
                           pl.BlockSpec((t, kw), lambda n: (0, aw // kw + 1)),
                           pl.BlockSpec((1, aw), lambda n: (0, 0))],
                 out_specs=[pl.BlockSpec((BLOCK, aw), lambda n: (n, 0)), pl.BlockSpec((BLOCK, aw), lambda n: (n, 0))],
                 out_shape=[jax.ShapeDtypeStruct((t, aw), F32), jax.ShapeDtypeStruct((t, aw), BF16)],
                 args=(sinks, proj, proj, proj, gain), sem=("parallel",), carry=carry)


def attn_bwd(proj, sinks, gain, attn_o, dcat, aw, carry=None):
    t = proj.shape[0]
    kw = N_KV_HEADS * HEAD_DIM
    n_heads = aw // HEAD_DIM
    group = n_heads // N_KV_HEADS
    pairs = group // 2
    assert kw == LANES and group % 2 == 0
    nb = t // BLOCK
    scale = HEAD_DIM ** -0.5

    def body(sink_ref, q_ref, k_ref, v_ref, g_ref, o_ref, dn_ref, dq_ref, dk_ref, dv_ref, dsink_ref, dg_ref):
        n = pl.program_id(0)
        cur = pl.multiple_of(n * BLOCK, BLOCK)
        prev = pl.multiple_of(jnp.maximum(n - 1, 0) * BLOCK, BLOCK)
        geo = _attn_geometry(n > 0)

        @pl.when(n == 0)
        def _():
            dk_ref[...] = jnp.zeros_like(dk_ref)
            dv_ref[...] = jnp.zeros_like(dv_ref)
            dsink_ref[...] = jnp.zeros_like(dsink_ref)

        o = o_ref[...]
        do_all, dg = _norm_bwd(o, g_ref[...], dn_ref[...])
        _accum(dg_ref, dg)
        kcat = jnp.concatenate([k_ref[pl.ds(prev, BLOCK), :], k_ref[pl.ds(cur, BLOCK), :]], axis=0)
        vcat = jnp.concatenate([v_ref[pl.ds(prev, BLOCK), :], v_ref[pl.ds(cur, BLOCK), :]], axis=0)
        q = q_ref[...] * scale
        lane = lax.broadcasted_iota(jnp.int32, (1, LANES), 1)
        lane_s = lax.broadcasted_iota(jnp.int32, (pairs * BLOCK, LANES), 1)
        qs = [_stack_pairs(q, g, pairs) for g in range(N_KV_HEADS)]
        dos = [_stack_pairs(do_all, g, pairs) for g in range(N_KV_HEADS)]
        kxs = [_to_half(kcat, g, odd) for g, odd in _PARITIES]
        scores = [_dot(qs[g], kx, NT) for kx, (g, odd) in zip(kxs, _PARITIES)]
        dps = [_dot(dos[g], _to_half(vcat, g, odd), NT) for g, odd in _PARITIES]
        deltas = []
        for g in range(N_KV_HEADS):
            prod = dos[g] * _stack_pairs(o, g, pairs)
            delta_even = jnp.sum(jnp.where(lane_s < HEAD_DIM, prod, 0.0), axis=-1, keepdims=True)
            deltas += [delta_even, jnp.sum(prod, axis=-1, keepdims=True) - delta_even]
        dsink = jnp.zeros((1, LANES), F32)
        ps, dss = [], []
        for i, (g, odd) in enumerate(_PARITIES):
            p, p_sink = _softmax_sink(scores[i], sink_ref, g, odd, group, n_heads, geo)
            ps.append(p)
            dss.append(p * (dps[i] - deltas[i]))
            sink_rows = p_sink * deltas[i]
            for pr in range(pairs):
                h = g * group + 2 * pr + int(odd)
                dsink = dsink + jnp.where(
                    lane == h, -jnp.sum(sink_rows[pr * BLOCK:(pr + 1) * BLOCK], axis=0, keepdims=True), 0.0)
        dq_pairs = [_dot(ds, kx, NN) for ds, kx in zip(dss, kxs)]
        dk_halves = [_dot(ds, qs[g], TN) for ds, (g, odd) in zip(dss, _PARITIES)]
        dv_halves = [_dot(p, dos[g], TN) for p, (g, odd) in zip(ps, _PARITIES)]
        dq_ref[...] = jnp.concatenate(
            [_unstack_pairs((dq_pairs[2 * g] + dq_pairs[2 * g + 1]) * scale, pairs) for g in range(N_KV_HEADS)],
            axis=1).astype(BF16)
        dk_upd = _from_halves(dk_halves[0], dk_halves[1], 0) + _from_halves(dk_halves[2], dk_halves[3], 1)
        dv_upd = _from_halves(dv_halves[0], dv_halves[1], 0) + _from_halves(dv_halves[2], dv_halves[3], 1)
        dk_ref[pl.ds(prev, BLOCK), :] += dk_upd[:BLOCK]
        dv_ref[pl.ds(prev, BLOCK), :] += dv_upd[:BLOCK]
        dk_ref[pl.ds(cur, BLOCK), :] += dk_upd[BLOCK:]
        dv_ref[pl.ds(cur, BLOCK), :] += dv_upd[BLOCK:]
        dsink_ref[...] += dsink

    return _call(body, name="attn_bwd", grid=(nb,),
                 in_specs=[pl.BlockSpec(memory_space=pltpu.SMEM),
                           pl.BlockSpec((BLOCK, aw), lambda n: (n, 0)),
                           pl.BlockSpec((t, kw), lambda n: (0, aw // kw)),
                           pl.BlockSpec((t, kw), lambda n: (0, aw // kw + 1)),
                           pl.BlockSpec((1, aw), lambda n: (0, 0)),
                           pl.BlockSpec((BLOCK, aw), lambda n: (n, 0)),
                           pl.BlockSpec((BLOCK, aw), lambda n: (n, 0))],
                 out_specs=[pl.BlockSpec((BLOCK, aw), lambda n: (n, 0)),
                            pl.BlockSpec((t, kw), lambda n: (0, 0)), pl.BlockSpec((t, kw), lambda n: (0, 0)),
                            pl.BlockSpec((1, LANES), lambda n: (0, 0)), pl.BlockSpec((1, aw), lambda n: (0, 0))],
                 out_shape=[jax.ShapeDtypeStruct((t, aw), BF16), jax.ShapeDtypeStruct((t, kw), F32),
                            jax.ShapeDtypeStruct((t, kw), F32), jax.ShapeDtypeStruct((1, LANES), F32),
                            jax.ShapeDtypeStruct((1, aw), F32)],
                 args=(sinks, proj, proj, proj, gain, attn_o, dcat), sem=("arbitrary",), carry=carry)


def _sigmoid(x):
    return 1.0 / (1.0 + jnp.exp(-x))


def _chunk_geometry():
    row = lax.broadcasted_iota(jnp.int32, (CHUNK, CHUNK), 0)
    col = lax.broadcasted_iota(jnp.int32, (CHUNK, CHUNK), 1)
    return row, col


def _cumsum_rows(x, reverse=False):
    row, col = _chunk_geometry()
    tri = (col >= row) if reverse else (col <= row)
    return lax.dot_general(tri.astype(F32), x, ((NN), ((), ())), precision=HI, preferred_element_type=F32)


def _rep_sub(x4):
    k = x4.shape[-1]
    return jnp.broadcast_to(x4[:, None, :], (CHUNK // SUB, SUB, k)).reshape(CHUNK, k)


def _gates(q_r, f_r, lb):
    sg = _sigmoid(f_r)
    f = lb + (1.0 - lb) * sg
    sq = _sigmoid(q_r)
    return sg, f, sq, q_r * sq


def _offdiag_terms(b, j):
    c = b[j * SUB + SUB - 1:j * SUB + SUB, :]
    return jnp.exp(jnp.minimum(b - c, 0.0)), jnp.exp(jnp.minimum(c - b, 0.0))


def _store_heads(ref, x):
    for j in range(ref.shape[0]):
        ref[j] = x[:, _head(j)]


def _sub_rows(ref, r):
    rows = [ref[j, pl.ds(r, CHUNK // SUB, stride=SUB), :] for j in range(ref.shape[0])]
    return _rep_sub(jnp.concatenate(rows, axis=1))


def _diag_mask():
    row, col = _chunk_geometry()
    return jnp.logical_and((row // SUB) == (col // SUB), row >= col)


HGRN_HEADS_PER_STEP = 8


def _wide(refs):
    return jnp.concatenate([r[...] for r in refs], axis=1)


def _head(j):
    return slice(j * RNN_HEAD_DIM, (j + 1) * RNN_HEAD_DIM)


def _cat_heads(parts, hs):
    return jnp.concatenate([p[:, hs] for p in parts], axis=1)


def _offdiag_factors(q, k, b):
    rowi = lax.broadcasted_iota(jnp.int32, b.shape, 0)
    qs, ks, ers, ecs = [], [], [], []
    for j in range(CHUNK // SUB - 1):
        e_row, e_col = _offdiag_terms(b, j)
        e_row = jnp.where(rowi >= (j + 1) * SUB, e_row, 0.0)
        e_col = jnp.where((rowi // SUB) == j, e_col, 0.0)
        qs.append(q * e_row)
        ks.append(k * e_col)
        ers.append(e_row)
        ecs.append(e_col)
    return qs, ks, ers, ecs


def hgrn_fwd(proj, attn_n, lb, norm_gain, col0, rw, carry=None):
    t, aw = attn_n.shape
    nh = rw // RNN_HEAD_DIM
    nc = t // CHUNK
    kd = RNN_HEAD_DIM
    cb = col0 // kd
    nsub = CHUNK // SUB
    hp = nh
    assert nh <= HGRN_HEADS_PER_STEP
    w = hp * kd

    def body(*refs):
        q_refs, f_refs, i_refs, g_refs = (refs[i * hp:(i + 1) * hp] for i in range(4))
        lb_ref, ng_ref, an_ref, cat_ref, o_ref, att_ref, st_ref, state, b_ref, k_ref = refs[4 * hp:]
        c = pl.program_id(1)

        @pl.when(c == 0)
        def _():
            state[...] = jnp.zeros_like(state)

        st_ref[...] = state[...]
        q_r, f_r, v, g_r = (_wide(rs) for rs in (q_refs, f_refs, i_refs, g_refs))
        _, f, _, q = _gates(q_r, f_r, lb_ref[...])
        k = 1.0 - f
        b = _cumsum_rows(jnp.log(f))
        _store_heads(b_ref, b)
        _store_heads(k_ref, k)
        qcat, kcat, _, _ = _offdiag_factors(q, k, b)
        row, col = _chunk_geometry()
        same = (row // SUB) == (col // SUB)
        rloc = lax.broadcasted_iota(jnp.int32, (CHUNK, w), 0) % SUB
        diag = [jnp.zeros((CHUNK, CHUNK), F32)] * hp
        for r in range(SUB):
            bs = _sub_rows(b_ref, r)
            ks = _sub_rows(k_ref, r)
            prod = q * jnp.exp(jnp.where(rloc >= r, b - bs, -jnp.inf)) * ks
            place = jnp.logical_and((col % SUB) == r, same)
            diag = [jnp.where(place, jnp.sum(prod[:, _head(j)], axis=-1, keepdims=True), diag[j]) for j in range(hp)]
        b_last = b[CHUNK - 1:CHUNK, :]
        qe = q * jnp.exp(b)
        kdec = k * jnp.exp(b_last - b)
        decay = jnp.exp(b_last)
        outs, normed, states = [], [], []
        for j in range(hp):
            hs = _head(j)
            att = diag[j] + _dot(_cat_heads(qcat, hs), _cat_heads(kcat, hs), NT)
            att_ref[j] = att
            sj = state[j]
            o = _dot(qe[:, hs], sj, NT) + _dot(att, v[:, hs], NN)
            outs.append(o)
            normed.append(o * _rstd(o))
            states.append(sj * decay[:, hs] + _dot(v[:, hs], kdec[:, hs], TN))
        for j in range(hp):
            state[j] = states[j]
        o_ref[...] = jnp.concatenate(outs, axis=1)
        gate = g_r * _sigmoid(g_r)
        cat_ref[:, :aw] = an_ref[...]
        cat_ref[:, aw:] = (jnp.concatenate(normed, axis=1) * jnp.tile(ng_ref[...], (1, hp)) * gate).astype(BF16)

    def col(kidx, j):
        return pl.BlockSpec((CHUNK, kd), lambda hg, c: (c, cb + kidx * nh + hg * hp + j))

    return _call(body, name="hgrn_fwd", grid=(1, nc),
                 in_specs=[col(kidx, j) for kidx in range(4) for j in range(hp)] +
                          [pl.BlockSpec((1, w), lambda hg, c: (0, hg)), pl.BlockSpec((1, kd), lambda hg, c: (0, 0)),
                           pl.BlockSpec((CHUNK, aw), lambda hg, c: (c, 0))],
                 out_specs=[pl.BlockSpec((CHUNK, aw + w), lambda hg, c: (c, 0)),
                            pl.BlockSpec((CHUNK, w), lambda hg, c: (c, hg)),
                            pl.BlockSpec((hp, CHUNK, CHUNK), lambda hg, c: (hg, c, 0)),
                            pl.BlockSpec((None, hp, kd, kd), lambda hg, c: (c, hg, 0, 0))],
                 out_shape=[jax.ShapeDtypeStruct((t, aw + rw), BF16), jax.ShapeDtypeStruct((t, rw), F32),
                            jax.ShapeDtypeStruct((nh, t, CHUNK), F32), jax.ShapeDtypeStruct((nc, nh, kd, kd), F32)],
                 args=(*([proj] * (4 * hp)), lb, norm_gain, attn_n),
                 scratch_shapes=[pltpu.VMEM((hp, kd, kd), F32), pltpu.VMEM((hp, CHUNK, kd), F32),
                                 pltpu.VMEM((hp, CHUNK, kd), F32)],
                 sem=("parallel", "arbitrary"), carry=carry)


def hgrn_bwd(proj, lb, norm_gain, o_all, att_all, st_all, dcat, dq_a, dk_a, dv_a, col0, rw, carry=None):
    t, iw = proj.shape
    aw, kw = dq_a.shape[1], dk_a.shape[1]
    nh = rw // RNN_HEAD_DIM
    nc = t // CHUNK
    kd = RNN_HEAD_DIM
    cb = col0 // kd
    nsub = CHUNK // SUB
    dcb = (dcat.shape[1] - rw) // kd
    hp = nh
    assert nh <= HGRN_HEADS_PER_STEP and dcb % hp == 0 and col0 == aw + 2 * kw and iw == col0 + 4 * rw
    w = hp * kd

    def per_head(x, fn):
        return jnp.concatenate([jnp.broadcast_to(fn(x[:, _head(j)]), (CHUNK, kd)) for j in range(hp)], axis=1)

    def body(*refs):
        q_refs, f_refs, i_refs, g_refs = (refs[i * hp:(i + 1) * hp] for i in range(4))
        (lb_ref, ng_ref, o_ref, att_ref, st0_ref, st1_ref, d_ref, dqa_ref, dka_ref, dva_ref, dp_ref, dlb_ref, dng_ref,
         dstate, b_ref, k_ref, dks_ref) = refs[4 * hp:]
        ci = pl.program_id(1)

        @pl.when(ci == 0)
        def _():
            dstate[...] = jnp.zeros_like(dstate)
            dlb_ref[...] = jnp.zeros_like(dlb_ref)
            dng_ref[...] = jnp.zeros_like(dng_ref)

        lbv = lb_ref[...]
        q_r, f_r, v, g_r = (_wide(rs) for rs in (q_refs, f_refs, i_refs, g_refs))
        sg, f, sq, q = _gates(q_r, f_r, lbv)
        k = 1.0 - f
        b = _cumsum_rows(jnp.log(f))
        _store_heads(b_ref, b)
        _store_heads(k_ref, k)
        row, col = _chunk_geometry()

        o = o_ref[...]
        ng = jnp.tile(ng_ref[...], (1, hp))
        sgg = _sigmoid(g_r)
        gate = g_r * sgg
        d_rnn = d_ref[...]
        r = per_head(o, _rstd)
        oh = o * r
        dp_ref[:, :aw] = dqa_ref[...]
        dp_ref[:, aw:aw + kw] = dka_ref[...].astype(BF16)
        dp_ref[:, aw + kw:col0] = dva_ref[...].astype(BF16)
        dp_ref[:, col0 + 3 * rw:] = (d_rnn * oh * ng * (sgg * (1.0 + g_r * (1.0 - sgg)))).astype(BF16)
        d_on = d_rnn * gate
        dng_rows = jnp.sum(d_on * oh, axis=0, keepdims=True)
        dng = dng_rows[:, _head(0)]
        for j in range(1, hp):
            dng = dng + dng_rows[:, _head(j)]
        dng_ref[...] += dng
        dyg = d_on * ng
        do = r * (dyg - oh * per_head(dyg * oh, lambda x: jnp.mean(x, axis=-1, keepdims=True)))

        b_last = b[CHUNK - 1:CHUNK, :]
        eb = jnp.exp(b)
        tail = jnp.exp(b_last - b)
        kdec = k * tail
        decay = jnp.exp(b_last)
        qe = q * eb
        qcat, kcat, ers, ecs = _offdiag_factors(q, k, b)
        diag_mask = _diag_mask()
        dqs, dks, dvs, dads, gsums, dstates = [], [], [], [], [], []
        for j in range(hp):
            hs = _head(j)
            do_h, v_h, dst = do[:, hs], v[:, hs], dstate[j]
            da = jnp.where(row >= col, _dot(do_h, v_h, NT), 0.0)
            dads.append(jnp.where(diag_mask, da, 0.0))
            dq = _dot(do_h, st0_ref[j], NN) * eb[:, hs]
            dk = _dot(v_h, dst, NN) * tail[:, hs]
            dvs.append(_dot(att_ref[j], do_h, TN) + _dot(kdec[:, hs], dst, NT))
            rq = _dot(da, _cat_heads(kcat, hs), NN)
            rk = _dot(da, _cat_heads(qcat, hs), TN)
            for jj in range(nsub - 1):
                dq = dq + ers[jj][:, hs] * rq[:, _head(jj)]
                dk = dk + ecs[jj][:, hs] * rk[:, _head(jj)]
            dqs.append(dq)
            dks.append(dk)
            gsums.append(jnp.sum(dst * st1_ref[j], axis=0, keepdims=True))
            dstates.append(dst * decay[:, hs] + _dot(do_h, qe[:, hs], TN))
        for j in range(hp):
            dstate[j] = dstates[j]
        dq = jnp.concatenate(dqs, axis=1)
        dk = jnp.concatenate(dks, axis=1)
        rloc = lax.broadcasted_iota(jnp.int32, (CHUNK, w), 0) % SUB
        for rr in range(SUB):
            bs = _sub_rows(b_ref, rr)
            ks = _sub_rows(k_ref, rr)
            e = jnp.exp(jnp.where(rloc >= rr, b - bs, -jnp.inf))
            pick = (col % SUB) == rr
            dacol = jnp.concatenate(
                [jnp.broadcast_to(jnp.sum(jnp.where(pick, dads[j], 0.0), axis=-1, keepdims=True), (CHUNK, kd))
                 for j in range(hp)], axis=1)
            wv = dacol * e
            dq = dq + wv * ks
            sums = jnp.sum((wv * q).reshape(nsub, SUB, w), axis=1)
            for j in range(hp):
                dks_ref[j, pl.ds(rr, nsub, stride=SUB), :] = sums[:, _head(j)]
        dk = dk + jnp.concatenate([dks_ref[j] for j in range(hp)], axis=1)

        dlf = _cumsum_rows(q * dq - k * dk, reverse=True) + jnp.concatenate(gsums, axis=1)
        dfv = dlf / f - dk
        dp_ref[:, col0 + rw:col0 + 2 * rw] = (dfv * (1.0 - lbv) * sg * (1.0 - sg)).astype(BF16)
        dlb_ref[...] += jnp.sum(dfv * (1.0 - sg), axis=0, keepdims=True)
        dp_ref[:, col0:col0 + rw] = (dq * (sq * (1.0 + q_r * (1.0 - sq)))).astype(BF16)
        dp_ref[:, col0 + 2 * rw:col0 + 3 * rw] = jnp.concatenate(dvs, axis=1).astype(BF16)

    def rev(c):
        return nc - 1 - c

    def col_in(kidx, j):
        return pl.BlockSpec((CHUNK, kd), lambda hg, c: (rev(c), cb + kidx * nh + hg * hp + j))

    def rows(width):
        return pl.BlockSpec((CHUNK, width), lambda hg, c: (rev(c), 0))

    return _call(body, name="hgrn_bwd", grid=(1, nc),
                 in_specs=[col_in(kidx, j) for kidx in range(4) for j in range(hp)] +
                          [pl.BlockSpec((1, w), lambda hg, c: (0, hg)), pl.BlockSpec((1, kd), lambda hg, c: (0, 0)),
                           rows(w),
                           pl.BlockSpec((hp, CHUNK, CHUNK), lambda hg, c: (hg, rev(c), 0)),
                           pl.BlockSpec((None, hp, kd, kd), lambda hg, c: (rev(c), hg, 0, 0)),
                           pl.BlockSpec((None, hp, kd, kd),
                                        lambda hg, c: (jnp.minimum(rev(c) + 1, nc - 1), hg, 0, 0)),
                           pl.BlockSpec((CHUNK, w), lambda hg, c: (rev(c), dcb // hp + hg)),
                           rows(aw), rows(kw), rows(kw)],
                 out_specs=[rows(iw),
                            pl.BlockSpec((1, w), lambda hg, c: (0, hg)),
                            pl.BlockSpec((None, 1, kd), lambda hg, c: (hg, 0, 0))],
                 out_shape=[jax.ShapeDtypeStruct((t, iw), BF16), jax.ShapeDtypeStruct((1, rw), F32),
                            jax.ShapeDtypeStruct((1, 1, kd), F32)],
                 args=(*([proj] * (4 * hp)), lb, norm_gain, o_all, att_all, st_all, st_all, dcat, dq_a, dk_a, dv_a),
                 scratch_shapes=[pltpu.VMEM((hp, kd, kd), F32), pltpu.VMEM((hp, CHUNK, kd), F32),
                                 pltpu.VMEM((hp, CHUNK, kd), F32), pltpu.VMEM((hp, CHUNK, kd), F32)],
                 sem=("parallel", "arbitrary"), carry=carry)


def all_gather_slabs(shards):
    n = len(shards)

    def body(*refs):
        ins, outs = refs[:n], refs[n:2 * n]
        send_sems, recv_sems, local_sems = refs[2 * n:]
        x, y, c = _coords()
        me, sibling = (x, y, c), (x, y, 1 - c)
        chips = [(1 - x, y), (x, 1 - y), (1 - x, 1 - y)]

        def copy(a, k, block, to, src=None):
            slab = outs[a].at[_slab_index(block)]
            return pltpu.make_async_remote_copy(
                src_ref=slab if src is None else src, dst_ref=slab,
                send_sem=send_sems.at[a, k], recv_sem=recv_sems.at[a, k],
                device_id=to, device_id_type=MESH)

        mine = [pltpu.make_async_copy(ins[a], outs[a].at[_slab_index(me)], local_sems.at[a]) for a in range(n)]
        for cp in mine:
            cp.start()
        first = []
        for a in range(n):
            first.append(copy(a, 0, me, sibling, src=ins[a]))
            first += [copy(a, 1 + j, me, (*chip, c), src=ins[a]) for j, chip in enumerate(chips)]
        for cp in first:
            cp.start()
        passed = []
        for j, chip in enumerate(chips):
            for a in range(n):
                copy(a, 1 + j, (*chip, c), me).wait_recv()
                fwd = copy(a, 4 + j, (*chip, c), sibling)
                fwd.start()
                passed.append(fwd)
        for a in range(n):
            copy(a, 0, sibling, me).wait_recv()
            for j, chip in enumerate(chips):
                copy(a, 4 + j, (*chip, 1 - c), me).wait_recv()
        for cp in first + passed:
            cp.wait_send()
        for cp in mine:
            cp.wait()

    return pl.pallas_call(
        body, name="all_gather_weights",
        in_specs=[ANY] * n, out_specs=[ANY] * n,
        out_shape=[jax.ShapeDtypeStruct((N_DEV, *s.shape), s.dtype) for s in shards],
        scratch_shapes=[pltpu.SemaphoreType.DMA((n, 7)), pltpu.SemaphoreType.DMA((n, 7)),
                        pltpu.SemaphoreType.DMA((n,))],
        compiler_params=pltpu.CompilerParams(has_side_effects=True),
    )(*shards)


def exchange_halves(name, array, axis):
    return _call(lambda: None, name=name, grid=(), in_specs=[], out_specs=[], out_shape=[], args=(),
                 carry=_scatter_step(array, axis))[1][0]


ADD_BLOCK_ELEMS = 1 << 20
ADAMW_BLOCK_ELEMS = 1 << 19


def add_kept_half(name, kept, got, sel, axis):
    minor = axis == "c"
    pieces, rows, cols = got.shape
    tr = _tile(rows, max(16, ADD_BLOCK_ELEMS // cols), mult=16)

    def body(sel_ref, k_ref, g_ref, o_ref):
        o_ref[...] = (k_ref[...].astype(F32) + g_ref[...].astype(F32)).astype(o_ref.dtype)

    kept_spec = (pl.BlockSpec((None, None, tr, cols), lambda p, i, s: (p, s[0], i, 0)) if minor else
                 pl.BlockSpec((None, None, tr, cols), lambda p, i, s: (s[0], p, i, 0)))
    return pl.pallas_call(
        body, name=name,
        grid_spec=pltpu.PrefetchScalarGridSpec(
            num_scalar_prefetch=1, grid=(pieces, rows // tr),
            in_specs=[kept_spec, pl.BlockSpec((None, tr, cols), lambda p, i, s: (p, i, 0))],
            out_specs=pl.BlockSpec((None, tr, cols), lambda p, i, s: (p, i, 0))),
        out_shape=jax.ShapeDtypeStruct(got.shape, got.dtype),
        compiler_params=_cparams(("parallel", "parallel")),
    )(sel, kept, got)


def _adamw(w, g, m, v):
    m = ADAM_B1 * m + (1.0 - ADAM_B1) * g
    v = ADAM_B2 * v + (1.0 - ADAM_B2) * (g * g)
    m_hat = m / (1.0 - ADAM_B1 ** ADAM_STEP)
    v_hat = v / (1.0 - ADAM_B2 ** ADAM_STEP)
    delta = -ADAM_LR * (m_hat / (jnp.sqrt(v_hat) + ADAM_EPS) + ADAM_WD * w)
    return delta, m, v


def add_adamw(name, kept, got, sel, w, m, v):
    rows, cols = w.shape
    tr = _tile(rows, max(16, ADAMW_BLOCK_ELEMS // cols), mult=16)

    def body(sel_ref, k_ref, g_ref, w_ref, m_ref, v_ref, go_ref, d_ref, mo_ref, vo_ref):
        g = k_ref[...].astype(F32) + g_ref[...].astype(F32)
        go_ref[...] = g
        d_ref[...], mo_ref[...], vo_ref[...] = _adamw(w_ref[...], g, m_ref[...], v_ref[...])

    tile = pl.BlockSpec((tr, cols), lambda i, s: (i, 0))
    return pl.pallas_call(
        body, name=name,
        grid_spec=pltpu.PrefetchScalarGridSpec(
            num_scalar_prefetch=1, grid=(rows // tr,),
            in_specs=[pl.BlockSpec((None, None, tr, cols), lambda i, s: (s[0], 0, i, 0)),
                      pl.BlockSpec((None, tr, cols), lambda i, s: (0, i, 0)), tile, tile, tile],
            out_specs=[tile] * 4),
        out_shape=[jax.ShapeDtypeStruct((rows, cols), F32)] * 4,
        compiler_params=_cparams(("parallel",)),
    )(sel, kept, got, w, m, v)


def small_allreduce_adamw(partial, scale, w, m, v):
    rows = partial.shape[0]

    def body(p_ref, s_ref, w_ref, m_ref, v_ref, g_ref, d_ref, mo_ref, vo_ref, slots, send_sems, recv_sems):
        x, y, c = _coords()
        my_slot = _slab_index((x, y, c))
        slots[my_slot] = p_ref[...]
        copies = []
        for mask in range(1, N_DEV):
            to = tuple(1 - v_ if (mask >> s_) & 1 else v_ for v_, s_ in ((x, 2), (y, 1), (c, 0)))
            copies.append(pltpu.make_async_remote_copy(
                src_ref=p_ref, dst_ref=slots.at[my_slot],
                send_sem=send_sems.at[mask - 1], recv_sem=recv_sems.at[mask - 1],
                device_id=to, device_id_type=MESH))
        for cp in copies:
            cp.start()
        for cp in copies:
            cp.wait()
        total = slots[0]
        for b in range(1, N_DEV):
            total = total + slots[b]
        g = total * s_ref[...]
        g_ref[...] = g
        d_ref[...], mo_ref[...], vo_ref[...] = _adamw(w_ref[...], g, m_ref[...], v_ref[...])

    vm = pl.BlockSpec(memory_space=pltpu.VMEM)
    return pl.pallas_call(
        body, name="small_allreduce_adamw",
        in_specs=[vm] * 5, out_specs=[vm] * 4,
        out_shape=[jax.ShapeDtypeStruct((rows, LANES), F32)] * 4,
        scratch_shapes=[pltpu.VMEM((N_DEV, rows, LANES), F32),
                        pltpu.SemaphoreType.DMA((N_DEV - 1,)), pltpu.SemaphoreType.DMA((N_DEV - 1,))],
        compiler_params=pltpu.CompilerParams(has_side_effects=True),
    )(partial, scale, w, m, v)


_SMALL = ("attn_sinks", "attn_out_gain", "rnn_lb_logits", "rnn_norm_gain", "mix_pre_gain", "mix_post_gain",
          "mlp_pre_gain", "mlp_post_gain")


def _pack(parts):
    rows = []
    for p in parts:
        flat = p.reshape(-1).astype(F32)
        pad = (-flat.shape[0]) % LANES
        rows.append(jnp.pad(flat, (0, pad)).reshape(-1, LANES))
    packed = jnp.concatenate(rows, axis=0)
    pad_rows = (-packed.shape[0]) % 8
    return jnp.pad(packed, ((0, pad_rows), (0, 0)))


def _unpack(packed, shapes):
    out, r = [], 0
    for s in shapes:
        size = math.prod(s)
        nrows = -(-size // LANES)
        out.append(packed[r:r + nrows].reshape(-1)[:size].reshape(s))
        r += nrows
    return out


class _Scatter:
    def __init__(self, tag, grad, sels):
        self.tag, self.sels = tag, sels
        self.shape = grad.shape[1:]
        self.cur = grad.reshape(4, 2, *self.shape)
        self.stage = 0

    def step(self):
        return _scatter_step(self.cur, "cxy"[self.stage])

    def land(self, got, w=None, m=None, v=None):
        axis = "cxy"[self.stage]
        name = "rs_add_%s_%s" % (axis, self.tag)
        sel = self.sels[axis]
        self.stage += 1
        if axis == "y":
            return add_adamw(name, self.cur, got, sel, w, m, v)
        summed = add_kept_half(name, self.cur, got, sel, axis)
        self.cur = summed.reshape(2, summed.shape[0] // 2, *self.shape)
        return None


def kernel(x, w_in, attn_sinks, attn_out_gain, rnn_lb_logits, rnn_norm_gain, w_out, mix_pre_gain, mix_post_gain, mlp_pre_gain, mlp_post_gain, w_up, w_down, loss_target, m_w_in, m_attn_sinks, m_attn_out_gain, m_rnn_lb_logits, m_rnn_norm_gain, m_w_out, m_mix_pre_gain, m_mix_post_gain, m_mlp_pre_gain, m_mlp_post_gain, m_w_up, m_w_down, v_w_in, v_attn_sinks, v_attn_out_gain, v_rnn_lb_logits, v_rnn_norm_gain, v_w_out, v_mix_pre_gain, v_mix_post_gain, v_mlp_pre_gain, v_mlp_post_gain, v_w_up, v_w_down):
    xs, target = x[0], loss_target[0]
    t, d = xs.shape
    aw = d // 2
    rw = d - aw
    col0 = aw + 2 * N_KV_HEADS * HEAD_DIM
    small_w = dict(attn_sinks=attn_sinks, attn_out_gain=attn_out_gain, rnn_lb_logits=rnn_lb_logits,
                   rnn_norm_gain=rnn_norm_gain, mix_pre_gain=mix_pre_gain, mix_post_gain=mix_post_gain,
                   mlp_pre_gain=mlp_pre_gain, mlp_post_gain=mlp_post_gain)
    small_m = dict(attn_sinks=m_attn_sinks, attn_out_gain=m_attn_out_gain, rnn_lb_logits=m_rnn_lb_logits,
                   rnn_norm_gain=m_rnn_norm_gain, mix_pre_gain=m_mix_pre_gain, mix_post_gain=m_mix_post_gain,
                   mlp_pre_gain=m_mlp_pre_gain, mlp_post_gain=m_mlp_post_gain)
    small_v = dict(attn_sinks=v_attn_sinks, attn_out_gain=v_attn_out_gain, rnn_lb_logits=v_rnn_lb_logits,
                   rnn_norm_gain=v_rnn_norm_gain, mix_pre_gain=v_mix_pre_gain, mix_post_gain=v_mix_post_gain,
                   mlp_pre_gain=v_mlp_pre_gain, mlp_post_gain=v_mlp_post_gain)
    cx, cy, cc = _coords()
    sels = {a: jnp.reshape(v_, (1,)).astype(jnp.int32) for a, v_ in (("x", cx), ("y", cy), ("c", cc))}

    w_in_t, m_in_t, v_in_t = w_in[0].T, m_w_in[0].T, v_w_in[0].T
    s_in, s_out, s_up, s_down = (w.astype(BF16) for w in (w_in_t, w_out[0], w_up[0], w_down[0]))
    probs = jax.nn.softmax(rnn_lb_logits.astype(F32), axis=0)
    lb = probs[0:1]

    wint = all_gather_slabs([s_in])[0].reshape(-1, d)
    h1 = pre_norm(xs, mix_pre_gain)
    up_rows = s_up.shape[0]
    proj, (wout_half, wup_part) = mm_nt(
        "in_proj", h1, wint, F32, carry=_merge(_gather_first(s_out), _gather_first(s_up, rows=(0, up_rows // 2))))
    (attn_o, attn_n), (wout, wup_half) = attn_fwd(
        proj, attn_sinks, attn_out_gain, aw,
        carry=_merge(_gather_second(wout_half), _gather_first(s_up, rows=(up_rows // 2, up_rows), into=wup_part)))
    (cat, o_r, att, st), (wup, wdown_half) = hgrn_fwd(
        proj, attn_n, lb, rnn_norm_gain, col0, rw,
        carry=_merge(_gather_second(wup_half), _gather_first(s_down)))
    wout = wout.reshape(-1, d)
    mixed, (wdown,) = mm_nn("out_proj", cat, wout, F32, carry=_gather_second(wdown_half))
    wdown = wdown.reshape(-1, d)
    x1, h2 = mid_fwd(mixed, mix_post_gain, xs, mlp_pre_gain)
    u = up_proj(h2, wup)
    y = down_proj(u, wdown)
    sse, dout, dy, dg_mlppost = loss_bwd(y, mlp_post_gain, x1, target)

    du = down_bwd_act(dy, wdown, u)
    rs_down = _Scatter("down", down_wgrad(u, dy).reshape(N_DEV, -1, d), sels)
    dh2, (got,) = up_bwd_x(du, wup, carry=rs_down.step())
    rs_down.land(got)
    dwup, (got,) = up_wgrad(h2, du, carry=rs_down.step())
    rs_down.land(got)
    rs_up = _Scatter("up", dwup, sels)
    (dx1, dmixed, dg_mlppre, dg_mixpost), (got,) = mid_bwd(dh2, x1, mlp_pre_gain, dout, mixed, mix_post_gain,
                                                          carry=rs_up.step())
    rs_up.land(got)
    dcat = mm_nt("out_bwd_x", dmixed, wout, F32)
    rs_out = _Scatter("out", mm_tn("out_wgrad", cat, dmixed, BF16).reshape(N_DEV, -1, d), sels)
    (dq_a, dk_a, dv_a, dsinks, daog), (got_d, got_o) = attn_bwd(
        proj, attn_sinks, attn_out_gain, attn_o, dcat, aw, carry=_merge(rs_down.step(), rs_out.step()))
    out_down = rs_down.land(got_d, w_down[0], m_w_down[0], v_w_down[0])
    rs_out.land(got_o)
    (dproj, dlb, dng), (got_u, got_o) = hgrn_bwd(
        proj, lb, rnn_norm_gain, o_r, att, st, dcat, dq_a, dk_a, dv_a, col0, rw,
        carry=_merge(rs_up.step(), rs_out.step()))
    rs_up.land(got_u)
    rs_out.land(got_o)
    dwin, (got_u, got_o) = mm_tn("in_wgrad", dproj, h1, BF16, carry=_merge(rs_up.step(), rs_out.step()))
    out_up = rs_up.land(got_u, w_up[0], m_w_up[0], v_w_up[0])
    out_out = rs_out.land(got_o, w_out[0], m_w_out[0], v_w_out[0])
    rs_in = _Scatter("in", dwin.reshape(N_DEV, -1, d), sels)
    rs_in.land(exchange_halves("rs_exchange_c_in", rs_in.cur, "c"))
    dh1, (got,) = mm_nn("in_bwd_x", dproj, wint, F32, tk=MM_K_TILE, carry=rs_in.step())
    rs_in.land(got)
    grad_x, dg_mixpre = first_bwd(dh1, xs, mix_pre_gain, dx1)
    out_in = rs_in.land(exchange_halves("rs_exchange_y_in", rs_in.cur, "y"), w_in_t, m_in_t, v_in_t)
    big_out = [out_in, out_out, out_up, out_down]

    n_heads = attn_sinks.shape[1]
    jac = probs[0] * probs[1]
    partial = _pack([sse, dsinks[0, :n_heads], daog, jnp.stack([dlb[0], dlb[0]]), jnp.sum(dng, axis=0),
                     dg_mixpre, dg_mixpost, dg_mlppre, dg_mlppost])
    ones = [jnp.ones(small_w[k].shape, F32) for k in _SMALL]
    ones[2] = jnp.stack([jac, -jac])
    scale = _pack([jnp.full((1,), 0.5 / d, F32)] + ones)
    zero = jnp.zeros((1,), F32)
    outs = small_allreduce_adamw(partial, scale, _pack([zero] + [small_w[k] for k in _SMALL]),
                                 _pack([zero] + [small_m[k] for k in _SMALL]),
                                 _pack([jnp.ones((1,), F32)] + [small_v[k] for k in _SMALL]))
    shapes = [(1,)] + [small_w[k].shape for k in _SMALL]
    sgrad, sdelta, snm, snv = (_unpack(o, shapes) for o in outs)
    loss = sgrad[0][0]

    def big(i, j):
        o = big_out[i][j]
        return (o.T if i == 0 else o)[None]

    def ordered(j, smalls):
        s = dict(zip(_SMALL, smalls[1:]))
        return [big(0, j), s["attn_sinks"], s["attn_out_gain"], s["rnn_lb_logits"], s["rnn_norm_gain"], big(1, j),
                s["mix_pre_gain"], s["mix_post_gain"], s["mlp_pre_gain"], s["mlp_post_gain"], big(2, j), big(3, j)]

    return (loss, grad_x[None], *ordered(0, sgrad), *ordered(1, sdelta), *ordered(2, snm), *ordered(3, snv))
```

```python
import math

import jax
import jax.numpy as jnp
from jax import lax
from jax.experimental import pallas as pl
from jax.experimental.pallas import tpu as pltpu

F32 = jnp.float32
BF16 = jnp.bfloat16

HEAD_DIM = 64
N_KV_HEADS = 2
BLOCK = 128
RNN_HEAD_DIM = 128
CHUNK = 64
SUB = 16
EPS = 1e-6

ADAM_LR = 0.001
ADAM_B1 = 0.9
ADAM_B2 = 0.999
ADAM_EPS = 1e-08
ADAM_WD = 0.01
ADAM_STEP = 10

N_DEV = 8
LANES = 128
V7X_VMEM_LIMIT = 56 * 1024 * 1024
MESH = pl.DeviceIdType.MESH
HI = lax.Precision.HIGHEST
ANY = pl.BlockSpec(memory_space=pl.ANY)
_AXES = ("x", "y", "c")


def _cparams(sem=None, **kw):
    return pltpu.CompilerParams(dimension_semantics=sem, vmem_limit_bytes=V7X_VMEM_LIMIT, **kw)


def _dot(a, b, dims):
    return lax.dot_general(a.astype(BF16), b.astype(BF16), (dims, ((), ())), preferred_element_type=F32)


NN = ((1,), (0,))
NT = ((1,), (1,))
TN = ((0,), (0,))


def _pick(n, pref):
    t = min(n, pref)
    while n % t:
        t //= 2
    return t


def _tile(n, pref, mult=LANES):
    if n <= pref:
        return n
    t = pref - pref % mult
    while n % t:
        t -= mult
    return t


def _coords():
    return lax.axis_index("x"), lax.axis_index("y"), lax.axis_index("c")


def _slab_index(dev):
    return 4 * dev[0] + 2 * dev[1] + dev[2]


class _Part:
    def __init__(self, operands, landings, aliases, n_sems, plan):
        self.operands, self.landings, self.aliases, self.n_sems, self.plan = operands, landings, aliases, n_sems, plan


def _merge(*parts):
    operands, landings, aliases, plans = [], [], {}, []
    s0 = 0
    for p in parts:
        o0, l0 = len(operands), len(landings)
        aliases.update({o0 + i: l0 + j for i, j in p.aliases.items()})
        plans.append((p.plan, o0, len(p.operands), l0, len(p.landings), s0))
        operands += p.operands
        landings += p.landings
        s0 += p.n_sems

    def plan(ops, lands, sem):
        starts, waits = [], []
        for f, o0, no, l0, nl, off in plans:
            s, w = f(ops[o0:o0 + no], lands[l0:l0 + nl], lambda kind, k, off=off: sem(kind, off + k))
            starts += s
            waits += w
        return starts, waits

    return _Part(operands, landings, aliases, s0, plan)


def _gather_peers(x, y, c):
    return [(x, y, 1 - c), (1 - x, y, c), (x, 1 - y, c), (1 - x, 1 - y, c)]


def _gather_first(shard, rows=None, into=None):
    lo, hi = (0, shard.shape[0]) if rows is None else rows

    def plan(ops, lands, sem):
        x, y, c = _coords()
        me, peers = (x, y, c), _gather_peers(x, y, c)
        src = ops[0].at[pl.ds(lo, hi - lo)]

        def slab(block):
            return lands[0].at[_slab_index(block), pl.ds(lo, hi - lo)]

        def cp(k, block, to):
            return pltpu.make_async_remote_copy(
                src_ref=src, dst_ref=slab(block),
                send_sem=sem(0, k), recv_sem=sem(1, k), device_id=to, device_id_type=MESH)

        local = pltpu.make_async_copy(src, slab(me), sem(2, 0))
        sends = [cp(k, me, to) for k, to in enumerate(peers)]
        recvs = [cp(k, frm, me) for k, frm in enumerate(peers)]
        return ([local.start] + [s.start for s in sends],
                [local.wait] + [s.wait_send for s in sends] + [r.wait_recv for r in recvs])

    landing = jax.ShapeDtypeStruct((N_DEV, *shard.shape), shard.dtype)
    if into is None:
        return _Part([shard], [landing], {}, 4, plan)
    return _Part([shard, into], [landing], {1: 0}, 4, plan)


def _gather_second(gathered):
    def plan(ops, lands, sem):
        x, y, c = _coords()
        sibling = (x, y, 1 - c)
        chips = [(1 - x, y), (x, 1 - y), (1 - x, 1 - y)]

        def cp(k, block):
            slab = lands[0].at[_slab_index(block)]
            return pltpu.make_async_remote_copy(
                src_ref=slab, dst_ref=slab, send_sem=sem(0, k), recv_sem=sem(1, k),
                device_id=sibling, device_id_type=MESH)

        sends = [cp(k, (*chip, c)) for k, chip in enumerate(chips)]
        recvs = [cp(k, (*chip, 1 - c)) for k, chip in enumerate(chips)]
        return [s.start for s in sends], [s.wait_send for s in sends] + [r.wait_recv for r in recvs]

    return _Part([gathered], [jax.ShapeDtypeStruct(gathered.shape, gathered.dtype)], {0: 0}, 3, plan)


def _scatter_step(array, axis):
    minor = axis == "c"
    pieces = array.shape[0] if minor else array.shape[1]

    def plan(ops, lands, sem):
        coords = list(_coords())
        ai = _AXES.index(axis)
        mine = coords[ai]
        peer = list(coords)
        peer[ai] = 1 - mine
        cps = []
        for p in range(pieces):
            src = ops[0].at[p, 1 - mine] if minor else ops[0].at[1 - mine, p]
            cps.append(pltpu.make_async_remote_copy(
                src_ref=src, dst_ref=lands[0].at[p], send_sem=sem(0, p), recv_sem=sem(1, p),
                device_id=tuple(peer), device_id_type=MESH))
        return [cp.start for cp in cps], [cp.wait for cp in cps]

    return _Part([array], [jax.ShapeDtypeStruct((pieces, *array.shape[2:]), array.dtype)], {}, pieces, plan)


def _grid_edges(grid):
    first = last = None
    for ax, n in enumerate(grid):
        p = pl.program_id(ax)
        f, l = p == 0, p == n - 1
        first = f if first is None else jnp.logical_and(first, f)
        last = l if last is None else jnp.logical_and(last, l)
    return first, last


def _call(body, *, name, grid, in_specs, out_specs, out_shape, args, scratch_shapes=(), sem=None, carry=None):
    if carry is None:
        return pl.pallas_call(
            body, name=name, grid=grid, in_specs=list(in_specs), out_specs=list(out_specs),
            out_shape=list(out_shape), scratch_shapes=list(scratch_shapes), compiler_params=_cparams(sem),
        )(*args)
    n_in, n_out, n_scr = len(in_specs), len(out_specs), len(scratch_shapes)
    n_cin, n_cout = len(carry.operands), len(carry.landings)

    def wrapped(*refs):
        ins, cins = refs[:n_in], refs[n_in:n_in + n_cin]
        o0 = n_in + n_cin
        outs, couts = refs[o0:o0 + n_out], refs[o0 + n_out:o0 + n_out + n_cout]
        s0 = o0 + n_out + n_cout
        scr, sems = refs[s0:s0 + n_scr], refs[s0 + n_scr:]
        first, last = _grid_edges(grid)

        def plan():
            return carry.plan(cins, couts, lambda kind, k: sems[kind].at[k])

        def start_all():
            for start in plan()[0]:
                start()

        def wait_all():
            for wait in plan()[1]:
                wait()

        if grid:
            pl.when(first)(start_all)
            body(*ins, *outs, *scr)
            pl.when(last)(wait_all)
        else:
            start_all()
            body(*ins, *outs, *scr)
            wait_all()

    sem_arrays = [pltpu.SemaphoreType.DMA((carry.n_sems,))] * 3
    res = pl.pallas_call(
        wrapped, name=name, grid=grid,
        in_specs=[*in_specs, *[ANY] * n_cin], out_specs=[*out_specs, *[ANY] * n_cout],
        out_shape=[*out_shape, *carry.landings],
        scratch_shapes=[*scratch_shapes, *sem_arrays],
        input_output_aliases={n_in + i: n_out + j for i, j in carry.aliases.items()},
        compiler_params=_cparams(("arbitrary",) * len(grid) if grid else None, has_side_effects=True),
    )(*args, *carry.operands)
    return res[:n_out], res[n_out:]


MM_TILE = 1024
MM_K_TILE = 2048
MXU_COLS = 256


def _matmul(name, a, b, dims, grid, a_spec, b_spec, out_shape, out_spec, epilogue,
            extras=(), extra_specs=(), prologue=None, carry=None):
    nk = grid[2]
    n_extra = len(extras)
    acc_shape = out_spec.block_shape[-2:]

    def lhs(a_ref):
        return a_ref[...] if prologue is None else prologue(a_ref[...])

    def body_one(a_ref, b_ref, *rest):
        epilogue(_dot(lhs(a_ref), b_ref[...], dims), rest[:n_extra], rest[n_extra:])

    def body_acc(a_ref, b_ref, *rest):
        acc = rest[-1]
        k = pl.program_id(2)
        part = _dot(lhs(a_ref), b_ref[...], dims)

        @pl.when(k == 0)
        def _():
            acc[...] = part

        @pl.when(k > 0)
        def _():
            acc[...] += part

        @pl.when(k == nk - 1)
        def _():
            epilogue(acc[...], rest[:n_extra], rest[n_extra:-1])

    res = _call(body_one if nk == 1 else body_acc, name=name, grid=grid,
                in_specs=[a_spec, b_spec, *extra_specs], out_specs=[out_spec], out_shape=[out_shape],
                args=(a, b, *extras), scratch_shapes=[] if nk == 1 else [pltpu.VMEM(acc_shape, F32)],
                sem=("parallel", "parallel", "arbitrary"), carry=carry)
    return res[0] if carry is None else (res[0][0], res[1])


def _store_as(acc, extra_refs, out_refs):
    out_refs[0][...] = acc.astype(out_refs[0].dtype)


def _square(u):
    return u * u


def mm_nn(name, a, b, out_dtype, tk=None, tm=MM_TILE, prologue=None, carry=None):
    (m, kk), n = a.shape, b.shape[1]
    tm, tn = _tile(m, tm), _tile(n, MM_TILE, mult=MXU_COLS)
    tk = kk if tk is None else _tile(kk, tk, mult=MXU_COLS)
    return _matmul(name, a, b, NN, (m // tm, n // tn, kk // tk),
                   pl.BlockSpec((tm, tk), lambda i, j, k: (i, k)),
                   pl.BlockSpec((tk, tn), lambda i, j, k: (k, j)),
                   jax.ShapeDtypeStruct((m, n), out_dtype),
                   pl.BlockSpec((tm, tn), lambda i, j, k: (i, j)), _store_as, prologue=prologue, carry=carry)


def mm_nt(name, a, b, out_dtype, epilogue=_store_as, extras=(), extra_specs=(), carry=None):
    (m, kk), n = a.shape, b.shape[0]
    tm, tn = _tile(m, MM_TILE), _tile(n, MM_TILE, mult=MXU_COLS)
    return _matmul(name, a, b, NT, (m // tm, n // tn, 1),
                   pl.BlockSpec((tm, kk), lambda i, j, k: (i, 0)),
                   pl.BlockSpec((tn, kk), lambda i, j, k: (j, 0)),
                   jax.ShapeDtypeStruct((m, n), out_dtype),
                   pl.BlockSpec((tm, tn), lambda i, j, k: (i, j)), epilogue,
                   extras=extras, extra_specs=extra_specs, carry=carry)


def mm_tn(name, a, b, out_dtype, prologue=None, carry=None):
    (kk, m), n = a.shape, b.shape[1]
    tm, tn, tk = _tile(m, MM_TILE), _tile(n, MM_TILE), _tile(kk, MM_K_TILE)
    return _matmul(name, a, b, TN, (m // tm, n // tn, kk // tk),
                   pl.BlockSpec((tk, tm), lambda i, j, k: (k, i)),
                   pl.BlockSpec((tk, tn), lambda i, j, k: (k, j)),
                   jax.ShapeDtypeStruct((m, n), out_dtype),
                   pl.BlockSpec((tm, tn), lambda i, j, k: (i, j)), _store_as, prologue=prologue, carry=carry)


def up_proj(h2, wup_slabs):
    (m, kk), (_, _, ns) = h2.shape, wup_slabs.shape
    tm, tn = _tile(m, MM_TILE), _tile(ns, MM_TILE)
    r = ns // tn
    n = N_DEV * ns

    def epi(acc, extra_refs, out_refs):
        out_refs[0][...] = jnp.maximum(acc, 0.0).astype(BF16)

    return _matmul("up_proj", h2, wup_slabs, NN, (m // tm, n // tn, 1),
                   pl.BlockSpec((tm, kk), lambda i, j, k: (i, 0)),
                   pl.BlockSpec((None, kk, tn), lambda i, j, k: (j // r, 0, j % r)),
                   jax.ShapeDtypeStruct((m, n), BF16),
                   pl.BlockSpec((tm, tn), lambda i, j, k: (i, j)), epi)


def down_proj(u, wdown):
    return mm_nn("down_proj", u, wdown, F32, tk=MM_K_TILE, prologue=_square)


def down_bwd_act(dy, wdown, u):
    tm, tn = _tile(dy.shape[0], MM_TILE), _tile(wdown.shape[0], MM_TILE)

    def epi(acc, extra_refs, out_refs):
        out_refs[0][...] = (acc * (2.0 * extra_refs[0][...].astype(F32))).astype(BF16)

    return mm_nt("down_bwd_act", dy, wdown, BF16, epilogue=epi, extras=(u,),
                 extra_specs=(pl.BlockSpec((tm, tn), lambda i, j, k: (i, j)),))


def down_wgrad(u, dy):
    return mm_tn("down_wgrad", u, dy, BF16, prologue=_square)


def up_bwd_x(du, wup_slabs, carry=None):
    (m, kk), (_, n, ns) = du.shape, wup_slabs.shape
    tm, tk = _tile(m, MM_TILE), _tile(ns, MM_TILE)
    r = ns // tk
    return _matmul("up_bwd_x", du, wup_slabs, NT, (m // tm, 1, kk // tk),
                   pl.BlockSpec((tm, tk), lambda i, j, k: (i, k)),
                   pl.BlockSpec((None, n, tk), lambda i, j, k: (k // r, 0, k % r)),
                   jax.ShapeDtypeStruct((m, n), F32),
                   pl.BlockSpec((tm, n), lambda i, j, k: (i, 0)), _store_as, carry=carry)


def up_wgrad(h2, du, carry=None):
    (kk, m), n = h2.shape, du.shape[1]
    ns = n // N_DEV
    tm, tn, tk = _tile(m, MM_TILE), _tile(ns, MM_TILE), _tile(kk, MM_K_TILE)
    r = ns // tn
    return _matmul("up_wgrad", h2, du, TN, (m // tm, n // tn, kk // tk),
                   pl.BlockSpec((tk, tm), lambda i, j, k: (k, i)),
                   pl.BlockSpec((tk, tn), lambda i, j, k: (k, j)),
                   jax.ShapeDtypeStruct((N_DEV, m, ns), BF16),
                   pl.BlockSpec((None, tm, tn), lambda i, j, k: (j // r, i, j % r)), _store_as, carry=carry)


def _rstd(x):
    return lax.rsqrt(jnp.mean(x * x, axis=-1, keepdims=True) + EPS)


def _norm_bwd(x, g, dy):
    r = _rstd(x)
    xh = x * r
    dyg = dy * g
    dx = r * (dyg - xh * jnp.mean(dyg * xh, axis=-1, keepdims=True))
    return dx, jnp.sum(dy * xh, axis=0, keepdims=True)


def _row_spec(tr, d):
    return pl.BlockSpec((tr, d), lambda i: (i, 0))


def _vec_spec(d):
    return pl.BlockSpec((1, d), lambda i: (0, 0))


def _accum(ref, val):
    @pl.when(pl.program_id(0) == 0)
    def _():
        ref[...] = jnp.zeros_like(ref)

    ref[...] += val


def pre_norm(x, g, carry=None, tr=256):
    t, d = x.shape
    tr = _pick(t, tr)

    def body(x_ref, g_ref, h_ref):
        xx = x_ref[...]
        h_ref[...] = (xx * _rstd(xx) * g_ref[...]).astype(BF16)

    return _call(body, name="pre_norm", grid=(t // tr,),
                 in_specs=[_row_spec(tr, d), _vec_spec(d)], out_specs=[_row_spec(tr, d)],
                 out_shape=[jax.ShapeDtypeStruct((t, d), BF16)], args=(x, g), sem=("parallel",), carry=carry)


def mid_fwd(mixed, g_post, x, g_pre2, tr=256):
    t, d = x.shape
    tr = _pick(t, tr)

    def body(m_ref, gp_ref, x_ref, g2_ref, x1_ref, h2_ref):
        mm = m_ref[...]
        x1 = x_ref[...] + mm * _rstd(mm) * gp_ref[...]
        x1_ref[...] = x1
        h2_ref[...] = (x1 * _rstd(x1) * g2_ref[...]).astype(BF16)

    return _call(body, name="mid_fwd", grid=(t // tr,),
                 in_specs=[_row_spec(tr, d), _vec_spec(d), _row_spec(tr, d), _vec_spec(d)],
                 out_specs=[_row_spec(tr, d), _row_spec(tr, d)],
                 out_shape=[jax.ShapeDtypeStruct((t, d), F32), jax.ShapeDtypeStruct((t, d), BF16)],
                 args=(mixed, g_post, x, g_pre2), sem=("parallel",))


def loss_bwd(y, g_post2, x1, target, tr=256):
    t, d = y.shape
    tr = _pick(t, tr)

    def body(y_ref, g_ref, x1_ref, t_ref, sse_ref, dout_ref, dy_ref, dg_ref):
        yy = y_ref[...]
        g = g_ref[...]
        err = x1_ref[...] + yy * _rstd(yy) * g - t_ref[...]
        _accum(sse_ref, jnp.sum(jnp.sum(err * err, axis=1, keepdims=True), axis=0, keepdims=True))
        dout = err * (1.0 / d)
        dout_ref[...] = dout
        dy, dg = _norm_bwd(yy, g, dout)
        dy_ref[...] = dy.astype(BF16)
        _accum(dg_ref, dg)

    return _call(body, name="loss_bwd", grid=(t // tr,),
                 in_specs=[_row_spec(tr, d), _vec_spec(d), _row_spec(tr, d), _row_spec(tr, d)],
                 out_specs=[pl.BlockSpec((1, 1), lambda i: (0, 0)), _row_spec(tr, d), _row_spec(tr, d), _vec_spec(d)],
                 out_shape=[jax.ShapeDtypeStruct((1, 1), F32), jax.ShapeDtypeStruct((t, d), F32),
                            jax.ShapeDtypeStruct((t, d), BF16), jax.ShapeDtypeStruct((1, d), F32)],
                 args=(y, g_post2, x1, target), sem=("arbitrary",))


def mid_bwd(dh2, x1, g_pre2, dout, mixed, g_post, carry=None, tr=256):
    t, d = x1.shape
    tr = _pick(t, tr)

    def body(dh_ref, x1_ref, g2_ref, do_ref, m_ref, gp_ref, dx1_ref, dm_ref, dg2_ref, dgp_ref):
        d1, dg2 = _norm_bwd(x1_ref[...], g2_ref[...], dh_ref[...])
        dx1 = do_ref[...] + d1
        dx1_ref[...] = dx1
        dm, dgp = _norm_bwd(m_ref[...], gp_ref[...], dx1)
        dm_ref[...] = dm.astype(BF16)
        _accum(dg2_ref, dg2)
        _accum(dgp_ref, dgp)

    return _call(body, name="mid_bwd", grid=(t // tr,),
                 in_specs=[_row_spec(tr, d), _row_spec(tr, d), _vec_spec(d), _row_spec(tr, d), _row_spec(tr, d),
                           _vec_spec(d)],
                 out_specs=[_row_spec(tr, d), _row_spec(tr, d), _vec_spec(d), _vec_spec(d)],
                 out_shape=[jax.ShapeDtypeStruct((t, d), F32), jax.ShapeDtypeStruct((t, d), BF16),
                            jax.ShapeDtypeStruct((1, d), F32), jax.ShapeDtypeStruct((1, d), F32)],
                 args=(dh2, x1, g_pre2, dout, mixed, g_post), sem=("arbitrary",), carry=carry)


def first_bwd(dh1, x, g_pre, dx1, carry=None, tr=256):
    t, d = x.shape
    tr = _pick(t, tr)

    def body(dh_ref, x_ref, g_ref, dx1_ref, gx_ref, dg_ref):
        d0, dg = _norm_bwd(x_ref[...], g_ref[...], dh_ref[...])
        gx_ref[...] = dx1_ref[...] + d0
        _accum(dg_ref, dg)

    return _call(body, name="first_bwd", grid=(t // tr,),
                 in_specs=[_row_spec(tr, d), _row_spec(tr, d), _vec_spec(d), _row_spec(tr, d)],
                 out_specs=[_row_spec(tr, d), _vec_spec(d)],
                 out_shape=[jax.ShapeDtypeStruct((t, d), F32), jax.ShapeDtypeStruct((1, d), F32)],
                 args=(dh1, x, g_pre, dx1), sem=("arbitrary",), carry=carry)


def _attn_geometry(has_prev):
    r = lax.broadcasted_iota(jnp.int32, (BLOCK, 2 * BLOCK), 0)
    c = lax.broadcasted_iota(jnp.int32, (BLOCK, 2 * BLOCK), 1)
    dist = r + BLOCK - c
    valid = jnp.logical_and(jnp.logical_and(dist >= 0, dist < BLOCK), jnp.logical_or(c >= BLOCK, has_prev))
    return dist.astype(F32), valid


def _stack_pairs(x, g, pairs):
    base = g * pairs * LANES
    return jnp.concatenate([x[:, base + p * LANES:base + (p + 1) * LANES] for p in range(pairs)], axis=0)


def _unstack_pairs(xs, pairs):
    return jnp.concatenate([xs[p * BLOCK:(p + 1) * BLOCK, :] for p in range(pairs)], axis=1)


def _to_half(x, g, odd):
    lane = lax.broadcasted_iota(jnp.int32, x.shape, 1)
    y = x if (g == 1) == odd else pltpu.roll(x, HEAD_DIM, axis=1)
    return jnp.where((lane >= HEAD_DIM) == odd, y, 0.0)


def _from_halves(even, odd, g):
    lane = lax.broadcasted_iota(jnp.int32, even.shape, 1)
    if g == 0:
        return jnp.where(lane < HEAD_DIM, even + pltpu.roll(odd, HEAD_DIM, axis=1), 0.0)
    return jnp.where(lane >= HEAD_DIM, pltpu.roll(even, HEAD_DIM, axis=1) + odd, 0.0)


_PARITIES = [(g, odd) for g in range(N_KV_HEADS) for odd in (False, True)]


def _softmax_sink(s, sink_ref, g, odd, group, n_heads, geo):
    dist, valid = geo
    pairs = group // 2
    heads = [g * group + 2 * p + int(odd) for p in range(pairs)]
    bias = jnp.concatenate([(2.0 ** (-8.0 * (h + 1) / n_heads)) * dist for h in heads], axis=0)
    sink = jnp.concatenate([jnp.full((BLOCK, 1), sink_ref[0, h], F32) for h in heads], axis=0)
    s = jnp.where(jnp.concatenate([valid] * pairs, axis=0), s - bias, -jnp.inf)
    m = jnp.maximum(jnp.max(s, axis=-1, keepdims=True), sink)
    p = jnp.exp(s - m)
    p_sink = jnp.exp(sink - m)
    inv = 1.0 / (jnp.sum(p, axis=-1, keepdims=True) + p_sink)
    return p * inv, p_sink * inv


def attn_fwd(proj, sinks, gain, aw, carry=None):
    t = proj.shape[0]
    kw = N_KV_HEADS * HEAD_DIM
    n_heads = aw // HEAD_DIM
    group = n_heads // N_KV_HEADS
    pairs = group // 2
    assert kw == LANES and group % 2 == 0
    nb = t // BLOCK
    scale = HEAD_DIM ** -0.5

    def body(sink_ref, q_ref, k_ref, v_ref, g_ref, o_ref, on_ref):
        n = pl.program_id(0)
        cur = pl.multiple_of(n * BLOCK, BLOCK)
        prev = pl.multiple_of(jnp.maximum(n - 1, 0) * BLOCK, BLOCK)
        geo = _attn_geometry(n > 0)
        kcat = jnp.concatenate([k_ref[pl.ds(prev, BLOCK), :], k_ref[pl.ds(cur, BLOCK), :]], axis=0)
        vcat = jnp.concatenate([v_ref[pl.ds(prev, BLOCK), :], v_ref[pl.ds(cur, BLOCK), :]], axis=0)
        q = q_ref[...] * scale
        qs = [_stack_pairs(q, g, pairs) for g in range(N_KV_HEADS)]
        scores = [_dot(qs[g], _to_half(kcat, g, odd), NT) for g, odd in _PARITIES]
        probs = [_softmax_sink(s, sink_ref, g, odd, group, n_heads, geo)[0] for s, (g, odd) in zip(scores, _PARITIES)]
        outs = [_dot(p, _to_half(vcat, g, odd), NN) for p, (g, odd) in zip(probs, _PARITIES)]
        o = jnp.concatenate([_unstack_pairs(outs[2 * g] + outs[2 * g + 1], pairs) for g in range(N_KV_HEADS)], axis=1)
        o_ref[...] = o
        on_ref[...] = (o * _rstd(o) * g_ref[...]).astype(BF16)

    return _call(body, name="attn_fwd", grid=(nb,),
                 in_specs=[pl.BlockSpec(memory_space=pltpu.SMEM),
                           pl.BlockSpec((BLOCK, aw), lambda n: (n, 0)),
                           pl.BlockSpec((t, kw), lambda n: (0, aw // kw)),
                           pl.BlockSpec((t, kw), lambda n: (0, aw // kw + 1)),
                           pl.BlockSpec((1, aw), lambda n: (0, 0))],
                 out_specs=[pl.BlockSpec((BLOCK, aw), lambda n: (n, 0)), pl.BlockSpec((BLOCK, aw), lambda n: (n, 0))],
                 out_shape=[jax.ShapeDtypeStruct((t, aw), F32), jax.ShapeDtypeStruct((t, aw), BF16)],
                 args=(sinks, proj, proj, proj, gain), sem=("parallel",), carry=carry)


def attn_bwd(proj, sinks, gain, attn_o, dcat, aw, carry=None):
    t = proj.shape[0]
    kw = N_KV_HEADS * HEAD_DIM
    n_heads = aw // HEAD_DIM
    group = n_heads // N_KV_HEADS
    pairs = group // 2
    assert kw == LANES and group % 2 == 0
    nb = t // BLOCK
    scale = HEAD_DIM ** -0.5

    def body(sink_ref, q_ref, k_ref, v_ref, g_ref, o_ref, dn_ref, dq_ref, dk_ref, dv_ref, dsink_ref, dg_ref):
        n = pl.program_id(0)
        cur = pl.multiple_of(n * BLOCK, BLOCK)
        prev = pl.multiple_of(jnp.maximum(n - 1, 0) * BLOCK, BLOCK)
        geo = _attn_geometry(n > 0)

        @pl.when(n == 0)
        def _():
            dk_ref[...] = jnp.zeros_like(dk_ref)
            dv_ref[...] = jnp.zeros_like(dv_ref)
            dsink_ref[...] = jnp.zeros_like(dsink_ref)

        o = o_ref[...]
        do_all, dg = _norm_bwd(o, g_ref[...], dn_ref[...])
        _accum(dg_ref, dg)
        kcat = jnp.concatenate([k_ref[pl.ds(prev, BLOCK), :], k_ref[pl.ds(cur, BLOCK), :]], axis=0)
        vcat = jnp.concatenate([v_ref[pl.ds(prev, BLOCK), :], v_ref[pl.ds(cur, BLOCK), :]], axis=0)
        q = q_ref[...] * scale
        lane = lax.broadcasted_iota(jnp.int32, (1, LANES), 1)
        lane_s = lax.broadcasted_iota(jnp.int32, (pairs * BLOCK, LANES), 1)
        qs = [_stack_pairs(q, g, pairs) for g in range(N_KV_HEADS)]
        dos = [_stack_pairs(do_all, g, pairs) for g in range(N_KV_HEADS)]
        kxs = [_to_half(kcat, g, odd) for g, odd in _PARITIES]
        scores = [_dot(qs[g], kx, NT) for kx, (g, odd) in zip(kxs, _PARITIES)]
        dps = [_dot(dos[g], _to_half(vcat, g, odd), NT) for g, odd in _PARITIES]
        deltas = []
        for g in range(N_KV_HEADS):
            prod = dos[g] * _stack_pairs(o, g, pairs)
            delta_even = jnp.sum(jnp.where(lane_s < HEAD_DIM, prod, 0.0), axis=-1, keepdims=True)
            deltas += [delta_even, jnp.sum(prod, axis=-1, keepdims=True) - delta_even]
        dsink = jnp.zeros((1, LANES), F32)
        ps, dss = [], []
        for i, (g, odd) in enumerate(_PARITIES):
            p, p_sink = _softmax_sink(scores[i], sink_ref, g, odd, group, n_heads, geo)
            ps.append(p)
            dss.append(p * (dps[i] - deltas[i]))
            sink_rows = p_sink * deltas[i]
            for pr in range(pairs):
                h = g * group + 2 * pr + int(odd)
                dsink = dsink + jnp.where(
                    lane == h, -jnp.sum(sink_rows[pr * BLOCK:(pr + 1) * BLOCK], axis=0, keepdims=True), 0.0)
        dq_pairs = [_dot(ds, kx, NN) for ds, kx in zip(dss, kxs)]
        dk_halves = [_dot(ds, qs[g], TN) for ds, (g, odd) in zip(dss, _PARITIES)]
        dv_halves = [_dot(p, dos[g], TN) for p, (g, odd) in zip(ps, _PARITIES)]
        dq_ref[...] = jnp.concatenate(
            [_unstack_pairs((dq_pairs[2 * g] + dq_pairs[2 * g + 1]) * scale, pairs) for g in range(N_KV_HEADS)],
            axis=1).astype(BF16)
        dk_upd = _from_halves(dk_halves[0], dk_halves[1], 0) + _from_halves(dk_halves[2], dk_halves[3], 1)
        dv_upd = _from_halves(dv_halves[0], dv_halves[1], 0) + _from_halves(dv_halves[2], dv_halves[3], 1)
        dk_ref[pl.ds(prev, BLOCK), :] += dk_upd[:BLOCK]
        dv_ref[pl.ds(prev, BLOCK), :] += dv_upd[:BLOCK]
        dk_ref[pl.ds(cur, BLOCK), :] += dk_upd[BLOCK:]
        dv_ref[pl.ds(cur, BLOCK), :] += dv_upd[BLOCK:]
        dsink_ref[...] += dsink

    return _call(body, name="attn_bwd", grid=(nb,),
                 in_specs=[pl.BlockSpec(memory_space=pltpu.SMEM),
                           pl.BlockSpec((BLOCK, aw), lambda n: (n, 0)),
                           pl.BlockSpec((t, kw), lambda n: (0, aw // kw)),
                           pl.BlockSpec((t, kw), lambda n: (0, aw // kw + 1)),
                           pl.BlockSpec((1, aw), lambda n: (0, 0)),
                           pl.BlockSpec((BLOCK, aw), lambda n: (n, 0)),
                           pl.BlockSpec((BLOCK, aw), lambda n: (n, 0))],
                 out_specs=[pl.BlockSpec((BLOCK, aw), lambda n: (n, 0)),
                            pl.BlockSpec((t, kw), lambda n: (0, 0)), pl.BlockSpec((t, kw), lambda n: (0, 0)),
                            pl.BlockSpec((1, LANES), lambda n: (0, 0)), pl.BlockSpec((1, aw), lambda n: (0, 0))],
                 out_shape=[jax.ShapeDtypeStruct((t, aw), BF16), jax.ShapeDtypeStruct((t, kw), F32),
                            jax.ShapeDtypeStruct((t, kw), F32), jax.ShapeDtypeStruct((1, LANES), F32),
                            jax.ShapeDtypeStruct((1, aw), F32)],
                 args=(sinks, proj, proj, proj, gain, attn_o, dcat), sem=("arbitrary",), carry=carry)


def _sigmoid(x):
    return 0.5 * jnp.tanh(0.5 * x) + 0.5


def _chunk_geometry():
    row = lax.broadcasted_iota(jnp.int32, (CHUNK, CHUNK), 0)
    col = lax.broadcasted_iota(jnp.int32, (CHUNK, CHUNK), 1)
    return row, col


def _cumsum_rows(x, reverse=False):
    row, col = _chunk_geometry()
    tri = (col >= row) if reverse else (col <= row)
    return lax.dot_general(tri.astype(F32), x, ((NN), ((), ())), precision=HI, preferred_element_type=F32)


def _rep_sub(x4):
    k = x4.shape[-1]
    return jnp.broadcast_to(x4[:, None, :], (CHUNK // SUB, SUB, k)).reshape(CHUNK, k)


def _gates(q_r, f_r, lb):
    sg = _sigmoid(f_r)
    f = lb + (1.0 - lb) * sg
    sq = _sigmoid(q_r)
    return sg, f, sq, q_r * sq


def _offdiag_terms(b, j):
    c = b[j * SUB + SUB - 1:j * SUB + SUB, :]
    return jnp.exp(jnp.minimum(b - c, 0.0)), jnp.exp(jnp.minimum(c - b, 0.0))


def _store_heads(ref, x):
    for j in range(ref.shape[0]):
        ref[j] = x[:, _head(j)]


def _sub_rows(ref, r):
    rows = [ref[j, pl.ds(r, CHUNK // SUB, stride=SUB), :] for j in range(ref.shape[0])]
    return _rep_sub(jnp.concatenate(rows, axis=1))


def _diag_mask():
    row, col = _chunk_geometry()
    return jnp.logical_and((row // SUB) == (col // SUB), row >= col)


HGRN_HEADS_PER_STEP = 8


def _wide(refs):
    return jnp.concatenate([r[...] for r in refs], axis=1)


def _head(j):
    return slice(j * RNN_HEAD_DIM, (j + 1) * RNN_HEAD_DIM)


def _cat_heads(parts, hs):
    return jnp.concatenate([p[:, hs] for p in parts], axis=1)


def _offdiag_factors(q, k, b):
    rowi = lax.broadcasted_iota(jnp.int32, b.shape, 0)
    qs, ks, ers, ecs = [], [], [], []
    for j in range(CHUNK // SUB - 1):
        e_row, e_col = _offdiag_terms(b, j)
        e_row = jnp.where(rowi >= (j + 1) * SUB, e_row, 0.0)
        e_col = jnp.where((rowi // SUB) == j, e_col, 0.0)
        qs.append(q * e_row)
        ks.append(k * e_col)
        ers.append(e_row)
        ecs.append(e_col)
    return qs, ks, ers, ecs


def hgrn_fwd(proj, attn_n, lb, norm_gain, col0, rw, carry=None):
    t, aw = attn_n.shape
    nh = rw // RNN_HEAD_DIM
    nc = t // CHUNK
    kd = RNN_HEAD_DIM
    cb = col0 // kd
    nsub = CHUNK // SUB
    hp = nh
    assert nh <= HGRN_HEADS_PER_STEP
    w = hp * kd

    def body(*refs):
        q_refs, f_refs, i_refs, g_refs = (refs[i * hp:(i + 1) * hp] for i in range(4))
        lb_ref, ng_ref, an_ref, cat_ref, o_ref, att_ref, st_ref, state, b_ref, k_ref = refs[4 * hp:]
        c = pl.program_id(1)

        @pl.when(c == 0)
        def _():
            state[...] = jnp.zeros_like(state)

        st_ref[...] = state[...]
        q_r, f_r, v, g_r = (_wide(rs) for rs in (q_refs, f_refs, i_refs, g_refs))
        _, f, _, q = _gates(q_r, f_r, lb_ref[...])
        k = 1.0 - f
        b = _cumsum_rows(jnp.log(f))
        _store_heads(b_ref, b)
        _store_heads(k_ref, k)
        qcat, kcat, _, _ = _offdiag_factors(q, k, b)
        row, col = _chunk_geometry()
        same = (row // SUB) == (col // SUB)
        rloc = lax.broadcasted_iota(jnp.int32, (CHUNK, w), 0) % SUB
        diag = [jnp.zeros((CHUNK, CHUNK), F32)] * hp
        for r in range(SUB):
            bs = _sub_rows(b_ref, r)
            ks = _sub_rows(k_ref, r)
            prod = q * jnp.exp(jnp.where(rloc >= r, b - bs, -jnp.inf)) * ks
            place = jnp.logical_and((col % SUB) == r, same)
            diag = [jnp.where(place, jnp.sum(prod[:, _head(j)], axis=-1, keepdims=True), diag[j]) for j in range(hp)]
        b_last = b[CHUNK - 1:CHUNK, :]
        qe = q * jnp.exp(b)
        kdec = k * jnp.exp(b_last - b)
        decay = jnp.exp(b_last)
        outs, normed, states = [], [], []
        for j in range(hp):
            hs = _head(j)
            att = diag[j] + _dot(_cat_heads(qcat, hs), _cat_heads(kcat, hs), NT)
            att_ref[j] = att
            sj = state[j]
            o = _dot(qe[:, hs], sj, NT) + _dot(att, v[:, hs], NN)
            outs.append(o)
            normed.append(o * _rstd(o))
            states.append(sj * decay[:, hs] + _dot(v[:, hs], kdec[:, hs], TN))
        for j in range(hp):
            state[j] = states[j]
        o_ref[...] = jnp.concatenate(outs, axis=1)
        gate = g_r * _sigmoid(g_r)
        cat_ref[:, :aw] = an_ref[...]
        cat_ref[:, aw:] = (jnp.concatenate(normed, axis=1) * jnp.tile(ng_ref[...], (1, hp)) * gate).astype(BF16)

    def col(kidx, j):
        return pl.BlockSpec((CHUNK, kd), lambda hg, c: (c, cb + kidx * nh + hg * hp + j))

    return _call(body, name="hgrn_fwd", grid=(1, nc),
                 in_specs=[col(kidx, j) for kidx in range(4) for j in range(hp)] +
                          [pl.BlockSpec((1, w), lambda hg, c: (0, hg)), pl.BlockSpec((1, kd), lambda hg, c: (0, 0)),
                           pl.BlockSpec((CHUNK, aw), lambda hg, c: (c, 0))],
                 out_specs=[pl.BlockSpec((CHUNK, aw + w), lambda hg, c: (c, 0)),
                            pl.BlockSpec((CHUNK, w), lambda hg, c: (c, hg)),
                            pl.BlockSpec((hp, CHUNK, CHUNK), lambda hg, c: (hg, c, 0)),
                            pl.BlockSpec((None, hp, kd, kd), lambda hg, c: (c, hg, 0, 0))],
                 out_shape=[jax.ShapeDtypeStruct((t, aw + rw), BF16), jax.ShapeDtypeStruct((t, rw), F32),
                            jax.ShapeDtypeStruct((nh, t, CHUNK), F32), jax.ShapeDtypeStruct((nc, nh, kd, kd), F32)],
                 args=(*([proj] * (4 * hp)), lb, norm_gain, attn_n),
                 scratch_shapes=[pltpu.VMEM((hp, kd, kd), F32), pltpu.VMEM((hp, CHUNK, kd), F32),
                                 pltpu.VMEM((hp, CHUNK, kd), F32)],
                 sem=("parallel", "arbitrary"), carry=carry)


def hgrn_bwd(proj, lb, norm_gain, o_all, att_all, st_all, dcat, dq_a, dk_a, dv_a, col0, rw, carry=None):
    t, iw = proj.shape
    aw, kw = dq_a.shape[1], dk_a.shape[1]
    nh = rw // RNN_HEAD_DIM
    nc = t // CHUNK
    kd = RNN_HEAD_DIM
    cb = col0 // kd
    nsub = CHUNK // SUB
    dcb = (dcat.shape[1] - rw) // kd
    hp = nh
    assert nh <= HGRN_HEADS_PER_STEP and dcb % hp == 0 and col0 == aw + 2 * kw and iw == col0 + 4 * rw
    w = hp * kd

    def per_head(x, fn):
        return jnp.concatenate([jnp.broadcast_to(fn(x[:, _head(j)]), (CHUNK, kd)) for j in range(hp)], axis=1)

    def body(*refs):
        q_refs, f_refs, i_refs, g_refs = (refs[i * hp:(i + 1) * hp] for i in range(4))
        (lb_ref, ng_ref, o_ref, att_ref, st0_ref, st1_ref, d_ref, dqa_ref, dka_ref, dva_ref, dp_ref, dlb_ref, dng_ref,
         dstate, b_ref, k_ref, dks_ref) = refs[4 * hp:]
        ci = pl.program_id(1)

        @pl.when(ci == 0)
        def _():
            dstate[...] = jnp.zeros_like(dstate)
            dlb_ref[...] = jnp.zeros_like(dlb_ref)
            dng_ref[...] = jnp.zeros_like(dng_ref)

        lbv = lb_ref[...]
        q_r, f_r, v, g_r = (_wide(rs) for rs in (q_refs, f_refs, i_refs, g_refs))
        sg, f, sq, q = _gates(q_r, f_r, lbv)
        k = 1.0 - f
        b = _cumsum_rows(jnp.log(f))
        _store_heads(b_ref, b)
        _store_heads(k_ref, k)
        row, col = _chunk_geometry()

        o = o_ref[...]
        ng = jnp.tile(ng_ref[...], (1, hp))
        sgg = _sigmoid(g_r)
        gate = g_r * sgg
        d_rnn = d_ref[...]
        r = per_head(o, _rstd)
        oh = o * r
        dp_ref[:, :aw] = dqa_ref[...]
        dp_ref[:, aw:aw + kw] = dka_ref[...].astype(BF16)
        dp_ref[:, aw + kw:col0] = dva_ref[...].astype(BF16)
        dp_ref[:, col0 + 3 * rw:] = (d_rnn * oh * ng * (sgg * (1.0 + g_r * (1.0 - sgg)))).astype(BF16)
        d_on = d_rnn * gate
        dng_rows = jnp.sum(d_on * oh, axis=0, keepdims=True)
        dng = dng_rows[:, _head(0)]
        for j in range(1, hp):
            dng = dng + dng_rows[:, _head(j)]
        dng_ref[...] += dng
        dyg = d_on * ng
        do = r * (dyg - oh * per_head(dyg * oh, lambda x: jnp.mean(x, axis=-1, keepdims=True)))

        b_last = b[CHUNK - 1:CHUNK, :]
        eb = jnp.exp(b)
        tail = jnp.exp(b_last - b)
        kdec = k * tail
        decay = jnp.exp(b_last)
        qe = q * eb
        qcat, kcat, ers, ecs = _offdiag_factors(q, k, b)
        diag_mask = _diag_mask()
        dqs, dks, dvs, dads, gsums, dstates = [], [], [], [], [], []
        for j in range(hp):
            hs = _head(j)
            do_h, v_h, dst = do[:, hs], v[:, hs], dstate[j]
            da = jnp.where(row >= col, _dot(do_h, v_h, NT), 0.0)
            dads.append(jnp.where(diag_mask, da, 0.0))
            dq = _dot(do_h, st0_ref[j], NN) * eb[:, hs]
            dk = _dot(v_h, dst, NN) * tail[:, hs]
            dvs.append(_dot(att_ref[j], do_h, TN) + _dot(kdec[:, hs], dst, NT))
            rq = _dot(da, _cat_heads(kcat, hs), NN)
            rk = _dot(da, _cat_heads(qcat, hs), TN)
            for jj in range(nsub - 1):
                dq = dq + ers[jj][:, hs] * rq[:, _head(jj)]
                dk = dk + ecs[jj][:, hs] * rk[:, _head(jj)]
            dqs.append(dq)
            dks.append(dk)
            gsums.append(jnp.sum(dst * st1_ref[j], axis=0, keepdims=True))
            dstates.append(dst * decay[:, hs] + _dot(do_h, qe[:, hs], TN))
        for j in range(hp):
            dstate[j] = dstates[j]
        dq = jnp.concatenate(dqs, axis=1)
        dk = jnp.concatenate(dks, axis=1)
        rloc = lax.broadcasted_iota(jnp.int32, (CHUNK, w), 0) % SUB
        for rr in range(SUB):
            bs = _sub_rows(b_ref, rr)
            ks = _sub_rows(k_ref, rr)
            e = jnp.exp(jnp.where(rloc >= rr, b - bs, -jnp.inf))
            pick = (col % SUB) == rr
            dacol = jnp.concatenate(
                [jnp.broadcast_to(jnp.sum(jnp.where(pick, dads[j], 0.0), axis=-1, keepdims=True), (CHUNK, kd))
                 for j in range(hp)], axis=1)
            wv = dacol * e
            dq = dq + wv * ks
            sums = jnp.sum((wv * q).reshape(nsub, SUB, w), axis=1)
            for j in range(hp):
                dks_ref[j, pl.ds(rr, nsub, stride=SUB), :] = sums[:, _head(j)]
        dk = dk + jnp.concatenate([dks_ref[j] for j in range(hp)], axis=1)

        dlf = _cumsum_rows(q * dq - k * dk, reverse=True) + jnp.concatenate(gsums, axis=1)
        dfv = dlf / f - dk
        dp_ref[:, col0 + rw:col0 + 2 * rw] = (dfv * (1.0 - lbv) * sg * (1.0 - sg)).astype(BF16)
        dlb_ref[...] += jnp.sum(dfv * (1.0 - sg), axis=0, keepdims=True)
        dp_ref[:, col0:col0 + rw] = (dq * (sq * (1.0 + q_r * (1.0 - sq)))).astype(BF16)
        dp_ref[:, col0 + 2 * rw:col0 + 3 * rw] = jnp.concatenate(dvs, axis=1).astype(BF16)

    def rev(c):
        return nc - 1 - c

    def col_in(kidx, j):
        return pl.BlockSpec((CHUNK, kd), lambda hg, c: (rev(c), cb + kidx * nh + hg * hp + j))

    def rows(width):
        return pl.BlockSpec((CHUNK, width), lambda hg, c: (rev(c), 0))

    return _call(body, name="hgrn_bwd", grid=(1, nc),
                 in_specs=[col_in(kidx, j) for kidx in range(4) for j in range(hp)] +
                          [pl.BlockSpec((1, w), lambda hg, c: (0, hg)), pl.BlockSpec((1, kd), lambda hg, c: (0, 0)),
                           rows(w),
                           pl.BlockSpec((hp, CHUNK, CHUNK), lambda hg, c: (hg, rev(c), 0)),
                           pl.BlockSpec((None, hp, kd, kd), lambda hg, c: (rev(c), hg, 0, 0)),
                           pl.BlockSpec((None, hp, kd, kd),
                                        lambda hg, c: (jnp.minimum(rev(c) + 1, nc - 1), hg, 0, 0)),
                           pl.BlockSpec((CHUNK, w), lambda hg, c: (rev(c), dcb // hp + hg)),
                           rows(aw), rows(kw), rows(kw)],
                 out_specs=[rows(iw),
                            pl.BlockSpec((1, w), lambda hg, c: (0, hg)),
                            pl.BlockSpec((None, 1, kd), lambda hg, c: (hg, 0, 0))],
                 out_shape=[jax.ShapeDtypeStruct((t, iw), BF16), jax.ShapeDtypeStruct((1, rw), F32),
                            jax.ShapeDtypeStruct((1, 1, kd), F32)],
                 args=(*([proj] * (4 * hp)), lb, norm_gain, o_all, att_all, st_all, st_all, dcat, dq_a, dk_a, dv_a),
                 scratch_shapes=[pltpu.VMEM((hp, kd, kd), F32), pltpu.VMEM((hp, CHUNK, kd), F32),
                                 pltpu.VMEM((hp, CHUNK, kd), F32), pltpu.VMEM((hp, CHUNK, kd), F32)],
                 sem=("parallel", "arbitrary"), carry=carry)


def comm_only(name, part):
    return _call(lambda: None, name=name, grid=(), in_specs=[], out_specs=[], out_shape=[], args=(), carry=part)[1]


def exchange_halves(name, array, axis):
    return comm_only(name, _scatter_step(array, axis))[0]


ADD_BLOCK_ELEMS = 1 << 20
ADAMW_BLOCK_ELEMS = 1 << 19


def add_kept_half(name, kept, got, sel, axis):
    minor = axis == "c"
    pieces, rows, cols = got.shape
    tr = _tile(rows, max(16, ADD_BLOCK_ELEMS // cols), mult=16)

    def body(sel_ref, k_ref, g_ref, o_ref):
        o_ref[...] = (k_ref[...].astype(F32) + g_ref[...].astype(F32)).astype(o_ref.dtype)

    kept_spec = (pl.BlockSpec((None, None, tr, cols), lambda p, i, s: (p, s[0], i, 0)) if minor else
                 pl.BlockSpec((None, None, tr, cols), lambda p, i, s: (s[0], p, i, 0)))
    return pl.pallas_call(
        body, name=name,
        grid_spec=pltpu.PrefetchScalarGridSpec(
            num_scalar_prefetch=1, grid=(pieces, rows // tr),
            in_specs=[kept_spec, pl.BlockSpec((None, tr, cols), lambda p, i, s: (p, i, 0))],
            out_specs=pl.BlockSpec((None, tr, cols), lambda p, i, s: (p, i, 0))),
        out_shape=jax.ShapeDtypeStruct(got.shape, got.dtype),
        compiler_params=_cparams(("parallel", "parallel")),
    )(sel, kept, got)


def _adamw(w, g, m, v):
    m = ADAM_B1 * m + (1.0 - ADAM_B1) * g
    v = ADAM_B2 * v + (1.0 - ADAM_B2) * (g * g)
    m_hat = m / (1.0 - ADAM_B1 ** ADAM_STEP)
    v_hat = v / (1.0 - ADAM_B2 ** ADAM_STEP)
    delta = -ADAM_LR * (m_hat / (jnp.sqrt(v_hat) + ADAM_EPS) + ADAM_WD * w)
    return delta, m, v


def add_adamw(name, kept, got, sel, w, m, v):
    rows, cols = w.shape
    tr = _tile(rows, max(16, ADAMW_BLOCK_ELEMS // cols), mult=16)

    def body(sel_ref, k_ref, g_ref, w_ref, m_ref, v_ref, go_ref, d_ref, mo_ref, vo_ref):
        g = k_ref[...].astype(F32) + g_ref[...].astype(F32)
        go_ref[...] = g
        d_ref[...], mo_ref[...], vo_ref[...] = _adamw(w_ref[...], g, m_ref[...], v_ref[...])

    tile = pl.BlockSpec((tr, cols), lambda i, s: (i, 0))
    return pl.pallas_call(
        body, name=name,
        grid_spec=pltpu.PrefetchScalarGridSpec(
            num_scalar_prefetch=1, grid=(rows // tr,),
            in_specs=[pl.BlockSpec((None, None, tr, cols), lambda i, s: (s[0], 0, i, 0)),
                      pl.BlockSpec((None, tr, cols), lambda i, s: (0, i, 0)), tile, tile, tile],
            out_specs=[tile] * 4),
        out_shape=[jax.ShapeDtypeStruct((rows, cols), F32)] * 4,
        compiler_params=_cparams(("parallel",)),
    )(sel, kept, got, w, m, v)


def small_allreduce_adamw(partial, scale, w, m, v):
    rows = partial.shape[0]

    def body(p_ref, s_ref, w_ref, m_ref, v_ref, g_ref, d_ref, mo_ref, vo_ref, slots, send_sems, recv_sems):
        x, y, c = _coords()
        my_slot = _slab_index((x, y, c))
        slots[my_slot] = p_ref[...]
        copies = []
        for mask in range(1, N_DEV):
            to = tuple(1 - v_ if (mask >> s_) & 1 else v_ for v_, s_ in ((x, 2), (y, 1), (c, 0)))
            copies.append(pltpu.make_async_remote_copy(
                src_ref=p_ref, dst_ref=slots.at[my_slot],
                send_sem=send_sems.at[mask - 1], recv_sem=recv_sems.at[mask - 1],
                device_id=to, device_id_type=MESH))
        for cp in copies:
            cp.start()
        for cp in copies:
            cp.wait()
        total = slots[0]
        for b in range(1, N_DEV):
            total = total + slots[b]
        g = total * s_ref[...]
        g_ref[...] = g
        d_ref[...], mo_ref[...], vo_ref[...] = _adamw(w_ref[...], g, m_ref[...], v_ref[...])

    vm = pl.BlockSpec(memory_space=pltpu.VMEM)
    return pl.pallas_call(
        body, name="small_allreduce_adamw",
        in_specs=[vm] * 5, out_specs=[vm] * 4,
        out_shape=[jax.ShapeDtypeStruct((rows, LANES), F32)] * 4,
        scratch_shapes=[pltpu.VMEM((N_DEV, rows, LANES), F32),
                        pltpu.SemaphoreType.DMA((N_DEV - 1,)), pltpu.SemaphoreType.DMA((N_DEV - 1,))],
        compiler_params=pltpu.CompilerParams(has_side_effects=True),
    )(partial, scale, w, m, v)


_SMALL = ("attn_sinks", "attn_out_gain", "rnn_lb_logits", "rnn_norm_gain", "mix_pre_gain", "mix_post_gain",
          "mlp_pre_gain", "mlp_post_gain")


def _pack(parts):
    rows = []
    for p in parts:
        flat = p.reshape(-1).astype(F32)
        pad = (-flat.shape[0]) % LANES
        rows.append(jnp.pad(flat, (0, pad)).reshape(-1, LANES))
    packed = jnp.concatenate(rows, axis=0)
    pad_rows = (-packed.shape[0]) % 8
    return jnp.pad(packed, ((0, pad_rows), (0, 0)))


def _unpack(packed, shapes):
    out, r = [], 0
    for s in shapes:
        size = math.prod(s)
        nrows = -(-size // LANES)
        out.append(packed[r:r + nrows].reshape(-1)[:size].reshape(s))
        r += nrows
    return out


class _Scatter:
    def __init__(self, tag, grad, sels):
        self.tag, self.sels = tag, sels
        self.shape = grad.shape[1:]
        self.cur = grad.reshape(4, 2, *self.shape)
        self.stage = 0

    def step(self):
        return _scatter_step(self.cur, "cxy"[self.stage])

    def land(self, got, w=None, m=None, v=None):
        axis = "cxy"[self.stage]
        name = "rs_add_%s_%s" % (axis, self.tag)
        sel = self.sels[axis]
        self.stage += 1
        if axis == "y":
            return add_adamw(name, self.cur, got, sel, w, m, v)
        summed = add_kept_half(name, self.cur, got, sel, axis)
        self.cur = summed.reshape(2, summed.shape[0] // 2, *self.shape)
        return None


def kernel(x, w_in, attn_sinks, attn_out_gain, rnn_lb_logits, rnn_norm_gain, w_out, mix_pre_gain, mix_post_gain, mlp_pre_gain, mlp_post_gain, w_up, w_down, loss_target, m_w_in, m_attn_sinks, m_attn_out_gain, m_rnn_lb_logits, m_rnn_norm_gain, m_w_out, m_mix_pre_gain, m_mix_post_gain, m_mlp_pre_gain, m_mlp_post_gain, m_w_up, m_w_down, v_w_in, v_attn_sinks, v_attn_out_gain, v_rnn_lb_logits, v_rnn_norm_gain, v_w_out, v_mix_pre_gain, v_mix_post_gain, v_mlp_pre_gain, v_mlp_post_gain, v_w_up, v_w_down):
    xs, target = x[0], loss_target[0]
    t, d = xs.shape
    aw = d // 2
    rw = d - aw
    col0 = aw + 2 * N_KV_HEADS * HEAD_DIM
    small_w = dict(attn_sinks=attn_sinks, attn_out_gain=attn_out_gain, rnn_lb_logits=rnn_lb_logits,
                   rnn_norm_gain=rnn_norm_gain, mix_pre_gain=mix_pre_gain, mix_post_gain=mix_post_gain,
                   mlp_pre_gain=mlp_pre_gain, mlp_post_gain=mlp_post_gain)
    small_m = dict(attn_sinks=m_attn_sinks, attn_out_gain=m_attn_out_gain, rnn_lb_logits=m_rnn_lb_logits,
                   rnn_norm_gain=m_rnn_norm_gain, mix_pre_gain=m_mix_pre_gain, mix_post_gain=m_mix_post_gain,
                   mlp_pre_gain=m_mlp_pre_gain, mlp_post_gain=m_mlp_post_gain)
    small_v = dict(attn_sinks=v_attn_sinks, attn_out_gain=v_attn_out_gain, rnn_lb_logits=v_rnn_lb_logits,
                   rnn_norm_gain=v_rnn_norm_gain, mix_pre_gain=v_mix_pre_gain, mix_post_gain=v_mix_post_gain,
                   mlp_pre_gain=v_mlp_pre_gain, mlp_post_gain=v_mlp_post_gain)
    cx, cy, cc = _coords()
    sels = {a: jnp.reshape(v_, (1,)).astype(jnp.int32) for a, v_ in (("x", cx), ("y", cy), ("c", cc))}

    w_in_t, m_in_t, v_in_t = w_in[0].T, m_w_in[0].T, v_w_in[0].T
    s_in, s_out, s_up, s_down = (w.astype(BF16) for w in (w_in_t, w_out[0], w_up[0], w_down[0]))
    probs = jax.nn.softmax(rnn_lb_logits.astype(F32), axis=0)
    lb = probs[0:1]

    (h1,), (wint_half,) = pre_norm(xs, mix_pre_gain, carry=_gather_first(s_in))
    wint = comm_only("gather_second_w_in", _gather_second(wint_half))[0].reshape(-1, d)
    up_rows = s_up.shape[0]
    proj, (wup_part,) = mm_nt("in_proj", h1, wint, F32, carry=_gather_first(s_up, rows=(0, up_rows // 2)))
    (attn_o, attn_n), (wup_half, wout_half) = attn_fwd(
        proj, attn_sinks, attn_out_gain, aw,
        carry=_merge(_gather_first(s_up, rows=(up_rows // 2, up_rows), into=wup_part), _gather_first(s_out)))
    (cat, o_r, att, st), (wup, wout, wdown_half) = hgrn_fwd(
        proj, attn_n, lb, rnn_norm_gain, col0, rw,
        carry=_merge(_gather_second(wup_half), _gather_second(wout_half), _gather_first(s_down)))
    wout = wout.reshape(-1, d)
    mixed, (wdown,) = mm_nn("out_proj", cat, wout, F32, carry=_gather_second(wdown_half))
    wdown = wdown.reshape(-1, d)
    x1, h2 = mid_fwd(mixed, mix_post_gain, xs, mlp_pre_gain)
    u = up_proj(h2, wup)
    y = down_proj(u, wdown)
    sse, dout, dy, dg_mlppost = loss_bwd(y, mlp_post_gain, x1, target)

    du = down_bwd_act(dy, wdown, u)
    rs_down = _Scatter("down", down_wgrad(u, dy).reshape(N_DEV, -1, d), sels)
    dh2, (got,) = up_bwd_x(du, wup, carry=rs_down.step())
    rs_down.land(got)
    dwup, (got,) = up_wgrad(h2, du, carry=rs_down.step())
    rs_down.land(got)
    rs_up = _Scatter("up", dwup, sels)
    (dx1, dmixed, dg_mlppre, dg_mixpost), (got,) = mid_bwd(dh2, x1, mlp_pre_gain, dout, mixed, mix_post_gain,
                                                          carry=rs_up.step())
    rs_up.land(got)
    dcat = mm_nt("out_bwd_x", dmixed, wout, F32)
    rs_out = _Scatter("out", mm_tn("out_wgrad", cat, dmixed, BF16).reshape(N_DEV, -1, d), sels)
    (dq_a, dk_a, dv_a, dsinks, daog), (got_d, got_o) = attn_bwd(
        proj, attn_sinks, attn_out_gain, attn_o, dcat, aw, carry=_merge(rs_down.step(), rs_out.step()))
    out_down = rs_down.land(got_d, w_down[0], m_w_down[0], v_w_down[0])
    rs_out.land(got_o)
    (dproj, dlb, dng), (got_u, got_o) = hgrn_bwd(
        proj, lb, rnn_norm_gain, o_r, att, st, dcat, dq_a, dk_a, dv_a, col0, rw,
        carry=_merge(rs_up.step(), rs_out.step()))
    rs_up.land(got_u)
    rs_out.land(got_o)
    dwin, (got_u, got_o) = mm_tn("in_wgrad", dproj, h1, BF16, carry=_merge(rs_up.step(), rs_out.step()))
    out_up = rs_up.land(got_u, w_up[0], m_w_up[0], v_w_up[0])
    out_out = rs_out.land(got_o, w_out[0], m_w_out[0], v_w_out[0])
    rs_in = _Scatter("in", dwin.reshape(N_DEV, -1, d), sels)
    rs_in.land(exchange_halves("rs_exchange_c_in", rs_in.cur, "c"))
    dh1, (got,) = mm_nn("in_bwd_x", dproj, wint, F32, tm=MM_TILE // 2, carry=rs_in.step())
    rs_in.land(got)
    grad_x, dg_mixpre = first_bwd(dh1, xs, mix_pre_gain, dx1)
    out_in = rs_in.land(exchange_halves("rs_exchange_y_in", rs_in.cur, "y"), w_in_t, m_in_t, v_in_t)
    big_out = [out_in, out_out, out_up, out_down]

    n_heads = attn_sinks.shape[1]
    jac = probs[0] * probs[1]
    partial = _pack([sse, dsinks[0, :n_heads], daog, jnp.stack([dlb[0], dlb[0]]), jnp.sum(dng, axis=0),
                     dg_mixpre, dg_mixpost, dg_mlppre, dg_mlppost])
    ones = [jnp.ones(small_w[k].shape, F32) for k in _SMALL]
    ones[2] = jnp.stack([jac, -jac])
    scale = _pack([jnp.full((1,), 0.5 / d, F32)] + ones)
    zero = jnp.zeros((1,), F32)
    outs = small_allreduce_adamw(partial, scale, _pack([zero] + [small_w[k] for k in _SMALL]),
                                 _pack([zero] + [small_m[k] for k in _SMALL]),
                                 _pack([jnp.ones((1,), F32)] + [small_v[k] for k in _SMALL]))
    shapes = [(1,)] + [small_w[k].shape for k in _SMALL]
    sgrad, sdelta, snm, snv = (_unpack(o, shapes) for o in outs)
    loss = sgrad[0][0]

    def big(i, j):
        o = big_out[i][j]
        return (o.T if i == 0 else o)[None]

    def ordered(j, smalls):
        s = dict(zip(_SMALL, smalls[1:]))
        return [big(0, j), s["attn_sinks"], s["attn_out_gain"], s["rnn_lb_logits"], s["rnn_norm_gain"], big(1, j),
                s["mix_pre_gain"], s["mix_post_gain"], s["mlp_pre_gain"], s["mlp_post_gain"], big(2, j), big(3, j)]

    return (loss, grad_x[None], *ordered(0, sgrad), *ordered(1, sdelta), *ordered(2, snm), *ordered(3, snv))
```

```python
import math

import jax
import jax.numpy as jnp
from jax import lax
from jax.experimental import pallas as pl
from jax.experimental.pallas import tpu as pltpu

F32 = jnp.float32
BF16 = jnp.bfloat16

HEAD_DIM = 64
N_KV_HEADS = 2
BLOCK = 128
RNN_HEAD_DIM = 128
CHUNK = 64
SUB = 16
EPS = 1e-6

ADAM_LR = 0.001
ADAM_B1 = 0.9
ADAM_B2 = 0.999
ADAM_EPS = 1e-08
ADAM_WD = 0.01
ADAM_STEP = 10

N_DEV = 8
LANES = 128
V7X_VMEM_LIMIT = 56 * 1024 * 1024
MESH = pl.DeviceIdType.MESH
HI = lax.Precision.HIGHEST
ANY = pl.BlockSpec(memory_space=pl.ANY)
_AXES = ("x", "y", "c")


def _cparams(sem=None, **kw):
    return pltpu.CompilerParams(dimension_semantics=sem, vmem_limit_bytes=V7X_VMEM_LIMIT, **kw)


def _dot(a, b, dims):
    return lax.dot_general(a.astype(BF16), b.astype(BF16), (dims, ((), ())), preferred_element_type=F32)


NN = ((1,), (0,))
NT = ((1,), (1,))
TN = ((0,), (0,))


def _pick(n, pref):
    t = min(n, pref)
    while n % t:
        t //= 2
    return t


def _tile(n, pref, mult=LANES):
    if n <= pref:
        return n
    t = pref - pref % mult
    while n % t:
        t -= mult
    return t


def _coords():
    return lax.axis_index("x"), lax.axis_index("y"), lax.axis_index("c")


def _slab_index(dev):
    return 4 * dev[0] + 2 * dev[1] + dev[2]


class _Part:
    def __init__(self, operands, landings, aliases, n_sems, plan):
        self.operands, self.landings, self.aliases, self.n_sems, self.plan = operands, landings, aliases, n_sems, plan


def _merge(*parts):
    operands, landings, aliases, plans = [], [], {}, []
    s0 = 0
    for p in parts:
        o0, l0 = len(operands), len(landings)
        aliases.update({o0 + i: l0 + j for i, j in p.aliases.items()})
        plans.append((p.plan, o0, len(p.operands), l0, len(p.landings), s0))
        operands += p.operands
        landings += p.landings
        s0 += p.n_sems

    def plan(ops, lands, sem):
        starts, waits = [], []
        for f, o0, no, l0, nl, off in plans:
            s, w = f(ops[o0:o0 + no], lands[l0:l0 + nl], lambda kind, k, off=off: sem(kind, off + k))
            starts += s
            waits += w
        return starts, waits

    return _Part(operands, landings, aliases, s0, plan)


def _gather_peers(x, y, c):
    return [(x, y, 1 - c), (1 - x, y, c), (x, 1 - y, c), (1 - x, 1 - y, c)]


def _gather_first(shard, rows=None, into=None):
    lo, hi = (0, shard.shape[0]) if rows is None else rows

    def plan(ops, lands, sem):
        x, y, c = _coords()
        me, peers = (x, y, c), _gather_peers(x, y, c)
        src = ops[0].at[pl.ds(lo, hi - lo)]

        def slab(block):
            return lands[0].at[_slab_index(block), pl.ds(lo, hi - lo)]

        def cp(k, block, to):
            return pltpu.make_async_remote_copy(
                src_ref=src, dst_ref=slab(block),
                send_sem=sem(0, k), recv_sem=sem(1, k), device_id=to, device_id_type=MESH)

        local = pltpu.make_async_copy(src, slab(me), sem(2, 0))
        sends = [cp(k, me, to) for k, to in enumerate(peers)]
        recvs = [cp(k, frm, me) for k, frm in enumerate(peers)]
        return ([local.start] + [s.start for s in sends],
                [local.wait] + [s.wait_send for s in sends] + [r.wait_recv for r in recvs])

    landing = jax.ShapeDtypeStruct((N_DEV, *shard.shape), shard.dtype)
    if into is None:
        return _Part([shard], [landing], {}, 4, plan)
    return _Part([shard, into], [landing], {1: 0}, 4, plan)


def _gather_second(gathered):
    def plan(ops, lands, sem):
        x, y, c = _coords()
        sibling = (x, y, 1 - c)
        chips = [(1 - x, y), (x, 1 - y), (1 - x, 1 - y)]

        def cp(k, block):
            slab = lands[0].at[_slab_index(block)]
            return pltpu.make_async_remote_copy(
                src_ref=slab, dst_ref=slab, send_sem=sem(0, k), recv_sem=sem(1, k),
                device_id=sibling, device_id_type=MESH)

        sends = [cp(k, (*chip, c)) for k, chip in enumerate(chips)]
        recvs = [cp(k, (*chip, 1 - c)) for k, chip in enumerate(chips)]
        return [s.start for s in sends], [s.wait_send for s in sends] + [r.wait_recv for r in recvs]

    return _Part([gathered], [jax.ShapeDtypeStruct(gathered.shape, gathered.dtype)], {0: 0}, 3, plan)


def _scatter_step(array, axis, minor=None, rows=None):
    minor = (axis == "c") if minor is None else minor
    pieces = array.shape[0] if minor else array.shape[1]
    lo, hi = (0, array.shape[2]) if rows is None else rows

    def plan(ops, lands, sem):
        coords = list(_coords())
        ai = _AXES.index(axis)
        mine = coords[ai]
        peer = list(coords)
        peer[ai] = 1 - mine
        cps = []
        for p in range(pieces):
            src = ops[0].at[p, 1 - mine, pl.ds(lo, hi - lo)] if minor else ops[0].at[1 - mine, p, pl.ds(lo, hi - lo)]
            cps.append(pltpu.make_async_remote_copy(
                src_ref=src, dst_ref=lands[0].at[p], send_sem=sem(0, p), recv_sem=sem(1, p),
                device_id=tuple(peer), device_id_type=MESH))
        return [cp.start for cp in cps], [cp.wait for cp in cps]

    return _Part([array], [jax.ShapeDtypeStruct((pieces, hi - lo, array.shape[3]), array.dtype)], {}, pieces, plan)


def _grid_edges(grid):
    first = last = None
    for ax, n in enumerate(grid):
        p = pl.program_id(ax)
        f, l = p == 0, p == n - 1
        first = f if first is None else jnp.logical_and(first, f)
        last = l if last is None else jnp.logical_and(last, l)
    return first, last


def _call(body, *, name, grid, in_specs, out_specs, out_shape, args, scratch_shapes=(), sem=None, carry=None):
    if carry is None:
        return pl.pallas_call(
            body, name=name, grid=grid, in_specs=list(in_specs), out_specs=list(out_specs),
            out_shape=list(out_shape), scratch_shapes=list(scratch_shapes), compiler_params=_cparams(sem),
        )(*args)
    n_in, n_out, n_scr = len(in_specs), len(out_specs), len(scratch_shapes)
    n_cin, n_cout = len(carry.operands), len(carry.landings)

    def wrapped(*refs):
        ins, cins = refs[:n_in], refs[n_in:n_in + n_cin]
        o0 = n_in + n_cin
        outs, couts = refs[o0:o0 + n_out], refs[o0 + n_out:o0 + n_out + n_cout]
        s0 = o0 + n_out + n_cout
        scr, sems = refs[s0:s0 + n_scr], refs[s0 + n_scr:]
        first, last = _grid_edges(grid)

        def plan():
            return carry.plan(cins, couts, lambda kind, k: sems[kind].at[k])

        def start_all():
            for start in plan()[0]:
                start()

        def wait_all():
            for wait in plan()[1]:
                wait()

        if grid:
            pl.when(first)(start_all)
            body(*ins, *outs, *scr)
            pl.when(last)(wait_all)
        else:
            start_all()
            body(*ins, *outs, *scr)
            wait_all()

    sem_arrays = [pltpu.SemaphoreType.DMA((carry.n_sems,))] * 3
    res = pl.pallas_call(
        wrapped, name=name, grid=grid,
        in_specs=[*in_specs, *[ANY] * n_cin], out_specs=[*out_specs, *[ANY] * n_cout],
        out_shape=[*out_shape, *carry.landings],
        scratch_shapes=[*scratch_shapes, *sem_arrays],
        input_output_aliases={n_in + i: n_out + j for i, j in carry.aliases.items()},
        compiler_params=_cparams(("arbitrary",) * len(grid) if grid else None, has_side_effects=True),
    )(*args, *carry.operands)
    return res[:n_out], res[n_out:]


MM_TILE = 1024
MM_K_TILE = 2048
MXU_COLS = 256


def _matmul(name, a, b, dims, grid, a_spec, b_spec, out_shape, out_spec, epilogue,
            extras=(), extra_specs=(), prologue=None, carry=None):
    nk = grid[2]
    n_extra = len(extras)
    acc_shape = out_spec.block_shape[-2:]

    def lhs(a_ref):
        return a_ref[...] if prologue is None else prologue(a_ref[...])

    def body_one(a_ref, b_ref, *rest):
        epilogue(_dot(lhs(a_ref), b_ref[...], dims), rest[:n_extra], rest[n_extra:])

    def body_acc(a_ref, b_ref, *rest):
        acc = rest[-1]
        k = pl.program_id(2)
        part = _dot(lhs(a_ref), b_ref[...], dims)

        @pl.when(k == 0)
        def _():
            acc[...] = part

        @pl.when(k > 0)
        def _():
            acc[...] += part

        @pl.when(k == nk - 1)
        def _():
            epilogue(acc[...], rest[:n_extra], rest[n_extra:-1])

    res = _call(body_one if nk == 1 else body_acc, name=name, grid=grid,
                in_specs=[a_spec, b_spec, *extra_specs], out_specs=[out_spec], out_shape=[out_shape],
                args=(a, b, *extras), scratch_shapes=[] if nk == 1 else [pltpu.VMEM(acc_shape, F32)],
                sem=("parallel", "parallel", "arbitrary"), carry=carry)
    return res[0] if carry is None else (res[0][0], res[1])


def _store_as(acc, extra_refs, out_refs):
    out_refs[0][...] = acc.astype(out_refs[0].dtype)


def _square(u):
    return u * u


def mm_nn(name, a, b, out_dtype, tk=None, tm=MM_TILE, prologue=None, carry=None):
    (m, kk), n = a.shape, b.shape[1]
    tm, tn = _tile(m, tm), _tile(n, MM_TILE, mult=MXU_COLS)
    tk = kk if tk is None else _tile(kk, tk, mult=MXU_COLS)
    return _matmul(name, a, b, NN, (m // tm, n // tn, kk // tk),
                   pl.BlockSpec((tm, tk), lambda i, j, k: (i, k)),
                   pl.BlockSpec((tk, tn), lambda i, j, k: (k, j)),
                   jax.ShapeDtypeStruct((m, n), out_dtype),
                   pl.BlockSpec((tm, tn), lambda i, j, k: (i, j)), _store_as, prologue=prologue, carry=carry)


def mm_nt(name, a, b, out_dtype, epilogue=_store_as, extras=(), extra_specs=(), carry=None):
    (m, kk), n = a.shape, b.shape[0]
    tm, tn = _tile(m, MM_TILE), _tile(n, MM_TILE, mult=MXU_COLS)
    return _matmul(name, a, b, NT, (m // tm, n // tn, 1),
                   pl.BlockSpec((tm, kk), lambda i, j, k: (i, 0)),
                   pl.BlockSpec((tn, kk), lambda i, j, k: (j, 0)),
                   jax.ShapeDtypeStruct((m, n), out_dtype),
                   pl.BlockSpec((tm, tn), lambda i, j, k: (i, j)), epilogue,
                   extras=extras, extra_specs=extra_specs, carry=carry)


def mm_tn(name, a, b, out_dtype, prologue=None, carry=None):
    (kk, m), n = a.shape, b.shape[1]
    tm, tn, tk = _tile(m, MM_TILE), _tile(n, MM_TILE), _tile(kk, MM_K_TILE)
    return _matmul(name, a, b, TN, (m // tm, n // tn, kk // tk),
                   pl.BlockSpec((tk, tm), lambda i, j, k: (k, i)),
                   pl.BlockSpec((tk, tn), lambda i, j, k: (k, j)),
                   jax.ShapeDtypeStruct((m, n), out_dtype),
                   pl.BlockSpec((tm, tn), lambda i, j, k: (i, j)), _store_as, prologue=prologue, carry=carry)


def up_proj(h2, wup_slabs):
    (m, kk), (_, _, ns) = h2.shape, wup_slabs.shape
    tm, tn = _tile(m, MM_TILE), _tile(ns, MM_TILE)
    r = ns // tn
    n = N_DEV * ns

    def epi(acc, extra_refs, out_refs):
        out_refs[0][...] = jnp.maximum(acc, 0.0).astype(BF16)

    return _matmul("up_proj", h2, wup_slabs, NN, (m // tm, n // tn, 1),
                   pl.BlockSpec((tm, kk), lambda i, j, k: (i, 0)),
                   pl.BlockSpec((None, kk, tn), lambda i, j, k: (j // r, 0, j % r)),
                   jax.ShapeDtypeStruct((m, n), BF16),
                   pl.BlockSpec((tm, tn), lambda i, j, k: (i, j)), epi)


def down_proj(u, wdown):
    return mm_nn("down_proj", u, wdown, F32, tk=MM_K_TILE, prologue=_square)


def down_bwd_act(dy, wdown, u):
    tm, tn = _tile(dy.shape[0], MM_TILE), _tile(wdown.shape[0], MM_TILE)

    def epi(acc, extra_refs, out_refs):
        out_refs[0][...] = (acc * (2.0 * extra_refs[0][...].astype(F32))).astype(BF16)

    return mm_nt("down_bwd_act", dy, wdown, BF16, epilogue=epi, extras=(u,),
                 extra_specs=(pl.BlockSpec((tm, tn), lambda i, j, k: (i, j)),))


def down_wgrad(u, dy):
    return mm_tn("down_wgrad", u, dy, BF16, prologue=_square)


def up_bwd_x(du, wup_slabs, carry=None):
    (m, kk), (_, n, ns) = du.shape, wup_slabs.shape
    tm, tk = _tile(m, MM_TILE), _tile(ns, MM_TILE)
    r = ns // tk
    return _matmul("up_bwd_x", du, wup_slabs, NT, (m // tm, 1, kk // tk),
                   pl.BlockSpec((tm, tk), lambda i, j, k: (i, k)),
                   pl.BlockSpec((None, n, tk), lambda i, j, k: (k // r, 0, k % r)),
                   jax.ShapeDtypeStruct((m, n), F32),
                   pl.BlockSpec((tm, n), lambda i, j, k: (i, 0)), _store_as, carry=carry)


def up_wgrad(h2, du, carry=None):
    (kk, m), n = h2.shape, du.shape[1]
    ns = n // N_DEV
    tm, tn, tk = _tile(m, MM_TILE), _tile(ns, MM_TILE), _tile(kk, MM_K_TILE)
    r = ns // tn
    return _matmul("up_wgrad", h2, du, TN, (m // tm, n // tn, kk // tk),
                   pl.BlockSpec((tk, tm), lambda i, j, k: (k, i)),
                   pl.BlockSpec((tk, tn), lambda i, j, k: (k, j)),
                   jax.ShapeDtypeStruct((N_DEV, m, ns), BF16),
                   pl.BlockSpec((None, tm, tn), lambda i, j, k: (j // r, i, j % r)), _store_as, carry=carry)


def _rstd(x):
    return lax.rsqrt(jnp.mean(x * x, axis=-1, keepdims=True) + EPS)


def _norm_bwd(x, g, dy):
    r = _rstd(x)
    xh = x * r
    dyg = dy * g
    dx = r * (dyg - xh * jnp.mean(dyg * xh, axis=-1, keepdims=True))
    return dx, jnp.sum(dy * xh, axis=0, keepdims=True)


def _row_spec(tr, d):
    return pl.BlockSpec((tr, d), lambda i: (i, 0))


def _vec_spec(d):
    return pl.BlockSpec((1, d), lambda i: (0, 0))


def _accum(ref, val):
    @pl.when(pl.program_id(0) == 0)
    def _():
        ref[...] = jnp.zeros_like(ref)

    ref[...] += val


def pre_norm(x, g, carry=None, tr=256):
    t, d = x.shape
    tr = _pick(t, tr)

    def body(x_ref, g_ref, h_ref):
        xx = x_ref[...]
        h_ref[...] = (xx * _rstd(xx) * g_ref[...]).astype(BF16)

    return _call(body, name="pre_norm", grid=(t // tr,),
                 in_specs=[_row_spec(tr, d), _vec_spec(d)], out_specs=[_row_spec(tr, d)],
                 out_shape=[jax.ShapeDtypeStruct((t, d), BF16)], args=(x, g), sem=("parallel",), carry=carry)


def mid_fwd(mixed, g_post, x, g_pre2, tr=256):
    t, d = x.shape
    tr = _pick(t, tr)

    def body(m_ref, gp_ref, x_ref, g2_ref, x1_ref, h2_ref):
        mm = m_ref[...]
        x1 = x_ref[...] + mm * _rstd(mm) * gp_ref[...]
        x1_ref[...] = x1
        h2_ref[...] = (x1 * _rstd(x1) * g2_ref[...]).astype(BF16)

    return _call(body, name="mid_fwd", grid=(t // tr,),
                 in_specs=[_row_spec(tr, d), _vec_spec(d), _row_spec(tr, d), _vec_spec(d)],
                 out_specs=[_row_spec(tr, d), _row_spec(tr, d)],
                 out_shape=[jax.ShapeDtypeStruct((t, d), F32), jax.ShapeDtypeStruct((t, d), BF16)],
                 args=(mixed, g_post, x, g_pre2), sem=("parallel",))


def loss_bwd(y, g_post2, x1, target, tr=256):
    t, d = y.shape
    tr = _pick(t, tr)

    def body(y_ref, g_ref, x1_ref, t_ref, sse_ref, dout_ref, dy_ref, dg_ref):
        yy = y_ref[...]
        g = g_ref[...]
        err = x1_ref[...] + yy * _rstd(yy) * g - t_ref[...]
        _accum(sse_ref, jnp.sum(jnp.sum(err * err, axis=1, keepdims=True), axis=0, keepdims=True))
        dout = err * (1.0 / d)
        dout_ref[...] = dout
        dy, dg = _norm_bwd(yy, g, dout)
        dy_ref[...] = dy.astype(BF16)
        _accum(dg_ref, dg)

    return _call(body, name="loss_bwd", grid=(t // tr,),
                 in_specs=[_row_spec(tr, d), _vec_spec(d), _row_spec(tr, d), _row_spec(tr, d)],
                 out_specs=[pl.BlockSpec((1, 1), lambda i: (0, 0)), _row_spec(tr, d), _row_spec(tr, d), _vec_spec(d)],
                 out_shape=[jax.ShapeDtypeStruct((1, 1), F32), jax.ShapeDtypeStruct((t, d), F32),
                            jax.ShapeDtypeStruct((t, d), BF16), jax.ShapeDtypeStruct((1, d), F32)],
                 args=(y, g_post2, x1, target), sem=("arbitrary",))


def mid_bwd(dh2, x1, g_pre2, dout, mixed, g_post, carry=None, tr=256):
    t, d = x1.shape
    tr = _pick(t, tr)

    def body(dh_ref, x1_ref, g2_ref, do_ref, m_ref, gp_ref, dx1_ref, dm_ref, dg2_ref, dgp_ref):
        d1, dg2 = _norm_bwd(x1_ref[...], g2_ref[...], dh_ref[...])
        dx1 = do_ref[...] + d1
        dx1_ref[...] = dx1
        dm, dgp = _norm_bwd(m_ref[...], gp_ref[...], dx1)
        dm_ref[...] = dm.astype(BF16)
        _accum(dg2_ref, dg2)
        _accum(dgp_ref, dgp)

    return _call(body, name="mid_bwd", grid=(t // tr,),
                 in_specs=[_row_spec(tr, d), _row_spec(tr, d), _vec_spec(d), _row_spec(tr, d), _row_spec(tr, d),
                           _vec_spec(d)],
                 out_specs=[_row_spec(tr, d), _row_spec(tr, d), _vec_spec(d), _vec_spec(d)],
                 out_shape=[jax.ShapeDtypeStruct((t, d), F32), jax.ShapeDtypeStruct((t, d), BF16),
                            jax.ShapeDtypeStruct((1, d), F32), jax.ShapeDtypeStruct((1, d), F32)],
                 args=(dh2, x1, g_pre2, dout, mixed, g_post), sem=("arbitrary",), carry=carry)


def first_bwd(dh1, x, g_pre, dx1, carry=None, tr=256):
    t, d = x.shape
    tr = _pick(t, tr)

    def body(dh_ref, x_ref, g_ref, dx1_ref, gx_ref, dg_ref):
        d0, dg = _norm_bwd(x_ref[...], g_ref[...], dh_ref[...])
        gx_ref[...] = dx1_ref[...] + d0
        _accum(dg_ref, dg)

    return _call(body, name="first_bwd", grid=(t // tr,),
                 in_specs=[_row_spec(tr, d), _row_spec(tr, d), _vec_spec(d), _row_spec(tr, d)],
                 out_specs=[_row_spec(tr, d), _vec_spec(d)],
                 out_shape=[jax.ShapeDtypeStruct((t, d), F32), jax.ShapeDtypeStruct((1, d), F32)],
                 args=(dh1, x, g_pre, dx1), sem=("arbitrary",), carry=carry)


def _attn_geometry(has_prev):
    r = lax.broadcasted_iota(jnp.int32, (BLOCK, 2 * BLOCK), 0)
    c = lax.broadcasted_iota(jnp.int32, (BLOCK, 2 * BLOCK), 1)
    dist = r + BLOCK - c
    valid = jnp.logical_and(jnp.logical_and(dist >= 0, dist < BLOCK), jnp.logical_or(c >= BLOCK, has_prev))
    return dist.astype(F32), valid


def _stack_pairs(x, g, pairs):
    base = g * pairs * LANES
    return jnp.concatenate([x[:, base + p * LANES:base + (p + 1) * LANES] for p in range(pairs)], axis=0)


def _unstack_pairs(xs, pairs):
    return jnp.concatenate([xs[p * BLOCK:(p + 1) * BLOCK, :] for p in range(pairs)], axis=1)


def _to_half(x, g, odd):
    lane = lax.broadcasted_iota(jnp.int32, x.shape, 1)
    y = x if (g == 1) == odd else pltpu.roll(x, HEAD_DIM, axis=1)
    return jnp.where((lane >= HEAD_DIM) == odd, y, 0.0)


def _from_halves(even, odd, g):
    lane = lax.broadcasted_iota(jnp.int32, even.shape, 1)
    if g == 0:
        return jnp.where(lane < HEAD_DIM, even + pltpu.roll(odd, HEAD_DIM, axis=1), 0.0)
    return jnp.where(lane >= HEAD_DIM, pltpu.roll(even, HEAD_DIM, axis=1) + odd, 0.0)


_PARITIES = [(g, odd) for g in range(N_KV_HEADS) for odd in (False, True)]


def _softmax_sink(s, sink_ref, g, odd, group, n_heads, geo):
    dist, valid = geo
    pairs = group // 2
    heads = [g * group + 2 * p + int(odd) for p in range(pairs)]
    bias = jnp.concatenate([(2.0 ** (-8.0 * (h + 1) / n_heads)) * dist for h in heads], axis=0)
    sink = jnp.concatenate([jnp.full((BLOCK, 1), sink_ref[0, h], F32) for h in heads], axis=0)
    s = jnp.where(jnp.concatenate([valid] * pairs, axis=0), s - bias, -jnp.inf)
    m = jnp.maximum(jnp.max(s, axis=-1, keepdims=True), sink)
    p = jnp.exp(s - m)
    p_sink = jnp.exp(sink - m)
    inv = 1.0 / (jnp.sum(p, axis=-1, keepdims=True) + p_sink)
    return p * inv, p_sink * inv


def attn_fwd(proj, sinks, gain, aw, carry=None):
    t = proj.shape[0]
    kw = N_KV_HEADS * HEAD_DIM
    n_heads = aw // HEAD_DIM
    group = n_heads // N_KV_HEADS
    pairs = group // 2
    assert kw == LANES and group % 2 == 0
    nb = t // BLOCK
    scale = HEAD_DIM ** -0.5

    def body(sink_ref, q_ref, k_ref, v_ref, g_ref, o_ref, on_ref):
        n = pl.program_id(0)
        cur = pl.multiple_of(n * BLOCK, BLOCK)
        prev = pl.multiple_of(jnp.maximum(n - 1, 0) * BLOCK, BLOCK)
        geo = _attn_geometry(n > 0)
        kcat = jnp.concatenate([k_ref[pl.ds(prev, BLOCK), :], k_ref[pl.ds(cur, BLOCK), :]], axis=0)
        vcat = jnp.concatenate([v_ref[pl.ds(prev, BLOCK), :], v_ref[pl.ds(cur, BLOCK), :]], axis=0)
        q = q_ref[...] * scale
        qs = [_stack_pairs(q, g, pairs) for g in range(N_KV_HEADS)]
        scores = [_dot(qs[g], _to_half(kcat, g, odd), NT) for g, odd in _PARITIES]
        probs = [_softmax_sink(s, sink_ref, g, odd, group, n_heads, geo)[0] for s, (g, odd) in zip(scores, _PARITIES)]
        outs = [_dot(p, _to_half(vcat, g, odd), NN) for p, (g, odd) in zip(probs, _PARITIES)]
        o = jnp.concatenate([_unstack_pairs(outs[2 * g] + outs[2 * g + 1], pairs) for g in range(N_KV_HEADS)], axis=1)
        o_ref[...] = o
        on_ref[...] = (o * _rstd(o) * g_ref[...]).astype(BF16)

    return _call(body, name="attn_fwd", grid=(nb,),
                 in_specs=[pl.BlockSpec(memory_space=pltpu.SMEM),
                           pl.BlockSpec((BLOCK, aw), lambda n: (n, 0)),
                           pl.BlockSpec((t, kw), lambda n: (0, aw // kw)),
                           pl.BlockSpec((t, kw), lambda n: (0, aw // kw + 1)),
                           pl.BlockSpec((1, aw), lambda n: (0, 0))],
                 out_specs=[pl.BlockSpec((BLOCK, aw), lambda n: (n, 0)), pl.BlockSpec((BLOCK, aw), lambda n: (n, 0))],
                 out_shape=[jax.ShapeDtypeStruct((t, aw), F32), jax.ShapeDtypeStruct((t, aw), BF16)],
                 args=(sinks, proj, proj, proj, gain), sem=("parallel",), carry=carry)


def attn_bwd(proj, sinks, gain, attn_o, dcat, aw, carry=None):
    t = proj.shape[0]
    kw = N_KV_HEADS * HEAD_DIM
    n_heads = aw // HEAD_DIM
    group = n_heads // N_KV_HEADS
    pairs = group // 2
    assert kw == LANES and group % 2 == 0
    nb = t // BLOCK
    scale = HEAD_DIM ** -0.5

    def body(sink_ref, q_ref, k_ref, v_ref, g_ref, o_ref, dn_ref, dq_ref, dk_ref, dv_ref, dsink_ref, dg_ref):
        n = pl.program_id(0)
        cur = pl.multiple_of(n * BLOCK, BLOCK)
        prev = pl.multiple_of(jnp.maximum(n - 1, 0) * BLOCK, BLOCK)
        geo = _attn_geometry(n > 0)

        @pl.when(n == 0)
        def _():
            dk_ref[...] = jnp.zeros_like(dk_ref)
            dv_ref[...] = jnp.zeros_like(dv_ref)
            dsink_ref[...] = jnp.zeros_like(dsink_ref)

        o = o_ref[...]
        do_all, dg = _norm_bwd(o, g_ref[...], dn_ref[...])
        _accum(dg_ref, dg)
        kcat = jnp.concatenate([k_ref[pl.ds(prev, BLOCK), :], k_ref[pl.ds(cur, BLOCK), :]], axis=0)
        vcat = jnp.concatenate([v_ref[pl.ds(prev, BLOCK), :], v_ref[pl.ds(cur, BLOCK), :]], axis=0)
        q = q_ref[...] * scale
        lane = lax.broadcasted_iota(jnp.int32, (1, LANES), 1)
        lane_s = lax.broadcasted_iota(jnp.int32, (pairs * BLOCK, LANES), 1)
        qs = [_stack_pairs(q, g, pairs) for g in range(N_KV_HEADS)]
        dos = [_stack_pairs(do_all, g, pairs) for g in range(N_KV_HEADS)]
        kxs = [_to_half(kcat, g, odd) for g, odd in _PARITIES]
        scores = [_dot(qs[g], kx, NT) for kx, (g, odd) in zip(kxs, _PARITIES)]
        dps = [_dot(dos[g], _to_half(vcat, g, odd), NT) for g, odd in _PARITIES]
        deltas = []
        for g in range(N_KV_HEADS):
            prod = dos[g] * _stack_pairs(o, g, pairs)
            delta_even = jnp.sum(jnp.where(lane_s < HEAD_DIM, prod, 0.0), axis=-1, keepdims=True)
            deltas += [delta_even, jnp.sum(prod, axis=-1, keepdims=True) - delta_even]
        dsink = jnp.zeros((1, LANES), F32)
        ps, dss = [], []
        for i, (g, odd) in enumerate(_PARITIES):
            p, p_sink = _softmax_sink(scores[i], sink_ref, g, odd, group, n_heads, geo)
            ps.append(p)
            dss.append(p * (dps[i] - deltas[i]))
            sink_rows = p_sink * deltas[i]
            for pr in range(pairs):
                h = g * group + 2 * pr + int(odd)
                dsink = dsink + jnp.where(
                    lane == h, -jnp.sum(sink_rows[pr * BLOCK:(pr + 1) * BLOCK], axis=0, keepdims=True), 0.0)
        dq_pairs = [_dot(ds, kx, NN) for ds, kx in zip(dss, kxs)]
        dk_halves = [_dot(ds, qs[g], TN) for ds, (g, odd) in zip(dss, _PARITIES)]
        dv_halves = [_dot(p, dos[g], TN) for p, (g, odd) in zip(ps, _PARITIES)]
        dq_ref[...] = jnp.concatenate(
            [_unstack_pairs((dq_pairs[2 * g] + dq_pairs[2 * g + 1]) * scale, pairs) for g in range(N_KV_HEADS)],
            axis=1).astype(BF16)
        dk_upd = _from_halves(dk_halves[0], dk_halves[1], 0) + _from_halves(dk_halves[2], dk_halves[3], 1)
        dv_upd = _from_halves(dv_halves[0], dv_halves[1], 0) + _from_halves(dv_halves[2], dv_halves[3], 1)
        dk_ref[pl.ds(prev, BLOCK), :] += dk_upd[:BLOCK]
        dv_ref[pl.ds(prev, BLOCK), :] += dv_upd[:BLOCK]
        dk_ref[pl.ds(cur, BLOCK), :] += dk_upd[BLOCK:]
        dv_ref[pl.ds(cur, BLOCK), :] += dv_upd[BLOCK:]
        dsink_ref[...] += dsink

    return _call(body, name="attn_bwd", grid=(nb,),
                 in_specs=[pl.BlockSpec(memory_space=pltpu.SMEM),
                           pl.BlockSpec((BLOCK, aw), lambda n: (n, 0)),
                           pl.BlockSpec((t, kw), lambda n: (0, aw // kw)),
                           pl.BlockSpec((t, kw), lambda n: (0, aw // kw + 1)),
                           pl.BlockSpec((1, aw), lambda n: (0, 0)),
                           pl.BlockSpec((BLOCK, aw), lambda n: (n, 0)),
                           pl.BlockSpec((BLOCK, aw), lambda n: (n, 0))],
                 out_specs=[pl.BlockSpec((BLOCK, aw), lambda n: (n, 0)),
                            pl.BlockSpec((t, kw), lambda n: (0, 0)), pl.BlockSpec((t, kw), lambda n: (0, 0)),
                            pl.BlockSpec((1, LANES), lambda n: (0, 0)), pl.BlockSpec((1, aw), lambda n: (0, 0))],
                 out_shape=[jax.ShapeDtypeStruct((t, aw), BF16), jax.ShapeDtypeStruct((t, kw), F32),
                            jax.ShapeDtypeStruct((t, kw), F32), jax.ShapeDtypeStruct((1, LANES), F32),
                            jax.ShapeDtypeStruct((1, aw), F32)],
                 args=(sinks, proj, proj, proj, gain, attn_o, dcat), sem=("arbitrary",), carry=carry)


def _sigmoid(x):
    return 0.5 * jnp.tanh(0.5 * x) + 0.5


def _chunk_geometry():
    row = lax.broadcasted_iota(jnp.int32, (CHUNK, CHUNK), 0)
    col = lax.broadcasted_iota(jnp.int32, (CHUNK, CHUNK), 1)
    return row, col


def _cumsum_rows(x, reverse=False):
    row, col = _chunk_geometry()
    tri = (col >= row) if reverse else (col <= row)
    return lax.dot_general(tri.astype(F32), x, ((NN), ((), ())), precision=HI, preferred_element_type=F32)


def _rep_sub(x4):
    k = x4.shape[-1]
    return jnp.broadcast_to(x4[:, None, :], (CHUNK // SUB, SUB, k)).reshape(CHUNK, k)


def _gates(q_r, f_r, lb):
    sg = _sigmoid(f_r)
    f = lb + (1.0 - lb) * sg
    sq = _sigmoid(q_r)
    return sg, f, sq, q_r * sq


def _offdiag_terms(b, j):
    c = b[j * SUB + SUB - 1:j * SUB + SUB, :]
    return jnp.exp(jnp.minimum(b - c, 0.0)), jnp.exp(jnp.minimum(c - b, 0.0))


def _store_heads(ref, x):
    for j in range(ref.shape[0]):
        ref[j] = x[:, _head(j)]


def _sub_rows(ref, r):
    rows = [ref[j, pl.ds(r, CHUNK // SUB, stride=SUB), :] for j in range(ref.shape[0])]
    return _rep_sub(jnp.concatenate(rows, axis=1))


def _diag_mask():
    row, col = _chunk_geometry()
    return jnp.logical_and((row // SUB) == (col // SUB), row >= col)


HGRN_HEADS_PER_STEP = 8


def _wide(refs):
    return jnp.concatenate([r[...] for r in refs], axis=1)


def _head(j):
    return slice(j * RNN_HEAD_DIM, (j + 1) * RNN_HEAD_DIM)


def _cat_heads(parts, hs):
    return jnp.concatenate([p[:, hs] for p in parts], axis=1)


def _offdiag_factors(q, k, b):
    rowi = lax.broadcasted_iota(jnp.int32, b.shape, 0)
    qs, ks, ers, ecs = [], [], [], []
    for j in range(CHUNK // SUB - 1):
        e_row, e_col = _offdiag_terms(b, j)
        e_row = jnp.where(rowi >= (j + 1) * SUB, e_row, 0.0)
        e_col = jnp.where((rowi // SUB) == j, e_col, 0.0)
        qs.append(q * e_row)
        ks.append(k * e_col)
        ers.append(e_row)
        ecs.append(e_col)
    return qs, ks, ers, ecs


def hgrn_fwd(proj, attn_n, lb, norm_gain, col0, rw, carry=None):
    t, aw = attn_n.shape
    nh = rw // RNN_HEAD_DIM
    nc = t // CHUNK
    kd = RNN_HEAD_DIM
    cb = col0 // kd
    nsub = CHUNK // SUB
    hp = nh
    assert nh <= HGRN_HEADS_PER_STEP
    w = hp * kd

    def body(*refs):
        q_refs, f_refs, i_refs, g_refs = (refs[i * hp:(i + 1) * hp] for i in range(4))
        lb_ref, ng_ref, an_ref, cat_ref, o_ref, att_ref, st_ref, state, b_ref, k_ref = refs[4 * hp:]
        c = pl.program_id(1)

        @pl.when(c == 0)
        def _():
            state[...] = jnp.zeros_like(state)

        st_ref[...] = state[...]
        q_r, f_r, v, g_r = (_wide(rs) for rs in (q_refs, f_refs, i_refs, g_refs))
        _, f, _, q = _gates(q_r, f_r, lb_ref[...])
        k = 1.0 - f
        b = _cumsum_rows(jnp.log(f))
        _store_heads(b_ref, b)
        _store_heads(k_ref, k)
        qcat, kcat, _, _ = _offdiag_factors(q, k, b)
        row, col = _chunk_geometry()
        same = (row // SUB) == (col // SUB)
        rloc = lax.broadcasted_iota(jnp.int32, (CHUNK, w), 0) % SUB
        diag = [jnp.zeros((CHUNK, CHUNK), F32)] * hp
        for r in range(SUB):
            bs = _sub_rows(b_ref, r)
            ks = _sub_rows(k_ref, r)
            prod = q * jnp.exp(jnp.where(rloc >= r, b - bs, -jnp.inf)) * ks
            place = jnp.logical_and((col % SUB) == r, same)
            diag = [jnp.where(place, jnp.sum(prod[:, _head(j)], axis=-1, keepdims=True), diag[j]) for j in range(hp)]
        b_last = b[CHUNK - 1:CHUNK, :]
        qe = q * jnp.exp(b)
        kdec = k * jnp.exp(b_last - b)
        decay = jnp.exp(b_last)
        outs, normed, states = [], [], []
        for j in range(hp):
            hs = _head(j)
            att = diag[j] + _dot(_cat_heads(qcat, hs), _cat_heads(kcat, hs), NT)
            att_ref[j] = att
            sj = state[j]
            o = _dot(qe[:, hs], sj, NT) + _dot(att, v[:, hs], NN)
            outs.append(o)
            normed.append(o * _rstd(o))
            states.append(sj * decay[:, hs] + _dot(v[:, hs], kdec[:, hs], TN))
        for j in range(hp):
            state[j] = states[j]
        o_ref[...] = jnp.concatenate(outs, axis=1)
        gate = g_r * _sigmoid(g_r)
        cat_ref[:, :aw] = an_ref[...]
        cat_ref[:, aw:] = (jnp.concatenate(normed, axis=1) * jnp.tile(ng_ref[...], (1, hp)) * gate).astype(BF16)

    def col(kidx, j):
        return pl.BlockSpec((CHUNK, kd), lambda hg, c: (c, cb + kidx * nh + hg * hp + j))

    return _call(body, name="hgrn_fwd", grid=(1, nc),
                 in_specs=[col(kidx, j) for kidx in range(4) for j in range(hp)] +
                          [pl.BlockSpec((1, w), lambda hg, c: (0, hg)), pl.BlockSpec((1, kd), lambda hg, c: (0, 0)),
                           pl.BlockSpec((CHUNK, aw), lambda hg, c: (c, 0))],
                 out_specs=[pl.BlockSpec((CHUNK, aw + w), lambda hg, c: (c, 0)),
                            pl.BlockSpec((CHUNK, w), lambda hg, c: (c, hg)),
                            pl.BlockSpec((hp, CHUNK, CHUNK), lambda hg, c: (hg, c, 0)),
                            pl.BlockSpec((None, hp, kd, kd), lambda hg, c: (c, hg, 0, 0))],
                 out_shape=[jax.ShapeDtypeStruct((t, aw + rw), BF16), jax.ShapeDtypeStruct((t, rw), F32),
                            jax.ShapeDtypeStruct((nh, t, CHUNK), F32), jax.ShapeDtypeStruct((nc, nh, kd, kd), F32)],
                 args=(*([proj] * (4 * hp)), lb, norm_gain, attn_n),
                 scratch_shapes=[pltpu.VMEM((hp, kd, kd), F32), pltpu.VMEM((hp, CHUNK, kd), F32),
                                 pltpu.VMEM((hp, CHUNK, kd), F32)],
                 sem=("parallel", "arbitrary"), carry=carry)


def hgrn_bwd(proj, lb, norm_gain, o_all, att_all, st_all, dcat, dq_a, dk_a, dv_a, col0, rw, carry=None):
    t, iw = proj.shape
    aw, kw = dq_a.shape[1], dk_a.shape[1]
    nh = rw // RNN_HEAD_DIM
    nc = t // CHUNK
    kd = RNN_HEAD_DIM
    cb = col0 // kd
    nsub = CHUNK // SUB
    dcb = (dcat.shape[1] - rw) // kd
    hp = nh
    assert nh <= HGRN_HEADS_PER_STEP and dcb % hp == 0 and col0 == aw + 2 * kw and iw == col0 + 4 * rw
    w = hp * kd

    def per_head(x, fn):
        return jnp.concatenate([jnp.broadcast_to(fn(x[:, _head(j)]), (CHUNK, kd)) for j in range(hp)], axis=1)

    def body(*refs):
        q_refs, f_refs, i_refs, g_refs = (refs[i * hp:(i + 1) * hp] for i in range(4))
        (lb_ref, ng_ref, o_ref, att_ref, st0_ref, st1_ref, d_ref, dqa_ref, dka_ref, dva_ref, dp_ref, dlb_ref, dng_ref,
         dstate, b_ref, k_ref, dks_ref) = refs[4 * hp:]
        ci = pl.program_id(1)

        @pl.when(ci == 0)
        def _():
            dstate[...] = jnp.zeros_like(dstate)
            dlb_ref[...] = jnp.zeros_like(dlb_ref)
            dng_ref[...] = jnp.zeros_like(dng_ref)

        lbv = lb_ref[...]
        q_r, f_r, v, g_r = (_wide(rs) for rs in (q_refs, f_refs, i_refs, g_refs))
        sg, f, sq, q = _gates(q_r, f_r, lbv)
        k = 1.0 - f
        b = _cumsum_rows(jnp.log(f))
        _store_heads(b_ref, b)
        _store_heads(k_ref, k)
        row, col = _chunk_geometry()

        o = o_ref[...]
        ng = jnp.tile(ng_ref[...], (1, hp))
        sgg = _sigmoid(g_r)
        gate = g_r * sgg
        d_rnn = d_ref[...]
        r = per_head(o, _rstd)
        oh = o * r
        dp_ref[:, :aw] = dqa_ref[...]
        dp_ref[:, aw:aw + kw] = dka_ref[...].astype(BF16)
        dp_ref[:, aw + kw:col0] = dva_ref[...].astype(BF16)
        dp_ref[:, col0 + 3 * rw:] = (d_rnn * oh * ng * (sgg * (1.0 + g_r * (1.0 - sgg)))).astype(BF16)
        d_on = d_rnn * gate
        dng_rows = jnp.sum(d_on * oh, axis=0, keepdims=True)
        dng = dng_rows[:, _head(0)]
        for j in range(1, hp):
            dng = dng + dng_rows[:, _head(j)]
        dng_ref[...] += dng
        dyg = d_on * ng
        do = r * (dyg - oh * per_head(dyg * oh, lambda x: jnp.mean(x, axis=-1, keepdims=True)))

        b_last = b[CHUNK - 1:CHUNK, :]
        eb = jnp.exp(b)
        tail = jnp.exp(b_last - b)
        kdec = k * tail
        decay = jnp.exp(b_last)
        qe = q * eb
        qcat, kcat, ers, ecs = _offdiag_factors(q, k, b)
        diag_mask = _diag_mask()
        dqs, dks, dvs, dads, gsums, dstates = [], [], [], [], [], []
        for j in range(hp):
            hs = _head(j)
            do_h, v_h, dst = do[:, hs], v[:, hs], dstate[j]
            da = jnp.where(row >= col, _dot(do_h, v_h, NT), 0.0)
            dads.append(jnp.where(diag_mask, da, 0.0))
            dq = _dot(do_h, st0_ref[j], NN) * eb[:, hs]
            dk = _dot(v_h, dst, NN) * tail[:, hs]
            dvs.append(_dot(att_ref[j], do_h, TN) + _dot(kdec[:, hs], dst, NT))
            rq = _dot(da, _cat_heads(kcat, hs), NN)
            rk = _dot(da, _cat_heads(qcat, hs), TN)
            for jj in range(nsub - 1):
                dq = dq + ers[jj][:, hs] * rq[:, _head(jj)]
                dk = dk + ecs[jj][:, hs] * rk[:, _head(jj)]
            dqs.append(dq)
            dks.append(dk)
            gsums.append(jnp.sum(dst * st1_ref[j], axis=0, keepdims=True))
            dstates.append(dst * decay[:, hs] + _dot(do_h, qe[:, hs], TN))
        for j in range(hp):
            dstate[j] = dstates[j]
        dq = jnp.concatenate(dqs, axis=1)
        dk = jnp.concatenate(dks, axis=1)
        rloc = lax.broadcasted_iota(jnp.int32, (CHUNK, w), 0) % SUB
        for rr in range(SUB):
            bs = _sub_rows(b_ref, rr)
            ks = _sub_rows(k_ref, rr)
            e = jnp.exp(jnp.where(rloc >= rr, b - bs, -jnp.inf))
            pick = (col % SUB) == rr
            dacol = jnp.concatenate(
                [jnp.broadcast_to(jnp.sum(jnp.where(pick, dads[j], 0.0), axis=-1, keepdims=True), (CHUNK, kd))
                 for j in range(hp)], axis=1)
            wv = dacol * e
            dq = dq + wv * ks
            sums = jnp.sum((wv * q).reshape(nsub, SUB, w), axis=1)
            for j in range(hp):
                dks_ref[j, pl.ds(rr, nsub, stride=SUB), :] = sums[:, _head(j)]
        dk = dk + jnp.concatenate([dks_ref[j] for j in range(hp)], axis=1)

        dlf = _cumsum_rows(q * dq - k * dk, reverse=True) + jnp.concatenate(gsums, axis=1)
        dfv = dlf / f - dk
        dp_ref[:, col0 + rw:col0 + 2 * rw] = (dfv * (1.0 - lbv) * sg * (1.0 - sg)).astype(BF16)
        dlb_ref[...] += jnp.sum(dfv * (1.0 - sg), axis=0, keepdims=True)
        dp_ref[:, col0:col0 + rw] = (dq * (sq * (1.0 + q_r * (1.0 - sq)))).astype(BF16)
        dp_ref[:, col0 + 2 * rw:col0 + 3 * rw] = jnp.concatenate(dvs, axis=1).astype(BF16)

    def rev(c):
        return nc - 1 - c

    def col_in(kidx, j):
        return pl.BlockSpec((CHUNK, kd), lambda hg, c: (rev(c), cb + kidx * nh + hg * hp + j))

    def rows(width):
        return pl.BlockSpec((CHUNK, width), lambda hg, c: (rev(c), 0))

    return _call(body, name="hgrn_bwd", grid=(1, nc),
                 in_specs=[col_in(kidx, j) for kidx in range(4) for j in range(hp)] +
                          [pl.BlockSpec((1, w), lambda hg, c: (0, hg)), pl.BlockSpec((1, kd), lambda hg, c: (0, 0)),
                           rows(w),
                           pl.BlockSpec((hp, CHUNK, CHUNK), lambda hg, c: (hg, rev(c), 0)),
                           pl.BlockSpec((None, hp, kd, kd), lambda hg, c: (rev(c), hg, 0, 0)),
                           pl.BlockSpec((None, hp, kd, kd),
                                        lambda hg, c: (jnp.minimum(rev(c) + 1, nc - 1), hg, 0, 0)),
                           pl.BlockSpec((CHUNK, w), lambda hg, c: (rev(c), dcb // hp + hg)),
                           rows(aw), rows(kw), rows(kw)],
                 out_specs=[rows(iw),
                            pl.BlockSpec((1, w), lambda hg, c: (0, hg)),
                            pl.BlockSpec((None, 1, kd), lambda hg, c: (hg, 0, 0))],
                 out_shape=[jax.ShapeDtypeStruct((t, iw), BF16), jax.ShapeDtypeStruct((1, rw), F32),
                            jax.ShapeDtypeStruct((1, 1, kd), F32)],
                 args=(*([proj] * (4 * hp)), lb, norm_gain, o_all, att_all, st_all, st_all, dcat, dq_a, dk_a, dv_a),
                 scratch_shapes=[pltpu.VMEM((hp, kd, kd), F32), pltpu.VMEM((hp, CHUNK, kd), F32),
                                 pltpu.VMEM((hp, CHUNK, kd), F32), pltpu.VMEM((hp, CHUNK, kd), F32)],
                 sem=("parallel", "arbitrary"), carry=carry)


def comm_only(name, part):
    return _call(lambda: None, name=name, grid=(), in_specs=[], out_specs=[], out_shape=[], args=(), carry=part)[1]


def exchange_halves(name, array, axis):
    return comm_only(name, _scatter_step(array, axis))[0]


ADD_BLOCK_ELEMS = 1 << 20
ADAMW_BLOCK_ELEMS = 1 << 19


def add_kept_half(name, kept, got, sel, minor, row0=0):
    pieces, rows, cols = got.shape
    tr = _tile(rows, max(16, ADD_BLOCK_ELEMS // cols), mult=16)
    assert row0 % tr == 0
    i0 = row0 // tr

    def body(sel_ref, k_ref, g_ref, o_ref):
        o_ref[...] = (k_ref[...].astype(F32) + g_ref[...].astype(F32)).astype(o_ref.dtype)

    kept_spec = (pl.BlockSpec((None, None, tr, cols), lambda p, i, s: (p, s[0], i + i0, 0)) if minor else
                 pl.BlockSpec((None, None, tr, cols), lambda p, i, s: (s[0], p, i + i0, 0)))
    return pl.pallas_call(
        body, name=name,
        grid_spec=pltpu.PrefetchScalarGridSpec(
            num_scalar_prefetch=1, grid=(pieces, rows // tr),
            in_specs=[kept_spec, pl.BlockSpec((None, tr, cols), lambda p, i, s: (p, i, 0))],
            out_specs=pl.BlockSpec((None, tr, cols), lambda p, i, s: (p, i, 0))),
        out_shape=jax.ShapeDtypeStruct(got.shape, got.dtype),
        compiler_params=_cparams(("parallel", "parallel")),
    )(sel, kept, got)


def _adamw(w, g, m, v):
    m = ADAM_B1 * m + (1.0 - ADAM_B1) * g
    v = ADAM_B2 * v + (1.0 - ADAM_B2) * (g * g)
    m_hat = m / (1.0 - ADAM_B1 ** ADAM_STEP)
    v_hat = v / (1.0 - ADAM_B2 ** ADAM_STEP)
    delta = -ADAM_LR * (m_hat / (jnp.sqrt(v_hat) + ADAM_EPS) + ADAM_WD * w)
    return delta, m, v


def add_adamw(name, kept, got, sel, w, m, v, row0=0):
    _, rows, cols = got.shape
    tr = _tile(rows, max(16, ADAMW_BLOCK_ELEMS // cols), mult=16)
    assert row0 % tr == 0
    i0 = row0 // tr

    def body(sel_ref, k_ref, g_ref, w_ref, m_ref, v_ref, go_ref, d_ref, mo_ref, vo_ref):
        g = k_ref[...].astype(F32) + g_ref[...].astype(F32)
        go_ref[...] = g
        d_ref[...], mo_ref[...], vo_ref[...] = _adamw(w_ref[...], g, m_ref[...], v_ref[...])

    tile = pl.BlockSpec((tr, cols), lambda i, s: (i, 0))
    shard_tile = pl.BlockSpec((tr, cols), lambda i, s: (i + i0, 0))
    return pl.pallas_call(
        body, name=name,
        grid_spec=pltpu.PrefetchScalarGridSpec(
            num_scalar_prefetch=1, grid=(rows // tr,),
            in_specs=[pl.BlockSpec((None, None, tr, cols), lambda i, s: (s[0], 0, i, 0)),
                      pl.BlockSpec((None, tr, cols), lambda i, s: (0, i, 0)), shard_tile, shard_tile, shard_tile],
            out_specs=[tile] * 4),
        out_shape=[jax.ShapeDtypeStruct((rows, cols), F32)] * 4,
        compiler_params=_cparams(("parallel",)),
    )(sel, kept, got, w, m, v)


def small_allreduce_adamw(partial, scale, w, m, v):
    rows = partial.shape[0]

    def body(p_ref, s_ref, w_ref, m_ref, v_ref, g_ref, d_ref, mo_ref, vo_ref, slots, send_sems, recv_sems):
        x, y, c = _coords()
        my_slot = _slab_index((x, y, c))
        slots[my_slot] = p_ref[...]
        copies = []
        for mask in range(1, N_DEV):
            to = tuple(1 - v_ if (mask >> s_) & 1 else v_ for v_, s_ in ((x, 2), (y, 1), (c, 0)))
            copies.append(pltpu.make_async_remote_copy(
                src_ref=p_ref, dst_ref=slots.at[my_slot],
                send_sem=send_sems.at[mask - 1], recv_sem=recv_sems.at[mask - 1],
                device_id=to, device_id_type=MESH))
        for cp in copies:
            cp.start()
        for cp in copies:
            cp.wait()
        total = slots[0]
        for b in range(1, N_DEV):
            total = total + slots[b]
        g = total * s_ref[...]
        g_ref[...] = g
        d_ref[...], mo_ref[...], vo_ref[...] = _adamw(w_ref[...], g, m_ref[...], v_ref[...])

    vm = pl.BlockSpec(memory_space=pltpu.VMEM)
    return pl.pallas_call(
        body, name="small_allreduce_adamw",
        in_specs=[vm] * 5, out_specs=[vm] * 4,
        out_shape=[jax.ShapeDtypeStruct((rows, LANES), F32)] * 4,
        scratch_shapes=[pltpu.VMEM((N_DEV, rows, LANES), F32),
                        pltpu.SemaphoreType.DMA((N_DEV - 1,)), pltpu.SemaphoreType.DMA((N_DEV - 1,))],
        compiler_params=pltpu.CompilerParams(has_side_effects=True),
    )(partial, scale, w, m, v)


_SMALL = ("attn_sinks", "attn_out_gain", "rnn_lb_logits", "rnn_norm_gain", "mix_pre_gain", "mix_post_gain",
          "mlp_pre_gain", "mlp_post_gain")


def _pack(parts):
    rows = []
    for p in parts:
        flat = p.reshape(-1).astype(F32)
        pad = (-flat.shape[0]) % LANES
        rows.append(jnp.pad(flat, (0, pad)).reshape(-1, LANES))
    packed = jnp.concatenate(rows, axis=0)
    pad_rows = (-packed.shape[0]) % 8
    return jnp.pad(packed, ((0, pad_rows), (0, 0)))


def _unpack(packed, shapes):
    out, r = [], 0
    for s in shapes:
        size = math.prod(s)
        nrows = -(-size // LANES)
        out.append(packed[r:r + nrows].reshape(-1)[:size].reshape(s))
        r += nrows
    return out


class _Scatter:
    def __init__(self, tag, grad, sels):
        self.tag, self.sels = tag, sels
        self.shape = grad.shape[1:]
        self.cur = grad.reshape(4, 2, *self.shape)
        self.stage = 0

    def step(self):
        return _scatter_step(self.cur, "cxy"[self.stage])

    def land(self, got, w=None, m=None, v=None):
        axis = "cxy"[self.stage]
        name = "rs_add_%s_%s" % (axis, self.tag)
        sel = self.sels[axis]
        self.stage += 1
        if axis == "y":
            return add_adamw(name, self.cur, got, sel, w, m, v)
        summed = add_kept_half(name, self.cur, got, sel, minor=axis == "c")
        self.cur = summed.reshape(2, summed.shape[0] // 2, *self.shape)
        return None


def scatter_both_links(tag, s1, sels, carrier, w, m, v):
    rows = s1.shape[2]
    half = rows // 2
    result, (got_a, got_b) = carrier(_merge(_scatter_step(s1, "x", rows=(0, half)),
                                            _scatter_step(s1, "y", minor=True, rows=(half, rows))))
    sa = add_kept_half("rs_add_x_%s_a" % tag, s1, got_a, sels["x"], minor=False)
    sb = add_kept_half("rs_add_y_%s_b" % tag, s1, got_b, sels["y"], minor=True, row0=half)
    sa, sb = (s.reshape(2, 1, *s.shape[1:]) for s in (sa, sb))
    got_a, got_b = comm_only("rs_exchange_last_%s" % tag, _merge(_scatter_step(sa, "y"), _scatter_step(sb, "x")))
    out_a = add_adamw("rs_add_y_%s_a" % tag, sa, got_a, sels["y"], w, m, v)
    out_b = add_adamw("rs_add_x_%s_b" % tag, sb, got_b, sels["x"], w, m, v, row0=half)
    return result, [jnp.concatenate([a, b], axis=0) for a, b in zip(out_a, out_b)]


def kernel(x, w_in, attn_sinks, attn_out_gain, rnn_lb_logits, rnn_norm_gain, w_out, mix_pre_gain, mix_post_gain, mlp_pre_gain, mlp_post_gain, w_up, w_down, loss_target, m_w_in, m_attn_sinks, m_attn_out_gain, m_rnn_lb_logits, m_rnn_norm_gain, m_w_out, m_mix_pre_gain, m_mix_post_gain, m_mlp_pre_gain, m_mlp_post_gain, m_w_up, m_w_down, v_w_in, v_attn_sinks, v_attn_out_gain, v_rnn_lb_logits, v_rnn_norm_gain, v_w_out, v_mix_pre_gain, v_mix_post_gain, v_mlp_pre_gain, v_mlp_post_gain, v_w_up, v_w_down):
    xs, target = x[0], loss_target[0]
    t, d = xs.shape
    aw = d // 2
    rw = d - aw
    col0 = aw + 2 * N_KV_HEADS * HEAD_DIM
    small_w = dict(attn_sinks=attn_sinks, attn_out_gain=attn_out_gain, rnn_lb_logits=rnn_lb_logits,
                   rnn_norm_gain=rnn_norm_gain, mix_pre_gain=mix_pre_gain, mix_post_gain=mix_post_gain,
                   mlp_pre_gain=mlp_pre_gain, mlp_post_gain=mlp_post_gain)
    small_m = dict(attn_sinks=m_attn_sinks, attn_out_gain=m_attn_out_gain, rnn_lb_logits=m_rnn_lb_logits,
                   rnn_norm_gain=m_rnn_norm_gain, mix_pre_gain=m_mix_pre_gain, mix_post_gain=m_mix_post_gain,
                   mlp_pre_gain=m_mlp_pre_gain, mlp_post_gain=m_mlp_post_gain)
    small_v = dict(attn_sinks=v_attn_sinks, attn_out_gain=v_attn_out_gain, rnn_lb_logits=v_rnn_lb_logits,
                   rnn_norm_gain=v_rnn_norm_gain, mix_pre_gain=v_mix_pre_gain, mix_post_gain=v_mix_post_gain,
                   mlp_pre_gain=v_mlp_pre_gain, mlp_post_gain=v_mlp_post_gain)
    cx, cy, cc = _coords()
    sels = {a: jnp.reshape(v_, (1,)).astype(jnp.int32) for a, v_ in (("x", cx), ("y", cy), ("c", cc))}

    w_in_t, m_in_t, v_in_t = w_in[0].T, m_w_in[0].T, v_w_in[0].T
    s_in, s_out, s_up, s_down = (w.astype(BF16) for w in (w_in_t, w_out[0], w_up[0], w_down[0]))
    probs = jax.nn.softmax(rnn_lb_logits.astype(F32), axis=0)
    lb = probs[0:1]

    (h1,), (wint_half,) = pre_norm(xs, mix_pre_gain, carry=_gather_first(s_in))
    wint = comm_only("gather_second_w_in", _gather_second(wint_half))[0].reshape(-1, d)
    up_rows = s_up.shape[0]
    proj, (wup_part,) = mm_nt("in_proj", h1, wint, F32, carry=_gather_first(s_up, rows=(0, up_rows // 2)))
    (attn_o, attn_n), (wup_half, wout_half) = attn_fwd(
        proj, attn_sinks, attn_out_gain, aw,
        carry=_merge(_gather_first(s_up, rows=(up_rows // 2, up_rows), into=wup_part), _gather_first(s_out)))
    (cat, o_r, att, st), (wup, wout, wdown_half) = hgrn_fwd(
        proj, attn_n, lb, rnn_norm_gain, col0, rw,
        carry=_merge(_gather_second(wup_half), _gather_second(wout_half), _gather_first(s_down)))
    wout = wout.reshape(-1, d)
    mixed, (wdown,) = mm_nn("out_proj", cat, wout, F32, carry=_gather_second(wdown_half))
    wdown = wdown.reshape(-1, d)
    x1, h2 = mid_fwd(mixed, mix_post_gain, xs, mlp_pre_gain)
    u = up_proj(h2, wup)
    y = down_proj(u, wdown)
    sse, dout, dy, dg_mlppost = loss_bwd(y, mlp_post_gain, x1, target)

    du = down_bwd_act(dy, wdown, u)
    rs_down = _Scatter("down", down_wgrad(u, dy).reshape(N_DEV, -1, d), sels)
    dh2, (got,) = up_bwd_x(du, wup, carry=rs_down.step())
    rs_down.land(got)
    dwup, (got,) = up_wgrad(h2, du, carry=rs_down.step())
    rs_down.land(got)
    rs_up = _Scatter("up", dwup, sels)
    (dx1, dmixed, dg_mlppre, dg_mixpost), (got,) = mid_bwd(dh2, x1, mlp_pre_gain, dout, mixed, mix_post_gain,
                                                          carry=rs_up.step())
    rs_up.land(got)
    dcat = mm_nt("out_bwd_x", dmixed, wout, F32)
    rs_out = _Scatter("out", mm_tn("out_wgrad", cat, dmixed, BF16).reshape(N_DEV, -1, d), sels)
    (dq_a, dk_a, dv_a, dsinks, daog), (got_d, got_o) = attn_bwd(
        proj, attn_sinks, attn_out_gain, attn_o, dcat, aw, carry=_merge(rs_down.step(), rs_out.step()))
    out_down = rs_down.land(got_d, w_down[0], m_w_down[0], v_w_down[0])
    rs_out.land(got_o)
    (dproj, dlb, dng), (got_u, got_o) = hgrn_bwd(
        proj, lb, rnn_norm_gain, o_r, att, st, dcat, dq_a, dk_a, dv_a, col0, rw,
        carry=_merge(rs_up.step(), rs_out.step()))
    rs_up.land(got_u)
    rs_out.land(got_o)
    dwin, (got_u, got_o) = mm_tn("in_wgrad", dproj, h1, BF16, carry=_merge(rs_up.step(), rs_out.step()))
    out_up = rs_up.land(got_u, w_up[0], m_w_up[0], v_w_up[0])
    out_out = rs_out.land(got_o, w_out[0], m_w_out[0], v_w_out[0])
    rs_in = _Scatter("in", dwin.reshape(N_DEV, -1, d), sels)
    rs_in.land(exchange_halves("rs_exchange_c_in", rs_in.cur, "c"))
    dh1, out_in = scatter_both_links(
        "in", rs_in.cur, sels, lambda part: mm_nn("in_bwd_x", dproj, wint, F32, tm=MM_TILE // 2, carry=part),
        w_in_t, m_in_t, v_in_t)
    grad_x, dg_mixpre = first_bwd(dh1, xs, mix_pre_gain, dx1)
    big_out = [out_in, out_out, out_up, out_down]

    n_heads = attn_sinks.shape[1]
    jac = probs[0] * probs[1]
    partial = _pack([sse, dsinks[0, :n_heads], daog, jnp.stack([dlb[0], dlb[0]]), jnp.sum(dng, axis=0),
                     dg_mixpre, dg_mixpost, dg_mlppre, dg_mlppost])
    ones = [jnp.ones(small_w[k].shape, F32) for k in _SMALL]
    ones[2] = jnp.stack([jac, -jac])
    scale = _pack([jnp.full((1,), 0.5 / d, F32)] + ones)
    zero = jnp.zeros((1,), F32)
    outs = small_allreduce_adamw(partial, scale, _pack([zero] + [small_w[k] for k in _SMALL]),
                                 _pack([zero] + [small_m[k] for k in _SMALL]),
                                 _pack([jnp.ones((1,), F32)] + [small_v[k] for k in _SMALL]))
    shapes = [(1,)] + [small_w[k].shape for k in _SMALL]
    sgrad, sdelta, snm, snv = (_unpack(o, shapes) for o in outs)
    loss = sgrad[0][0]

    def big(i, j):
        o = big_out[i][j]
        return (o.T if i == 0 else o)[None]

    def ordered(j, smalls):
        s = dict(zip(_SMALL, smalls[1:]))
        return [big(0, j), s["attn_sinks"], s["attn_out_gain"], s["rnn_lb_logits"], s["rnn_norm_gain"], big(1, j),
                s["mix_pre_gain"], s["mix_post_gain"], s["mlp_pre_gain"], s["mlp_post_gain"], big(2, j), big(3, j)]

    return (loss, grad_x[None], *ordered(0, sgrad), *ordered(1, sdelta), *ordered(2, snm), *ordered(3, snv))
```

```python
import math

import jax
import jax.numpy as jnp
from jax import lax
from jax.experimental import pallas as pl
from jax.experimental.pallas import tpu as pltpu

F32 = jnp.float32
BF16 = jnp.bfloat16

HEAD_DIM = 64
N_KV_HEADS = 2
BLOCK = 128
RNN_HEAD_DIM = 128
CHUNK = 64
SUB_FWD = 16
SUB_BWD = 16
EPS = 1e-6

ADAM_LR = 0.001
ADAM_B1 = 0.9
ADAM_B2 = 0.999
ADAM_EPS = 1e-08
ADAM_WD = 0.01
ADAM_STEP = 10

N_DEV = 8
LANES = 128
V7X_VMEM_LIMIT = 56 * 1024 * 1024
MESH = pl.DeviceIdType.MESH
HI = lax.Precision.HIGHEST
ANY = pl.BlockSpec(memory_space=pl.ANY)
_AXES = ("x", "y", "c")


def _cparams(sem=None, **kw):
    return pltpu.CompilerParams(dimension_semantics=sem, vmem_limit_bytes=V7X_VMEM_LIMIT, **kw)


def _dot(a, b, dims):
    return lax.dot_general(a.astype(BF16), b.astype(BF16), (dims, ((), ())), preferred_element_type=F32)


NN = ((1,), (0,))
NT = ((1,), (1,))
TN = ((0,), (0,))


def _pick(n, pref):
    t = min(n, pref)
    while n % t:
        t //= 2
    return t


def _tile(n, pref, mult=LANES):
    if n <= pref:
        return n
    t = pref - pref % mult
    while n % t:
        t -= mult
    return t


def _coords():
    return lax.axis_index("x"), lax.axis_index("y"), lax.axis_index("c")


def _slab_index(dev):
    return 4 * dev[0] + 2 * dev[1] + dev[2]


class _Part:
    def __init__(self, operands, landings, aliases, n_sems, plan):
        self.operands, self.landings, self.aliases, self.n_sems, self.plan = operands, landings, aliases, n_sems, plan


def _merge(*parts):
    operands, landings, aliases, plans = [], [], {}, []
    s0 = 0
    for p in parts:
        o0, l0 = len(operands), len(landings)
        aliases.update({o0 + i: l0 + j for i, j in p.aliases.items()})
        plans.append((p.plan, o0, len(p.operands), l0, len(p.landings), s0))
        operands += p.operands
        landings += p.landings
        s0 += p.n_sems

    def plan(ops, lands, sem):
        starts, waits = [], []
        for f, o0, no, l0, nl, off in plans:
            s, w = f(ops[o0:o0 + no], lands[l0:l0 + nl], lambda kind, k, off=off: sem(kind, off + k))
            starts += s
            waits += w
        return starts, waits

    return _Part(operands, landings, aliases, s0, plan)


def _gather_peers(x, y, c):
    return [(x, y, 1 - c), (1 - x, y, c), (x, 1 - y, c), (1 - x, 1 - y, c)]


def _gather_first(shard, rows=None, into=None, diagonal=True):
    lo, hi = (0, shard.shape[0]) if rows is None else rows
    n_peers = 4 if diagonal else 3

    def plan(ops, lands, sem):
        x, y, c = _coords()
        me, peers = (x, y, c), _gather_peers(x, y, c)[:n_peers]
        src = ops[0].at[pl.ds(lo, hi - lo)]

        def slab(block):
            return lands[0].at[_slab_index(block), pl.ds(lo, hi - lo)]

        def cp(k, block, to):
            return pltpu.make_async_remote_copy(
                src_ref=src, dst_ref=slab(block),
                send_sem=sem(0, k), recv_sem=sem(1, k), device_id=to, device_id_type=MESH)

        local = pltpu.make_async_copy(src, slab(me), sem(2, 0))
        sends = [cp(k, me, to) for k, to in enumerate(peers)]
        recvs = [cp(k, frm, me) for k, frm in enumerate(peers)]
        return ([local.start] + [s.start for s in sends],
                [local.wait] + [s.wait_send for s in sends] + [r.wait_recv for r in recvs])

    landing = jax.ShapeDtypeStruct((N_DEV, *shard.shape), shard.dtype)
    if into is None:
        return _Part([shard], [landing], {}, 4, plan)
    return _Part([shard, into], [landing], {1: 0}, 4, plan)


def _flip(dev, flips):
    return tuple(1 - v if f else v for v, f in zip(dev, flips))


def _pass_slabs(gathered, moves, then=()):
    def wave(lands, sem, k0, wave_moves):
        me = _coords()
        sends, recvs = [], []
        for k, (block, dest, rows) in enumerate(wave_moves, start=k0):
            lo, hi = (0, gathered.shape[1]) if rows is None else rows

            def cp(blk, to, k=k, lo=lo, hi=hi):
                slab = lands[0].at[_slab_index(blk), pl.ds(lo, hi - lo)]
                return pltpu.make_async_remote_copy(
                    src_ref=slab, dst_ref=slab, send_sem=sem(0, k), recv_sem=sem(1, k),
                    device_id=to, device_id_type=MESH)

            sends.append(cp(_flip(me, block), _flip(me, dest)))
            recvs.append(cp(_flip(_flip(me, dest), block), me))
        return [s.start for s in sends], [s.wait_send for s in sends] + [r.wait_recv for r in recvs]

    def plan(ops, lands, sem):
        starts, waits = wave(lands, sem, 0, moves)
        if then:
            starts2, waits2 = wave(lands, sem, len(moves), then)
            waits = waits + starts2 + waits2
        return starts, waits

    return _Part([gathered], [jax.ShapeDtypeStruct(gathered.shape, gathered.dtype)], {0: 0},
                 len(moves) + len(then), plan)


_X, _Y, _C, _XY = (1, 0, 0), (0, 1, 0), (0, 0, 1), (1, 1, 0)


def _gather_second(gathered):
    def plan(ops, lands, sem):
        x, y, c = _coords()
        sibling = (x, y, 1 - c)
        chips = [(1 - x, y), (x, 1 - y), (1 - x, 1 - y)]

        def cp(k, block):
            slab = lands[0].at[_slab_index(block)]
            return pltpu.make_async_remote_copy(
                src_ref=slab, dst_ref=slab, send_sem=sem(0, k), recv_sem=sem(1, k),
                device_id=sibling, device_id_type=MESH)

        sends = [cp(k, (*chip, c)) for k, chip in enumerate(chips)]
        recvs = [cp(k, (*chip, 1 - c)) for k, chip in enumerate(chips)]
        return [s.start for s in sends], [s.wait_send for s in sends] + [r.wait_recv for r in recvs]

    return _Part([gathered], [jax.ShapeDtypeStruct(gathered.shape, gathered.dtype)], {0: 0}, 3, plan)


def _scatter_step(array, axis, minor=None, rows=None):
    minor = (axis == "c") if minor is None else minor
    pieces = array.shape[0] if minor else array.shape[1]
    lo, hi = (0, array.shape[2]) if rows is None else rows

    def plan(ops, lands, sem):
        coords = list(_coords())
        ai = _AXES.index(axis)
        mine = coords[ai]
        peer = list(coords)
        peer[ai] = 1 - mine
        cps = []
        for p in range(pieces):
            src = ops[0].at[p, 1 - mine, pl.ds(lo, hi - lo)] if minor else ops[0].at[1 - mine, p, pl.ds(lo, hi - lo)]
            cps.append(pltpu.make_async_remote_copy(
                src_ref=src, dst_ref=lands[0].at[p], send_sem=sem(0, p), recv_sem=sem(1, p),
                device_id=tuple(peer), device_id_type=MESH))
        return [cp.start for cp in cps], [cp.wait for cp in cps]

    return _Part([array], [jax.ShapeDtypeStruct((pieces, hi - lo, array.shape[3]), array.dtype)], {}, pieces, plan)


def _grid_edges(grid):
    first = last = None
    for ax, n in enumerate(grid):
        p = pl.program_id(ax)
        f, l = p == 0, p == n - 1
        first = f if first is None else jnp.logical_and(first, f)
        last = l if last is None else jnp.logical_and(last, l)
    return first, last


def _call(body, *, name, grid, in_specs, out_specs, out_shape, args, scratch_shapes=(), sem=None, carry=None):
    if carry is None:
        return pl.pallas_call(
            body, name=name, grid=grid, in_specs=list(in_specs), out_specs=list(out_specs),
            out_shape=list(out_shape), scratch_shapes=list(scratch_shapes), compiler_params=_cparams(sem),
        )(*args)
    n_in, n_out, n_scr = len(in_specs), len(out_specs), len(scratch_shapes)
    n_cin, n_cout = len(carry.operands), len(carry.landings)

    def wrapped(*refs):
        ins, cins = refs[:n_in], refs[n_in:n_in + n_cin]
        o0 = n_in + n_cin
        outs, couts = refs[o0:o0 + n_out], refs[o0 + n_out:o0 + n_out + n_cout]
        s0 = o0 + n_out + n_cout
        scr, sems = refs[s0:s0 + n_scr], refs[s0 + n_scr:]
        first, last = _grid_edges(grid)

        def plan():
            return carry.plan(cins, couts, lambda kind, k: sems[kind].at[k])

        def start_all():
            for start in plan()[0]:
                start()

        def wait_all():
            for wait in plan()[1]:
                wait()

        if grid:
            pl.when(first)(start_all)
            body(*ins, *outs, *scr)
            pl.when(last)(wait_all)
        else:
            start_all()
            body(*ins, *outs, *scr)
            wait_all()

    sem_arrays = [pltpu.SemaphoreType.DMA((carry.n_sems,))] * 3
    res = pl.pallas_call(
        wrapped, name=name, grid=grid,
        in_specs=[*in_specs, *[ANY] * n_cin], out_specs=[*out_specs, *[ANY] * n_cout],
        out_shape=[*out_shape, *carry.landings],
        scratch_shapes=[*scratch_shapes, *sem_arrays],
        input_output_aliases={n_in + i: n_out + j for i, j in carry.aliases.items()},
        compiler_params=_cparams(("arbitrary",) * len(grid) if grid else None, has_side_effects=True),
    )(*args, *carry.operands)
    return res[:n_out], res[n_out:]


MM_TILE = 1024
MM_K_TILE = 2048
MXU_COLS = 256


def _matmul(name, a, b, dims, grid, a_spec, b_spec, out_shape, out_spec, epilogue,
            extras=(), extra_specs=(), prologue=None, carry=None):
    nk = grid[2]
    n_extra = len(extras)
    acc_shape = out_spec.block_shape[-2:]

    def lhs(a_ref):
        return a_ref[...] if prologue is None else prologue(a_ref[...])

    def body_one(a_ref, b_ref, *rest):
        epilogue(_dot(lhs(a_ref), b_ref[...], dims), rest[:n_extra], rest[n_extra:])

    def body_acc(a_ref, b_ref, *rest):
        acc = rest[-1]
        k = pl.program_id(2)
        part = _dot(lhs(a_ref), b_ref[...], dims)

        @pl.when(k == 0)
        def _():
            acc[...] = part

        @pl.when(k > 0)
        def _():
            acc[...] += part

        @pl.when(k == nk - 1)
        def _():
            epilogue(acc[...], rest[:n_extra], rest[n_extra:-1])

    res = _call(body_one if nk == 1 else body_acc, name=name, grid=grid,
                in_specs=[a_spec, b_spec, *extra_specs], out_specs=[out_spec], out_shape=[out_shape],
                args=(a, b, *extras), scratch_shapes=[] if nk == 1 else [pltpu.VMEM(acc_shape, F32)],
                sem=("parallel", "parallel", "arbitrary"), carry=carry)
    return res[0] if carry is None else (res[0][0], res[1])


def _store_as(acc, extra_refs, out_refs):
    out_refs[0][...] = acc.astype(out_refs[0].dtype)


def _square(u):
    return u * u


def mm_nn(name, a, b, out_dtype, tk=None, tm=MM_TILE, prologue=None, carry=None):
    (m, kk), n = a.shape, b.shape[1]
    tm, tn = _tile(m, tm), _tile(n, MM_TILE, mult=MXU_COLS)
    tk = kk if tk is None else _tile(kk, tk, mult=MXU_COLS)
    return _matmul(name, a, b, NN, (m // tm, n // tn, kk // tk),
                   pl.BlockSpec((tm, tk), lambda i, j, k: (i, k)),
                   pl.BlockSpec((tk, tn), lambda i, j, k: (k, j)),
                   jax.ShapeDtypeStruct((m, n), out_dtype),
                   pl.BlockSpec((tm, tn), lambda i, j, k: (i, j)), _store_as, prologue=prologue, carry=carry)


def mm_nt(name, a, b, out_dtype, epilogue=_store_as, extras=(), extra_specs=(), carry=None):
    (m, kk), n = a.shape, b.shape[0]
    tm, tn = _tile(m, MM_TILE), _tile(n, MM_TILE, mult=MXU_COLS)
    return _matmul(name, a, b, NT, (m // tm, n // tn, 1),
                   pl.BlockSpec((tm, kk), lambda i, j, k: (i, 0)),
                   pl.BlockSpec((tn, kk), lambda i, j, k: (j, 0)),
                   jax.ShapeDtypeStruct((m, n), out_dtype),
                   pl.BlockSpec((tm, tn), lambda i, j, k: (i, j)), epilogue,
                   extras=extras, extra_specs=extra_specs, carry=carry)


def mm_tn(name, a, b, out_dtype, prologue=None, carry=None):
    (kk, m), n = a.shape, b.shape[1]
    tm, tn, tk = _tile(m, MM_TILE), _tile(n, MM_TILE), _tile(kk, MM_K_TILE)
    return _matmul(name, a, b, TN, (m // tm, n // tn, kk // tk),
                   pl.BlockSpec((tk, tm), lambda i, j, k: (k, i)),
                   pl.BlockSpec((tk, tn), lambda i, j, k: (k, j)),
                   jax.ShapeDtypeStruct((m, n), out_dtype),
                   pl.BlockSpec((tm, tn), lambda i, j, k: (i, j)), _store_as, prologue=prologue, carry=carry)


def up_proj(h2, wup_slabs):
    (m, kk), (_, _, ns) = h2.shape, wup_slabs.shape
    tm, tn = _tile(m, MM_TILE), _tile(ns, MM_TILE)
    r = ns // tn
    n = N_DEV * ns

    def epi(acc, extra_refs, out_refs):
        out_refs[0][...] = jnp.maximum(acc, 0.0).astype(BF16)

    return _matmul("up_proj", h2, wup_slabs, NN, (m // tm, n // tn, 1),
                   pl.BlockSpec((tm, kk), lambda i, j, k: (i, 0)),
                   pl.BlockSpec((None, kk, tn), lambda i, j, k: (j // r, 0, j % r)),
                   jax.ShapeDtypeStruct((m, n), BF16),
                   pl.BlockSpec((tm, tn), lambda i, j, k: (i, j)), epi)


def down_proj(u, wdown):
    return mm_nn("down_proj", u, wdown, F32, tk=MM_K_TILE, prologue=_square)


def down_bwd_act(dy, wdown, u):
    tm, tn = _tile(dy.shape[0], MM_TILE), _tile(wdown.shape[0], MM_TILE)

    def epi(acc, extra_refs, out_refs):
        out_refs[0][...] = (acc * (2.0 * extra_refs[0][...].astype(F32))).astype(BF16)

    return mm_nt("down_bwd_act", dy, wdown, BF16, epilogue=epi, extras=(u,),
                 extra_specs=(pl.BlockSpec((tm, tn), lambda i, j, k: (i, j)),))


def down_wgrad(u, dy):
    return mm_tn("down_wgrad", u, dy, BF16, prologue=_square)


def up_bwd_x(du, wup_slabs, carry=None):
    (m, kk), (_, n, ns) = du.shape, wup_slabs.shape
    tm, tk = _tile(m, MM_TILE), _tile(ns, MM_TILE)
    r = ns // tk
    return _matmul("up_bwd_x", du, wup_slabs, NT, (m // tm, 1, kk // tk),
                   pl.BlockSpec((tm, tk), lambda i, j, k: (i, k)),
                   pl.BlockSpec((None, n, tk), lambda i, j, k: (k // r, 0, k % r)),
                   jax.ShapeDtypeStruct((m, n), F32),
                   pl.BlockSpec((tm, n), lambda i, j, k: (i, 0)), _store_as, carry=carry)


def up_wgrad(h2, du, carry=None):
    (kk, m), n = h2.shape, du.shape[1]
    ns = n // N_DEV
    tm, tn, tk = _tile(m, MM_TILE), _tile(ns, MM_TILE), _tile(kk, MM_K_TILE)
    r = ns // tn
    return _matmul("up_wgrad", h2, du, TN, (m // tm, n // tn, kk // tk),
                   pl.BlockSpec((tk, tm), lambda i, j, k: (k, i)),
                   pl.BlockSpec((tk, tn), lambda i, j, k: (k, j)),
                   jax.ShapeDtypeStruct((N_DEV, m, ns), BF16),
                   pl.BlockSpec((None, tm, tn), lambda i, j, k: (j // r, i, j % r)), _store_as, carry=carry)


def _rstd(x):
    return lax.rsqrt(jnp.mean(x * x, axis=-1, keepdims=True) + EPS)


def _norm_bwd(x, g, dy):
    r = _rstd(x)
    xh = x * r
    dyg = dy * g
    dx = r * (dyg - xh * jnp.mean(dyg * xh, axis=-1, keepdims=True))
    return dx, jnp.sum(dy * xh, axis=0, keepdims=True)


def _row_spec(tr, d):
    return pl.BlockSpec((tr, d), lambda i: (i, 0))


def _vec_spec(d):
    return pl.BlockSpec((1, d), lambda i: (0, 0))


def _accum(ref, val):
    @pl.when(pl.program_id(0) == 0)
    def _():
        ref[...] = jnp.zeros_like(ref)

    ref[...] += val


def pre_norm(x, g, carry=None, tr=256):
    t, d = x.shape
    tr = _pick(t, tr)

    def body(x_ref, g_ref, h_ref):
        xx = x_ref[...]
        h_ref[...] = (xx * _rstd(xx) * g_ref[...]).astype(BF16)

    return _call(body, name="pre_norm", grid=(t // tr,),
                 in_specs=[_row_spec(tr, d), _vec_spec(d)], out_specs=[_row_spec(tr, d)],
                 out_shape=[jax.ShapeDtypeStruct((t, d), BF16)], args=(x, g), sem=("parallel",), carry=carry)


def mid_fwd(mixed, g_post, x, g_pre2, tr=256):
    t, d = x.shape
    tr = _pick(t, tr)

    def body(m_ref, gp_ref, x_ref, g2_ref, x1_ref, h2_ref):
        mm = m_ref[...]
        x1 = x_ref[...] + mm * _rstd(mm) * gp_ref[...]
        x1_ref[...] = x1
        h2_ref[...] = (x1 * _rstd(x1) * g2_ref[...]).astype(BF16)

    return _call(body, name="mid_fwd", grid=(t // tr,),
                 in_specs=[_row_spec(tr, d), _vec_spec(d), _row_spec(tr, d), _vec_spec(d)],
                 out_specs=[_row_spec(tr, d), _row_spec(tr, d)],
                 out_shape=[jax.ShapeDtypeStruct((t, d), F32), jax.ShapeDtypeStruct((t, d), BF16)],
                 args=(mixed, g_post, x, g_pre2), sem=("parallel",))


def loss_bwd(y, g_post2, x1, target, tr=256):
    t, d = y.shape
    tr = _pick(t, tr)

    def body(y_ref, g_ref, x1_ref, t_ref, sse_ref, dout_ref, dy_ref, dg_ref):
        yy = y_ref[...]
        g = g_ref[...]
        err = x1_ref[...] + yy * _rstd(yy) * g - t_ref[...]
        _accum(sse_ref, jnp.sum(jnp.sum(err * err, axis=1, keepdims=True), axis=0, keepdims=True))
        dout = err * (1.0 / d)
        dout_ref[...] = dout
        dy, dg = _norm_bwd(yy, g, dout)
        dy_ref[...] = dy.astype(BF16)
        _accum(dg_ref, dg)

    return _call(body, name="loss_bwd", grid=(t // tr,),
                 in_specs=[_row_spec(tr, d), _vec_spec(d), _row_spec(tr, d), _row_spec(tr, d)],
                 out_specs=[pl.BlockSpec((1, 1), lambda i: (0, 0)), _row_spec(tr, d), _row_spec(tr, d), _vec_spec(d)],
                 out_shape=[jax.ShapeDtypeStruct((1, 1), F32), jax.ShapeDtypeStruct((t, d), F32),
                            jax.ShapeDtypeStruct((t, d), BF16), jax.ShapeDtypeStruct((1, d), F32)],
                 args=(y, g_post2, x1, target), sem=("arbitrary",))


def mid_bwd(dh2, x1, g_pre2, dout, mixed, g_post, carry=None, tr=256):
    t, d = x1.shape
    tr = _pick(t, tr)

    def body(dh_ref, x1_ref, g2_ref, do_ref, m_ref, gp_ref, dx1_ref, dm_ref, dg2_ref, dgp_ref):
        d1, dg2 = _norm_bwd(x1_ref[...], g2_ref[...], dh_ref[...])
        dx1 = do_ref[...] + d1
        dx1_ref[...] = dx1
        dm, dgp = _norm_bwd(m_ref[...], gp_ref[...], dx1)
        dm_ref[...] = dm.astype(BF16)
        _accum(dg2_ref, dg2)
        _accum(dgp_ref, dgp)

    return _call(body, name="mid_bwd", grid=(t // tr,),
                 in_specs=[_row_spec(tr, d), _row_spec(tr, d), _vec_spec(d), _row_spec(tr, d), _row_spec(tr, d),
                           _vec_spec(d)],
                 out_specs=[_row_spec(tr, d), _row_spec(tr, d), _vec_spec(d), _vec_spec(d)],
                 out_shape=[jax.ShapeDtypeStruct((t, d), F32), jax.ShapeDtypeStruct((t, d), BF16),
                            jax.ShapeDtypeStruct((1, d), F32), jax.ShapeDtypeStruct((1, d), F32)],
                 args=(dh2, x1, g_pre2, dout, mixed, g_post), sem=("arbitrary",), carry=carry)


def first_bwd(dh1, x, g_pre, dx1, carry=None, tr=256):
    t, d = x.shape
    tr = _pick(t, tr)

    def body(dh_ref, x_ref, g_ref, dx1_ref, gx_ref, dg_ref):
        d0, dg = _norm_bwd(x_ref[...], g_ref[...], dh_ref[...])
        gx_ref[...] = dx1_ref[...] + d0
        _accum(dg_ref, dg)

    return _call(body, name="first_bwd", grid=(t // tr,),
                 in_specs=[_row_spec(tr, d), _row_spec(tr, d), _vec_spec(d), _row_spec(tr, d)],
                 out_specs=[_row_spec(tr, d), _vec_spec(d)],
                 out_shape=[jax.ShapeDtypeStruct((t, d), F32), jax.ShapeDtypeStruct((1, d), F32)],
                 args=(dh1, x, g_pre, dx1), sem=("arbitrary",), carry=carry)


def _attn_geometry(has_prev):
    r = lax.broadcasted_iota(jnp.int32, (BLOCK, 2 * BLOCK), 0)
    c = lax.broadcasted_iota(jnp.int32, (BLOCK, 2 * BLOCK), 1)
    dist = r + BLOCK - c
    valid = jnp.logical_and(jnp.logical_and(dist >= 0, dist < BLOCK), jnp.logical_or(c >= BLOCK, has_prev))
    return dist.astype(F32), valid


def _stack_pairs(x, g, pairs):
    base = g * pairs * LANES
    return jnp.concatenate([x[:, base + p * LANES:base + (p + 1) * LANES] for p in range(pairs)], axis=0)


def _unstack_pairs(xs, pairs):
    return jnp.concatenate([xs[p * BLOCK:(p + 1) * BLOCK, :] for p in range(pairs)], axis=1)


def _to_half(x, g, odd):
    lane = lax.broadcasted_iota(jnp.int32, x.shape, 1)
    y = x if (g == 1) == odd else pltpu.roll(x, HEAD_DIM, axis=1)
    return jnp.where((lane >= HEAD_DIM) == odd, y, 0.0)


def _from_halves(even, odd, g):
    lane = lax.broadcasted_iota(jnp.int32, even.shape, 1)
    if g == 0:
        return jnp.where(lane < HEAD_DIM, even + pltpu.roll(odd, HEAD_DIM, axis=1), 0.0)
    return jnp.where(lane >= HEAD_DIM, pltpu.roll(even, HEAD_DIM, axis=1) + odd, 0.0)


_PARITIES = [(g, odd) for g in range(N_KV_HEADS) for odd in (False, True)]


def _softmax_sink(s, sink_ref, g, odd, group, n_heads, geo):
    dist, valid = geo
    pairs = group // 2
    heads = [g * group + 2 * p + int(odd) for p in range(pairs)]
    bias = jnp.concatenate([(2.0 ** (-8.0 * (h + 1) / n_heads)) * dist for h in heads], axis=0)
    sink = jnp.concatenate([jnp.full((BLOCK, 1), sink_ref[0, h], F32) for h in heads], axis=0)
    s = jnp.where(jnp.concatenate([valid] * pairs, axis=0), s - bias, -jnp.inf)
    m = jnp.maximum(jnp.max(s, axis=-1, keepdims=True), sink)
    p = jnp.exp(s - m)
    p_sink = jnp.exp(sink - m)
    inv = 1.0 / (jnp.sum(p, axis=-1, keepdims=True) + p_sink)
    return p * inv, p_sink * inv


def attn_fwd(proj, sinks, gain, aw, carry=None):
    t = proj.shape[0]
    kw = N_KV_HEADS * HEAD_DIM
    n_heads = aw // HEAD_DIM
    group = n_heads // N_KV_HEADS
    pairs = group // 2
    assert kw == LANES and group % 2 == 0
    nb = t // BLOCK
    scale = HEAD_DIM ** -0.5

    def body(sink_ref, q_ref, k_ref, v_ref, g_ref, o_ref, on_ref):
        n = pl.program_id(0)
        cur = pl.multiple_of(n * BLOCK, BLOCK)
        prev = pl.multiple_of(jnp.maximum(n - 1, 0) * BLOCK, BLOCK)
        geo = _attn_geometry(n > 0)
        kcat = jnp.concatenate([k_ref[pl.ds(prev, BLOCK), :], k_ref[pl.ds(cur, BLOCK), :]], axis=0)
        vcat = jnp.concatenate([v_ref[pl.ds(prev, BLOCK), :], v_ref[pl.ds(cur, BLOCK), :]], axis=0)
        q = q_ref[...] * scale
        qs = [_stack_pairs(q, g, pairs) for g in range(N_KV_HEADS)]
        scores = [_dot(qs[g], _to_half(kcat, g, odd), NT) for g, odd in _PARITIES]
        probs = [_softmax_sink(s, sink_ref, g, odd, group, n_heads, geo)[0] for s, (g, odd) in zip(scores, _PARITIES)]
        outs = [_dot(p, _to_half(vcat, g, odd), NN) for p, (g, odd) in zip(probs, _PARITIES)]
        o = jnp.concatenate([_unstack_pairs(outs[2 * g] + outs[2 * g + 1], pairs) for g in range(N_KV_HEADS)], axis=1)
        o_ref[...] = o
        on_ref[...] = (o * _rstd(o) * g_ref[...]).astype(BF16)

    return _call(body, name="attn_fwd", grid=(nb,),
                 in_specs=[pl.BlockSpec(memory_space=pltpu.SMEM),
                           pl.BlockSpec((BLOCK, aw), lambda n: (n, 0)),
                           pl.BlockSpec((t, kw), lambda n: (0, aw // kw)),
                           pl.BlockSpec((t, kw), lambda n: (0, aw // kw + 1)),
                           pl.BlockSpec((1, aw), lambda n: (0, 0))],
                 out_specs=[pl.BlockSpec((BLOCK, aw), lambda n: (n, 0)), pl.BlockSpec((BLOCK, aw), lambda n: (n, 0))],
                 out_shape=[jax.ShapeDtypeStruct((t, aw), F32), jax.ShapeDtypeStruct((t, aw), BF16)],
                 args=(sinks, proj, proj, proj, gain), sem=("parallel",), carry=carry)


def attn_bwd(proj, sinks, gain, attn_o, dcat, aw, carry=None):
    t = proj.shape[0]
    kw = N_KV_HEADS * HEAD_DIM
    n_heads = aw // HEAD_DIM
    group = n_heads // N_KV_HEADS
    pairs = group // 2
    assert kw == LANES and group % 2 == 0
    nb = t // BLOCK
    scale = HEAD_DIM ** -0.5

    def body(sink_ref, q_ref, k_ref, v_ref, g_ref, o_ref, dn_ref, dq_ref, dk_ref, dv_ref, dsink_ref, dg_ref):
        n = pl.program_id(0)
        cur = pl.multiple_of(n * BLOCK, BLOCK)
        prev = pl.multiple_of(jnp.maximum(n - 1, 0) * BLOCK, BLOCK)
        geo = _attn_geometry(n > 0)

        @pl.when(n == 0)
        def _():
            dk_ref[...] = jnp.zeros_like(dk_ref)
            dv_ref[...] = jnp.zeros_like(dv_ref)
            dsink_ref[...] = jnp.zeros_like(dsink_ref)

        o = o_ref[...]
        do_all, dg = _norm_bwd(o, g_ref[...], dn_ref[...])
        _accum(dg_ref, dg)
        kcat = jnp.concatenate([k_ref[pl.ds(prev, BLOCK), :], k_ref[pl.ds(cur, BLOCK), :]], axis=0)
        vcat = jnp.concatenate([v_ref[pl.ds(prev, BLOCK), :], v_ref[pl.ds(cur, BLOCK), :]], axis=0)
        q = q_ref[...] * scale
        lane = lax.broadcasted_iota(jnp.int32, (1, LANES), 1)
        lane_s = lax.broadcasted_iota(jnp.int32, (pairs * BLOCK, LANES), 1)
        qs = [_stack_pairs(q, g, pairs) for g in range(N_KV_HEADS)]
        dos = [_stack_pairs(do_all, g, pairs) for g in range(N_KV_HEADS)]
        kxs = [_to_half(kcat, g, odd) for g, odd in _PARITIES]
        scores = [_dot(qs[g], kx, NT) for kx, (g, odd) in zip(kxs, _PARITIES)]
        dps = [_dot(dos[g], _to_half(vcat, g, odd), NT) for g, odd in _PARITIES]
        deltas = []
        for g in range(N_KV_HEADS):
            prod = dos[g] * _stack_pairs(o, g, pairs)
            delta_even = jnp.sum(jnp.where(lane_s < HEAD_DIM, prod, 0.0), axis=-1, keepdims=True)
            deltas += [delta_even, jnp.sum(prod, axis=-1, keepdims=True) - delta_even]
        dsink = jnp.zeros((1, LANES), F32)
        ps, dss = [], []
        for i, (g, odd) in enumerate(_PARITIES):
            p, p_sink = _softmax_sink(scores[i], sink_ref, g, odd, group, n_heads, geo)
            ps.append(p)
            dss.append(p * (dps[i] - deltas[i]))
            sink_rows = p_sink * deltas[i]
            for pr in range(pairs):
                h = g * group + 2 * pr + int(odd)
                dsink = dsink + jnp.where(
                    lane == h, -jnp.sum(sink_rows[pr * BLOCK:(pr + 1) * BLOCK], axis=0, keepdims=True), 0.0)
        dq_pairs = [_dot(ds, kx, NN) for ds, kx in zip(dss, kxs)]
        dk_halves = [_dot(ds, qs[g], TN) for ds, (g, odd) in zip(dss, _PARITIES)]
        dv_halves = [_dot(p, dos[g], TN) for p, (g, odd) in zip(ps, _PARITIES)]
        dq_ref[...] = jnp.concatenate(
            [_unstack_pairs((dq_pairs[2 * g] + dq_pairs[2 * g + 1]) * scale, pairs) for g in range(N_KV_HEADS)],
            axis=1).astype(BF16)
        dk_upd = _from_halves(dk_halves[0], dk_halves[1], 0) + _from_halves(dk_halves[2], dk_halves[3], 1)
        dv_upd = _from_halves(dv_halves[0], dv_halves[1], 0) + _from_halves(dv_halves[2], dv_halves[3], 1)
        dk_ref[pl.ds(prev, BLOCK), :] += dk_upd[:BLOCK]
        dv_ref[pl.ds(prev, BLOCK), :] += dv_upd[:BLOCK]
        dk_ref[pl.ds(cur, BLOCK), :] += dk_upd[BLOCK:]
        dv_ref[pl.ds(cur, BLOCK), :] += dv_upd[BLOCK:]
        dsink_ref[...] += dsink

    return _call(body, name="attn_bwd", grid=(nb,),
                 in_specs=[pl.BlockSpec(memory_space=pltpu.SMEM),
                           pl.BlockSpec((BLOCK, aw), lambda n: (n, 0)),
                           pl.BlockSpec((t, kw), lambda n: (0, aw // kw)),
                           pl.BlockSpec((t, kw), lambda n: (0, aw // kw + 1)),
                           pl.BlockSpec((1, aw), lambda n: (0, 0)),
                           pl.BlockSpec((BLOCK, aw), lambda n: (n, 0)),
                           pl.BlockSpec((BLOCK, aw), lambda n: (n, 0))],
                 out_specs=[pl.BlockSpec((BLOCK, aw), lambda n: (n, 0)),
                            pl.BlockSpec((t, kw), lambda n: (0, 0)), pl.BlockSpec((t, kw), lambda n: (0, 0)),
                            pl.BlockSpec((1, LANES), lambda n: (0, 0)), pl.BlockSpec((1, aw), lambda n: (0, 0))],
                 out_shape=[jax.ShapeDtypeStruct((t, aw), BF16), jax.ShapeDtypeStruct((t, kw), F32),
                            jax.ShapeDtypeStruct((t, kw), F32), jax.ShapeDtypeStruct((1, LANES), F32),
                            jax.ShapeDtypeStruct((1, aw), F32)],
                 args=(sinks, proj, proj, proj, gain, attn_o, dcat), sem=("arbitrary",), carry=carry)


def _sigmoid(x):
    return 0.5 * jnp.tanh(0.5 * x) + 0.5


def _chunk_geometry():
    row = lax.broadcasted_iota(jnp.int32, (CHUNK, CHUNK), 0)
    col = lax.broadcasted_iota(jnp.int32, (CHUNK, CHUNK), 1)
    return row, col


def _cumsum_rows(x, reverse=False):
    row, col = _chunk_geometry()
    tri = (col >= row) if reverse else (col <= row)
    return lax.dot_general(tri.astype(F32), x, ((NN), ((), ())), precision=HI, preferred_element_type=F32)


def _rep_sub(x4, sub):
    k = x4.shape[-1]
    return jnp.broadcast_to(x4[:, None, :], (CHUNK // sub, sub, k)).reshape(CHUNK, k)


def _gates(q_r, f_r, lb):
    sg = _sigmoid(f_r)
    f = lb + (1.0 - lb) * sg
    sq = _sigmoid(q_r)
    return sg, f, sq, q_r * sq


def _offdiag_terms(b, j, sub):
    c = b[j * sub + sub - 1:j * sub + sub, :]
    return jnp.exp(jnp.minimum(b - c, 0.0)), jnp.exp(jnp.minimum(c - b, 0.0))


def _store_heads(ref, x):
    for j in range(ref.shape[0]):
        ref[j] = x[:, _head(j)]


def _sub_rows(ref, r, sub):
    rows = [ref[j, pl.ds(r, CHUNK // sub, stride=sub), :] for j in range(ref.shape[0])]
    return _rep_sub(jnp.concatenate(rows, axis=1), sub)


def _diag_mask(sub):
    row, col = _chunk_geometry()
    return jnp.logical_and((row // sub) == (col // sub), row >= col)


HGRN_HEADS_PER_STEP = 8


def _wide(refs):
    return jnp.concatenate([r[...] for r in refs], axis=1)


def _head(j):
    return slice(j * RNN_HEAD_DIM, (j + 1) * RNN_HEAD_DIM)


def _cat_heads(parts, hs):
    return jnp.concatenate([p[:, hs] for p in parts], axis=1)


def _offdiag_factors(q, k, b, sub):
    rowi = lax.broadcasted_iota(jnp.int32, b.shape, 0)
    qs, ks, ers, ecs = [], [], [], []
    for j in range(CHUNK // sub - 1):
        e_row, e_col = _offdiag_terms(b, j, sub)
        e_row = jnp.where(rowi >= (j + 1) * sub, e_row, 0.0)
        e_col = jnp.where((rowi // sub) == j, e_col, 0.0)
        qs.append(q * e_row)
        ks.append(k * e_col)
        ers.append(e_row)
        ecs.append(e_col)
    return qs, ks, ers, ecs


def hgrn_fwd(proj, attn_n, lb, norm_gain, col0, rw, carry=None):
    t, aw = attn_n.shape
    nh = rw // RNN_HEAD_DIM
    nc = t // CHUNK
    kd = RNN_HEAD_DIM
    cb = col0 // kd
    sub = SUB_FWD
    nsub = CHUNK // sub
    hp = nh
    assert nh <= HGRN_HEADS_PER_STEP
    w = hp * kd

    def body(*refs):
        q_refs, f_refs, i_refs, g_refs = (refs[i * hp:(i + 1) * hp] for i in range(4))
        lb_ref, ng_ref, an_ref, cat_ref, o_ref, att_ref, st_ref, state, b_ref, k_ref = refs[4 * hp:]
        c = pl.program_id(1)

        @pl.when(c == 0)
        def _():
            state[...] = jnp.zeros_like(state)

        st_ref[...] = state[...]
        q_r, f_r, v, g_r = (_wide(rs) for rs in (q_refs, f_refs, i_refs, g_refs))
        _, f, _, q = _gates(q_r, f_r, lb_ref[...])
        k = 1.0 - f
        b = _cumsum_rows(jnp.log(f))
        _store_heads(b_ref, b)
        _store_heads(k_ref, k)
        qcat, kcat, _, _ = _offdiag_factors(q, k, b, sub)
        row, col = _chunk_geometry()
        same = (row // sub) == (col // sub)
        rloc = lax.broadcasted_iota(jnp.int32, (CHUNK, w), 0) % sub
        diag = [jnp.zeros((CHUNK, CHUNK), F32)] * hp
        for r in range(sub):
            bs = _sub_rows(b_ref, r, sub)
            ks = _sub_rows(k_ref, r, sub)
            prod = q * jnp.exp(jnp.where(rloc >= r, b - bs, -jnp.inf)) * ks
            place = jnp.logical_and((col % sub) == r, same)
            diag = [jnp.where(place, jnp.sum(prod[:, _head(j)], axis=-1, keepdims=True), diag[j]) for j in range(hp)]
        b_last = b[CHUNK - 1:CHUNK, :]
        qe = q * jnp.exp(b)
        kdec = k * jnp.exp(b_last - b)
        decay = jnp.exp(b_last)
        outs, normed, states = [], [], []
        for j in range(hp):
            hs = _head(j)
            att = diag[j] + _dot(_cat_heads(qcat, hs), _cat_heads(kcat, hs), NT)
            att_ref[j] = att
            sj = state[j]
            o = _dot(qe[:, hs], sj, NT) + _dot(att, v[:, hs], NN)
            outs.append(o)
            normed.append(o * _rstd(o))
            states.append(sj * decay[:, hs] + _dot(v[:, hs], kdec[:, hs], TN))
        for j in range(hp):
            state[j] = states[j]
        o_ref[...] = jnp.concatenate(outs, axis=1)
        gate = g_r * _sigmoid(g_r)
        cat_ref[:, :aw] = an_ref[...]
        cat_ref[:, aw:] = (jnp.concatenate(normed, axis=1) * jnp.tile(ng_ref[...], (1, hp)) * gate).astype(BF16)

    def col(kidx, j):
        return pl.BlockSpec((CHUNK, kd), lambda hg, c: (c, cb + kidx * nh + hg * hp + j))

    return _call(body, name="hgrn_fwd", grid=(1, nc),
                 in_specs=[col(kidx, j) for kidx in range(4) for j in range(hp)] +
                          [pl.BlockSpec((1, w), lambda hg, c: (0, hg)), pl.BlockSpec((1, kd), lambda hg, c: (0, 0)),
                           pl.BlockSpec((CHUNK, aw), lambda hg, c: (c, 0))],
                 out_specs=[pl.BlockSpec((CHUNK, aw + w), lambda hg, c: (c, 0)),
                            pl.BlockSpec((CHUNK, w), lambda hg, c: (c, hg)),
                            pl.BlockSpec((hp, CHUNK, CHUNK), lambda hg, c: (hg, c, 0)),
                            pl.BlockSpec((None, hp, kd, kd), lambda hg, c: (c, hg, 0, 0))],
                 out_shape=[jax.ShapeDtypeStruct((t, aw + rw), BF16), jax.ShapeDtypeStruct((t, rw), F32),
                            jax.ShapeDtypeStruct((nh, t, CHUNK), F32), jax.ShapeDtypeStruct((nc, nh, kd, kd), F32)],
                 args=(*([proj] * (4 * hp)), lb, norm_gain, attn_n),
                 scratch_shapes=[pltpu.VMEM((hp, kd, kd), F32), pltpu.VMEM((hp, CHUNK, kd), F32),
                                 pltpu.VMEM((hp, CHUNK, kd), F32)],
                 sem=("parallel", "arbitrary"), carry=carry)


def hgrn_bwd(proj, lb, norm_gain, o_all, att_all, st_all, dcat, dq_a, dk_a, dv_a, col0, rw, carry=None):
    t, iw = proj.shape
    aw, kw = dq_a.shape[1], dk_a.shape[1]
    nh = rw // RNN_HEAD_DIM
    nc = t // CHUNK
    kd = RNN_HEAD_DIM
    cb = col0 // kd
    sub = SUB_BWD
    nsub = CHUNK // sub
    dcb = (dcat.shape[1] - rw) // kd
    hp = nh
    assert nh <= HGRN_HEADS_PER_STEP and dcb % hp == 0 and col0 == aw + 2 * kw and iw == col0 + 4 * rw
    w = hp * kd

    def per_head(x, fn):
        return jnp.concatenate([jnp.broadcast_to(fn(x[:, _head(j)]), (CHUNK, kd)) for j in range(hp)], axis=1)

    def body(*refs):
        q_refs, f_refs, i_refs, g_refs = (refs[i * hp:(i + 1) * hp] for i in range(4))
        (lb_ref, ng_ref, o_ref, att_ref, st0_ref, st1_ref, d_ref, dqa_ref, dka_ref, dva_ref, dp_ref, dlb_ref, dng_ref,
         dstate, b_ref, k_ref, dks_ref) = refs[4 * hp:]
        ci = pl.program_id(1)

        @pl.when(ci == 0)
        def _():
            dstate[...] = jnp.zeros_like(dstate)
            dlb_ref[...] = jnp.zeros_like(dlb_ref)
            dng_ref[...] = jnp.zeros_like(dng_ref)

        lbv = lb_ref[...]
        q_r, f_r, v, g_r = (_wide(rs) for rs in (q_refs, f_refs, i_refs, g_refs))
        sg, f, sq, q = _gates(q_r, f_r, lbv)
        k = 1.0 - f
        b = _cumsum_rows(jnp.log(f))
        _store_heads(b_ref, b)
        _store_heads(k_ref, k)
        row, col = _chunk_geometry()

        o = o_ref[...]
        ng = jnp.tile(ng_ref[...], (1, hp))
        sgg = _sigmoid(g_r)
        gate = g_r * sgg
        d_rnn = d_ref[...]
        r = per_head(o, _rstd)
        oh = o * r
        dp_ref[:, :aw] = dqa_ref[...]
        dp_ref[:, aw:aw + kw] = dka_ref[...].astype(BF16)
        dp_ref[:, aw + kw:col0] = dva_ref[...].astype(BF16)
        dp_ref[:, col0 + 3 * rw:] = (d_rnn * oh * ng * (sgg * (1.0 + g_r * (1.0 - sgg)))).astype(BF16)
        d_on = d_rnn * gate
        dng_rows = jnp.sum(d_on * oh, axis=0, keepdims=True)
        dng = dng_rows[:, _head(0)]
        for j in range(1, hp):
            dng = dng + dng_rows[:, _head(j)]
        dng_ref[...] += dng
        dyg = d_on * ng
        do = r * (dyg - oh * per_head(dyg * oh, lambda x: jnp.mean(x, axis=-1, keepdims=True)))

        b_last = b[CHUNK - 1:CHUNK, :]
        eb = jnp.exp(b)
        tail = jnp.exp(b_last - b)
        kdec = k * tail
        decay = jnp.exp(b_last)
        qe = q * eb
        qcat, kcat, ers, ecs = _offdiag_factors(q, k, b, sub)
        diag_mask = _diag_mask(sub)
        dqs, dks, dvs, dads, gsums, dstates = [], [], [], [], [], []
        for j in range(hp):
            hs = _head(j)
            do_h, v_h, dst = do[:, hs], v[:, hs], dstate[j]
            da = jnp.where(row >= col, _dot(do_h, v_h, NT), 0.0)
            dads.append(jnp.where(diag_mask, da, 0.0))
            dq = _dot(do_h, st0_ref[j], NN) * eb[:, hs]
            dk = _dot(v_h, dst, NN) * tail[:, hs]
            dvs.append(_dot(att_ref[j], do_h, TN) + _dot(kdec[:, hs], dst, NT))
            rq = _dot(da, _cat_heads(kcat, hs), NN)
            rk = _dot(da, _cat_heads(qcat, hs), TN)
            for jj in range(nsub - 1):
                dq = dq + ers[jj][:, hs] * rq[:, _head(jj)]
                dk = dk + ecs[jj][:, hs] * rk[:, _head(jj)]
            dqs.append(dq)
            dks.append(dk)
            gsums.append(jnp.sum(dst * st1_ref[j], axis=0, keepdims=True))
            dstates.append(dst * decay[:, hs] + _dot(do_h, qe[:, hs], TN))
        for j in range(hp):
            dstate[j] = dstates[j]
        dq = jnp.concatenate(dqs, axis=1)
        dk = jnp.concatenate(dks, axis=1)
        rloc = lax.broadcasted_iota(jnp.int32, (CHUNK, w), 0) % sub
        for rr in range(sub):
            bs = _sub_rows(b_ref, rr, sub)
            ks = _sub_rows(k_ref, rr, sub)
            e = jnp.exp(jnp.where(rloc >= rr, b - bs, -jnp.inf))
            pick = (col % sub) == rr
            dacol = jnp.concatenate(
                [jnp.broadcast_to(jnp.sum(jnp.where(pick, dads[j], 0.0), axis=-1, keepdims=True), (CHUNK, kd))
                 for j in range(hp)], axis=1)
            wv = dacol * e
            dq = dq + wv * ks
            sums = jnp.sum((wv * q).reshape(nsub, sub, w), axis=1)
            for j in range(hp):
                dks_ref[j, pl.ds(rr, nsub, stride=sub), :] = sums[:, _head(j)]
        dk = dk + jnp.concatenate([dks_ref[j] for j in range(hp)], axis=1)

        dlf = _cumsum_rows(q * dq - k * dk, reverse=True) + jnp.concatenate(gsums, axis=1)
        dfv = dlf / f - dk
        dp_ref[:, col0 + rw:col0 + 2 * rw] = (dfv * (1.0 - lbv) * sg * (1.0 - sg)).astype(BF16)
        dlb_ref[...] += jnp.sum(dfv * (1.0 - sg), axis=0, keepdims=True)
        dp_ref[:, col0:col0 + rw] = (dq * (sq * (1.0 + q_r * (1.0 - sq)))).astype(BF16)
        dp_ref[:, col0 + 2 * rw:col0 + 3 * rw] = jnp.concatenate(dvs, axis=1).astype(BF16)

    def rev(c):
        return nc - 1 - c

    def col_in(kidx, j):
        return pl.BlockSpec((CHUNK, kd), lambda hg, c: (rev(c), cb + kidx * nh + hg * hp + j))

    def rows(width):
        return pl.BlockSpec((CHUNK, width), lambda hg, c: (rev(c), 0))

    return _call(body, name="hgrn_bwd", grid=(1, nc),
                 in_specs=[col_in(kidx, j) for kidx in range(4) for j in range(hp)] +
                          [pl.BlockSpec((1, w), lambda hg, c: (0, hg)), pl.BlockSpec((1, kd), lambda hg, c: (0, 0)),
                           rows(w),
                           pl.BlockSpec((hp, CHUNK, CHUNK), lambda hg, c: (hg, rev(c), 0)),
                           pl.BlockSpec((None, hp, kd, kd), lambda hg, c: (rev(c), hg, 0, 0)),
                           pl.BlockSpec((None, hp, kd, kd),
                                        lambda hg, c: (jnp.minimum(rev(c) + 1, nc - 1), hg, 0, 0)),
                           pl.BlockSpec((CHUNK, w), lambda hg, c: (rev(c), dcb // hp + hg)),
                           rows(aw), rows(kw), rows(kw)],
                 out_specs=[rows(iw),
                            pl.BlockSpec((1, w), lambda hg, c: (0, hg)),
                            pl.BlockSpec((None, 1, kd), lambda hg, c: (hg, 0, 0))],
                 out_shape=[jax.ShapeDtypeStruct((t, iw), BF16), jax.ShapeDtypeStruct((1, rw), F32),
                            jax.ShapeDtypeStruct((1, 1, kd), F32)],
                 args=(*([proj] * (4 * hp)), lb, norm_gain, o_all, att_all, st_all, st_all, dcat, dq_a, dk_a, dv_a),
                 scratch_shapes=[pltpu.VMEM((hp, kd, kd), F32), pltpu.VMEM((hp, CHUNK, kd), F32),
                                 pltpu.VMEM((hp, CHUNK, kd), F32), pltpu.VMEM((hp, CHUNK, kd), F32)],
                 sem=("parallel", "arbitrary"), carry=carry)


def comm_only(name, part):
    return _call(lambda: None, name=name, grid=(), in_specs=[], out_specs=[], out_shape=[], args=(), carry=part)[1]


def exchange_halves(name, array, axis):
    return comm_only(name, _scatter_step(array, axis))[0]


ADD_BLOCK_ELEMS = 1 << 20
ADAMW_BLOCK_ELEMS = 1 << 19


def add_kept_half(name, kept, got, sel, minor, row0=0):
    pieces, rows, cols = got.shape
    tr = _tile(rows, max(16, ADD_BLOCK_ELEMS // cols), mult=16)
    assert row0 % tr == 0
    i0 = row0 // tr

    def body(sel_ref, k_ref, g_ref, o_ref):
        o_ref[...] = (k_ref[...].astype(F32) + g_ref[...].astype(F32)).astype(o_ref.dtype)

    kept_spec = (pl.BlockSpec((None, None, tr, cols), lambda p, i, s: (p, s[0], i + i0, 0)) if minor else
                 pl.BlockSpec((None, None, tr, cols), lambda p, i, s: (s[0], p, i + i0, 0)))
    return pl.pallas_call(
        body, name=name,
        grid_spec=pltpu.PrefetchScalarGridSpec(
            num_scalar_prefetch=1, grid=(pieces, rows // tr),
            in_specs=[kept_spec, pl.BlockSpec((None, tr, cols), lambda p, i, s: (p, i, 0))],
            out_specs=pl.BlockSpec((None, tr, cols), lambda p, i, s: (p, i, 0))),
        out_shape=jax.ShapeDtypeStruct(got.shape, got.dtype),
        compiler_params=_cparams(("parallel", "parallel")),
    )(sel, kept, got)


def _adamw(w, g, m, v):
    m = ADAM_B1 * m + (1.0 - ADAM_B1) * g
    v = ADAM_B2 * v + (1.0 - ADAM_B2) * (g * g)
    m_hat = m / (1.0 - ADAM_B1 ** ADAM_STEP)
    v_hat = v / (1.0 - ADAM_B2 ** ADAM_STEP)
    delta = -ADAM_LR * (m_hat / (jnp.sqrt(v_hat) + ADAM_EPS) + ADAM_WD * w)
    return delta, m, v


def add_adamw(name, kept, got, sel, w, m, v, row0=0):
    _, rows, cols = got.shape
    tr = _tile(rows, max(16, ADAMW_BLOCK_ELEMS // cols), mult=16)
    assert row0 % tr == 0
    i0 = row0 // tr

    def body(sel_ref, k_ref, g_ref, w_ref, m_ref, v_ref, go_ref, d_ref, mo_ref, vo_ref):
        g = k_ref[...].astype(F32) + g_ref[...].astype(F32)
        go_ref[...] = g
        d_ref[...], mo_ref[...], vo_ref[...] = _adamw(w_ref[...], g, m_ref[...], v_ref[...])

    tile = pl.BlockSpec((tr, cols), lambda i, s: (i, 0))
    shard_tile = pl.BlockSpec((tr, cols), lambda i, s: (i + i0, 0))
    return pl.pallas_call(
        body, name=name,
        grid_spec=pltpu.PrefetchScalarGridSpec(
            num_scalar_prefetch=1, grid=(rows // tr,),
            in_specs=[pl.BlockSpec((None, None, tr, cols), lambda i, s: (s[0], 0, i, 0)),
                      pl.BlockSpec((None, tr, cols), lambda i, s: (0, i, 0)), shard_tile, shard_tile, shard_tile],
            out_specs=[tile] * 4),
        out_shape=[jax.ShapeDtypeStruct((rows, cols), F32)] * 4,
        compiler_params=_cparams(("parallel",)),
    )(sel, kept, got, w, m, v)


def small_allreduce_adamw(partial, scale, w, m, v):
    rows = partial.shape[0]

    def body(p_ref, s_ref, w_ref, m_ref, v_ref, g_ref, d_ref, mo_ref, vo_ref, slots, send_sems, recv_sems):
        x, y, c = _coords()
        my_slot = _slab_index((x, y, c))
        slots[my_slot] = p_ref[...]
        copies = []
        for mask in range(1, N_DEV):
            to = tuple(1 - v_ if (mask >> s_) & 1 else v_ for v_, s_ in ((x, 2), (y, 1), (c, 0)))
            copies.append(pltpu.make_async_remote_copy(
                src_ref=p_ref, dst_ref=slots.at[my_slot],
                send_sem=send_sems.at[mask - 1], recv_sem=recv_sems.at[mask - 1],
                device_id=to, device_id_type=MESH))
        for cp in copies:
            cp.start()
        for cp in copies:
            cp.wait()
        total = slots[0]
        for b in range(1, N_DEV):
            total = total + slots[b]
        g = total * s_ref[...]
        g_ref[...] = g
        d_ref[...], mo_ref[...], vo_ref[...] = _adamw(w_ref[...], g, m_ref[...], v_ref[...])

    vm = pl.BlockSpec(memory_space=pltpu.VMEM)
    return pl.pallas_call(
        body, name="small_allreduce_adamw",
        in_specs=[vm] * 5, out_specs=[vm] * 4,
        out_shape=[jax.ShapeDtypeStruct((rows, LANES), F32)] * 4,
        scratch_shapes=[pltpu.VMEM((N_DEV, rows, LANES), F32),
                        pltpu.SemaphoreType.DMA((N_DEV - 1,)), pltpu.SemaphoreType.DMA((N_DEV - 1,))],
        compiler_params=pltpu.CompilerParams(has_side_effects=True),
    )(partial, scale, w, m, v)


_SMALL = ("attn_sinks", "attn_out_gain", "rnn_lb_logits", "rnn_norm_gain", "mix_pre_gain", "mix_post_gain",
          "mlp_pre_gain", "mlp_post_gain")


def _pack(parts):
    rows = []
    for p in parts:
        flat = p.reshape(-1).astype(F32)
        pad = (-flat.shape[0]) % LANES
        rows.append(jnp.pad(flat, (0, pad)).reshape(-1, LANES))
    packed = jnp.concatenate(rows, axis=0)
    pad_rows = (-packed.shape[0]) % 8
    return jnp.pad(packed, ((0, pad_rows), (0, 0)))


def _unpack(packed, shapes):
    out, r = [], 0
    for s in shapes:
        size = math.prod(s)
        nrows = -(-size // LANES)
        out.append(packed[r:r + nrows].reshape(-1)[:size].reshape(s))
        r += nrows
    return out


class _Scatter:
    def __init__(self, tag, grad, sels):
        self.tag, self.sels = tag, sels
        self.shape = grad.shape[1:]
        self.cur = grad.reshape(4, 2, *self.shape)
        self.stage = 0

    def step(self):
        return _scatter_step(self.cur, "cxy"[self.stage])

    def land(self, got, w=None, m=None, v=None):
        axis = "cxy"[self.stage]
        name = "rs_add_%s_%s" % (axis, self.tag)
        sel = self.sels[axis]
        self.stage += 1
        if axis == "y":
            return add_adamw(name, self.cur, got, sel, w, m, v)
        summed = add_kept_half(name, self.cur, got, sel, minor=axis == "c")
        self.cur = summed.reshape(2, summed.shape[0] // 2, *self.shape)
        return None


def scatter_both_links(tag, s1, sels, carrier, w, m, v):
    rows = s1.shape[2]
    half = rows // 2
    result, (got_a, got_b) = carrier(_merge(_scatter_step(s1, "x", rows=(0, half)),
                                            _scatter_step(s1, "y", minor=True, rows=(half, rows))))
    sa = add_kept_half("rs_add_x_%s_a" % tag, s1, got_a, sels["x"], minor=False)
    sb = add_kept_half("rs_add_y_%s_b" % tag, s1, got_b, sels["y"], minor=True, row0=half)
    sa, sb = (s.reshape(2, 1, *s.shape[1:]) for s in (sa, sb))
    got_a, got_b = comm_only("rs_exchange_last_%s" % tag, _merge(_scatter_step(sa, "y"), _scatter_step(sb, "x")))
    out_a = add_adamw("rs_add_y_%s_a" % tag, sa, got_a, sels["y"], w, m, v)
    out_b = add_adamw("rs_add_x_%s_b" % tag, sb, got_b, sels["x"], w, m, v, row0=half)
    return result, [jnp.concatenate([a, b], axis=0) for a, b in zip(out_a, out_b)]


def kernel(x, w_in, attn_sinks, attn_out_gain, rnn_lb_logits, rnn_norm_gain, w_out, mix_pre_gain, mix_post_gain, mlp_pre_gain, mlp_post_gain, w_up, w_down, loss_target, m_w_in, m_attn_sinks, m_attn_out_gain, m_rnn_lb_logits, m_rnn_norm_gain, m_w_out, m_mix_pre_gain, m_mix_post_gain, m_mlp_pre_gain, m_mlp_post_gain, m_w_up, m_w_down, v_w_in, v_attn_sinks, v_attn_out_gain, v_rnn_lb_logits, v_rnn_norm_gain, v_w_out, v_mix_pre_gain, v_mix_post_gain, v_mlp_pre_gain, v_mlp_post_gain, v_w_up, v_w_down):
    xs, target = x[0], loss_target[0]
    t, d = xs.shape
    aw = d // 2
    rw = d - aw
    col0 = aw + 2 * N_KV_HEADS * HEAD_DIM
    small_w = dict(attn_sinks=attn_sinks, attn_out_gain=attn_out_gain, rnn_lb_logits=rnn_lb_logits,
                   rnn_norm_gain=rnn_norm_gain, mix_pre_gain=mix_pre_gain, mix_post_gain=mix_post_gain,
                   mlp_pre_gain=mlp_pre_gain, mlp_post_gain=mlp_post_gain)
    small_m = dict(attn_sinks=m_attn_sinks, attn_out_gain=m_attn_out_gain, rnn_lb_logits=m_rnn_lb_logits,
                   rnn_norm_gain=m_rnn_norm_gain, mix_pre_gain=m_mix_pre_gain, mix_post_gain=m_mix_post_gain,
                   mlp_pre_gain=m_mlp_pre_gain, mlp_post_gain=m_mlp_post_gain)
    small_v = dict(attn_sinks=v_attn_sinks, attn_out_gain=v_attn_out_gain, rnn_lb_logits=v_rnn_lb_logits,
                   rnn_norm_gain=v_rnn_norm_gain, mix_pre_gain=v_mix_pre_gain, mix_post_gain=v_mix_post_gain,
                   mlp_pre_gain=v_mlp_pre_gain, mlp_post_gain=v_mlp_post_gain)
    cx, cy, cc = _coords()
    sels = {a: jnp.reshape(v_, (1,)).astype(jnp.int32) for a, v_ in (("x", cx), ("y", cy), ("c", cc))}

    w_in_t, m_in_t, v_in_t = w_in[0].T, m_w_in[0].T, v_w_in[0].T
    s_in, s_out, s_up, s_down = (w.astype(BF16) for w in (w_in_t, w_out[0], w_up[0], w_down[0]))
    probs = jax.nn.softmax(rnn_lb_logits.astype(F32), axis=0)
    lb = probs[0:1]

    (h1,), (wint_part,) = pre_norm(xs, mix_pre_gain, carry=_gather_first(s_in, diagonal=False))
    in_rows = s_in.shape[0]
    wint = comm_only("gather_rest_w_in", _pass_slabs(
        wint_part,
        [(_X, _Y, (0, in_rows // 2)), (_Y, _X, (in_rows // 2, in_rows)), (_X, _C, None), (_Y, _C, None)],
        then=[(_XY, _C, None)]))[0].reshape(-1, d)
    up_rows = s_up.shape[0]
    up_cut = up_rows * 9 // 16
    proj, (wup_part,) = mm_nt("in_proj", h1, wint, F32, carry=_gather_first(s_up, rows=(0, up_cut)))
    (attn_o, attn_n), (wup_half, wout_half) = attn_fwd(
        proj, attn_sinks, attn_out_gain, aw,
        carry=_merge(_gather_first(s_up, rows=(up_cut, up_rows), into=wup_part), _gather_first(s_out)))
    (cat, o_r, att, st), (wup, wout, wdown_half) = hgrn_fwd(
        proj, attn_n, lb, rnn_norm_gain, col0, rw,
        carry=_merge(_gather_second(wup_half), _gather_second(wout_half), _gather_first(s_down)))
    wout = wout.reshape(-1, d)
    mixed, (wdown,) = mm_nn("out_proj", cat, wout, F32, carry=_gather_second(wdown_half))
    wdown = wdown.reshape(-1, d)
    x1, h2 = mid_fwd(mixed, mix_post_gain, xs, mlp_pre_gain)
    u = up_proj(h2, wup)
    y = down_proj(u, wdown)
    sse, dout, dy, dg_mlppost = loss_bwd(y, mlp_post_gain, x1, target)

    du = down_bwd_act(dy, wdown, u)
    rs_down = _Scatter("down", down_wgrad(u, dy).reshape(N_DEV, -1, d), sels)
    dh2, (got,) = up_bwd_x(du, wup, carry=rs_down.step())
    rs_down.land(got)
    dwup, (got,) = up_wgrad(h2, du, carry=rs_down.step())
    rs_down.land(got)
    rs_up = _Scatter("up", dwup, sels)
    (dx1, dmixed, dg_mlppre, dg_mixpost), (got,) = mid_bwd(dh2, x1, mlp_pre_gain, dout, mixed, mix_post_gain,
                                                          carry=rs_up.step())
    rs_up.land(got)
    dcat = mm_nt("out_bwd_x", dmixed, wout, F32)
    rs_out = _Scatter("out", mm_tn("out_wgrad", cat, dmixed, BF16).reshape(N_DEV, -1, d), sels)
    (dq_a, dk_a, dv_a, dsinks, daog), (got_d, got_o) = attn_bwd(
        proj, attn_sinks, attn_out_gain, attn_o, dcat, aw, carry=_merge(rs_down.step(), rs_out.step()))
    out_down = rs_down.land(got_d, w_down[0], m_w_down[0], v_w_down[0])
    rs_out.land(got_o)
    (dproj, dlb, dng), (got_u, got_o) = hgrn_bwd(
        proj, lb, rnn_norm_gain, o_r, att, st, dcat, dq_a, dk_a, dv_a, col0, rw,
        carry=_merge(rs_up.step(), rs_out.step()))
    rs_up.land(got_u)
    rs_out.land(got_o)
    dwin, (got_u, got_o) = mm_tn("in_wgrad", dproj, h1, BF16, carry=_merge(rs_up.step(), rs_out.step()))
    out_up = rs_up.land(got_u, w_up[0], m_w_up[0], v_w_up[0])
    out_out = rs_out.land(got_o, w_out[0], m_w_out[0], v_w_out[0])
    rs_in = _Scatter("in", dwin.reshape(N_DEV, -1, d), sels)
    rs_in.land(exchange_halves("rs_exchange_c_in", rs_in.cur, "c"))
    dh1, out_in = scatter_both_links(
        "in", rs_in.cur, sels, lambda part: mm_nn("in_bwd_x", dproj, wint, F32, tm=MM_TILE // 2, carry=part),
        w_in_t, m_in_t, v_in_t)
    grad_x, dg_mixpre = first_bwd(dh1, xs, mix_pre_gain, dx1)
    big_out = [out_in, out_out, out_up, out_down]

    n_heads = attn_sinks.shape[1]
    jac = probs[0] * probs[1]
    partial = _pack([sse, dsinks[0, :n_heads], daog, jnp.stack([dlb[0], dlb[0]]), jnp.sum(dng, axis=0),
                     dg_mixpre, dg_mixpost, dg_mlppre, dg_mlppost])
    ones = [jnp.ones(small_w[k].shape, F32) for k in _SMALL]
    ones[2] = jnp.stack([jac, -jac])
    scale = _pack([jnp.full((1,), 0.5 / d, F32)] + ones)
    zero = jnp.zeros((1,), F32)
    outs = small_allreduce_adamw(partial, scale, _pack([zero] + [small_w[k] for k in _SMALL]),
                                 _pack([zero] + [small_m[k] for k in _SMALL]),
                                 _pack([jnp.ones((1,), F32)] + [small_v[k] for k in _SMALL]))
    shapes = [(1,)] + [small_w[k].shape for k in _SMALL]
    sgrad, sdelta, snm, snv = (_unpack(o, shapes) for o in outs)
    loss = sgrad[0][0]

    def big(i, j):
        o = big_out[i][j]
        return (o.T if i == 0 else o)[None]

    def ordered(j, smalls):
        s = dict(zip(_SMALL, smalls[1:]))
        return [big(0, j), s["attn_sinks"], s["attn_out_gain"], s["rnn_lb_logits"], s["rnn_norm_gain"], big(1, j),
                s["mix_pre_gain"], s["mix_post_gain"], s["mlp_pre_gain"], s["mlp_post_gain"], big(2, j), big(3, j)]

    return (loss, grad_x[None], *ordered(0, sgrad), *ordered(1, sdelta), *ordered(2, snm), *ordered(3, snv))
```

```python
import math

import jax
import jax.numpy as jnp
from jax import lax
from jax.experimental import pallas as pl
from jax.experimental.pallas import tpu as pltpu

F32 = jnp.float32
BF16 = jnp.bfloat16

HEAD_DIM = 64
N_KV_HEADS = 2
BLOCK = 128
RNN_HEAD_DIM = 128
CHUNK = 64
SUB_FWD = 16
SUB_BWD = 16
EPS = 1e-6

ADAM_LR = 0.001
ADAM_B1 = 0.9
ADAM_B2 = 0.999
ADAM_EPS = 1e-08
ADAM_WD = 0.01
ADAM_STEP = 10

N_DEV = 8
LANES = 128
V7X_VMEM_LIMIT = 56 * 1024 * 1024
MESH = pl.DeviceIdType.MESH
HI = lax.Precision.HIGHEST
ANY = pl.BlockSpec(memory_space=pl.ANY)
_AXES = ("x", "y", "c")


def _cparams(sem=None, **kw):
    return pltpu.CompilerParams(dimension_semantics=sem, vmem_limit_bytes=V7X_VMEM_LIMIT, **kw)


def _dot(a, b, dims):
    return lax.dot_general(a.astype(BF16), b.astype(BF16), (dims, ((), ())), preferred_element_type=F32)


NN = ((1,), (0,))
NT = ((1,), (1,))
TN = ((0,), (0,))


def _pick(n, pref):
    t = min(n, pref)
    while n % t:
        t //= 2
    return t


def _tile(n, pref, mult=LANES):
    if n <= pref:
        return n
    t = pref - pref % mult
    while n % t:
        t -= mult
    return t


def _coords():
    return lax.axis_index("x"), lax.axis_index("y"), lax.axis_index("c")


def _slab_index(dev):
    return 4 * dev[0] + 2 * dev[1] + dev[2]


class _Part:
    def __init__(self, operands, landings, aliases, n_sems, plan):
        self.operands, self.landings, self.aliases, self.n_sems, self.plan = operands, landings, aliases, n_sems, plan


def _merge(*parts):
    operands, landings, aliases, plans = [], [], {}, []
    s0 = 0
    for p in parts:
        o0, l0 = len(operands), len(landings)
        aliases.update({o0 + i: l0 + j for i, j in p.aliases.items()})
        plans.append((p.plan, o0, len(p.operands), l0, len(p.landings), s0))
        operands += p.operands
        landings += p.landings
        s0 += p.n_sems

    def plan(ops, lands, sem):
        starts, waits = [], []
        for f, o0, no, l0, nl, off in plans:
            s, w = f(ops[o0:o0 + no], lands[l0:l0 + nl], lambda kind, k, off=off: sem(kind, off + k))
            starts += s
            waits += w
        return starts, waits

    return _Part(operands, landings, aliases, s0, plan)


def _gather_peers(x, y, c):
    return [(x, y, 1 - c), (1 - x, y, c), (x, 1 - y, c), (1 - x, 1 - y, c)]


def _gather_first(shard, rows=None, into=None, diagonal=True):
    lo, hi = (0, shard.shape[0]) if rows is None else rows
    n_peers = 4 if diagonal else 3

    def plan(ops, lands, sem):
        x, y, c = _coords()
        me, peers = (x, y, c), _gather_peers(x, y, c)[:n_peers]
        src = ops[0].at[pl.ds(lo, hi - lo)]

        def slab(block):
            return lands[0].at[_slab_index(block), pl.ds(lo, hi - lo)]

        def cp(k, block, to):
            return pltpu.make_async_remote_copy(
                src_ref=src, dst_ref=slab(block),
                send_sem=sem(0, k), recv_sem=sem(1, k), device_id=to, device_id_type=MESH)

        local = pltpu.make_async_copy(src, slab(me), sem(2, 0))
        sends = [cp(k, me, to) for k, to in enumerate(peers)]
        recvs = [cp(k, frm, me) for k, frm in enumerate(peers)]
        return ([local.start] + [s.start for s in sends],
                [local.wait] + [s.wait_send for s in sends] + [r.wait_recv for r in recvs])

    landing = jax.ShapeDtypeStruct((N_DEV, *shard.shape), shard.dtype)
    if into is None:
        return _Part([shard], [landing], {}, 4, plan)
    return _Part([shard, into], [landing], {1: 0}, 4, plan)


def _flip(dev, flips):
    return tuple(1 - v if f else v for v, f in zip(dev, flips))


def _pass_slabs(gathered, moves, then=()):
    def wave(lands, sem, k0, wave_moves):
        me = _coords()
        sends, recvs = [], []
        for k, (block, dest, rows) in enumerate(wave_moves, start=k0):
            lo, hi = (0, gathered.shape[1]) if rows is None else rows

            def cp(blk, to, k=k, lo=lo, hi=hi):
                slab = lands[0].at[_slab_index(blk), pl.ds(lo, hi - lo)]
                return pltpu.make_async_remote_copy(
                    src_ref=slab, dst_ref=slab, send_sem=sem(0, k), recv_sem=sem(1, k),
                    device_id=to, device_id_type=MESH)

            sends.append(cp(_flip(me, block), _flip(me, dest)))
            recvs.append(cp(_flip(_flip(me, dest), block), me))
        return [s.start for s in sends], [s.wait_send for s in sends] + [r.wait_recv for r in recvs]

    def plan(ops, lands, sem):
        starts, waits = wave(lands, sem, 0, moves)
        if then:
            starts2, waits2 = wave(lands, sem, len(moves), then)
            waits = waits + starts2 + waits2
        return starts, waits

    return _Part([gathered], [jax.ShapeDtypeStruct(gathered.shape, gathered.dtype)], {0: 0},
                 len(moves) + len(then), plan)


_X, _Y, _C, _XY = (1, 0, 0), (0, 1, 0), (0, 0, 1), (1, 1, 0)


def _gather_second(gathered):
    def plan(ops, lands, sem):
        x, y, c = _coords()
        sibling = (x, y, 1 - c)
        chips = [(1 - x, y), (x, 1 - y), (1 - x, 1 - y)]

        def cp(k, block):
            slab = lands[0].at[_slab_index(block)]
            return pltpu.make_async_remote_copy(
                src_ref=slab, dst_ref=slab, send_sem=sem(0, k), recv_sem=sem(1, k),
                device_id=sibling, device_id_type=MESH)

        sends = [cp(k, (*chip, c)) for k, chip in enumerate(chips)]
        recvs = [cp(k, (*chip, 1 - c)) for k, chip in enumerate(chips)]
        return [s.start for s in sends], [s.wait_send for s in sends] + [r.wait_recv for r in recvs]

    return _Part([gathered], [jax.ShapeDtypeStruct(gathered.shape, gathered.dtype)], {0: 0}, 3, plan)


def _scatter_step(array, axis, minor=None, rows=None):
    minor = (axis == "c") if minor is None else minor
    pieces = array.shape[0] if minor else array.shape[1]
    lo, hi = (0, array.shape[2]) if rows is None else rows

    def plan(ops, lands, sem):
        coords = list(_coords())
        ai = _AXES.index(axis)
        mine = coords[ai]
        peer = list(coords)
        peer[ai] = 1 - mine
        cps = []
        for p in range(pieces):
            src = ops[0].at[p, 1 - mine, pl.ds(lo, hi - lo)] if minor else ops[0].at[1 - mine, p, pl.ds(lo, hi - lo)]
            cps.append(pltpu.make_async_remote_copy(
                src_ref=src, dst_ref=lands[0].at[p], send_sem=sem(0, p), recv_sem=sem(1, p),
                device_id=tuple(peer), device_id_type=MESH))
        return [cp.start for cp in cps], [cp.wait for cp in cps]

    return _Part([array], [jax.ShapeDtypeStruct((pieces, hi - lo, array.shape[3]), array.dtype)], {}, pieces, plan)


def _grid_edges(grid):
    first = last = None
    for ax, n in enumerate(grid):
        p = pl.program_id(ax)
        f, l = p == 0, p == n - 1
        first = f if first is None else jnp.logical_and(first, f)
        last = l if last is None else jnp.logical_and(last, l)
    return first, last


def _call(body, *, name, grid, in_specs, out_specs, out_shape, args, scratch_shapes=(), sem=None, carry=None):
    if carry is None:
        return pl.pallas_call(
            body, name=name, grid=grid, in_specs=list(in_specs), out_specs=list(out_specs),
            out_shape=list(out_shape), scratch_shapes=list(scratch_shapes), compiler_params=_cparams(sem),
        )(*args)
    n_in, n_out, n_scr = len(in_specs), len(out_specs), len(scratch_shapes)
    n_cin, n_cout = len(carry.operands), len(carry.landings)

    def wrapped(*refs):
        ins, cins = refs[:n_in], refs[n_in:n_in + n_cin]
        o0 = n_in + n_cin
        outs, couts = refs[o0:o0 + n_out], refs[o0 + n_out:o0 + n_out + n_cout]
        s0 = o0 + n_out + n_cout
        scr, sems = refs[s0:s0 + n_scr], refs[s0 + n_scr:]
        first, last = _grid_edges(grid)

        def plan():
            return carry.plan(cins, couts, lambda kind, k: sems[kind].at[k])

        def start_all():
            for start in plan()[0]:
                start()

        def wait_all():
            for wait in plan()[1]:
                wait()

        if grid:
            pl.when(first)(start_all)
            body(*ins, *outs, *scr)
            pl.when(last)(wait_all)
        else:
            start_all()
            body(*ins, *outs, *scr)
            wait_all()

    sem_arrays = [pltpu.SemaphoreType.DMA((carry.n_sems,))] * 3
    res = pl.pallas_call(
        wrapped, name=name, grid=grid,
        in_specs=[*in_specs, *[ANY] * n_cin], out_specs=[*out_specs, *[ANY] * n_cout],
        out_shape=[*out_shape, *carry.landings],
        scratch_shapes=[*scratch_shapes, *sem_arrays],
        input_output_aliases={n_in + i: n_out + j for i, j in carry.aliases.items()},
        compiler_params=_cparams(("arbitrary",) * len(grid) if grid else None, has_side_effects=True),
    )(*args, *carry.operands)
    return res[:n_out], res[n_out:]


MM_TILE = 1024
MM_K_TILE = 2048
MXU_COLS = 256
MM_VMEM_BUDGET = 50 * 1024 * 1024


def _matmul(name, a, b, dims, grid, a_spec, b_spec, out_shape, out_spec, epilogue,
            extras=(), extra_specs=(), prologue=None, carry=None):
    nk = grid[2]
    n_extra = len(extras)
    acc_shape = out_spec.block_shape[-2:]

    def lhs(a_ref):
        return a_ref[...] if prologue is None else prologue(a_ref[...])

    def body_one(a_ref, b_ref, *rest):
        epilogue(_dot(lhs(a_ref), b_ref[...], dims), rest[:n_extra], rest[n_extra:])

    def body_acc(a_ref, b_ref, *rest):
        acc = rest[-1]
        k = pl.program_id(2)
        part = _dot(lhs(a_ref), b_ref[...], dims)

        @pl.when(k == 0)
        def _():
            acc[...] = part

        @pl.when(k > 0)
        def _():
            acc[...] += part

        @pl.when(k == nk - 1)
        def _():
            epilogue(acc[...], rest[:n_extra], rest[n_extra:-1])

    res = _call(body_one if nk == 1 else body_acc, name=name, grid=grid,
                in_specs=[a_spec, b_spec, *extra_specs], out_specs=[out_spec], out_shape=[out_shape],
                args=(a, b, *extras), scratch_shapes=[] if nk == 1 else [pltpu.VMEM(acc_shape, F32)],
                sem=("parallel", "parallel", "arbitrary"), carry=carry)
    return res[0] if carry is None else (res[0][0], res[1])


def _store_as(acc, extra_refs, out_refs):
    out_refs[0][...] = acc.astype(out_refs[0].dtype)


def _square(u):
    return u * u


def mm_nn(name, a, b, out_dtype, tk=None, tm=MM_TILE, tn=MM_TILE, prologue=None, carry=None):
    (m, kk), n = a.shape, b.shape[1]
    tm, tn = _tile(m, tm), _tile(n, tn, mult=MXU_COLS)
    tk = kk if tk is None else _tile(kk, tk, mult=MXU_COLS)
    return _matmul(name, a, b, NN, (m // tm, n // tn, kk // tk),
                   pl.BlockSpec((tm, tk), lambda i, j, k: (i, k)),
                   pl.BlockSpec((tk, tn), lambda i, j, k: (k, j)),
                   jax.ShapeDtypeStruct((m, n), out_dtype),
                   pl.BlockSpec((tm, tn), lambda i, j, k: (i, j)), _store_as, prologue=prologue, carry=carry)


def mm_nt(name, a, b, out_dtype, epilogue=_store_as, extras=(), extra_specs=(), carry=None):
    (m, kk), n = a.shape, b.shape[0]
    tm, tn = _tile(m, MM_TILE), _tile(n, MM_TILE, mult=MXU_COLS)
    return _matmul(name, a, b, NT, (m // tm, n // tn, 1),
                   pl.BlockSpec((tm, kk), lambda i, j, k: (i, 0)),
                   pl.BlockSpec((tn, kk), lambda i, j, k: (j, 0)),
                   jax.ShapeDtypeStruct((m, n), out_dtype),
                   pl.BlockSpec((tm, tn), lambda i, j, k: (i, j)), epilogue,
                   extras=extras, extra_specs=extra_specs, carry=carry)


def _whole_k_fits(tm, tn, kk, out_dtype, prologue):
    operands = 2 * 2 * kk * (tm + tn)
    out = 2 * tm * tn * jnp.dtype(out_dtype).itemsize + 4 * tm * tn
    return operands + out + (2 * kk * tm if prologue is not None else 0) <= MM_VMEM_BUDGET


def mm_tn(name, a, b, out_dtype, prologue=None, carry=None):
    (kk, m), n = a.shape, b.shape[1]
    tm, tn = _tile(m, MM_TILE), _tile(n, MM_TILE)
    tk = kk if _whole_k_fits(tm, tn, kk, out_dtype, prologue) else _tile(kk, MM_K_TILE)
    return _matmul(name, a, b, TN, (m // tm, n // tn, kk // tk),
                   pl.BlockSpec((tk, tm), lambda i, j, k: (k, i)),
                   pl.BlockSpec((tk, tn), lambda i, j, k: (k, j)),
                   jax.ShapeDtypeStruct((m, n), out_dtype),
                   pl.BlockSpec((tm, tn), lambda i, j, k: (i, j)), _store_as, prologue=prologue, carry=carry)


def up_proj(h2, wup_slabs):
    (m, kk), (_, _, ns) = h2.shape, wup_slabs.shape
    tm, tn = _tile(m, MM_TILE), _tile(ns, MM_TILE)
    r = ns // tn
    n = N_DEV * ns

    def epi(acc, extra_refs, out_refs):
        out_refs[0][...] = jnp.maximum(acc, 0.0).astype(BF16)

    return _matmul("up_proj", h2, wup_slabs, NN, (m // tm, n // tn, 1),
                   pl.BlockSpec((tm, kk), lambda i, j, k: (i, 0)),
                   pl.BlockSpec((None, kk, tn), lambda i, j, k: (j // r, 0, j % r)),
                   jax.ShapeDtypeStruct((m, n), BF16),
                   pl.BlockSpec((tm, tn), lambda i, j, k: (i, j)), epi)


def down_proj(u, wdown):
    return mm_nn("down_proj", u, wdown, F32, tm=MM_TILE // 2, tn=MM_TILE // 2, prologue=_square)


def down_bwd_act(dy, wdown, u):
    tm, tn = _tile(dy.shape[0], MM_TILE), _tile(wdown.shape[0], MM_TILE)

    def epi(acc, extra_refs, out_refs):
        out_refs[0][...] = (acc * (2.0 * extra_refs[0][...].astype(F32))).astype(BF16)

    return mm_nt("down_bwd_act", dy, wdown, BF16, epilogue=epi, extras=(u,),
                 extra_specs=(pl.BlockSpec((tm, tn), lambda i, j, k: (i, j)),))


def down_wgrad(u, dy):
    return mm_tn("down_wgrad", u, dy, BF16, prologue=_square)


def up_bwd_x(du, wup_slabs, carry=None):
    (m, kk), (slabs, n, ns) = du.shape, wup_slabs.shape
    tm, tn = _tile(m, MM_TILE // 2), _tile(n, MM_TILE // 2, mult=MXU_COLS)

    def body(a_ref, b_ref, o_ref):
        acc = _dot(a_ref[:, :ns], b_ref[0], NT)
        for s in range(1, slabs):
            acc = acc + _dot(a_ref[:, s * ns:(s + 1) * ns], b_ref[s], NT)
        o_ref[...] = acc

    res = _call(body, name="up_bwd_x", grid=(m // tm, n // tn),
                in_specs=[pl.BlockSpec((tm, kk), lambda i, j: (i, 0)),
                          pl.BlockSpec((slabs, tn, ns), lambda i, j: (0, j, 0))],
                out_specs=[pl.BlockSpec((tm, tn), lambda i, j: (i, j))],
                out_shape=[jax.ShapeDtypeStruct((m, n), F32)], args=(du, wup_slabs),
                sem=("parallel", "parallel"), carry=carry)
    return res[0] if carry is None else (res[0][0], res[1])


def up_wgrad(h2, du, carry=None):
    (kk, m), n = h2.shape, du.shape[1]
    ns = n // N_DEV
    tm, tn = _tile(m, MM_TILE), _tile(ns, MM_TILE)
    tk = kk if _whole_k_fits(tm, tn, kk, BF16, None) else _tile(kk, MM_K_TILE)
    r = ns // tn
    return _matmul("up_wgrad", h2, du, TN, (m // tm, n // tn, kk // tk),
                   pl.BlockSpec((tk, tm), lambda i, j, k: (k, i)),
                   pl.BlockSpec((tk, tn), lambda i, j, k: (k, j)),
                   jax.ShapeDtypeStruct((N_DEV, m, ns), BF16),
                   pl.BlockSpec((None, tm, tn), lambda i, j, k: (j // r, i, j % r)), _store_as, carry=carry)


def _rstd(x):
    return lax.rsqrt(jnp.mean(x * x, axis=-1, keepdims=True) + EPS)


def _norm_bwd(x, g, dy):
    r = _rstd(x)
    xh = x * r
    dyg = dy * g
    dx = r * (dyg - xh * jnp.mean(dyg * xh, axis=-1, keepdims=True))
    return dx, jnp.sum(dy * xh, axis=0, keepdims=True)


def _row_spec(tr, d):
    return pl.BlockSpec((tr, d), lambda i: (i, 0))


def _vec_spec(d):
    return pl.BlockSpec((1, d), lambda i: (0, 0))


def _accum(ref, val):
    @pl.when(pl.program_id(0) == 0)
    def _():
        ref[...] = jnp.zeros_like(ref)

    ref[...] += val


def pre_norm(x, g, carry=None, tr=256):
    t, d = x.shape
    tr = _pick(t, tr)

    def body(x_ref, g_ref, h_ref):
        xx = x_ref[...]
        h_ref[...] = (xx * _rstd(xx) * g_ref[...]).astype(BF16)

    return _call(body, name="pre_norm", grid=(t // tr,),
                 in_specs=[_row_spec(tr, d), _vec_spec(d)], out_specs=[_row_spec(tr, d)],
                 out_shape=[jax.ShapeDtypeStruct((t, d), BF16)], args=(x, g), sem=("parallel",), carry=carry)


def mid_fwd(mixed, g_post, x, g_pre2, tr=256):
    t, d = x.shape
    tr = _pick(t, tr)

    def body(m_ref, gp_ref, x_ref, g2_ref, x1_ref, h2_ref):
        mm = m_ref[...]
        x1 = x_ref[...] + mm * _rstd(mm) * gp_ref[...]
        x1_ref[...] = x1
        h2_ref[...] = (x1 * _rstd(x1) * g2_ref[...]).astype(BF16)

    return _call(body, name="mid_fwd", grid=(t // tr,),
                 in_specs=[_row_spec(tr, d), _vec_spec(d), _row_spec(tr, d), _vec_spec(d)],
                 out_specs=[_row_spec(tr, d), _row_spec(tr, d)],
                 out_shape=[jax.ShapeDtypeStruct((t, d), F32), jax.ShapeDtypeStruct((t, d), BF16)],
                 args=(mixed, g_post, x, g_pre2), sem=("parallel",))


def loss_bwd(y, g_post2, x1, target, tr=256):
    t, d = y.shape
    tr = _pick(t, tr)

    def body(y_ref, g_ref, x1_ref, t_ref, sse_ref, dout_ref, dy_ref, dg_ref):
        yy = y_ref[...]
        g = g_ref[...]
        err = x1_ref[...] + yy * _rstd(yy) * g - t_ref[...]
        _accum(sse_ref, jnp.sum(jnp.sum(err * err, axis=1, keepdims=True), axis=0, keepdims=True))
        dout = err * (1.0 / d)
        dout_ref[...] = dout
        dy, dg = _norm_bwd(yy, g, dout)
        dy_ref[...] = dy.astype(BF16)
        _accum(dg_ref, dg)

    return _call(body, name="loss_bwd", grid=(t // tr,),
                 in_specs=[_row_spec(tr, d), _vec_spec(d), _row_spec(tr, d), _row_spec(tr, d)],
                 out_specs=[pl.BlockSpec((1, 1), lambda i: (0, 0)), _row_spec(tr, d), _row_spec(tr, d), _vec_spec(d)],
                 out_shape=[jax.ShapeDtypeStruct((1, 1), F32), jax.ShapeDtypeStruct((t, d), F32),
                            jax.ShapeDtypeStruct((t, d), BF16), jax.ShapeDtypeStruct((1, d), F32)],
                 args=(y, g_post2, x1, target), sem=("arbitrary",))


def mid_bwd(dh2, x1, g_pre2, dout, mixed, g_post, carry=None, tr=256):
    t, d = x1.shape
    tr = _pick(t, tr)

    def body(dh_ref, x1_ref, g2_ref, do_ref, m_ref, gp_ref, dx1_ref, dm_ref, dg2_ref, dgp_ref):
        d1, dg2 = _norm_bwd(x1_ref[...], g2_ref[...], dh_ref[...])
        dx1 = do_ref[...] + d1
        dx1_ref[...] = dx1
        dm, dgp = _norm_bwd(m_ref[...], gp_ref[...], dx1)
        dm_ref[...] = dm.astype(BF16)
        _accum(dg2_ref, dg2)
        _accum(dgp_ref, dgp)

    return _call(body, name="mid_bwd", grid=(t // tr,),
                 in_specs=[_row_spec(tr, d), _row_spec(tr, d), _vec_spec(d), _row_spec(tr, d), _row_spec(tr, d),
                           _vec_spec(d)],
                 out_specs=[_row_spec(tr, d), _row_spec(tr, d), _vec_spec(d), _vec_spec(d)],
                 out_shape=[jax.ShapeDtypeStruct((t, d), F32), jax.ShapeDtypeStruct((t, d), BF16),
                            jax.ShapeDtypeStruct((1, d), F32), jax.ShapeDtypeStruct((1, d), F32)],
                 args=(dh2, x1, g_pre2, dout, mixed, g_post), sem=("arbitrary",), carry=carry)


def first_bwd(dh1, x, g_pre, dx1, carry=None, tr=256):
    t, d = x.shape
    tr = _pick(t, tr)

    def body(dh_ref, x_ref, g_ref, dx1_ref, gx_ref, dg_ref):
        d0, dg = _norm_bwd(x_ref[...], g_ref[...], dh_ref[...])
        gx_ref[...] = dx1_ref[...] + d0
        _accum(dg_ref, dg)

    return _call(body, name="first_bwd", grid=(t // tr,),
                 in_specs=[_row_spec(tr, d), _row_spec(tr, d), _vec_spec(d), _row_spec(tr, d)],
                 out_specs=[_row_spec(tr, d), _vec_spec(d)],
                 out_shape=[jax.ShapeDtypeStruct((t, d), F32), jax.ShapeDtypeStruct((1, d), F32)],
                 args=(dh1, x, g_pre, dx1), sem=("arbitrary",), carry=carry)


def _attn_geometry(has_prev):
    r = lax.broadcasted_iota(jnp.int32, (BLOCK, 2 * BLOCK), 0)
    c = lax.broadcasted_iota(jnp.int32, (BLOCK, 2 * BLOCK), 1)
    dist = r + BLOCK - c
    valid = jnp.logical_and(jnp.logical_and(dist >= 0, dist < BLOCK), jnp.logical_or(c >= BLOCK, has_prev))
    return dist.astype(F32), valid


def _stack_pairs(x, g, pairs):
    base = g * pairs * LANES
    return jnp.concatenate([x[:, base + p * LANES:base + (p + 1) * LANES] for p in range(pairs)], axis=0)


def _unstack_pairs(xs, pairs):
    return jnp.concatenate([xs[p * BLOCK:(p + 1) * BLOCK, :] for p in range(pairs)], axis=1)


def _to_half(x, g, odd):
    lane = lax.broadcasted_iota(jnp.int32, x.shape, 1)
    y = x if (g == 1) == odd else pltpu.roll(x, HEAD_DIM, axis=1)
    return jnp.where((lane >= HEAD_DIM) == odd, y, 0.0)


def _from_halves(even, odd, g):
    lane = lax.broadcasted_iota(jnp.int32, even.shape, 1)
    if g == 0:
        return jnp.where(lane < HEAD_DIM, even + pltpu.roll(odd, HEAD_DIM, axis=1), 0.0)
    return jnp.where(lane >= HEAD_DIM, pltpu.roll(even, HEAD_DIM, axis=1) + odd, 0.0)


_PARITIES = [(g, odd) for g in range(N_KV_HEADS) for odd in (False, True)]


def _softmax_sink(s, sink_ref, g, odd, group, n_heads, geo):
    dist, valid = geo
    pairs = group // 2
    heads = [g * group + 2 * p + int(odd) for p in range(pairs)]
    bias = jnp.concatenate([(2.0 ** (-8.0 * (h + 1) / n_heads)) * dist for h in heads], axis=0)
    sink = jnp.concatenate([jnp.full((BLOCK, 1), sink_ref[0, h], F32) for h in heads], axis=0)
    s = jnp.where(jnp.concatenate([valid] * pairs, axis=0), s - bias, -jnp.inf)
    m = jnp.maximum(jnp.max(s, axis=-1, keepdims=True), sink)
    p = jnp.exp(s - m)
    p_sink = jnp.exp(sink - m)
    inv = 1.0 / (jnp.sum(p, axis=-1, keepdims=True) + p_sink)
    return p * inv, p_sink * inv


def attn_fwd(proj, sinks, gain, aw, carry=None):
    t = proj.shape[0]
    kw = N_KV_HEADS * HEAD_DIM
    n_heads = aw // HEAD_DIM
    group = n_heads // N_KV_HEADS
    pairs = group // 2
    assert kw == LANES and group % 2 == 0
    nb = t // BLOCK
    scale = HEAD_DIM ** -0.5

    def body(sink_ref, q_ref, k_ref, v_ref, g_ref, o_ref, on_ref):
        n = pl.program_id(0)
        cur = pl.multiple_of(n * BLOCK, BLOCK)
        prev = pl.multiple_of(jnp.maximum(n - 1, 0) * BLOCK, BLOCK)
        geo = _attn_geometry(n > 0)
        kcat = jnp.concatenate([k_ref[pl.ds(prev, BLOCK), :], k_ref[pl.ds(cur, BLOCK), :]], axis=0)
        vcat = jnp.concatenate([v_ref[pl.ds(prev, BLOCK), :], v_ref[pl.ds(cur, BLOCK), :]], axis=0)
        q = q_ref[...] * scale
        qs = [_stack_pairs(q, g, pairs) for g in range(N_KV_HEADS)]
        scores = [_dot(qs[g], _to_half(kcat, g, odd), NT) for g, odd in _PARITIES]
        probs = [_softmax_sink(s, sink_ref, g, odd, group, n_heads, geo)[0] for s, (g, odd) in zip(scores, _PARITIES)]
        outs = [_dot(p, _to_half(vcat, g, odd), NN) for p, (g, odd) in zip(probs, _PARITIES)]
        o = jnp.concatenate([_unstack_pairs(outs[2 * g] + outs[2 * g + 1], pairs) for g in range(N_KV_HEADS)], axis=1)
        o_ref[...] = o
        on_ref[...] = (o * _rstd(o) * g_ref[...]).astype(BF16)

    return _call(body, name="attn_fwd", grid=(nb,),
                 in_specs=[pl.BlockSpec(memory_space=pltpu.SMEM),
                           pl.BlockSpec((BLOCK, aw), lambda n: (n, 0)),
                           pl.BlockSpec((t, kw), lambda n: (0, aw // kw)),
                           pl.BlockSpec((t, kw), lambda n: (0, aw // kw + 1)),
                           pl.BlockSpec((1, aw), lambda n: (0, 0))],
                 out_specs=[pl.BlockSpec((BLOCK, aw), lambda n: (n, 0)), pl.BlockSpec((BLOCK, aw), lambda n: (n, 0))],
                 out_shape=[jax.ShapeDtypeStruct((t, aw), F32), jax.ShapeDtypeStruct((t, aw), BF16)],
                 args=(sinks, proj, proj, proj, gain), sem=("parallel",), carry=carry)


def attn_bwd(proj, sinks, gain, attn_o, dcat, aw, carry=None):
    t = proj.shape[0]
    kw = N_KV_HEADS * HEAD_DIM
    n_heads = aw // HEAD_DIM
    group = n_heads // N_KV_HEADS
    pairs = group // 2
    assert kw == LANES and group % 2 == 0
    nb = t // BLOCK
    scale = HEAD_DIM ** -0.5

    def body(sink_ref, q_ref, k_ref, v_ref, g_ref, o_ref, dn_ref, dq_ref, dk_ref, dv_ref, dsink_ref, dg_ref):
        n = pl.program_id(0)
        cur = pl.multiple_of(n * BLOCK, BLOCK)
        prev = pl.multiple_of(jnp.maximum(n - 1, 0) * BLOCK, BLOCK)
        geo = _attn_geometry(n > 0)

        @pl.when(n == 0)
        def _():
            dk_ref[...] = jnp.zeros_like(dk_ref)
            dv_ref[...] = jnp.zeros_like(dv_ref)
            dsink_ref[...] = jnp.zeros_like(dsink_ref)

        o = o_ref[...]
        do_all, dg = _norm_bwd(o, g_ref[...], dn_ref[...])
        _accum(dg_ref, dg)
        kcat = jnp.concatenate([k_ref[pl.ds(prev, BLOCK), :], k_ref[pl.ds(cur, BLOCK), :]], axis=0)
        vcat = jnp.concatenate([v_ref[pl.ds(prev, BLOCK), :], v_ref[pl.ds(cur, BLOCK), :]], axis=0)
        q = q_ref[...] * scale
        lane = lax.broadcasted_iota(jnp.int32, (1, LANES), 1)
        lane_s = lax.broadcasted_iota(jnp.int32, (pairs * BLOCK, LANES), 1)
        qs = [_stack_pairs(q, g, pairs) for g in range(N_KV_HEADS)]
        dos = [_stack_pairs(do_all, g, pairs) for g in range(N_KV_HEADS)]
        kxs = [_to_half(kcat, g, odd) for g, odd in _PARITIES]
        scores = [_dot(qs[g], kx, NT) for kx, (g, odd) in zip(kxs, _PARITIES)]
        dps = [_dot(dos[g], _to_half(vcat, g, odd), NT) for g, odd in _PARITIES]
        deltas = []
        for g in range(N_KV_HEADS):
            prod = dos[g] * _stack_pairs(o, g, pairs)
            delta_even = jnp.sum(jnp.where(lane_s < HEAD_DIM, prod, 0.0), axis=-1, keepdims=True)
            deltas += [delta_even, jnp.sum(prod, axis=-1, keepdims=True) - delta_even]
        dsink = jnp.zeros((1, LANES), F32)
        ps, dss = [], []
        for i, (g, odd) in enumerate(_PARITIES):
            p, p_sink = _softmax_sink(scores[i], sink_ref, g, odd, group, n_heads, geo)
            ps.append(p)
            dss.append(p * (dps[i] - deltas[i]))
            sink_rows = p_sink * deltas[i]
            for pr in range(pairs):
                h = g * group + 2 * pr + int(odd)
                dsink = dsink + jnp.where(
                    lane == h, -jnp.sum(sink_rows[pr * BLOCK:(pr + 1) * BLOCK], axis=0, keepdims=True), 0.0)
        dq_pairs = [_dot(ds, kx, NN) for ds, kx in zip(dss, kxs)]
        dk_halves = [_dot(ds, qs[g], TN) for ds, (g, odd) in zip(dss, _PARITIES)]
        dv_halves = [_dot(p, dos[g], TN) for p, (g, odd) in zip(ps, _PARITIES)]
        dq_ref[...] = jnp.concatenate(
            [_unstack_pairs((dq_pairs[2 * g] + dq_pairs[2 * g + 1]) * scale, pairs) for g in range(N_KV_HEADS)],
            axis=1).astype(BF16)
        dk_upd = _from_halves(dk_halves[0], dk_halves[1], 0) + _from_halves(dk_halves[2], dk_halves[3], 1)
        dv_upd = _from_halves(dv_halves[0], dv_halves[1], 0) + _from_halves(dv_halves[2], dv_halves[3], 1)
        dk_ref[pl.ds(prev, BLOCK), :] += dk_upd[:BLOCK]
        dv_ref[pl.ds(prev, BLOCK), :] += dv_upd[:BLOCK]
        dk_ref[pl.ds(cur, BLOCK), :] += dk_upd[BLOCK:]
        dv_ref[pl.ds(cur, BLOCK), :] += dv_upd[BLOCK:]
        dsink_ref[...] += dsink

    return _call(body, name="attn_bwd", grid=(nb,),
                 in_specs=[pl.BlockSpec(memory_space=pltpu.SMEM),
                           pl.BlockSpec((BLOCK, aw), lambda n: (n, 0)),
                           pl.BlockSpec((t, kw), lambda n: (0, aw // kw)),
                           pl.BlockSpec((t, kw), lambda n: (0, aw // kw + 1)),
                           pl.BlockSpec((1, aw), lambda n: (0, 0)),
                           pl.BlockSpec((BLOCK, aw), lambda n: (n, 0)),
                           pl.BlockSpec((BLOCK, aw), lambda n: (n, 0))],
                 out_specs=[pl.BlockSpec((BLOCK, aw), lambda n: (n, 0)),
                            pl.BlockSpec((t, kw), lambda n: (0, 0)), pl.BlockSpec((t, kw), lambda n: (0, 0)),
                            pl.BlockSpec((1, LANES), lambda n: (0, 0)), pl.BlockSpec((1, aw), lambda n: (0, 0))],
                 out_shape=[jax.ShapeDtypeStruct((t, aw), BF16), jax.ShapeDtypeStruct((t, kw), F32),
                            jax.ShapeDtypeStruct((t, kw), F32), jax.ShapeDtypeStruct((1, LANES), F32),
                            jax.ShapeDtypeStruct((1, aw), F32)],
                 args=(sinks, proj, proj, proj, gain, attn_o, dcat), sem=("arbitrary",), carry=carry)


def _sigmoid(x):
    return 0.5 * jnp.tanh(0.5 * x) + 0.5


def _chunk_geometry():
    row = lax.broadcasted_iota(jnp.int32, (CHUNK, CHUNK), 0)
    col = lax.broadcasted_iota(jnp.int32, (CHUNK, CHUNK), 1)
    return row, col


def _cumsum_rows(x, reverse=False):
    row, col = _chunk_geometry()
    tri = (col >= row) if reverse else (col <= row)
    return lax.dot_general(tri.astype(F32), x, ((NN), ((), ())), precision=HI, preferred_element_type=F32)


def _rep_sub(x4, sub):
    k = x4.shape[-1]
    return jnp.broadcast_to(x4[:, None, :], (CHUNK // sub, sub, k)).reshape(CHUNK, k)


def _gates(q_r, f_r, lb):
    sg = _sigmoid(f_r)
    f = lb + (1.0 - lb) * sg
    sq = _sigmoid(q_r)
    return sg, f, sq, q_r * sq


def _offdiag_terms(b, j, sub):
    c = b[j * sub + sub - 1:j * sub + sub, :]
    return jnp.exp(jnp.minimum(b - c, 0.0)), jnp.exp(jnp.minimum(c - b, 0.0))


def _store_heads(ref, x):
    for j in range(ref.shape[0]):
        ref[j] = x[:, _head(j)]


def _sub_rows(ref, r, sub):
    rows = [ref[j, pl.ds(r, CHUNK // sub, stride=sub), :] for j in range(ref.shape[0])]
    return _rep_sub(jnp.concatenate(rows, axis=1), sub)


def _diag_mask(sub):
    row, col = _chunk_geometry()
    return jnp.logical_and((row // sub) == (col // sub), row >= col)


HGRN_HEADS_PER_STEP = 8


def _wide(refs):
    return jnp.concatenate([r[...] for r in refs], axis=1)


def _head(j):
    return slice(j * RNN_HEAD_DIM, (j + 1) * RNN_HEAD_DIM)


def _cat_heads(parts, hs):
    return jnp.concatenate([p[:, hs] for p in parts], axis=1)


def _offdiag_factors(q, k, b, sub):
    rowi = lax.broadcasted_iota(jnp.int32, b.shape, 0)
    qs, ks, ers, ecs = [], [], [], []
    for j in range(CHUNK // sub - 1):
        e_row, e_col = _offdiag_terms(b, j, sub)
        e_row = jnp.where(rowi >= (j + 1) * sub, e_row, 0.0)
        e_col = jnp.where((rowi // sub) == j, e_col, 0.0)
        qs.append(q * e_row)
        ks.append(k * e_col)
        ers.append(e_row)
        ecs.append(e_col)
    return qs, ks, ers, ecs


def hgrn_fwd(proj, attn_n, lb, norm_gain, col0, rw, carry=None):
    t, aw = attn_n.shape
    nh = rw // RNN_HEAD_DIM
    nc = t // CHUNK
    kd = RNN_HEAD_DIM
    cb = col0 // kd
    sub = SUB_FWD
    nsub = CHUNK // sub
    hp = nh
    assert nh <= HGRN_HEADS_PER_STEP
    w = hp * kd

    def body(*refs):
        q_refs, f_refs, i_refs, g_refs = (refs[i * hp:(i + 1) * hp] for i in range(4))
        lb_ref, ng_ref, an_ref, cat_ref, o_ref, att_ref, st_ref, state, b_ref, k_ref = refs[4 * hp:]
        c = pl.program_id(1)

        @pl.when(c == 0)
        def _():
            state[...] = jnp.zeros_like(state)

        st_ref[...] = state[...]
        q_r, f_r, v, g_r = (_wide(rs) for rs in (q_refs, f_refs, i_refs, g_refs))
        _, f, _, q = _gates(q_r, f_r, lb_ref[...])
        k = 1.0 - f
        b = _cumsum_rows(jnp.log(f))
        _store_heads(b_ref, b)
        _store_heads(k_ref, k)
        qcat, kcat, _, _ = _offdiag_factors(q, k, b, sub)
        row, col = _chunk_geometry()
        same = (row // sub) == (col // sub)
        rloc = lax.broadcasted_iota(jnp.int32, (CHUNK, w), 0) % sub
        diag = [jnp.zeros((CHUNK, CHUNK), F32)] * hp
        for r in range(sub):
            bs = _sub_rows(b_ref, r, sub)
            ks = _sub_rows(k_ref, r, sub)
            prod = q * jnp.exp(jnp.where(rloc >= r, b - bs, -jnp.inf)) * ks
            place = jnp.logical_and((col % sub) == r, same)
            diag = [jnp.where(place, jnp.sum(prod[:, _head(j)], axis=-1, keepdims=True), diag[j]) for j in range(hp)]
        b_last = b[CHUNK - 1:CHUNK, :]
        qe = q * jnp.exp(b)
        kdec = k * jnp.exp(b_last - b)
        decay = jnp.exp(b_last)
        outs, normed, states = [], [], []
        for j in range(hp):
            hs = _head(j)
            att = diag[j] + _dot(_cat_heads(qcat, hs), _cat_heads(kcat, hs), NT)
            att_ref[j] = att
            sj = state[j]
            o = _dot(qe[:, hs], sj, NT) + _dot(att, v[:, hs], NN)
            outs.append(o)
            normed.append(o * _rstd(o))
            states.append(sj * decay[:, hs] + _dot(v[:, hs], kdec[:, hs], TN))
        for j in range(hp):
            state[j] = states[j]
        o_ref[...] = jnp.concatenate(outs, axis=1)
        gate = g_r * _sigmoid(g_r)
        cat_ref[:, :aw] = an_ref[...]
        cat_ref[:, aw:] = (jnp.concatenate(normed, axis=1) * jnp.tile(ng_ref[...], (1, hp)) * gate).astype(BF16)

    def col(kidx, j):
        return pl.BlockSpec((CHUNK, kd), lambda hg, c: (c, cb + kidx * nh + hg * hp + j))

    return _call(body, name="hgrn_fwd", grid=(1, nc),
                 in_specs=[col(kidx, j) for kidx in range(4) for j in range(hp)] +
                          [pl.BlockSpec((1, w), lambda hg, c: (0, hg)), pl.BlockSpec((1, kd), lambda hg, c: (0, 0)),
                           pl.BlockSpec((CHUNK, aw), lambda hg, c: (c, 0))],
                 out_specs=[pl.BlockSpec((CHUNK, aw + w), lambda hg, c: (c, 0)),
                            pl.BlockSpec((CHUNK, w), lambda hg, c: (c, hg)),
                            pl.BlockSpec((hp, CHUNK, CHUNK), lambda hg, c: (hg, c, 0)),
                            pl.BlockSpec((None, hp, kd, kd), lambda hg, c: (c, hg, 0, 0))],
                 out_shape=[jax.ShapeDtypeStruct((t, aw + rw), BF16), jax.ShapeDtypeStruct((t, rw), F32),
                            jax.ShapeDtypeStruct((nh, t, CHUNK), F32), jax.ShapeDtypeStruct((nc, nh, kd, kd), F32)],
                 args=(*([proj] * (4 * hp)), lb, norm_gain, attn_n),
                 scratch_shapes=[pltpu.VMEM((hp, kd, kd), F32), pltpu.VMEM((hp, CHUNK, kd), F32),
                                 pltpu.VMEM((hp, CHUNK, kd), F32)],
                 sem=("parallel", "arbitrary"), carry=carry)


def hgrn_bwd(proj, lb, norm_gain, o_all, att_all, st_all, dcat, dq_a, dk_a, dv_a, col0, rw, carry=None):
    t, iw = proj.shape
    aw, kw = dq_a.shape[1], dk_a.shape[1]
    nh = rw // RNN_HEAD_DIM
    nc = t // CHUNK
    kd = RNN_HEAD_DIM
    cb = col0 // kd
    sub = SUB_BWD
    nsub = CHUNK // sub
    dcb = (dcat.shape[1] - rw) // kd
    hp = nh
    assert nh <= HGRN_HEADS_PER_STEP and dcb % hp == 0 and col0 == aw + 2 * kw and iw == col0 + 4 * rw
    w = hp * kd

    def per_head(x, fn):
        return jnp.concatenate([jnp.broadcast_to(fn(x[:, _head(j)]), (CHUNK, kd)) for j in range(hp)], axis=1)

    def body(*refs):
        q_refs, f_refs, i_refs, g_refs = (refs[i * hp:(i + 1) * hp] for i in range(4))
        (lb_ref, ng_ref, o_ref, att_ref, st0_ref, st1_ref, d_ref, dqa_ref, dka_ref, dva_ref, dp_ref, dlb_ref, dng_ref,
         dstate, b_ref, k_ref, dks_ref) = refs[4 * hp:]
        ci = pl.program_id(1)

        @pl.when(ci == 0)
        def _():
            dstate[...] = jnp.zeros_like(dstate)
            dlb_ref[...] = jnp.zeros_like(dlb_ref)
            dng_ref[...] = jnp.zeros_like(dng_ref)

        lbv = lb_ref[...]
        q_r, f_r, v, g_r = (_wide(rs) for rs in (q_refs, f_refs, i_refs, g_refs))
        sg, f, sq, q = _gates(q_r, f_r, lbv)
        k = 1.0 - f
        b = _cumsum_rows(jnp.log(f))
        _store_heads(b_ref, b)
        _store_heads(k_ref, k)
        row, col = _chunk_geometry()

        o = o_ref[...]
        ng = jnp.tile(ng_ref[...], (1, hp))
        sgg = _sigmoid(g_r)
        gate = g_r * sgg
        d_rnn = d_ref[...]
        r = per_head(o, _rstd)
        oh = o * r
        dp_ref[:, :aw] = dqa_ref[...]
        dp_ref[:, aw:aw + kw] = dka_ref[...].astype(BF16)
        dp_ref[:, aw + kw:col0] = dva_ref[...].astype(BF16)
        dp_ref[:, col0 + 3 * rw:] = (d_rnn * oh * ng * (sgg * (1.0 + g_r * (1.0 - sgg)))).astype(BF16)
        d_on = d_rnn * gate
        dng_rows = jnp.sum(d_on * oh, axis=0, keepdims=True)
        dng = dng_rows[:, _head(0)]
        for j in range(1, hp):
            dng = dng + dng_rows[:, _head(j)]
        dng_ref[...] += dng
        dyg = d_on * ng
        do = r * (dyg - oh * per_head(dyg * oh, lambda x: jnp.mean(x, axis=-1, keepdims=True)))

        b_last = b[CHUNK - 1:CHUNK, :]
        eb = jnp.exp(b)
        tail = jnp.exp(b_last - b)
        kdec = k * tail
        decay = jnp.exp(b_last)
        qe = q * eb
        qcat, kcat, ers, ecs = _offdiag_factors(q, k, b, sub)
        diag_mask = _diag_mask(sub)
        dqs, dks, dvs, dads, gsums, dstates = [], [], [], [], [], []
        for j in range(hp):
            hs = _head(j)
            do_h, v_h, dst = do[:, hs], v[:, hs], dstate[j]
            da = jnp.where(row >= col, _dot(do_h, v_h, NT), 0.0)
            dads.append(jnp.where(diag_mask, da, 0.0))
            dq = _dot(do_h, st0_ref[j], NN) * eb[:, hs]
            dk = _dot(v_h, dst, NN) * tail[:, hs]
            dvs.append(_dot(att_ref[j], do_h, TN) + _dot(kdec[:, hs], dst, NT))
            rq = _dot(da, _cat_heads(kcat, hs), NN)
            rk = _dot(da, _cat_heads(qcat, hs), TN)
            for jj in range(nsub - 1):
                dq = dq + ers[jj][:, hs] * rq[:, _head(jj)]
                dk = dk + ecs[jj][:, hs] * rk[:, _head(jj)]
            dqs.append(dq)
            dks.append(dk)
            gsums.append(jnp.sum(dst * st1_ref[j], axis=0, keepdims=True))
            dstates.append(dst * decay[:, hs] + _dot(do_h, qe[:, hs], TN))
        for j in range(hp):
            dstate[j] = dstates[j]
        dq = jnp.concatenate(dqs, axis=1)
        dk = jnp.concatenate(dks, axis=1)
        rloc = lax.broadcasted_iota(jnp.int32, (CHUNK, w), 0) % sub
        for rr in range(sub):
            bs = _sub_rows(b_ref, rr, sub)
            ks = _sub_rows(k_ref, rr, sub)
            e = jnp.exp(jnp.where(rloc >= rr, b - bs, -jnp.inf))
            pick = (col % sub) == rr
            dacol = jnp.concatenate(
                [jnp.broadcast_to(jnp.sum(jnp.where(pick, dads[j], 0.0), axis=-1, keepdims=True), (CHUNK, kd))
                 for j in range(hp)], axis=1)
            wv = dacol * e
            dq = dq + wv * ks
            sums = jnp.sum((wv * q).reshape(nsub, sub, w), axis=1)
            for j in range(hp):
                dks_ref[j, pl.ds(rr, nsub, stride=sub), :] = sums[:, _head(j)]
        dk = dk + jnp.concatenate([dks_ref[j] for j in range(hp)], axis=1)

        dlf = _cumsum_rows(q * dq - k * dk, reverse=True) + jnp.concatenate(gsums, axis=1)
        dfv = dlf / f - dk
        dp_ref[:, col0 + rw:col0 + 2 * rw] = (dfv * (1.0 - lbv) * sg * (1.0 - sg)).astype(BF16)
        dlb_ref[...] += jnp.sum(dfv * (1.0 - sg), axis=0, keepdims=True)
        dp_ref[:, col0:col0 + rw] = (dq * (sq * (1.0 + q_r * (1.0 - sq)))).astype(BF16)
        dp_ref[:, col0 + 2 * rw:col0 + 3 * rw] = jnp.concatenate(dvs, axis=1).astype(BF16)

    def rev(c):
        return nc - 1 - c

    def col_in(kidx, j):
        return pl.BlockSpec((CHUNK, kd), lambda hg, c: (rev(c), cb + kidx * nh + hg * hp + j))

    def rows(width):
        return pl.BlockSpec((CHUNK, width), lambda hg, c: (rev(c), 0))

    return _call(body, name="hgrn_bwd", grid=(1, nc),
                 in_specs=[col_in(kidx, j) for kidx in range(4) for j in range(hp)] +
                          [pl.BlockSpec((1, w), lambda hg, c: (0, hg)), pl.BlockSpec((1, kd), lambda hg, c: (0, 0)),
                           rows(w),
                           pl.BlockSpec((hp, CHUNK, CHUNK), lambda hg, c: (hg, rev(c), 0)),
                           pl.BlockSpec((None, hp, kd, kd), lambda hg, c: (rev(c), hg, 0, 0)),
                           pl.BlockSpec((None, hp, kd, kd),
                                        lambda hg, c: (jnp.minimum(rev(c) + 1, nc - 1), hg, 0, 0)),
                           pl.BlockSpec((CHUNK, w), lambda hg, c: (rev(c), dcb // hp + hg)),
                           rows(aw), rows(kw), rows(kw)],
                 out_specs=[rows(iw),
                            pl.BlockSpec((1, w), lambda hg, c: (0, hg)),
                            pl.BlockSpec((None, 1, kd), lambda hg, c: (hg, 0, 0))],
                 out_shape=[jax.ShapeDtypeStruct((t, iw), BF16), jax.ShapeDtypeStruct((1, rw), F32),
                            jax.ShapeDtypeStruct((1, 1, kd), F32)],
                 args=(*([proj] * (4 * hp)), lb, norm_gain, o_all, att_all, st_all, st_all, dcat, dq_a, dk_a, dv_a),
                 scratch_shapes=[pltpu.VMEM((hp, kd, kd), F32), pltpu.VMEM((hp, CHUNK, kd), F32),
                                 pltpu.VMEM((hp, CHUNK, kd), F32), pltpu.VMEM((hp, CHUNK, kd), F32)],
                 sem=("parallel", "arbitrary"), carry=carry)


def comm_only(name, part):
    return _call(lambda: None, name=name, grid=(), in_specs=[], out_specs=[], out_shape=[], args=(), carry=part)[1]


ADD_BLOCK_ELEMS = 1 << 20
ADAMW_BLOCK_ELEMS = 1 << 19


def add_kept_half(name, kept, got, sel, minor, row0=0):
    pieces, rows, cols = got.shape
    tr = _tile(rows, max(16, ADD_BLOCK_ELEMS // cols), mult=16)
    assert row0 % tr == 0
    i0 = row0 // tr

    def body(sel_ref, k_ref, g_ref, o_ref):
        o_ref[...] = (k_ref[...].astype(F32) + g_ref[...].astype(F32)).astype(o_ref.dtype)

    kept_spec = (pl.BlockSpec((None, None, tr, cols), lambda p, i, s: (p, s[0], i + i0, 0)) if minor else
                 pl.BlockSpec((None, None, tr, cols), lambda p, i, s: (s[0], p, i + i0, 0)))
    return pl.pallas_call(
        body, name=name,
        grid_spec=pltpu.PrefetchScalarGridSpec(
            num_scalar_prefetch=1, grid=(pieces, rows // tr),
            in_specs=[kept_spec, pl.BlockSpec((None, tr, cols), lambda p, i, s: (p, i, 0))],
            out_specs=pl.BlockSpec((None, tr, cols), lambda p, i, s: (p, i, 0))),
        out_shape=jax.ShapeDtypeStruct(got.shape, got.dtype),
        compiler_params=_cparams(("parallel", "parallel")),
    )(sel, kept, got)


def _adamw(w, g, m, v):
    m = ADAM_B1 * m + (1.0 - ADAM_B1) * g
    v = ADAM_B2 * v + (1.0 - ADAM_B2) * (g * g)
    m_hat = m / (1.0 - ADAM_B1 ** ADAM_STEP)
    v_hat = v / (1.0 - ADAM_B2 ** ADAM_STEP)
    delta = -ADAM_LR * (m_hat / (jnp.sqrt(v_hat) + ADAM_EPS) + ADAM_WD * w)
    return delta, m, v


def add_adamw(name, kept, got, sel, w, m, v, row0=0, into=None):
    _, rows, cols = got.shape
    tr = _tile(rows, max(16, ADAMW_BLOCK_ELEMS // cols), mult=16)
    assert row0 % tr == 0
    i0 = row0 // tr
    n_into = 0 if into is None else len(into)

    def body(sel_ref, k_ref, g_ref, w_ref, m_ref, v_ref, *rest):
        go_ref, d_ref, mo_ref, vo_ref = rest[n_into:]
        g = k_ref[...].astype(F32) + g_ref[...].astype(F32)
        go_ref[...] = g
        d_ref[...], mo_ref[...], vo_ref[...] = _adamw(w_ref[...], g, m_ref[...], v_ref[...])

    shard_tile = pl.BlockSpec((tr, cols), lambda i, s: (i + i0, 0))
    return pl.pallas_call(
        body, name=name,
        grid_spec=pltpu.PrefetchScalarGridSpec(
            num_scalar_prefetch=1, grid=(rows // tr,),
            in_specs=[pl.BlockSpec((None, None, tr, cols), lambda i, s: (s[0], 0, i, 0)),
                      pl.BlockSpec((None, tr, cols), lambda i, s: (0, i, 0)), shard_tile, shard_tile, shard_tile,
                      *[ANY] * n_into],
            out_specs=[shard_tile] * 4),
        out_shape=[jax.ShapeDtypeStruct(w.shape, F32)] * 4,
        input_output_aliases={6 + k: k for k in range(n_into)},
        compiler_params=_cparams(("parallel",)),
    )(sel, kept, got, w, m, v, *(into or ()))


def small_allreduce_adamw(partial, scale, w, m, v):
    rows = partial.shape[0]

    def body(p_ref, s_ref, w_ref, m_ref, v_ref, g_ref, d_ref, mo_ref, vo_ref, slots, send_sems, recv_sems):
        x, y, c = _coords()
        my_slot = _slab_index((x, y, c))
        slots[my_slot] = p_ref[...]
        copies = []
        for mask in range(1, N_DEV):
            to = tuple(1 - v_ if (mask >> s_) & 1 else v_ for v_, s_ in ((x, 2), (y, 1), (c, 0)))
            copies.append(pltpu.make_async_remote_copy(
                src_ref=p_ref, dst_ref=slots.at[my_slot],
                send_sem=send_sems.at[mask - 1], recv_sem=recv_sems.at[mask - 1],
                device_id=to, device_id_type=MESH))
        for cp in copies:
            cp.start()
        for cp in copies:
            cp.wait()
        total = slots[0]
        for b in range(1, N_DEV):
            total = total + slots[b]
        g = total * s_ref[...]
        g_ref[...] = g
        d_ref[...], mo_ref[...], vo_ref[...] = _adamw(w_ref[...], g, m_ref[...], v_ref[...])

    vm = pl.BlockSpec(memory_space=pltpu.VMEM)
    return pl.pallas_call(
        body, name="small_allreduce_adamw",
        in_specs=[vm] * 5, out_specs=[vm] * 4,
        out_shape=[jax.ShapeDtypeStruct((rows, LANES), F32)] * 4,
        scratch_shapes=[pltpu.VMEM((N_DEV, rows, LANES), F32),
                        pltpu.SemaphoreType.DMA((N_DEV - 1,)), pltpu.SemaphoreType.DMA((N_DEV - 1,))],
        compiler_params=pltpu.CompilerParams(has_side_effects=True),
    )(partial, scale, w, m, v)


_SMALL = ("attn_sinks", "attn_out_gain", "rnn_lb_logits", "rnn_norm_gain", "mix_pre_gain", "mix_post_gain",
          "mlp_pre_gain", "mlp_post_gain")


def _pack(parts):
    rows = []
    for p in parts:
        flat = p.reshape(-1).astype(F32)
        pad = (-flat.shape[0]) % LANES
        rows.append(jnp.pad(flat, (0, pad)).reshape(-1, LANES))
    packed = jnp.concatenate(rows, axis=0)
    pad_rows = (-packed.shape[0]) % 8
    return jnp.pad(packed, ((0, pad_rows), (0, 0)))


def _unpack(packed, shapes):
    out, r = [], 0
    for s in shapes:
        size = math.prod(s)
        nrows = -(-size // LANES)
        out.append(packed[r:r + nrows].reshape(-1)[:size].reshape(s))
        r += nrows
    return out


class _Scatter:
    def __init__(self, tag, grad, sels, both_links=False):
        self.tag, self.sels, self.both = tag, sels, both_links
        self.shape = grad.shape[1:]
        self.half = self.shape[0] // 2
        self.cur = grad.reshape(4, 2, *self.shape)
        self.stage = 0

    def step(self):
        if self.stage == 0 or not self.both:
            return _scatter_step(self.cur, "cxy"[self.stage])
        if self.stage == 1:
            return _merge(_scatter_step(self.cur, "x", rows=(0, self.half)),
                          _scatter_step(self.cur, "y", minor=True, rows=(self.half, self.shape[0])))
        upper, lower = self.cur
        return _merge(_scatter_step(upper, "y"), _scatter_step(lower, "x"))

    def land(self, got, w=None, m=None, v=None):
        stage, tag, sels = self.stage, self.tag, self.sels
        self.stage += 1
        if stage == 0 or not self.both:
            axis = "cxy"[stage]
            name = "rs_add_%s_%s" % (axis, tag)
            if axis == "y":
                return add_adamw(name, self.cur, got, sels[axis], w, m, v)
            summed = add_kept_half(name, self.cur, got, sels[axis], minor=axis == "c")
            self.cur = summed.reshape(2, summed.shape[0] // 2, *self.shape)
            return None
        got_upper, got_lower = got
        if stage == 1:
            upper = add_kept_half("rs_add_x_%s_upper" % tag, self.cur, got_upper, sels["x"], minor=False)
            lower = add_kept_half("rs_add_y_%s_lower" % tag, self.cur, got_lower, sels["y"], minor=True,
                                  row0=self.half)
            self.cur = tuple(s.reshape(2, 1, *s.shape[1:]) for s in (upper, lower))
            return None
        upper, lower = self.cur
        out_upper = add_adamw("rs_add_y_%s_upper" % tag, upper, got_upper, sels["y"], w, m, v)
        return add_adamw("rs_add_x_%s_lower" % tag, lower, got_lower, sels["x"], w, m, v, row0=self.half,
                         into=out_upper)


def kernel(x, w_in, attn_sinks, attn_out_gain, rnn_lb_logits, rnn_norm_gain, w_out, mix_pre_gain, mix_post_gain, mlp_pre_gain, mlp_post_gain, w_up, w_down, loss_target, m_w_in, m_attn_sinks, m_attn_out_gain, m_rnn_lb_logits, m_rnn_norm_gain, m_w_out, m_mix_pre_gain, m_mix_post_gain, m_mlp_pre_gain, m_mlp_post_gain, m_w_up, m_w_down, v_w_in, v_attn_sinks, v_attn_out_gain, v_rnn_lb_logits, v_rnn_norm_gain, v_w_out, v_mix_pre_gain, v_mix_post_gain, v_mlp_pre_gain, v_mlp_post_gain, v_w_up, v_w_down):
    xs, target = x[0], loss_target[0]
    t, d = xs.shape
    aw = d // 2
    rw = d - aw
    col0 = aw + 2 * N_KV_HEADS * HEAD_DIM
    small_w = dict(attn_sinks=attn_sinks, attn_out_gain=attn_out_gain, rnn_lb_logits=rnn_lb_logits,
                   rnn_norm_gain=rnn_norm_gain, mix_pre_gain=mix_pre_gain, mix_post_gain=mix_post_gain,
                   mlp_pre_gain=mlp_pre_gain, mlp_post_gain=mlp_post_gain)
    small_m = dict(attn_sinks=m_attn_sinks, attn_out_gain=m_attn_out_gain, rnn_lb_logits=m_rnn_lb_logits,
                   rnn_norm_gain=m_rnn_norm_gain, mix_pre_gain=m_mix_pre_gain, mix_post_gain=m_mix_post_gain,
                   mlp_pre_gain=m_mlp_pre_gain, mlp_post_gain=m_mlp_post_gain)
    small_v = dict(attn_sinks=v_attn_sinks, attn_out_gain=v_attn_out_gain, rnn_lb_logits=v_rnn_lb_logits,
                   rnn_norm_gain=v_rnn_norm_gain, mix_pre_gain=v_mix_pre_gain, mix_post_gain=v_mix_post_gain,
                   mlp_pre_gain=v_mlp_pre_gain, mlp_post_gain=v_mlp_post_gain)
    cx, cy, cc = _coords()
    sels = {a: jnp.reshape(v_, (1,)).astype(jnp.int32) for a, v_ in (("x", cx), ("y", cy), ("c", cc))}

    w_in_t, m_in_t, v_in_t = w_in[0].T, m_w_in[0].T, v_w_in[0].T
    s_in, s_out, s_up, s_down = (w.astype(BF16) for w in (w_in_t, w_out[0], w_up[0], w_down[0]))
    probs = jax.nn.softmax(rnn_lb_logits.astype(F32), axis=0)
    lb = probs[0:1]

    (h1,), (wint_part,) = pre_norm(xs, mix_pre_gain, carry=_gather_first(s_in, diagonal=False))
    in_rows = s_in.shape[0]
    wint = comm_only("gather_rest_w_in", _pass_slabs(
        wint_part,
        [(_X, _Y, (0, in_rows // 2)), (_Y, _X, (in_rows // 2, in_rows)), (_X, _C, None), (_Y, _C, None)],
        then=[(_XY, _C, None)]))[0].reshape(-1, d)
    up_rows = s_up.shape[0]
    up_cut = up_rows * 9 // 16
    proj, (wup_part,) = mm_nt("in_proj", h1, wint, F32, carry=_gather_first(s_up, rows=(0, up_cut)))
    (attn_o, attn_n), (wup_half, wout_half) = attn_fwd(
        proj, attn_sinks, attn_out_gain, aw,
        carry=_merge(_gather_first(s_up, rows=(up_cut, up_rows), into=wup_part), _gather_first(s_out)))
    (cat, o_r, att, st), (wup, wout, wdown_half) = hgrn_fwd(
        proj, attn_n, lb, rnn_norm_gain, col0, rw,
        carry=_merge(_gather_second(wup_half), _gather_second(wout_half), _gather_first(s_down)))
    wout = wout.reshape(-1, d)
    mixed, (wdown,) = mm_nn("out_proj", cat, wout, F32, carry=_gather_second(wdown_half))
    wdown = wdown.reshape(-1, d)
    x1, h2 = mid_fwd(mixed, mix_post_gain, xs, mlp_pre_gain)
    u = up_proj(h2, wup)
    y = down_proj(u, wdown)
    sse, dout, dy, dg_mlppost = loss_bwd(y, mlp_post_gain, x1, target)

    du = down_bwd_act(dy, wdown, u)
    rs_down = _Scatter("down", down_wgrad(u, dy).reshape(N_DEV, -1, d), sels, both_links=True)
    dh2, (got,) = up_bwd_x(du, wup, carry=rs_down.step())
    rs_down.land(got)
    dwup, gots = up_wgrad(h2, du, carry=rs_down.step())
    rs_down.land(gots)
    rs_up = _Scatter("up", dwup, sels)
    (dx1, dmixed, dg_mlppre, dg_mixpost), (got,) = mid_bwd(dh2, x1, mlp_pre_gain, dout, mixed, mix_post_gain,
                                                          carry=rs_up.step())
    rs_up.land(got)
    dcat = mm_nt("out_bwd_x", dmixed, wout, F32)
    rs_out = _Scatter("out", mm_tn("out_wgrad", cat, dmixed, BF16).reshape(N_DEV, -1, d), sels)
    (dq_a, dk_a, dv_a, dsinks, daog), (*gots, got_o) = attn_bwd(
        proj, attn_sinks, attn_out_gain, attn_o, dcat, aw, carry=_merge(rs_down.step(), rs_out.step()))
    out_down = rs_down.land(gots, w_down[0], m_w_down[0], v_w_down[0])
    rs_out.land(got_o)
    (dproj, dlb, dng), (got_u, got_o) = hgrn_bwd(
        proj, lb, rnn_norm_gain, o_r, att, st, dcat, dq_a, dk_a, dv_a, col0, rw,
        carry=_merge(rs_up.step(), rs_out.step()))
    rs_up.land(got_u)
    rs_out.land(got_o)
    dwin, (got_u, got_o) = mm_tn("in_wgrad", dproj, h1, BF16, carry=_merge(rs_up.step(), rs_out.step()))
    out_up = rs_up.land(got_u, w_up[0], m_w_up[0], v_w_up[0])
    out_out = rs_out.land(got_o, w_out[0], m_w_out[0], v_w_out[0])
    rs_in = _Scatter("in", dwin.reshape(N_DEV, -1, d), sels, both_links=True)
    rs_in.land(comm_only("rs_exchange_c_in", rs_in.step())[0])
    dh1, gots = mm_nn("in_bwd_x", dproj, wint, F32, tm=MM_TILE // 2, carry=rs_in.step())
    rs_in.land(gots)
    grad_x, dg_mixpre = first_bwd(dh1, xs, mix_pre_gain, dx1)
    out_in = rs_in.land(comm_only("rs_exchange_last_in", rs_in.step()), w_in_t, m_in_t, v_in_t)
    big_out = [out_in, out_out, out_up, out_down]

    n_heads = attn_sinks.shape[1]
    jac = probs[0] * probs[1]
    partial = _pack([sse, dsinks[0, :n_heads], daog, jnp.stack([dlb[0], dlb[0]]), jnp.sum(dng, axis=0),
                     dg_mixpre, dg_mixpost, dg_mlppre, dg_mlppost])
    ones = [jnp.ones(small_w[k].shape, F32) for k in _SMALL]
    ones[2] = jnp.stack([jac, -jac])
    scale = _pack([jnp.full((1,), 0.5 / d, F32)] + ones)
    zero = jnp.zeros((1,), F32)
    outs = small_allreduce_adamw(partial, scale, _pack([zero] + [small_w[k] for k in _SMALL]),
                                 _pack([zero] + [small_m[k] for k in _SMALL]),
                                 _pack([jnp.ones((1,), F32)] + [small_v[k] for k in _SMALL]))
    shapes = [(1,)] + [small_w[k].shape for k in _SMALL]
    sgrad, sdelta, snm, snv = (_unpack(o, shapes) for o in outs)
    loss = sgrad[0][0]

    def big(i, j):
        o = big_out[i][j]
        return (o.T if i == 0 else o)[None]

    def ordered(j, smalls):
        s = dict(zip(_SMALL, smalls[1:]))
        return [big(0, j), s["attn_sinks"], s["attn_out_gain"], s["rnn_lb_logits"], s["rnn_norm_gain"], big(1, j),
                s["mix_pre_gain"], s["mix_post_gain"], s["mlp_pre_gain"], s["mlp_post_gain"], big(2, j), big(3, j)]

    return (loss, grad_x[None], *ordered(0, sgrad), *ordered(1, sdelta), *ordered(2, snm), *ordered(3, snv))
```

```python
import math

import jax
import jax.numpy as jnp
from jax import lax
from jax.experimental import pallas as pl
from jax.experimental.pallas import tpu as pltpu

F32 = jnp.float32
BF16 = jnp.bfloat16

HEAD_DIM = 64
N_KV_HEADS = 2
BLOCK = 128
RNN_HEAD_DIM = 128
CHUNK = 64
SUB_FWD = 16
SUB_BWD = 16
EPS = 1e-6

ADAM_LR = 0.001
ADAM_B1 = 0.9
ADAM_B2 = 0.999
ADAM_EPS = 1e-08
ADAM_WD = 0.01
ADAM_STEP = 10

N_DEV = 8
LANES = 128
V7X_VMEM_LIMIT = 56 * 1024 * 1024
MESH = pl.DeviceIdType.MESH
HI = lax.Precision.HIGHEST
ANY = pl.BlockSpec(memory_space=pl.ANY)
_AXES = ("x", "y", "c")


def _cparams(sem=None, **kw):
    return pltpu.CompilerParams(dimension_semantics=sem, vmem_limit_bytes=V7X_VMEM_LIMIT, **kw)


def _dot(a, b, dims):
    return lax.dot_general(a.astype(BF16), b.astype(BF16), (dims, ((), ())), preferred_element_type=F32)


NN = ((1,), (0,))
NT = ((1,), (1,))
TN = ((0,), (0,))


def _pick(n, pref):
    t = min(n, pref)
    while n % t:
        t //= 2
    return t


def _tile(n, pref, mult=LANES):
    if n <= pref:
        return n
    t = pref - pref % mult
    while n % t:
        t -= mult
    return t


def _coords():
    return lax.axis_index("x"), lax.axis_index("y"), lax.axis_index("c")


def _slab_index(dev):
    return 4 * dev[0] + 2 * dev[1] + dev[2]


class _Part:
    def __init__(self, operands, landings, aliases, n_sems, plan):
        self.operands, self.landings, self.aliases, self.n_sems, self.plan = operands, landings, aliases, n_sems, plan


def _merge(*parts):
    operands, landings, aliases, plans = [], [], {}, []
    s0 = 0
    for p in parts:
        o0, l0 = len(operands), len(landings)
        aliases.update({o0 + i: l0 + j for i, j in p.aliases.items()})
        plans.append((p.plan, o0, len(p.operands), l0, len(p.landings), s0))
        operands += p.operands
        landings += p.landings
        s0 += p.n_sems

    def plan(ops, lands, sem):
        starts, waits = [], []
        for f, o0, no, l0, nl, off in plans:
            s, w = f(ops[o0:o0 + no], lands[l0:l0 + nl], lambda kind, k, off=off: sem(kind, off + k))
            starts += s
            waits += w
        return starts, waits

    return _Part(operands, landings, aliases, s0, plan)


def _gather_peers(x, y, c):
    return [(x, y, 1 - c), (1 - x, y, c), (x, 1 - y, c), (1 - x, 1 - y, c)]


def _gather_first(shard, rows=None, into=None, diagonal=True):
    lo, hi = (0, shard.shape[0]) if rows is None else rows
    n_peers = 4 if diagonal else 3

    def plan(ops, lands, sem):
        x, y, c = _coords()
        me, peers = (x, y, c), _gather_peers(x, y, c)[:n_peers]
        src = ops[0].at[pl.ds(lo, hi - lo)]

        def slab(block):
            return lands[0].at[_slab_index(block), pl.ds(lo, hi - lo)]

        def cp(k, block, to):
            return pltpu.make_async_remote_copy(
                src_ref=src, dst_ref=slab(block),
                send_sem=sem(0, k), recv_sem=sem(1, k), device_id=to, device_id_type=MESH)

        local = pltpu.make_async_copy(src, slab(me), sem(2, 0))
        sends = [cp(k, me, to) for k, to in enumerate(peers)]
        recvs = [cp(k, frm, me) for k, frm in enumerate(peers)]
        return ([local.start] + [s.start for s in sends],
                [local.wait] + [s.wait_send for s in sends] + [r.wait_recv for r in recvs])

    landing = jax.ShapeDtypeStruct((N_DEV, *shard.shape), shard.dtype)
    if into is None:
        return _Part([shard], [landing], {}, 4, plan)
    return _Part([shard, into], [landing], {1: 0}, 4, plan)


def _flip(dev, flips):
    return tuple(1 - v if f else v for v, f in zip(dev, flips))


def _pass_slabs(gathered, moves, then=()):
    def wave(lands, sem, k0, wave_moves):
        me = _coords()
        sends, recvs = [], []
        for k, (block, dest, rows) in enumerate(wave_moves, start=k0):
            lo, hi = (0, gathered.shape[1]) if rows is None else rows

            def cp(blk, to, k=k, lo=lo, hi=hi):
                slab = lands[0].at[_slab_index(blk), pl.ds(lo, hi - lo)]
                return pltpu.make_async_remote_copy(
                    src_ref=slab, dst_ref=slab, send_sem=sem(0, k), recv_sem=sem(1, k),
                    device_id=to, device_id_type=MESH)

            sends.append(cp(_flip(me, block), _flip(me, dest)))
            recvs.append(cp(_flip(_flip(me, dest), block), me))
        return [s.start for s in sends], [s.wait_send for s in sends] + [r.wait_recv for r in recvs]

    def plan(ops, lands, sem):
        starts, waits = wave(lands, sem, 0, moves)
        if then:
            starts2, waits2 = wave(lands, sem, len(moves), then)
            waits = waits + starts2 + waits2
        return starts, waits

    return _Part([gathered], [jax.ShapeDtypeStruct(gathered.shape, gathered.dtype)], {0: 0},
                 len(moves) + len(then), plan)


_X, _Y, _C, _XY = (1, 0, 0), (0, 1, 0), (0, 0, 1), (1, 1, 0)


def _gather_second(gathered):
    def plan(ops, lands, sem):
        x, y, c = _coords()
        sibling = (x, y, 1 - c)
        chips = [(1 - x, y), (x, 1 - y), (1 - x, 1 - y)]

        def cp(k, block):
            slab = lands[0].at[_slab_index(block)]
            return pltpu.make_async_remote_copy(
                src_ref=slab, dst_ref=slab, send_sem=sem(0, k), recv_sem=sem(1, k),
                device_id=sibling, device_id_type=MESH)

        sends = [cp(k, (*chip, c)) for k, chip in enumerate(chips)]
        recvs = [cp(k, (*chip, 1 - c)) for k, chip in enumerate(chips)]
        return [s.start for s in sends], [s.wait_send for s in sends] + [r.wait_recv for r in recvs]

    return _Part([gathered], [jax.ShapeDtypeStruct(gathered.shape, gathered.dtype)], {0: 0}, 3, plan)


def _scatter_step(array, axis, minor=None, rows=None):
    minor = (axis == "c") if minor is None else minor
    pieces = array.shape[0] if minor else array.shape[1]
    lo, hi = (0, array.shape[2]) if rows is None else rows

    def plan(ops, lands, sem):
        coords = list(_coords())
        ai = _AXES.index(axis)
        mine = coords[ai]
        peer = list(coords)
        peer[ai] = 1 - mine
        cps = []
        for p in range(pieces):
            src = ops[0].at[p, 1 - mine, pl.ds(lo, hi - lo)] if minor else ops[0].at[1 - mine, p, pl.ds(lo, hi - lo)]
            cps.append(pltpu.make_async_remote_copy(
                src_ref=src, dst_ref=lands[0].at[p], send_sem=sem(0, p), recv_sem=sem(1, p),
                device_id=tuple(peer), device_id_type=MESH))
        return [cp.start for cp in cps], [cp.wait for cp in cps]

    return _Part([array], [jax.ShapeDtypeStruct((pieces, hi - lo, array.shape[3]), array.dtype)], {}, pieces, plan)


def _grid_edges(grid):
    first = last = None
    for ax, n in enumerate(grid):
        p = pl.program_id(ax)
        f, l = p == 0, p == n - 1
        first = f if first is None else jnp.logical_and(first, f)
        last = l if last is None else jnp.logical_and(last, l)
    return first, last


def _call(body, *, name, grid, in_specs, out_specs, out_shape, args, scratch_shapes=(), sem=None, carry=None):
    if carry is None:
        return pl.pallas_call(
            body, name=name, grid=grid, in_specs=list(in_specs), out_specs=list(out_specs),
            out_shape=list(out_shape), scratch_shapes=list(scratch_shapes), compiler_params=_cparams(sem),
        )(*args)
    n_in, n_out, n_scr = len(in_specs), len(out_specs), len(scratch_shapes)
    n_cin, n_cout = len(carry.operands), len(carry.landings)

    def wrapped(*refs):
        ins, cins = refs[:n_in], refs[n_in:n_in + n_cin]
        o0 = n_in + n_cin
        outs, couts = refs[o0:o0 + n_out], refs[o0 + n_out:o0 + n_out + n_cout]
        s0 = o0 + n_out + n_cout
        scr, sems = refs[s0:s0 + n_scr], refs[s0 + n_scr:]
        first, last = _grid_edges(grid)

        def plan():
            return carry.plan(cins, couts, lambda kind, k: sems[kind].at[k])

        def start_all():
            for start in plan()[0]:
                start()

        def wait_all():
            for wait in plan()[1]:
                wait()

        if grid:
            pl.when(first)(start_all)
            body(*ins, *outs, *scr)
            pl.when(last)(wait_all)
        else:
            start_all()
            body(*ins, *outs, *scr)
            wait_all()

    sem_arrays = [pltpu.SemaphoreType.DMA((carry.n_sems,))] * 3
    res = pl.pallas_call(
        wrapped, name=name, grid=grid,
        in_specs=[*in_specs, *[ANY] * n_cin], out_specs=[*out_specs, *[ANY] * n_cout],
        out_shape=[*out_shape, *carry.landings],
        scratch_shapes=[*scratch_shapes, *sem_arrays],
        input_output_aliases={n_in + i: n_out + j for i, j in carry.aliases.items()},
        compiler_params=_cparams(("arbitrary",) * len(grid) if grid else None, has_side_effects=True),
    )(*args, *carry.operands)
    return res[:n_out], res[n_out:]


MM_TILE = 1024
MM_K_TILE = 2048
MXU_COLS = 256
MM_VMEM_BUDGET = 50 * 1024 * 1024


def _matmul(name, a, b, dims, grid, a_spec, b_spec, out_shape, out_spec, epilogue,
            extras=(), extra_specs=(), prologue=None, carry=None):
    nk = grid[2]
    n_extra = len(extras)
    acc_shape = out_spec.block_shape[-2:]

    def lhs(a_ref):
        return a_ref[...] if prologue is None else prologue(a_ref[...])

    def body_one(a_ref, b_ref, *rest):
        epilogue(_dot(lhs(a_ref), b_ref[...], dims), rest[:n_extra], rest[n_extra:])

    def body_acc(a_ref, b_ref, *rest):
        acc = rest[-1]
        k = pl.program_id(2)
        part = _dot(lhs(a_ref), b_ref[...], dims)

        @pl.when(k == 0)
        def _():
            acc[...] = part

        @pl.when(k > 0)
        def _():
            acc[...] += part

        @pl.when(k == nk - 1)
        def _():
            epilogue(acc[...], rest[:n_extra], rest[n_extra:-1])

    res = _call(body_one if nk == 1 else body_acc, name=name, grid=grid,
                in_specs=[a_spec, b_spec, *extra_specs], out_specs=[out_spec], out_shape=[out_shape],
                args=(a, b, *extras), scratch_shapes=[] if nk == 1 else [pltpu.VMEM(acc_shape, F32)],
                sem=("parallel", "parallel", "arbitrary"), carry=carry)
    return res[0] if carry is None else (res[0][0], res[1])


def _store_as(acc, extra_refs, out_refs):
    out_refs[0][...] = acc.astype(out_refs[0].dtype)


def _square(u):
    return u * u


def mm_nn(name, a, b, out_dtype, tk=None, tm=MM_TILE, tn=MM_TILE, prologue=None, carry=None):
    (m, kk), n = a.shape, b.shape[1]
    tm, tn = _tile(m, tm), _tile(n, tn, mult=MXU_COLS)
    tk = kk if tk is None else _tile(kk, tk, mult=MXU_COLS)
    return _matmul(name, a, b, NN, (m // tm, n // tn, kk // tk),
                   pl.BlockSpec((tm, tk), lambda i, j, k: (i, k)),
                   pl.BlockSpec((tk, tn), lambda i, j, k: (k, j)),
                   jax.ShapeDtypeStruct((m, n), out_dtype),
                   pl.BlockSpec((tm, tn), lambda i, j, k: (i, j)), _store_as, prologue=prologue, carry=carry)


def mm_nt(name, a, b, out_dtype, epilogue=_store_as, extras=(), extra_specs=(), carry=None):
    (m, kk), n = a.shape, b.shape[0]
    tm, tn = _tile(m, MM_TILE), _tile(n, MM_TILE, mult=MXU_COLS)
    return _matmul(name, a, b, NT, (m // tm, n // tn, 1),
                   pl.BlockSpec((tm, kk), lambda i, j, k: (i, 0)),
                   pl.BlockSpec((tn, kk), lambda i, j, k: (j, 0)),
                   jax.ShapeDtypeStruct((m, n), out_dtype),
                   pl.BlockSpec((tm, tn), lambda i, j, k: (i, j)), epilogue,
                   extras=extras, extra_specs=extra_specs, carry=carry)


def _whole_k_fits(tm, tn, kk, out_dtype, prologue):
    operands = 2 * 2 * kk * (tm + tn)
    out = 2 * tm * tn * jnp.dtype(out_dtype).itemsize + 4 * tm * tn
    return operands + out + (2 * kk * tm if prologue is not None else 0) <= MM_VMEM_BUDGET


def mm_tn(name, a, b, out_dtype, prologue=None, carry=None):
    (kk, m), n = a.shape, b.shape[1]
    tm, tn = _tile(m, MM_TILE), _tile(n, MM_TILE)
    tk = kk if _whole_k_fits(tm, tn, kk, out_dtype, prologue) else _tile(kk, MM_K_TILE)
    return _matmul(name, a, b, TN, (m // tm, n // tn, kk // tk),
                   pl.BlockSpec((tk, tm), lambda i, j, k: (k, i)),
                   pl.BlockSpec((tk, tn), lambda i, j, k: (k, j)),
                   jax.ShapeDtypeStruct((m, n), out_dtype),
                   pl.BlockSpec((tm, tn), lambda i, j, k: (i, j)), _store_as, prologue=prologue, carry=carry)


def up_proj(h2, wup_slabs):
    (m, kk), (_, _, ns) = h2.shape, wup_slabs.shape
    tm, tn = _tile(m, MM_TILE), _tile(ns, MM_TILE)
    r = ns // tn
    n = N_DEV * ns

    def epi(acc, extra_refs, out_refs):
        out_refs[0][...] = jnp.maximum(acc, 0.0).astype(BF16)

    return _matmul("up_proj", h2, wup_slabs, NN, (m // tm, n // tn, 1),
                   pl.BlockSpec((tm, kk), lambda i, j, k: (i, 0)),
                   pl.BlockSpec((None, kk, tn), lambda i, j, k: (j // r, 0, j % r)),
                   jax.ShapeDtypeStruct((m, n), BF16),
                   pl.BlockSpec((tm, tn), lambda i, j, k: (i, j)), epi)


def down_proj(u, wdown):
    return mm_nn("down_proj", u, wdown, F32, tm=MM_TILE // 2, tn=MM_TILE // 2, prologue=_square)


def down_bwd_act(dy, wdown, u):
    tm, tn = _tile(dy.shape[0], MM_TILE), _tile(wdown.shape[0], MM_TILE)

    def epi(acc, extra_refs, out_refs):
        out_refs[0][...] = (acc * (2.0 * extra_refs[0][...].astype(F32))).astype(BF16)

    return mm_nt("down_bwd_act", dy, wdown, BF16, epilogue=epi, extras=(u,),
                 extra_specs=(pl.BlockSpec((tm, tn), lambda i, j, k: (i, j)),))


def down_wgrad(u, dy):
    return mm_tn("down_wgrad", u, dy, BF16, prologue=_square)


def up_bwd_x(du, wup_slabs, carry=None):
    (m, kk), (slabs, n, ns) = du.shape, wup_slabs.shape
    tm, tn = _tile(m, MM_TILE // 2), _tile(n, MM_TILE // 2, mult=MXU_COLS)

    def body(a_ref, b_ref, o_ref):
        acc = _dot(a_ref[:, :ns], b_ref[0], NT)
        for s in range(1, slabs):
            acc = acc + _dot(a_ref[:, s * ns:(s + 1) * ns], b_ref[s], NT)
        o_ref[...] = acc

    res = _call(body, name="up_bwd_x", grid=(m // tm, n // tn),
                in_specs=[pl.BlockSpec((tm, kk), lambda i, j: (i, 0)),
                          pl.BlockSpec((slabs, tn, ns), lambda i, j: (0, j, 0))],
                out_specs=[pl.BlockSpec((tm, tn), lambda i, j: (i, j))],
                out_shape=[jax.ShapeDtypeStruct((m, n), F32)], args=(du, wup_slabs),
                sem=("parallel", "parallel"), carry=carry)
    return res[0] if carry is None else (res[0][0], res[1])


def up_wgrad(h2, du, carry=None):
    (kk, m), n = h2.shape, du.shape[1]
    ns = n // N_DEV
    tm, tn = _tile(m, MM_TILE), _tile(ns, MM_TILE)
    tk = kk if _whole_k_fits(tm, tn, kk, BF16, None) else _tile(kk, MM_K_TILE)
    r = ns // tn
    return _matmul("up_wgrad", h2, du, TN, (m // tm, n // tn, kk // tk),
                   pl.BlockSpec((tk, tm), lambda i, j, k: (k, i)),
                   pl.BlockSpec((tk, tn), lambda i, j, k: (k, j)),
                   jax.ShapeDtypeStruct((N_DEV, m, ns), BF16),
                   pl.BlockSpec((None, tm, tn), lambda i, j, k: (j // r, i, j % r)), _store_as, carry=carry)


def _rstd(x):
    return lax.rsqrt(jnp.mean(x * x, axis=-1, keepdims=True) + EPS)


def _norm_bwd(x, g, dy):
    r = _rstd(x)
    xh = x * r
    dyg = dy * g
    dx = r * (dyg - xh * jnp.mean(dyg * xh, axis=-1, keepdims=True))
    return dx, jnp.sum(dy * xh, axis=0, keepdims=True)


def _row_spec(tr, d):
    return pl.BlockSpec((tr, d), lambda i: (i, 0))


def _vec_spec(d):
    return pl.BlockSpec((1, d), lambda i: (0, 0))


def _accum(ref, val):
    @pl.when(pl.program_id(0) == 0)
    def _():
        ref[...] = jnp.zeros_like(ref)

    ref[...] += val


def pre_norm(x, g, carry=None, tr=256):
    t, d = x.shape
    tr = _pick(t, tr)

    def body(x_ref, g_ref, h_ref):
        xx = x_ref[...]
        h_ref[...] = (xx * _rstd(xx) * g_ref[...]).astype(BF16)

    return _call(body, name="pre_norm", grid=(t // tr,),
                 in_specs=[_row_spec(tr, d), _vec_spec(d)], out_specs=[_row_spec(tr, d)],
                 out_shape=[jax.ShapeDtypeStruct((t, d), BF16)], args=(x, g), sem=("parallel",), carry=carry)


def mid_fwd(mixed, g_post, x, g_pre2, tr=256):
    t, d = x.shape
    tr = _pick(t, tr)

    def body(m_ref, gp_ref, x_ref, g2_ref, x1_ref, h2_ref):
        mm = m_ref[...]
        x1 = x_ref[...] + mm * _rstd(mm) * gp_ref[...]
        x1_ref[...] = x1
        h2_ref[...] = (x1 * _rstd(x1) * g2_ref[...]).astype(BF16)

    return _call(body, name="mid_fwd", grid=(t // tr,),
                 in_specs=[_row_spec(tr, d), _vec_spec(d), _row_spec(tr, d), _vec_spec(d)],
                 out_specs=[_row_spec(tr, d), _row_spec(tr, d)],
                 out_shape=[jax.ShapeDtypeStruct((t, d), F32), jax.ShapeDtypeStruct((t, d), BF16)],
                 args=(mixed, g_post, x, g_pre2), sem=("parallel",))


def loss_bwd(y, g_post2, x1, target, tr=256):
    t, d = y.shape
    tr = _pick(t, tr)

    def body(y_ref, g_ref, x1_ref, t_ref, sse_ref, dout_ref, dy_ref, dg_ref):
        yy = y_ref[...]
        g = g_ref[...]
        err = x1_ref[...] + yy * _rstd(yy) * g - t_ref[...]
        _accum(sse_ref, jnp.sum(jnp.sum(err * err, axis=1, keepdims=True), axis=0, keepdims=True))
        dout = err * (1.0 / d)
        dout_ref[...] = dout
        dy, dg = _norm_bwd(yy, g, dout)
        dy_ref[...] = dy.astype(BF16)
        _accum(dg_ref, dg)

    return _call(body, name="loss_bwd", grid=(t // tr,),
                 in_specs=[_row_spec(tr, d), _vec_spec(d), _row_spec(tr, d), _row_spec(tr, d)],
                 out_specs=[pl.BlockSpec((1, 1), lambda i: (0, 0)), _row_spec(tr, d), _row_spec(tr, d), _vec_spec(d)],
                 out_shape=[jax.ShapeDtypeStruct((1, 1), F32), jax.ShapeDtypeStruct((t, d), F32),
                            jax.ShapeDtypeStruct((t, d), BF16), jax.ShapeDtypeStruct((1, d), F32)],
                 args=(y, g_post2, x1, target), sem=("arbitrary",))


def mid_bwd(dh2, x1, g_pre2, dout, mixed, g_post, carry=None, tr=256):
    t, d = x1.shape
    tr = _pick(t, tr)

    def body(dh_ref, x1_ref, g2_ref, do_ref, m_ref, gp_ref, dx1_ref, dm_ref, dg2_ref, dgp_ref):
        d1, dg2 = _norm_bwd(x1_ref[...], g2_ref[...], dh_ref[...])
        dx1 = do_ref[...] + d1
        dx1_ref[...] = dx1
        dm, dgp = _norm_bwd(m_ref[...], gp_ref[...], dx1)
        dm_ref[...] = dm.astype(BF16)
        _accum(dg2_ref, dg2)
        _accum(dgp_ref, dgp)

    return _call(body, name="mid_bwd", grid=(t // tr,),
                 in_specs=[_row_spec(tr, d), _row_spec(tr, d), _vec_spec(d), _row_spec(tr, d), _row_spec(tr, d),
                           _vec_spec(d)],
                 out_specs=[_row_spec(tr, d), _row_spec(tr, d), _vec_spec(d), _vec_spec(d)],
                 out_shape=[jax.ShapeDtypeStruct((t, d), F32), jax.ShapeDtypeStruct((t, d), BF16),
                            jax.ShapeDtypeStruct((1, d), F32), jax.ShapeDtypeStruct((1, d), F32)],
                 args=(dh2, x1, g_pre2, dout, mixed, g_post), sem=("arbitrary",), carry=carry)


def first_bwd(dh1, x, g_pre, dx1, carry=None, tr=256):
    t, d = x.shape
    tr = _pick(t, tr)

    def body(dh_ref, x_ref, g_ref, dx1_ref, gx_ref, dg_ref):
        d0, dg = _norm_bwd(x_ref[...], g_ref[...], dh_ref[...])
        gx_ref[...] = dx1_ref[...] + d0
        _accum(dg_ref, dg)

    return _call(body, name="first_bwd", grid=(t // tr,),
                 in_specs=[_row_spec(tr, d), _row_spec(tr, d), _vec_spec(d), _row_spec(tr, d)],
                 out_specs=[_row_spec(tr, d), _vec_spec(d)],
                 out_shape=[jax.ShapeDtypeStruct((t, d), F32), jax.ShapeDtypeStruct((1, d), F32)],
                 args=(dh1, x, g_pre, dx1), sem=("arbitrary",), carry=carry)


def _attn_geometry(has_prev):
    r = lax.broadcasted_iota(jnp.int32, (BLOCK, 2 * BLOCK), 0)
    c = lax.broadcasted_iota(jnp.int32, (BLOCK, 2 * BLOCK), 1)
    dist = r + BLOCK - c
    valid = jnp.logical_and(jnp.logical_and(dist >= 0, dist < BLOCK), jnp.logical_or(c >= BLOCK, has_prev))
    return dist.astype(F32), valid


def _stack_pairs(x, g, pairs):
    base = g * pairs * LANES
    return jnp.concatenate([x[:, base + p * LANES:base + (p + 1) * LANES] for p in range(pairs)], axis=0)


def _unstack_pairs(xs, pairs):
    return jnp.concatenate([xs[p * BLOCK:(p + 1) * BLOCK, :] for p in range(pairs)], axis=1)


def _to_half(x, g, odd):
    lane = lax.broadcasted_iota(jnp.int32, x.shape, 1)
    y = x if (g == 1) == odd else pltpu.roll(x, HEAD_DIM, axis=1)
    return jnp.where((lane >= HEAD_DIM) == odd, y, 0.0)


def _from_halves(even, odd, g):
    lane = lax.broadcasted_iota(jnp.int32, even.shape, 1)
    if g == 0:
        return jnp.where(lane < HEAD_DIM, even + pltpu.roll(odd, HEAD_DIM, axis=1), 0.0)
    return jnp.where(lane >= HEAD_DIM, pltpu.roll(even, HEAD_DIM, axis=1) + odd, 0.0)


_PARITIES = [(g, odd) for g in range(N_KV_HEADS) for odd in (False, True)]


def _softmax_sink(s, sink_ref, g, odd, group, n_heads, geo):
    dist, valid = geo
    pairs = group // 2
    heads = [g * group + 2 * p + int(odd) for p in range(pairs)]
    bias = jnp.concatenate([(2.0 ** (-8.0 * (h + 1) / n_heads)) * dist for h in heads], axis=0)
    sink = jnp.concatenate([jnp.full((BLOCK, 1), sink_ref[0, h], F32) for h in heads], axis=0)
    s = jnp.where(jnp.concatenate([valid] * pairs, axis=0), s - bias, -jnp.inf)
    m = jnp.maximum(jnp.max(s, axis=-1, keepdims=True), sink)
    p = jnp.exp(s - m)
    p_sink = jnp.exp(sink - m)
    inv = 1.0 / (jnp.sum(p, axis=-1, keepdims=True) + p_sink)
    return p * inv, p_sink * inv


def attn_fwd(proj, sinks, gain, aw, carry=None):
    t = proj.shape[0]
    kw = N_KV_HEADS * HEAD_DIM
    n_heads = aw // HEAD_DIM
    group = n_heads // N_KV_HEADS
    pairs = group // 2
    assert kw == LANES and group % 2 == 0
    nb = t // BLOCK
    scale = HEAD_DIM ** -0.5

    def body(sink_ref, q_ref, k_ref, v_ref, g_ref, o_ref, on_ref):
        n = pl.program_id(0)
        cur = pl.multiple_of(n * BLOCK, BLOCK)
        prev = pl.multiple_of(jnp.maximum(n - 1, 0) * BLOCK, BLOCK)
        geo = _attn_geometry(n > 0)
        kcat = jnp.concatenate([k_ref[pl.ds(prev, BLOCK), :], k_ref[pl.ds(cur, BLOCK), :]], axis=0)
        vcat = jnp.concatenate([v_ref[pl.ds(prev, BLOCK), :], v_ref[pl.ds(cur, BLOCK), :]], axis=0)
        q = q_ref[...] * scale
        qs = [_stack_pairs(q, g, pairs) for g in range(N_KV_HEADS)]
        scores = [_dot(qs[g], _to_half(kcat, g, odd), NT) for g, odd in _PARITIES]
        probs = [_softmax_sink(s, sink_ref, g, odd, group, n_heads, geo)[0] for s, (g, odd) in zip(scores, _PARITIES)]
        outs = [_dot(p, _to_half(vcat, g, odd), NN) for p, (g, odd) in zip(probs, _PARITIES)]
        o = jnp.concatenate([_unstack_pairs(outs[2 * g] + outs[2 * g + 1], pairs) for g in range(N_KV_HEADS)], axis=1)
        o_ref[...] = o
        on_ref[...] = (o * _rstd(o) * g_ref[...]).astype(BF16)

    return _call(body, name="attn_fwd", grid=(nb,),
                 in_specs=[pl.BlockSpec(memory_space=pltpu.SMEM),
                           pl.BlockSpec((BLOCK, aw), lambda n: (n, 0)),
                           pl.BlockSpec((t, kw), lambda n: (0, aw // kw)),
                           pl.BlockSpec((t, kw), lambda n: (0, aw // kw + 1)),
                           pl.BlockSpec((1, aw), lambda n: (0, 0))],
                 out_specs=[pl.BlockSpec((BLOCK, aw), lambda n: (n, 0)), pl.BlockSpec((BLOCK, aw), lambda n: (n, 0))],
                 out_shape=[jax.ShapeDtypeStruct((t, aw), F32), jax.ShapeDtypeStruct((t, aw), BF16)],
                 args=(sinks, proj, proj, proj, gain), sem=("parallel",), carry=carry)


def attn_bwd(proj, sinks, gain, attn_o, dcat, aw, carry=None):
    t = proj.shape[0]
    kw = N_KV_HEADS * HEAD_DIM
    n_heads = aw // HEAD_DIM
    group = n_heads // N_KV_HEADS
    pairs = group // 2
    assert kw == LANES and group % 2 == 0
    nb = t // BLOCK
    scale = HEAD_DIM ** -0.5

    def body(sink_ref, q_ref, k_ref, v_ref, g_ref, o_ref, dn_ref, dq_ref, dk_ref, dv_ref, dsink_ref, dg_ref):
        n = pl.program_id(0)
        cur = pl.multiple_of(n * BLOCK, BLOCK)
        prev = pl.multiple_of(jnp.maximum(n - 1, 0) * BLOCK, BLOCK)
        geo = _attn_geometry(n > 0)

        @pl.when(n == 0)
        def _():
            dk_ref[...] = jnp.zeros_like(dk_ref)
            dv_ref[...] = jnp.zeros_like(dv_ref)
            dsink_ref[...] = jnp.zeros_like(dsink_ref)

        o = o_ref[...]
        do_all, dg = _norm_bwd(o, g_ref[...], dn_ref[...])
        _accum(dg_ref, dg)
        kcat = jnp.concatenate([k_ref[pl.ds(prev, BLOCK), :], k_ref[pl.ds(cur, BLOCK), :]], axis=0)
        vcat = jnp.concatenate([v_ref[pl.ds(prev, BLOCK), :], v_ref[pl.ds(cur, BLOCK), :]], axis=0)
        q = q_ref[...] * scale
        lane = lax.broadcasted_iota(jnp.int32, (1, LANES), 1)
        lane_s = lax.broadcasted_iota(jnp.int32, (pairs * BLOCK, LANES), 1)
        qs = [_stack_pairs(q, g, pairs) for g in range(N_KV_HEADS)]
        dos = [_stack_pairs(do_all, g, pairs) for g in range(N_KV_HEADS)]
        kxs = [_to_half(kcat, g, odd) for g, odd in _PARITIES]
        scores = [_dot(qs[g], kx, NT) for kx, (g, odd) in zip(kxs, _PARITIES)]
        dps = [_dot(dos[g], _to_half(vcat, g, odd), NT) for g, odd in _PARITIES]
        deltas = []
        for g in range(N_KV_HEADS):
            prod = dos[g] * _stack_pairs(o, g, pairs)
            delta_even = jnp.sum(jnp.where(lane_s < HEAD_DIM, prod, 0.0), axis=-1, keepdims=True)
            deltas += [delta_even, jnp.sum(prod, axis=-1, keepdims=True) - delta_even]
        dsink = jnp.zeros((1, LANES), F32)
        ps, dss = [], []
        for i, (g, odd) in enumerate(_PARITIES):
            p, p_sink = _softmax_sink(scores[i], sink_ref, g, odd, group, n_heads, geo)
            ps.append(p)
            dss.append(p * (dps[i] - deltas[i]))
            sink_rows = p_sink * deltas[i]
            for pr in range(pairs):
                h = g * group + 2 * pr + int(odd)
                dsink = dsink + jnp.where(
                    lane == h, -jnp.sum(sink_rows[pr * BLOCK:(pr + 1) * BLOCK], axis=0, keepdims=True), 0.0)
        dq_pairs = [_dot(ds, kx, NN) for ds, kx in zip(dss, kxs)]
        dk_halves = [_dot(ds, qs[g], TN) for ds, (g, odd) in zip(dss, _PARITIES)]
        dv_halves = [_dot(p, dos[g], TN) for p, (g, odd) in zip(ps, _PARITIES)]
        dq_ref[...] = jnp.concatenate(
            [_unstack_pairs((dq_pairs[2 * g] + dq_pairs[2 * g + 1]) * scale, pairs) for g in range(N_KV_HEADS)],
            axis=1).astype(BF16)
        dk_upd = _from_halves(dk_halves[0], dk_halves[1], 0) + _from_halves(dk_halves[2], dk_halves[3], 1)
        dv_upd = _from_halves(dv_halves[0], dv_halves[1], 0) + _from_halves(dv_halves[2], dv_halves[3], 1)
        dk_ref[pl.ds(prev, BLOCK), :] += dk_upd[:BLOCK]
        dv_ref[pl.ds(prev, BLOCK), :] += dv_upd[:BLOCK]
        dk_ref[pl.ds(cur, BLOCK), :] += dk_upd[BLOCK:]
        dv_ref[pl.ds(cur, BLOCK), :] += dv_upd[BLOCK:]
        dsink_ref[...] += dsink

    return _call(body, name="attn_bwd", grid=(nb,),
                 in_specs=[pl.BlockSpec(memory_space=pltpu.SMEM),
                           pl.BlockSpec((BLOCK, aw), lambda n: (n, 0)),
                           pl.BlockSpec((t, kw), lambda n: (0, aw // kw)),
                           pl.BlockSpec((t, kw), lambda n: (0, aw // kw + 1)),
                           pl.BlockSpec((1, aw), lambda n: (0, 0)),
                           pl.BlockSpec((BLOCK, aw), lambda n: (n, 0)),
                           pl.BlockSpec((BLOCK, aw), lambda n: (n, 0))],
                 out_specs=[pl.BlockSpec((BLOCK, aw), lambda n: (n, 0)),
                            pl.BlockSpec((t, kw), lambda n: (0, 0)), pl.BlockSpec((t, kw), lambda n: (0, 0)),
                            pl.BlockSpec((1, LANES), lambda n: (0, 0)), pl.BlockSpec((1, aw), lambda n: (0, 0))],
                 out_shape=[jax.ShapeDtypeStruct((t, aw), BF16), jax.ShapeDtypeStruct((t, kw), F32),
                            jax.ShapeDtypeStruct((t, kw), F32), jax.ShapeDtypeStruct((1, LANES), F32),
                            jax.ShapeDtypeStruct((1, aw), F32)],
                 args=(sinks, proj, proj, proj, gain, attn_o, dcat), sem=("arbitrary",), carry=carry)


def _sigmoid(x):
    return 0.5 * jnp.tanh(0.5 * x) + 0.5


def _chunk_geometry():
    row = lax.broadcasted_iota(jnp.int32, (CHUNK, CHUNK), 0)
    col = lax.broadcasted_iota(jnp.int32, (CHUNK, CHUNK), 1)
    return row, col


def _cumsum_rows(x, reverse=False):
    row, col = _chunk_geometry()
    tri = (col >= row) if reverse else (col <= row)
    return lax.dot_general(tri.astype(F32), x, ((NN), ((), ())), precision=HI, preferred_element_type=F32)


def _rep_sub(x4, sub):
    k = x4.shape[-1]
    return jnp.broadcast_to(x4[:, None, :], (CHUNK // sub, sub, k)).reshape(CHUNK, k)


def _gates(q_r, f_r, lb):
    sg = _sigmoid(f_r)
    f = lb + (1.0 - lb) * sg
    sq = _sigmoid(q_r)
    return sg, f, sq, q_r * sq


def _offdiag_terms(b, j, sub):
    c = b[j * sub + sub - 1:j * sub + sub, :]
    return jnp.exp(jnp.minimum(b - c, 0.0)), jnp.exp(jnp.minimum(c - b, 0.0))


def _store_heads(ref, x):
    for j in range(ref.shape[0]):
        ref[j] = x[:, _head(j)]


def _sub_rows(ref, r, sub):
    rows = [ref[j, pl.ds(r, CHUNK // sub, stride=sub), :] for j in range(ref.shape[0])]
    return _rep_sub(jnp.concatenate(rows, axis=1), sub)


def _diag_mask(sub):
    row, col = _chunk_geometry()
    return jnp.logical_and((row // sub) == (col // sub), row >= col)


HGRN_HEADS_PER_STEP = 8


def _wide(refs):
    return jnp.concatenate([r[...] for r in refs], axis=1)


def _head(j):
    return slice(j * RNN_HEAD_DIM, (j + 1) * RNN_HEAD_DIM)


def _cat_heads(parts, hs):
    return jnp.concatenate([p[:, hs] for p in parts], axis=1)


def _offdiag_factors(q, k, b, sub):
    rowi = lax.broadcasted_iota(jnp.int32, b.shape, 0)
    qs, ks, ers, ecs = [], [], [], []
    for j in range(CHUNK // sub - 1):
        e_row, e_col = _offdiag_terms(b, j, sub)
        e_row = jnp.where(rowi >= (j + 1) * sub, e_row, 0.0)
        e_col = jnp.where((rowi // sub) == j, e_col, 0.0)
        qs.append(q * e_row)
        ks.append(k * e_col)
        ers.append(e_row)
        ecs.append(e_col)
    return qs, ks, ers, ecs


def hgrn_fwd(proj, attn_n, lb, norm_gain, col0, rw, carry=None):
    t, aw = attn_n.shape
    nh = rw // RNN_HEAD_DIM
    nc = t // CHUNK
    kd = RNN_HEAD_DIM
    cb = col0 // kd
    sub = SUB_FWD
    nsub = CHUNK // sub
    hp = nh
    assert nh <= HGRN_HEADS_PER_STEP
    w = hp * kd

    def body(*refs):
        q_refs, f_refs, i_refs, g_refs = (refs[i * hp:(i + 1) * hp] for i in range(4))
        lb_ref, ng_ref, an_ref, cat_ref, o_ref, att_ref, st_ref, state, b_ref, k_ref = refs[4 * hp:]
        c = pl.program_id(1)

        @pl.when(c == 0)
        def _():
            state[...] = jnp.zeros_like(state)

        st_ref[...] = state[...]
        q_r, f_r, v, g_r = (_wide(rs) for rs in (q_refs, f_refs, i_refs, g_refs))
        _, f, _, q = _gates(q_r, f_r, lb_ref[...])
        k = 1.0 - f
        b = _cumsum_rows(jnp.log(f))
        _store_heads(b_ref, b)
        _store_heads(k_ref, k)
        qcat, kcat, _, _ = _offdiag_factors(q, k, b, sub)
        row, col = _chunk_geometry()
        same = (row // sub) == (col // sub)
        rloc = lax.broadcasted_iota(jnp.int32, (CHUNK, w), 0) % sub
        diag = [jnp.zeros((CHUNK, CHUNK), F32)] * hp
        for r in range(sub):
            bs = _sub_rows(b_ref, r, sub)
            ks = _sub_rows(k_ref, r, sub)
            prod = q * jnp.exp(jnp.where(rloc >= r, b - bs, -jnp.inf)) * ks
            place = jnp.logical_and((col % sub) == r, same)
            diag = [jnp.where(place, jnp.sum(prod[:, _head(j)], axis=-1, keepdims=True), diag[j]) for j in range(hp)]
        b_last = b[CHUNK - 1:CHUNK, :]
        qe = q * jnp.exp(b)
        kdec = k * jnp.exp(b_last - b)
        decay = jnp.exp(b_last)
        outs, normed, states = [], [], []
        for j in range(hp):
            hs = _head(j)
            att = diag[j] + _dot(_cat_heads(qcat, hs), _cat_heads(kcat, hs), NT)
            att_ref[j] = att
            sj = state[j]
            o = _dot(qe[:, hs], sj, NT) + _dot(att, v[:, hs], NN)
            outs.append(o)
            normed.append(o * _rstd(o))
            states.append(sj * decay[:, hs] + _dot(v[:, hs], kdec[:, hs], TN))
        for j in range(hp):
            state[j] = states[j]
        o_ref[...] = jnp.concatenate(outs, axis=1)
        gate = g_r * _sigmoid(g_r)
        cat_ref[:, :aw] = an_ref[...]
        cat_ref[:, aw:] = (jnp.concatenate(normed, axis=1) * jnp.tile(ng_ref[...], (1, hp)) * gate).astype(BF16)

    def col(kidx, j):
        return pl.BlockSpec((CHUNK, kd), lambda hg, c: (c, cb + kidx * nh + hg * hp + j))

    return _call(body, name="hgrn_fwd", grid=(1, nc),
                 in_specs=[col(kidx, j) for kidx in range(4) for j in range(hp)] +
                          [pl.BlockSpec((1, w), lambda hg, c: (0, hg)), pl.BlockSpec((1, kd), lambda hg, c: (0, 0)),
                           pl.BlockSpec((CHUNK, aw), lambda hg, c: (c, 0))],
                 out_specs=[pl.BlockSpec((CHUNK, aw + w), lambda hg, c: (c, 0)),
                            pl.BlockSpec((CHUNK, w), lambda hg, c: (c, hg)),
                            pl.BlockSpec((hp, CHUNK, CHUNK), lambda hg, c: (hg, c, 0)),
                            pl.BlockSpec((None, hp, kd, kd), lambda hg, c: (c, hg, 0, 0))],
                 out_shape=[jax.ShapeDtypeStruct((t, aw + rw), BF16), jax.ShapeDtypeStruct((t, rw), F32),
                            jax.ShapeDtypeStruct((nh, t, CHUNK), F32), jax.ShapeDtypeStruct((nc, nh, kd, kd), F32)],
                 args=(*([proj] * (4 * hp)), lb, norm_gain, attn_n),
                 scratch_shapes=[pltpu.VMEM((hp, kd, kd), F32), pltpu.VMEM((hp, CHUNK, kd), F32),
                                 pltpu.VMEM((hp, CHUNK, kd), F32)],
                 sem=("parallel", "arbitrary"), carry=carry)


def hgrn_bwd(proj, lb, norm_gain, o_all, att_all, st_all, dcat, dq_a, dk_a, dv_a, col0, rw, carry=None):
    t, iw = proj.shape
    aw, kw = dq_a.shape[1], dk_a.shape[1]
    nh = rw // RNN_HEAD_DIM
    nc = t // CHUNK
    kd = RNN_HEAD_DIM
    cb = col0 // kd
    sub = SUB_BWD
    nsub = CHUNK // sub
    dcb = (dcat.shape[1] - rw) // kd
    hp = nh
    assert nh <= HGRN_HEADS_PER_STEP and dcb % hp == 0 and col0 == aw + 2 * kw and iw == col0 + 4 * rw
    w = hp * kd

    def per_head(x, fn):
        return jnp.concatenate([jnp.broadcast_to(fn(x[:, _head(j)]), (CHUNK, kd)) for j in range(hp)], axis=1)

    def body(*refs):
        q_refs, f_refs, i_refs, g_refs = (refs[i * hp:(i + 1) * hp] for i in range(4))
        (lb_ref, ng_ref, o_ref, att_ref, st0_ref, st1_ref, d_ref, dqa_ref, dka_ref, dva_ref, dp_ref, dlb_ref, dng_ref,
         dstate, b_ref, k_ref, dks_ref) = refs[4 * hp:]
        ci = pl.program_id(1)

        @pl.when(ci == 0)
        def _():
            dstate[...] = jnp.zeros_like(dstate)
            dlb_ref[...] = jnp.zeros_like(dlb_ref)
            dng_ref[...] = jnp.zeros_like(dng_ref)

        lbv = lb_ref[...]
        q_r, f_r, v, g_r = (_wide(rs) for rs in (q_refs, f_refs, i_refs, g_refs))
        sg, f, sq, q = _gates(q_r, f_r, lbv)
        k = 1.0 - f
        b = _cumsum_rows(jnp.log(f))
        _store_heads(b_ref, b)
        _store_heads(k_ref, k)
        row, col = _chunk_geometry()

        o = o_ref[...]
        ng = jnp.tile(ng_ref[...], (1, hp))
        sgg = _sigmoid(g_r)
        gate = g_r * sgg
        d_rnn = d_ref[...]
        r = per_head(o, _rstd)
        oh = o * r
        dp_ref[:, :aw] = dqa_ref[...]
        dp_ref[:, aw:aw + kw] = dka_ref[...].astype(BF16)
        dp_ref[:, aw + kw:col0] = dva_ref[...].astype(BF16)
        dp_ref[:, col0 + 3 * rw:] = (d_rnn * oh * ng * (sgg * (1.0 + g_r * (1.0 - sgg)))).astype(BF16)
        d_on = d_rnn * gate
        dng_rows = jnp.sum(d_on * oh, axis=0, keepdims=True)
        dng = dng_rows[:, _head(0)]
        for j in range(1, hp):
            dng = dng + dng_rows[:, _head(j)]
        dng_ref[...] += dng
        dyg = d_on * ng
        do = r * (dyg - oh * per_head(dyg * oh, lambda x: jnp.mean(x, axis=-1, keepdims=True)))

        b_last = b[CHUNK - 1:CHUNK, :]
        eb = jnp.exp(b)
        tail = jnp.exp(b_last - b)
        kdec = k * tail
        decay = jnp.exp(b_last)
        qe = q * eb
        qcat, kcat, ers, ecs = _offdiag_factors(q, k, b, sub)
        diag_mask = _diag_mask(sub)
        dqs, dks, dvs, dads, gsums, dstates = [], [], [], [], [], []
        for j in range(hp):
            hs = _head(j)
            do_h, v_h, dst = do[:, hs], v[:, hs], dstate[j]
            da = jnp.where(row >= col, _dot(do_h, v_h, NT), 0.0)
            dads.append(jnp.where(diag_mask, da, 0.0))
            dq = _dot(do_h, st0_ref[j], NN) * eb[:, hs]
            dk = _dot(v_h, dst, NN) * tail[:, hs]
            dvs.append(_dot(att_ref[j], do_h, TN) + _dot(kdec[:, hs], dst, NT))
            rq = _dot(da, _cat_heads(kcat, hs), NN)
            rk = _dot(da, _cat_heads(qcat, hs), TN)
            for jj in range(nsub - 1):
                dq = dq + ers[jj][:, hs] * rq[:, _head(jj)]
                dk = dk + ecs[jj][:, hs] * rk[:, _head(jj)]
            dqs.append(dq)
            dks.append(dk)
            gsums.append(jnp.sum(dst * st1_ref[j], axis=0, keepdims=True))
            dstates.append(dst * decay[:, hs] + _dot(do_h, qe[:, hs], TN))
        for j in range(hp):
            dstate[j] = dstates[j]
        dq = jnp.concatenate(dqs, axis=1)
        dk = jnp.concatenate(dks, axis=1)
        rloc = lax.broadcasted_iota(jnp.int32, (CHUNK, w), 0) % sub
        for rr in range(sub):
            bs = _sub_rows(b_ref, rr, sub)
            ks = _sub_rows(k_ref, rr, sub)
            e = jnp.exp(jnp.where(rloc >= rr, b - bs, -jnp.inf))
            pick = (col % sub) == rr
            dacol = jnp.concatenate(
                [jnp.broadcast_to(jnp.sum(jnp.where(pick, dads[j], 0.0), axis=-1, keepdims=True), (CHUNK, kd))
                 for j in range(hp)], axis=1)
            wv = dacol * e
            dq = dq + wv * ks
            sums = jnp.sum((wv * q).reshape(nsub, sub, w), axis=1)
            for j in range(hp):
                dks_ref[j, pl.ds(rr, nsub, stride=sub), :] = sums[:, _head(j)]
        dk = dk + jnp.concatenate([dks_ref[j] for j in range(hp)], axis=1)

        dlf = _cumsum_rows(q * dq - k * dk, reverse=True) + jnp.concatenate(gsums, axis=1)
        dfv = dlf / f - dk
        dp_ref[:, col0 + rw:col0 + 2 * rw] = (dfv * (1.0 - lbv) * sg * (1.0 - sg)).astype(BF16)
        dlb_ref[...] += jnp.sum(dfv * (1.0 - sg), axis=0, keepdims=True)
        dp_ref[:, col0:col0 + rw] = (dq * (sq * (1.0 + q_r * (1.0 - sq)))).astype(BF16)
        dp_ref[:, col0 + 2 * rw:col0 + 3 * rw] = jnp.concatenate(dvs, axis=1).astype(BF16)

    def rev(c):
        return nc - 1 - c

    def col_in(kidx, j):
        return pl.BlockSpec((CHUNK, kd), lambda hg, c: (rev(c), cb + kidx * nh + hg * hp + j))

    def rows(width):
        return pl.BlockSpec((CHUNK, width), lambda hg, c: (rev(c), 0))

    return _call(body, name="hgrn_bwd", grid=(1, nc),
                 in_specs=[col_in(kidx, j) for kidx in range(4) for j in range(hp)] +
                          [pl.BlockSpec((1, w), lambda hg, c: (0, hg)), pl.BlockSpec((1, kd), lambda hg, c: (0, 0)),
                           rows(w),
                           pl.BlockSpec((hp, CHUNK, CHUNK), lambda hg, c: (hg, rev(c), 0)),
                           pl.BlockSpec((None, hp, kd, kd), lambda hg, c: (rev(c), hg, 0, 0)),
                           pl.BlockSpec((None, hp, kd, kd),
                                        lambda hg, c: (jnp.minimum(rev(c) + 1, nc - 1), hg, 0, 0)),
                           pl.BlockSpec((CHUNK, w), lambda hg, c: (rev(c), dcb // hp + hg)),
                           rows(aw), rows(kw), rows(kw)],
                 out_specs=[rows(iw),
                            pl.BlockSpec((1, w), lambda hg, c: (0, hg)),
                            pl.BlockSpec((None, 1, kd), lambda hg, c: (hg, 0, 0))],
                 out_shape=[jax.ShapeDtypeStruct((t, iw), BF16), jax.ShapeDtypeStruct((1, rw), F32),
                            jax.ShapeDtypeStruct((1, 1, kd), F32)],
                 args=(*([proj] * (4 * hp)), lb, norm_gain, o_all, att_all, st_all, st_all, dcat, dq_a, dk_a, dv_a),
                 scratch_shapes=[pltpu.VMEM((hp, kd, kd), F32), pltpu.VMEM((hp, CHUNK, kd), F32),
                                 pltpu.VMEM((hp, CHUNK, kd), F32), pltpu.VMEM((hp, CHUNK, kd), F32)],
                 sem=("parallel", "arbitrary"), carry=carry)


def comm_only(name, part):
    return _call(lambda: None, name=name, grid=(), in_specs=[], out_specs=[], out_shape=[], args=(), carry=part)[1]


ADD_BLOCK_ELEMS = 1 << 20
ADAMW_BLOCK_ELEMS = 1 << 19


def add_kept_half(name, kept, got, sel, minor, row0=0):
    pieces, rows, cols = got.shape
    tr = _tile(rows, max(16, ADD_BLOCK_ELEMS // cols), mult=16)
    assert row0 % tr == 0
    i0 = row0 // tr

    def body(sel_ref, k_ref, g_ref, o_ref):
        o_ref[...] = (k_ref[...].astype(F32) + g_ref[...].astype(F32)).astype(o_ref.dtype)

    kept_spec = (pl.BlockSpec((None, None, tr, cols), lambda p, i, s: (p, s[0], i + i0, 0)) if minor else
                 pl.BlockSpec((None, None, tr, cols), lambda p, i, s: (s[0], p, i + i0, 0)))
    return pl.pallas_call(
        body, name=name,
        grid_spec=pltpu.PrefetchScalarGridSpec(
            num_scalar_prefetch=1, grid=(pieces, rows // tr),
            in_specs=[kept_spec, pl.BlockSpec((None, tr, cols), lambda p, i, s: (p, i, 0))],
            out_specs=pl.BlockSpec((None, tr, cols), lambda p, i, s: (p, i, 0))),
        out_shape=jax.ShapeDtypeStruct(got.shape, got.dtype),
        compiler_params=_cparams(("parallel", "parallel")),
    )(sel, kept, got)


def _adamw(w, g, m, v):
    m = ADAM_B1 * m + (1.0 - ADAM_B1) * g
    v = ADAM_B2 * v + (1.0 - ADAM_B2) * (g * g)
    m_hat = m / (1.0 - ADAM_B1 ** ADAM_STEP)
    v_hat = v / (1.0 - ADAM_B2 ** ADAM_STEP)
    delta = -ADAM_LR * (m_hat / (jnp.sqrt(v_hat) + ADAM_EPS) + ADAM_WD * w)
    return delta, m, v


def add_adamw(name, kept, got, sel, w, m, v, row0=0, into=None):
    _, rows, cols = got.shape
    tr = _tile(rows, max(16, ADAMW_BLOCK_ELEMS // cols), mult=16)
    assert row0 % tr == 0
    i0 = row0 // tr
    n_into = 0 if into is None else len(into)

    def body(sel_ref, k_ref, g_ref, w_ref, m_ref, v_ref, *rest):
        go_ref, d_ref, mo_ref, vo_ref = rest[n_into:]
        g = k_ref[...].astype(F32) + g_ref[...].astype(F32)
        go_ref[...] = g
        d_ref[...], mo_ref[...], vo_ref[...] = _adamw(w_ref[...], g, m_ref[...], v_ref[...])

    shard_tile = pl.BlockSpec((tr, cols), lambda i, s: (i + i0, 0))
    return pl.pallas_call(
        body, name=name,
        grid_spec=pltpu.PrefetchScalarGridSpec(
            num_scalar_prefetch=1, grid=(rows // tr,),
            in_specs=[pl.BlockSpec((None, None, tr, cols), lambda i, s: (s[0], 0, i, 0)),
                      pl.BlockSpec((None, tr, cols), lambda i, s: (0, i, 0)), shard_tile, shard_tile, shard_tile,
                      *[ANY] * n_into],
            out_specs=[shard_tile] * 4),
        out_shape=[jax.ShapeDtypeStruct(w.shape, F32)] * 4,
        input_output_aliases={6 + k: k for k in range(n_into)},
        compiler_params=_cparams(("parallel",)),
    )(sel, kept, got, w, m, v, *(into or ()))


def small_allreduce_adamw(partial, scale, w, m, v):
    rows = partial.shape[0]

    def body(p_ref, s_ref, w_ref, m_ref, v_ref, g_ref, d_ref, mo_ref, vo_ref, slots, send_sems, recv_sems):
        x, y, c = _coords()
        my_slot = _slab_index((x, y, c))
        slots[my_slot] = p_ref[...]
        copies = []
        for mask in range(1, N_DEV):
            to = tuple(1 - v_ if (mask >> s_) & 1 else v_ for v_, s_ in ((x, 2), (y, 1), (c, 0)))
            copies.append(pltpu.make_async_remote_copy(
                src_ref=p_ref, dst_ref=slots.at[my_slot],
                send_sem=send_sems.at[mask - 1], recv_sem=recv_sems.at[mask - 1],
                device_id=to, device_id_type=MESH))
        for cp in copies:
            cp.start()
        for cp in copies:
            cp.wait()
        total = slots[0]
        for b in range(1, N_DEV):
            total = total + slots[b]
        g = total * s_ref[...]
        g_ref[...] = g
        d_ref[...], mo_ref[...], vo_ref[...] = _adamw(w_ref[...], g, m_ref[...], v_ref[...])

    vm = pl.BlockSpec(memory_space=pltpu.VMEM)
    return pl.pallas_call(
        body, name="small_allreduce_adamw",
        in_specs=[vm] * 5, out_specs=[vm] * 4,
        out_shape=[jax.ShapeDtypeStruct((rows, LANES), F32)] * 4,
        scratch_shapes=[pltpu.VMEM((N_DEV, rows, LANES), F32),
                        pltpu.SemaphoreType.DMA((N_DEV - 1,)), pltpu.SemaphoreType.DMA((N_DEV - 1,))],
        compiler_params=pltpu.CompilerParams(has_side_effects=True),
    )(partial, scale, w, m, v)


_SMALL = ("attn_sinks", "attn_out_gain", "rnn_lb_logits", "rnn_norm_gain", "mix_pre_gain", "mix_post_gain",
          "mlp_pre_gain", "mlp_post_gain")


def _pack(parts):
    rows = []
    for p in parts:
        flat = p.reshape(-1).astype(F32)
        pad = (-flat.shape[0]) % LANES
        rows.append(jnp.pad(flat, (0, pad)).reshape(-1, LANES))
    packed = jnp.concatenate(rows, axis=0)
    pad_rows = (-packed.shape[0]) % 8
    return jnp.pad(packed, ((0, pad_rows), (0, 0)))


def _unpack(packed, shapes):
    out, r = [], 0
    for s in shapes:
        size = math.prod(s)
        nrows = -(-size // LANES)
        out.append(packed[r:r + nrows].reshape(-1)[:size].reshape(s))
        r += nrows
    return out


class _Scatter:
    def __init__(self, tag, grad, sels, both_links=False):
        self.tag, self.sels, self.both = tag, sels, both_links
        self.shape = grad.shape[1:]
        self.half = self.shape[0] // 2
        self.cur = grad.reshape(4, 2, *self.shape)
        self.stage = 0

    def step(self):
        if self.stage == 0 or not self.both:
            return _scatter_step(self.cur, "cxy"[self.stage])
        if self.stage == 1:
            return _merge(_scatter_step(self.cur, "x", rows=(0, self.half)),
                          _scatter_step(self.cur, "y", minor=True, rows=(self.half, self.shape[0])))
        upper, lower = self.cur
        return _merge(_scatter_step(upper, "y"), _scatter_step(lower, "x"))

    def land(self, got, w=None, m=None, v=None):
        stage, tag, sels = self.stage, self.tag, self.sels
        self.stage += 1
        if stage == 0 or not self.both:
            axis = "cxy"[stage]
            name = "rs_add_%s_%s" % (axis, tag)
            if axis == "y":
                return add_adamw(name, self.cur, got, sels[axis], w, m, v)
            summed = add_kept_half(name, self.cur, got, sels[axis], minor=axis == "c")
            self.cur = summed.reshape(2, summed.shape[0] // 2, *self.shape)
            return None
        got_upper, got_lower = got
        if stage == 1:
            upper = add_kept_half("rs_add_x_%s_upper" % tag, self.cur, got_upper, sels["x"], minor=False)
            lower = add_kept_half("rs_add_y_%s_lower" % tag, self.cur, got_lower, sels["y"], minor=True,
                                  row0=self.half)
            self.cur = tuple(s.reshape(2, 1, *s.shape[1:]) for s in (upper, lower))
            return None
        upper, lower = self.cur
        out_upper = add_adamw("rs_add_y_%s_upper" % tag, upper, got_upper, sels["y"], w, m, v)
        return add_adamw("rs_add_x_%s_lower" % tag, lower, got_lower, sels["x"], w, m, v, row0=self.half,
                         into=out_upper)


def kernel(x, w_in, attn_sinks, attn_out_gain, rnn_lb_logits, rnn_norm_gain, w_out, mix_pre_gain, mix_post_gain, mlp_pre_gain, mlp_post_gain, w_up, w_down, loss_target, m_w_in, m_attn_sinks, m_attn_out_gain, m_rnn_lb_logits, m_rnn_norm_gain, m_w_out, m_mix_pre_gain, m_mix_post_gain, m_mlp_pre_gain, m_mlp_post_gain, m_w_up, m_w_down, v_w_in, v_attn_sinks, v_attn_out_gain, v_rnn_lb_logits, v_rnn_norm_gain, v_w_out, v_mix_pre_gain, v_mix_post_gain, v_mlp_pre_gain, v_mlp_post_gain, v_w_up, v_w_down):
    xs, target = x[0], loss_target[0]
    t, d = xs.shape
    aw = d // 2
    rw = d - aw
    col0 = aw + 2 * N_KV_HEADS * HEAD_DIM
    small_w = dict(attn_sinks=attn_sinks, attn_out_gain=attn_out_gain, rnn_lb_logits=rnn_lb_logits,
                   rnn_norm_gain=rnn_norm_gain, mix_pre_gain=mix_pre_gain, mix_post_gain=mix_post_gain,
                   mlp_pre_gain=mlp_pre_gain, mlp_post_gain=mlp_post_gain)
    small_m = dict(attn_sinks=m_attn_sinks, attn_out_gain=m_attn_out_gain, rnn_lb_logits=m_rnn_lb_logits,
                   rnn_norm_gain=m_rnn_norm_gain, mix_pre_gain=m_mix_pre_gain, mix_post_gain=m_mix_post_gain,
                   mlp_pre_gain=m_mlp_pre_gain, mlp_post_gain=m_mlp_post_gain)
    small_v = dict(attn_sinks=v_attn_sinks, attn_out_gain=v_attn_out_gain, rnn_lb_logits=v_rnn_lb_logits,
                   rnn_norm_gain=v_rnn_norm_gain, mix_pre_gain=v_mix_pre_gain, mix_post_gain=v_mix_post_gain,
                   mlp_pre_gain=v_mlp_pre_gain, mlp_post_gain=v_mlp_post_gain)
    cx, cy, cc = _coords()
    sels = {a: jnp.reshape(v_, (1,)).astype(jnp.int32) for a, v_ in (("x", cx), ("y", cy), ("c", cc))}

    w_in_t, m_in_t, v_in_t = w_in[0].T, m_w_in[0].T, v_w_in[0].T
    s_in, s_out, s_up, s_down = (w.astype(BF16) for w in (w_in_t, w_out[0], w_up[0], w_down[0]))
    probs = jax.nn.softmax(rnn_lb_logits.astype(F32), axis=0)
    lb = probs[0:1]

    (h1,), (wint_part,) = pre_norm(xs, mix_pre_gain, carry=_gather_first(s_in, diagonal=False))
    in_rows = s_in.shape[0]
    wint = comm_only("gather_rest_w_in", _pass_slabs(
        wint_part,
        [(_X, _Y, (0, in_rows // 2)), (_Y, _X, (in_rows // 2, in_rows)), (_X, _C, None), (_Y, _C, None)],
        then=[(_XY, _C, None)]))[0].reshape(-1, d)
    up_rows = s_up.shape[0]
    up_cut = up_rows * 9 // 16
    proj, (wup_part,) = mm_nt("in_proj", h1, wint, F32, carry=_gather_first(s_up, rows=(0, up_cut)))
    (attn_o, attn_n), (wup_half, wout_half) = attn_fwd(
        proj, attn_sinks, attn_out_gain, aw,
        carry=_merge(_gather_first(s_up, rows=(up_cut, up_rows), into=wup_part), _gather_first(s_out)))
    (cat, o_r, att, st), (wup, wout, wdown_half) = hgrn_fwd(
        proj, attn_n, lb, rnn_norm_gain, col0, rw,
        carry=_merge(_gather_second(wup_half), _gather_second(wout_half), _gather_first(s_down)))
    wout = wout.reshape(-1, d)
    mixed, (wdown,) = mm_nn("out_proj", cat, wout, F32, carry=_gather_second(wdown_half))
    wdown = wdown.reshape(-1, d)
    x1, h2 = mid_fwd(mixed, mix_post_gain, xs, mlp_pre_gain)
    u = up_proj(h2, wup)
    y = down_proj(u, wdown)
    sse, dout, dy, dg_mlppost = loss_bwd(y, mlp_post_gain, x1, target)

    du = down_bwd_act(dy, wdown, u)
    rs_down = _Scatter("down", down_wgrad(u, dy).reshape(N_DEV, -1, d), sels, both_links=True)
    dh2, (got,) = up_bwd_x(du, wup, carry=rs_down.step())
    rs_down.land(got)
    dwup, gots = up_wgrad(h2, du, carry=rs_down.step())
    rs_down.land(gots)
    rs_up = _Scatter("up", dwup, sels, both_links=True)
    (dx1, dmixed, dg_mlppre, dg_mixpost), (got,) = mid_bwd(dh2, x1, mlp_pre_gain, dout, mixed, mix_post_gain,
                                                          carry=rs_up.step())
    rs_up.land(got)
    dcat = mm_nt("out_bwd_x", dmixed, wout, F32)
    rs_out = _Scatter("out", mm_tn("out_wgrad", cat, dmixed, BF16).reshape(N_DEV, -1, d), sels)
    (dq_a, dk_a, dv_a, dsinks, daog), (*gots, got_o) = attn_bwd(
        proj, attn_sinks, attn_out_gain, attn_o, dcat, aw, carry=_merge(rs_down.step(), rs_out.step()))
    out_down = rs_down.land(gots, w_down[0], m_w_down[0], v_w_down[0])
    rs_out.land(got_o)
    (dproj, dlb, dng), (*gots, got_o) = hgrn_bwd(
        proj, lb, rnn_norm_gain, o_r, att, st, dcat, dq_a, dk_a, dv_a, col0, rw,
        carry=_merge(rs_up.step(), rs_out.step()))
    rs_up.land(gots)
    rs_out.land(got_o)
    dwin, (*gots, got_o) = mm_tn("in_wgrad", dproj, h1, BF16, carry=_merge(rs_up.step(), rs_out.step()))
    out_up = rs_up.land(gots, w_up[0], m_w_up[0], v_w_up[0])
    out_out = rs_out.land(got_o, w_out[0], m_w_out[0], v_w_out[0])
    rs_in = _Scatter("in", dwin.reshape(N_DEV, -1, d), sels, both_links=True)
    rs_in.land(comm_only("rs_exchange_c_in", rs_in.step())[0])
    dh1, gots = mm_nn("in_bwd_x", dproj, wint, F32, tm=MM_TILE // 2, carry=rs_in.step())
    rs_in.land(gots)
    grad_x, dg_mixpre = first_bwd(dh1, xs, mix_pre_gain, dx1)
    out_in = rs_in.land(comm_only("rs_exchange_last_in", rs_in.step()), w_in_t, m_in_t, v_in_t)
    big_out = [out_in, out_out, out_up, out_down]

    n_heads = attn_sinks.shape[1]
    jac = probs[0] * probs[1]
    partial = _pack([sse, dsinks[0, :n_heads], daog, jnp.stack([dlb[0], dlb[0]]), jnp.sum(dng, axis=0),
                     dg_mixpre, dg_mixpost, dg_mlppre, dg_mlppost])
    ones = [jnp.ones(small_w[k].shape, F32) for k in _SMALL]
    ones[2] = jnp.stack([jac, -jac])
    scale = _pack([jnp.full((1,), 0.5 / d, F32)] + ones)
    zero = jnp.zeros((1,), F32)
    outs = small_allreduce_adamw(partial, scale, _pack([zero] + [small_w[k] for k in _SMALL]),
                                 _pack([zero] + [small_m[k] for k in _SMALL]),
                                 _pack([jnp.ones((1,), F32)] + [small_v[k] for k in _SMALL]))
    shapes = [(1,)] + [small_w[k].shape for k in _SMALL]
    sgrad, sdelta, snm, snv = (_unpack(o, shapes) for o in outs)
    loss = sgrad[0][0]

    def big(i, j):
        o = big_out[i][j]
        return (o.T if i == 0 else o)[None]

    def ordered(j, smalls):
        s = dict(zip(_SMALL, smalls[1:]))
        return [big(0, j), s["attn_sinks"], s["attn_out_gain"], s["rnn_lb_logits"], s["rnn_norm_gain"], big(1, j),
                s["mix_pre_gain"], s["mix_post_gain"], s["mlp_pre_gain"], s["mlp_post_gain"], big(2, j), big(3, j)]

    return (loss, grad_x[None], *ordered(0, sgrad), *ordered(1, sdelta), *ordered(2, snm), *ordered(3, snv))
```

```python
import math

import jax
import jax.numpy as jnp
from jax import lax
from jax.experimental import pallas as pl
from jax.experimental.pallas import tpu as pltpu

F32 = jnp.float32
BF16 = jnp.bfloat16

HEAD_DIM = 64
N_KV_HEADS = 2
BLOCK = 128
RNN_HEAD_DIM = 128
CHUNK = 64
SUB_FWD = 16
SUB_BWD = 16
EPS = 1e-6

ADAM_LR = 0.001
ADAM_B1 = 0.9
ADAM_B2 = 0.999
ADAM_EPS = 1e-08
ADAM_WD = 0.01
ADAM_STEP = 10

N_DEV = 8
LANES = 128
V7X_VMEM_LIMIT = 56 * 1024 * 1024
MESH = pl.DeviceIdType.MESH
HI = lax.Precision.HIGHEST
ANY = pl.BlockSpec(memory_space=pl.ANY)
_AXES = ("x", "y", "c")


def _cparams(sem=None, **kw):
    return pltpu.CompilerParams(dimension_semantics=sem, vmem_limit_bytes=V7X_VMEM_LIMIT, **kw)


def _dot(a, b, dims):
    return lax.dot_general(a.astype(BF16), b.astype(BF16), (dims, ((), ())), preferred_element_type=F32)


NN = ((1,), (0,))
NT = ((1,), (1,))
TN = ((0,), (0,))


def _pick(n, pref):
    t = min(n, pref)
    while n % t:
        t //= 2
    return t


def _tile(n, pref, mult=LANES):
    if n <= pref:
        return n
    t = pref - pref % mult
    while n % t:
        t -= mult
    return t


def _coords():
    return lax.axis_index("x"), lax.axis_index("y"), lax.axis_index("c")


def _slab_index(dev):
    return 4 * dev[0] + 2 * dev[1] + dev[2]


class _Part:
    def __init__(self, operands, landings, aliases, n_sems, plan):
        self.operands, self.landings, self.aliases, self.n_sems, self.plan = operands, landings, aliases, n_sems, plan


def _merge(*parts):
    operands, landings, aliases, plans = [], [], {}, []
    s0 = 0
    for p in parts:
        o0, l0 = len(operands), len(landings)
        aliases.update({o0 + i: l0 + j for i, j in p.aliases.items()})
        plans.append((p.plan, o0, len(p.operands), l0, len(p.landings), s0))
        operands += p.operands
        landings += p.landings
        s0 += p.n_sems

    def plan(ops, lands, sem):
        starts, waits = [], []
        for f, o0, no, l0, nl, off in plans:
            s, w = f(ops[o0:o0 + no], lands[l0:l0 + nl], lambda kind, k, off=off: sem(kind, off + k))
            starts += s
            waits += w
        return starts, waits

    return _Part(operands, landings, aliases, s0, plan)


def _gather_peers(x, y, c):
    return [(x, y, 1 - c), (1 - x, y, c), (x, 1 - y, c), (1 - x, 1 - y, c)]


def _gather_first(shard, rows=None, into=None, diagonal=True):
    lo, hi = (0, shard.shape[0]) if rows is None else rows
    n_peers = 4 if diagonal else 3

    def plan(ops, lands, sem):
        x, y, c = _coords()
        me, peers = (x, y, c), _gather_peers(x, y, c)[:n_peers]
        src = ops[0].at[pl.ds(lo, hi - lo)]

        def slab(block):
            return lands[0].at[_slab_index(block), pl.ds(lo, hi - lo)]

        def cp(k, block, to):
            return pltpu.make_async_remote_copy(
                src_ref=src, dst_ref=slab(block),
                send_sem=sem(0, k), recv_sem=sem(1, k), device_id=to, device_id_type=MESH)

        local = pltpu.make_async_copy(src, slab(me), sem(2, 0))
        sends = [cp(k, me, to) for k, to in enumerate(peers)]
        recvs = [cp(k, frm, me) for k, frm in enumerate(peers)]
        return ([local.start] + [s.start for s in sends],
                [local.wait] + [s.wait_send for s in sends] + [r.wait_recv for r in recvs])

    landing = jax.ShapeDtypeStruct((N_DEV, *shard.shape), shard.dtype)
    if into is None:
        return _Part([shard], [landing], {}, 4, plan)
    return _Part([shard, into], [landing], {1: 0}, 4, plan)


def _flip(dev, flips):
    return tuple(1 - v if f else v for v, f in zip(dev, flips))


def _pass_slabs(gathered, moves, then=()):
    def wave(lands, sem, k0, wave_moves):
        me = _coords()
        sends, recvs = [], []
        for k, (block, dest, rows) in enumerate(wave_moves, start=k0):
            lo, hi = (0, gathered.shape[1]) if rows is None else rows

            def cp(blk, to, k=k, lo=lo, hi=hi):
                slab = lands[0].at[_slab_index(blk), pl.ds(lo, hi - lo)]
                return pltpu.make_async_remote_copy(
                    src_ref=slab, dst_ref=slab, send_sem=sem(0, k), recv_sem=sem(1, k),
                    device_id=to, device_id_type=MESH)

            sends.append(cp(_flip(me, block), _flip(me, dest)))
            recvs.append(cp(_flip(_flip(me, dest), block), me))
        return [s.start for s in sends], [s.wait_send for s in sends] + [r.wait_recv for r in recvs]

    def plan(ops, lands, sem):
        starts, waits = wave(lands, sem, 0, moves)
        if then:
            starts2, waits2 = wave(lands, sem, len(moves), then)
            waits = waits + starts2 + waits2
        return starts, waits

    return _Part([gathered], [jax.ShapeDtypeStruct(gathered.shape, gathered.dtype)], {0: 0},
                 len(moves) + len(then), plan)


_X, _Y, _C, _XY = (1, 0, 0), (0, 1, 0), (0, 0, 1), (1, 1, 0)


def _gather_second(gathered):
    def plan(ops, lands, sem):
        x, y, c = _coords()
        sibling = (x, y, 1 - c)
        chips = [(1 - x, y), (x, 1 - y), (1 - x, 1 - y)]

        def cp(k, block):
            slab = lands[0].at[_slab_index(block)]
            return pltpu.make_async_remote_copy(
                src_ref=slab, dst_ref=slab, send_sem=sem(0, k), recv_sem=sem(1, k),
                device_id=sibling, device_id_type=MESH)

        sends = [cp(k, (*chip, c)) for k, chip in enumerate(chips)]
        recvs = [cp(k, (*chip, 1 - c)) for k, chip in enumerate(chips)]
        return [s.start for s in sends], [s.wait_send for s in sends] + [r.wait_recv for r in recvs]

    return _Part([gathered], [jax.ShapeDtypeStruct(gathered.shape, gathered.dtype)], {0: 0}, 3, plan)


def _scatter_step(array, axis, minor=None, rows=None):
    minor = (axis == "c") if minor is None else minor
    pieces = array.shape[0] if minor else array.shape[1]
    lo, hi = (0, array.shape[2]) if rows is None else rows

    def plan(ops, lands, sem):
        coords = list(_coords())
        ai = _AXES.index(axis)
        mine = coords[ai]
        peer = list(coords)
        peer[ai] = 1 - mine
        cps = []
        for p in range(pieces):
            src = ops[0].at[p, 1 - mine, pl.ds(lo, hi - lo)] if minor else ops[0].at[1 - mine, p, pl.ds(lo, hi - lo)]
            cps.append(pltpu.make_async_remote_copy(
                src_ref=src, dst_ref=lands[0].at[p], send_sem=sem(0, p), recv_sem=sem(1, p),
                device_id=tuple(peer), device_id_type=MESH))
        return [cp.start for cp in cps], [cp.wait for cp in cps]

    return _Part([array], [jax.ShapeDtypeStruct((pieces, hi - lo, array.shape[3]), array.dtype)], {}, pieces, plan)


def _grid_edges(grid):
    first = last = None
    for ax, n in enumerate(grid):
        p = pl.program_id(ax)
        f, l = p == 0, p == n - 1
        first = f if first is None else jnp.logical_and(first, f)
        last = l if last is None else jnp.logical_and(last, l)
    return first, last


def _call(body, *, name, grid, in_specs, out_specs, out_shape, args, scratch_shapes=(), sem=None, carry=None):
    if carry is None:
        return pl.pallas_call(
            body, name=name, grid=grid, in_specs=list(in_specs), out_specs=list(out_specs),
            out_shape=list(out_shape), scratch_shapes=list(scratch_shapes), compiler_params=_cparams(sem),
        )(*args)
    n_in, n_out, n_scr = len(in_specs), len(out_specs), len(scratch_shapes)
    n_cin, n_cout = len(carry.operands), len(carry.landings)

    def wrapped(*refs):
        ins, cins = refs[:n_in], refs[n_in:n_in + n_cin]
        o0 = n_in + n_cin
        outs, couts = refs[o0:o0 + n_out], refs[o0 + n_out:o0 + n_out + n_cout]
        s0 = o0 + n_out + n_cout
        scr, sems = refs[s0:s0 + n_scr], refs[s0 + n_scr:]
        first, last = _grid_edges(grid)

        def plan():
            return carry.plan(cins, couts, lambda kind, k: sems[kind].at[k])

        def start_all():
            for start in plan()[0]:
                start()

        def wait_all():
            for wait in plan()[1]:
                wait()

        if grid:
            pl.when(first)(start_all)
            body(*ins, *outs, *scr)
            pl.when(last)(wait_all)
        else:
            start_all()
            body(*ins, *outs, *scr)
            wait_all()

    sem_arrays = [pltpu.SemaphoreType.DMA((carry.n_sems,))] * 3
    res = pl.pallas_call(
        wrapped, name=name, grid=grid,
        in_specs=[*in_specs, *[ANY] * n_cin], out_specs=[*out_specs, *[ANY] * n_cout],
        out_shape=[*out_shape, *carry.landings],
        scratch_shapes=[*scratch_shapes, *sem_arrays],
        input_output_aliases={n_in + i: n_out + j for i, j in carry.aliases.items()},
        compiler_params=_cparams(("arbitrary",) * len(grid) if grid else None, has_side_effects=True),
    )(*args, *carry.operands)
    return res[:n_out], res[n_out:]


MM_TILE = 1024
MM_K_TILE = 2048
MXU_COLS = 256
MM_VMEM_BUDGET = 50 * 1024 * 1024


def _matmul(name, a, b, dims, grid, a_spec, b_spec, out_shape, out_spec, epilogue,
            extras=(), extra_specs=(), prologue=None, carry=None):
    nk = grid[2]
    n_extra = len(extras)
    acc_shape = out_spec.block_shape[-2:]

    def lhs(a_ref):
        return a_ref[...] if prologue is None else prologue(a_ref[...])

    def body_one(a_ref, b_ref, *rest):
        epilogue(_dot(lhs(a_ref), b_ref[...], dims), rest[:n_extra], rest[n_extra:])

    def body_acc(a_ref, b_ref, *rest):
        acc = rest[-1]
        k = pl.program_id(2)
        part = _dot(lhs(a_ref), b_ref[...], dims)

        @pl.when(k == 0)
        def _():
            acc[...] = part

        @pl.when(k > 0)
        def _():
            acc[...] += part

        @pl.when(k == nk - 1)
        def _():
            epilogue(acc[...], rest[:n_extra], rest[n_extra:-1])

    res = _call(body_one if nk == 1 else body_acc, name=name, grid=grid,
                in_specs=[a_spec, b_spec, *extra_specs], out_specs=[out_spec], out_shape=[out_shape],
                args=(a, b, *extras), scratch_shapes=[] if nk == 1 else [pltpu.VMEM(acc_shape, F32)],
                sem=("parallel", "parallel", "arbitrary"), carry=carry)
    return res[0] if carry is None else (res[0][0], res[1])


def _store_as(acc, extra_refs, out_refs):
    out_refs[0][...] = acc.astype(out_refs[0].dtype)


def _square(u):
    return u * u


def mm_nn(name, a, b, out_dtype, tk=None, tm=MM_TILE, tn=MM_TILE, prologue=None, carry=None):
    (m, kk), n = a.shape, b.shape[1]
    tm, tn = _tile(m, tm), _tile(n, tn, mult=MXU_COLS)
    tk = kk if tk is None else _tile(kk, tk, mult=MXU_COLS)
    return _matmul(name, a, b, NN, (m // tm, n // tn, kk // tk),
                   pl.BlockSpec((tm, tk), lambda i, j, k: (i, k)),
                   pl.BlockSpec((tk, tn), lambda i, j, k: (k, j)),
                   jax.ShapeDtypeStruct((m, n), out_dtype),
                   pl.BlockSpec((tm, tn), lambda i, j, k: (i, j)), _store_as, prologue=prologue, carry=carry)


def mm_nt(name, a, b, out_dtype, epilogue=_store_as, extras=(), extra_specs=(), carry=None):
    (m, kk), n = a.shape, b.shape[0]
    tm, tn = _tile(m, MM_TILE), _tile(n, MM_TILE, mult=MXU_COLS)
    return _matmul(name, a, b, NT, (m // tm, n // tn, 1),
                   pl.BlockSpec((tm, kk), lambda i, j, k: (i, 0)),
                   pl.BlockSpec((tn, kk), lambda i, j, k: (j, 0)),
                   jax.ShapeDtypeStruct((m, n), out_dtype),
                   pl.BlockSpec((tm, tn), lambda i, j, k: (i, j)), epilogue,
                   extras=extras, extra_specs=extra_specs, carry=carry)


def _whole_k_fits(tm, tn, kk, out_dtype, prologue):
    operands = 2 * 2 * kk * (tm + tn)
    out = 2 * tm * tn * jnp.dtype(out_dtype).itemsize + 4 * tm * tn
    return operands + out + (2 * kk * tm if prologue is not None else 0) <= MM_VMEM_BUDGET


def mm_tn(name, a, b, out_dtype, prologue=None, carry=None):
    (kk, m), n = a.shape, b.shape[1]
    tm, tn = _tile(m, MM_TILE), _tile(n, MM_TILE)
    tk = kk if _whole_k_fits(tm, tn, kk, out_dtype, prologue) else _tile(kk, MM_K_TILE)
    return _matmul(name, a, b, TN, (m // tm, n // tn, kk // tk),
                   pl.BlockSpec((tk, tm), lambda i, j, k: (k, i)),
                   pl.BlockSpec((tk, tn), lambda i, j, k: (k, j)),
                   jax.ShapeDtypeStruct((m, n), out_dtype),
                   pl.BlockSpec((tm, tn), lambda i, j, k: (i, j)), _store_as, prologue=prologue, carry=carry)


def up_proj(h2, wup_slabs, carry=None):
    (m, kk), (_, _, ns) = h2.shape, wup_slabs.shape
    tm, tn = _tile(m, MM_TILE), _tile(ns, MM_TILE)
    r = ns // tn
    n = N_DEV * ns

    def epi(acc, extra_refs, out_refs):
        out_refs[0][...] = jnp.maximum(acc, 0.0).astype(BF16)

    return _matmul("up_proj", h2, wup_slabs, NN, (m // tm, n // tn, 1),
                   pl.BlockSpec((tm, kk), lambda i, j, k: (i, 0)),
                   pl.BlockSpec((None, kk, tn), lambda i, j, k: (j // r, 0, j % r)),
                   jax.ShapeDtypeStruct((m, n), BF16),
                   pl.BlockSpec((tm, tn), lambda i, j, k: (i, j)), epi, carry=carry)


def down_proj(u, wdown):
    return mm_nn("down_proj", u, wdown, F32, tm=MM_TILE // 2, tn=MM_TILE // 2, prologue=_square)


def down_bwd_act(dy, wdown, u):
    tm, tn = _tile(dy.shape[0], MM_TILE), _tile(wdown.shape[0], MM_TILE)

    def epi(acc, extra_refs, out_refs):
        out_refs[0][...] = (acc * (2.0 * extra_refs[0][...].astype(F32))).astype(BF16)

    return mm_nt("down_bwd_act", dy, wdown, BF16, epilogue=epi, extras=(u,),
                 extra_specs=(pl.BlockSpec((tm, tn), lambda i, j, k: (i, j)),))


def down_wgrad(u, dy):
    return mm_tn("down_wgrad", u, dy, BF16, prologue=_square)


def up_bwd_x(du, wup_slabs, carry=None):
    (m, kk), (slabs, n, ns) = du.shape, wup_slabs.shape
    tm, tn = _tile(m, MM_TILE // 2), _tile(n, MM_TILE // 2, mult=MXU_COLS)

    def body(a_ref, b_ref, o_ref):
        acc = _dot(a_ref[:, :ns], b_ref[0], NT)
        for s in range(1, slabs):
            acc = acc + _dot(a_ref[:, s * ns:(s + 1) * ns], b_ref[s], NT)
        o_ref[...] = acc

    res = _call(body, name="up_bwd_x", grid=(m // tm, n // tn),
                in_specs=[pl.BlockSpec((tm, kk), lambda i, j: (i, 0)),
                          pl.BlockSpec((slabs, tn, ns), lambda i, j: (0, j, 0))],
                out_specs=[pl.BlockSpec((tm, tn), lambda i, j: (i, j))],
                out_shape=[jax.ShapeDtypeStruct((m, n), F32)], args=(du, wup_slabs),
                sem=("parallel", "parallel"), carry=carry)
    return res[0] if carry is None else (res[0][0], res[1])


def up_wgrad(h2, du, carry=None):
    (kk, m), n = h2.shape, du.shape[1]
    ns = n // N_DEV
    tm, tn = _tile(m, MM_TILE), _tile(ns, MM_TILE)
    tk = kk if _whole_k_fits(tm, tn, kk, BF16, None) else _tile(kk, MM_K_TILE)
    r = ns // tn
    return _matmul("up_wgrad", h2, du, TN, (m // tm, n // tn, kk // tk),
                   pl.BlockSpec((tk, tm), lambda i, j, k: (k, i)),
                   pl.BlockSpec((tk, tn), lambda i, j, k: (k, j)),
                   jax.ShapeDtypeStruct((N_DEV, m, ns), BF16),
                   pl.BlockSpec((None, tm, tn), lambda i, j, k: (j // r, i, j % r)), _store_as, carry=carry)


def _rstd(x):
    return lax.rsqrt(jnp.mean(x * x, axis=-1, keepdims=True) + EPS)


def _norm_bwd(x, g, dy):
    r = _rstd(x)
    xh = x * r
    dyg = dy * g
    dx = r * (dyg - xh * jnp.mean(dyg * xh, axis=-1, keepdims=True))
    return dx, jnp.sum(dy * xh, axis=0, keepdims=True)


def _row_spec(tr, d):
    return pl.BlockSpec((tr, d), lambda i: (i, 0))


def _vec_spec(d):
    return pl.BlockSpec((1, d), lambda i: (0, 0))


def _accum(ref, val):
    @pl.when(pl.program_id(0) == 0)
    def _():
        ref[...] = jnp.zeros_like(ref)

    ref[...] += val


def pre_norm(x, g, carry=None, tr=256):
    t, d = x.shape
    tr = _pick(t, tr)

    def body(x_ref, g_ref, h_ref):
        xx = x_ref[...]
        h_ref[...] = (xx * _rstd(xx) * g_ref[...]).astype(BF16)

    return _call(body, name="pre_norm", grid=(t // tr,),
                 in_specs=[_row_spec(tr, d), _vec_spec(d)], out_specs=[_row_spec(tr, d)],
                 out_shape=[jax.ShapeDtypeStruct((t, d), BF16)], args=(x, g), sem=("parallel",), carry=carry)


def mid_fwd(mixed, g_post, x, g_pre2, tr=256):
    t, d = x.shape
    tr = _pick(t, tr)

    def body(m_ref, gp_ref, x_ref, g2_ref, x1_ref, h2_ref):
        mm = m_ref[...]
        x1 = x_ref[...] + mm * _rstd(mm) * gp_ref[...]
        x1_ref[...] = x1
        h2_ref[...] = (x1 * _rstd(x1) * g2_ref[...]).astype(BF16)

    return _call(body, name="mid_fwd", grid=(t // tr,),
                 in_specs=[_row_spec(tr, d), _vec_spec(d), _row_spec(tr, d), _vec_spec(d)],
                 out_specs=[_row_spec(tr, d), _row_spec(tr, d)],
                 out_shape=[jax.ShapeDtypeStruct((t, d), F32), jax.ShapeDtypeStruct((t, d), BF16)],
                 args=(mixed, g_post, x, g_pre2), sem=("parallel",))


def loss_bwd(y, g_post2, x1, target, tr=256):
    t, d = y.shape
    tr = _pick(t, tr)

    def body(y_ref, g_ref, x1_ref, t_ref, sse_ref, dout_ref, dy_ref, dg_ref):
        yy = y_ref[...]
        g = g_ref[...]
        err = x1_ref[...] + yy * _rstd(yy) * g - t_ref[...]
        _accum(sse_ref, jnp.sum(jnp.sum(err * err, axis=1, keepdims=True), axis=0, keepdims=True))
        dout = err * (1.0 / d)
        dout_ref[...] = dout
        dy, dg = _norm_bwd(yy, g, dout)
        dy_ref[...] = dy.astype(BF16)
        _accum(dg_ref, dg)

    return _call(body, name="loss_bwd", grid=(t // tr,),
                 in_specs=[_row_spec(tr, d), _vec_spec(d), _row_spec(tr, d), _row_spec(tr, d)],
                 out_specs=[pl.BlockSpec((1, 1), lambda i: (0, 0)), _row_spec(tr, d), _row_spec(tr, d), _vec_spec(d)],
                 out_shape=[jax.ShapeDtypeStruct((1, 1), F32), jax.ShapeDtypeStruct((t, d), F32),
                            jax.ShapeDtypeStruct((t, d), BF16), jax.ShapeDtypeStruct((1, d), F32)],
                 args=(y, g_post2, x1, target), sem=("arbitrary",))


def mid_bwd(dh2, x1, g_pre2, dout, mixed, g_post, carry=None, tr=256):
    t, d = x1.shape
    tr = _pick(t, tr)

    def body(dh_ref, x1_ref, g2_ref, do_ref, m_ref, gp_ref, dx1_ref, dm_ref, dg2_ref, dgp_ref):
        d1, dg2 = _norm_bwd(x1_ref[...], g2_ref[...], dh_ref[...])
        dx1 = do_ref[...] + d1
        dx1_ref[...] = dx1
        dm, dgp = _norm_bwd(m_ref[...], gp_ref[...], dx1)
        dm_ref[...] = dm.astype(BF16)
        _accum(dg2_ref, dg2)
        _accum(dgp_ref, dgp)

    return _call(body, name="mid_bwd", grid=(t // tr,),
                 in_specs=[_row_spec(tr, d), _row_spec(tr, d), _vec_spec(d), _row_spec(tr, d), _row_spec(tr, d),
                           _vec_spec(d)],
                 out_specs=[_row_spec(tr, d), _row_spec(tr, d), _vec_spec(d), _vec_spec(d)],
                 out_shape=[jax.ShapeDtypeStruct((t, d), F32), jax.ShapeDtypeStruct((t, d), BF16),
                            jax.ShapeDtypeStruct((1, d), F32), jax.ShapeDtypeStruct((1, d), F32)],
                 args=(dh2, x1, g_pre2, dout, mixed, g_post), sem=("arbitrary",), carry=carry)


def first_bwd(dh1, x, g_pre, dx1, carry=None, tr=256):
    t, d = x.shape
    tr = _pick(t, tr)

    def body(dh_ref, x_ref, g_ref, dx1_ref, gx_ref, dg_ref):
        d0, dg = _norm_bwd(x_ref[...], g_ref[...], dh_ref[...])
        gx_ref[...] = dx1_ref[...] + d0
        _accum(dg_ref, dg)

    return _call(body, name="first_bwd", grid=(t // tr,),
                 in_specs=[_row_spec(tr, d), _row_spec(tr, d), _vec_spec(d), _row_spec(tr, d)],
                 out_specs=[_row_spec(tr, d), _vec_spec(d)],
                 out_shape=[jax.ShapeDtypeStruct((t, d), F32), jax.ShapeDtypeStruct((1, d), F32)],
                 args=(dh1, x, g_pre, dx1), sem=("arbitrary",), carry=carry)


def _attn_geometry(has_prev):
    r = lax.broadcasted_iota(jnp.int32, (BLOCK, 2 * BLOCK), 0)
    c = lax.broadcasted_iota(jnp.int32, (BLOCK, 2 * BLOCK), 1)
    dist = r + BLOCK - c
    valid = jnp.logical_and(jnp.logical_and(dist >= 0, dist < BLOCK), jnp.logical_or(c >= BLOCK, has_prev))
    return dist.astype(F32), valid


def _stack_pairs(x, g, pairs):
    base = g * pairs * LANES
    return jnp.concatenate([x[:, base + p * LANES:base + (p + 1) * LANES] for p in range(pairs)], axis=0)


def _unstack_pairs(xs, pairs):
    return jnp.concatenate([xs[p * BLOCK:(p + 1) * BLOCK, :] for p in range(pairs)], axis=1)


def _to_half(x, g, odd):
    lane = lax.broadcasted_iota(jnp.int32, x.shape, 1)
    y = x if (g == 1) == odd else pltpu.roll(x, HEAD_DIM, axis=1)
    return jnp.where((lane >= HEAD_DIM) == odd, y, 0.0)


def _from_halves(even, odd, g):
    lane = lax.broadcasted_iota(jnp.int32, even.shape, 1)
    if g == 0:
        return jnp.where(lane < HEAD_DIM, even + pltpu.roll(odd, HEAD_DIM, axis=1), 0.0)
    return jnp.where(lane >= HEAD_DIM, pltpu.roll(even, HEAD_DIM, axis=1) + odd, 0.0)


_PARITIES = [(g, odd) for g in range(N_KV_HEADS) for odd in (False, True)]


def _softmax_sink(s, sink_ref, g, odd, group, n_heads, geo):
    dist, valid = geo
    pairs = group // 2
    heads = [g * group + 2 * p + int(odd) for p in range(pairs)]
    bias = jnp.concatenate([(2.0 ** (-8.0 * (h + 1) / n_heads)) * dist for h in heads], axis=0)
    sink = jnp.concatenate([jnp.full((BLOCK, 1), sink_ref[0, h], F32) for h in heads], axis=0)
    s = jnp.where(jnp.concatenate([valid] * pairs, axis=0), s - bias, -jnp.inf)
    m = jnp.maximum(jnp.max(s, axis=-1, keepdims=True), sink)
    p = jnp.exp(s - m)
    p_sink = jnp.exp(sink - m)
    inv = 1.0 / (jnp.sum(p, axis=-1, keepdims=True) + p_sink)
    return p * inv, p_sink * inv


def attn_fwd(proj, sinks, gain, aw, carry=None):
    t = proj.shape[0]
    kw = N_KV_HEADS * HEAD_DIM
    n_heads = aw // HEAD_DIM
    group = n_heads // N_KV_HEADS
    pairs = group // 2
    assert kw == LANES and group % 2 == 0
    nb = t // BLOCK
    scale = HEAD_DIM ** -0.5

    def body(sink_ref, q_ref, k_ref, v_ref, g_ref, o_ref, on_ref):
        n = pl.program_id(0)
        cur = pl.multiple_of(n * BLOCK, BLOCK)
        prev = pl.multiple_of(jnp.maximum(n - 1, 0) * BLOCK, BLOCK)
        geo = _attn_geometry(n > 0)
        kcat = jnp.concatenate([k_ref[pl.ds(prev, BLOCK), :], k_ref[pl.ds(cur, BLOCK), :]], axis=0)
        vcat = jnp.concatenate([v_ref[pl.ds(prev, BLOCK), :], v_ref[pl.ds(cur, BLOCK), :]], axis=0)
        q = q_ref[...] * scale
        qs = [_stack_pairs(q, g, pairs) for g in range(N_KV_HEADS)]
        scores = [_dot(qs[g], _to_half(kcat, g, odd), NT) for g, odd in _PARITIES]
        probs = [_softmax_sink(s, sink_ref, g, odd, group, n_heads, geo)[0] for s, (g, odd) in zip(scores, _PARITIES)]
        outs = [_dot(p, _to_half(vcat, g, odd), NN) for p, (g, odd) in zip(probs, _PARITIES)]
        o = jnp.concatenate([_unstack_pairs(outs[2 * g] + outs[2 * g + 1], pairs) for g in range(N_KV_HEADS)], axis=1)
        o_ref[...] = o
        on_ref[...] = (o * _rstd(o) * g_ref[...]).astype(BF16)

    return _call(body, name="attn_fwd", grid=(nb,),
                 in_specs=[pl.BlockSpec(memory_space=pltpu.SMEM),
                           pl.BlockSpec((BLOCK, aw), lambda n: (n, 0)),
                           pl.BlockSpec((t, kw), lambda n: (0, aw // kw)),
                           pl.BlockSpec((t, kw), lambda n: (0, aw // kw + 1)),
                           pl.BlockSpec((1, aw), lambda n: (0, 0))],
                 out_specs=[pl.BlockSpec((BLOCK, aw), lambda n: (n, 0)), pl.BlockSpec((BLOCK, aw), lambda n: (n, 0))],
                 out_shape=[jax.ShapeDtypeStruct((t, aw), F32), jax.ShapeDtypeStruct((t, aw), BF16)],
                 args=(sinks, proj, proj, proj, gain), sem=("parallel",), carry=carry)


def attn_bwd(proj, sinks, gain, attn_o, dcat, aw, carry=None):
    t = proj.shape[0]
    kw = N_KV_HEADS * HEAD_DIM
    n_heads = aw // HEAD_DIM
    group = n_heads // N_KV_HEADS
    pairs = group // 2
    assert kw == LANES and group % 2 == 0
    nb = t // BLOCK
    scale = HEAD_DIM ** -0.5

    def body(sink_ref, q_ref, k_ref, v_ref, g_ref, o_ref, dn_ref, dq_ref, dk_ref, dv_ref, dsink_ref, dg_ref):
        n = pl.program_id(0)
        cur = pl.multiple_of(n * BLOCK, BLOCK)
        prev = pl.multiple_of(jnp.maximum(n - 1, 0) * BLOCK, BLOCK)
        geo = _attn_geometry(n > 0)

        @pl.when(n == 0)
        def _():
            dk_ref[...] = jnp.zeros_like(dk_ref)
            dv_ref[...] = jnp.zeros_like(dv_ref)
            dsink_ref[...] = jnp.zeros_like(dsink_ref)

        o = o_ref[...]
        do_all, dg = _norm_bwd(o, g_ref[...], dn_ref[...])
        _accum(dg_ref, dg)
        kcat = jnp.concatenate([k_ref[pl.ds(prev, BLOCK), :], k_ref[pl.ds(cur, BLOCK), :]], axis=0)
        vcat = jnp.concatenate([v_ref[pl.ds(prev, BLOCK), :], v_ref[pl.ds(cur, BLOCK), :]], axis=0)
        q = q_ref[...] * scale
        lane = lax.broadcasted_iota(jnp.int32, (1, LANES), 1)
        lane_s = lax.broadcasted_iota(jnp.int32, (pairs * BLOCK, LANES), 1)
        qs = [_stack_pairs(q, g, pairs) for g in range(N_KV_HEADS)]
        dos = [_stack_pairs(do_all, g, pairs) for g in range(N_KV_HEADS)]
        kxs = [_to_half(kcat, g, odd) for g, odd in _PARITIES]
        scores = [_dot(qs[g], kx, NT) for kx, (g, odd) in zip(kxs, _PARITIES)]
        dps = [_dot(dos[g], _to_half(vcat, g, odd), NT) for g, odd in _PARITIES]
        deltas = []
        for g in range(N_KV_HEADS):
            prod = dos[g] * _stack_pairs(o, g, pairs)
            delta_even = jnp.sum(jnp.where(lane_s < HEAD_DIM, prod, 0.0), axis=-1, keepdims=True)
            deltas += [delta_even, jnp.sum(prod, axis=-1, keepdims=True) - delta_even]
        dsink = jnp.zeros((1, LANES), F32)
        ps, dss = [], []
        for i, (g, odd) in enumerate(_PARITIES):
            p, p_sink = _softmax_sink(scores[i], sink_ref, g, odd, group, n_heads, geo)
            ps.append(p)
            dss.append(p * (dps[i] - deltas[i]))
            sink_rows = p_sink * deltas[i]
            for pr in range(pairs):
                h = g * group + 2 * pr + int(odd)
                dsink = dsink + jnp.where(
                    lane == h, -jnp.sum(sink_rows[pr * BLOCK:(pr + 1) * BLOCK], axis=0, keepdims=True), 0.0)
        dq_pairs = [_dot(ds, kx, NN) for ds, kx in zip(dss, kxs)]
        dk_halves = [_dot(ds, qs[g], TN) for ds, (g, odd) in zip(dss, _PARITIES)]
        dv_halves = [_dot(p, dos[g], TN) for p, (g, odd) in zip(ps, _PARITIES)]
        dq_ref[...] = jnp.concatenate(
            [_unstack_pairs((dq_pairs[2 * g] + dq_pairs[2 * g + 1]) * scale, pairs) for g in range(N_KV_HEADS)],
            axis=1).astype(BF16)
        dk_upd = _from_halves(dk_halves[0], dk_halves[1], 0) + _from_halves(dk_halves[2], dk_halves[3], 1)
        dv_upd = _from_halves(dv_halves[0], dv_halves[1], 0) + _from_halves(dv_halves[2], dv_halves[3], 1)
        dk_ref[pl.ds(prev, BLOCK), :] += dk_upd[:BLOCK]
        dv_ref[pl.ds(prev, BLOCK), :] += dv_upd[:BLOCK]
        dk_ref[pl.ds(cur, BLOCK), :] += dk_upd[BLOCK:]
        dv_ref[pl.ds(cur, BLOCK), :] += dv_upd[BLOCK:]
        dsink_ref[...] += dsink

    return _call(body, name="attn_bwd", grid=(nb,),
                 in_specs=[pl.BlockSpec(memory_space=pltpu.SMEM),
                           pl.BlockSpec((BLOCK, aw), lambda n: (n, 0)),
                           pl.BlockSpec((t, kw), lambda n: (0, aw // kw)),
                           pl.BlockSpec((t, kw), lambda n: (0, aw // kw + 1)),
                           pl.BlockSpec((1, aw), lambda n: (0, 0)),
                           pl.BlockSpec((BLOCK, aw), lambda n: (n, 0)),
                           pl.BlockSpec((BLOCK, aw), lambda n: (n, 0))],
                 out_specs=[pl.BlockSpec((BLOCK, aw), lambda n: (n, 0)),
                            pl.BlockSpec((t, kw), lambda n: (0, 0)), pl.BlockSpec((t, kw), lambda n: (0, 0)),
                            pl.BlockSpec((1, LANES), lambda n: (0, 0)), pl.BlockSpec((1, aw), lambda n: (0, 0))],
                 out_shape=[jax.ShapeDtypeStruct((t, aw), BF16), jax.ShapeDtypeStruct((t, kw), F32),
                            jax.ShapeDtypeStruct((t, kw), F32), jax.ShapeDtypeStruct((1, LANES), F32),
                            jax.ShapeDtypeStruct((1, aw), F32)],
                 args=(sinks, proj, proj, proj, gain, attn_o, dcat), sem=("arbitrary",), carry=carry)


def _sigmoid(x):
    return 0.5 * jnp.tanh(0.5 * x) + 0.5


def _chunk_geometry():
    row = lax.broadcasted_iota(jnp.int32, (CHUNK, CHUNK), 0)
    col = lax.broadcasted_iota(jnp.int32, (CHUNK, CHUNK), 1)
    return row, col


def _cumsum_rows(x, reverse=False):
    row, col = _chunk_geometry()
    tri = (col >= row) if reverse else (col <= row)
    return lax.dot_general(tri.astype(F32), x, ((NN), ((), ())), precision=HI, preferred_element_type=F32)


def _rep_sub(x4, sub):
    k = x4.shape[-1]
    return jnp.broadcast_to(x4[:, None, :], (CHUNK // sub, sub, k)).reshape(CHUNK, k)


def _gates(q_r, f_r, lb):
    sg = _sigmoid(f_r)
    f = lb + (1.0 - lb) * sg
    sq = _sigmoid(q_r)
    return sg, f, sq, q_r * sq


def _offdiag_terms(b, j, sub):
    c = b[j * sub + sub - 1:j * sub + sub, :]
    return jnp.exp(jnp.minimum(b - c, 0.0)), jnp.exp(jnp.minimum(c - b, 0.0))


def _store_heads(ref, x):
    for j in range(ref.shape[0]):
        ref[j] = x[:, _head(j)]


def _sub_rows(ref, r, sub):
    rows = [ref[j, pl.ds(r, CHUNK // sub, stride=sub), :] for j in range(ref.shape[0])]
    return _rep_sub(jnp.concatenate(rows, axis=1), sub)


def _diag_mask(sub):
    row, col = _chunk_geometry()
    return jnp.logical_and((row // sub) == (col // sub), row >= col)


HGRN_HEADS_PER_STEP = 8


def _wide(refs):
    return jnp.concatenate([r[...] for r in refs], axis=1)


def _head(j):
    return slice(j * RNN_HEAD_DIM, (j + 1) * RNN_HEAD_DIM)


def _cat_heads(parts, hs):
    return jnp.concatenate([p[:, hs] for p in parts], axis=1)


def _offdiag_factors(q, k, b, sub):
    rowi = lax.broadcasted_iota(jnp.int32, b.shape, 0)
    qs, ks, ers, ecs = [], [], [], []
    for j in range(CHUNK // sub - 1):
        e_row, e_col = _offdiag_terms(b, j, sub)
        e_row = jnp.where(rowi >= (j + 1) * sub, e_row, 0.0)
        e_col = jnp.where((rowi // sub) == j, e_col, 0.0)
        qs.append(q * e_row)
        ks.append(k * e_col)
        ers.append(e_row)
        ecs.append(e_col)
    return qs, ks, ers, ecs


def hgrn_fwd(proj, attn_n, lb, norm_gain, col0, rw, carry=None):
    t, aw = attn_n.shape
    nh = rw // RNN_HEAD_DIM
    nc = t // CHUNK
    kd = RNN_HEAD_DIM
    cb = col0 // kd
    sub = SUB_FWD
    nsub = CHUNK // sub
    hp = nh
    assert nh <= HGRN_HEADS_PER_STEP
    w = hp * kd

    def body(*refs):
        q_refs, f_refs, i_refs, g_refs = (refs[i * hp:(i + 1) * hp] for i in range(4))
        lb_ref, ng_ref, an_ref, cat_ref, o_ref, att_ref, st_ref, state, b_ref, k_ref = refs[4 * hp:]
        c = pl.program_id(1)

        @pl.when(c == 0)
        def _():
            state[...] = jnp.zeros_like(state)

        st_ref[...] = state[...]
        q_r, f_r, v, g_r = (_wide(rs) for rs in (q_refs, f_refs, i_refs, g_refs))
        _, f, _, q = _gates(q_r, f_r, lb_ref[...])
        k = 1.0 - f
        b = _cumsum_rows(jnp.log(f))
        _store_heads(b_ref, b)
        _store_heads(k_ref, k)
        qcat, kcat, _, _ = _offdiag_factors(q, k, b, sub)
        row, col = _chunk_geometry()
        same = (row // sub) == (col // sub)
        rloc = lax.broadcasted_iota(jnp.int32, (CHUNK, w), 0) % sub
        diag = [jnp.zeros((CHUNK, CHUNK), F32)] * hp
        for r in range(sub):
            bs = _sub_rows(b_ref, r, sub)
            ks = _sub_rows(k_ref, r, sub)
            prod = q * jnp.exp(jnp.where(rloc >= r, b - bs, -jnp.inf)) * ks
            place = jnp.logical_and((col % sub) == r, same)
            diag = [jnp.where(place, jnp.sum(prod[:, _head(j)], axis=-1, keepdims=True), diag[j]) for j in range(hp)]
        b_last = b[CHUNK - 1:CHUNK, :]
        qe = q * jnp.exp(b)
        kdec = k * jnp.exp(b_last - b)
        decay = jnp.exp(b_last)
        outs, normed, states = [], [], []
        for j in range(hp):
            hs = _head(j)
            att = diag[j] + _dot(_cat_heads(qcat, hs), _cat_heads(kcat, hs), NT)
            att_ref[j] = att
            sj = state[j]
            o = _dot(qe[:, hs], sj, NT) + _dot(att, v[:, hs], NN)
            outs.append(o)
            normed.append(o * _rstd(o))
            states.append(sj * decay[:, hs] + _dot(v[:, hs], kdec[:, hs], TN))
        for j in range(hp):
            state[j] = states[j]
        o_ref[...] = jnp.concatenate(outs, axis=1)
        gate = g_r * _sigmoid(g_r)
        cat_ref[:, :aw] = an_ref[...]
        cat_ref[:, aw:] = (jnp.concatenate(normed, axis=1) * jnp.tile(ng_ref[...], (1, hp)) * gate).astype(BF16)

    def col(kidx, j):
        return pl.BlockSpec((CHUNK, kd), lambda hg, c: (c, cb + kidx * nh + hg * hp + j))

    return _call(body, name="hgrn_fwd", grid=(1, nc),
                 in_specs=[col(kidx, j) for kidx in range(4) for j in range(hp)] +
                          [pl.BlockSpec((1, w), lambda hg, c: (0, hg)), pl.BlockSpec((1, kd), lambda hg, c: (0, 0)),
                           pl.BlockSpec((CHUNK, aw), lambda hg, c: (c, 0))],
                 out_specs=[pl.BlockSpec((CHUNK, aw + w), lambda hg, c: (c, 0)),
                            pl.BlockSpec((CHUNK, w), lambda hg, c: (c, hg)),
                            pl.BlockSpec((hp, CHUNK, CHUNK), lambda hg, c: (hg, c, 0)),
                            pl.BlockSpec((None, hp, kd, kd), lambda hg, c: (c, hg, 0, 0))],
                 out_shape=[jax.ShapeDtypeStruct((t, aw + rw), BF16), jax.ShapeDtypeStruct((t, rw), F32),
                            jax.ShapeDtypeStruct((nh, t, CHUNK), F32), jax.ShapeDtypeStruct((nc, nh, kd, kd), F32)],
                 args=(*([proj] * (4 * hp)), lb, norm_gain, attn_n),
                 scratch_shapes=[pltpu.VMEM((hp, kd, kd), F32), pltpu.VMEM((hp, CHUNK, kd), F32),
                                 pltpu.VMEM((hp, CHUNK, kd), F32)],
                 sem=("parallel", "arbitrary"), carry=carry)


def hgrn_bwd(proj, lb, norm_gain, o_all, att_all, st_all, dcat, dq_a, dk_a, dv_a, col0, rw, carry=None):
    t, iw = proj.shape
    aw, kw = dq_a.shape[1], dk_a.shape[1]
    nh = rw // RNN_HEAD_DIM
    nc = t // CHUNK
    kd = RNN_HEAD_DIM
    cb = col0 // kd
    sub = SUB_BWD
    nsub = CHUNK // sub
    dcb = (dcat.shape[1] - rw) // kd
    hp = nh
    assert nh <= HGRN_HEADS_PER_STEP and dcb % hp == 0 and col0 == aw + 2 * kw and iw == col0 + 4 * rw
    w = hp * kd

    def per_head(x, fn):
        return jnp.concatenate([jnp.broadcast_to(fn(x[:, _head(j)]), (CHUNK, kd)) for j in range(hp)], axis=1)

    def body(*refs):
        q_refs, f_refs, i_refs, g_refs = (refs[i * hp:(i + 1) * hp] for i in range(4))
        (lb_ref, ng_ref, o_ref, att_ref, st0_ref, st1_ref, d_ref, dqa_ref, dka_ref, dva_ref, dp_ref, dlb_ref, dng_ref,
         dstate, b_ref, k_ref, dks_ref) = refs[4 * hp:]
        ci = pl.program_id(1)

        @pl.when(ci == 0)
        def _():
            dstate[...] = jnp.zeros_like(dstate)
            dlb_ref[...] = jnp.zeros_like(dlb_ref)
            dng_ref[...] = jnp.zeros_like(dng_ref)

        lbv = lb_ref[...]
        q_r, f_r, v, g_r = (_wide(rs) for rs in (q_refs, f_refs, i_refs, g_refs))
        sg, f, sq, q = _gates(q_r, f_r, lbv)
        k = 1.0 - f
        b = _cumsum_rows(jnp.log(f))
        _store_heads(b_ref, b)
        _store_heads(k_ref, k)
        row, col = _chunk_geometry()

        o = o_ref[...]
        ng = jnp.tile(ng_ref[...], (1, hp))
        sgg = _sigmoid(g_r)
        gate = g_r * sgg
        d_rnn = d_ref[...]
        r = per_head(o, _rstd)
        oh = o * r
        dp_ref[:, :aw] = dqa_ref[...]
        dp_ref[:, aw:aw + kw] = dka_ref[...].astype(BF16)
        dp_ref[:, aw + kw:col0] = dva_ref[...].astype(BF16)
        dp_ref[:, col0 + 3 * rw:] = (d_rnn * oh * ng * (sgg * (1.0 + g_r * (1.0 - sgg)))).astype(BF16)
        d_on = d_rnn * gate
        dng_rows = jnp.sum(d_on * oh, axis=0, keepdims=True)
        dng = dng_rows[:, _head(0)]
        for j in range(1, hp):
            dng = dng + dng_rows[:, _head(j)]
        dng_ref[...] += dng
        dyg = d_on * ng
        do = r * (dyg - oh * per_head(dyg * oh, lambda x: jnp.mean(x, axis=-1, keepdims=True)))

        b_last = b[CHUNK - 1:CHUNK, :]
        eb = jnp.exp(b)
        tail = jnp.exp(b_last - b)
        kdec = k * tail
        decay = jnp.exp(b_last)
        qe = q * eb
        qcat, kcat, ers, ecs = _offdiag_factors(q, k, b, sub)
        diag_mask = _diag_mask(sub)
        dqs, dks, dvs, dads, gsums, dstates = [], [], [], [], [], []
        for j in range(hp):
            hs = _head(j)
            do_h, v_h, dst = do[:, hs], v[:, hs], dstate[j]
            da = jnp.where(row >= col, _dot(do_h, v_h, NT), 0.0)
            dads.append(jnp.where(diag_mask, da, 0.0))
            dq = _dot(do_h, st0_ref[j], NN) * eb[:, hs]
            dk = _dot(v_h, dst, NN) * tail[:, hs]
            dvs.append(_dot(att_ref[j], do_h, TN) + _dot(kdec[:, hs], dst, NT))
            rq = _dot(da, _cat_heads(kcat, hs), NN)
            rk = _dot(da, _cat_heads(qcat, hs), TN)
            for jj in range(nsub - 1):
                dq = dq + ers[jj][:, hs] * rq[:, _head(jj)]
                dk = dk + ecs[jj][:, hs] * rk[:, _head(jj)]
            dqs.append(dq)
            dks.append(dk)
            gsums.append(jnp.sum(dst * st1_ref[j], axis=0, keepdims=True))
            dstates.append(dst * decay[:, hs] + _dot(do_h, qe[:, hs], TN))
        for j in range(hp):
            dstate[j] = dstates[j]
        dq = jnp.concatenate(dqs, axis=1)
        dk = jnp.concatenate(dks, axis=1)
        rloc = lax.broadcasted_iota(jnp.int32, (CHUNK, w), 0) % sub
        for rr in range(sub):
            bs = _sub_rows(b_ref, rr, sub)
            ks = _sub_rows(k_ref, rr, sub)
            e = jnp.exp(jnp.where(rloc >= rr, b - bs, -jnp.inf))
            pick = (col % sub) == rr
            dacol = jnp.concatenate(
                [jnp.broadcast_to(jnp.sum(jnp.where(pick, dads[j], 0.0), axis=-1, keepdims=True), (CHUNK, kd))
                 for j in range(hp)], axis=1)
            wv = dacol * e
            dq = dq + wv * ks
            sums = jnp.sum((wv * q).reshape(nsub, sub, w), axis=1)
            for j in range(hp):
                dks_ref[j, pl.ds(rr, nsub, stride=sub), :] = sums[:, _head(j)]
        dk = dk + jnp.concatenate([dks_ref[j] for j in range(hp)], axis=1)

        dlf = _cumsum_rows(q * dq - k * dk, reverse=True) + jnp.concatenate(gsums, axis=1)
        dfv = dlf / f - dk
        dp_ref[:, col0 + rw:col0 + 2 * rw] = (dfv * (1.0 - lbv) * sg * (1.0 - sg)).astype(BF16)
        dlb_ref[...] += jnp.sum(dfv * (1.0 - sg), axis=0, keepdims=True)
        dp_ref[:, col0:col0 + rw] = (dq * (sq * (1.0 + q_r * (1.0 - sq)))).astype(BF16)
        dp_ref[:, col0 + 2 * rw:col0 + 3 * rw] = jnp.concatenate(dvs, axis=1).astype(BF16)

    def rev(c):
        return nc - 1 - c

    def col_in(kidx, j):
        return pl.BlockSpec((CHUNK, kd), lambda hg, c: (rev(c), cb + kidx * nh + hg * hp + j))

    def rows(width):
        return pl.BlockSpec((CHUNK, width), lambda hg, c: (rev(c), 0))

    return _call(body, name="hgrn_bwd", grid=(1, nc),
                 in_specs=[col_in(kidx, j) for kidx in range(4) for j in range(hp)] +
                          [pl.BlockSpec((1, w), lambda hg, c: (0, hg)), pl.BlockSpec((1, kd), lambda hg, c: (0, 0)),
                           rows(w),
                           pl.BlockSpec((hp, CHUNK, CHUNK), lambda hg, c: (hg, rev(c), 0)),
                           pl.BlockSpec((None, hp, kd, kd), lambda hg, c: (rev(c), hg, 0, 0)),
                           pl.BlockSpec((None, hp, kd, kd),
                                        lambda hg, c: (jnp.minimum(rev(c) + 1, nc - 1), hg, 0, 0)),
                           pl.BlockSpec((CHUNK, w), lambda hg, c: (rev(c), dcb // hp + hg)),
                           rows(aw), rows(kw), rows(kw)],
                 out_specs=[rows(iw),
                            pl.BlockSpec((1, w), lambda hg, c: (0, hg)),
                            pl.BlockSpec((None, 1, kd), lambda hg, c: (hg, 0, 0))],
                 out_shape=[jax.ShapeDtypeStruct((t, iw), BF16), jax.ShapeDtypeStruct((1, rw), F32),
                            jax.ShapeDtypeStruct((1, 1, kd), F32)],
                 args=(*([proj] * (4 * hp)), lb, norm_gain, o_all, att_all, st_all, st_all, dcat, dq_a, dk_a, dv_a),
                 scratch_shapes=[pltpu.VMEM((hp, kd, kd), F32), pltpu.VMEM((hp, CHUNK, kd), F32),
                                 pltpu.VMEM((hp, CHUNK, kd), F32), pltpu.VMEM((hp, CHUNK, kd), F32)],
                 sem=("parallel", "arbitrary"), carry=carry)


def comm_only(name, part):
    return _call(lambda: None, name=name, grid=(), in_specs=[], out_specs=[], out_shape=[], args=(), carry=part)[1]


ADD_BLOCK_ELEMS = 1 << 20
ADAMW_BLOCK_ELEMS = 1 << 19


def add_kept_half(name, kept, got, sel, minor, row0=0):
    pieces, rows, cols = got.shape
    tr = _tile(rows, max(16, ADD_BLOCK_ELEMS // cols), mult=16)
    assert row0 % tr == 0
    i0 = row0 // tr

    def body(sel_ref, k_ref, g_ref, o_ref):
        o_ref[...] = (k_ref[...].astype(F32) + g_ref[...].astype(F32)).astype(o_ref.dtype)

    kept_spec = (pl.BlockSpec((None, None, tr, cols), lambda p, i, s: (p, s[0], i + i0, 0)) if minor else
                 pl.BlockSpec((None, None, tr, cols), lambda p, i, s: (s[0], p, i + i0, 0)))
    return pl.pallas_call(
        body, name=name,
        grid_spec=pltpu.PrefetchScalarGridSpec(
            num_scalar_prefetch=1, grid=(pieces, rows // tr),
            in_specs=[kept_spec, pl.BlockSpec((None, tr, cols), lambda p, i, s: (p, i, 0))],
            out_specs=pl.BlockSpec((None, tr, cols), lambda p, i, s: (p, i, 0))),
        out_shape=jax.ShapeDtypeStruct(got.shape, got.dtype),
        compiler_params=_cparams(("parallel", "parallel")),
    )(sel, kept, got)


def _adamw(w, g, m, v):
    m = ADAM_B1 * m + (1.0 - ADAM_B1) * g
    v = ADAM_B2 * v + (1.0 - ADAM_B2) * (g * g)
    m_hat = m / (1.0 - ADAM_B1 ** ADAM_STEP)
    v_hat = v / (1.0 - ADAM_B2 ** ADAM_STEP)
    delta = -ADAM_LR * (m_hat / (jnp.sqrt(v_hat) + ADAM_EPS) + ADAM_WD * w)
    return delta, m, v


def add_adamw(name, kept, got, sel, w, m, v, row0=0, into=None):
    _, rows, cols = got.shape
    tr = _tile(rows, max(16, ADAMW_BLOCK_ELEMS // cols), mult=16)
    assert row0 % tr == 0
    i0 = row0 // tr
    n_into = 0 if into is None else len(into)

    def body(sel_ref, k_ref, g_ref, w_ref, m_ref, v_ref, *rest):
        go_ref, d_ref, mo_ref, vo_ref = rest[n_into:]
        g = k_ref[...].astype(F32) + g_ref[...].astype(F32)
        go_ref[...] = g
        d_ref[...], mo_ref[...], vo_ref[...] = _adamw(w_ref[...], g, m_ref[...], v_ref[...])

    shard_tile = pl.BlockSpec((tr, cols), lambda i, s: (i + i0, 0))
    return pl.pallas_call(
        body, name=name,
        grid_spec=pltpu.PrefetchScalarGridSpec(
            num_scalar_prefetch=1, grid=(rows // tr,),
            in_specs=[pl.BlockSpec((None, None, tr, cols), lambda i, s: (s[0], 0, i, 0)),
                      pl.BlockSpec((None, tr, cols), lambda i, s: (0, i, 0)), shard_tile, shard_tile, shard_tile,
                      *[ANY] * n_into],
            out_specs=[shard_tile] * 4),
        out_shape=[jax.ShapeDtypeStruct(w.shape, F32)] * 4,
        input_output_aliases={6 + k: k for k in range(n_into)},
        compiler_params=_cparams(("parallel",)),
    )(sel, kept, got, w, m, v, *(into or ()))


def small_allreduce_adamw(partial, scale, w, m, v):
    rows = partial.shape[0]

    def body(p_ref, s_ref, w_ref, m_ref, v_ref, g_ref, d_ref, mo_ref, vo_ref, slots, send_sems, recv_sems):
        x, y, c = _coords()
        my_slot = _slab_index((x, y, c))
        slots[my_slot] = p_ref[...]
        copies = []
        for mask in range(1, N_DEV):
            to = tuple(1 - v_ if (mask >> s_) & 1 else v_ for v_, s_ in ((x, 2), (y, 1), (c, 0)))
            copies.append(pltpu.make_async_remote_copy(
                src_ref=p_ref, dst_ref=slots.at[my_slot],
                send_sem=send_sems.at[mask - 1], recv_sem=recv_sems.at[mask - 1],
                device_id=to, device_id_type=MESH))
        for cp in copies:
            cp.start()
        for cp in copies:
            cp.wait()
        total = slots[0]
        for b in range(1, N_DEV):
            total = total + slots[b]
        g = total * s_ref[...]
        g_ref[...] = g
        d_ref[...], mo_ref[...], vo_ref[...] = _adamw(w_ref[...], g, m_ref[...], v_ref[...])

    vm = pl.BlockSpec(memory_space=pltpu.VMEM)
    return pl.pallas_call(
        body, name="small_allreduce_adamw",
        in_specs=[vm] * 5, out_specs=[vm] * 4,
        out_shape=[jax.ShapeDtypeStruct((rows, LANES), F32)] * 4,
        scratch_shapes=[pltpu.VMEM((N_DEV, rows, LANES), F32),
                        pltpu.SemaphoreType.DMA((N_DEV - 1,)), pltpu.SemaphoreType.DMA((N_DEV - 1,))],
        compiler_params=pltpu.CompilerParams(has_side_effects=True),
    )(partial, scale, w, m, v)


_SMALL = ("attn_sinks", "attn_out_gain", "rnn_lb_logits", "rnn_norm_gain", "mix_pre_gain", "mix_post_gain",
          "mlp_pre_gain", "mlp_post_gain")


def _pack(parts):
    rows = []
    for p in parts:
        flat = p.reshape(-1).astype(F32)
        pad = (-flat.shape[0]) % LANES
        rows.append(jnp.pad(flat, (0, pad)).reshape(-1, LANES))
    packed = jnp.concatenate(rows, axis=0)
    pad_rows = (-packed.shape[0]) % 8
    return jnp.pad(packed, ((0, pad_rows), (0, 0)))


def _unpack(packed, shapes):
    out, r = [], 0
    for s in shapes:
        size = math.prod(s)
        nrows = -(-size // LANES)
        out.append(packed[r:r + nrows].reshape(-1)[:size].reshape(s))
        r += nrows
    return out


class _Scatter:
    def __init__(self, tag, grad, sels, both_links=False):
        self.tag, self.sels, self.both = tag, sels, both_links
        self.shape = grad.shape[1:]
        self.half = self.shape[0] // 2
        self.cur = grad.reshape(4, 2, *self.shape)
        self.stage = 0

    def step(self):
        if self.stage == 0 or not self.both:
            return _scatter_step(self.cur, "cxy"[self.stage])
        if self.stage == 1:
            return _merge(_scatter_step(self.cur, "x", rows=(0, self.half)),
                          _scatter_step(self.cur, "y", minor=True, rows=(self.half, self.shape[0])))
        upper, lower = self.cur
        return _merge(_scatter_step(upper, "y"), _scatter_step(lower, "x"))

    def land(self, got, w=None, m=None, v=None):
        stage, tag, sels = self.stage, self.tag, self.sels
        self.stage += 1
        if stage == 0 or not self.both:
            axis = "cxy"[stage]
            name = "rs_add_%s_%s" % (axis, tag)
            if axis == "y":
                return add_adamw(name, self.cur, got, sels[axis], w, m, v)
            summed = add_kept_half(name, self.cur, got, sels[axis], minor=axis == "c")
            self.cur = summed.reshape(2, summed.shape[0] // 2, *self.shape)
            return None
        got_upper, got_lower = got
        if stage == 1:
            upper = add_kept_half("rs_add_x_%s_upper" % tag, self.cur, got_upper, sels["x"], minor=False)
            lower = add_kept_half("rs_add_y_%s_lower" % tag, self.cur, got_lower, sels["y"], minor=True,
                                  row0=self.half)
            self.cur = tuple(s.reshape(2, 1, *s.shape[1:]) for s in (upper, lower))
            return None
        upper, lower = self.cur
        out_upper = add_adamw("rs_add_y_%s_upper" % tag, upper, got_upper, sels["y"], w, m, v)
        return add_adamw("rs_add_x_%s_lower" % tag, lower, got_lower, sels["x"], w, m, v, row0=self.half,
                         into=out_upper)


def kernel(x, w_in, attn_sinks, attn_out_gain, rnn_lb_logits, rnn_norm_gain, w_out, mix_pre_gain, mix_post_gain, mlp_pre_gain, mlp_post_gain, w_up, w_down, loss_target, m_w_in, m_attn_sinks, m_attn_out_gain, m_rnn_lb_logits, m_rnn_norm_gain, m_w_out, m_mix_pre_gain, m_mix_post_gain, m_mlp_pre_gain, m_mlp_post_gain, m_w_up, m_w_down, v_w_in, v_attn_sinks, v_attn_out_gain, v_rnn_lb_logits, v_rnn_norm_gain, v_w_out, v_mix_pre_gain, v_mix_post_gain, v_mlp_pre_gain, v_mlp_post_gain, v_w_up, v_w_down):
    xs, target = x[0], loss_target[0]
    t, d = xs.shape
    aw = d // 2
    rw = d - aw
    col0 = aw + 2 * N_KV_HEADS * HEAD_DIM
    small_w = dict(attn_sinks=attn_sinks, attn_out_gain=attn_out_gain, rnn_lb_logits=rnn_lb_logits,
                   rnn_norm_gain=rnn_norm_gain, mix_pre_gain=mix_pre_gain, mix_post_gain=mix_post_gain,
                   mlp_pre_gain=mlp_pre_gain, mlp_post_gain=mlp_post_gain)
    small_m = dict(attn_sinks=m_attn_sinks, attn_out_gain=m_attn_out_gain, rnn_lb_logits=m_rnn_lb_logits,
                   rnn_norm_gain=m_rnn_norm_gain, mix_pre_gain=m_mix_pre_gain, mix_post_gain=m_mix_post_gain,
                   mlp_pre_gain=m_mlp_pre_gain, mlp_post_gain=m_mlp_post_gain)
    small_v = dict(attn_sinks=v_attn_sinks, attn_out_gain=v_attn_out_gain, rnn_lb_logits=v_rnn_lb_logits,
                   rnn_norm_gain=v_rnn_norm_gain, mix_pre_gain=v_mix_pre_gain, mix_post_gain=v_mix_post_gain,
                   mlp_pre_gain=v_mlp_pre_gain, mlp_post_gain=v_mlp_post_gain)
    cx, cy, cc = _coords()
    sels = {a: jnp.reshape(v_, (1,)).astype(jnp.int32) for a, v_ in (("x", cx), ("y", cy), ("c", cc))}

    w_in_t, m_in_t, v_in_t = w_in[0].T, m_w_in[0].T, v_w_in[0].T
    s_in, s_out, s_up, s_down = (w.astype(BF16) for w in (w_in_t, w_out[0], w_up[0], w_down[0]))
    probs = jax.nn.softmax(rnn_lb_logits.astype(F32), axis=0)
    lb = probs[0:1]

    (h1,), (wint_part,) = pre_norm(xs, mix_pre_gain, carry=_gather_first(s_in, diagonal=False))
    in_rows = s_in.shape[0]
    wint = comm_only("gather_rest_w_in", _pass_slabs(
        wint_part,
        [(_X, _Y, (0, in_rows // 2)), (_Y, _X, (in_rows // 2, in_rows)), (_X, _C, None), (_Y, _C, None)],
        then=[(_XY, _C, None)]))[0].reshape(-1, d)
    up_rows, down_rows = s_up.shape[0], s_down.shape[0]
    up_cut, down_cut = up_rows * 5 // 16, down_rows * 3 // 4
    proj, (wout_half, wup_part) = mm_nt(
        "in_proj", h1, wint, F32, carry=_merge(_gather_first(s_out), _gather_first(s_up, rows=(0, up_cut))))
    (attn_o, attn_n), (wout, wup_half) = attn_fwd(
        proj, attn_sinks, attn_out_gain, aw,
        carry=_merge(_gather_second(wout_half), _gather_first(s_up, rows=(up_cut, up_rows), into=wup_part)))
    (cat, o_r, att, st), (wup, wdown_part) = hgrn_fwd(
        proj, attn_n, lb, rnn_norm_gain, col0, rw,
        carry=_merge(_gather_second(wup_half), _gather_first(s_down, rows=(0, down_cut))))
    wout = wout.reshape(-1, d)
    mixed, (wdown_half,) = mm_nn("out_proj", cat, wout, F32,
                                 carry=_gather_first(s_down, rows=(down_cut, down_rows), into=wdown_part))
    x1, h2 = mid_fwd(mixed, mix_post_gain, xs, mlp_pre_gain)
    u, (wdown,) = up_proj(h2, wup, carry=_gather_second(wdown_half))
    wdown = wdown.reshape(-1, d)
    y = down_proj(u, wdown)
    sse, dout, dy, dg_mlppost = loss_bwd(y, mlp_post_gain, x1, target)

    du = down_bwd_act(dy, wdown, u)
    rs_down = _Scatter("down", down_wgrad(u, dy).reshape(N_DEV, -1, d), sels, both_links=True)
    dh2, (got,) = up_bwd_x(du, wup, carry=rs_down.step())
    rs_down.land(got)
    dwup, gots = up_wgrad(h2, du, carry=rs_down.step())
    rs_down.land(gots)
    rs_up = _Scatter("up", dwup, sels, both_links=True)
    (dx1, dmixed, dg_mlppre, dg_mixpost), (got,) = mid_bwd(dh2, x1, mlp_pre_gain, dout, mixed, mix_post_gain,
                                                          carry=rs_up.step())
    rs_up.land(got)
    dcat = mm_nt("out_bwd_x", dmixed, wout, F32)
    rs_out = _Scatter("out", mm_tn("out_wgrad", cat, dmixed, BF16).reshape(N_DEV, -1, d), sels)
    (dq_a, dk_a, dv_a, dsinks, daog), (*gots, got_o) = attn_bwd(
        proj, attn_sinks, attn_out_gain, attn_o, dcat, aw, carry=_merge(rs_down.step(), rs_out.step()))
    out_down = rs_down.land(gots, w_down[0], m_w_down[0], v_w_down[0])
    rs_out.land(got_o)
    (dproj, dlb, dng), (*gots, got_o) = hgrn_bwd(
        proj, lb, rnn_norm_gain, o_r, att, st, dcat, dq_a, dk_a, dv_a, col0, rw,
        carry=_merge(rs_up.step(), rs_out.step()))
    rs_up.land(gots)
    rs_out.land(got_o)
    dwin, (*gots, got_o) = mm_tn("in_wgrad", dproj, h1, BF16, carry=_merge(rs_up.step(), rs_out.step()))
    out_up = rs_up.land(gots, w_up[0], m_w_up[0], v_w_up[0])
    out_out = rs_out.land(got_o, w_out[0], m_w_out[0], v_w_out[0])
    rs_in = _Scatter("in", dwin.reshape(N_DEV, -1, d), sels, both_links=True)
    rs_in.land(comm_only("rs_exchange_c_in", rs_in.step())[0])
    dh1, gots = mm_nn("in_bwd_x", dproj, wint, F32, tm=MM_TILE // 2, carry=rs_in.step())
    rs_in.land(gots)
    grad_x, dg_mixpre = first_bwd(dh1, xs, mix_pre_gain, dx1)
    out_in = rs_in.land(comm_only("rs_exchange_last_in", rs_in.step()), w_in_t, m_in_t, v_in_t)
    big_out = [out_in, out_out, out_up, out_down]

    n_heads = attn_sinks.shape[1]
    jac = probs[0] * probs[1]
    partial = _pack([sse, dsinks[0, :n_heads], daog, jnp.stack([dlb[0], dlb[0]]), jnp.sum(dng, axis=0),
                     dg_mixpre, dg_mixpost, dg_mlppre, dg_mlppost])
    ones = [jnp.ones(small_w[k].shape, F32) for k in _SMALL]
    ones[2] = jnp.stack([jac, -jac])
    scale = _pack([jnp.full((1,), 0.5 / d, F32)] + ones)
    zero = jnp.zeros((1,), F32)
    outs = small_allreduce_adamw(partial, scale, _pack([zero] + [small_w[k] for k in _SMALL]),
                                 _pack([zero] + [small_m[k] for k in _SMALL]),
                                 _pack([jnp.ones((1,), F32)] + [small_v[k] for k in _SMALL]))
    shapes = [(1,)] + [small_w[k].shape for k in _SMALL]
    sgrad, sdelta, snm, snv = (_unpack(o, shapes) for o in outs)
    loss = sgrad[0][0]

    def big(i, j):
        o = big_out[i][j]
        return (o.T if i == 0 else o)[None]

    def ordered(j, smalls):
        s = dict(zip(_SMALL, smalls[1:]))
        return [big(0, j), s["attn_sinks"], s["attn_out_gain"], s["rnn_lb_logits"], s["rnn_norm_gain"], big(1, j),
                s["mix_pre_gain"], s["mix_post_gain"], s["mlp_pre_gain"], s["mlp_post_gain"], big(2, j), big(3, j)]

    return (loss, grad_x[None], *ordered(0, sgrad), *ordered(1, sdelta), *ordered(2, snm), *ordered(3, snv))
```

```python
import math

import jax
import jax.numpy as jnp
from jax import lax
from jax.experimental import pallas as pl
from jax.experimental.pallas import tpu as pltpu

F32 = jnp.float32
BF16 = jnp.bfloat16

HEAD_DIM = 64
N_KV_HEADS = 2
BLOCK = 128
RNN_HEAD_DIM = 128
CHUNK = 64
SUB_FWD = 16
SUB_BWD = 16
EPS = 1e-6

ADAM_LR = 0.001
ADAM_B1 = 0.9
ADAM_B2 = 0.999
ADAM_EPS = 1e-08
ADAM_WD = 0.01
ADAM_STEP = 10

N_DEV = 8
LANES = 128
V7X_VMEM_LIMIT = 56 * 1024 * 1024
MESH = pl.DeviceIdType.MESH
HI = lax.Precision.HIGHEST
ANY = pl.BlockSpec(memory_space=pl.ANY)
_AXES = ("x", "y", "c")


def _cparams(sem=None, **kw):
    return pltpu.CompilerParams(dimension_semantics=sem, vmem_limit_bytes=V7X_VMEM_LIMIT, **kw)


def _dot(a, b, dims):
    return lax.dot_general(a.astype(BF16), b.astype(BF16), (dims, ((), ())), preferred_element_type=F32)


NN = ((1,), (0,))
NT = ((1,), (1,))
TN = ((0,), (0,))


def _pick(n, pref):
    t = min(n, pref)
    while n % t:
        t //= 2
    return t


def _tile(n, pref, mult=LANES):
    if n <= pref:
        return n
    t = pref - pref % mult
    while n % t:
        t -= mult
    return t


def _coords():
    return lax.axis_index("x"), lax.axis_index("y"), lax.axis_index("c")


def _slab_index(dev):
    return 4 * dev[0] + 2 * dev[1] + dev[2]


class _Part:
    def __init__(self, operands, landings, aliases, n_sems, plan):
        self.operands, self.landings, self.aliases, self.n_sems, self.plan = operands, landings, aliases, n_sems, plan


def _merge(*parts):
    operands, landings, aliases, plans = [], [], {}, []
    s0 = 0
    for p in parts:
        o0, l0 = len(operands), len(landings)
        aliases.update({o0 + i: l0 + j for i, j in p.aliases.items()})
        plans.append((p.plan, o0, len(p.operands), l0, len(p.landings), s0))
        operands += p.operands
        landings += p.landings
        s0 += p.n_sems

    def plan(ops, lands, sem):
        starts, waits = [], []
        for f, o0, no, l0, nl, off in plans:
            s, w = f(ops[o0:o0 + no], lands[l0:l0 + nl], lambda kind, k, off=off: sem(kind, off + k))
            starts += s
            waits += w
        return starts, waits

    return _Part(operands, landings, aliases, s0, plan)


def _gather_peers(x, y, c):
    return [(x, y, 1 - c), (1 - x, y, c), (x, 1 - y, c), (1 - x, 1 - y, c)]


def _gather_first(shard, rows=None, into=None, diagonal=True):
    lo, hi = (0, shard.shape[0]) if rows is None else rows
    n_peers = 4 if diagonal else 3

    def plan(ops, lands, sem):
        x, y, c = _coords()
        me, peers = (x, y, c), _gather_peers(x, y, c)[:n_peers]
        src = ops[0].at[pl.ds(lo, hi - lo)]

        def slab(block):
            return lands[0].at[_slab_index(block), pl.ds(lo, hi - lo)]

        def cp(k, block, to):
            return pltpu.make_async_remote_copy(
                src_ref=src, dst_ref=slab(block),
                send_sem=sem(0, k), recv_sem=sem(1, k), device_id=to, device_id_type=MESH)

        local = pltpu.make_async_copy(src, slab(me), sem(2, 0))
        sends = [cp(k, me, to) for k, to in enumerate(peers)]
        recvs = [cp(k, frm, me) for k, frm in enumerate(peers)]
        return ([local.start] + [s.start for s in sends],
                [local.wait] + [s.wait_send for s in sends] + [r.wait_recv for r in recvs])

    landing = jax.ShapeDtypeStruct((N_DEV, *shard.shape), shard.dtype)
    if into is None:
        return _Part([shard], [landing], {}, 4, plan)
    return _Part([shard, into], [landing], {1: 0}, 4, plan)


def _flip(dev, flips):
    return tuple(1 - v if f else v for v, f in zip(dev, flips))


def _pass_slabs(gathered, moves, then=()):
    def wave(lands, sem, k0, wave_moves):
        me = _coords()
        sends, recvs = [], []
        for k, (block, dest, rows) in enumerate(wave_moves, start=k0):
            lo, hi = (0, gathered.shape[1]) if rows is None else rows

            def cp(blk, to, k=k, lo=lo, hi=hi):
                slab = lands[0].at[_slab_index(blk), pl.ds(lo, hi - lo)]
                return pltpu.make_async_remote_copy(
                    src_ref=slab, dst_ref=slab, send_sem=sem(0, k), recv_sem=sem(1, k),
                    device_id=to, device_id_type=MESH)

            sends.append(cp(_flip(me, block), _flip(me, dest)))
            recvs.append(cp(_flip(_flip(me, dest), block), me))
        return [s.start for s in sends], [s.wait_send for s in sends] + [r.wait_recv for r in recvs]

    def plan(ops, lands, sem):
        starts, waits = wave(lands, sem, 0, moves)
        if then:
            starts2, waits2 = wave(lands, sem, len(moves), then)
            waits = waits + starts2 + waits2
        return starts, waits

    return _Part([gathered], [jax.ShapeDtypeStruct(gathered.shape, gathered.dtype)], {0: 0},
                 len(moves) + len(then), plan)


_X, _Y, _C, _XY = (1, 0, 0), (0, 1, 0), (0, 0, 1), (1, 1, 0)


def _gather_second(gathered):
    def plan(ops, lands, sem):
        x, y, c = _coords()
        sibling = (x, y, 1 - c)
        chips = [(1 - x, y), (x, 1 - y), (1 - x, 1 - y)]

        def cp(k, block):
            slab = lands[0].at[_slab_index(block)]
            return pltpu.make_async_remote_copy(
                src_ref=slab, dst_ref=slab, send_sem=sem(0, k), recv_sem=sem(1, k),
                device_id=sibling, device_id_type=MESH)

        sends = [cp(k, (*chip, c)) for k, chip in enumerate(chips)]
        recvs = [cp(k, (*chip, 1 - c)) for k, chip in enumerate(chips)]
        return [s.start for s in sends], [s.wait_send for s in sends] + [r.wait_recv for r in recvs]

    return _Part([gathered], [jax.ShapeDtypeStruct(gathered.shape, gathered.dtype)], {0: 0}, 3, plan)


def _scatter_step(array, axis, minor=None, rows=None):
    minor = (axis == "c") if minor is None else minor
    pieces = array.shape[0] if minor else array.shape[1]
    lo, hi = (0, array.shape[2]) if rows is None else rows

    def plan(ops, lands, sem):
        coords = list(_coords())
        ai = _AXES.index(axis)
        mine = coords[ai]
        peer = list(coords)
        peer[ai] = 1 - mine
        cps = []
        for p in range(pieces):
            src = ops[0].at[p, 1 - mine, pl.ds(lo, hi - lo)] if minor else ops[0].at[1 - mine, p, pl.ds(lo, hi - lo)]
            cps.append(pltpu.make_async_remote_copy(
                src_ref=src, dst_ref=lands[0].at[p], send_sem=sem(0, p), recv_sem=sem(1, p),
                device_id=tuple(peer), device_id_type=MESH))
        return [cp.start for cp in cps], [cp.wait for cp in cps]

    return _Part([array], [jax.ShapeDtypeStruct((pieces, hi - lo, array.shape[3]), array.dtype)], {}, pieces, plan)


def _grid_edges(grid):
    first = last = None
    for ax, n in enumerate(grid):
        p = pl.program_id(ax)
        f, l = p == 0, p == n - 1
        first = f if first is None else jnp.logical_and(first, f)
        last = l if last is None else jnp.logical_and(last, l)
    return first, last


def _call(body, *, name, grid, in_specs, out_specs, out_shape, args, scratch_shapes=(), sem=None, carry=None):
    if carry is None:
        return pl.pallas_call(
            body, name=name, grid=grid, in_specs=list(in_specs), out_specs=list(out_specs),
            out_shape=list(out_shape), scratch_shapes=list(scratch_shapes), compiler_params=_cparams(sem),
        )(*args)
    n_in, n_out, n_scr = len(in_specs), len(out_specs), len(scratch_shapes)
    n_cin, n_cout = len(carry.operands), len(carry.landings)

    def wrapped(*refs):
        ins, cins = refs[:n_in], refs[n_in:n_in + n_cin]
        o0 = n_in + n_cin
        outs, couts = refs[o0:o0 + n_out], refs[o0 + n_out:o0 + n_out + n_cout]
        s0 = o0 + n_out + n_cout
        scr, sems = refs[s0:s0 + n_scr], refs[s0 + n_scr:]
        first, last = _grid_edges(grid)

        def plan():
            return carry.plan(cins, couts, lambda kind, k: sems[kind].at[k])

        def start_all():
            for start in plan()[0]:
                start()

        def wait_all():
            for wait in plan()[1]:
                wait()

        if grid:
            pl.when(first)(start_all)
            body(*ins, *outs, *scr)
            pl.when(last)(wait_all)
        else:
            start_all()
            body(*ins, *outs, *scr)
            wait_all()

    sem_arrays = [pltpu.SemaphoreType.DMA((carry.n_sems,))] * 3
    res = pl.pallas_call(
        wrapped, name=name, grid=grid,
        in_specs=[*in_specs, *[ANY] * n_cin], out_specs=[*out_specs, *[ANY] * n_cout],
        out_shape=[*out_shape, *carry.landings],
        scratch_shapes=[*scratch_shapes, *sem_arrays],
        input_output_aliases={n_in + i: n_out + j for i, j in carry.aliases.items()},
        compiler_params=_cparams(("arbitrary",) * len(grid) if grid else None, has_side_effects=True),
    )(*args, *carry.operands)
    return res[:n_out], res[n_out:]


MM_TILE = 1024
MM_K_TILE = 2048
MXU_COLS = 256
MM_VMEM_BUDGET = 50 * 1024 * 1024


def _matmul(name, a, b, dims, grid, a_spec, b_spec, out_shape, out_spec, epilogue,
            extras=(), extra_specs=(), prologue=None, carry=None):
    nk = grid[2]
    n_extra = len(extras)
    acc_shape = out_spec.block_shape[-2:]

    def lhs(a_ref):
        return a_ref[...] if prologue is None else prologue(a_ref[...])

    def body_one(a_ref, b_ref, *rest):
        epilogue(_dot(lhs(a_ref), b_ref[...], dims), rest[:n_extra], rest[n_extra:])

    def body_acc(a_ref, b_ref, *rest):
        acc = rest[-1]
        k = pl.program_id(2)
        part = _dot(lhs(a_ref), b_ref[...], dims)

        @pl.when(k == 0)
        def _():
            acc[...] = part

        @pl.when(k > 0)
        def _():
            acc[...] += part

        @pl.when(k == nk - 1)
        def _():
            epilogue(acc[...], rest[:n_extra], rest[n_extra:-1])

    res = _call(body_one if nk == 1 else body_acc, name=name, grid=grid,
                in_specs=[a_spec, b_spec, *extra_specs], out_specs=[out_spec], out_shape=[out_shape],
                args=(a, b, *extras), scratch_shapes=[] if nk == 1 else [pltpu.VMEM(acc_shape, F32)],
                sem=("parallel", "parallel", "arbitrary"), carry=carry)
    return res[0] if carry is None else (res[0][0], res[1])


def _store_as(acc, extra_refs, out_refs):
    out_refs[0][...] = acc.astype(out_refs[0].dtype)


def _square(u):
    return u * u


def mm_nn(name, a, b, out_dtype, tk=None, tm=MM_TILE, tn=MM_TILE, prologue=None, carry=None):
    (m, kk), n = a.shape, b.shape[1]
    tm, tn = _tile(m, tm), _tile(n, tn, mult=MXU_COLS)
    tk = kk if tk is None else _tile(kk, tk, mult=MXU_COLS)
    return _matmul(name, a, b, NN, (m // tm, n // tn, kk // tk),
                   pl.BlockSpec((tm, tk), lambda i, j, k: (i, k)),
                   pl.BlockSpec((tk, tn), lambda i, j, k: (k, j)),
                   jax.ShapeDtypeStruct((m, n), out_dtype),
                   pl.BlockSpec((tm, tn), lambda i, j, k: (i, j)), _store_as, prologue=prologue, carry=carry)


def mm_nt(name, a, b, out_dtype, epilogue=_store_as, extras=(), extra_specs=(), carry=None):
    (m, kk), n = a.shape, b.shape[0]
    tm, tn = _tile(m, MM_TILE), _tile(n, MM_TILE, mult=MXU_COLS)
    return _matmul(name, a, b, NT, (m // tm, n // tn, 1),
                   pl.BlockSpec((tm, kk), lambda i, j, k: (i, 0)),
                   pl.BlockSpec((tn, kk), lambda i, j, k: (j, 0)),
                   jax.ShapeDtypeStruct((m, n), out_dtype),
                   pl.BlockSpec((tm, tn), lambda i, j, k: (i, j)), epilogue,
                   extras=extras, extra_specs=extra_specs, carry=carry)


def _whole_k_fits(tm, tn, kk, out_dtype, prologue):
    operands = 2 * 2 * kk * (tm + tn)
    out = 2 * tm * tn * jnp.dtype(out_dtype).itemsize + 4 * tm * tn
    return operands + out + (2 * kk * tm if prologue is not None else 0) <= MM_VMEM_BUDGET


def mm_tn(name, a, b, out_dtype, prologue=None, carry=None):
    (kk, m), n = a.shape, b.shape[1]
    tm, tn = _tile(m, MM_TILE), _tile(n, MM_TILE)
    tk = kk if _whole_k_fits(tm, tn, kk, out_dtype, prologue) else _tile(kk, MM_K_TILE)
    return _matmul(name, a, b, TN, (m // tm, n // tn, kk // tk),
                   pl.BlockSpec((tk, tm), lambda i, j, k: (k, i)),
                   pl.BlockSpec((tk, tn), lambda i, j, k: (k, j)),
                   jax.ShapeDtypeStruct((m, n), out_dtype),
                   pl.BlockSpec((tm, tn), lambda i, j, k: (i, j)), _store_as, prologue=prologue, carry=carry)


def up_proj(h2, wup_slabs):
    (m, kk), (_, _, ns) = h2.shape, wup_slabs.shape
    tm, tn = _tile(m, MM_TILE), _tile(ns, MM_TILE)
    r = ns // tn
    n = N_DEV * ns

    def epi(acc, extra_refs, out_refs):
        out_refs[0][...] = jnp.maximum(acc, 0.0).astype(BF16)

    return _matmul("up_proj", h2, wup_slabs, NN, (m // tm, n // tn, 1),
                   pl.BlockSpec((tm, kk), lambda i, j, k: (i, 0)),
                   pl.BlockSpec((None, kk, tn), lambda i, j, k: (j // r, 0, j % r)),
                   jax.ShapeDtypeStruct((m, n), BF16),
                   pl.BlockSpec((tm, tn), lambda i, j, k: (i, j)), epi)


def down_proj(u, wdown):
    return mm_nn("down_proj", u, wdown, F32, tn=MM_TILE // 2, tk=2 * MM_K_TILE, prologue=_square)


def down_bwd_act(dy, wdown, u):
    tm, tn = _tile(dy.shape[0], MM_TILE), _tile(wdown.shape[0], MM_TILE)

    def epi(acc, extra_refs, out_refs):
        out_refs[0][...] = (acc * (2.0 * extra_refs[0][...].astype(F32))).astype(BF16)

    return mm_nt("down_bwd_act", dy, wdown, BF16, epilogue=epi, extras=(u,),
                 extra_specs=(pl.BlockSpec((tm, tn), lambda i, j, k: (i, j)),))


def down_wgrad(u, dy):
    return mm_tn("down_wgrad", u, dy, BF16, prologue=_square)


def up_bwd_x(du, wup_slabs, carry=None):
    (m, kk), (slabs, n, ns) = du.shape, wup_slabs.shape
    tm, tn = _tile(m, MM_TILE), _tile(n, MM_TILE // 2, mult=MXU_COLS)
    per_step = slabs // 2

    def body(a_ref, b_ref, o_ref, acc):
        k = pl.program_id(2)
        part = _dot(a_ref[:, :ns], b_ref[0], NT)
        for s in range(1, per_step):
            part = part + _dot(a_ref[:, s * ns:(s + 1) * ns], b_ref[s], NT)

        @pl.when(k == 0)
        def _():
            acc[...] = part

        @pl.when(k == 1)
        def _():
            o_ref[...] = acc[...] + part

    res = _call(body, name="up_bwd_x", grid=(m // tm, n // tn, 2),
                in_specs=[pl.BlockSpec((tm, per_step * ns), lambda i, j, k: (i, k)),
                          pl.BlockSpec((per_step, tn, ns), lambda i, j, k: (k, j, 0))],
                out_specs=[pl.BlockSpec((tm, tn), lambda i, j, k: (i, j))],
                out_shape=[jax.ShapeDtypeStruct((m, n), F32)], args=(du, wup_slabs),
                scratch_shapes=[pltpu.VMEM((tm, tn), F32)],
                sem=("parallel", "parallel", "arbitrary"), carry=carry)
    return res[0] if carry is None else (res[0][0], res[1])


def up_wgrad(h2, du, carry=None):
    (kk, m), n = h2.shape, du.shape[1]
    ns = n // N_DEV
    tm, tn = _tile(m, MM_TILE), _tile(ns, MM_TILE)
    tk = kk if _whole_k_fits(tm, tn, kk, BF16, None) else _tile(kk, MM_K_TILE)
    r = ns // tn
    return _matmul("up_wgrad", h2, du, TN, (m // tm, n // tn, kk // tk),
                   pl.BlockSpec((tk, tm), lambda i, j, k: (k, i)),
                   pl.BlockSpec((tk, tn), lambda i, j, k: (k, j)),
                   jax.ShapeDtypeStruct((N_DEV, m, ns), BF16),
                   pl.BlockSpec((None, tm, tn), lambda i, j, k: (j // r, i, j % r)), _store_as, carry=carry)


def _rstd(x):
    return lax.rsqrt(jnp.mean(x * x, axis=-1, keepdims=True) + EPS)


def _norm_bwd(x, g, dy):
    r = _rstd(x)
    xh = x * r
    dyg = dy * g
    dx = r * (dyg - xh * jnp.mean(dyg * xh, axis=-1, keepdims=True))
    return dx, jnp.sum(dy * xh, axis=0, keepdims=True)


def _row_spec(tr, d):
    return pl.BlockSpec((tr, d), lambda i: (i, 0))


def _vec_spec(d):
    return pl.BlockSpec((1, d), lambda i: (0, 0))


def _accum(ref, val):
    @pl.when(pl.program_id(0) == 0)
    def _():
        ref[...] = jnp.zeros_like(ref)

    ref[...] += val


def pre_norm(x, g, carry=None, tr=256):
    t, d = x.shape
    tr = _pick(t, tr)

    def body(x_ref, g_ref, h_ref):
        xx = x_ref[...]
        h_ref[...] = (xx * _rstd(xx) * g_ref[...]).astype(BF16)

    return _call(body, name="pre_norm", grid=(t // tr,),
                 in_specs=[_row_spec(tr, d), _vec_spec(d)], out_specs=[_row_spec(tr, d)],
                 out_shape=[jax.ShapeDtypeStruct((t, d), BF16)], args=(x, g), sem=("parallel",), carry=carry)


def mid_fwd(mixed, g_post, x, g_pre2, tr=256):
    t, d = x.shape
    tr = _pick(t, tr)

    def body(m_ref, gp_ref, x_ref, g2_ref, x1_ref, h2_ref):
        mm = m_ref[...]
        x1 = x_ref[...] + mm * _rstd(mm) * gp_ref[...]
        x1_ref[...] = x1
        h2_ref[...] = (x1 * _rstd(x1) * g2_ref[...]).astype(BF16)

    return _call(body, name="mid_fwd", grid=(t // tr,),
                 in_specs=[_row_spec(tr, d), _vec_spec(d), _row_spec(tr, d), _vec_spec(d)],
                 out_specs=[_row_spec(tr, d), _row_spec(tr, d)],
                 out_shape=[jax.ShapeDtypeStruct((t, d), F32), jax.ShapeDtypeStruct((t, d), BF16)],
                 args=(mixed, g_post, x, g_pre2), sem=("parallel",))


def loss_bwd(y, g_post2, x1, target, tr=256):
    t, d = y.shape
    tr = _pick(t, tr)

    def body(y_ref, g_ref, x1_ref, t_ref, sse_ref, dout_ref, dy_ref, dg_ref):
        yy = y_ref[...]
        g = g_ref[...]
        err = x1_ref[...] + yy * _rstd(yy) * g - t_ref[...]
        _accum(sse_ref, jnp.sum(jnp.sum(err * err, axis=1, keepdims=True), axis=0, keepdims=True))
        dout = err * (1.0 / d)
        dout_ref[...] = dout
        dy, dg = _norm_bwd(yy, g, dout)
        dy_ref[...] = dy.astype(BF16)
        _accum(dg_ref, dg)

    return _call(body, name="loss_bwd", grid=(t // tr,),
                 in_specs=[_row_spec(tr, d), _vec_spec(d), _row_spec(tr, d), _row_spec(tr, d)],
                 out_specs=[pl.BlockSpec((1, 1), lambda i: (0, 0)), _row_spec(tr, d), _row_spec(tr, d), _vec_spec(d)],
                 out_shape=[jax.ShapeDtypeStruct((1, 1), F32), jax.ShapeDtypeStruct((t, d), F32),
                            jax.ShapeDtypeStruct((t, d), BF16), jax.ShapeDtypeStruct((1, d), F32)],
                 args=(y, g_post2, x1, target), sem=("arbitrary",))


def mid_bwd(dh2, x1, g_pre2, dout, mixed, g_post, carry=None, tr=256):
    t, d = x1.shape
    tr = _pick(t, tr)

    def body(dh_ref, x1_ref, g2_ref, do_ref, m_ref, gp_ref, dx1_ref, dm_ref, dg2_ref, dgp_ref):
        d1, dg2 = _norm_bwd(x1_ref[...], g2_ref[...], dh_ref[...])
        dx1 = do_ref[...] + d1
        dx1_ref[...] = dx1
        dm, dgp = _norm_bwd(m_ref[...], gp_ref[...], dx1)
        dm_ref[...] = dm.astype(BF16)
        _accum(dg2_ref, dg2)
        _accum(dgp_ref, dgp)

    return _call(body, name="mid_bwd", grid=(t // tr,),
                 in_specs=[_row_spec(tr, d), _row_spec(tr, d), _vec_spec(d), _row_spec(tr, d), _row_spec(tr, d),
                           _vec_spec(d)],
                 out_specs=[_row_spec(tr, d), _row_spec(tr, d), _vec_spec(d), _vec_spec(d)],
                 out_shape=[jax.ShapeDtypeStruct((t, d), F32), jax.ShapeDtypeStruct((t, d), BF16),
                            jax.ShapeDtypeStruct((1, d), F32), jax.ShapeDtypeStruct((1, d), F32)],
                 args=(dh2, x1, g_pre2, dout, mixed, g_post), sem=("arbitrary",), carry=carry)


def first_bwd(dh1, x, g_pre, dx1, carry=None, tr=256):
    t, d = x.shape
    tr = _pick(t, tr)

    def body(dh_ref, x_ref, g_ref, dx1_ref, gx_ref, dg_ref):
        d0, dg = _norm_bwd(x_ref[...], g_ref[...], dh_ref[...])
        gx_ref[...] = dx1_ref[...] + d0
        _accum(dg_ref, dg)

    return _call(body, name="first_bwd", grid=(t // tr,),
                 in_specs=[_row_spec(tr, d), _row_spec(tr, d), _vec_spec(d), _row_spec(tr, d)],
                 out_specs=[_row_spec(tr, d), _vec_spec(d)],
                 out_shape=[jax.ShapeDtypeStruct((t, d), F32), jax.ShapeDtypeStruct((1, d), F32)],
                 args=(dh1, x, g_pre, dx1), sem=("arbitrary",), carry=carry)


def _attn_geometry(has_prev):
    r = lax.broadcasted_iota(jnp.int32, (BLOCK, 2 * BLOCK), 0)
    c = lax.broadcasted_iota(jnp.int32, (BLOCK, 2 * BLOCK), 1)
    dist = r + BLOCK - c
    valid = jnp.logical_and(jnp.logical_and(dist >= 0, dist < BLOCK), jnp.logical_or(c >= BLOCK, has_prev))
    return dist.astype(F32), valid


def _stack_pairs(x, g, pairs):
    base = g * pairs * LANES
    return jnp.concatenate([x[:, base + p * LANES:base + (p + 1) * LANES] for p in range(pairs)], axis=0)


def _unstack_pairs(xs, pairs):
    return jnp.concatenate([xs[p * BLOCK:(p + 1) * BLOCK, :] for p in range(pairs)], axis=1)


def _to_half(x, g, odd):
    lane = lax.broadcasted_iota(jnp.int32, x.shape, 1)
    y = x if (g == 1) == odd else pltpu.roll(x, HEAD_DIM, axis=1)
    return jnp.where((lane >= HEAD_DIM) == odd, y, 0.0)


def _from_halves(even, odd, g):
    lane = lax.broadcasted_iota(jnp.int32, even.shape, 1)
    if g == 0:
        return jnp.where(lane < HEAD_DIM, even + pltpu.roll(odd, HEAD_DIM, axis=1), 0.0)
    return jnp.where(lane >= HEAD_DIM, pltpu.roll(even, HEAD_DIM, axis=1) + odd, 0.0)


_PARITIES = [(g, odd) for g in range(N_KV_HEADS) for odd in (False, True)]


def _softmax_sink(s, sink_ref, g, odd, group, n_heads, geo):
    dist, valid = geo
    pairs = group // 2
    heads = [g * group + 2 * p + int(odd) for p in range(pairs)]
    bias = jnp.concatenate([(2.0 ** (-8.0 * (h + 1) / n_heads)) * dist for h in heads], axis=0)
    sink = jnp.concatenate([jnp.full((BLOCK, 1), sink_ref[0, h], F32) for h in heads], axis=0)
    s = jnp.where(jnp.concatenate([valid] * pairs, axis=0), s - bias, -jnp.inf)
    m = jnp.maximum(jnp.max(s, axis=-1, keepdims=True), sink)
    p = jnp.exp(s - m)
    p_sink = jnp.exp(sink - m)
    inv = 1.0 / (jnp.sum(p, axis=-1, keepdims=True) + p_sink)
    return p * inv, p_sink * inv


def attn_fwd(proj, sinks, gain, aw, carry=None):
    t = proj.shape[0]
    kw = N_KV_HEADS * HEAD_DIM
    n_heads = aw // HEAD_DIM
    group = n_heads // N_KV_HEADS
    pairs = group // 2
    assert kw == LANES and group % 2 == 0
    nb = t // BLOCK
    scale = HEAD_DIM ** -0.5

    def body(sink_ref, q_ref, k_ref, v_ref, g_ref, o_ref, on_ref):
        n = pl.program_id(0)
        cur = pl.multiple_of(n * BLOCK, BLOCK)
        prev = pl.multiple_of(jnp.maximum(n - 1, 0) * BLOCK, BLOCK)
        geo = _attn_geometry(n > 0)
        kcat = jnp.concatenate([k_ref[pl.ds(prev, BLOCK), :], k_ref[pl.ds(cur, BLOCK), :]], axis=0)
        vcat = jnp.concatenate([v_ref[pl.ds(prev, BLOCK), :], v_ref[pl.ds(cur, BLOCK), :]], axis=0)
        q = q_ref[...] * scale
        qs = [_stack_pairs(q, g, pairs) for g in range(N_KV_HEADS)]
        scores = [_dot(qs[g], _to_half(kcat, g, odd), NT) for g, odd in _PARITIES]
        probs = [_softmax_sink(s, sink_ref, g, odd, group, n_heads, geo)[0] for s, (g, odd) in zip(scores, _PARITIES)]
        outs = [_dot(p, _to_half(vcat, g, odd), NN) for p, (g, odd) in zip(probs, _PARITIES)]
        o = jnp.concatenate([_unstack_pairs(outs[2 * g] + outs[2 * g + 1], pairs) for g in range(N_KV_HEADS)], axis=1)
        o_ref[...] = o
        on_ref[...] = (o * _rstd(o) * g_ref[...]).astype(BF16)

    return _call(body, name="attn_fwd", grid=(nb,),
                 in_specs=[pl.BlockSpec(memory_space=pltpu.SMEM),
                           pl.BlockSpec((BLOCK, aw), lambda n: (n, 0)),
                           pl.BlockSpec((t, kw), lambda n: (0, aw // kw)),
                           pl.BlockSpec((t, kw), lambda n: (0, aw // kw + 1)),
                           pl.BlockSpec((1, aw), lambda n: (0, 0))],
                 out_specs=[pl.BlockSpec((BLOCK, aw), lambda n: (n, 0)), pl.BlockSpec((BLOCK, aw), lambda n: (n, 0))],
                 out_shape=[jax.ShapeDtypeStruct((t, aw), F32), jax.ShapeDtypeStruct((t, aw), BF16)],
                 args=(sinks, proj, proj, proj, gain), sem=("parallel",), carry=carry)


def attn_bwd(proj, sinks, gain, attn_o, dcat, aw, carry=None):
    t = proj.shape[0]
    kw = N_KV_HEADS * HEAD_DIM
    n_heads = aw // HEAD_DIM
    group = n_heads // N_KV_HEADS
    pairs = group // 2
    assert kw == LANES and group % 2 == 0
    nb = t // BLOCK
    scale = HEAD_DIM ** -0.5

    def body(sink_ref, q_ref, k_ref, v_ref, g_ref, o_ref, dn_ref, dq_ref, dk_ref, dv_ref, dsink_ref, dg_ref):
        n = pl.program_id(0)
        cur = pl.multiple_of(n * BLOCK, BLOCK)
        prev = pl.multiple_of(jnp.maximum(n - 1, 0) * BLOCK, BLOCK)
        geo = _attn_geometry(n > 0)

        @pl.when(n == 0)
        def _():
            dk_ref[...] = jnp.zeros_like(dk_ref)
            dv_ref[...] = jnp.zeros_like(dv_ref)
            dsink_ref[...] = jnp.zeros_like(dsink_ref)

        o = o_ref[...]
        do_all, dg = _norm_bwd(o, g_ref[...], dn_ref[...])
        _accum(dg_ref, dg)
        kcat = jnp.concatenate([k_ref[pl.ds(prev, BLOCK), :], k_ref[pl.ds(cur, BLOCK), :]], axis=0)
        vcat = jnp.concatenate([v_ref[pl.ds(prev, BLOCK), :], v_ref[pl.ds(cur, BLOCK), :]], axis=0)
        q = q_ref[...] * scale
        lane = lax.broadcasted_iota(jnp.int32, (1, LANES), 1)
        lane_s = lax.broadcasted_iota(jnp.int32, (pairs * BLOCK, LANES), 1)
        qs = [_stack_pairs(q, g, pairs) for g in range(N_KV_HEADS)]
        dos = [_stack_pairs(do_all, g, pairs) for g in range(N_KV_HEADS)]
        kxs = [_to_half(kcat, g, odd) for g, odd in _PARITIES]
        scores = [_dot(qs[g], kx, NT) for kx, (g, odd) in zip(kxs, _PARITIES)]
        dps = [_dot(dos[g], _to_half(vcat, g, odd), NT) for g, odd in _PARITIES]
        deltas = []
        for g in range(N_KV_HEADS):
            prod = dos[g] * _stack_pairs(o, g, pairs)
            delta_even = jnp.sum(jnp.where(lane_s < HEAD_DIM, prod, 0.0), axis=-1, keepdims=True)
            deltas += [delta_even, jnp.sum(prod, axis=-1, keepdims=True) - delta_even]
        dsink = jnp.zeros((1, LANES), F32)
        ps, dss = [], []
        for i, (g, odd) in enumerate(_PARITIES):
            p, p_sink = _softmax_sink(scores[i], sink_ref, g, odd, group, n_heads, geo)
            ps.append(p)
            dss.append(p * (dps[i] - deltas[i]))
            sink_rows = p_sink * deltas[i]
            for pr in range(pairs):
                h = g * group + 2 * pr + int(odd)
                dsink = dsink + jnp.where(
                    lane == h, -jnp.sum(sink_rows[pr * BLOCK:(pr + 1) * BLOCK], axis=0, keepdims=True), 0.0)
        dq_pairs = [_dot(ds, kx, NN) for ds, kx in zip(dss, kxs)]
        dk_halves = [_dot(ds, qs[g], TN) for ds, (g, odd) in zip(dss, _PARITIES)]
        dv_halves = [_dot(p, dos[g], TN) for p, (g, odd) in zip(ps, _PARITIES)]
        dq_ref[...] = jnp.concatenate(
            [_unstack_pairs((dq_pairs[2 * g] + dq_pairs[2 * g + 1]) * scale, pairs) for g in range(N_KV_HEADS)],
            axis=1).astype(BF16)
        dk_upd = _from_halves(dk_halves[0], dk_halves[1], 0) + _from_halves(dk_halves[2], dk_halves[3], 1)
        dv_upd = _from_halves(dv_halves[0], dv_halves[1], 0) + _from_halves(dv_halves[2], dv_halves[3], 1)
        dk_ref[pl.ds(prev, BLOCK), :] += dk_upd[:BLOCK]
        dv_ref[pl.ds(prev, BLOCK), :] += dv_upd[:BLOCK]
        dk_ref[pl.ds(cur, BLOCK), :] += dk_upd[BLOCK:]
        dv_ref[pl.ds(cur, BLOCK), :] += dv_upd[BLOCK:]
        dsink_ref[...] += dsink

    return _call(body, name="attn_bwd", grid=(nb,),
                 in_specs=[pl.BlockSpec(memory_space=pltpu.SMEM),
                           pl.BlockSpec((BLOCK, aw), lambda n: (n, 0)),
                           pl.BlockSpec((t, kw), lambda n: (0, aw // kw)),
                           pl.BlockSpec((t, kw), lambda n: (0, aw // kw + 1)),
                           pl.BlockSpec((1, aw), lambda n: (0, 0)),
                           pl.BlockSpec((BLOCK, aw), lambda n: (n, 0)),
                           pl.BlockSpec((BLOCK, aw), lambda n: (n, 0))],
                 out_specs=[pl.BlockSpec((BLOCK, aw), lambda n: (n, 0)),
                            pl.BlockSpec((t, kw), lambda n: (0, 0)), pl.BlockSpec((t, kw), lambda n: (0, 0)),
                            pl.BlockSpec((1, LANES), lambda n: (0, 0)), pl.BlockSpec((1, aw), lambda n: (0, 0))],
                 out_shape=[jax.ShapeDtypeStruct((t, aw), BF16), jax.ShapeDtypeStruct((t, kw), F32),
                            jax.ShapeDtypeStruct((t, kw), F32), jax.ShapeDtypeStruct((1, LANES), F32),
                            jax.ShapeDtypeStruct((1, aw), F32)],
                 args=(sinks, proj, proj, proj, gain, attn_o, dcat), sem=("arbitrary",), carry=carry)


def _sigmoid(x):
    return 0.5 * jnp.tanh(0.5 * x) + 0.5


def _chunk_geometry():
    row = lax.broadcasted_iota(jnp.int32, (CHUNK, CHUNK), 0)
    col = lax.broadcasted_iota(jnp.int32, (CHUNK, CHUNK), 1)
    return row, col


def _cumsum_rows(x, reverse=False):
    row, col = _chunk_geometry()
    tri = (col >= row) if reverse else (col <= row)
    return lax.dot_general(tri.astype(F32), x, ((NN), ((), ())), precision=HI, preferred_element_type=F32)


def _rep_sub(x4, sub):
    k = x4.shape[-1]
    return jnp.broadcast_to(x4[:, None, :], (CHUNK // sub, sub, k)).reshape(CHUNK, k)


def _gates(q_r, f_r, lb):
    sg = _sigmoid(f_r)
    f = lb + (1.0 - lb) * sg
    sq = _sigmoid(q_r)
    return sg, f, sq, q_r * sq


def _offdiag_terms(b, j, sub):
    c = b[j * sub + sub - 1:j * sub + sub, :]
    return jnp.exp(jnp.minimum(b - c, 0.0)), jnp.exp(jnp.minimum(c - b, 0.0))


def _store_heads(ref, x):
    for j in range(ref.shape[0]):
        ref[j] = x[:, _head(j)]


def _sub_rows(ref, r, sub):
    rows = [ref[j, pl.ds(r, CHUNK // sub, stride=sub), :] for j in range(ref.shape[0])]
    return _rep_sub(jnp.concatenate(rows, axis=1), sub)


def _diag_mask(sub):
    row, col = _chunk_geometry()
    return jnp.logical_and((row // sub) == (col // sub), row >= col)


HGRN_HEADS_PER_STEP = 8


def _wide(refs):
    return jnp.concatenate([r[...] for r in refs], axis=1)


def _head(j):
    return slice(j * RNN_HEAD_DIM, (j + 1) * RNN_HEAD_DIM)


def _cat_heads(parts, hs):
    return jnp.concatenate([p[:, hs] for p in parts], axis=1)


def _offdiag_factors(q, k, b, sub):
    rowi = lax.broadcasted_iota(jnp.int32, b.shape, 0)
    qs, ks, ers, ecs = [], [], [], []
    for j in range(CHUNK // sub - 1):
        e_row, e_col = _offdiag_terms(b, j, sub)
        e_row = jnp.where(rowi >= (j + 1) * sub, e_row, 0.0)
        e_col = jnp.where((rowi // sub) == j, e_col, 0.0)
        qs.append(q * e_row)
        ks.append(k * e_col)
        ers.append(e_row)
        ecs.append(e_col)
    return qs, ks, ers, ecs


def hgrn_fwd(proj, attn_n, lb, norm_gain, col0, rw, carry=None):
    t, aw = attn_n.shape
    nh = rw // RNN_HEAD_DIM
    nc = t // CHUNK
    kd = RNN_HEAD_DIM
    cb = col0 // kd
    sub = SUB_FWD
    nsub = CHUNK // sub
    hp = nh
    assert nh <= HGRN_HEADS_PER_STEP
    w = hp * kd

    def body(*refs):
        q_refs, f_refs, i_refs, g_refs = (refs[i * hp:(i + 1) * hp] for i in range(4))
        lb_ref, ng_ref, an_ref, cat_ref, o_ref, att_ref, st_ref, state, b_ref, k_ref = refs[4 * hp:]
        c = pl.program_id(1)

        @pl.when(c == 0)
        def _():
            state[...] = jnp.zeros_like(state)

        st_ref[...] = state[...]
        q_r, f_r, v, g_r = (_wide(rs) for rs in (q_refs, f_refs, i_refs, g_refs))
        _, f, _, q = _gates(q_r, f_r, lb_ref[...])
        k = 1.0 - f
        b = _cumsum_rows(jnp.log(f))
        _store_heads(b_ref, b)
        _store_heads(k_ref, k)
        qcat, kcat, _, _ = _offdiag_factors(q, k, b, sub)
        row, col = _chunk_geometry()
        same = (row // sub) == (col // sub)
        rloc = lax.broadcasted_iota(jnp.int32, (CHUNK, w), 0) % sub
        diag = [jnp.zeros((CHUNK, CHUNK), F32)] * hp
        for r in range(sub):
            bs = _sub_rows(b_ref, r, sub)
            ks = _sub_rows(k_ref, r, sub)
            prod = q * jnp.exp(jnp.where(rloc >= r, b - bs, -jnp.inf)) * ks
            place = jnp.logical_and((col % sub) == r, same)
            diag = [jnp.where(place, jnp.sum(prod[:, _head(j)], axis=-1, keepdims=True), diag[j]) for j in range(hp)]
        b_last = b[CHUNK - 1:CHUNK, :]
        qe = q * jnp.exp(b)
        kdec = k * jnp.exp(b_last - b)
        decay = jnp.exp(b_last)
        outs, normed, states = [], [], []
        for j in range(hp):
            hs = _head(j)
            att = diag[j] + _dot(_cat_heads(qcat, hs), _cat_heads(kcat, hs), NT)
            att_ref[j] = att
            sj = state[j]
            o = _dot(qe[:, hs], sj, NT) + _dot(att, v[:, hs], NN)
            outs.append(o)
            normed.append(o * _rstd(o))
            states.append(sj * decay[:, hs] + _dot(v[:, hs], kdec[:, hs], TN))
        for j in range(hp):
            state[j] = states[j]
        o_ref[...] = jnp.concatenate(outs, axis=1)
        gate = g_r * _sigmoid(g_r)
        cat_ref[:, :aw] = an_ref[...]
        cat_ref[:, aw:] = (jnp.concatenate(normed, axis=1) * jnp.tile(ng_ref[...], (1, hp)) * gate).astype(BF16)

    def col(kidx, j):
        return pl.BlockSpec((CHUNK, kd), lambda hg, c: (c, cb + kidx * nh + hg * hp + j))

    return _call(body, name="hgrn_fwd", grid=(1, nc),
                 in_specs=[col(kidx, j) for kidx in range(4) for j in range(hp)] +
                          [pl.BlockSpec((1, w), lambda hg, c: (0, hg)), pl.BlockSpec((1, kd), lambda hg, c: (0, 0)),
                           pl.BlockSpec((CHUNK, aw), lambda hg, c: (c, 0))],
                 out_specs=[pl.BlockSpec((CHUNK, aw + w), lambda hg, c: (c, 0)),
                            pl.BlockSpec((CHUNK, w), lambda hg, c: (c, hg)),
                            pl.BlockSpec((hp, CHUNK, CHUNK), lambda hg, c: (hg, c, 0)),
                            pl.BlockSpec((None, hp, kd, kd), lambda hg, c: (c, hg, 0, 0))],
                 out_shape=[jax.ShapeDtypeStruct((t, aw + rw), BF16), jax.ShapeDtypeStruct((t, rw), F32),
                            jax.ShapeDtypeStruct((nh, t, CHUNK), F32), jax.ShapeDtypeStruct((nc, nh, kd, kd), F32)],
                 args=(*([proj] * (4 * hp)), lb, norm_gain, attn_n),
                 scratch_shapes=[pltpu.VMEM((hp, kd, kd), F32), pltpu.VMEM((hp, CHUNK, kd), F32),
                                 pltpu.VMEM((hp, CHUNK, kd), F32)],
                 sem=("parallel", "arbitrary"), carry=carry)


def hgrn_bwd(proj, lb, norm_gain, o_all, att_all, st_all, dcat, dq_a, dk_a, dv_a, col0, rw, carry=None):
    t, iw = proj.shape
    aw, kw = dq_a.shape[1], dk_a.shape[1]
    nh = rw // RNN_HEAD_DIM
    nc = t // CHUNK
    kd = RNN_HEAD_DIM
    cb = col0 // kd
    sub = SUB_BWD
    nsub = CHUNK // sub
    dcb = (dcat.shape[1] - rw) // kd
    hp = nh
    assert nh <= HGRN_HEADS_PER_STEP and dcb % hp == 0 and col0 == aw + 2 * kw and iw == col0 + 4 * rw
    w = hp * kd

    def per_head(x, fn):
        return jnp.concatenate([jnp.broadcast_to(fn(x[:, _head(j)]), (CHUNK, kd)) for j in range(hp)], axis=1)

    def body(*refs):
        q_refs, f_refs, i_refs, g_refs = (refs[i * hp:(i + 1) * hp] for i in range(4))
        (lb_ref, ng_ref, o_ref, att_ref, st0_ref, st1_ref, d_ref, dqa_ref, dka_ref, dva_ref, dp_ref, dlb_ref, dng_ref,
         dstate, b_ref, k_ref, dks_ref) = refs[4 * hp:]
        ci = pl.program_id(1)

        @pl.when(ci == 0)
        def _():
            dstate[...] = jnp.zeros_like(dstate)
            dlb_ref[...] = jnp.zeros_like(dlb_ref)
            dng_ref[...] = jnp.zeros_like(dng_ref)

        lbv = lb_ref[...]
        q_r, f_r, v, g_r = (_wide(rs) for rs in (q_refs, f_refs, i_refs, g_refs))
        sg, f, sq, q = _gates(q_r, f_r, lbv)
        k = 1.0 - f
        b = _cumsum_rows(jnp.log(f))
        _store_heads(b_ref, b)
        _store_heads(k_ref, k)
        row, col = _chunk_geometry()

        o = o_ref[...]
        ng = jnp.tile(ng_ref[...], (1, hp))
        sgg = _sigmoid(g_r)
        gate = g_r * sgg
        d_rnn = d_ref[...]
        r = per_head(o, _rstd)
        oh = o * r
        dp_ref[:, :aw] = dqa_ref[...]
        dp_ref[:, aw:aw + kw] = dka_ref[...].astype(BF16)
        dp_ref[:, aw + kw:col0] = dva_ref[...].astype(BF16)
        dp_ref[:, col0 + 3 * rw:] = (d_rnn * oh * ng * (sgg * (1.0 + g_r * (1.0 - sgg)))).astype(BF16)
        d_on = d_rnn * gate
        dng_rows = jnp.sum(d_on * oh, axis=0, keepdims=True)
        dng = dng_rows[:, _head(0)]
        for j in range(1, hp):
            dng = dng + dng_rows[:, _head(j)]
        dng_ref[...] += dng
        dyg = d_on * ng
        do = r * (dyg - oh * per_head(dyg * oh, lambda x: jnp.mean(x, axis=-1, keepdims=True)))

        b_last = b[CHUNK - 1:CHUNK, :]
        eb = jnp.exp(b)
        tail = jnp.exp(b_last - b)
        kdec = k * tail
        decay = jnp.exp(b_last)
        qe = q * eb
        qcat, kcat, ers, ecs = _offdiag_factors(q, k, b, sub)
        diag_mask = _diag_mask(sub)
        dqs, dks, dvs, dads, gsums, dstates = [], [], [], [], [], []
        for j in range(hp):
            hs = _head(j)
            do_h, v_h, dst = do[:, hs], v[:, hs], dstate[j]
            da = jnp.where(row >= col, _dot(do_h, v_h, NT), 0.0)
            dads.append(jnp.where(diag_mask, da, 0.0))
            dq = _dot(do_h, st0_ref[j], NN) * eb[:, hs]
            dk = _dot(v_h, dst, NN) * tail[:, hs]
            dvs.append(_dot(att_ref[j], do_h, TN) + _dot(kdec[:, hs], dst, NT))
            rq = _dot(da, _cat_heads(kcat, hs), NN)
            rk = _dot(da, _cat_heads(qcat, hs), TN)
            for jj in range(nsub - 1):
                dq = dq + ers[jj][:, hs] * rq[:, _head(jj)]
                dk = dk + ecs[jj][:, hs] * rk[:, _head(jj)]
            dqs.append(dq)
            dks.append(dk)
            gsums.append(jnp.sum(dst * st1_ref[j], axis=0, keepdims=True))
            dstates.append(dst * decay[:, hs] + _dot(do_h, qe[:, hs], TN))
        for j in range(hp):
            dstate[j] = dstates[j]
        dq = jnp.concatenate(dqs, axis=1)
        dk = jnp.concatenate(dks, axis=1)
        rloc = lax.broadcasted_iota(jnp.int32, (CHUNK, w), 0) % sub
        for rr in range(sub):
            bs = _sub_rows(b_ref, rr, sub)
            ks = _sub_rows(k_ref, rr, sub)
            e = jnp.exp(jnp.where(rloc >= rr, b - bs, -jnp.inf))
            pick = (col % sub) == rr
            dacol = jnp.concatenate(
                [jnp.broadcast_to(jnp.sum(jnp.where(pick, dads[j], 0.0), axis=-1, keepdims=True), (CHUNK, kd))
                 for j in range(hp)], axis=1)
            wv = dacol * e
            dq = dq + wv * ks
            sums = jnp.sum((wv * q).reshape(nsub, sub, w), axis=1)
            for j in range(hp):
                dks_ref[j, pl.ds(rr, nsub, stride=sub), :] = sums[:, _head(j)]
        dk = dk + jnp.concatenate([dks_ref[j] for j in range(hp)], axis=1)

        dlf = _cumsum_rows(q * dq - k * dk, reverse=True) + jnp.concatenate(gsums, axis=1)
        dfv = dlf / f - dk
        dp_ref[:, col0 + rw:col0 + 2 * rw] = (dfv * (1.0 - lbv) * sg * (1.0 - sg)).astype(BF16)
        dlb_ref[...] += jnp.sum(dfv * (1.0 - sg), axis=0, keepdims=True)
        dp_ref[:, col0:col0 + rw] = (dq * (sq * (1.0 + q_r * (1.0 - sq)))).astype(BF16)
        dp_ref[:, col0 + 2 * rw:col0 + 3 * rw] = jnp.concatenate(dvs, axis=1).astype(BF16)

    def rev(c):
        return nc - 1 - c

    def col_in(kidx, j):
        return pl.BlockSpec((CHUNK, kd), lambda hg, c: (rev(c), cb + kidx * nh + hg * hp + j))

    def rows(width):
        return pl.BlockSpec((CHUNK, width), lambda hg, c: (rev(c), 0))

    return _call(body, name="hgrn_bwd", grid=(1, nc),
                 in_specs=[col_in(kidx, j) for kidx in range(4) for j in range(hp)] +
                          [pl.BlockSpec((1, w), lambda hg, c: (0, hg)), pl.BlockSpec((1, kd), lambda hg, c: (0, 0)),
                           rows(w),
                           pl.BlockSpec((hp, CHUNK, CHUNK), lambda hg, c: (hg, rev(c), 0)),
                           pl.BlockSpec((None, hp, kd, kd), lambda hg, c: (rev(c), hg, 0, 0)),
                           pl.BlockSpec((None, hp, kd, kd),
                                        lambda hg, c: (jnp.minimum(rev(c) + 1, nc - 1), hg, 0, 0)),
                           pl.BlockSpec((CHUNK, w), lambda hg, c: (rev(c), dcb // hp + hg)),
                           rows(aw), rows(kw), rows(kw)],
                 out_specs=[rows(iw),
                            pl.BlockSpec((1, w), lambda hg, c: (0, hg)),
                            pl.BlockSpec((None, 1, kd), lambda hg, c: (hg, 0, 0))],
                 out_shape=[jax.ShapeDtypeStruct((t, iw), BF16), jax.ShapeDtypeStruct((1, rw), F32),
                            jax.ShapeDtypeStruct((1, 1, kd), F32)],
                 args=(*([proj] * (4 * hp)), lb, norm_gain, o_all, att_all, st_all, st_all, dcat, dq_a, dk_a, dv_a),
                 scratch_shapes=[pltpu.VMEM((hp, kd, kd), F32), pltpu.VMEM((hp, CHUNK, kd), F32),
                                 pltpu.VMEM((hp, CHUNK, kd), F32), pltpu.VMEM((hp, CHUNK, kd), F32)],
                 sem=("parallel", "arbitrary"), carry=carry)


def comm_only(name, part):
    return _call(lambda: None, name=name, grid=(), in_specs=[], out_specs=[], out_shape=[], args=(), carry=part)[1]


ADD_BLOCK_ELEMS = 1 << 20
ADAMW_BLOCK_ELEMS = 1 << 19


def add_kept_half(name, kept, got, sel, minor, row0=0):
    pieces, rows, cols = got.shape
    tr = _tile(rows, max(16, ADD_BLOCK_ELEMS // cols), mult=16)
    assert row0 % tr == 0
    i0 = row0 // tr

    def body(sel_ref, k_ref, g_ref, o_ref):
        o_ref[...] = (k_ref[...].astype(F32) + g_ref[...].astype(F32)).astype(o_ref.dtype)

    kept_spec = (pl.BlockSpec((None, None, tr, cols), lambda p, i, s: (p, s[0], i + i0, 0)) if minor else
                 pl.BlockSpec((None, None, tr, cols), lambda p, i, s: (s[0], p, i + i0, 0)))
    return pl.pallas_call(
        body, name=name,
        grid_spec=pltpu.PrefetchScalarGridSpec(
            num_scalar_prefetch=1, grid=(pieces, rows // tr),
            in_specs=[kept_spec, pl.BlockSpec((None, tr, cols), lambda p, i, s: (p, i, 0))],
            out_specs=pl.BlockSpec((None, tr, cols), lambda p, i, s: (p, i, 0))),
        out_shape=jax.ShapeDtypeStruct(got.shape, got.dtype),
        compiler_params=_cparams(("parallel", "parallel")),
    )(sel, kept, got)


def _adamw(w, g, m, v):
    m = ADAM_B1 * m + (1.0 - ADAM_B1) * g
    v = ADAM_B2 * v + (1.0 - ADAM_B2) * (g * g)
    m_hat = m / (1.0 - ADAM_B1 ** ADAM_STEP)
    v_hat = v / (1.0 - ADAM_B2 ** ADAM_STEP)
    delta = -ADAM_LR * (m_hat / (jnp.sqrt(v_hat) + ADAM_EPS) + ADAM_WD * w)
    return delta, m, v


def add_adamw(name, kept, got, sel, w, m, v, row0=0, into=None):
    _, rows, cols = got.shape
    tr = _tile(rows, max(16, ADAMW_BLOCK_ELEMS // cols), mult=16)
    assert row0 % tr == 0
    i0 = row0 // tr
    n_into = 0 if into is None else len(into)

    def body(sel_ref, k_ref, g_ref, w_ref, m_ref, v_ref, *rest):
        go_ref, d_ref, mo_ref, vo_ref = rest[n_into:]
        g = k_ref[...].astype(F32) + g_ref[...].astype(F32)
        go_ref[...] = g
        d_ref[...], mo_ref[...], vo_ref[...] = _adamw(w_ref[...], g, m_ref[...], v_ref[...])

    shard_tile = pl.BlockSpec((tr, cols), lambda i, s: (i + i0, 0))
    return pl.pallas_call(
        body, name=name,
        grid_spec=pltpu.PrefetchScalarGridSpec(
            num_scalar_prefetch=1, grid=(rows // tr,),
            in_specs=[pl.BlockSpec((None, None, tr, cols), lambda i, s: (s[0], 0, i, 0)),
                      pl.BlockSpec((None, tr, cols), lambda i, s: (0, i, 0)), shard_tile, shard_tile, shard_tile,
                      *[ANY] * n_into],
            out_specs=[shard_tile] * 4),
        out_shape=[jax.ShapeDtypeStruct(w.shape, F32)] * 4,
        input_output_aliases={6 + k: k for k in range(n_into)},
        compiler_params=_cparams(("parallel",)),
    )(sel, kept, got, w, m, v, *(into or ()))


def small_allreduce_adamw(partial, scale, w, m, v):
    rows = partial.shape[0]

    def body(p_ref, s_ref, w_ref, m_ref, v_ref, g_ref, d_ref, mo_ref, vo_ref, slots, send_sems, recv_sems):
        x, y, c = _coords()
        my_slot = _slab_index((x, y, c))
        slots[my_slot] = p_ref[...]
        copies = []
        for mask in range(1, N_DEV):
            to = tuple(1 - v_ if (mask >> s_) & 1 else v_ for v_, s_ in ((x, 2), (y, 1), (c, 0)))
            copies.append(pltpu.make_async_remote_copy(
                src_ref=p_ref, dst_ref=slots.at[my_slot],
                send_sem=send_sems.at[mask - 1], recv_sem=recv_sems.at[mask - 1],
                device_id=to, device_id_type=MESH))
        for cp in copies:
            cp.start()
        for cp in copies:
            cp.wait()
        total = slots[0]
        for b in range(1, N_DEV):
            total = total + slots[b]
        g = total * s_ref[...]
        g_ref[...] = g
        d_ref[...], mo_ref[...], vo_ref[...] = _adamw(w_ref[...], g, m_ref[...], v_ref[...])

    vm = pl.BlockSpec(memory_space=pltpu.VMEM)
    return pl.pallas_call(
        body, name="small_allreduce_adamw",
        in_specs=[vm] * 5, out_specs=[vm] * 4,
        out_shape=[jax.ShapeDtypeStruct((rows, LANES), F32)] * 4,
        scratch_shapes=[pltpu.VMEM((N_DEV, rows, LANES), F32),
                        pltpu.SemaphoreType.DMA((N_DEV - 1,)), pltpu.SemaphoreType.DMA((N_DEV - 1,))],
        compiler_params=pltpu.CompilerParams(has_side_effects=True),
    )(partial, scale, w, m, v)


_SMALL = ("attn_sinks", "attn_out_gain", "rnn_lb_logits", "rnn_norm_gain", "mix_pre_gain", "mix_post_gain",
          "mlp_pre_gain", "mlp_post_gain")


def _pack(parts):
    rows = []
    for p in parts:
        flat = p.reshape(-1).astype(F32)
        pad = (-flat.shape[0]) % LANES
        rows.append(jnp.pad(flat, (0, pad)).reshape(-1, LANES))
    packed = jnp.concatenate(rows, axis=0)
    pad_rows = (-packed.shape[0]) % 8
    return jnp.pad(packed, ((0, pad_rows), (0, 0)))


def _unpack(packed, shapes):
    out, r = [], 0
    for s in shapes:
        size = math.prod(s)
        nrows = -(-size // LANES)
        out.append(packed[r:r + nrows].reshape(-1)[:size].reshape(s))
        r += nrows
    return out


class _Scatter:
    def __init__(self, tag, grad, sels, both_links=False):
        self.tag, self.sels, self.both = tag, sels, both_links
        self.shape = grad.shape[1:]
        self.half = self.shape[0] // 2
        self.cur = grad.reshape(4, 2, *self.shape)
        self.stage = 0

    def step(self):
        if self.stage == 0 or not self.both:
            return _scatter_step(self.cur, "cxy"[self.stage])
        if self.stage == 1:
            return _merge(_scatter_step(self.cur, "x", rows=(0, self.half)),
                          _scatter_step(self.cur, "y", minor=True, rows=(self.half, self.shape[0])))
        upper, lower = self.cur
        return _merge(_scatter_step(upper, "y"), _scatter_step(lower, "x"))

    def land(self, got, w=None, m=None, v=None):
        stage, tag, sels = self.stage, self.tag, self.sels
        self.stage += 1
        if stage == 0 or not self.both:
            axis = "cxy"[stage]
            name = "rs_add_%s_%s" % (axis, tag)
            if axis == "y":
                return add_adamw(name, self.cur, got, sels[axis], w, m, v)
            summed = add_kept_half(name, self.cur, got, sels[axis], minor=axis == "c")
            self.cur = summed.reshape(2, summed.shape[0] // 2, *self.shape)
            return None
        got_upper, got_lower = got
        if stage == 1:
            upper = add_kept_half("rs_add_x_%s_upper" % tag, self.cur, got_upper, sels["x"], minor=False)
            lower = add_kept_half("rs_add_y_%s_lower" % tag, self.cur, got_lower, sels["y"], minor=True,
                                  row0=self.half)
            self.cur = tuple(s.reshape(2, 1, *s.shape[1:]) for s in (upper, lower))
            return None
        upper, lower = self.cur
        out_upper = add_adamw("rs_add_y_%s_upper" % tag, upper, got_upper, sels["y"], w, m, v)
        return add_adamw("rs_add_x_%s_lower" % tag, lower, got_lower, sels["x"], w, m, v, row0=self.half,
                         into=out_upper)


def kernel(x, w_in, attn_sinks, attn_out_gain, rnn_lb_logits, rnn_norm_gain, w_out, mix_pre_gain, mix_post_gain, mlp_pre_gain, mlp_post_gain, w_up, w_down, loss_target, m_w_in, m_attn_sinks, m_attn_out_gain, m_rnn_lb_logits, m_rnn_norm_gain, m_w_out, m_mix_pre_gain, m_mix_post_gain, m_mlp_pre_gain, m_mlp_post_gain, m_w_up, m_w_down, v_w_in, v_attn_sinks, v_attn_out_gain, v_rnn_lb_logits, v_rnn_norm_gain, v_w_out, v_mix_pre_gain, v_mix_post_gain, v_mlp_pre_gain, v_mlp_post_gain, v_w_up, v_w_down):
    xs, target = x[0], loss_target[0]
    t, d = xs.shape
    aw = d // 2
    rw = d - aw
    col0 = aw + 2 * N_KV_HEADS * HEAD_DIM
    small_w = dict(attn_sinks=attn_sinks, attn_out_gain=attn_out_gain, rnn_lb_logits=rnn_lb_logits,
                   rnn_norm_gain=rnn_norm_gain, mix_pre_gain=mix_pre_gain, mix_post_gain=mix_post_gain,
                   mlp_pre_gain=mlp_pre_gain, mlp_post_gain=mlp_post_gain)
    small_m = dict(attn_sinks=m_attn_sinks, attn_out_gain=m_attn_out_gain, rnn_lb_logits=m_rnn_lb_logits,
                   rnn_norm_gain=m_rnn_norm_gain, mix_pre_gain=m_mix_pre_gain, mix_post_gain=m_mix_post_gain,
                   mlp_pre_gain=m_mlp_pre_gain, mlp_post_gain=m_mlp_post_gain)
    small_v = dict(attn_sinks=v_attn_sinks, attn_out_gain=v_attn_out_gain, rnn_lb_logits=v_rnn_lb_logits,
                   rnn_norm_gain=v_rnn_norm_gain, mix_pre_gain=v_mix_pre_gain, mix_post_gain=v_mix_post_gain,
                   mlp_pre_gain=v_mlp_pre_gain, mlp_post_gain=v_mlp_post_gain)
    cx, cy, cc = _coords()
    sels = {a: jnp.reshape(v_, (1,)).astype(jnp.int32) for a, v_ in (("x", cx), ("y", cy), ("c", cc))}

    w_in_t, m_in_t, v_in_t = w_in[0].T, m_w_in[0].T, v_w_in[0].T
    s_in, s_out, s_up, s_down = (w.astype(BF16) for w in (w_in_t, w_out[0], w_up[0], w_down[0]))
    probs = jax.nn.softmax(rnn_lb_logits.astype(F32), axis=0)
    lb = probs[0:1]

    (h1,), (wint_part,) = pre_norm(xs, mix_pre_gain, carry=_gather_first(s_in, diagonal=False))
    in_rows = s_in.shape[0]
    wint = comm_only("gather_rest_w_in", _pass_slabs(
        wint_part,
        [(_X, _Y, (0, in_rows // 2)), (_Y, _X, (in_rows // 2, in_rows)), (_X, _C, None), (_Y, _C, None)],
        then=[(_XY, _C, None)]))[0].reshape(-1, d)
    up_rows = s_up.shape[0]
    up_cut = up_rows * 9 // 16
    proj, (wup_part,) = mm_nt("in_proj", h1, wint, F32, carry=_gather_first(s_up, rows=(0, up_cut)))
    (attn_o, attn_n), (wup_half, wout_half) = attn_fwd(
        proj, attn_sinks, attn_out_gain, aw,
        carry=_merge(_gather_first(s_up, rows=(up_cut, up_rows), into=wup_part), _gather_first(s_out)))
    (cat, o_r, att, st), (wup, wout, wdown_half) = hgrn_fwd(
        proj, attn_n, lb, rnn_norm_gain, col0, rw,
        carry=_merge(_gather_second(wup_half), _gather_second(wout_half), _gather_first(s_down)))
    wout = wout.reshape(-1, d)
    mixed, (wdown,) = mm_nn("out_proj", cat, wout, F32, carry=_gather_second(wdown_half))
    wdown = wdown.reshape(-1, d)
    x1, h2 = mid_fwd(mixed, mix_post_gain, xs, mlp_pre_gain)
    u = up_proj(h2, wup)
    y = down_proj(u, wdown)
    sse, dout, dy, dg_mlppost = loss_bwd(y, mlp_post_gain, x1, target)

    du = down_bwd_act(dy, wdown, u)
    rs_down = _Scatter("down", down_wgrad(u, dy).reshape(N_DEV, -1, d), sels, both_links=True)
    dh2, (got,) = up_bwd_x(du, wup, carry=rs_down.step())
    rs_down.land(got)
    dwup, gots = up_wgrad(h2, du, carry=rs_down.step())
    rs_down.land(gots)
    rs_up = _Scatter("up", dwup, sels, both_links=True)
    (dx1, dmixed, dg_mlppre, dg_mixpost), (got,) = mid_bwd(dh2, x1, mlp_pre_gain, dout, mixed, mix_post_gain,
                                                          carry=rs_up.step())
    rs_up.land(got)
    dcat = mm_nt("out_bwd_x", dmixed, wout, F32)
    rs_out = _Scatter("out", mm_tn("out_wgrad", cat, dmixed, BF16).reshape(N_DEV, -1, d), sels)
    (dq_a, dk_a, dv_a, dsinks, daog), (*gots, got_o) = attn_bwd(
        proj, attn_sinks, attn_out_gain, attn_o, dcat, aw, carry=_merge(rs_down.step(), rs_out.step()))
    out_down = rs_down.land(gots, w_down[0], m_w_down[0], v_w_down[0])
    rs_out.land(got_o)
    (dproj, dlb, dng), (*gots, got_o) = hgrn_bwd(
        proj, lb, rnn_norm_gain, o_r, att, st, dcat, dq_a, dk_a, dv_a, col0, rw,
        carry=_merge(rs_up.step(), rs_out.step()))
    rs_up.land(gots)
    rs_out.land(got_o)
    dwin, (*gots, got_o) = mm_tn("in_wgrad", dproj, h1, BF16, carry=_merge(rs_up.step(), rs_out.step()))
    out_up = rs_up.land(gots, w_up[0], m_w_up[0], v_w_up[0])
    out_out = rs_out.land(got_o, w_out[0], m_w_out[0], v_w_out[0])
    rs_in = _Scatter("in", dwin.reshape(N_DEV, -1, d), sels, both_links=True)
    rs_in.land(comm_only("rs_exchange_c_in", rs_in.step())[0])
    dh1, gots = mm_nn("in_bwd_x", dproj, wint, F32, tm=MM_TILE // 2, carry=rs_in.step())
    rs_in.land(gots)
    grad_x, dg_mixpre = first_bwd(dh1, xs, mix_pre_gain, dx1)
    out_in = rs_in.land(comm_only("rs_exchange_last_in", rs_in.step()), w_in_t, m_in_t, v_in_t)
    big_out = [out_in, out_out, out_up, out_down]

    n_heads = attn_sinks.shape[1]
    jac = probs[0] * probs[1]
    partial = _pack([sse, dsinks[0, :n_heads], daog, jnp.stack([dlb[0], dlb[0]]), jnp.sum(dng, axis=0),
                     dg_mixpre, dg_mixpost, dg_mlppre, dg_mlppost])
    ones = [jnp.ones(small_w[k].shape, F32) for k in _SMALL]
    ones[2] = jnp.stack([jac, -jac])
    scale = _pack([jnp.full((1,), 0.5 / d, F32)] + ones)
    zero = jnp.zeros((1,), F32)
    outs = small_allreduce_adamw(partial, scale, _pack([zero] + [small_w[k] for k in _SMALL]),
                                 _pack([zero] + [small_m[k] for k in _SMALL]),
                                 _pack([jnp.ones((1,), F32)] + [small_v[k] for k in _SMALL]))
    shapes = [(1,)] + [small_w[k].shape for k in _SMALL]
    sgrad, sdelta, snm, snv = (_unpack(o, shapes) for o in outs)
    loss = sgrad[0][0]

    def big(i, j):
        o = big_out[i][j]
        return (o.T if i == 0 else o)[None]

    def ordered(j, smalls):
        s = dict(zip(_SMALL, smalls[1:]))
        return [big(0, j), s["attn_sinks"], s["attn_out_gain"], s["rnn_lb_logits"], s["rnn_norm_gain"], big(1, j),
                s["mix_pre_gain"], s["mix_post_gain"], s["mlp_pre_gain"], s["mlp_post_gain"], big(2, j), big(3, j)]

    return (loss, grad_x[None], *ordered(0, sgrad), *ordered(1, sdelta), *ordered(2, snm), *ordered(3, snv))
```

```python
import math

import jax
import jax.numpy as jnp
from jax import lax
from jax.experimental import pallas as pl
from jax.experimental.pallas import tpu as pltpu

F32 = jnp.float32
BF16 = jnp.bfloat16

HEAD_DIM = 64
N_KV_HEADS = 2
BLOCK = 128
RNN_HEAD_DIM = 128
CHUNK = 64
SUB_FWD = 16
SUB_BWD = 16
EPS = 1e-6

ADAM_LR = 0.001
ADAM_B1 = 0.9
ADAM_B2 = 0.999
ADAM_EPS = 1e-08
ADAM_WD = 0.01
ADAM_STEP = 10

N_DEV = 8
LANES = 128
V7X_VMEM_LIMIT = 56 * 1024 * 1024
MESH = pl.DeviceIdType.MESH
HI = lax.Precision.HIGHEST
ANY = pl.BlockSpec(memory_space=pl.ANY)
_AXES = ("x", "y", "c")


def _cparams(sem=None, **kw):
    return pltpu.CompilerParams(dimension_semantics=sem, vmem_limit_bytes=V7X_VMEM_LIMIT, **kw)


def _dot(a, b, dims):
    return lax.dot_general(a.astype(BF16), b.astype(BF16), (dims, ((), ())), preferred_element_type=F32)


NN = ((1,), (0,))
NT = ((1,), (1,))
TN = ((0,), (0,))


def _pick(n, pref):
    t = min(n, pref)
    while n % t:
        t //= 2
    return t


def _tile(n, pref, mult=LANES):
    if n <= pref:
        return n
    t = pref - pref % mult
    while n % t:
        t -= mult
    return t


def _coords():
    return lax.axis_index("x"), lax.axis_index("y"), lax.axis_index("c")


def _slab_index(dev):
    return 4 * dev[0] + 2 * dev[1] + dev[2]


class _Part:
    def __init__(self, operands, landings, aliases, n_sems, plan):
        self.operands, self.landings, self.aliases, self.n_sems, self.plan = operands, landings, aliases, n_sems, plan


def _merge(*parts):
    operands, landings, aliases, plans = [], [], {}, []
    s0 = 0
    for p in parts:
        o0, l0 = len(operands), len(landings)
        aliases.update({o0 + i: l0 + j for i, j in p.aliases.items()})
        plans.append((p.plan, o0, len(p.operands), l0, len(p.landings), s0))
        operands += p.operands
        landings += p.landings
        s0 += p.n_sems

    def plan(ops, lands, sem):
        starts, waits = [], []
        for f, o0, no, l0, nl, off in plans:
            s, w = f(ops[o0:o0 + no], lands[l0:l0 + nl], lambda kind, k, off=off: sem(kind, off + k))
            starts += s
            waits += w
        return starts, waits

    return _Part(operands, landings, aliases, s0, plan)


def _gather_peers(x, y, c):
    return [(x, y, 1 - c), (1 - x, y, c), (x, 1 - y, c), (1 - x, 1 - y, c)]


def _gather_first(shard, rows=None, into=None, diagonal=True):
    lo, hi = (0, shard.shape[0]) if rows is None else rows
    n_peers = 4 if diagonal else 3

    def plan(ops, lands, sem):
        x, y, c = _coords()
        me, peers = (x, y, c), _gather_peers(x, y, c)[:n_peers]
        src = ops[0].at[pl.ds(lo, hi - lo)]

        def slab(block):
            return lands[0].at[_slab_index(block), pl.ds(lo, hi - lo)]

        def cp(k, block, to):
            return pltpu.make_async_remote_copy(
                src_ref=src, dst_ref=slab(block),
                send_sem=sem(0, k), recv_sem=sem(1, k), device_id=to, device_id_type=MESH)

        local = pltpu.make_async_copy(src, slab(me), sem(2, 0))
        sends = [cp(k, me, to) for k, to in enumerate(peers)]
        recvs = [cp(k, frm, me) for k, frm in enumerate(peers)]
        return ([local.start] + [s.start for s in sends],
                [local.wait] + [s.wait_send for s in sends] + [r.wait_recv for r in recvs])

    landing = jax.ShapeDtypeStruct((N_DEV, *shard.shape), shard.dtype)
    if into is None:
        return _Part([shard], [landing], {}, 4, plan)
    return _Part([shard, into], [landing], {1: 0}, 4, plan)


def _flip(dev, flips):
    return tuple(1 - v if f else v for v, f in zip(dev, flips))


def _pass_slabs(gathered, moves, then=()):
    def wave(lands, sem, k0, wave_moves):
        me = _coords()
        sends, recvs = [], []
        for k, (block, dest, rows) in enumerate(wave_moves, start=k0):
            lo, hi = (0, gathered.shape[1]) if rows is None else rows

            def cp(blk, to, k=k, lo=lo, hi=hi):
                slab = lands[0].at[_slab_index(blk), pl.ds(lo, hi - lo)]
                return pltpu.make_async_remote_copy(
                    src_ref=slab, dst_ref=slab, send_sem=sem(0, k), recv_sem=sem(1, k),
                    device_id=to, device_id_type=MESH)

            sends.append(cp(_flip(me, block), _flip(me, dest)))
            recvs.append(cp(_flip(_flip(me, dest), block), me))
        return [s.start for s in sends], [s.wait_send for s in sends] + [r.wait_recv for r in recvs]

    def plan(ops, lands, sem):
        starts, waits = wave(lands, sem, 0, moves)
        if then:
            starts2, waits2 = wave(lands, sem, len(moves), then)
            waits = waits + starts2 + waits2
        return starts, waits

    return _Part([gathered], [jax.ShapeDtypeStruct(gathered.shape, gathered.dtype)], {0: 0},
                 len(moves) + len(then), plan)


_X, _Y, _C, _XY = (1, 0, 0), (0, 1, 0), (0, 0, 1), (1, 1, 0)


def _gather_second(gathered):
    def plan(ops, lands, sem):
        x, y, c = _coords()
        sibling = (x, y, 1 - c)
        chips = [(1 - x, y), (x, 1 - y), (1 - x, 1 - y)]

        def cp(k, block):
            slab = lands[0].at[_slab_index(block)]
            return pltpu.make_async_remote_copy(
                src_ref=slab, dst_ref=slab, send_sem=sem(0, k), recv_sem=sem(1, k),
                device_id=sibling, device_id_type=MESH)

        sends = [cp(k, (*chip, c)) for k, chip in enumerate(chips)]
        recvs = [cp(k, (*chip, 1 - c)) for k, chip in enumerate(chips)]
        return [s.start for s in sends], [s.wait_send for s in sends] + [r.wait_recv for r in recvs]

    return _Part([gathered], [jax.ShapeDtypeStruct(gathered.shape, gathered.dtype)], {0: 0}, 3, plan)


def _scatter_step(array, axis, minor=None, rows=None):
    minor = (axis == "c") if minor is None else minor
    pieces = array.shape[0] if minor else array.shape[1]
    lo, hi = (0, array.shape[2]) if rows is None else rows

    def plan(ops, lands, sem):
        coords = list(_coords())
        ai = _AXES.index(axis)
        mine = coords[ai]
        peer = list(coords)
        peer[ai] = 1 - mine
        cps = []
        for p in range(pieces):
            src = ops[0].at[p, 1 - mine, pl.ds(lo, hi - lo)] if minor else ops[0].at[1 - mine, p, pl.ds(lo, hi - lo)]
            cps.append(pltpu.make_async_remote_copy(
                src_ref=src, dst_ref=lands[0].at[p], send_sem=sem(0, p), recv_sem=sem(1, p),
                device_id=tuple(peer), device_id_type=MESH))
        return [cp.start for cp in cps], [cp.wait for cp in cps]

    return _Part([array], [jax.ShapeDtypeStruct((pieces, hi - lo, array.shape[3]), array.dtype)], {}, pieces, plan)


def _grid_edges(grid):
    first = last = None
    for ax, n in enumerate(grid):
        p = pl.program_id(ax)
        f, l = p == 0, p == n - 1
        first = f if first is None else jnp.logical_and(first, f)
        last = l if last is None else jnp.logical_and(last, l)
    return first, last


def _call(body, *, name, grid, in_specs, out_specs, out_shape, args, scratch_shapes=(), sem=None, carry=None):
    if carry is None:
        return pl.pallas_call(
            body, name=name, grid=grid, in_specs=list(in_specs), out_specs=list(out_specs),
            out_shape=list(out_shape), scratch_shapes=list(scratch_shapes), compiler_params=_cparams(sem),
        )(*args)
    n_in, n_out, n_scr = len(in_specs), len(out_specs), len(scratch_shapes)
    n_cin, n_cout = len(carry.operands), len(carry.landings)

    def wrapped(*refs):
        ins, cins = refs[:n_in], refs[n_in:n_in + n_cin]
        o0 = n_in + n_cin
        outs, couts = refs[o0:o0 + n_out], refs[o0 + n_out:o0 + n_out + n_cout]
        s0 = o0 + n_out + n_cout
        scr, sems = refs[s0:s0 + n_scr], refs[s0 + n_scr:]
        first, last = _grid_edges(grid)

        def plan():
            return carry.plan(cins, couts, lambda kind, k: sems[kind].at[k])

        def start_all():
            for start in plan()[0]:
                start()

        def wait_all():
            for wait in plan()[1]:
                wait()

        if grid:
            pl.when(first)(start_all)
            body(*ins, *outs, *scr)
            pl.when(last)(wait_all)
        else:
            start_all()
            body(*ins, *outs, *scr)
            wait_all()

    sem_arrays = [pltpu.SemaphoreType.DMA((carry.n_sems,))] * 3
    res = pl.pallas_call(
        wrapped, name=name, grid=grid,
        in_specs=[*in_specs, *[ANY] * n_cin], out_specs=[*out_specs, *[ANY] * n_cout],
        out_shape=[*out_shape, *carry.landings],
        scratch_shapes=[*scratch_shapes, *sem_arrays],
        input_output_aliases={n_in + i: n_out + j for i, j in carry.aliases.items()},
        compiler_params=_cparams(("arbitrary",) * len(grid) if grid else None, has_side_effects=True),
    )(*args, *carry.operands)
    return res[:n_out], res[n_out:]


MM_TILE = 1024
MM_K_TILE = 2048
MXU_COLS = 256
MM_VMEM_BUDGET = 50 * 1024 * 1024


def _matmul(name, a, b, dims, grid, a_spec, b_spec, out_shape, out_spec, epilogue,
            extras=(), extra_specs=(), prologue=None, carry=None):
    nk = grid[2]
    n_extra = len(extras)
    acc_shape = out_spec.block_shape[-2:]

    def lhs(a_ref):
        return a_ref[...] if prologue is None else prologue(a_ref[...])

    def body_one(a_ref, b_ref, *rest):
        epilogue(_dot(lhs(a_ref), b_ref[...], dims), rest[:n_extra], rest[n_extra:])

    def body_acc(a_ref, b_ref, *rest):
        acc = rest[-1]
        k = pl.program_id(2)
        part = _dot(lhs(a_ref), b_ref[...], dims)

        @pl.when(k == 0)
        def _():
            acc[...] = part

        @pl.when(k > 0)
        def _():
            acc[...] += part

        @pl.when(k == nk - 1)
        def _():
            epilogue(acc[...], rest[:n_extra], rest[n_extra:-1])

    res = _call(body_one if nk == 1 else body_acc, name=name, grid=grid,
                in_specs=[a_spec, b_spec, *extra_specs], out_specs=[out_spec], out_shape=[out_shape],
                args=(a, b, *extras), scratch_shapes=[] if nk == 1 else [pltpu.VMEM(acc_shape, F32)],
                sem=("parallel", "parallel", "arbitrary"), carry=carry)
    return res[0] if carry is None else (res[0][0], res[1])


def _store_as(acc, extra_refs, out_refs):
    out_refs[0][...] = acc.astype(out_refs[0].dtype)


def _square(u):
    return u * u


def mm_nn(name, a, b, out_dtype, tk=None, tm=MM_TILE, tn=MM_TILE, prologue=None, carry=None):
    (m, kk), n = a.shape, b.shape[1]
    tm, tn = _tile(m, tm), _tile(n, tn, mult=MXU_COLS)
    tk = kk if tk is None else _tile(kk, tk, mult=MXU_COLS)
    return _matmul(name, a, b, NN, (m // tm, n // tn, kk // tk),
                   pl.BlockSpec((tm, tk), lambda i, j, k: (i, k)),
                   pl.BlockSpec((tk, tn), lambda i, j, k: (k, j)),
                   jax.ShapeDtypeStruct((m, n), out_dtype),
                   pl.BlockSpec((tm, tn), lambda i, j, k: (i, j)), _store_as, prologue=prologue, carry=carry)


def mm_nt(name, a, b, out_dtype, epilogue=_store_as, extras=(), extra_specs=(), tn=MM_TILE, carry=None):
    (m, kk), n = a.shape, b.shape[0]
    tm, tn = _tile(m, MM_TILE), _tile(n, tn, mult=MXU_COLS)
    return _matmul(name, a, b, NT, (m // tm, n // tn, 1),
                   pl.BlockSpec((tm, kk), lambda i, j, k: (i, 0)),
                   pl.BlockSpec((tn, kk), lambda i, j, k: (j, 0)),
                   jax.ShapeDtypeStruct((m, n), out_dtype),
                   pl.BlockSpec((tm, tn), lambda i, j, k: (i, j)), epilogue,
                   extras=extras, extra_specs=extra_specs, carry=carry)


def _whole_k_fits(tm, tn, kk, out_dtype, prologue):
    operands = 2 * 2 * kk * (tm + tn)
    out = 2 * tm * tn * jnp.dtype(out_dtype).itemsize + 4 * tm * tn
    return operands + out + (2 * kk * tm if prologue is not None else 0) <= MM_VMEM_BUDGET


def mm_tn(name, a, b, out_dtype, prologue=None, carry=None):
    (kk, m), n = a.shape, b.shape[1]
    tm, tn = _tile(m, MM_TILE), _tile(n, MM_TILE)
    tk = kk if _whole_k_fits(tm, tn, kk, out_dtype, prologue) else _tile(kk, MM_K_TILE)
    return _matmul(name, a, b, TN, (m // tm, n // tn, kk // tk),
                   pl.BlockSpec((tk, tm), lambda i, j, k: (k, i)),
                   pl.BlockSpec((tk, tn), lambda i, j, k: (k, j)),
                   jax.ShapeDtypeStruct((m, n), out_dtype),
                   pl.BlockSpec((tm, tn), lambda i, j, k: (i, j)), _store_as, prologue=prologue, carry=carry)


def up_proj(h2, wup_slabs):
    (m, kk), (_, _, ns) = h2.shape, wup_slabs.shape
    tm, tn = _tile(m, MM_TILE), _tile(ns, MM_TILE)
    r = ns // tn
    n = N_DEV * ns

    def epi(acc, extra_refs, out_refs):
        out_refs[0][...] = jnp.maximum(acc, 0.0).astype(BF16)

    return _matmul("up_proj", h2, wup_slabs, NN, (m // tm, n // tn, 1),
                   pl.BlockSpec((tm, kk), lambda i, j, k: (i, 0)),
                   pl.BlockSpec((None, kk, tn), lambda i, j, k: (j // r, 0, j % r)),
                   jax.ShapeDtypeStruct((m, n), BF16),
                   pl.BlockSpec((tm, tn), lambda i, j, k: (i, j)), epi)


def down_proj(u, wdown):
    return mm_nn("down_proj", u, wdown, F32, tm=MM_TILE // 2, tn=MM_TILE // 2, prologue=_square)


def down_bwd_act(dy, wdown, u):
    tm, tn = _tile(dy.shape[0], MM_TILE), _tile(wdown.shape[0], MM_TILE)

    def epi(acc, extra_refs, out_refs):
        out_refs[0][...] = (acc * (2.0 * extra_refs[0][...].astype(F32))).astype(BF16)

    return mm_nt("down_bwd_act", dy, wdown, BF16, epilogue=epi, extras=(u,),
                 extra_specs=(pl.BlockSpec((tm, tn), lambda i, j, k: (i, j)),))


def down_wgrad(u, dy):
    return mm_tn("down_wgrad", u, dy, BF16, prologue=_square)


def up_bwd_x(du, wup_slabs, carry=None):
    (m, kk), (slabs, n, ns) = du.shape, wup_slabs.shape
    tm, tn = _tile(m, MM_TILE // 2), _tile(n, MM_TILE // 2, mult=MXU_COLS)

    def body(a_ref, b_ref, o_ref):
        acc = _dot(a_ref[:, :ns], b_ref[0], NT)
        for s in range(1, slabs):
            acc = acc + _dot(a_ref[:, s * ns:(s + 1) * ns], b_ref[s], NT)
        o_ref[...] = acc

    res = _call(body, name="up_bwd_x", grid=(m // tm, n // tn),
                in_specs=[pl.BlockSpec((tm, kk), lambda i, j: (i, 0)),
                          pl.BlockSpec((slabs, tn, ns), lambda i, j: (0, j, 0))],
                out_specs=[pl.BlockSpec((tm, tn), lambda i, j: (i, j))],
                out_shape=[jax.ShapeDtypeStruct((m, n), F32)], args=(du, wup_slabs),
                sem=("parallel", "parallel"), carry=carry)
    return res[0] if carry is None else (res[0][0], res[1])


def up_wgrad(h2, du, carry=None):
    (kk, m), n = h2.shape, du.shape[1]
    ns = n // N_DEV
    tm, tn = _tile(m, MM_TILE), _tile(ns, MM_TILE)
    tk = kk if _whole_k_fits(tm, tn, kk, BF16, None) else _tile(kk, MM_K_TILE)
    r = ns // tn
    return _matmul("up_wgrad", h2, du, TN, (m // tm, n // tn, kk // tk),
                   pl.BlockSpec((tk, tm), lambda i, j, k: (k, i)),
                   pl.BlockSpec((tk, tn), lambda i, j, k: (k, j)),
                   jax.ShapeDtypeStruct((N_DEV, m, ns), BF16),
                   pl.BlockSpec((None, tm, tn), lambda i, j, k: (j // r, i, j % r)), _store_as, carry=carry)


def _rstd(x):
    return lax.rsqrt(jnp.mean(x * x, axis=-1, keepdims=True) + EPS)


def _norm_bwd(x, g, dy):
    r = _rstd(x)
    xh = x * r
    dyg = dy * g
    dx = r * (dyg - xh * jnp.mean(dyg * xh, axis=-1, keepdims=True))
    return dx, jnp.sum(dy * xh, axis=0, keepdims=True)


def _row_spec(tr, d):
    return pl.BlockSpec((tr, d), lambda i: (i, 0))


def _vec_spec(d):
    return pl.BlockSpec((1, d), lambda i: (0, 0))


def _accum(ref, val):
    @pl.when(pl.program_id(0) == 0)
    def _():
        ref[...] = jnp.zeros_like(ref)

    ref[...] += val


def pre_norm(x, g, carry=None, tr=256):
    t, d = x.shape
    tr = _pick(t, tr)

    def body(x_ref, g_ref, h_ref):
        xx = x_ref[...]
        h_ref[...] = (xx * _rstd(xx) * g_ref[...]).astype(BF16)

    return _call(body, name="pre_norm", grid=(t // tr,),
                 in_specs=[_row_spec(tr, d), _vec_spec(d)], out_specs=[_row_spec(tr, d)],
                 out_shape=[jax.ShapeDtypeStruct((t, d), BF16)], args=(x, g), sem=("parallel",), carry=carry)


def mid_fwd(mixed, g_post, x, g_pre2, tr=256):
    t, d = x.shape
    tr = _pick(t, tr)

    def body(m_ref, gp_ref, x_ref, g2_ref, x1_ref, h2_ref):
        mm = m_ref[...]
        x1 = x_ref[...] + mm * _rstd(mm) * gp_ref[...]
        x1_ref[...] = x1
        h2_ref[...] = (x1 * _rstd(x1) * g2_ref[...]).astype(BF16)

    return _call(body, name="mid_fwd", grid=(t // tr,),
                 in_specs=[_row_spec(tr, d), _vec_spec(d), _row_spec(tr, d), _vec_spec(d)],
                 out_specs=[_row_spec(tr, d), _row_spec(tr, d)],
                 out_shape=[jax.ShapeDtypeStruct((t, d), F32), jax.ShapeDtypeStruct((t, d), BF16)],
                 args=(mixed, g_post, x, g_pre2), sem=("parallel",))


def loss_bwd(y, g_post2, x1, target, tr=256):
    t, d = y.shape
    tr = _pick(t, tr)

    def body(y_ref, g_ref, x1_ref, t_ref, sse_ref, dout_ref, dy_ref, dg_ref):
        yy = y_ref[...]
        g = g_ref[...]
        err = x1_ref[...] + yy * _rstd(yy) * g - t_ref[...]
        _accum(sse_ref, jnp.sum(jnp.sum(err * err, axis=1, keepdims=True), axis=0, keepdims=True))
        dout = err * (1.0 / d)
        dout_ref[...] = dout
        dy, dg = _norm_bwd(yy, g, dout)
        dy_ref[...] = dy.astype(BF16)
        _accum(dg_ref, dg)

    return _call(body, name="loss_bwd", grid=(t // tr,),
                 in_specs=[_row_spec(tr, d), _vec_spec(d), _row_spec(tr, d), _row_spec(tr, d)],
                 out_specs=[pl.BlockSpec((1, 1), lambda i: (0, 0)), _row_spec(tr, d), _row_spec(tr, d), _vec_spec(d)],
                 out_shape=[jax.ShapeDtypeStruct((1, 1), F32), jax.ShapeDtypeStruct((t, d), F32),
                            jax.ShapeDtypeStruct((t, d), BF16), jax.ShapeDtypeStruct((1, d), F32)],
                 args=(y, g_post2, x1, target), sem=("arbitrary",))


def mid_bwd(dh2, x1, g_pre2, dout, mixed, g_post, carry=None, tr=256):
    t, d = x1.shape
    tr = _pick(t, tr)

    def body(dh_ref, x1_ref, g2_ref, do_ref, m_ref, gp_ref, dx1_ref, dm_ref, dg2_ref, dgp_ref):
        d1, dg2 = _norm_bwd(x1_ref[...], g2_ref[...], dh_ref[...])
        dx1 = do_ref[...] + d1
        dx1_ref[...] = dx1
        dm, dgp = _norm_bwd(m_ref[...], gp_ref[...], dx1)
        dm_ref[...] = dm.astype(BF16)
        _accum(dg2_ref, dg2)
        _accum(dgp_ref, dgp)

    return _call(body, name="mid_bwd", grid=(t // tr,),
                 in_specs=[_row_spec(tr, d), _row_spec(tr, d), _vec_spec(d), _row_spec(tr, d), _row_spec(tr, d),
                           _vec_spec(d)],
                 out_specs=[_row_spec(tr, d), _row_spec(tr, d), _vec_spec(d), _vec_spec(d)],
                 out_shape=[jax.ShapeDtypeStruct((t, d), F32), jax.ShapeDtypeStruct((t, d), BF16),
                            jax.ShapeDtypeStruct((1, d), F32), jax.ShapeDtypeStruct((1, d), F32)],
                 args=(dh2, x1, g_pre2, dout, mixed, g_post), sem=("arbitrary",), carry=carry)


def first_bwd(dh1, x, g_pre, dx1, carry=None, tr=256):
    t, d = x.shape
    tr = _pick(t, tr)

    def body(dh_ref, x_ref, g_ref, dx1_ref, gx_ref, dg_ref):
        d0, dg = _norm_bwd(x_ref[...], g_ref[...], dh_ref[...])
        gx_ref[...] = dx1_ref[...] + d0
        _accum(dg_ref, dg)

    return _call(body, name="first_bwd", grid=(t // tr,),
                 in_specs=[_row_spec(tr, d), _row_spec(tr, d), _vec_spec(d), _row_spec(tr, d)],
                 out_specs=[_row_spec(tr, d), _vec_spec(d)],
                 out_shape=[jax.ShapeDtypeStruct((t, d), F32), jax.ShapeDtypeStruct((1, d), F32)],
                 args=(dh1, x, g_pre, dx1), sem=("arbitrary",), carry=carry)


def _attn_geometry(has_prev):
    r = lax.broadcasted_iota(jnp.int32, (BLOCK, 2 * BLOCK), 0)
    c = lax.broadcasted_iota(jnp.int32, (BLOCK, 2 * BLOCK), 1)
    dist = r + BLOCK - c
    valid = jnp.logical_and(jnp.logical_and(dist >= 0, dist < BLOCK), jnp.logical_or(c >= BLOCK, has_prev))
    return dist.astype(F32), valid


def _stack_pairs(x, g, pairs):
    base = g * pairs * LANES
    return jnp.concatenate([x[:, base + p * LANES:base + (p + 1) * LANES] for p in range(pairs)], axis=0)


def _unstack_pairs(xs, pairs):
    return jnp.concatenate([xs[p * BLOCK:(p + 1) * BLOCK, :] for p in range(pairs)], axis=1)


def _to_half(x, g, odd):
    lane = lax.broadcasted_iota(jnp.int32, x.shape, 1)
    y = x if (g == 1) == odd else pltpu.roll(x, HEAD_DIM, axis=1)
    return jnp.where((lane >= HEAD_DIM) == odd, y, 0.0)


def _from_halves(even, odd, g):
    lane = lax.broadcasted_iota(jnp.int32, even.shape, 1)
    if g == 0:
        return jnp.where(lane < HEAD_DIM, even + pltpu.roll(odd, HEAD_DIM, axis=1), 0.0)
    return jnp.where(lane >= HEAD_DIM, pltpu.roll(even, HEAD_DIM, axis=1) + odd, 0.0)


_PARITIES = [(g, odd) for g in range(N_KV_HEADS) for odd in (False, True)]


def _softmax_sink(s, sink_ref, g, odd, group, n_heads, geo):
    dist, valid = geo
    pairs = group // 2
    heads = [g * group + 2 * p + int(odd) for p in range(pairs)]
    bias = jnp.concatenate([(2.0 ** (-8.0 * (h + 1) / n_heads)) * dist for h in heads], axis=0)
    sink = jnp.concatenate([jnp.full((BLOCK, 1), sink_ref[0, h], F32) for h in heads], axis=0)
    s = jnp.where(jnp.concatenate([valid] * pairs, axis=0), s - bias, -jnp.inf)
    m = jnp.maximum(jnp.max(s, axis=-1, keepdims=True), sink)
    p = jnp.exp(s - m)
    p_sink = jnp.exp(sink - m)
    inv = 1.0 / (jnp.sum(p, axis=-1, keepdims=True) + p_sink)
    return p * inv, p_sink * inv


def attn_fwd(proj, sinks, gain, aw, carry=None):
    t = proj.shape[0]
    kw = N_KV_HEADS * HEAD_DIM
    n_heads = aw // HEAD_DIM
    group = n_heads // N_KV_HEADS
    pairs = group // 2
    assert kw == LANES and group % 2 == 0
    nb = t // BLOCK
    scale = HEAD_DIM ** -0.5

    def body(sink_ref, q_ref, k_ref, v_ref, g_ref, o_ref, on_ref):
        n = pl.program_id(0)
        cur = pl.multiple_of(n * BLOCK, BLOCK)
        prev = pl.multiple_of(jnp.maximum(n - 1, 0) * BLOCK, BLOCK)
        geo = _attn_geometry(n > 0)
        kcat = jnp.concatenate([k_ref[pl.ds(prev, BLOCK), :], k_ref[pl.ds(cur, BLOCK), :]], axis=0)
        vcat = jnp.concatenate([v_ref[pl.ds(prev, BLOCK), :], v_ref[pl.ds(cur, BLOCK), :]], axis=0)
        q = q_ref[...] * scale
        groups = []
        for g in range(N_KV_HEADS):
            qs = _stack_pairs(q, g, pairs)
            o_pairs = None
            for odd in (False, True):
                s = _dot(qs, _to_half(kcat, g, odd), NT)
                p = _softmax_sink(s, sink_ref, g, odd, group, n_heads, geo)[0]
                o_half = _dot(p, _to_half(vcat, g, odd), NN)
                o_pairs = o_half if o_pairs is None else o_pairs + o_half
            groups.append(_unstack_pairs(o_pairs, pairs))
        o = jnp.concatenate(groups, axis=1)
        o_ref[...] = o
        on_ref[...] = (o * _rstd(o) * g_ref[...]).astype(BF16)

    return _call(body, name="attn_fwd", grid=(nb,),
                 in_specs=[pl.BlockSpec(memory_space=pltpu.SMEM),
                           pl.BlockSpec((BLOCK, aw), lambda n: (n, 0)),
                           pl.BlockSpec((t, kw), lambda n: (0, aw // kw)),
                           pl.BlockSpec((t, kw), lambda n: (0, aw // kw + 1)),
                           pl.BlockSpec((1, aw), lambda n: (0, 0))],
                 out_specs=[pl.BlockSpec((BLOCK, aw), lambda n: (n, 0)), pl.BlockSpec((BLOCK, aw), lambda n: (n, 0))],
                 out_shape=[jax.ShapeDtypeStruct((t, aw), F32), jax.ShapeDtypeStruct((t, aw), BF16)],
                 args=(sinks, proj, proj, proj, gain), sem=("parallel",), carry=carry)


def attn_bwd(proj, sinks, gain, attn_o, dcat, aw, carry=None):
    t = proj.shape[0]
    kw = N_KV_HEADS * HEAD_DIM
    n_heads = aw // HEAD_DIM
    group = n_heads // N_KV_HEADS
    pairs = group // 2
    assert kw == LANES and group % 2 == 0
    nb = t // BLOCK
    scale = HEAD_DIM ** -0.5

    def body(sink_ref, q_ref, k_ref, v_ref, g_ref, o_ref, dn_ref, dq_ref, dk_ref, dv_ref, dsink_ref, dg_ref):
        n = pl.program_id(0)
        cur = pl.multiple_of(n * BLOCK, BLOCK)
        prev = pl.multiple_of(jnp.maximum(n - 1, 0) * BLOCK, BLOCK)
        geo = _attn_geometry(n > 0)

        @pl.when(n == 0)
        def _():
            dk_ref[...] = jnp.zeros_like(dk_ref)
            dv_ref[...] = jnp.zeros_like(dv_ref)
            dsink_ref[...] = jnp.zeros_like(dsink_ref)

        o = o_ref[...]
        do_all, dg = _norm_bwd(o, g_ref[...], dn_ref[...])
        _accum(dg_ref, dg)
        kcat = jnp.concatenate([k_ref[pl.ds(prev, BLOCK), :], k_ref[pl.ds(cur, BLOCK), :]], axis=0)
        vcat = jnp.concatenate([v_ref[pl.ds(prev, BLOCK), :], v_ref[pl.ds(cur, BLOCK), :]], axis=0)
        q = q_ref[...] * scale
        lane = lax.broadcasted_iota(jnp.int32, (1, LANES), 1)
        lane_s = lax.broadcasted_iota(jnp.int32, (pairs * BLOCK, LANES), 1)
        qs = [_stack_pairs(q, g, pairs) for g in range(N_KV_HEADS)]
        dos = [_stack_pairs(do_all, g, pairs) for g in range(N_KV_HEADS)]
        kxs = [_to_half(kcat, g, odd) for g, odd in _PARITIES]
        scores = [_dot(qs[g], kx, NT) for kx, (g, odd) in zip(kxs, _PARITIES)]
        dps = [_dot(dos[g], _to_half(vcat, g, odd), NT) for g, odd in _PARITIES]
        deltas = []
        for g in range(N_KV_HEADS):
            prod = dos[g] * _stack_pairs(o, g, pairs)
            delta_even = jnp.sum(jnp.where(lane_s < HEAD_DIM, prod, 0.0), axis=-1, keepdims=True)
            deltas += [delta_even, jnp.sum(prod, axis=-1, keepdims=True) - delta_even]
        dsink = jnp.zeros((1, LANES), F32)
        ps, dss = [], []
        for i, (g, odd) in enumerate(_PARITIES):
            p, p_sink = _softmax_sink(scores[i], sink_ref, g, odd, group, n_heads, geo)
            ps.append(p)
            dss.append(p * (dps[i] - deltas[i]))
            sink_rows = p_sink * deltas[i]
            for pr in range(pairs):
                h = g * group + 2 * pr + int(odd)
                dsink = dsink + jnp.where(
                    lane == h, -jnp.sum(sink_rows[pr * BLOCK:(pr + 1) * BLOCK], axis=0, keepdims=True), 0.0)
        dq_pairs = [_dot(ds, kx, NN) for ds, kx in zip(dss, kxs)]
        dk_halves = [_dot(ds, qs[g], TN) for ds, (g, odd) in zip(dss, _PARITIES)]
        dv_halves = [_dot(p, dos[g], TN) for p, (g, odd) in zip(ps, _PARITIES)]
        dq_ref[...] = jnp.concatenate(
            [_unstack_pairs((dq_pairs[2 * g] + dq_pairs[2 * g + 1]) * scale, pairs) for g in range(N_KV_HEADS)],
            axis=1).astype(BF16)
        dk_upd = _from_halves(dk_halves[0], dk_halves[1], 0) + _from_halves(dk_halves[2], dk_halves[3], 1)
        dv_upd = _from_halves(dv_halves[0], dv_halves[1], 0) + _from_halves(dv_halves[2], dv_halves[3], 1)
        dk_ref[pl.ds(prev, BLOCK), :] += dk_upd[:BLOCK]
        dv_ref[pl.ds(prev, BLOCK), :] += dv_upd[:BLOCK]
        dk_ref[pl.ds(cur, BLOCK), :] += dk_upd[BLOCK:]
        dv_ref[pl.ds(cur, BLOCK), :] += dv_upd[BLOCK:]
        dsink_ref[...] += dsink

    return _call(body, name="attn_bwd", grid=(nb,),
                 in_specs=[pl.BlockSpec(memory_space=pltpu.SMEM),
                           pl.BlockSpec((BLOCK, aw), lambda n: (n, 0)),
                           pl.BlockSpec((t, kw), lambda n: (0, aw // kw)),
                           pl.BlockSpec((t, kw), lambda n: (0, aw // kw + 1)),
                           pl.BlockSpec((1, aw), lambda n: (0, 0)),
                           pl.BlockSpec((BLOCK, aw), lambda n: (n, 0)),
                           pl.BlockSpec((BLOCK, aw), lambda n: (n, 0))],
                 out_specs=[pl.BlockSpec((BLOCK, aw), lambda n: (n, 0)),
                            pl.BlockSpec((t, kw), lambda n: (0, 0)), pl.BlockSpec((t, kw), lambda n: (0, 0)),
                            pl.BlockSpec((1, LANES), lambda n: (0, 0)), pl.BlockSpec((1, aw), lambda n: (0, 0))],
                 out_shape=[jax.ShapeDtypeStruct((t, aw), BF16), jax.ShapeDtypeStruct((t, kw), F32),
                            jax.ShapeDtypeStruct((t, kw), F32), jax.ShapeDtypeStruct((1, LANES), F32),
                            jax.ShapeDtypeStruct((1, aw), F32)],
                 args=(sinks, proj, proj, proj, gain, attn_o, dcat), sem=("arbitrary",), carry=carry)


def _sigmoid(x):
    return 0.5 * jnp.tanh(0.5 * x) + 0.5


def _chunk_geometry():
    row = lax.broadcasted_iota(jnp.int32, (CHUNK, CHUNK), 0)
    col = lax.broadcasted_iota(jnp.int32, (CHUNK, CHUNK), 1)
    return row, col


def _cumsum_rows(x, reverse=False):
    row, col = _chunk_geometry()
    tri = (col >= row) if reverse else (col <= row)
    return lax.dot_general(tri.astype(F32), x, ((NN), ((), ())), precision=HI, preferred_element_type=F32)


def _rep_sub(x4, sub):
    k = x4.shape[-1]
    return jnp.broadcast_to(x4[:, None, :], (CHUNK // sub, sub, k)).reshape(CHUNK, k)


def _gates(q_r, f_r, lb):
    sg = _sigmoid(f_r)
    f = lb + (1.0 - lb) * sg
    sq = _sigmoid(q_r)
    return sg, f, sq, q_r * sq


def _offdiag_terms(b, j, sub):
    c = b[j * sub + sub - 1:j * sub + sub, :]
    return jnp.exp(jnp.minimum(b - c, 0.0)), jnp.exp(jnp.minimum(c - b, 0.0))


def _store_heads(ref, x):
    for j in range(ref.shape[0]):
        ref[j] = x[:, _head(j)]


def _sub_rows(ref, r, sub):
    rows = [ref[j, pl.ds(r, CHUNK // sub, stride=sub), :] for j in range(ref.shape[0])]
    return _rep_sub(jnp.concatenate(rows, axis=1), sub)


def _diag_mask(sub):
    row, col = _chunk_geometry()
    return jnp.logical_and((row // sub) == (col // sub), row >= col)


HGRN_HEADS_PER_STEP = 8


def _wide(refs):
    return jnp.concatenate([r[...] for r in refs], axis=1)


def _head(j):
    return slice(j * RNN_HEAD_DIM, (j + 1) * RNN_HEAD_DIM)


def _cat_heads(parts, hs):
    return jnp.concatenate([p[:, hs] for p in parts], axis=1)


def _offdiag_factors(q, k, b, sub):
    rowi = lax.broadcasted_iota(jnp.int32, b.shape, 0)
    qs, ks, ers, ecs = [], [], [], []
    for j in range(CHUNK // sub - 1):
        e_row, e_col = _offdiag_terms(b, j, sub)
        e_row = jnp.where(rowi >= (j + 1) * sub, e_row, 0.0)
        e_col = jnp.where((rowi // sub) == j, e_col, 0.0)
        qs.append(q * e_row)
        ks.append(k * e_col)
        ers.append(e_row)
        ecs.append(e_col)
    return qs, ks, ers, ecs


def hgrn_fwd(proj, attn_n, lb, norm_gain, col0, rw, carry=None):
    t, aw = attn_n.shape
    nh = rw // RNN_HEAD_DIM
    nc = t // CHUNK
    kd = RNN_HEAD_DIM
    cb = col0 // kd
    sub = SUB_FWD
    nsub = CHUNK // sub
    hp = nh
    assert nh <= HGRN_HEADS_PER_STEP
    w = hp * kd

    def body(*refs):
        q_refs, f_refs, i_refs, g_refs = (refs[i * hp:(i + 1) * hp] for i in range(4))
        lb_ref, ng_ref, an_ref, cat_ref, o_ref, att_ref, st_ref, state, b_ref, k_ref = refs[4 * hp:]
        c = pl.program_id(1)

        @pl.when(c == 0)
        def _():
            state[...] = jnp.zeros_like(state)

        st_ref[...] = state[...]
        q_r, f_r, v, g_r = (_wide(rs) for rs in (q_refs, f_refs, i_refs, g_refs))
        _, f, _, q = _gates(q_r, f_r, lb_ref[...])
        k = 1.0 - f
        b = _cumsum_rows(jnp.log(f))
        _store_heads(b_ref, b)
        _store_heads(k_ref, k)
        qcat, kcat, _, _ = _offdiag_factors(q, k, b, sub)
        row, col = _chunk_geometry()
        same = (row // sub) == (col // sub)
        rloc = lax.broadcasted_iota(jnp.int32, (CHUNK, w), 0) % sub
        diag = [jnp.zeros((CHUNK, CHUNK), F32)] * hp
        for r in range(sub):
            bs = _sub_rows(b_ref, r, sub)
            ks = _sub_rows(k_ref, r, sub)
            prod = q * jnp.exp(jnp.where(rloc >= r, b - bs, -jnp.inf)) * ks
            place = jnp.logical_and((col % sub) == r, same)
            diag = [jnp.where(place, jnp.sum(prod[:, _head(j)], axis=-1, keepdims=True), diag[j]) for j in range(hp)]
        b_last = b[CHUNK - 1:CHUNK, :]
        qe = q * jnp.exp(b)
        kdec = k * jnp.exp(b_last - b)
        decay = jnp.exp(b_last)
        outs, normed, states = [], [], []
        for j in range(hp):
            hs = _head(j)
            att = diag[j] + _dot(_cat_heads(qcat, hs), _cat_heads(kcat, hs), NT)
            att_ref[j] = att
            sj = state[j]
            o = _dot(qe[:, hs], sj, NT) + _dot(att, v[:, hs], NN)
            outs.append(o)
            normed.append(o * _rstd(o))
            states.append(sj * decay[:, hs] + _dot(v[:, hs], kdec[:, hs], TN))
        for j in range(hp):
            state[j] = states[j]
        o_ref[...] = jnp.concatenate(outs, axis=1)
        gate = g_r * _sigmoid(g_r)
        cat_ref[:, :aw] = an_ref[...]
        cat_ref[:, aw:] = (jnp.concatenate(normed, axis=1) * jnp.tile(ng_ref[...], (1, hp)) * gate).astype(BF16)

    def col(kidx, j):
        return pl.BlockSpec((CHUNK, kd), lambda hg, c: (c, cb + kidx * nh + hg * hp + j))

    return _call(body, name="hgrn_fwd", grid=(1, nc),
                 in_specs=[col(kidx, j) for kidx in range(4) for j in range(hp)] +
                          [pl.BlockSpec((1, w), lambda hg, c: (0, hg)), pl.BlockSpec((1, kd), lambda hg, c: (0, 0)),
                           pl.BlockSpec((CHUNK, aw), lambda hg, c: (c, 0))],
                 out_specs=[pl.BlockSpec((CHUNK, aw + w), lambda hg, c: (c, 0)),
                            pl.BlockSpec((CHUNK, w), lambda hg, c: (c, hg)),
                            pl.BlockSpec((hp, CHUNK, CHUNK), lambda hg, c: (hg, c, 0)),
                            pl.BlockSpec((None, hp, kd, kd), lambda hg, c: (c, hg, 0, 0))],
                 out_shape=[jax.ShapeDtypeStruct((t, aw + rw), BF16), jax.ShapeDtypeStruct((t, rw), F32),
                            jax.ShapeDtypeStruct((nh, t, CHUNK), F32), jax.ShapeDtypeStruct((nc, nh, kd, kd), F32)],
                 args=(*([proj] * (4 * hp)), lb, norm_gain, attn_n),
                 scratch_shapes=[pltpu.VMEM((hp, kd, kd), F32), pltpu.VMEM((hp, CHUNK, kd), F32),
                                 pltpu.VMEM((hp, CHUNK, kd), F32)],
                 sem=("parallel", "arbitrary"), carry=carry)


def hgrn_bwd(proj, lb, norm_gain, o_all, att_all, st_all, dcat, dq_a, dk_a, dv_a, col0, rw, carry=None):
    t, iw = proj.shape
    aw, kw = dq_a.shape[1], dk_a.shape[1]
    nh = rw // RNN_HEAD_DIM
    nc = t // CHUNK
    kd = RNN_HEAD_DIM
    cb = col0 // kd
    sub = SUB_BWD
    nsub = CHUNK // sub
    dcb = (dcat.shape[1] - rw) // kd
    hp = nh
    assert nh <= HGRN_HEADS_PER_STEP and dcb % hp == 0 and col0 == aw + 2 * kw and iw == col0 + 4 * rw
    w = hp * kd

    def per_head(x, fn):
        return jnp.concatenate([jnp.broadcast_to(fn(x[:, _head(j)]), (CHUNK, kd)) for j in range(hp)], axis=1)

    def body(*refs):
        q_refs, f_refs, i_refs, g_refs = (refs[i * hp:(i + 1) * hp] for i in range(4))
        (lb_ref, ng_ref, o_ref, att_ref, st0_ref, st1_ref, d_ref, dqa_ref, dka_ref, dva_ref, dp_ref, dlb_ref, dng_ref,
         dstate, b_ref, k_ref, dks_ref) = refs[4 * hp:]
        ci = pl.program_id(1)

        @pl.when(ci == 0)
        def _():
            dstate[...] = jnp.zeros_like(dstate)
            dlb_ref[...] = jnp.zeros_like(dlb_ref)
            dng_ref[...] = jnp.zeros_like(dng_ref)

        lbv = lb_ref[...]
        q_r, f_r, v, g_r = (_wide(rs) for rs in (q_refs, f_refs, i_refs, g_refs))
        sg, f, sq, q = _gates(q_r, f_r, lbv)
        k = 1.0 - f
        b = _cumsum_rows(jnp.log(f))
        _store_heads(b_ref, b)
        _store_heads(k_ref, k)
        row, col = _chunk_geometry()

        o = o_ref[...]
        ng = jnp.tile(ng_ref[...], (1, hp))
        sgg = _sigmoid(g_r)
        gate = g_r * sgg
        d_rnn = d_ref[...]
        r = per_head(o, _rstd)
        oh = o * r
        dp_ref[:, :aw] = dqa_ref[...]
        dp_ref[:, aw:aw + kw] = dka_ref[...].astype(BF16)
        dp_ref[:, aw + kw:col0] = dva_ref[...].astype(BF16)
        dp_ref[:, col0 + 3 * rw:] = (d_rnn * oh * ng * (sgg * (1.0 + g_r * (1.0 - sgg)))).astype(BF16)
        d_on = d_rnn * gate
        dng_rows = jnp.sum(d_on * oh, axis=0, keepdims=True)
        dng = dng_rows[:, _head(0)]
        for j in range(1, hp):
            dng = dng + dng_rows[:, _head(j)]
        dng_ref[...] += dng
        dyg = d_on * ng
        do = r * (dyg - oh * per_head(dyg * oh, lambda x: jnp.mean(x, axis=-1, keepdims=True)))

        b_last = b[CHUNK - 1:CHUNK, :]
        eb = jnp.exp(b)
        tail = jnp.exp(b_last - b)
        kdec = k * tail
        decay = jnp.exp(b_last)
        qe = q * eb
        qcat, kcat, ers, ecs = _offdiag_factors(q, k, b, sub)
        diag_mask = _diag_mask(sub)
        dqs, dks, dvs, dads, gsums, dstates = [], [], [], [], [], []
        for j in range(hp):
            hs = _head(j)
            do_h, v_h, dst = do[:, hs], v[:, hs], dstate[j]
            da = jnp.where(row >= col, _dot(do_h, v_h, NT), 0.0)
            dads.append(jnp.where(diag_mask, da, 0.0))
            dq = _dot(do_h, st0_ref[j], NN) * eb[:, hs]
            dk = _dot(v_h, dst, NN) * tail[:, hs]
            dvs.append(_dot(att_ref[j], do_h, TN) + _dot(kdec[:, hs], dst, NT))
            rq = _dot(da, _cat_heads(kcat, hs), NN)
            rk = _dot(da, _cat_heads(qcat, hs), TN)
            for jj in range(nsub - 1):
                dq = dq + ers[jj][:, hs] * rq[:, _head(jj)]
                dk = dk + ecs[jj][:, hs] * rk[:, _head(jj)]
            dqs.append(dq)
            dks.append(dk)
            gsums.append(jnp.sum(dst * st1_ref[j], axis=0, keepdims=True))
            dstates.append(dst * decay[:, hs] + _dot(do_h, qe[:, hs], TN))
        for j in range(hp):
            dstate[j] = dstates[j]
        dq = jnp.concatenate(dqs, axis=1)
        dk = jnp.concatenate(dks, axis=1)
        rloc = lax.broadcasted_iota(jnp.int32, (CHUNK, w), 0) % sub
        for rr in range(sub):
            bs = _sub_rows(b_ref, rr, sub)
            ks = _sub_rows(k_ref, rr, sub)
            e = jnp.exp(jnp.where(rloc >= rr, b - bs, -jnp.inf))
            pick = (col % sub) == rr
            dacol = jnp.concatenate(
                [jnp.broadcast_to(jnp.sum(jnp.where(pick, dads[j], 0.0), axis=-1, keepdims=True), (CHUNK, kd))
                 for j in range(hp)], axis=1)
            wv = dacol * e
            dq = dq + wv * ks
            sums = jnp.sum((wv * q).reshape(nsub, sub, w), axis=1)
            for j in range(hp):
                dks_ref[j, pl.ds(rr, nsub, stride=sub), :] = sums[:, _head(j)]
        dk = dk + jnp.concatenate([dks_ref[j] for j in range(hp)], axis=1)

        dlf = _cumsum_rows(q * dq - k * dk, reverse=True) + jnp.concatenate(gsums, axis=1)
        dfv = dlf / f - dk
        dp_ref[:, col0 + rw:col0 + 2 * rw] = (dfv * (1.0 - lbv) * sg * (1.0 - sg)).astype(BF16)
        dlb_ref[...] += jnp.sum(dfv * (1.0 - sg), axis=0, keepdims=True)
        dp_ref[:, col0:col0 + rw] = (dq * (sq * (1.0 + q_r * (1.0 - sq)))).astype(BF16)
        dp_ref[:, col0 + 2 * rw:col0 + 3 * rw] = jnp.concatenate(dvs, axis=1).astype(BF16)

    def rev(c):
        return nc - 1 - c

    def col_in(kidx, j):
        return pl.BlockSpec((CHUNK, kd), lambda hg, c: (rev(c), cb + kidx * nh + hg * hp + j))

    def rows(width):
        return pl.BlockSpec((CHUNK, width), lambda hg, c: (rev(c), 0))

    return _call(body, name="hgrn_bwd", grid=(1, nc),
                 in_specs=[col_in(kidx, j) for kidx in range(4) for j in range(hp)] +
                          [pl.BlockSpec((1, w), lambda hg, c: (0, hg)), pl.BlockSpec((1, kd), lambda hg, c: (0, 0)),
                           rows(w),
                           pl.BlockSpec((hp, CHUNK, CHUNK), lambda hg, c: (hg, rev(c), 0)),
                           pl.BlockSpec((None, hp, kd, kd), lambda hg, c: (rev(c), hg, 0, 0)),
                           pl.BlockSpec((None, hp, kd, kd),
                                        lambda hg, c: (jnp.minimum(rev(c) + 1, nc - 1), hg, 0, 0)),
                           pl.BlockSpec((CHUNK, w), lambda hg, c: (rev(c), dcb // hp + hg)),
                           rows(aw), rows(kw), rows(kw)],
                 out_specs=[rows(iw),
                            pl.BlockSpec((1, w), lambda hg, c: (0, hg)),
                            pl.BlockSpec((None, 1, kd), lambda hg, c: (hg, 0, 0))],
                 out_shape=[jax.ShapeDtypeStruct((t, iw), BF16), jax.ShapeDtypeStruct((1, rw), F32),
                            jax.ShapeDtypeStruct((1, 1, kd), F32)],
                 args=(*([proj] * (4 * hp)), lb, norm_gain, o_all, att_all, st_all, st_all, dcat, dq_a, dk_a, dv_a),
                 scratch_shapes=[pltpu.VMEM((hp, kd, kd), F32), pltpu.VMEM((hp, CHUNK, kd), F32),
                                 pltpu.VMEM((hp, CHUNK, kd), F32), pltpu.VMEM((hp, CHUNK, kd), F32)],
                 sem=("parallel", "arbitrary"), carry=carry)


def comm_only(name, part):
    return _call(lambda: None, name=name, grid=(), in_specs=[], out_specs=[], out_shape=[], args=(), carry=part)[1]


ADD_BLOCK_ELEMS = 1 << 20
ADAMW_BLOCK_ELEMS = 1 << 19


def add_kept_half(name, kept, got, sel, minor, row0=0):
    pieces, rows, cols = got.shape
    tr = _tile(rows, max(16, ADD_BLOCK_ELEMS // cols), mult=16)
    assert row0 % tr == 0
    i0 = row0 // tr

    def body(sel_ref, k_ref, g_ref, o_ref):
        o_ref[...] = (k_ref[...].astype(F32) + g_ref[...].astype(F32)).astype(o_ref.dtype)

    kept_spec = (pl.BlockSpec((None, None, tr, cols), lambda p, i, s: (p, s[0], i + i0, 0)) if minor else
                 pl.BlockSpec((None, None, tr, cols), lambda p, i, s: (s[0], p, i + i0, 0)))
    return pl.pallas_call(
        body, name=name,
        grid_spec=pltpu.PrefetchScalarGridSpec(
            num_scalar_prefetch=1, grid=(pieces, rows // tr),
            in_specs=[kept_spec, pl.BlockSpec((None, tr, cols), lambda p, i, s: (p, i, 0))],
            out_specs=pl.BlockSpec((None, tr, cols), lambda p, i, s: (p, i, 0))),
        out_shape=jax.ShapeDtypeStruct(got.shape, got.dtype),
        compiler_params=_cparams(("parallel", "parallel")),
    )(sel, kept, got)


def _adamw(w, g, m, v):
    m = ADAM_B1 * m + (1.0 - ADAM_B1) * g
    v = ADAM_B2 * v + (1.0 - ADAM_B2) * (g * g)
    m_hat = m / (1.0 - ADAM_B1 ** ADAM_STEP)
    v_hat = v / (1.0 - ADAM_B2 ** ADAM_STEP)
    delta = -ADAM_LR * (m_hat / (jnp.sqrt(v_hat) + ADAM_EPS) + ADAM_WD * w)
    return delta, m, v


def add_adamw(name, kept, got, sel, w, m, v, row0=0, into=None):
    _, rows, cols = got.shape
    tr = _tile(rows, max(16, ADAMW_BLOCK_ELEMS // cols), mult=16)
    assert row0 % tr == 0
    i0 = row0 // tr
    n_into = 0 if into is None else len(into)

    def body(sel_ref, k_ref, g_ref, w_ref, m_ref, v_ref, *rest):
        go_ref, d_ref, mo_ref, vo_ref = rest[n_into:]
        g = k_ref[...].astype(F32) + g_ref[...].astype(F32)
        go_ref[...] = g
        d_ref[...], mo_ref[...], vo_ref[...] = _adamw(w_ref[...], g, m_ref[...], v_ref[...])

    shard_tile = pl.BlockSpec((tr, cols), lambda i, s: (i + i0, 0))
    return pl.pallas_call(
        body, name=name,
        grid_spec=pltpu.PrefetchScalarGridSpec(
            num_scalar_prefetch=1, grid=(rows // tr,),
            in_specs=[pl.BlockSpec((None, None, tr, cols), lambda i, s: (s[0], 0, i, 0)),
                      pl.BlockSpec((None, tr, cols), lambda i, s: (0, i, 0)), shard_tile, shard_tile, shard_tile,
                      *[ANY] * n_into],
            out_specs=[shard_tile] * 4),
        out_shape=[jax.ShapeDtypeStruct(w.shape, F32)] * 4,
        input_output_aliases={6 + k: k for k in range(n_into)},
        compiler_params=_cparams(("parallel",)),
    )(sel, kept, got, w, m, v, *(into or ()))


def small_allreduce_adamw(partial, scale, w, m, v):
    rows = partial.shape[0]

    def body(p_ref, s_ref, w_ref, m_ref, v_ref, g_ref, d_ref, mo_ref, vo_ref, slots, send_sems, recv_sems):
        x, y, c = _coords()
        my_slot = _slab_index((x, y, c))
        slots[my_slot] = p_ref[...]
        copies = []
        for mask in range(1, N_DEV):
            to = tuple(1 - v_ if (mask >> s_) & 1 else v_ for v_, s_ in ((x, 2), (y, 1), (c, 0)))
            copies.append(pltpu.make_async_remote_copy(
                src_ref=p_ref, dst_ref=slots.at[my_slot],
                send_sem=send_sems.at[mask - 1], recv_sem=recv_sems.at[mask - 1],
                device_id=to, device_id_type=MESH))
        for cp in copies:
            cp.start()
        for cp in copies:
            cp.wait()
        total = slots[0]
        for b in range(1, N_DEV):
            total = total + slots[b]
        g = total * s_ref[...]
        g_ref[...] = g
        d_ref[...], mo_ref[...], vo_ref[...] = _adamw(w_ref[...], g, m_ref[...], v_ref[...])

    vm = pl.BlockSpec(memory_space=pltpu.VMEM)
    return pl.pallas_call(
        body, name="small_allreduce_adamw",
        in_specs=[vm] * 5, out_specs=[vm] * 4,
        out_shape=[jax.ShapeDtypeStruct((rows, LANES), F32)] * 4,
        scratch_shapes=[pltpu.VMEM((N_DEV, rows, LANES), F32),
                        pltpu.SemaphoreType.DMA((N_DEV - 1,)), pltpu.SemaphoreType.DMA((N_DEV - 1,))],
        compiler_params=pltpu.CompilerParams(has_side_effects=True),
    )(partial, scale, w, m, v)


_SMALL = ("attn_sinks", "attn_out_gain", "rnn_lb_logits", "rnn_norm_gain", "mix_pre_gain", "mix_post_gain",
          "mlp_pre_gain", "mlp_post_gain")


def _pack(parts):
    rows = []
    for p in parts:
        flat = p.reshape(-1).astype(F32)
        pad = (-flat.shape[0]) % LANES
        rows.append(jnp.pad(flat, (0, pad)).reshape(-1, LANES))
    packed = jnp.concatenate(rows, axis=0)
    pad_rows = (-packed.shape[0]) % 8
    return jnp.pad(packed, ((0, pad_rows), (0, 0)))


def _unpack(packed, shapes):
    out, r = [], 0
    for s in shapes:
        size = math.prod(s)
        nrows = -(-size // LANES)
        out.append(packed[r:r + nrows].reshape(-1)[:size].reshape(s))
        r += nrows
    return out


class _Scatter:
    def __init__(self, tag, grad, sels, both_links=False):
        self.tag, self.sels, self.both = tag, sels, both_links
        self.shape = grad.shape[1:]
        self.half = self.shape[0] // 2
        self.cur = grad.reshape(4, 2, *self.shape)
        self.stage = 0

    def step(self):
        if self.stage == 0 or not self.both:
            return _scatter_step(self.cur, "cxy"[self.stage])
        if self.stage == 1:
            return _merge(_scatter_step(self.cur, "x", rows=(0, self.half)),
                          _scatter_step(self.cur, "y", minor=True, rows=(self.half, self.shape[0])))
        upper, lower = self.cur
        return _merge(_scatter_step(upper, "y"), _scatter_step(lower, "x"))

    def land(self, got, w=None, m=None, v=None):
        stage, tag, sels = self.stage, self.tag, self.sels
        self.stage += 1
        if stage == 0 or not self.both:
            axis = "cxy"[stage]
            name = "rs_add_%s_%s" % (axis, tag)
            if axis == "y":
                return add_adamw(name, self.cur, got, sels[axis], w, m, v)
            summed = add_kept_half(name, self.cur, got, sels[axis], minor=axis == "c")
            self.cur = summed.reshape(2, summed.shape[0] // 2, *self.shape)
            return None
        got_upper, got_lower = got
        if stage == 1:
            upper = add_kept_half("rs_add_x_%s_upper" % tag, self.cur, got_upper, sels["x"], minor=False)
            lower = add_kept_half("rs_add_y_%s_lower" % tag, self.cur, got_lower, sels["y"], minor=True,
                                  row0=self.half)
            self.cur = tuple(s.reshape(2, 1, *s.shape[1:]) for s in (upper, lower))
            return None
        upper, lower = self.cur
        out_upper = add_adamw("rs_add_y_%s_upper" % tag, upper, got_upper, sels["y"], w, m, v)
        return add_adamw("rs_add_x_%s_lower" % tag, lower, got_lower, sels["x"], w, m, v, row0=self.half,
                         into=out_upper)


def kernel(x, w_in, attn_sinks, attn_out_gain, rnn_lb_logits, rnn_norm_gain, w_out, mix_pre_gain, mix_post_gain, mlp_pre_gain, mlp_post_gain, w_up, w_down, loss_target, m_w_in, m_attn_sinks, m_attn_out_gain, m_rnn_lb_logits, m_rnn_norm_gain, m_w_out, m_mix_pre_gain, m_mix_post_gain, m_mlp_pre_gain, m_mlp_post_gain, m_w_up, m_w_down, v_w_in, v_attn_sinks, v_attn_out_gain, v_rnn_lb_logits, v_rnn_norm_gain, v_w_out, v_mix_pre_gain, v_mix_post_gain, v_mlp_pre_gain, v_mlp_post_gain, v_w_up, v_w_down):
    xs, target = x[0], loss_target[0]
    t, d = xs.shape
    aw = d // 2
    rw = d - aw
    col0 = aw + 2 * N_KV_HEADS * HEAD_DIM
    small_w = dict(attn_sinks=attn_sinks, attn_out_gain=attn_out_gain, rnn_lb_logits=rnn_lb_logits,
                   rnn_norm_gain=rnn_norm_gain, mix_pre_gain=mix_pre_gain, mix_post_gain=mix_post_gain,
                   mlp_pre_gain=mlp_pre_gain, mlp_post_gain=mlp_post_gain)
    small_m = dict(attn_sinks=m_attn_sinks, attn_out_gain=m_attn_out_gain, rnn_lb_logits=m_rnn_lb_logits,
                   rnn_norm_gain=m_rnn_norm_gain, mix_pre_gain=m_mix_pre_gain, mix_post_gain=m_mix_post_gain,
                   mlp_pre_gain=m_mlp_pre_gain, mlp_post_gain=m_mlp_post_gain)
    small_v = dict(attn_sinks=v_attn_sinks, attn_out_gain=v_attn_out_gain, rnn_lb_logits=v_rnn_lb_logits,
                   rnn_norm_gain=v_rnn_norm_gain, mix_pre_gain=v_mix_pre_gain, mix_post_gain=v_mix_post_gain,
                   mlp_pre_gain=v_mlp_pre_gain, mlp_post_gain=v_mlp_post_gain)
    cx, cy, cc = _coords()
    sels = {a: jnp.reshape(v_, (1,)).astype(jnp.int32) for a, v_ in (("x", cx), ("y", cy), ("c", cc))}

    w_in_t, m_in_t, v_in_t = w_in[0].T, m_w_in[0].T, v_w_in[0].T
    s_in, s_out, s_up, s_down = (w.astype(BF16) for w in (w_in_t, w_out[0], w_up[0], w_down[0]))
    probs = jax.nn.softmax(rnn_lb_logits.astype(F32), axis=0)
    lb = probs[0:1]

    (h1,), (wint_part,) = pre_norm(xs, mix_pre_gain, carry=_gather_first(s_in, diagonal=False))
    in_rows = s_in.shape[0]
    wint = comm_only("gather_rest_w_in", _pass_slabs(
        wint_part,
        [(_X, _Y, (0, in_rows // 2)), (_Y, _X, (in_rows // 2, in_rows)), (_X, _C, None), (_Y, _C, None)],
        then=[(_XY, _C, None)]))[0].reshape(-1, d)
    up_rows = s_up.shape[0]
    up_cut = up_rows * 9 // 16
    proj, (wup_part,) = mm_nt("in_proj", h1, wint, F32, tn=2 * MM_TILE,
                              carry=_gather_first(s_up, rows=(0, up_cut)))
    (attn_o, attn_n), (wup_half, wout_half) = attn_fwd(
        proj, attn_sinks, attn_out_gain, aw,
        carry=_merge(_gather_first(s_up, rows=(up_cut, up_rows), into=wup_part), _gather_first(s_out)))
    (cat, o_r, att, st), (wup, wout, wdown_half) = hgrn_fwd(
        proj, attn_n, lb, rnn_norm_gain, col0, rw,
        carry=_merge(_gather_second(wup_half), _gather_second(wout_half), _gather_first(s_down)))
    wout = wout.reshape(-1, d)
    mixed, (wdown,) = mm_nn("out_proj", cat, wout, F32, carry=_gather_second(wdown_half))
    wdown = wdown.reshape(-1, d)
    x1, h2 = mid_fwd(mixed, mix_post_gain, xs, mlp_pre_gain)
    u = up_proj(h2, wup)
    y = down_proj(u, wdown)
    sse, dout, dy, dg_mlppost = loss_bwd(y, mlp_post_gain, x1, target)

    du = down_bwd_act(dy, wdown, u)
    rs_down = _Scatter("down", down_wgrad(u, dy).reshape(N_DEV, -1, d), sels, both_links=True)
    dh2, (got,) = up_bwd_x(du, wup, carry=rs_down.step())
    rs_down.land(got)
    dwup, gots = up_wgrad(h2, du, carry=rs_down.step())
    rs_down.land(gots)
    rs_up = _Scatter("up", dwup, sels, both_links=True)
    (dx1, dmixed, dg_mlppre, dg_mixpost), (got,) = mid_bwd(dh2, x1, mlp_pre_gain, dout, mixed, mix_post_gain,
                                                          carry=rs_up.step())
    rs_up.land(got)
    dcat = mm_nt("out_bwd_x", dmixed, wout, F32)
    rs_out = _Scatter("out", mm_tn("out_wgrad", cat, dmixed, BF16).reshape(N_DEV, -1, d), sels)
    (dq_a, dk_a, dv_a, dsinks, daog), (*gots, got_o) = attn_bwd(
        proj, attn_sinks, attn_out_gain, attn_o, dcat, aw, carry=_merge(rs_down.step(), rs_out.step()))
    out_down = rs_down.land(gots, w_down[0], m_w_down[0], v_w_down[0])
    rs_out.land(got_o)
    (dproj, dlb, dng), (*gots, got_o) = hgrn_bwd(
        proj, lb, rnn_norm_gain, o_r, att, st, dcat, dq_a, dk_a, dv_a, col0, rw,
        carry=_merge(rs_up.step(), rs_out.step()))
    rs_up.land(gots)
    rs_out.land(got_o)
    dwin, (*gots, got_o) = mm_tn("in_wgrad", dproj, h1, BF16, carry=_merge(rs_up.step(), rs_out.step()))
    out_up = rs_up.land(gots, w_up[0], m_w_up[0], v_w_up[0])
    out_out = rs_out.land(got_o, w_out[0], m_w_out[0], v_w_out[0])
    rs_in = _Scatter("in", dwin.reshape(N_DEV, -1, d), sels, both_links=True)
    rs_in.land(comm_only("rs_exchange_c_in", rs_in.step())[0])
    dh1, gots = mm_nn("in_bwd_x", dproj, wint, F32, tm=MM_TILE // 2, carry=rs_in.step())
    rs_in.land(gots)
    grad_x, dg_mixpre = first_bwd(dh1, xs, mix_pre_gain, dx1)
    out_in = rs_in.land(comm_only("rs_exchange_last_in", rs_in.step()), w_in_t, m_in_t, v_in_t)
    big_out = [out_in, out_out, out_up, out_down]

    n_heads = attn_sinks.shape[1]
    jac = probs[0] * probs[1]
    partial = _pack([sse, dsinks[0, :n_heads], daog, jnp.stack([dlb[0], dlb[0]]), jnp.sum(dng, axis=0),
                     dg_mixpre, dg_mixpost, dg_mlppre, dg_mlppost])
    ones = [jnp.ones(small_w[k].shape, F32) for k in _SMALL]
    ones[2] = jnp.stack([jac, -jac])
    scale = _pack([jnp.full((1,), 0.5 / d, F32)] + ones)
    zero = jnp.zeros((1,), F32)
    outs = small_allreduce_adamw(partial, scale, _pack([zero] + [small_w[k] for k in _SMALL]),
                                 _pack([zero] + [small_m[k] for k in _SMALL]),
                                 _pack([jnp.ones((1,), F32)] + [small_v[k] for k in _SMALL]))
    shapes = [(1,)] + [small_w[k].shape for k in _SMALL]
    sgrad, sdelta, snm, snv = (_unpack(o, shapes) for o in outs)
    loss = sgrad[0][0]

    def big(i, j):
        o = big_out[i][j]
        return (o.T if i == 0 else o)[None]

    def ordered(j, smalls):
        s = dict(zip(_SMALL, smalls[1:]))
        return [big(0, j), s["attn_sinks"], s["attn_out_gain"], s["rnn_lb_logits"], s["rnn_norm_gain"], big(1, j),
                s["mix_pre_gain"], s["mix_post_gain"], s["mlp_pre_gain"], s["mlp_post_gain"], big(2, j), big(3, j)]

    return (loss, grad_x[None], *ordered(0, sgrad), *ordered(1, sdelta), *ordered(2, snm), *ordered(3, snv))
```

```python
import math

import jax
import jax.numpy as jnp
from jax import lax
from jax.experimental import pallas as pl
from jax.experimental.pallas import tpu as pltpu

F32 = jnp.float32
BF16 = jnp.bfloat16

HEAD_DIM = 64
N_KV_HEADS = 2
BLOCK = 128
RNN_HEAD_DIM = 128
CHUNK = 64
SUB_FWD = 16
SUB_BWD = 16
EPS = 1e-6

ADAM_LR = 0.001
ADAM_B1 = 0.9
ADAM_B2 = 0.999
ADAM_EPS = 1e-08
ADAM_WD = 0.01
ADAM_STEP = 10

N_DEV = 8
LANES = 128
V7X_VMEM_LIMIT = 56 * 1024 * 1024
MESH = pl.DeviceIdType.MESH
HI = lax.Precision.HIGHEST
ANY = pl.BlockSpec(memory_space=pl.ANY)
_AXES = ("x", "y", "c")


def _cparams(sem=None, **kw):
    return pltpu.CompilerParams(dimension_semantics=sem, vmem_limit_bytes=V7X_VMEM_LIMIT, **kw)


def _dot(a, b, dims):
    return lax.dot_general(a.astype(BF16), b.astype(BF16), (dims, ((), ())), preferred_element_type=F32)


NN = ((1,), (0,))
NT = ((1,), (1,))
TN = ((0,), (0,))


def _pick(n, pref):
    t = min(n, pref)
    while n % t:
        t //= 2
    return t


def _tile(n, pref, mult=LANES):
    if n <= pref:
        return n
    t = pref - pref % mult
    while n % t:
        t -= mult
    return t


def _coords():
    return lax.axis_index("x"), lax.axis_index("y"), lax.axis_index("c")


def _slab_index(dev):
    return 4 * dev[0] + 2 * dev[1] + dev[2]


class _Part:
    def __init__(self, operands, landings, aliases, n_sems, plan):
        self.operands, self.landings, self.aliases, self.n_sems, self.plan = operands, landings, aliases, n_sems, plan


def _merge(*parts):
    operands, landings, aliases, plans = [], [], {}, []
    s0 = 0
    for p in parts:
        o0, l0 = len(operands), len(landings)
        aliases.update({o0 + i: l0 + j for i, j in p.aliases.items()})
        plans.append((p.plan, o0, len(p.operands), l0, len(p.landings), s0))
        operands += p.operands
        landings += p.landings
        s0 += p.n_sems

    def plan(ops, lands, sem):
        starts, waits = [], []
        for f, o0, no, l0, nl, off in plans:
            s, w = f(ops[o0:o0 + no], lands[l0:l0 + nl], lambda kind, k, off=off: sem(kind, off + k))
            starts += s
            waits += w
        return starts, waits

    return _Part(operands, landings, aliases, s0, plan)


def _gather_peers(x, y, c):
    return [(x, y, 1 - c), (1 - x, y, c), (x, 1 - y, c), (1 - x, 1 - y, c)]


def _gather_first(shard, rows=None, into=None, diagonal=True):
    lo, hi = (0, shard.shape[0]) if rows is None else rows
    n_peers = 4 if diagonal else 3

    def plan(ops, lands, sem):
        x, y, c = _coords()
        me, peers = (x, y, c), _gather_peers(x, y, c)[:n_peers]
        src = ops[0].at[pl.ds(lo, hi - lo)]

        def slab(block):
            return lands[0].at[_slab_index(block), pl.ds(lo, hi - lo)]

        def cp(k, block, to):
            return pltpu.make_async_remote_copy(
                src_ref=src, dst_ref=slab(block),
                send_sem=sem(0, k), recv_sem=sem(1, k), device_id=to, device_id_type=MESH)

        local = pltpu.make_async_copy(src, slab(me), sem(2, 0))
        sends = [cp(k, me, to) for k, to in enumerate(peers)]
        recvs = [cp(k, frm, me) for k, frm in enumerate(peers)]
        return ([local.start] + [s.start for s in sends],
                [local.wait] + [s.wait_send for s in sends] + [r.wait_recv for r in recvs])

    landing = jax.ShapeDtypeStruct((N_DEV, *shard.shape), shard.dtype)
    if into is None:
        return _Part([shard], [landing], {}, 4, plan)
    return _Part([shard, into], [landing], {1: 0}, 4, plan)


def _flip(dev, flips):
    return tuple(1 - v if f else v for v, f in zip(dev, flips))


def _pass_slabs(gathered, moves, then=()):
    def wave(lands, sem, k0, wave_moves):
        me = _coords()
        sends, recvs = [], []
        for k, (block, dest, rows) in enumerate(wave_moves, start=k0):
            lo, hi = (0, gathered.shape[1]) if rows is None else rows

            def cp(blk, to, k=k, lo=lo, hi=hi):
                slab = lands[0].at[_slab_index(blk), pl.ds(lo, hi - lo)]
                return pltpu.make_async_remote_copy(
                    src_ref=slab, dst_ref=slab, send_sem=sem(0, k), recv_sem=sem(1, k),
                    device_id=to, device_id_type=MESH)

            sends.append(cp(_flip(me, block), _flip(me, dest)))
            recvs.append(cp(_flip(_flip(me, dest), block), me))
        return [s.start for s in sends], [s.wait_send for s in sends] + [r.wait_recv for r in recvs]

    def plan(ops, lands, sem):
        starts, waits = wave(lands, sem, 0, moves)
        if then:
            starts2, waits2 = wave(lands, sem, len(moves), then)
            waits = waits + starts2 + waits2
        return starts, waits

    return _Part([gathered], [jax.ShapeDtypeStruct(gathered.shape, gathered.dtype)], {0: 0},
                 len(moves) + len(then), plan)


_X, _Y, _C, _XY = (1, 0, 0), (0, 1, 0), (0, 0, 1), (1, 1, 0)


def _gather_second(gathered):
    def plan(ops, lands, sem):
        x, y, c = _coords()
        sibling = (x, y, 1 - c)
        chips = [(1 - x, y), (x, 1 - y), (1 - x, 1 - y)]

        def cp(k, block):
            slab = lands[0].at[_slab_index(block)]
            return pltpu.make_async_remote_copy(
                src_ref=slab, dst_ref=slab, send_sem=sem(0, k), recv_sem=sem(1, k),
                device_id=sibling, device_id_type=MESH)

        sends = [cp(k, (*chip, c)) for k, chip in enumerate(chips)]
        recvs = [cp(k, (*chip, 1 - c)) for k, chip in enumerate(chips)]
        return [s.start for s in sends], [s.wait_send for s in sends] + [r.wait_recv for r in recvs]

    return _Part([gathered], [jax.ShapeDtypeStruct(gathered.shape, gathered.dtype)], {0: 0}, 3, plan)


def _scatter_step(array, axis, minor=None, rows=None):
    minor = (axis == "c") if minor is None else minor
    pieces = array.shape[0] if minor else array.shape[1]
    lo, hi = (0, array.shape[2]) if rows is None else rows

    def plan(ops, lands, sem):
        coords = list(_coords())
        ai = _AXES.index(axis)
        mine = coords[ai]
        peer = list(coords)
        peer[ai] = 1 - mine
        cps = []
        for p in range(pieces):
            src = ops[0].at[p, 1 - mine, pl.ds(lo, hi - lo)] if minor else ops[0].at[1 - mine, p, pl.ds(lo, hi - lo)]
            cps.append(pltpu.make_async_remote_copy(
                src_ref=src, dst_ref=lands[0].at[p], send_sem=sem(0, p), recv_sem=sem(1, p),
                device_id=tuple(peer), device_id_type=MESH))
        return [cp.start for cp in cps], [cp.wait for cp in cps]

    return _Part([array], [jax.ShapeDtypeStruct((pieces, hi - lo, array.shape[3]), array.dtype)], {}, pieces, plan)


def _grid_edges(grid):
    first = last = None
    for ax, n in enumerate(grid):
        p = pl.program_id(ax)
        f, l = p == 0, p == n - 1
        first = f if first is None else jnp.logical_and(first, f)
        last = l if last is None else jnp.logical_and(last, l)
    return first, last


def _call(body, *, name, grid, in_specs, out_specs, out_shape, args, scratch_shapes=(), sem=None, carry=None):
    if carry is None:
        return pl.pallas_call(
            body, name=name, grid=grid, in_specs=list(in_specs), out_specs=list(out_specs),
            out_shape=list(out_shape), scratch_shapes=list(scratch_shapes), compiler_params=_cparams(sem),
        )(*args)
    n_in, n_out, n_scr = len(in_specs), len(out_specs), len(scratch_shapes)
    n_cin, n_cout = len(carry.operands), len(carry.landings)

    def wrapped(*refs):
        ins, cins = refs[:n_in], refs[n_in:n_in + n_cin]
        o0 = n_in + n_cin
        outs, couts = refs[o0:o0 + n_out], refs[o0 + n_out:o0 + n_out + n_cout]
        s0 = o0 + n_out + n_cout
        scr, sems = refs[s0:s0 + n_scr], refs[s0 + n_scr:]
        first, last = _grid_edges(grid)

        def plan():
            return carry.plan(cins, couts, lambda kind, k: sems[kind].at[k])

        def start_all():
            for start in plan()[0]:
                start()

        def wait_all():
            for wait in plan()[1]:
                wait()

        if grid:
            pl.when(first)(start_all)
            body(*ins, *outs, *scr)
            pl.when(last)(wait_all)
        else:
            start_all()
            body(*ins, *outs, *scr)
            wait_all()

    sem_arrays = [pltpu.SemaphoreType.DMA((carry.n_sems,))] * 3
    res = pl.pallas_call(
        wrapped, name=name, grid=grid,
        in_specs=[*in_specs, *[ANY] * n_cin], out_specs=[*out_specs, *[ANY] * n_cout],
        out_shape=[*out_shape, *carry.landings],
        scratch_shapes=[*scratch_shapes, *sem_arrays],
        input_output_aliases={n_in + i: n_out + j for i, j in carry.aliases.items()},
        compiler_params=_cparams(("arbitrary",) * len(grid) if grid else None, has_side_effects=True),
    )(*args, *carry.operands)
    return res[:n_out], res[n_out:]


MM_TILE = 1024
MM_K_TILE = 2048
MXU_COLS = 256
MM_VMEM_BUDGET = 50 * 1024 * 1024


def _matmul(name, a, b, dims, grid, a_spec, b_spec, out_shape, out_spec, epilogue,
            extras=(), extra_specs=(), prologue=None, carry=None):
    nk = grid[2]
    n_extra = len(extras)
    acc_shape = out_spec.block_shape[-2:]

    def lhs(a_ref):
        return a_ref[...] if prologue is None else prologue(a_ref[...])

    def body_one(a_ref, b_ref, *rest):
        epilogue(_dot(lhs(a_ref), b_ref[...], dims), rest[:n_extra], rest[n_extra:])

    def body_acc(a_ref, b_ref, *rest):
        acc = rest[-1]
        k = pl.program_id(2)
        part = _dot(lhs(a_ref), b_ref[...], dims)

        @pl.when(k == 0)
        def _():
            acc[...] = part

        @pl.when(k > 0)
        def _():
            acc[...] += part

        @pl.when(k == nk - 1)
        def _():
            epilogue(acc[...], rest[:n_extra], rest[n_extra:-1])

    res = _call(body_one if nk == 1 else body_acc, name=name, grid=grid,
                in_specs=[a_spec, b_spec, *extra_specs], out_specs=[out_spec], out_shape=[out_shape],
                args=(a, b, *extras), scratch_shapes=[] if nk == 1 else [pltpu.VMEM(acc_shape, F32)],
                sem=("parallel", "parallel", "arbitrary"), carry=carry)
    return res[0] if carry is None else (res[0][0], res[1])


def _store_as(acc, extra_refs, out_refs):
    out_refs[0][...] = acc.astype(out_refs[0].dtype)


def _square(u):
    return u * u


def mm_nn(name, a, b, out_dtype, tk=None, tm=MM_TILE, tn=MM_TILE, prologue=None, carry=None):
    (m, kk), n = a.shape, b.shape[1]
    tm, tn = _tile(m, tm), _tile(n, tn, mult=MXU_COLS)
    tk = kk if tk is None else _tile(kk, tk, mult=MXU_COLS)
    return _matmul(name, a, b, NN, (m // tm, n // tn, kk // tk),
                   pl.BlockSpec((tm, tk), lambda i, j, k: (i, k)),
                   pl.BlockSpec((tk, tn), lambda i, j, k: (k, j)),
                   jax.ShapeDtypeStruct((m, n), out_dtype),
                   pl.BlockSpec((tm, tn), lambda i, j, k: (i, j)), _store_as, prologue=prologue, carry=carry)


def mm_nt(name, a, b, out_dtype, epilogue=_store_as, extras=(), extra_specs=(), tn=MM_TILE, carry=None):
    (m, kk), n = a.shape, b.shape[0]
    tm, tn = _tile(m, MM_TILE), _tile(n, tn, mult=MXU_COLS)
    return _matmul(name, a, b, NT, (m // tm, n // tn, 1),
                   pl.BlockSpec((tm, kk), lambda i, j, k: (i, 0)),
                   pl.BlockSpec((tn, kk), lambda i, j, k: (j, 0)),
                   jax.ShapeDtypeStruct((m, n), out_dtype),
                   pl.BlockSpec((tm, tn), lambda i, j, k: (i, j)), epilogue,
                   extras=extras, extra_specs=extra_specs, carry=carry)


def _whole_k_fits(tm, tn, kk, out_dtype, prologue):
    operands = 2 * 2 * kk * (tm + tn)
    out = 2 * tm * tn * jnp.dtype(out_dtype).itemsize + 4 * tm * tn
    return operands + out + (2 * kk * tm if prologue is not None else 0) <= MM_VMEM_BUDGET


def mm_tn(name, a, b, out_dtype, prologue=None, carry=None):
    (kk, m), n = a.shape, b.shape[1]
    tm, tn = _tile(m, MM_TILE), _tile(n, MM_TILE)
    tk = kk if _whole_k_fits(tm, tn, kk, out_dtype, prologue) else _tile(kk, MM_K_TILE)
    return _matmul(name, a, b, TN, (m // tm, n // tn, kk // tk),
                   pl.BlockSpec((tk, tm), lambda i, j, k: (k, i)),
                   pl.BlockSpec((tk, tn), lambda i, j, k: (k, j)),
                   jax.ShapeDtypeStruct((m, n), out_dtype),
                   pl.BlockSpec((tm, tn), lambda i, j, k: (i, j)), _store_as, prologue=prologue, carry=carry)


def up_proj(h2, wup_slabs):
    (m, kk), (_, _, ns) = h2.shape, wup_slabs.shape
    tm, tn = _tile(m, MM_TILE), _tile(ns, MM_TILE)
    r = ns // tn
    n = N_DEV * ns

    def epi(acc, extra_refs, out_refs):
        out_refs[0][...] = jnp.maximum(acc, 0.0).astype(BF16)

    return _matmul("up_proj", h2, wup_slabs, NN, (m // tm, n // tn, 1),
                   pl.BlockSpec((tm, kk), lambda i, j, k: (i, 0)),
                   pl.BlockSpec((None, kk, tn), lambda i, j, k: (j // r, 0, j % r)),
                   jax.ShapeDtypeStruct((m, n), BF16),
                   pl.BlockSpec((tm, tn), lambda i, j, k: (i, j)), epi)


def down_proj(u, wdown):
    return mm_nn("down_proj", u, wdown, F32, tm=MM_TILE // 2, tn=MM_TILE // 2, prologue=_square)


def down_bwd_act(dy, wdown, u):
    tm, tn = _tile(dy.shape[0], MM_TILE), _tile(wdown.shape[0], MM_TILE)

    def epi(acc, extra_refs, out_refs):
        out_refs[0][...] = (acc * (2.0 * extra_refs[0][...].astype(F32))).astype(BF16)

    return mm_nt("down_bwd_act", dy, wdown, BF16, epilogue=epi, extras=(u,),
                 extra_specs=(pl.BlockSpec((tm, tn), lambda i, j, k: (i, j)),))


def down_wgrad(u, dy):
    return mm_tn("down_wgrad", u, dy, BF16, prologue=_square)


def up_bwd_x(du, wup_slabs, carry=None):
    (m, kk), (slabs, n, ns) = du.shape, wup_slabs.shape
    tm, tn = _tile(m, MM_TILE // 2), _tile(n, MM_TILE // 2, mult=MXU_COLS)

    def body(a_ref, b_ref, o_ref):
        acc = _dot(a_ref[:, :ns], b_ref[0], NT)
        for s in range(1, slabs):
            acc = acc + _dot(a_ref[:, s * ns:(s + 1) * ns], b_ref[s], NT)
        o_ref[...] = acc

    res = _call(body, name="up_bwd_x", grid=(m // tm, n // tn),
                in_specs=[pl.BlockSpec((tm, kk), lambda i, j: (i, 0)),
                          pl.BlockSpec((slabs, tn, ns), lambda i, j: (0, j, 0))],
                out_specs=[pl.BlockSpec((tm, tn), lambda i, j: (i, j))],
                out_shape=[jax.ShapeDtypeStruct((m, n), F32)], args=(du, wup_slabs),
                sem=("parallel", "parallel"), carry=carry)
    return res[0] if carry is None else (res[0][0], res[1])


def up_wgrad(h2, du, carry=None):
    (kk, m), n = h2.shape, du.shape[1]
    ns = n // N_DEV
    tm, tn = _tile(m, MM_TILE), _tile(ns, MM_TILE)
    tk = kk if _whole_k_fits(tm, tn, kk, BF16, None) else _tile(kk, MM_K_TILE)
    r = ns // tn
    return _matmul("up_wgrad", h2, du, TN, (m // tm, n // tn, kk // tk),
                   pl.BlockSpec((tk, tm), lambda i, j, k: (k, i)),
                   pl.BlockSpec((tk, tn), lambda i, j, k: (k, j)),
                   jax.ShapeDtypeStruct((N_DEV, m, ns), BF16),
                   pl.BlockSpec((None, tm, tn), lambda i, j, k: (j // r, i, j % r)), _store_as, carry=carry)


def _rstd(x):
    return lax.rsqrt(jnp.mean(x * x, axis=-1, keepdims=True) + EPS)


def _norm_bwd(x, g, dy):
    r = _rstd(x)
    xh = x * r
    dyg = dy * g
    dx = r * (dyg - xh * jnp.mean(dyg * xh, axis=-1, keepdims=True))
    return dx, jnp.sum(dy * xh, axis=0, keepdims=True)


def _row_spec(tr, d):
    return pl.BlockSpec((tr, d), lambda i: (i, 0))


def _vec_spec(d):
    return pl.BlockSpec((1, d), lambda i: (0, 0))


def _accum(ref, val):
    @pl.when(pl.program_id(0) == 0)
    def _():
        ref[...] = jnp.zeros_like(ref)

    ref[...] += val


def pre_norm(x, g, carry=None, tr=256):
    t, d = x.shape
    tr = _pick(t, tr)

    def body(x_ref, g_ref, h_ref):
        xx = x_ref[...]
        h_ref[...] = (xx * _rstd(xx) * g_ref[...]).astype(BF16)

    return _call(body, name="pre_norm", grid=(t // tr,),
                 in_specs=[_row_spec(tr, d), _vec_spec(d)], out_specs=[_row_spec(tr, d)],
                 out_shape=[jax.ShapeDtypeStruct((t, d), BF16)], args=(x, g), sem=("parallel",), carry=carry)


def mid_fwd(mixed, g_post, x, g_pre2, tr=512):
    t, d = x.shape
    tr = _pick(t, tr)

    def body(m_ref, gp_ref, x_ref, g2_ref, x1_ref, h2_ref):
        mm = m_ref[...]
        x1 = x_ref[...] + mm * _rstd(mm) * gp_ref[...]
        x1_ref[...] = x1
        h2_ref[...] = (x1 * _rstd(x1) * g2_ref[...]).astype(BF16)

    return _call(body, name="mid_fwd", grid=(t // tr,),
                 in_specs=[_row_spec(tr, d), _vec_spec(d), _row_spec(tr, d), _vec_spec(d)],
                 out_specs=[_row_spec(tr, d), _row_spec(tr, d)],
                 out_shape=[jax.ShapeDtypeStruct((t, d), F32), jax.ShapeDtypeStruct((t, d), BF16)],
                 args=(mixed, g_post, x, g_pre2), sem=("parallel",))


def loss_bwd(y, g_post2, x1, target, tr=512):
    t, d = y.shape
    tr = _pick(t, tr)

    def body(y_ref, g_ref, x1_ref, t_ref, sse_ref, dout_ref, dy_ref, dg_ref):
        yy = y_ref[...]
        g = g_ref[...]
        err = x1_ref[...] + yy * _rstd(yy) * g - t_ref[...]
        _accum(sse_ref, jnp.sum(jnp.sum(err * err, axis=1, keepdims=True), axis=0, keepdims=True))
        dout = err * (1.0 / d)
        dout_ref[...] = dout
        dy, dg = _norm_bwd(yy, g, dout)
        dy_ref[...] = dy.astype(BF16)
        _accum(dg_ref, dg)

    return _call(body, name="loss_bwd", grid=(t // tr,),
                 in_specs=[_row_spec(tr, d), _vec_spec(d), _row_spec(tr, d), _row_spec(tr, d)],
                 out_specs=[pl.BlockSpec((1, 1), lambda i: (0, 0)), _row_spec(tr, d), _row_spec(tr, d), _vec_spec(d)],
                 out_shape=[jax.ShapeDtypeStruct((1, 1), F32), jax.ShapeDtypeStruct((t, d), F32),
                            jax.ShapeDtypeStruct((t, d), BF16), jax.ShapeDtypeStruct((1, d), F32)],
                 args=(y, g_post2, x1, target), sem=("arbitrary",))


def mid_bwd(dh2, x1, g_pre2, dout, mixed, g_post, carry=None, tr=256):
    t, d = x1.shape
    tr = _pick(t, tr)

    def body(dh_ref, x1_ref, g2_ref, do_ref, m_ref, gp_ref, dx1_ref, dm_ref, dg2_ref, dgp_ref):
        d1, dg2 = _norm_bwd(x1_ref[...], g2_ref[...], dh_ref[...])
        dx1 = do_ref[...] + d1
        dx1_ref[...] = dx1
        dm, dgp = _norm_bwd(m_ref[...], gp_ref[...], dx1)
        dm_ref[...] = dm.astype(BF16)
        _accum(dg2_ref, dg2)
        _accum(dgp_ref, dgp)

    return _call(body, name="mid_bwd", grid=(t // tr,),
                 in_specs=[_row_spec(tr, d), _row_spec(tr, d), _vec_spec(d), _row_spec(tr, d), _row_spec(tr, d),
                           _vec_spec(d)],
                 out_specs=[_row_spec(tr, d), _row_spec(tr, d), _vec_spec(d), _vec_spec(d)],
                 out_shape=[jax.ShapeDtypeStruct((t, d), F32), jax.ShapeDtypeStruct((t, d), BF16),
                            jax.ShapeDtypeStruct((1, d), F32), jax.ShapeDtypeStruct((1, d), F32)],
                 args=(dh2, x1, g_pre2, dout, mixed, g_post), sem=("arbitrary",), carry=carry)


def first_bwd(dh1, x, g_pre, dx1, carry=None, tr=512):
    t, d = x.shape
    tr = _pick(t, tr)

    def body(dh_ref, x_ref, g_ref, dx1_ref, gx_ref, dg_ref):
        d0, dg = _norm_bwd(x_ref[...], g_ref[...], dh_ref[...])
        gx_ref[...] = dx1_ref[...] + d0
        _accum(dg_ref, dg)

    return _call(body, name="first_bwd", grid=(t // tr,),
                 in_specs=[_row_spec(tr, d), _row_spec(tr, d), _vec_spec(d), _row_spec(tr, d)],
                 out_specs=[_row_spec(tr, d), _vec_spec(d)],
                 out_shape=[jax.ShapeDtypeStruct((t, d), F32), jax.ShapeDtypeStruct((1, d), F32)],
                 args=(dh1, x, g_pre, dx1), sem=("arbitrary",), carry=carry)


def _attn_geometry(has_prev):
    r = lax.broadcasted_iota(jnp.int32, (BLOCK, 2 * BLOCK), 0)
    c = lax.broadcasted_iota(jnp.int32, (BLOCK, 2 * BLOCK), 1)
    dist = r + BLOCK - c
    valid = jnp.logical_and(jnp.logical_and(dist >= 0, dist < BLOCK), jnp.logical_or(c >= BLOCK, has_prev))
    return dist.astype(F32), valid


def _stack_pairs(x, g, pairs):
    base = g * pairs * LANES
    return jnp.concatenate([x[:, base + p * LANES:base + (p + 1) * LANES] for p in range(pairs)], axis=0)


def _unstack_pairs(xs, pairs):
    return jnp.concatenate([xs[p * BLOCK:(p + 1) * BLOCK, :] for p in range(pairs)], axis=1)


def _to_half(x, g, odd):
    lane = lax.broadcasted_iota(jnp.int32, x.shape, 1)
    y = x if (g == 1) == odd else pltpu.roll(x, HEAD_DIM, axis=1)
    return jnp.where((lane >= HEAD_DIM) == odd, y, 0.0)


def _from_halves(even, odd, g):
    lane = lax.broadcasted_iota(jnp.int32, even.shape, 1)
    if g == 0:
        return jnp.where(lane < HEAD_DIM, even + pltpu.roll(odd, HEAD_DIM, axis=1), 0.0)
    return jnp.where(lane >= HEAD_DIM, pltpu.roll(even, HEAD_DIM, axis=1) + odd, 0.0)


_PARITIES = [(g, odd) for g in range(N_KV_HEADS) for odd in (False, True)]


def _softmax_sink(s, sink_ref, g, odd, group, n_heads, geo):
    dist, valid = geo
    pairs = group // 2
    heads = [g * group + 2 * p + int(odd) for p in range(pairs)]
    bias = jnp.concatenate([(2.0 ** (-8.0 * (h + 1) / n_heads)) * dist for h in heads], axis=0)
    sink = jnp.concatenate([jnp.full((BLOCK, 1), sink_ref[0, h], F32) for h in heads], axis=0)
    s = jnp.where(jnp.concatenate([valid] * pairs, axis=0), s - bias, -jnp.inf)
    m = jnp.maximum(jnp.max(s, axis=-1, keepdims=True), sink)
    p = jnp.exp(s - m)
    p_sink = jnp.exp(sink - m)
    inv = 1.0 / (jnp.sum(p, axis=-1, keepdims=True) + p_sink)
    return p * inv, p_sink * inv


def attn_fwd(proj, sinks, gain, aw, carry=None):
    t = proj.shape[0]
    kw = N_KV_HEADS * HEAD_DIM
    n_heads = aw // HEAD_DIM
    group = n_heads // N_KV_HEADS
    pairs = group // 2
    assert kw == LANES and group % 2 == 0
    nb = t // BLOCK
    scale = HEAD_DIM ** -0.5

    def body(sink_ref, q_ref, k_ref, v_ref, g_ref, o_ref, on_ref):
        n = pl.program_id(0)
        cur = pl.multiple_of(n * BLOCK, BLOCK)
        prev = pl.multiple_of(jnp.maximum(n - 1, 0) * BLOCK, BLOCK)
        geo = _attn_geometry(n > 0)
        kcat = jnp.concatenate([k_ref[pl.ds(prev, BLOCK), :], k_ref[pl.ds(cur, BLOCK), :]], axis=0)
        vcat = jnp.concatenate([v_ref[pl.ds(prev, BLOCK), :], v_ref[pl.ds(cur, BLOCK), :]], axis=0)
        q = q_ref[...] * scale
        groups = []
        for g in range(N_KV_HEADS):
            qs = _stack_pairs(q, g, pairs)
            o_pairs = None
            for odd in (False, True):
                s = _dot(qs, _to_half(kcat, g, odd), NT)
                p = _softmax_sink(s, sink_ref, g, odd, group, n_heads, geo)[0]
                o_half = _dot(p, _to_half(vcat, g, odd), NN)
                o_pairs = o_half if o_pairs is None else o_pairs + o_half
            groups.append(_unstack_pairs(o_pairs, pairs))
        o = jnp.concatenate(groups, axis=1)
        o_ref[...] = o
        on_ref[...] = (o * _rstd(o) * g_ref[...]).astype(BF16)

    return _call(body, name="attn_fwd", grid=(nb,),
                 in_specs=[pl.BlockSpec(memory_space=pltpu.SMEM),
                           pl.BlockSpec((BLOCK, aw), lambda n: (n, 0)),
                           pl.BlockSpec((t, kw), lambda n: (0, aw // kw)),
                           pl.BlockSpec((t, kw), lambda n: (0, aw // kw + 1)),
                           pl.BlockSpec((1, aw), lambda n: (0, 0))],
                 out_specs=[pl.BlockSpec((BLOCK, aw), lambda n: (n, 0)), pl.BlockSpec((BLOCK, aw), lambda n: (n, 0))],
                 out_shape=[jax.ShapeDtypeStruct((t, aw), F32), jax.ShapeDtypeStruct((t, aw), BF16)],
                 args=(sinks, proj, proj, proj, gain), sem=("parallel",), carry=carry)


def attn_bwd(proj, sinks, gain, attn_o, dcat, aw, carry=None):
    t = proj.shape[0]
    kw = N_KV_HEADS * HEAD_DIM
    n_heads = aw // HEAD_DIM
    group = n_heads // N_KV_HEADS
    pairs = group // 2
    assert kw == LANES and group % 2 == 0
    nb = t // BLOCK
    scale = HEAD_DIM ** -0.5

    def body(sink_ref, q_ref, k_ref, v_ref, g_ref, o_ref, dn_ref, dq_ref, dk_ref, dv_ref, dsink_ref, dg_ref):
        n = pl.program_id(0)
        cur = pl.multiple_of(n * BLOCK, BLOCK)
        prev = pl.multiple_of(jnp.maximum(n - 1, 0) * BLOCK, BLOCK)
        geo = _attn_geometry(n > 0)

        @pl.when(n == 0)
        def _():
            dk_ref[...] = jnp.zeros_like(dk_ref)
            dv_ref[...] = jnp.zeros_like(dv_ref)
            dsink_ref[...] = jnp.zeros_like(dsink_ref)

        o = o_ref[...]
        do_all, dg = _norm_bwd(o, g_ref[...], dn_ref[...])
        _accum(dg_ref, dg)
        kcat = jnp.concatenate([k_ref[pl.ds(prev, BLOCK), :], k_ref[pl.ds(cur, BLOCK), :]], axis=0)
        vcat = jnp.concatenate([v_ref[pl.ds(prev, BLOCK), :], v_ref[pl.ds(cur, BLOCK), :]], axis=0)
        q = q_ref[...] * scale
        lane = lax.broadcasted_iota(jnp.int32, (1, LANES), 1)
        lane_s = lax.broadcasted_iota(jnp.int32, (pairs * BLOCK, LANES), 1)
        qs = [_stack_pairs(q, g, pairs) for g in range(N_KV_HEADS)]
        dos = [_stack_pairs(do_all, g, pairs) for g in range(N_KV_HEADS)]
        kxs = [_to_half(kcat, g, odd) for g, odd in _PARITIES]
        scores = [_dot(qs[g], kx, NT) for kx, (g, odd) in zip(kxs, _PARITIES)]
        dps = [_dot(dos[g], _to_half(vcat, g, odd), NT) for g, odd in _PARITIES]
        deltas = []
        for g in range(N_KV_HEADS):
            prod = dos[g] * _stack_pairs(o, g, pairs)
            delta_even = jnp.sum(jnp.where(lane_s < HEAD_DIM, prod, 0.0), axis=-1, keepdims=True)
            deltas += [delta_even, jnp.sum(prod, axis=-1, keepdims=True) - delta_even]
        dsink = jnp.zeros((1, LANES), F32)
        ps, dss = [], []
        for i, (g, odd) in enumerate(_PARITIES):
            p, p_sink = _softmax_sink(scores[i], sink_ref, g, odd, group, n_heads, geo)
            ps.append(p)
            dss.append(p * (dps[i] - deltas[i]))
            sink_rows = p_sink * deltas[i]
            for pr in range(pairs):
                h = g * group + 2 * pr + int(odd)
                dsink = dsink + jnp.where(
                    lane == h, -jnp.sum(sink_rows[pr * BLOCK:(pr + 1) * BLOCK], axis=0, keepdims=True), 0.0)
        dq_pairs = [_dot(ds, kx, NN) for ds, kx in zip(dss, kxs)]
        dk_halves = [_dot(ds, qs[g], TN) for ds, (g, odd) in zip(dss, _PARITIES)]
        dv_halves = [_dot(p, dos[g], TN) for p, (g, odd) in zip(ps, _PARITIES)]
        dq_ref[...] = jnp.concatenate(
            [_unstack_pairs((dq_pairs[2 * g] + dq_pairs[2 * g + 1]) * scale, pairs) for g in range(N_KV_HEADS)],
            axis=1).astype(BF16)
        dk_upd = _from_halves(dk_halves[0], dk_halves[1], 0) + _from_halves(dk_halves[2], dk_halves[3], 1)
        dv_upd = _from_halves(dv_halves[0], dv_halves[1], 0) + _from_halves(dv_halves[2], dv_halves[3], 1)
        dk_ref[pl.ds(prev, BLOCK), :] += dk_upd[:BLOCK]
        dv_ref[pl.ds(prev, BLOCK), :] += dv_upd[:BLOCK]
        dk_ref[pl.ds(cur, BLOCK), :] += dk_upd[BLOCK:]
        dv_ref[pl.ds(cur, BLOCK), :] += dv_upd[BLOCK:]
        dsink_ref[...] += dsink

    return _call(body, name="attn_bwd", grid=(nb,),
                 in_specs=[pl.BlockSpec(memory_space=pltpu.SMEM),
                           pl.BlockSpec((BLOCK, aw), lambda n: (n, 0)),
                           pl.BlockSpec((t, kw), lambda n: (0, aw // kw)),
                           pl.BlockSpec((t, kw), lambda n: (0, aw // kw + 1)),
                           pl.BlockSpec((1, aw), lambda n: (0, 0)),
                           pl.BlockSpec((BLOCK, aw), lambda n: (n, 0)),
                           pl.BlockSpec((BLOCK, aw), lambda n: (n, 0))],
                 out_specs=[pl.BlockSpec((BLOCK, aw), lambda n: (n, 0)),
                            pl.BlockSpec((t, kw), lambda n: (0, 0)), pl.BlockSpec((t, kw), lambda n: (0, 0)),
                            pl.BlockSpec((1, LANES), lambda n: (0, 0)), pl.BlockSpec((1, aw), lambda n: (0, 0))],
                 out_shape=[jax.ShapeDtypeStruct((t, aw), BF16), jax.ShapeDtypeStruct((t, kw), F32),
                            jax.ShapeDtypeStruct((t, kw), F32), jax.ShapeDtypeStruct((1, LANES), F32),
                            jax.ShapeDtypeStruct((1, aw), F32)],
                 args=(sinks, proj, proj, proj, gain, attn_o, dcat), sem=("arbitrary",), carry=carry)


def _sigmoid(x):
    return 0.5 * jnp.tanh(0.5 * x) + 0.5


def _chunk_geometry():
    row = lax.broadcasted_iota(jnp.int32, (CHUNK, CHUNK), 0)
    col = lax.broadcasted_iota(jnp.int32, (CHUNK, CHUNK), 1)
    return row, col


def _cumsum_rows(x, reverse=False):
    row, col = _chunk_geometry()
    tri = (col >= row) if reverse else (col <= row)
    return lax.dot_general(tri.astype(F32), x, ((NN), ((), ())), precision=HI, preferred_element_type=F32)


def _rep_sub(x4, sub):
    k = x4.shape[-1]
    return jnp.broadcast_to(x4[:, None, :], (CHUNK // sub, sub, k)).reshape(CHUNK, k)


def _gates(q_r, f_r, lb):
    sg = _sigmoid(f_r)
    f = lb + (1.0 - lb) * sg
    sq = _sigmoid(q_r)
    return sg, f, sq, q_r * sq


def _offdiag_terms(b, j, sub):
    c = b[j * sub + sub - 1:j * sub + sub, :]
    return jnp.exp(jnp.minimum(b - c, 0.0)), jnp.exp(jnp.minimum(c - b, 0.0))


def _store_heads(ref, x):
    for j in range(ref.shape[0]):
        ref[j] = x[:, _head(j)]


def _sub_rows(ref, r, sub):
    rows = [ref[j, pl.ds(r, CHUNK // sub, stride=sub), :] for j in range(ref.shape[0])]
    return _rep_sub(jnp.concatenate(rows, axis=1), sub)


def _diag_mask(sub):
    row, col = _chunk_geometry()
    return jnp.logical_and((row // sub) == (col // sub), row >= col)


HGRN_HEADS_PER_STEP = 8


def _wide(refs):
    return jnp.concatenate([r[...] for r in refs], axis=1)


def _head(j):
    return slice(j * RNN_HEAD_DIM, (j + 1) * RNN_HEAD_DIM)


def _cat_heads(parts, hs):
    return jnp.concatenate([p[:, hs] for p in parts], axis=1)


def _offdiag_factors(q, k, b, sub):
    rowi = lax.broadcasted_iota(jnp.int32, b.shape, 0)
    qs, ks, ers, ecs = [], [], [], []
    for j in range(CHUNK // sub - 1):
        e_row, e_col = _offdiag_terms(b, j, sub)
        e_row = jnp.where(rowi >= (j + 1) * sub, e_row, 0.0)
        e_col = jnp.where((rowi // sub) == j, e_col, 0.0)
        qs.append(q * e_row)
        ks.append(k * e_col)
        ers.append(e_row)
        ecs.append(e_col)
    return qs, ks, ers, ecs


def hgrn_fwd(proj, attn_n, lb, norm_gain, col0, rw, carry=None):
    t, aw = attn_n.shape
    nh = rw // RNN_HEAD_DIM
    nc = t // CHUNK
    kd = RNN_HEAD_DIM
    cb = col0 // kd
    sub = SUB_FWD
    nsub = CHUNK // sub
    hp = nh
    assert nh <= HGRN_HEADS_PER_STEP
    w = hp * kd

    def body(*refs):
        q_refs, f_refs, i_refs, g_refs = (refs[i * hp:(i + 1) * hp] for i in range(4))
        lb_ref, ng_ref, an_ref, cat_ref, o_ref, att_ref, st_ref, state, b_ref, k_ref = refs[4 * hp:]
        c = pl.program_id(1)

        @pl.when(c == 0)
        def _():
            state[...] = jnp.zeros_like(state)

        st_ref[...] = state[...]
        q_r, f_r, v, g_r = (_wide(rs) for rs in (q_refs, f_refs, i_refs, g_refs))
        _, f, _, q = _gates(q_r, f_r, lb_ref[...])
        k = 1.0 - f
        b = _cumsum_rows(jnp.log(f))
        _store_heads(b_ref, b)
        _store_heads(k_ref, k)
        qcat, kcat, _, _ = _offdiag_factors(q, k, b, sub)
        row, col = _chunk_geometry()
        same = (row // sub) == (col // sub)
        rloc = lax.broadcasted_iota(jnp.int32, (CHUNK, w), 0) % sub
        diag = [jnp.zeros((CHUNK, CHUNK), F32)] * hp
        for r in range(sub):
            bs = _sub_rows(b_ref, r, sub)
            ks = _sub_rows(k_ref, r, sub)
            prod = q * jnp.exp(jnp.where(rloc >= r, b - bs, -jnp.inf)) * ks
            place = jnp.logical_and((col % sub) == r, same)
            diag = [jnp.where(place, jnp.sum(prod[:, _head(j)], axis=-1, keepdims=True), diag[j]) for j in range(hp)]
        b_last = b[CHUNK - 1:CHUNK, :]
        qe = q * jnp.exp(b)
        kdec = k * jnp.exp(b_last - b)
        decay = jnp.exp(b_last)
        outs, normed, states = [], [], []
        for j in range(hp):
            hs = _head(j)
            att = diag[j] + _dot(_cat_heads(qcat, hs), _cat_heads(kcat, hs), NT)
            att_ref[j] = att
            sj = state[j]
            o = _dot(qe[:, hs], sj, NT) + _dot(att, v[:, hs], NN)
            outs.append(o)
            normed.append(o * _rstd(o))
            states.append(sj * decay[:, hs] + _dot(v[:, hs], kdec[:, hs], TN))
        for j in range(hp):
            state[j] = states[j]
        o_ref[...] = jnp.concatenate(outs, axis=1)
        gate = g_r * _sigmoid(g_r)
        cat_ref[:, :aw] = an_ref[...]
        cat_ref[:, aw:] = (jnp.concatenate(normed, axis=1) * jnp.tile(ng_ref[...], (1, hp)) * gate).astype(BF16)

    def col(kidx, j):
        return pl.BlockSpec((CHUNK, kd), lambda hg, c: (c, cb + kidx * nh + hg * hp + j))

    return _call(body, name="hgrn_fwd", grid=(1, nc),
                 in_specs=[col(kidx, j) for kidx in range(4) for j in range(hp)] +
                          [pl.BlockSpec((1, w), lambda hg, c: (0, hg)), pl.BlockSpec((1, kd), lambda hg, c: (0, 0)),
                           pl.BlockSpec((CHUNK, aw), lambda hg, c: (c, 0))],
                 out_specs=[pl.BlockSpec((CHUNK, aw + w), lambda hg, c: (c, 0)),
                            pl.BlockSpec((CHUNK, w), lambda hg, c: (c, hg)),
                            pl.BlockSpec((hp, CHUNK, CHUNK), lambda hg, c: (hg, c, 0)),
                            pl.BlockSpec((None, hp, kd, kd), lambda hg, c: (c, hg, 0, 0))],
                 out_shape=[jax.ShapeDtypeStruct((t, aw + rw), BF16), jax.ShapeDtypeStruct((t, rw), F32),
                            jax.ShapeDtypeStruct((nh, t, CHUNK), F32), jax.ShapeDtypeStruct((nc, nh, kd, kd), F32)],
                 args=(*([proj] * (4 * hp)), lb, norm_gain, attn_n),
                 scratch_shapes=[pltpu.VMEM((hp, kd, kd), F32), pltpu.VMEM((hp, CHUNK, kd), F32),
                                 pltpu.VMEM((hp, CHUNK, kd), F32)],
                 sem=("parallel", "arbitrary"), carry=carry)


def hgrn_bwd(proj, lb, norm_gain, o_all, att_all, st_all, dcat, dq_a, dk_a, dv_a, col0, rw, carry=None):
    t, iw = proj.shape
    aw, kw = dq_a.shape[1], dk_a.shape[1]
    nh = rw // RNN_HEAD_DIM
    nc = t // CHUNK
    kd = RNN_HEAD_DIM
    cb = col0 // kd
    sub = SUB_BWD
    nsub = CHUNK // sub
    dcb = (dcat.shape[1] - rw) // kd
    hp = nh
    assert nh <= HGRN_HEADS_PER_STEP and dcb % hp == 0 and col0 == aw + 2 * kw and iw == col0 + 4 * rw
    w = hp * kd

    def per_head(x, fn):
        return jnp.concatenate([jnp.broadcast_to(fn(x[:, _head(j)]), (CHUNK, kd)) for j in range(hp)], axis=1)

    def body(*refs):
        q_refs, f_refs, i_refs, g_refs = (refs[i * hp:(i + 1) * hp] for i in range(4))
        (lb_ref, ng_ref, o_ref, att_ref, st0_ref, st1_ref, d_ref, dqa_ref, dka_ref, dva_ref, dp_ref, dlb_ref, dng_ref,
         dstate, b_ref, k_ref, dks_ref) = refs[4 * hp:]
        ci = pl.program_id(1)

        @pl.when(ci == 0)
        def _():
            dstate[...] = jnp.zeros_like(dstate)
            dlb_ref[...] = jnp.zeros_like(dlb_ref)
            dng_ref[...] = jnp.zeros_like(dng_ref)

        lbv = lb_ref[...]
        q_r, f_r, v, g_r = (_wide(rs) for rs in (q_refs, f_refs, i_refs, g_refs))
        sg, f, sq, q = _gates(q_r, f_r, lbv)
        k = 1.0 - f
        b = _cumsum_rows(jnp.log(f))
        _store_heads(b_ref, b)
        _store_heads(k_ref, k)
        row, col = _chunk_geometry()

        o = o_ref[...]
        ng = jnp.tile(ng_ref[...], (1, hp))
        sgg = _sigmoid(g_r)
        gate = g_r * sgg
        d_rnn = d_ref[...]
        r = per_head(o, _rstd)
        oh = o * r
        dp_ref[:, :aw] = dqa_ref[...]
        dp_ref[:, aw:aw + kw] = dka_ref[...].astype(BF16)
        dp_ref[:, aw + kw:col0] = dva_ref[...].astype(BF16)
        dp_ref[:, col0 + 3 * rw:] = (d_rnn * oh * ng * (sgg * (1.0 + g_r * (1.0 - sgg)))).astype(BF16)
        d_on = d_rnn * gate
        dng_rows = jnp.sum(d_on * oh, axis=0, keepdims=True)
        dng = dng_rows[:, _head(0)]
        for j in range(1, hp):
            dng = dng + dng_rows[:, _head(j)]
        dng_ref[...] += dng
        dyg = d_on * ng
        do = r * (dyg - oh * per_head(dyg * oh, lambda x: jnp.mean(x, axis=-1, keepdims=True)))

        b_last = b[CHUNK - 1:CHUNK, :]
        eb = jnp.exp(b)
        tail = jnp.exp(b_last - b)
        kdec = k * tail
        decay = jnp.exp(b_last)
        qe = q * eb
        qcat, kcat, ers, ecs = _offdiag_factors(q, k, b, sub)
        diag_mask = _diag_mask(sub)
        dqs, dks, dvs, dads, gsums, dstates = [], [], [], [], [], []
        for j in range(hp):
            hs = _head(j)
            do_h, v_h, dst = do[:, hs], v[:, hs], dstate[j]
            da = jnp.where(row >= col, _dot(do_h, v_h, NT), 0.0)
            dads.append(jnp.where(diag_mask, da, 0.0))
            dq = _dot(do_h, st0_ref[j], NN) * eb[:, hs]
            dk = _dot(v_h, dst, NN) * tail[:, hs]
            dvs.append(_dot(att_ref[j], do_h, TN) + _dot(kdec[:, hs], dst, NT))
            rq = _dot(da, _cat_heads(kcat, hs), NN)
            rk = _dot(da, _cat_heads(qcat, hs), TN)
            for jj in range(nsub - 1):
                dq = dq + ers[jj][:, hs] * rq[:, _head(jj)]
                dk = dk + ecs[jj][:, hs] * rk[:, _head(jj)]
            dqs.append(dq)
            dks.append(dk)
            gsums.append(jnp.sum(dst * st1_ref[j], axis=0, keepdims=True))
            dstates.append(dst * decay[:, hs] + _dot(do_h, qe[:, hs], TN))
        for j in range(hp):
            dstate[j] = dstates[j]
        dq = jnp.concatenate(dqs, axis=1)
        dk = jnp.concatenate(dks, axis=1)
        rloc = lax.broadcasted_iota(jnp.int32, (CHUNK, w), 0) % sub
        for rr in range(sub):
            bs = _sub_rows(b_ref, rr, sub)
            ks = _sub_rows(k_ref, rr, sub)
            e = jnp.exp(jnp.where(rloc >= rr, b - bs, -jnp.inf))
            pick = (col % sub) == rr
            dacol = jnp.concatenate(
                [jnp.broadcast_to(jnp.sum(jnp.where(pick, dads[j], 0.0), axis=-1, keepdims=True), (CHUNK, kd))
                 for j in range(hp)], axis=1)
            wv = dacol * e
            dq = dq + wv * ks
            sums = jnp.sum((wv * q).reshape(nsub, sub, w), axis=1)
            for j in range(hp):
                dks_ref[j, pl.ds(rr, nsub, stride=sub), :] = sums[:, _head(j)]
        dk = dk + jnp.concatenate([dks_ref[j] for j in range(hp)], axis=1)

        dlf = _cumsum_rows(q * dq - k * dk, reverse=True) + jnp.concatenate(gsums, axis=1)
        dfv = dlf / f - dk
        dp_ref[:, col0 + rw:col0 + 2 * rw] = (dfv * (1.0 - lbv) * sg * (1.0 - sg)).astype(BF16)
        dlb_ref[...] += jnp.sum(dfv * (1.0 - sg), axis=0, keepdims=True)
        dp_ref[:, col0:col0 + rw] = (dq * (sq * (1.0 + q_r * (1.0 - sq)))).astype(BF16)
        dp_ref[:, col0 + 2 * rw:col0 + 3 * rw] = jnp.concatenate(dvs, axis=1).astype(BF16)

    def rev(c):
        return nc - 1 - c

    def col_in(kidx, j):
        return pl.BlockSpec((CHUNK, kd), lambda hg, c: (rev(c), cb + kidx * nh + hg * hp + j))

    def rows(width):
        return pl.BlockSpec((CHUNK, width), lambda hg, c: (rev(c), 0))

    return _call(body, name="hgrn_bwd", grid=(1, nc),
                 in_specs=[col_in(kidx, j) for kidx in range(4) for j in range(hp)] +
                          [pl.BlockSpec((1, w), lambda hg, c: (0, hg)), pl.BlockSpec((1, kd), lambda hg, c: (0, 0)),
                           rows(w),
                           pl.BlockSpec((hp, CHUNK, CHUNK), lambda hg, c: (hg, rev(c), 0)),
                           pl.BlockSpec((None, hp, kd, kd), lambda hg, c: (rev(c), hg, 0, 0)),
                           pl.BlockSpec((None, hp, kd, kd),
                                        lambda hg, c: (jnp.minimum(rev(c) + 1, nc - 1), hg, 0, 0)),
                           pl.BlockSpec((CHUNK, w), lambda hg, c: (rev(c), dcb // hp + hg)),
                           rows(aw), rows(kw), rows(kw)],
                 out_specs=[rows(iw),
                            pl.BlockSpec((1, w), lambda hg, c: (0, hg)),
                            pl.BlockSpec((None, 1, kd), lambda hg, c: (hg, 0, 0))],
                 out_shape=[jax.ShapeDtypeStruct((t, iw), BF16), jax.ShapeDtypeStruct((1, rw), F32),
                            jax.ShapeDtypeStruct((1, 1, kd), F32)],
                 args=(*([proj] * (4 * hp)), lb, norm_gain, o_all, att_all, st_all, st_all, dcat, dq_a, dk_a, dv_a),
                 scratch_shapes=[pltpu.VMEM((hp, kd, kd), F32), pltpu.VMEM((hp, CHUNK, kd), F32),
                                 pltpu.VMEM((hp, CHUNK, kd), F32), pltpu.VMEM((hp, CHUNK, kd), F32)],
                 sem=("parallel", "arbitrary"), carry=carry)


def comm_only(name, part):
    return _call(lambda: None, name=name, grid=(), in_specs=[], out_specs=[], out_shape=[], args=(), carry=part)[1]


ADD_BLOCK_ELEMS = 1 << 20
ADAMW_BLOCK_ELEMS = 1 << 19


def add_kept_half(name, kept, got, sel, minor, row0=0):
    pieces, rows, cols = got.shape
    tr = _tile(rows, max(16, ADD_BLOCK_ELEMS // cols), mult=16)
    assert row0 % tr == 0
    i0 = row0 // tr

    def body(sel_ref, k_ref, g_ref, o_ref):
        o_ref[...] = (k_ref[...].astype(F32) + g_ref[...].astype(F32)).astype(o_ref.dtype)

    kept_spec = (pl.BlockSpec((None, None, tr, cols), lambda p, i, s: (p, s[0], i + i0, 0)) if minor else
                 pl.BlockSpec((None, None, tr, cols), lambda p, i, s: (s[0], p, i + i0, 0)))
    return pl.pallas_call(
        body, name=name,
        grid_spec=pltpu.PrefetchScalarGridSpec(
            num_scalar_prefetch=1, grid=(pieces, rows // tr),
            in_specs=[kept_spec, pl.BlockSpec((None, tr, cols), lambda p, i, s: (p, i, 0))],
            out_specs=pl.BlockSpec((None, tr, cols), lambda p, i, s: (p, i, 0))),
        out_shape=jax.ShapeDtypeStruct(got.shape, got.dtype),
        compiler_params=_cparams(("parallel", "parallel")),
    )(sel, kept, got)


def _adamw(w, g, m, v):
    m = ADAM_B1 * m + (1.0 - ADAM_B1) * g
    v = ADAM_B2 * v + (1.0 - ADAM_B2) * (g * g)
    m_hat = m / (1.0 - ADAM_B1 ** ADAM_STEP)
    v_hat = v / (1.0 - ADAM_B2 ** ADAM_STEP)
    delta = -ADAM_LR * (m_hat / (jnp.sqrt(v_hat) + ADAM_EPS) + ADAM_WD * w)
    return delta, m, v


def add_adamw(name, kept, got, sel, w, m, v, row0=0, into=None):
    _, rows, cols = got.shape
    tr = _tile(rows, max(16, ADAMW_BLOCK_ELEMS // cols), mult=16)
    assert row0 % tr == 0
    i0 = row0 // tr
    n_into = 0 if into is None else len(into)

    def body(sel_ref, k_ref, g_ref, w_ref, m_ref, v_ref, *rest):
        go_ref, d_ref, mo_ref, vo_ref = rest[n_into:]
        g = k_ref[...].astype(F32) + g_ref[...].astype(F32)
        go_ref[...] = g
        d_ref[...], mo_ref[...], vo_ref[...] = _adamw(w_ref[...], g, m_ref[...], v_ref[...])

    shard_tile = pl.BlockSpec((tr, cols), lambda i, s: (i + i0, 0))
    return pl.pallas_call(
        body, name=name,
        grid_spec=pltpu.PrefetchScalarGridSpec(
            num_scalar_prefetch=1, grid=(rows // tr,),
            in_specs=[pl.BlockSpec((None, None, tr, cols), lambda i, s: (s[0], 0, i, 0)),
                      pl.BlockSpec((None, tr, cols), lambda i, s: (0, i, 0)), shard_tile, shard_tile, shard_tile,
                      *[ANY] * n_into],
            out_specs=[shard_tile] * 4),
        out_shape=[jax.ShapeDtypeStruct(w.shape, F32)] * 4,
        input_output_aliases={6 + k: k for k in range(n_into)},
        compiler_params=_cparams(("parallel",)),
    )(sel, kept, got, w, m, v, *(into or ()))


def small_allreduce_adamw(partial, scale, w, m, v):
    rows = partial.shape[0]

    def body(p_ref, s_ref, w_ref, m_ref, v_ref, g_ref, d_ref, mo_ref, vo_ref, slots, send_sems, recv_sems):
        x, y, c = _coords()
        my_slot = _slab_index((x, y, c))
        slots[my_slot] = p_ref[...]
        copies = []
        for mask in range(1, N_DEV):
            to = tuple(1 - v_ if (mask >> s_) & 1 else v_ for v_, s_ in ((x, 2), (y, 1), (c, 0)))
            copies.append(pltpu.make_async_remote_copy(
                src_ref=p_ref, dst_ref=slots.at[my_slot],
                send_sem=send_sems.at[mask - 1], recv_sem=recv_sems.at[mask - 1],
                device_id=to, device_id_type=MESH))
        for cp in copies:
            cp.start()
        for cp in copies:
            cp.wait()
        total = slots[0]
        for b in range(1, N_DEV):
            total = total + slots[b]
        g = total * s_ref[...]
        g_ref[...] = g
        d_ref[...], mo_ref[...], vo_ref[...] = _adamw(w_ref[...], g, m_ref[...], v_ref[...])

    vm = pl.BlockSpec(memory_space=pltpu.VMEM)
    return pl.pallas_call(
        body, name="small_allreduce_adamw",
        in_specs=[vm] * 5, out_specs=[vm] * 4,
        out_shape=[jax.ShapeDtypeStruct((rows, LANES), F32)] * 4,
        scratch_shapes=[pltpu.VMEM((N_DEV, rows, LANES), F32),
                        pltpu.SemaphoreType.DMA((N_DEV - 1,)), pltpu.SemaphoreType.DMA((N_DEV - 1,))],
        compiler_params=pltpu.CompilerParams(has_side_effects=True),
    )(partial, scale, w, m, v)


_SMALL = ("attn_sinks", "attn_out_gain", "rnn_lb_logits", "rnn_norm_gain", "mix_pre_gain", "mix_post_gain",
          "mlp_pre_gain", "mlp_post_gain")


def _pack(parts):
    rows = []
    for p in parts:
        flat = p.reshape(-1).astype(F32)
        pad = (-flat.shape[0]) % LANES
        rows.append(jnp.pad(flat, (0, pad)).reshape(-1, LANES))
    packed = jnp.concatenate(rows, axis=0)
    pad_rows = (-packed.shape[0]) % 8
    return jnp.pad(packed, ((0, pad_rows), (0, 0)))


def _unpack(packed, shapes):
    out, r = [], 0
    for s in shapes:
        size = math.prod(s)
        nrows = -(-size // LANES)
        out.append(packed[r:r + nrows].reshape(-1)[:size].reshape(s))
        r += nrows
    return out


class _Scatter:
    def __init__(self, tag, grad, sels, both_links=False):
        self.tag, self.sels, self.both = tag, sels, both_links
        self.shape = grad.shape[1:]
        self.half = self.shape[0] // 2
        self.cur = grad.reshape(4, 2, *self.shape)
        self.stage = 0

    def step(self):
        if self.stage == 0 or not self.both:
            return _scatter_step(self.cur, "cxy"[self.stage])
        if self.stage == 1:
            return _merge(_scatter_step(self.cur, "x", rows=(0, self.half)),
                          _scatter_step(self.cur, "y", minor=True, rows=(self.half, self.shape[0])))
        upper, lower = self.cur
        return _merge(_scatter_step(upper, "y"), _scatter_step(lower, "x"))

    def land(self, got, w=None, m=None, v=None):
        stage, tag, sels = self.stage, self.tag, self.sels
        self.stage += 1
        if stage == 0 or not self.both:
            axis = "cxy"[stage]
            name = "rs_add_%s_%s" % (axis, tag)
            if axis == "y":
                return add_adamw(name, self.cur, got, sels[axis], w, m, v)
            summed = add_kept_half(name, self.cur, got, sels[axis], minor=axis == "c")
            self.cur = summed.reshape(2, summed.shape[0] // 2, *self.shape)
            return None
        got_upper, got_lower = got
        if stage == 1:
            upper = add_kept_half("rs_add_x_%s_upper" % tag, self.cur, got_upper, sels["x"], minor=False)
            lower = add_kept_half("rs_add_y_%s_lower" % tag, self.cur, got_lower, sels["y"], minor=True,
                                  row0=self.half)
            self.cur = tuple(s.reshape(2, 1, *s.shape[1:]) for s in (upper, lower))
            return None
        upper, lower = self.cur
        out_upper = add_adamw("rs_add_y_%s_upper" % tag, upper, got_upper, sels["y"], w, m, v)
        return add_adamw("rs_add_x_%s_lower" % tag, lower, got_lower, sels["x"], w, m, v, row0=self.half,
                         into=out_upper)


def kernel(x, w_in, attn_sinks, attn_out_gain, rnn_lb_logits, rnn_norm_gain, w_out, mix_pre_gain, mix_post_gain, mlp_pre_gain, mlp_post_gain, w_up, w_down, loss_target, m_w_in, m_attn_sinks, m_attn_out_gain, m_rnn_lb_logits, m_rnn_norm_gain, m_w_out, m_mix_pre_gain, m_mix_post_gain, m_mlp_pre_gain, m_mlp_post_gain, m_w_up, m_w_down, v_w_in, v_attn_sinks, v_attn_out_gain, v_rnn_lb_logits, v_rnn_norm_gain, v_w_out, v_mix_pre_gain, v_mix_post_gain, v_mlp_pre_gain, v_mlp_post_gain, v_w_up, v_w_down):
    xs, target = x[0], loss_target[0]
    t, d = xs.shape
    aw = d // 2
    rw = d - aw
    col0 = aw + 2 * N_KV_HEADS * HEAD_DIM
    small_w = dict(attn_sinks=attn_sinks, attn_out_gain=attn_out_gain, rnn_lb_logits=rnn_lb_logits,
                   rnn_norm_gain=rnn_norm_gain, mix_pre_gain=mix_pre_gain, mix_post_gain=mix_post_gain,
                   mlp_pre_gain=mlp_pre_gain, mlp_post_gain=mlp_post_gain)
    small_m = dict(attn_sinks=m_attn_sinks, attn_out_gain=m_attn_out_gain, rnn_lb_logits=m_rnn_lb_logits,
                   rnn_norm_gain=m_rnn_norm_gain, mix_pre_gain=m_mix_pre_gain, mix_post_gain=m_mix_post_gain,
                   mlp_pre_gain=m_mlp_pre_gain, mlp_post_gain=m_mlp_post_gain)
    small_v = dict(attn_sinks=v_attn_sinks, attn_out_gain=v_attn_out_gain, rnn_lb_logits=v_rnn_lb_logits,
                   rnn_norm_gain=v_rnn_norm_gain, mix_pre_gain=v_mix_pre_gain, mix_post_gain=v_mix_post_gain,
                   mlp_pre_gain=v_mlp_pre_gain, mlp_post_gain=v_mlp_post_gain)
    cx, cy, cc = _coords()
    sels = {a: jnp.reshape(v_, (1,)).astype(jnp.int32) for a, v_ in (("x", cx), ("y", cy), ("c", cc))}

    w_in_t, m_in_t, v_in_t = w_in[0].T, m_w_in[0].T, v_w_in[0].T
    s_in, s_out, s_up, s_down = (w.astype(BF16) for w in (w_in_t, w_out[0], w_up[0], w_down[0]))
    probs = jax.nn.softmax(rnn_lb_logits.astype(F32), axis=0)
    lb = probs[0:1]

    (h1,), (wint_part,) = pre_norm(xs, mix_pre_gain, carry=_gather_first(s_in, diagonal=False))
    in_rows = s_in.shape[0]
    wint = comm_only("gather_rest_w_in", _pass_slabs(
        wint_part,
        [(_X, _Y, (0, in_rows // 2)), (_Y, _X, (in_rows // 2, in_rows)), (_X, _C, None), (_Y, _C, None)],
        then=[(_XY, _C, None)]))[0].reshape(-1, d)
    up_rows = s_up.shape[0]
    up_cut = up_rows * 9 // 16
    proj, (wup_part,) = mm_nt("in_proj", h1, wint, F32, tn=2 * MM_TILE,
                              carry=_gather_first(s_up, rows=(0, up_cut)))
    (attn_o, attn_n), (wup_half, wout_half) = attn_fwd(
        proj, attn_sinks, attn_out_gain, aw,
        carry=_merge(_gather_first(s_up, rows=(up_cut, up_rows), into=wup_part), _gather_first(s_out)))
    (cat, o_r, att, st), (wup, wout, wdown_half) = hgrn_fwd(
        proj, attn_n, lb, rnn_norm_gain, col0, rw,
        carry=_merge(_gather_second(wup_half), _gather_second(wout_half), _gather_first(s_down)))
    wout = wout.reshape(-1, d)
    mixed, (wdown,) = mm_nn("out_proj", cat, wout, F32, carry=_gather_second(wdown_half))
    wdown = wdown.reshape(-1, d)
    x1, h2 = mid_fwd(mixed, mix_post_gain, xs, mlp_pre_gain)
    u = up_proj(h2, wup)
    y = down_proj(u, wdown)
    sse, dout, dy, dg_mlppost = loss_bwd(y, mlp_post_gain, x1, target)

    du = down_bwd_act(dy, wdown, u)
    rs_down = _Scatter("down", down_wgrad(u, dy).reshape(N_DEV, -1, d), sels, both_links=True)
    dh2, (got,) = up_bwd_x(du, wup, carry=rs_down.step())
    rs_down.land(got)
    dwup, gots = up_wgrad(h2, du, carry=rs_down.step())
    rs_down.land(gots)
    rs_up = _Scatter("up", dwup, sels, both_links=True)
    (dx1, dmixed, dg_mlppre, dg_mixpost), (got,) = mid_bwd(dh2, x1, mlp_pre_gain, dout, mixed, mix_post_gain,
                                                          carry=rs_up.step())
    rs_up.land(got)
    dcat = mm_nt("out_bwd_x", dmixed, wout, F32)
    rs_out = _Scatter("out", mm_tn("out_wgrad", cat, dmixed, BF16).reshape(N_DEV, -1, d), sels)
    (dq_a, dk_a, dv_a, dsinks, daog), (*gots, got_o) = attn_bwd(
        proj, attn_sinks, attn_out_gain, attn_o, dcat, aw, carry=_merge(rs_down.step(), rs_out.step()))
    out_down = rs_down.land(gots, w_down[0], m_w_down[0], v_w_down[0])
    rs_out.land(got_o)
    (dproj, dlb, dng), (*gots, got_o) = hgrn_bwd(
        proj, lb, rnn_norm_gain, o_r, att, st, dcat, dq_a, dk_a, dv_a, col0, rw,
        carry=_merge(rs_up.step(), rs_out.step()))
    rs_up.land(gots)
    rs_out.land(got_o)
    dwin, (*gots, got_o) = mm_tn("in_wgrad", dproj, h1, BF16, carry=_merge(rs_up.step(), rs_out.step()))
    out_up = rs_up.land(gots, w_up[0], m_w_up[0], v_w_up[0])
    out_out = rs_out.land(got_o, w_out[0], m_w_out[0], v_w_out[0])
    rs_in = _Scatter("in", dwin.reshape(N_DEV, -1, d), sels, both_links=True)
    rs_in.land(comm_only("rs_exchange_c_in", rs_in.step())[0])
    dh1, gots = mm_nn("in_bwd_x", dproj, wint, F32, tm=MM_TILE // 2, carry=rs_in.step())
    rs_in.land(gots)
    grad_x, dg_mixpre = first_bwd(dh1, xs, mix_pre_gain, dx1)
    out_in = rs_in.land(comm_only("rs_exchange_last_in", rs_in.step()), w_in_t, m_in_t, v_in_t)
    big_out = [out_in, out_out, out_up, out_down]

    n_heads = attn_sinks.shape[1]
    jac = probs[0] * probs[1]
    partial = _pack([sse, dsinks[0, :n_heads], daog, jnp.stack([dlb[0], dlb[0]]), jnp.sum(dng, axis=0),
                     dg_mixpre, dg_mixpost, dg_mlppre, dg_mlppost])
    ones = [jnp.ones(small_w[k].shape, F32) for k in _SMALL]
    ones[2] = jnp.stack([jac, -jac])
    scale = _pack([jnp.full((1,), 0.5 / d, F32)] + ones)
    zero = jnp.zeros((1,), F32)
    outs = small_allreduce_adamw(partial, scale, _pack([zero] + [small_w[k] for k in _SMALL]),
                                 _pack([zero] + [small_m[k] for k in _SMALL]),
                                 _pack([jnp.ones((1,), F32)] + [small_v[k] for k in _SMALL]))
    shapes = [(1,)] + [small_w[k].shape for k in _SMALL]
    sgrad, sdelta, snm, snv = (_unpack(o, shapes) for o in outs)
    loss = sgrad[0][0]

    def big(i, j):
        o = big_out[i][j]
        return (o.T if i == 0 else o)[None]

    def ordered(j, smalls):
        s = dict(zip(_SMALL, smalls[1:]))
        return [big(0, j), s["attn_sinks"], s["attn_out_gain"], s["rnn_lb_logits"], s["rnn_norm_gain"], big(1, j),
                s["mix_pre_gain"], s["mix_post_gain"], s["mlp_pre_gain"], s["mlp_post_gain"], big(2, j), big(3, j)]

    return (loss, grad_x[None], *ordered(0, sgrad), *ordered(1, sdelta), *ordered(2, snm), *ordered(3, snv))
```

```python
import math

import jax
import jax.numpy as jnp
from jax import lax
from jax.experimental import pallas as pl
from jax.experimental.pallas import tpu as pltpu

F32 = jnp.float32
BF16 = jnp.bfloat16

HEAD_DIM = 64
N_KV_HEADS = 2
BLOCK = 128
RNN_HEAD_DIM = 128
CHUNK = 64
SUB_FWD = 16
SUB_BWD = 16
EPS = 1e-6

ADAM_LR = 0.001
ADAM_B1 = 0.9
ADAM_B2 = 0.999
ADAM_EPS = 1e-08
ADAM_WD = 0.01
ADAM_STEP = 10

N_DEV = 8
LANES = 128
V7X_VMEM_LIMIT = 56 * 1024 * 1024
MESH = pl.DeviceIdType.MESH
HI = lax.Precision.HIGHEST
ANY = pl.BlockSpec(memory_space=pl.ANY)
_AXES = ("x", "y", "c")


def _cparams(sem=None, **kw):
    return pltpu.CompilerParams(dimension_semantics=sem, vmem_limit_bytes=V7X_VMEM_LIMIT, **kw)


def _dot(a, b, dims):
    return lax.dot_general(a.astype(BF16), b.astype(BF16), (dims, ((), ())), preferred_element_type=F32)


NN = ((1,), (0,))
NT = ((1,), (1,))
TN = ((0,), (0,))


def _pick(n, pref):
    t = min(n, pref)
    while n % t:
        t //= 2
    return t


def _tile(n, pref, mult=LANES):
    if n <= pref:
        return n
    t = pref - pref % mult
    while n % t:
        t -= mult
    return t


def _coords():
    return lax.axis_index("x"), lax.axis_index("y"), lax.axis_index("c")


def _slab_index(dev):
    return 4 * dev[0] + 2 * dev[1] + dev[2]


class _Part:
    def __init__(self, operands, landings, aliases, n_sems, plan):
        self.operands, self.landings, self.aliases, self.n_sems, self.plan = operands, landings, aliases, n_sems, plan


def _merge(*parts):
    operands, landings, aliases, plans = [], [], {}, []
    s0 = 0
    for p in parts:
        o0, l0 = len(operands), len(landings)
        aliases.update({o0 + i: l0 + j for i, j in p.aliases.items()})
        plans.append((p.plan, o0, len(p.operands), l0, len(p.landings), s0))
        operands += p.operands
        landings += p.landings
        s0 += p.n_sems

    def plan(ops, lands, sem):
        starts, waits = [], []
        for f, o0, no, l0, nl, off in plans:
            s, w = f(ops[o0:o0 + no], lands[l0:l0 + nl], lambda kind, k, off=off: sem(kind, off + k))
            starts += s
            waits += w
        return starts, waits

    return _Part(operands, landings, aliases, s0, plan)


def _gather_peers(x, y, c):
    return [(x, y, 1 - c), (1 - x, y, c), (x, 1 - y, c), (1 - x, 1 - y, c)]


def _gather_first(shard, rows=None, into=None, diagonal=True):
    lo, hi = (0, shard.shape[0]) if rows is None else rows
    n_peers = 4 if diagonal else 3

    def plan(ops, lands, sem):
        x, y, c = _coords()
        me, peers = (x, y, c), _gather_peers(x, y, c)[:n_peers]
        src = ops[0].at[pl.ds(lo, hi - lo)]

        def slab(block):
            return lands[0].at[_slab_index(block), pl.ds(lo, hi - lo)]

        def cp(k, block, to):
            return pltpu.make_async_remote_copy(
                src_ref=src, dst_ref=slab(block),
                send_sem=sem(0, k), recv_sem=sem(1, k), device_id=to, device_id_type=MESH)

        local = pltpu.make_async_copy(src, slab(me), sem(2, 0))
        sends = [cp(k, me, to) for k, to in enumerate(peers)]
        recvs = [cp(k, frm, me) for k, frm in enumerate(peers)]
        return ([local.start] + [s.start for s in sends],
                [local.wait] + [s.wait_send for s in sends] + [r.wait_recv for r in recvs])

    landing = jax.ShapeDtypeStruct((N_DEV, *shard.shape), shard.dtype)
    if into is None:
        return _Part([shard], [landing], {}, 4, plan)
    return _Part([shard, into], [landing], {1: 0}, 4, plan)


def _flip(dev, flips):
    return tuple(1 - v if f else v for v, f in zip(dev, flips))


def _pass_slabs(gathered, moves, then=()):
    def wave(lands, sem, k0, wave_moves):
        me = _coords()
        sends, recvs = [], []
        for k, (block, dest, rows) in enumerate(wave_moves, start=k0):
            lo, hi = (0, gathered.shape[1]) if rows is None else rows

            def cp(blk, to, k=k, lo=lo, hi=hi):
                slab = lands[0].at[_slab_index(blk), pl.ds(lo, hi - lo)]
                return pltpu.make_async_remote_copy(
                    src_ref=slab, dst_ref=slab, send_sem=sem(0, k), recv_sem=sem(1, k),
                    device_id=to, device_id_type=MESH)

            sends.append(cp(_flip(me, block), _flip(me, dest)))
            recvs.append(cp(_flip(_flip(me, dest), block), me))
        return [s.start for s in sends], [s.wait_send for s in sends] + [r.wait_recv for r in recvs]

    def plan(ops, lands, sem):
        starts, waits = wave(lands, sem, 0, moves)
        if then:
            starts2, waits2 = wave(lands, sem, len(moves), then)
            waits = waits + starts2 + waits2
        return starts, waits

    return _Part([gathered], [jax.ShapeDtypeStruct(gathered.shape, gathered.dtype)], {0: 0},
                 len(moves) + len(then), plan)


_X, _Y, _C, _XY = (1, 0, 0), (0, 1, 0), (0, 0, 1), (1, 1, 0)


def _gather_second(gathered):
    def plan(ops, lands, sem):
        x, y, c = _coords()
        sibling = (x, y, 1 - c)
        chips = [(1 - x, y), (x, 1 - y), (1 - x, 1 - y)]

        def cp(k, block):
            slab = lands[0].at[_slab_index(block)]
            return pltpu.make_async_remote_copy(
                src_ref=slab, dst_ref=slab, send_sem=sem(0, k), recv_sem=sem(1, k),
                device_id=sibling, device_id_type=MESH)

        sends = [cp(k, (*chip, c)) for k, chip in enumerate(chips)]
        recvs = [cp(k, (*chip, 1 - c)) for k, chip in enumerate(chips)]
        return [s.start for s in sends], [s.wait_send for s in sends] + [r.wait_recv for r in recvs]

    return _Part([gathered], [jax.ShapeDtypeStruct(gathered.shape, gathered.dtype)], {0: 0}, 3, plan)


def _scatter_step(array, axis, minor=None, rows=None):
    minor = (axis == "c") if minor is None else minor
    pieces = array.shape[0] if minor else array.shape[1]
    lo, hi = (0, array.shape[2]) if rows is None else rows

    def plan(ops, lands, sem):
        coords = list(_coords())
        ai = _AXES.index(axis)
        mine = coords[ai]
        peer = list(coords)
        peer[ai] = 1 - mine
        cps = []
        for p in range(pieces):
            src = ops[0].at[p, 1 - mine, pl.ds(lo, hi - lo)] if minor else ops[0].at[1 - mine, p, pl.ds(lo, hi - lo)]
            cps.append(pltpu.make_async_remote_copy(
                src_ref=src, dst_ref=lands[0].at[p], send_sem=sem(0, p), recv_sem=sem(1, p),
                device_id=tuple(peer), device_id_type=MESH))
        return [cp.start for cp in cps], [cp.wait for cp in cps]

    return _Part([array], [jax.ShapeDtypeStruct((pieces, hi - lo, array.shape[3]), array.dtype)], {}, pieces, plan)


def _grid_edges(grid):
    first = last = None
    for ax, n in enumerate(grid):
        p = pl.program_id(ax)
        f, l = p == 0, p == n - 1
        first = f if first is None else jnp.logical_and(first, f)
        last = l if last is None else jnp.logical_and(last, l)
    return first, last


def _call(body, *, name, grid, in_specs, out_specs, out_shape, args, scratch_shapes=(), sem=None, carry=None):
    if carry is None:
        return pl.pallas_call(
            body, name=name, grid=grid, in_specs=list(in_specs), out_specs=list(out_specs),
            out_shape=list(out_shape), scratch_shapes=list(scratch_shapes), compiler_params=_cparams(sem),
        )(*args)
    n_in, n_out, n_scr = len(in_specs), len(out_specs), len(scratch_shapes)
    n_cin, n_cout = len(carry.operands), len(carry.landings)

    def wrapped(*refs):
        ins, cins = refs[:n_in], refs[n_in:n_in + n_cin]
        o0 = n_in + n_cin
        outs, couts = refs[o0:o0 + n_out], refs[o0 + n_out:o0 + n_out + n_cout]
        s0 = o0 + n_out + n_cout
        scr, sems = refs[s0:s0 + n_scr], refs[s0 + n_scr:]
        first, last = _grid_edges(grid)

        def plan():
            return carry.plan(cins, couts, lambda kind, k: sems[kind].at[k])

        def start_all():
            for start in plan()[0]:
                start()

        def wait_all():
            for wait in plan()[1]:
                wait()

        if grid:
            pl.when(first)(start_all)
            body(*ins, *outs, *scr)
            pl.when(last)(wait_all)
        else:
            start_all()
            body(*ins, *outs, *scr)
            wait_all()

    sem_arrays = [pltpu.SemaphoreType.DMA((carry.n_sems,))] * 3
    res = pl.pallas_call(
        wrapped, name=name, grid=grid,
        in_specs=[*in_specs, *[ANY] * n_cin], out_specs=[*out_specs, *[ANY] * n_cout],
        out_shape=[*out_shape, *carry.landings],
        scratch_shapes=[*scratch_shapes, *sem_arrays],
        input_output_aliases={n_in + i: n_out + j for i, j in carry.aliases.items()},
        compiler_params=_cparams(("arbitrary",) * len(grid) if grid else None, has_side_effects=True),
    )(*args, *carry.operands)
    return res[:n_out], res[n_out:]


MM_TILE = 1024
MM_K_TILE = 2048
MXU_COLS = 256
MM_VMEM_BUDGET = 50 * 1024 * 1024


def _matmul(name, a, b, dims, grid, a_spec, b_spec, out_shape, out_spec, epilogue,
            extras=(), extra_specs=(), prologue=None, carry=None):
    nk = grid[2]
    n_extra = len(extras)
    acc_shape = out_spec.block_shape[-2:]

    def lhs(a_ref):
        return a_ref[...] if prologue is None else prologue(a_ref[...])

    def body_one(a_ref, b_ref, *rest):
        epilogue(_dot(lhs(a_ref), b_ref[...], dims), rest[:n_extra], rest[n_extra:])

    def body_acc(a_ref, b_ref, *rest):
        acc = rest[-1]
        k = pl.program_id(2)
        part = _dot(lhs(a_ref), b_ref[...], dims)

        @pl.when(k == 0)
        def _():
            acc[...] = part

        @pl.when(k > 0)
        def _():
            acc[...] += part

        @pl.when(k == nk - 1)
        def _():
            epilogue(acc[...], rest[:n_extra], rest[n_extra:-1])

    res = _call(body_one if nk == 1 else body_acc, name=name, grid=grid,
                in_specs=[a_spec, b_spec, *extra_specs], out_specs=[out_spec], out_shape=[out_shape],
                args=(a, b, *extras), scratch_shapes=[] if nk == 1 else [pltpu.VMEM(acc_shape, F32)],
                sem=("parallel", "parallel", "arbitrary"), carry=carry)
    return res[0] if carry is None else (res[0][0], res[1])


def _store_as(acc, extra_refs, out_refs):
    out_refs[0][...] = acc.astype(out_refs[0].dtype)


def _square(u):
    return u * u


def mm_nn(name, a, b, out_dtype, tk=None, tm=MM_TILE, tn=MM_TILE, prologue=None, carry=None):
    (m, kk), n = a.shape, b.shape[1]
    tm, tn = _tile(m, tm), _tile(n, tn, mult=MXU_COLS)
    tk = kk if tk is None else _tile(kk, tk, mult=MXU_COLS)
    return _matmul(name, a, b, NN, (m // tm, n // tn, kk // tk),
                   pl.BlockSpec((tm, tk), lambda i, j, k: (i, k)),
                   pl.BlockSpec((tk, tn), lambda i, j, k: (k, j)),
                   jax.ShapeDtypeStruct((m, n), out_dtype),
                   pl.BlockSpec((tm, tn), lambda i, j, k: (i, j)), _store_as, prologue=prologue, carry=carry)


def mm_nt(name, a, b, out_dtype, epilogue=_store_as, extras=(), extra_specs=(), tn=MM_TILE, carry=None):
    (m, kk), n = a.shape, b.shape[0]
    tm, tn = _tile(m, MM_TILE), _tile(n, tn, mult=MXU_COLS)
    return _matmul(name, a, b, NT, (m // tm, n // tn, 1),
                   pl.BlockSpec((tm, kk), lambda i, j, k: (i, 0)),
                   pl.BlockSpec((tn, kk), lambda i, j, k: (j, 0)),
                   jax.ShapeDtypeStruct((m, n), out_dtype),
                   pl.BlockSpec((tm, tn), lambda i, j, k: (i, j)), epilogue,
                   extras=extras, extra_specs=extra_specs, carry=carry)


def _whole_k_fits(tm, tn, kk, out_dtype, prologue):
    operands = 2 * 2 * kk * (tm + tn)
    out = 2 * tm * tn * jnp.dtype(out_dtype).itemsize + 4 * tm * tn
    return operands + out + (2 * kk * tm if prologue is not None else 0) <= MM_VMEM_BUDGET


def mm_tn(name, a, b, out_dtype, prologue=None, carry=None):
    (kk, m), n = a.shape, b.shape[1]
    tm, tn = _tile(m, MM_TILE), _tile(n, MM_TILE)
    tk = kk if _whole_k_fits(tm, tn, kk, out_dtype, prologue) else _tile(kk, MM_K_TILE)
    return _matmul(name, a, b, TN, (m // tm, n // tn, kk // tk),
                   pl.BlockSpec((tk, tm), lambda i, j, k: (k, i)),
                   pl.BlockSpec((tk, tn), lambda i, j, k: (k, j)),
                   jax.ShapeDtypeStruct((m, n), out_dtype),
                   pl.BlockSpec((tm, tn), lambda i, j, k: (i, j)), _store_as, prologue=prologue, carry=carry)


def up_proj(h2, wup_slabs):
    (m, kk), (_, _, ns) = h2.shape, wup_slabs.shape
    tm, tn = _tile(m, MM_TILE), _tile(ns, MM_TILE)
    r = ns // tn
    n = N_DEV * ns

    def epi(acc, extra_refs, out_refs):
        out_refs[0][...] = jnp.maximum(acc, 0.0).astype(BF16)

    return _matmul("up_proj", h2, wup_slabs, NN, (m // tm, n // tn, 1),
                   pl.BlockSpec((tm, kk), lambda i, j, k: (i, 0)),
                   pl.BlockSpec((None, kk, tn), lambda i, j, k: (j // r, 0, j % r)),
                   jax.ShapeDtypeStruct((m, n), BF16),
                   pl.BlockSpec((tm, tn), lambda i, j, k: (i, j)), epi)


def down_proj(u, wdown):
    return mm_nn("down_proj", u, wdown, F32, tm=MM_TILE // 2, tn=MM_TILE // 2, prologue=_square)


def down_bwd_act(dy, wdown, u):
    tm, tn = _tile(dy.shape[0], MM_TILE), _tile(wdown.shape[0], MM_TILE)

    def epi(acc, extra_refs, out_refs):
        out_refs[0][...] = (acc * (2.0 * extra_refs[0][...].astype(F32))).astype(BF16)

    return mm_nt("down_bwd_act", dy, wdown, BF16, epilogue=epi, extras=(u,),
                 extra_specs=(pl.BlockSpec((tm, tn), lambda i, j, k: (i, j)),))


def down_wgrad(u, dy):
    return mm_tn("down_wgrad", u, dy, BF16, prologue=_square)


def up_bwd_x(du, wup_slabs, carry=None):
    (m, kk), (slabs, n, ns) = du.shape, wup_slabs.shape
    tm, tn = _tile(m, MM_TILE // 2), _tile(n, MM_TILE // 2, mult=MXU_COLS)

    def body(a_ref, b_ref, o_ref):
        acc = _dot(a_ref[:, :ns], b_ref[0], NT)
        for s in range(1, slabs):
            acc = acc + _dot(a_ref[:, s * ns:(s + 1) * ns], b_ref[s], NT)
        o_ref[...] = acc

    res = _call(body, name="up_bwd_x", grid=(m // tm, n // tn),
                in_specs=[pl.BlockSpec((tm, kk), lambda i, j: (i, 0)),
                          pl.BlockSpec((slabs, tn, ns), lambda i, j: (0, j, 0))],
                out_specs=[pl.BlockSpec((tm, tn), lambda i, j: (i, j))],
                out_shape=[jax.ShapeDtypeStruct((m, n), F32)], args=(du, wup_slabs),
                sem=("parallel", "parallel"), carry=carry)
    return res[0] if carry is None else (res[0][0], res[1])


def up_wgrad(h2, du, carry=None):
    (kk, m), n = h2.shape, du.shape[1]
    ns = n // N_DEV
    tm, tn = _tile(m, MM_TILE), _tile(ns, MM_TILE)
    tk = kk if _whole_k_fits(tm, tn, kk, BF16, None) else _tile(kk, MM_K_TILE)
    r = ns // tn
    return _matmul("up_wgrad", h2, du, TN, (m // tm, n // tn, kk // tk),
                   pl.BlockSpec((tk, tm), lambda i, j, k: (k, i)),
                   pl.BlockSpec((tk, tn), lambda i, j, k: (k, j)),
                   jax.ShapeDtypeStruct((N_DEV, m, ns), BF16),
                   pl.BlockSpec((None, tm, tn), lambda i, j, k: (j // r, i, j % r)), _store_as, carry=carry)


def _rstd(x):
    return lax.rsqrt(jnp.mean(x * x, axis=-1, keepdims=True) + EPS)


def _norm_bwd(x, g, dy):
    r = _rstd(x)
    xh = x * r
    dyg = dy * g
    dx = r * (dyg - xh * jnp.mean(dyg * xh, axis=-1, keepdims=True))
    return dx, jnp.sum(dy * xh, axis=0, keepdims=True)


def _row_spec(tr, d):
    return pl.BlockSpec((tr, d), lambda i: (i, 0))


def _vec_spec(d):
    return pl.BlockSpec((1, d), lambda i: (0, 0))


def _accum(ref, val):
    @pl.when(pl.program_id(0) == 0)
    def _():
        ref[...] = jnp.zeros_like(ref)

    ref[...] += val


def pre_norm(x, g, carry=None, tr=256):
    t, d = x.shape
    tr = _pick(t, tr)

    def body(x_ref, g_ref, h_ref):
        xx = x_ref[...]
        h_ref[...] = (xx * _rstd(xx) * g_ref[...]).astype(BF16)

    return _call(body, name="pre_norm", grid=(t // tr,),
                 in_specs=[_row_spec(tr, d), _vec_spec(d)], out_specs=[_row_spec(tr, d)],
                 out_shape=[jax.ShapeDtypeStruct((t, d), BF16)], args=(x, g), sem=("parallel",), carry=carry)


def mid_fwd(mixed, g_post, x, g_pre2, tr=512):
    t, d = x.shape
    tr = _pick(t, tr)

    def body(m_ref, gp_ref, x_ref, g2_ref, x1_ref, h2_ref):
        mm = m_ref[...]
        x1 = x_ref[...] + mm * _rstd(mm) * gp_ref[...]
        x1_ref[...] = x1
        h2_ref[...] = (x1 * _rstd(x1) * g2_ref[...]).astype(BF16)

    return _call(body, name="mid_fwd", grid=(t // tr,),
                 in_specs=[_row_spec(tr, d), _vec_spec(d), _row_spec(tr, d), _vec_spec(d)],
                 out_specs=[_row_spec(tr, d), _row_spec(tr, d)],
                 out_shape=[jax.ShapeDtypeStruct((t, d), F32), jax.ShapeDtypeStruct((t, d), BF16)],
                 args=(mixed, g_post, x, g_pre2), sem=("parallel",))


def loss_bwd(y, g_post2, x1, target, tr=512):
    t, d = y.shape
    tr = _pick(t, tr)

    def body(y_ref, g_ref, x1_ref, t_ref, sse_ref, dout_ref, dy_ref, dg_ref):
        yy = y_ref[...]
        g = g_ref[...]
        err = x1_ref[...] + yy * _rstd(yy) * g - t_ref[...]
        _accum(sse_ref, jnp.sum(jnp.sum(err * err, axis=1, keepdims=True), axis=0, keepdims=True))
        dout = err * (1.0 / d)
        dout_ref[...] = dout
        dy, dg = _norm_bwd(yy, g, dout)
        dy_ref[...] = dy.astype(BF16)
        _accum(dg_ref, dg)

    return _call(body, name="loss_bwd", grid=(t // tr,),
                 in_specs=[_row_spec(tr, d), _vec_spec(d), _row_spec(tr, d), _row_spec(tr, d)],
                 out_specs=[pl.BlockSpec((1, 1), lambda i: (0, 0)), _row_spec(tr, d), _row_spec(tr, d), _vec_spec(d)],
                 out_shape=[jax.ShapeDtypeStruct((1, 1), F32), jax.ShapeDtypeStruct((t, d), F32),
                            jax.ShapeDtypeStruct((t, d), BF16), jax.ShapeDtypeStruct((1, d), F32)],
                 args=(y, g_post2, x1, target), sem=("arbitrary",))


def mid_bwd(dh2, x1, g_pre2, dout, mixed, g_post, carry=None, tr=256):
    t, d = x1.shape
    tr = _pick(t, tr)

    def body(dh_ref, x1_ref, g2_ref, do_ref, m_ref, gp_ref, dx1_ref, dm_ref, dg2_ref, dgp_ref):
        d1, dg2 = _norm_bwd(x1_ref[...], g2_ref[...], dh_ref[...])
        dx1 = do_ref[...] + d1
        dx1_ref[...] = dx1
        dm, dgp = _norm_bwd(m_ref[...], gp_ref[...], dx1)
        dm_ref[...] = dm.astype(BF16)
        _accum(dg2_ref, dg2)
        _accum(dgp_ref, dgp)

    return _call(body, name="mid_bwd", grid=(t // tr,),
                 in_specs=[_row_spec(tr, d), _row_spec(tr, d), _vec_spec(d), _row_spec(tr, d), _row_spec(tr, d),
                           _vec_spec(d)],
                 out_specs=[_row_spec(tr, d), _row_spec(tr, d), _vec_spec(d), _vec_spec(d)],
                 out_shape=[jax.ShapeDtypeStruct((t, d), F32), jax.ShapeDtypeStruct((t, d), BF16),
                            jax.ShapeDtypeStruct((1, d), F32), jax.ShapeDtypeStruct((1, d), F32)],
                 args=(dh2, x1, g_pre2, dout, mixed, g_post), sem=("arbitrary",), carry=carry)


def first_bwd(dh1, x, g_pre, dx1, carry=None, tr=512):
    t, d = x.shape
    tr = _pick(t, tr)

    def body(dh_ref, x_ref, g_ref, dx1_ref, gx_ref, dg_ref):
        d0, dg = _norm_bwd(x_ref[...], g_ref[...], dh_ref[...])
        gx_ref[...] = dx1_ref[...] + d0
        _accum(dg_ref, dg)

    return _call(body, name="first_bwd", grid=(t // tr,),
                 in_specs=[_row_spec(tr, d), _row_spec(tr, d), _vec_spec(d), _row_spec(tr, d)],
                 out_specs=[_row_spec(tr, d), _vec_spec(d)],
                 out_shape=[jax.ShapeDtypeStruct((t, d), F32), jax.ShapeDtypeStruct((1, d), F32)],
                 args=(dh1, x, g_pre, dx1), sem=("arbitrary",), carry=carry)


def _attn_geometry(has_prev):
    r = lax.broadcasted_iota(jnp.int32, (BLOCK, 2 * BLOCK), 0)
    c = lax.broadcasted_iota(jnp.int32, (BLOCK, 2 * BLOCK), 1)
    dist = r + BLOCK - c
    valid = jnp.logical_and(jnp.logical_and(dist >= 0, dist < BLOCK), jnp.logical_or(c >= BLOCK, has_prev))
    return dist.astype(F32), valid


def _stack_pairs(x, g, pairs):
    base = g * pairs * LANES
    return jnp.concatenate([x[:, base + p * LANES:base + (p + 1) * LANES] for p in range(pairs)], axis=0)


def _unstack_pairs(xs, pairs):
    return jnp.concatenate([xs[p * BLOCK:(p + 1) * BLOCK, :] for p in range(pairs)], axis=1)


def _to_half(x, g, odd):
    lane = lax.broadcasted_iota(jnp.int32, x.shape, 1)
    y = x if (g == 1) == odd else pltpu.roll(x, HEAD_DIM, axis=1)
    return jnp.where((lane >= HEAD_DIM) == odd, y, 0.0)


def _from_halves(even, odd, g):
    lane = lax.broadcasted_iota(jnp.int32, even.shape, 1)
    if g == 0:
        return jnp.where(lane < HEAD_DIM, even + pltpu.roll(odd, HEAD_DIM, axis=1), 0.0)
    return jnp.where(lane >= HEAD_DIM, pltpu.roll(even, HEAD_DIM, axis=1) + odd, 0.0)


_PARITIES = [(g, odd) for g in range(N_KV_HEADS) for odd in (False, True)]


def _softmax_sink(s, sink_ref, g, odd, group, n_heads, geo):
    dist, valid = geo
    pairs = group // 2
    heads = [g * group + 2 * p + int(odd) for p in range(pairs)]
    bias = jnp.concatenate([(2.0 ** (-8.0 * (h + 1) / n_heads)) * dist for h in heads], axis=0)
    sink = jnp.concatenate([jnp.full((BLOCK, 1), sink_ref[0, h], F32) for h in heads], axis=0)
    s = jnp.where(jnp.concatenate([valid] * pairs, axis=0), s - bias, -jnp.inf)
    m = jnp.maximum(jnp.max(s, axis=-1, keepdims=True), sink)
    p = jnp.exp(s - m)
    p_sink = jnp.exp(sink - m)
    inv = 1.0 / (jnp.sum(p, axis=-1, keepdims=True) + p_sink)
    return p * inv, p_sink * inv


def attn_fwd(proj, sinks, gain, aw, carry=None):
    t = proj.shape[0]
    kw = N_KV_HEADS * HEAD_DIM
    n_heads = aw // HEAD_DIM
    group = n_heads // N_KV_HEADS
    pairs = group // 2
    assert kw == LANES and group % 2 == 0
    nb = t // BLOCK
    scale = HEAD_DIM ** -0.5

    def body(sink_ref, q_ref, k_ref, v_ref, g_ref, o_ref, on_ref):
        n = pl.program_id(0)
        cur = pl.multiple_of(n * BLOCK, BLOCK)
        prev = pl.multiple_of(jnp.maximum(n - 1, 0) * BLOCK, BLOCK)
        geo = _attn_geometry(n > 0)
        kcat = jnp.concatenate([k_ref[pl.ds(prev, BLOCK), :], k_ref[pl.ds(cur, BLOCK), :]], axis=0)
        vcat = jnp.concatenate([v_ref[pl.ds(prev, BLOCK), :], v_ref[pl.ds(cur, BLOCK), :]], axis=0)
        q = q_ref[...] * scale
        groups = []
        for g in range(N_KV_HEADS):
            qs = _stack_pairs(q, g, pairs)
            o_pairs = None
            for odd in (False, True):
                s = _dot(qs, _to_half(kcat, g, odd), NT)
                p = _softmax_sink(s, sink_ref, g, odd, group, n_heads, geo)[0]
                o_half = _dot(p, _to_half(vcat, g, odd), NN)
                o_pairs = o_half if o_pairs is None else o_pairs + o_half
            groups.append(_unstack_pairs(o_pairs, pairs))
        o = jnp.concatenate(groups, axis=1)
        o_ref[...] = o
        on_ref[...] = (o * _rstd(o) * g_ref[...]).astype(BF16)

    return _call(body, name="attn_fwd", grid=(nb,),
                 in_specs=[pl.BlockSpec(memory_space=pltpu.SMEM),
                           pl.BlockSpec((BLOCK, aw), lambda n: (n, 0)),
                           pl.BlockSpec((t, kw), lambda n: (0, aw // kw)),
                           pl.BlockSpec((t, kw), lambda n: (0, aw // kw + 1)),
                           pl.BlockSpec((1, aw), lambda n: (0, 0))],
                 out_specs=[pl.BlockSpec((BLOCK, aw), lambda n: (n, 0)), pl.BlockSpec((BLOCK, aw), lambda n: (n, 0))],
                 out_shape=[jax.ShapeDtypeStruct((t, aw), F32), jax.ShapeDtypeStruct((t, aw), BF16)],
                 args=(sinks, proj, proj, proj, gain), sem=("parallel",), carry=carry)


def attn_bwd(proj, sinks, gain, attn_o, dcat, aw, carry=None):
    t = proj.shape[0]
    kw = N_KV_HEADS * HEAD_DIM
    n_heads = aw // HEAD_DIM
    group = n_heads // N_KV_HEADS
    pairs = group // 2
    assert kw == LANES and group % 2 == 0
    nb = t // BLOCK
    scale = HEAD_DIM ** -0.5

    def body(sink_ref, q_ref, k_ref, v_ref, g_ref, o_ref, dn_ref, dq_ref, dk_ref, dv_ref, dsink_ref, dg_ref):
        n = pl.program_id(0)
        cur = pl.multiple_of(n * BLOCK, BLOCK)
        prev = pl.multiple_of(jnp.maximum(n - 1, 0) * BLOCK, BLOCK)
        geo = _attn_geometry(n > 0)

        @pl.when(n == 0)
        def _():
            dk_ref[...] = jnp.zeros_like(dk_ref)
            dv_ref[...] = jnp.zeros_like(dv_ref)
            dsink_ref[...] = jnp.zeros_like(dsink_ref)

        o = o_ref[...]
        do_all, dg = _norm_bwd(o, g_ref[...], dn_ref[...])
        _accum(dg_ref, dg)
        kcat = jnp.concatenate([k_ref[pl.ds(prev, BLOCK), :], k_ref[pl.ds(cur, BLOCK), :]], axis=0)
        vcat = jnp.concatenate([v_ref[pl.ds(prev, BLOCK), :], v_ref[pl.ds(cur, BLOCK), :]], axis=0)
        q = q_ref[...] * scale
        lane = lax.broadcasted_iota(jnp.int32, (1, LANES), 1)
        lane_s = lax.broadcasted_iota(jnp.int32, (pairs * BLOCK, LANES), 1)
        qs = [_stack_pairs(q, g, pairs) for g in range(N_KV_HEADS)]
        dos = [_stack_pairs(do_all, g, pairs) for g in range(N_KV_HEADS)]
        kxs = [_to_half(kcat, g, odd) for g, odd in _PARITIES]
        scores = [_dot(qs[g], kx, NT) for kx, (g, odd) in zip(kxs, _PARITIES)]
        dps = [_dot(dos[g], _to_half(vcat, g, odd), NT) for g, odd in _PARITIES]
        deltas = []
        for g in range(N_KV_HEADS):
            prod = dos[g] * _stack_pairs(o, g, pairs)
            delta_even = jnp.sum(jnp.where(lane_s < HEAD_DIM, prod, 0.0), axis=-1, keepdims=True)
            deltas += [delta_even, jnp.sum(prod, axis=-1, keepdims=True) - delta_even]
        dsink = jnp.zeros((1, LANES), F32)
        ps, dss = [], []
        for i, (g, odd) in enumerate(_PARITIES):
            p, p_sink = _softmax_sink(scores[i], sink_ref, g, odd, group, n_heads, geo)
            ps.append(p)
            dss.append(p * (dps[i] - deltas[i]))
            sink_rows = p_sink * deltas[i]
            for pr in range(pairs):
                h = g * group + 2 * pr + int(odd)
                dsink = dsink + jnp.where(
                    lane == h, -jnp.sum(sink_rows[pr * BLOCK:(pr + 1) * BLOCK], axis=0, keepdims=True), 0.0)
        dq_pairs = [_dot(ds, kx, NN) for ds, kx in zip(dss, kxs)]
        dk_halves = [_dot(ds, qs[g], TN) for ds, (g, odd) in zip(dss, _PARITIES)]
        dv_halves = [_dot(p, dos[g], TN) for p, (g, odd) in zip(ps, _PARITIES)]
        dq_ref[...] = jnp.concatenate(
            [_unstack_pairs((dq_pairs[2 * g] + dq_pairs[2 * g + 1]) * scale, pairs) for g in range(N_KV_HEADS)],
            axis=1).astype(BF16)
        dk_upd = _from_halves(dk_halves[0], dk_halves[1], 0) + _from_halves(dk_halves[2], dk_halves[3], 1)
        dv_upd = _from_halves(dv_halves[0], dv_halves[1], 0) + _from_halves(dv_halves[2], dv_halves[3], 1)
        dk_ref[pl.ds(prev, BLOCK), :] += dk_upd[:BLOCK]
        dv_ref[pl.ds(prev, BLOCK), :] += dv_upd[:BLOCK]
        dk_ref[pl.ds(cur, BLOCK), :] += dk_upd[BLOCK:]
        dv_ref[pl.ds(cur, BLOCK), :] += dv_upd[BLOCK:]
        dsink_ref[...] += dsink

    return _call(body, name="attn_bwd", grid=(nb,),
                 in_specs=[pl.BlockSpec(memory_space=pltpu.SMEM),
                           pl.BlockSpec((BLOCK, aw), lambda n: (n, 0)),
                           pl.BlockSpec((t, kw), lambda n: (0, aw // kw)),
                           pl.BlockSpec((t, kw), lambda n: (0, aw // kw + 1)),
                           pl.BlockSpec((1, aw), lambda n: (0, 0)),
                           pl.BlockSpec((BLOCK, aw), lambda n: (n, 0)),
                           pl.BlockSpec((BLOCK, aw), lambda n: (n, 0))],
                 out_specs=[pl.BlockSpec((BLOCK, aw), lambda n: (n, 0)),
                            pl.BlockSpec((t, kw), lambda n: (0, 0)), pl.BlockSpec((t, kw), lambda n: (0, 0)),
                            pl.BlockSpec((1, LANES), lambda n: (0, 0)), pl.BlockSpec((1, aw), lambda n: (0, 0))],
                 out_shape=[jax.ShapeDtypeStruct((t, aw), BF16), jax.ShapeDtypeStruct((t, kw), F32),
                            jax.ShapeDtypeStruct((t, kw), F32), jax.ShapeDtypeStruct((1, LANES), F32),
                            jax.ShapeDtypeStruct((1, aw), F32)],
                 args=(sinks, proj, proj, proj, gain, attn_o, dcat), sem=("arbitrary",), carry=carry)


def _sigmoid(x):
    return 0.5 * jnp.tanh(0.5 * x) + 0.5


def _chunk_geometry():
    row = lax.broadcasted_iota(jnp.int32, (CHUNK, CHUNK), 0)
    col = lax.broadcasted_iota(jnp.int32, (CHUNK, CHUNK), 1)
    return row, col


def _cumsum_rows(x, reverse=False):
    row, col = _chunk_geometry()
    tri = (col >= row) if reverse else (col <= row)
    return lax.dot_general(tri.astype(F32), x, ((NN), ((), ())), precision=HI, preferred_element_type=F32)


def _rep_sub(x4, sub):
    k = x4.shape[-1]
    return jnp.broadcast_to(x4[:, None, :], (CHUNK // sub, sub, k)).reshape(CHUNK, k)


def _gates(q_r, f_r, lb):
    sg = _sigmoid(f_r)
    f = lb + (1.0 - lb) * sg
    sq = _sigmoid(q_r)
    return sg, f, sq, q_r * sq


def _offdiag_terms(b, j, sub):
    c = b[j * sub + sub - 1:j * sub + sub, :]
    return jnp.exp(jnp.minimum(b - c, 0.0)), jnp.exp(jnp.minimum(c - b, 0.0))


def _store_heads(ref, x):
    for j in range(ref.shape[0]):
        ref[j] = x[:, _head(j)]


def _sub_rows(ref, r, sub):
    rows = [ref[j, pl.ds(r, CHUNK // sub, stride=sub), :] for j in range(ref.shape[0])]
    return _rep_sub(jnp.concatenate(rows, axis=1), sub)


def _diag_mask(sub):
    row, col = _chunk_geometry()
    return jnp.logical_and((row // sub) == (col // sub), row >= col)


HGRN_HEADS_PER_STEP = 8


def _wide(refs):
    return jnp.concatenate([r[...] for r in refs], axis=1)


def _head(j):
    return slice(j * RNN_HEAD_DIM, (j + 1) * RNN_HEAD_DIM)


def _cat_heads(parts, hs):
    return jnp.concatenate([p[:, hs] for p in parts], axis=1)


def _offdiag_factors(q, k, b, sub):
    rowi = lax.broadcasted_iota(jnp.int32, b.shape, 0)
    qs, ks, ers, ecs = [], [], [], []
    for j in range(CHUNK // sub - 1):
        e_row, e_col = _offdiag_terms(b, j, sub)
        e_row = jnp.where(rowi >= (j + 1) * sub, e_row, 0.0)
        e_col = jnp.where((rowi // sub) == j, e_col, 0.0)
        qs.append(q * e_row)
        ks.append(k * e_col)
        ers.append(e_row)
        ecs.append(e_col)
    return qs, ks, ers, ecs


def hgrn_fwd(proj, attn_n, lb, norm_gain, col0, rw, carry=None):
    t, aw = attn_n.shape
    nh = rw // RNN_HEAD_DIM
    nc = t // CHUNK
    kd = RNN_HEAD_DIM
    cb = col0 // kd
    sub = SUB_FWD
    nsub = CHUNK // sub
    hp = nh
    assert nh <= HGRN_HEADS_PER_STEP
    w = hp * kd

    def body(*refs):
        q_refs, f_refs, i_refs, g_refs = (refs[i * hp:(i + 1) * hp] for i in range(4))
        lb_ref, ng_ref, an_ref, cat_ref, o_ref, att_ref, st_ref, state, b_ref, k_ref = refs[4 * hp:]
        c = pl.program_id(1)

        @pl.when(c == 0)
        def _():
            state[...] = jnp.zeros_like(state)

        st_ref[...] = state[...]
        q_r, f_r, v, g_r = (_wide(rs) for rs in (q_refs, f_refs, i_refs, g_refs))
        _, f, _, q = _gates(q_r, f_r, lb_ref[...])
        k = 1.0 - f
        b = _cumsum_rows(jnp.log(f))
        _store_heads(b_ref, b)
        _store_heads(k_ref, k)
        qcat, kcat, _, _ = _offdiag_factors(q, k, b, sub)
        row, col = _chunk_geometry()
        same = (row // sub) == (col // sub)
        rloc = lax.broadcasted_iota(jnp.int32, (CHUNK, w), 0) % sub
        diag = [jnp.zeros((CHUNK, CHUNK), F32)] * hp
        for r in range(sub):
            bs = _sub_rows(b_ref, r, sub)
            ks = _sub_rows(k_ref, r, sub)
            prod = q * jnp.exp(jnp.where(rloc >= r, b - bs, -jnp.inf)) * ks
            place = jnp.logical_and((col % sub) == r, same)
            diag = [jnp.where(place, jnp.sum(prod[:, _head(j)], axis=-1, keepdims=True), diag[j]) for j in range(hp)]
        b_last = b[CHUNK - 1:CHUNK, :]
        qe = q * jnp.exp(b)
        kdec = k * jnp.exp(b_last - b)
        decay = jnp.exp(b_last)
        outs, normed, states = [], [], []
        for j in range(hp):
            hs = _head(j)
            att = diag[j] + _dot(_cat_heads(qcat, hs), _cat_heads(kcat, hs), NT)
            att_ref[j] = att
            sj = state[j]
            o = _dot(qe[:, hs], sj, NT) + _dot(att, v[:, hs], NN)
            outs.append(o)
            normed.append(o * _rstd(o))
            states.append(sj * decay[:, hs] + _dot(v[:, hs], kdec[:, hs], TN))
        for j in range(hp):
            state[j] = states[j]
        o_ref[...] = jnp.concatenate(outs, axis=1)
        gate = g_r * _sigmoid(g_r)
        cat_ref[:, :aw] = an_ref[...]
        cat_ref[:, aw:] = (jnp.concatenate(normed, axis=1) * jnp.tile(ng_ref[...], (1, hp)) * gate).astype(BF16)

    def col(kidx, j):
        return pl.BlockSpec((CHUNK, kd), lambda hg, c: (c, cb + kidx * nh + hg * hp + j))

    return _call(body, name="hgrn_fwd", grid=(1, nc),
                 in_specs=[col(kidx, j) for kidx in range(4) for j in range(hp)] +
                          [pl.BlockSpec((1, w), lambda hg, c: (0, hg)), pl.BlockSpec((1, kd), lambda hg, c: (0, 0)),
                           pl.BlockSpec((CHUNK, aw), lambda hg, c: (c, 0))],
                 out_specs=[pl.BlockSpec((CHUNK, aw + w), lambda hg, c: (c, 0)),
                            pl.BlockSpec((CHUNK, w), lambda hg, c: (c, hg)),
                            pl.BlockSpec((hp, CHUNK, CHUNK), lambda hg, c: (hg, c, 0)),
                            pl.BlockSpec((None, hp, kd, kd), lambda hg, c: (c, hg, 0, 0))],
                 out_shape=[jax.ShapeDtypeStruct((t, aw + rw), BF16), jax.ShapeDtypeStruct((t, rw), F32),
                            jax.ShapeDtypeStruct((nh, t, CHUNK), F32), jax.ShapeDtypeStruct((nc, nh, kd, kd), F32)],
                 args=(*([proj] * (4 * hp)), lb, norm_gain, attn_n),
                 scratch_shapes=[pltpu.VMEM((hp, kd, kd), F32), pltpu.VMEM((hp, CHUNK, kd), F32),
                                 pltpu.VMEM((hp, CHUNK, kd), F32)],
                 sem=("parallel", "arbitrary"), carry=carry)


def hgrn_bwd(proj, lb, norm_gain, o_all, att_all, st_all, dcat, dq_a, dk_a, dv_a, col0, rw, carry=None):
    t, iw = proj.shape
    aw, kw = dq_a.shape[1], dk_a.shape[1]
    nh = rw // RNN_HEAD_DIM
    nc = t // CHUNK
    kd = RNN_HEAD_DIM
    cb = col0 // kd
    sub = SUB_BWD
    nsub = CHUNK // sub
    dcb = (dcat.shape[1] - rw) // kd
    hp = nh
    assert nh <= HGRN_HEADS_PER_STEP and dcb % hp == 0 and col0 == aw + 2 * kw and iw == col0 + 4 * rw
    w = hp * kd

    def per_head(x, fn):
        return jnp.concatenate([jnp.broadcast_to(fn(x[:, _head(j)]), (CHUNK, kd)) for j in range(hp)], axis=1)

    def body(*refs):
        q_refs, f_refs, i_refs, g_refs = (refs[i * hp:(i + 1) * hp] for i in range(4))
        (lb_ref, ng_ref, o_ref, att_ref, st0_ref, st1_ref, d_ref, dqa_ref, dka_ref, dva_ref, dp_ref, dlb_ref, dng_ref,
         dstate, b_ref, k_ref, dks_ref) = refs[4 * hp:]
        ci = pl.program_id(1)

        @pl.when(ci == 0)
        def _():
            dstate[...] = jnp.zeros_like(dstate)
            dlb_ref[...] = jnp.zeros_like(dlb_ref)
            dng_ref[...] = jnp.zeros_like(dng_ref)

        lbv = lb_ref[...]
        q_r, f_r, v, g_r = (_wide(rs) for rs in (q_refs, f_refs, i_refs, g_refs))
        sg, f, sq, q = _gates(q_r, f_r, lbv)
        k = 1.0 - f
        b = _cumsum_rows(jnp.log(f))
        _store_heads(b_ref, b)
        _store_heads(k_ref, k)
        row, col = _chunk_geometry()

        o = o_ref[...]
        ng = jnp.tile(ng_ref[...], (1, hp))
        sgg = _sigmoid(g_r)
        gate = g_r * sgg
        d_rnn = d_ref[...]
        r = per_head(o, _rstd)
        oh = o * r
        dp_ref[:, :aw] = dqa_ref[...]
        dp_ref[:, aw:aw + kw] = dka_ref[...].astype(BF16)
        dp_ref[:, aw + kw:col0] = dva_ref[...].astype(BF16)
        dp_ref[:, col0 + 3 * rw:] = (d_rnn * oh * ng * (sgg * (1.0 + g_r * (1.0 - sgg)))).astype(BF16)
        d_on = d_rnn * gate
        dng_rows = jnp.sum(d_on * oh, axis=0, keepdims=True)
        dng = dng_rows[:, _head(0)]
        for j in range(1, hp):
            dng = dng + dng_rows[:, _head(j)]
        dng_ref[...] += dng
        dyg = d_on * ng
        do = r * (dyg - oh * per_head(dyg * oh, lambda x: jnp.mean(x, axis=-1, keepdims=True)))

        b_last = b[CHUNK - 1:CHUNK, :]
        eb = jnp.exp(b)
        tail = jnp.exp(b_last - b)
        kdec = k * tail
        decay = jnp.exp(b_last)
        qe = q * eb
        qcat, kcat, ers, ecs = _offdiag_factors(q, k, b, sub)
        diag_mask = _diag_mask(sub)
        dqs, dks, dvs, dads, gsums, dstates = [], [], [], [], [], []
        for j in range(hp):
            hs = _head(j)
            do_h, v_h, dst = do[:, hs], v[:, hs], dstate[j]
            da = jnp.where(row >= col, _dot(do_h, v_h, NT), 0.0)
            dads.append(jnp.where(diag_mask, da, 0.0))
            dq = _dot(do_h, st0_ref[j], NN) * eb[:, hs]
            dk = _dot(v_h, dst, NN) * tail[:, hs]
            dvs.append(_dot(att_ref[j], do_h, TN) + _dot(kdec[:, hs], dst, NT))
            rq = _dot(da, _cat_heads(kcat, hs), NN)
            rk = _dot(da, _cat_heads(qcat, hs), TN)
            for jj in range(nsub - 1):
                dq = dq + ers[jj][:, hs] * rq[:, _head(jj)]
                dk = dk + ecs[jj][:, hs] * rk[:, _head(jj)]
            dqs.append(dq)
            dks.append(dk)
            gsums.append(jnp.sum(dst * st1_ref[j], axis=0, keepdims=True))
            dstates.append(dst * decay[:, hs] + _dot(do_h, qe[:, hs], TN))
        for j in range(hp):
            dstate[j] = dstates[j]
        dq = jnp.concatenate(dqs, axis=1)
        dk = jnp.concatenate(dks, axis=1)
        rloc = lax.broadcasted_iota(jnp.int32, (CHUNK, w), 0) % sub
        for rr in range(sub):
            bs = _sub_rows(b_ref, rr, sub)
            ks = _sub_rows(k_ref, rr, sub)
            e = jnp.exp(jnp.where(rloc >= rr, b - bs, -jnp.inf))
            pick = (col % sub) == rr
            dacol = jnp.concatenate(
                [jnp.broadcast_to(jnp.sum(jnp.where(pick, dads[j], 0.0), axis=-1, keepdims=True), (CHUNK, kd))
                 for j in range(hp)], axis=1)
            wv = dacol * e
            dq = dq + wv * ks
            sums = jnp.sum((wv * q).reshape(nsub, sub, w), axis=1)
            for j in range(hp):
                dks_ref[j, pl.ds(rr, nsub, stride=sub), :] = sums[:, _head(j)]
        dk = dk + jnp.concatenate([dks_ref[j] for j in range(hp)], axis=1)

        dlf = _cumsum_rows(q * dq - k * dk, reverse=True) + jnp.concatenate(gsums, axis=1)
        dfv = dlf / f - dk
        dp_ref[:, col0 + rw:col0 + 2 * rw] = (dfv * (1.0 - lbv) * sg * (1.0 - sg)).astype(BF16)
        dlb_ref[...] += jnp.sum(dfv * (1.0 - sg), axis=0, keepdims=True)
        dp_ref[:, col0:col0 + rw] = (dq * (sq * (1.0 + q_r * (1.0 - sq)))).astype(BF16)
        dp_ref[:, col0 + 2 * rw:col0 + 3 * rw] = jnp.concatenate(dvs, axis=1).astype(BF16)

    def rev(c):
        return nc - 1 - c

    def col_in(kidx, j):
        return pl.BlockSpec((CHUNK, kd), lambda hg, c: (rev(c), cb + kidx * nh + hg * hp + j))

    def rows(width):
        return pl.BlockSpec((CHUNK, width), lambda hg, c: (rev(c), 0))

    return _call(body, name="hgrn_bwd", grid=(1, nc),
                 in_specs=[col_in(kidx, j) for kidx in range(4) for j in range(hp)] +
                          [pl.BlockSpec((1, w), lambda hg, c: (0, hg)), pl.BlockSpec((1, kd), lambda hg, c: (0, 0)),
                           rows(w),
                           pl.BlockSpec((hp, CHUNK, CHUNK), lambda hg, c: (hg, rev(c), 0)),
                           pl.BlockSpec((None, hp, kd, kd), lambda hg, c: (rev(c), hg, 0, 0)),
                           pl.BlockSpec((None, hp, kd, kd),
                                        lambda hg, c: (jnp.minimum(rev(c) + 1, nc - 1), hg, 0, 0)),
                           pl.BlockSpec((CHUNK, w), lambda hg, c: (rev(c), dcb // hp + hg)),
                           rows(aw), rows(kw), rows(kw)],
                 out_specs=[rows(iw),
                            pl.BlockSpec((1, w), lambda hg, c: (0, hg)),
                            pl.BlockSpec((None, 1, kd), lambda hg, c: (hg, 0, 0))],
                 out_shape=[jax.ShapeDtypeStruct((t, iw), BF16), jax.ShapeDtypeStruct((1, rw), F32),
                            jax.ShapeDtypeStruct((1, 1, kd), F32)],
                 args=(*([proj] * (4 * hp)), lb, norm_gain, o_all, att_all, st_all, st_all, dcat, dq_a, dk_a, dv_a),
                 scratch_shapes=[pltpu.VMEM((hp, kd, kd), F32), pltpu.VMEM((hp, CHUNK, kd), F32),
                                 pltpu.VMEM((hp, CHUNK, kd), F32), pltpu.VMEM((hp, CHUNK, kd), F32)],
                 sem=("parallel", "arbitrary"), carry=carry)


def comm_only(name, part):
    return _call(lambda: None, name=name, grid=(), in_specs=[], out_specs=[], out_shape=[], args=(), carry=part)[1]


ADD_BLOCK_ELEMS = 1 << 21
ADAMW_BLOCK_ELEMS = 1 << 19


def add_kept_half(name, kept, got, sel, minor, row0=0):
    pieces, rows, cols = got.shape
    tr = _tile(rows, max(16, ADD_BLOCK_ELEMS // cols), mult=16)
    assert row0 % tr == 0
    i0 = row0 // tr

    def body(sel_ref, k_ref, g_ref, o_ref):
        o_ref[...] = (k_ref[...].astype(F32) + g_ref[...].astype(F32)).astype(o_ref.dtype)

    kept_spec = (pl.BlockSpec((None, None, tr, cols), lambda p, i, s: (p, s[0], i + i0, 0)) if minor else
                 pl.BlockSpec((None, None, tr, cols), lambda p, i, s: (s[0], p, i + i0, 0)))
    return pl.pallas_call(
        body, name=name,
        grid_spec=pltpu.PrefetchScalarGridSpec(
            num_scalar_prefetch=1, grid=(pieces, rows // tr),
            in_specs=[kept_spec, pl.BlockSpec((None, tr, cols), lambda p, i, s: (p, i, 0))],
            out_specs=pl.BlockSpec((None, tr, cols), lambda p, i, s: (p, i, 0))),
        out_shape=jax.ShapeDtypeStruct(got.shape, got.dtype),
        compiler_params=_cparams(("parallel", "parallel")),
    )(sel, kept, got)


def _adamw(w, g, m, v):
    m = ADAM_B1 * m + (1.0 - ADAM_B1) * g
    v = ADAM_B2 * v + (1.0 - ADAM_B2) * (g * g)
    m_hat = m / (1.0 - ADAM_B1 ** ADAM_STEP)
    v_hat = v / (1.0 - ADAM_B2 ** ADAM_STEP)
    delta = -ADAM_LR * (m_hat / (jnp.sqrt(v_hat) + ADAM_EPS) + ADAM_WD * w)
    return delta, m, v


def add_adamw(name, kept, got, sel, w, m, v, row0=0, into=None):
    _, rows, cols = got.shape
    tr = _tile(rows, max(16, ADAMW_BLOCK_ELEMS // cols), mult=16)
    assert row0 % tr == 0
    i0 = row0 // tr
    n_into = 0 if into is None else len(into)

    def body(sel_ref, k_ref, g_ref, w_ref, m_ref, v_ref, *rest):
        go_ref, d_ref, mo_ref, vo_ref = rest[n_into:]
        g = k_ref[...].astype(F32) + g_ref[...].astype(F32)
        go_ref[...] = g
        d_ref[...], mo_ref[...], vo_ref[...] = _adamw(w_ref[...], g, m_ref[...], v_ref[...])

    shard_tile = pl.BlockSpec((tr, cols), lambda i, s: (i + i0, 0))
    return pl.pallas_call(
        body, name=name,
        grid_spec=pltpu.PrefetchScalarGridSpec(
            num_scalar_prefetch=1, grid=(rows // tr,),
            in_specs=[pl.BlockSpec((None, None, tr, cols), lambda i, s: (s[0], 0, i, 0)),
                      pl.BlockSpec((None, tr, cols), lambda i, s: (0, i, 0)), shard_tile, shard_tile, shard_tile,
                      *[ANY] * n_into],
            out_specs=[shard_tile] * 4),
        out_shape=[jax.ShapeDtypeStruct(w.shape, F32)] * 4,
        input_output_aliases={6 + k: k for k in range(n_into)},
        compiler_params=_cparams(("parallel",)),
    )(sel, kept, got, w, m, v, *(into or ()))


def small_allreduce_adamw(partial, scale, w, m, v):
    rows = partial.shape[0]

    def body(p_ref, s_ref, w_ref, m_ref, v_ref, g_ref, d_ref, mo_ref, vo_ref, slots, send_sems, recv_sems):
        x, y, c = _coords()
        my_slot = _slab_index((x, y, c))
        slots[my_slot] = p_ref[...]
        copies = []
        for mask in range(1, N_DEV):
            to = tuple(1 - v_ if (mask >> s_) & 1 else v_ for v_, s_ in ((x, 2), (y, 1), (c, 0)))
            copies.append(pltpu.make_async_remote_copy(
                src_ref=p_ref, dst_ref=slots.at[my_slot],
                send_sem=send_sems.at[mask - 1], recv_sem=recv_sems.at[mask - 1],
                device_id=to, device_id_type=MESH))
        for cp in copies:
            cp.start()
        for cp in copies:
            cp.wait()
        total = slots[0]
        for b in range(1, N_DEV):
            total = total + slots[b]
        g = total * s_ref[...]
        g_ref[...] = g
        d_ref[...], mo_ref[...], vo_ref[...] = _adamw(w_ref[...], g, m_ref[...], v_ref[...])

    vm = pl.BlockSpec(memory_space=pltpu.VMEM)
    return pl.pallas_call(
        body, name="small_allreduce_adamw",
        in_specs=[vm] * 5, out_specs=[vm] * 4,
        out_shape=[jax.ShapeDtypeStruct((rows, LANES), F32)] * 4,
        scratch_shapes=[pltpu.VMEM((N_DEV, rows, LANES), F32),
                        pltpu.SemaphoreType.DMA((N_DEV - 1,)), pltpu.SemaphoreType.DMA((N_DEV - 1,))],
        compiler_params=pltpu.CompilerParams(has_side_effects=True),
    )(partial, scale, w, m, v)


_SMALL = ("attn_sinks", "attn_out_gain", "rnn_lb_logits", "rnn_norm_gain", "mix_pre_gain", "mix_post_gain",
          "mlp_pre_gain", "mlp_post_gain")


def _pack(parts):
    rows = []
    for p in parts:
        flat = p.reshape(-1).astype(F32)
        pad = (-flat.shape[0]) % LANES
        rows.append(jnp.pad(flat, (0, pad)).reshape(-1, LANES))
    packed = jnp.concatenate(rows, axis=0)
    pad_rows = (-packed.shape[0]) % 8
    return jnp.pad(packed, ((0, pad_rows), (0, 0)))


def _unpack(packed, shapes):
    out, r = [], 0
    for s in shapes:
        size = math.prod(s)
        nrows = -(-size // LANES)
        out.append(packed[r:r + nrows].reshape(-1)[:size].reshape(s))
        r += nrows
    return out


class _Scatter:
    def __init__(self, tag, grad, sels, both_links=False):
        self.tag, self.sels, self.both = tag, sels, both_links
        self.shape = grad.shape[1:]
        self.half = self.shape[0] // 2
        self.cur = grad.reshape(4, 2, *self.shape)
        self.stage = 0

    def step(self):
        if self.stage == 0 or not self.both:
            return _scatter_step(self.cur, "cxy"[self.stage])
        if self.stage == 1:
            return _merge(_scatter_step(self.cur, "x", rows=(0, self.half)),
                          _scatter_step(self.cur, "y", minor=True, rows=(self.half, self.shape[0])))
        upper, lower = self.cur
        return _merge(_scatter_step(upper, "y"), _scatter_step(lower, "x"))

    def land(self, got, w=None, m=None, v=None):
        stage, tag, sels = self.stage, self.tag, self.sels
        self.stage += 1
        if stage == 0 or not self.both:
            axis = "cxy"[stage]
            name = "rs_add_%s_%s" % (axis, tag)
            if axis == "y":
                return add_adamw(name, self.cur, got, sels[axis], w, m, v)
            summed = add_kept_half(name, self.cur, got, sels[axis], minor=axis == "c")
            self.cur = summed.reshape(2, summed.shape[0] // 2, *self.shape)
            return None
        got_upper, got_lower = got
        if stage == 1:
            upper = add_kept_half("rs_add_x_%s_upper" % tag, self.cur, got_upper, sels["x"], minor=False)
            lower = add_kept_half("rs_add_y_%s_lower" % tag, self.cur, got_lower, sels["y"], minor=True,
                                  row0=self.half)
            self.cur = tuple(s.reshape(2, 1, *s.shape[1:]) for s in (upper, lower))
            return None
        upper, lower = self.cur
        out_upper = add_adamw("rs_add_y_%s_upper" % tag, upper, got_upper, sels["y"], w, m, v)
        return add_adamw("rs_add_x_%s_lower" % tag, lower, got_lower, sels["x"], w, m, v, row0=self.half,
                         into=out_upper)


def kernel(x, w_in, attn_sinks, attn_out_gain, rnn_lb_logits, rnn_norm_gain, w_out, mix_pre_gain, mix_post_gain, mlp_pre_gain, mlp_post_gain, w_up, w_down, loss_target, m_w_in, m_attn_sinks, m_attn_out_gain, m_rnn_lb_logits, m_rnn_norm_gain, m_w_out, m_mix_pre_gain, m_mix_post_gain, m_mlp_pre_gain, m_mlp_post_gain, m_w_up, m_w_down, v_w_in, v_attn_sinks, v_attn_out_gain, v_rnn_lb_logits, v_rnn_norm_gain, v_w_out, v_mix_pre_gain, v_mix_post_gain, v_mlp_pre_gain, v_mlp_post_gain, v_w_up, v_w_down):
    xs, target = x[0], loss_target[0]
    t, d = xs.shape
    aw = d // 2
    rw = d - aw
    col0 = aw + 2 * N_KV_HEADS * HEAD_DIM
    small_w = dict(attn_sinks=attn_sinks, attn_out_gain=attn_out_gain, rnn_lb_logits=rnn_lb_logits,
                   rnn_norm_gain=rnn_norm_gain, mix_pre_gain=mix_pre_gain, mix_post_gain=mix_post_gain,
                   mlp_pre_gain=mlp_pre_gain, mlp_post_gain=mlp_post_gain)
    small_m = dict(attn_sinks=m_attn_sinks, attn_out_gain=m_attn_out_gain, rnn_lb_logits=m_rnn_lb_logits,
                   rnn_norm_gain=m_rnn_norm_gain, mix_pre_gain=m_mix_pre_gain, mix_post_gain=m_mix_post_gain,
                   mlp_pre_gain=m_mlp_pre_gain, mlp_post_gain=m_mlp_post_gain)
    small_v = dict(attn_sinks=v_attn_sinks, attn_out_gain=v_attn_out_gain, rnn_lb_logits=v_rnn_lb_logits,
                   rnn_norm_gain=v_rnn_norm_gain, mix_pre_gain=v_mix_pre_gain, mix_post_gain=v_mix_post_gain,
                   mlp_pre_gain=v_mlp_pre_gain, mlp_post_gain=v_mlp_post_gain)
    cx, cy, cc = _coords()
    sels = {a: jnp.reshape(v_, (1,)).astype(jnp.int32) for a, v_ in (("x", cx), ("y", cy), ("c", cc))}

    w_in_t, m_in_t, v_in_t = w_in[0].T, m_w_in[0].T, v_w_in[0].T
    s_in, s_out, s_up, s_down = (w.astype(BF16) for w in (w_in_t, w_out[0], w_up[0], w_down[0]))
    probs = jax.nn.softmax(rnn_lb_logits.astype(F32), axis=0)
    lb = probs[0:1]

    (h1,), (wint_part,) = pre_norm(xs, mix_pre_gain, carry=_gather_first(s_in, diagonal=False))
    in_rows = s_in.shape[0]
    wint = comm_only("gather_rest_w_in", _pass_slabs(
        wint_part,
        [(_X, _Y, (0, in_rows // 2)), (_Y, _X, (in_rows // 2, in_rows)), (_X, _C, None), (_Y, _C, None)],
        then=[(_XY, _C, None)]))[0].reshape(-1, d)
    up_rows = s_up.shape[0]
    up_cut = up_rows * 9 // 16
    proj, (wup_part,) = mm_nt("in_proj", h1, wint, F32, tn=2 * MM_TILE,
                              carry=_gather_first(s_up, rows=(0, up_cut)))
    (attn_o, attn_n), (wup_half, wout_half) = attn_fwd(
        proj, attn_sinks, attn_out_gain, aw,
        carry=_merge(_gather_first(s_up, rows=(up_cut, up_rows), into=wup_part), _gather_first(s_out)))
    (cat, o_r, att, st), (wup, wout, wdown_half) = hgrn_fwd(
        proj, attn_n, lb, rnn_norm_gain, col0, rw,
        carry=_merge(_gather_second(wup_half), _gather_second(wout_half), _gather_first(s_down)))
    wout = wout.reshape(-1, d)
    mixed, (wdown,) = mm_nn("out_proj", cat, wout, F32, carry=_gather_second(wdown_half))
    wdown = wdown.reshape(-1, d)
    x1, h2 = mid_fwd(mixed, mix_post_gain, xs, mlp_pre_gain)
    u = up_proj(h2, wup)
    y = down_proj(u, wdown)
    sse, dout, dy, dg_mlppost = loss_bwd(y, mlp_post_gain, x1, target)

    du = down_bwd_act(dy, wdown, u)
    rs_down = _Scatter("down", down_wgrad(u, dy).reshape(N_DEV, -1, d), sels, both_links=True)
    dh2, (got,) = up_bwd_x(du, wup, carry=rs_down.step())
    rs_down.land(got)
    dwup, gots = up_wgrad(h2, du, carry=rs_down.step())
    rs_down.land(gots)
    rs_up = _Scatter("up", dwup, sels, both_links=True)
    (dx1, dmixed, dg_mlppre, dg_mixpost), (got,) = mid_bwd(dh2, x1, mlp_pre_gain, dout, mixed, mix_post_gain,
                                                          carry=rs_up.step())
    rs_up.land(got)
    dcat = mm_nt("out_bwd_x", dmixed, wout, F32)
    rs_out = _Scatter("out", mm_tn("out_wgrad", cat, dmixed, BF16).reshape(N_DEV, -1, d), sels)
    (dq_a, dk_a, dv_a, dsinks, daog), (*gots, got_o) = attn_bwd(
        proj, attn_sinks, attn_out_gain, attn_o, dcat, aw, carry=_merge(rs_down.step(), rs_out.step()))
    out_down = rs_down.land(gots, w_down[0], m_w_down[0], v_w_down[0])
    rs_out.land(got_o)
    (dproj, dlb, dng), (*gots, got_o) = hgrn_bwd(
        proj, lb, rnn_norm_gain, o_r, att, st, dcat, dq_a, dk_a, dv_a, col0, rw,
        carry=_merge(rs_up.step(), rs_out.step()))
    rs_up.land(gots)
    rs_out.land(got_o)
    dwin, (*gots, got_o) = mm_tn("in_wgrad", dproj, h1, BF16, carry=_merge(rs_up.step(), rs_out.step()))
    out_up = rs_up.land(gots, w_up[0], m_w_up[0], v_w_up[0])
    out_out = rs_out.land(got_o, w_out[0], m_w_out[0], v_w_out[0])
    rs_in = _Scatter("in", dwin.reshape(N_DEV, -1, d), sels, both_links=True)
    rs_in.land(comm_only("rs_exchange_c_in", rs_in.step())[0])
    dh1, gots = mm_nn("in_bwd_x", dproj, wint, F32, tm=MM_TILE // 2, carry=rs_in.step())
    rs_in.land(gots)
    grad_x, dg_mixpre = first_bwd(dh1, xs, mix_pre_gain, dx1)
    out_in = rs_in.land(comm_only("rs_exchange_last_in", rs_in.step()), w_in_t, m_in_t, v_in_t)
    big_out = [out_in, out_out, out_up, out_down]

    n_heads = attn_sinks.shape[1]
    jac = probs[0] * probs[1]
    partial = _pack([sse, dsinks[0, :n_heads], daog, jnp.stack([dlb[0], dlb[0]]), jnp.sum(dng, axis=0),
                     dg_mixpre, dg_mixpost, dg_mlppre, dg_mlppost])
    ones = [jnp.ones(small_w[k].shape, F32) for k in _SMALL]
    ones[2] = jnp.stack([jac, -jac])
    scale = _pack([jnp.full((1,), 0.5 / d, F32)] + ones)
    zero = jnp.zeros((1,), F32)
    outs = small_allreduce_adamw(partial, scale, _pack([zero] + [small_w[k] for k in _SMALL]),
                                 _pack([zero] + [small_m[k] for k in _SMALL]),
                                 _pack([jnp.ones((1,), F32)] + [small_v[k] for k in _SMALL]))
    shapes = [(1,)] + [small_w[k].shape for k in _SMALL]
    sgrad, sdelta, snm, snv = (_unpack(o, shapes) for o in outs)
    loss = sgrad[0][0]

    def big(i, j):
        o = big_out[i][j]
        return (o.T if i == 0 else o)[None]

    def ordered(j, smalls):
        s = dict(zip(_SMALL, smalls[1:]))
        return [big(0, j), s["attn_sinks"], s["attn_out_gain"], s["rnn_lb_logits"], s["rnn_norm_gain"], big(1, j),
                s["mix_pre_gain"], s["mix_post_gain"], s["mlp_pre_gain"], s["mlp_post_gain"], big(2, j), big(3, j)]

    return (loss, grad_x[None], *ordered(0, sgrad), *ordered(1, sdelta), *ordered(2, snm), *ordered(3, snv))
```

```python
import math

import jax
import jax.numpy as jnp
from jax import lax
from jax.experimental import pallas as pl
from jax.experimental.pallas import tpu as pltpu

F32 = jnp.float32
BF16 = jnp.bfloat16

HEAD_DIM = 64
N_KV_HEADS = 2
BLOCK = 128
RNN_HEAD_DIM = 128
CHUNK = 64
SUB_FWD = 16
SUB_BWD = 16
EPS = 1e-6

ADAM_LR = 0.001
ADAM_B1 = 0.9
ADAM_B2 = 0.999
ADAM_EPS = 1e-08
ADAM_WD = 0.01
ADAM_STEP = 10

N_DEV = 8
LANES = 128
V7X_VMEM_LIMIT = 56 * 1024 * 1024
MESH = pl.DeviceIdType.MESH
HI = lax.Precision.HIGHEST
ANY = pl.BlockSpec(memory_space=pl.ANY)
_AXES = ("x", "y", "c")


def _cparams(sem=None, **kw):
    return pltpu.CompilerParams(dimension_semantics=sem, vmem_limit_bytes=V7X_VMEM_LIMIT, **kw)


def _dot(a, b, dims):
    return lax.dot_general(a.astype(BF16), b.astype(BF16), (dims, ((), ())), preferred_element_type=F32)


NN = ((1,), (0,))
NT = ((1,), (1,))
TN = ((0,), (0,))


def _pick(n, pref):
    t = min(n, pref)
    while n % t:
        t //= 2
    return t


def _tile(n, pref, mult=LANES):
    if n <= pref:
        return n
    t = pref - pref % mult
    while n % t:
        t -= mult
    return t


def _coords():
    return lax.axis_index("x"), lax.axis_index("y"), lax.axis_index("c")


def _slab_index(dev):
    return 4 * dev[0] + 2 * dev[1] + dev[2]


class _Part:
    def __init__(self, operands, landings, aliases, n_sems, plan):
        self.operands, self.landings, self.aliases, self.n_sems, self.plan = operands, landings, aliases, n_sems, plan


def _merge(*parts):
    operands, landings, aliases, plans = [], [], {}, []
    s0 = 0
    for p in parts:
        o0, l0 = len(operands), len(landings)
        aliases.update({o0 + i: l0 + j for i, j in p.aliases.items()})
        plans.append((p.plan, o0, len(p.operands), l0, len(p.landings), s0))
        operands += p.operands
        landings += p.landings
        s0 += p.n_sems

    def plan(ops, lands, sem):
        starts, waits = [], []
        for f, o0, no, l0, nl, off in plans:
            s, w = f(ops[o0:o0 + no], lands[l0:l0 + nl], lambda kind, k, off=off: sem(kind, off + k))
            starts += s
            waits += w
        return starts, waits

    return _Part(operands, landings, aliases, s0, plan)


def _gather_peers(x, y, c):
    return [(x, y, 1 - c), (1 - x, y, c), (x, 1 - y, c), (1 - x, 1 - y, c)]


def _gather_first(shard, rows=None, into=None, diagonal=True):
    lo, hi = (0, shard.shape[0]) if rows is None else rows
    n_peers = 4 if diagonal else 3

    def plan(ops, lands, sem):
        x, y, c = _coords()
        me, peers = (x, y, c), _gather_peers(x, y, c)[:n_peers]
        src = ops[0].at[pl.ds(lo, hi - lo)]

        def slab(block):
            return lands[0].at[_slab_index(block), pl.ds(lo, hi - lo)]

        def cp(k, block, to):
            return pltpu.make_async_remote_copy(
                src_ref=src, dst_ref=slab(block),
                send_sem=sem(0, k), recv_sem=sem(1, k), device_id=to, device_id_type=MESH)

        local = pltpu.make_async_copy(src, slab(me), sem(2, 0))
        sends = [cp(k, me, to) for k, to in enumerate(peers)]
        recvs = [cp(k, frm, me) for k, frm in enumerate(peers)]
        return ([local.start] + [s.start for s in sends],
                [local.wait] + [s.wait_send for s in sends] + [r.wait_recv for r in recvs])

    landing = jax.ShapeDtypeStruct((N_DEV, *shard.shape), shard.dtype)
    if into is None:
        return _Part([shard], [landing], {}, 4, plan)
    return _Part([shard, into], [landing], {1: 0}, 4, plan)


def _flip(dev, flips):
    return tuple(1 - v if f else v for v, f in zip(dev, flips))


def _pass_slabs(gathered, moves, then=()):
    def wave(lands, sem, k0, wave_moves):
        me = _coords()
        sends, recvs = [], []
        for k, (block, dest, rows) in enumerate(wave_moves, start=k0):
            lo, hi = (0, gathered.shape[1]) if rows is None else rows

            def cp(blk, to, k=k, lo=lo, hi=hi):
                slab = lands[0].at[_slab_index(blk), pl.ds(lo, hi - lo)]
                return pltpu.make_async_remote_copy(
                    src_ref=slab, dst_ref=slab, send_sem=sem(0, k), recv_sem=sem(1, k),
                    device_id=to, device_id_type=MESH)

            sends.append(cp(_flip(me, block), _flip(me, dest)))
            recvs.append(cp(_flip(_flip(me, dest), block), me))
        return [s.start for s in sends], [s.wait_send for s in sends] + [r.wait_recv for r in recvs]

    def plan(ops, lands, sem):
        starts, waits = wave(lands, sem, 0, moves)
        if then:
            starts2, waits2 = wave(lands, sem, len(moves), then)
            waits = waits + starts2 + waits2
        return starts, waits

    return _Part([gathered], [jax.ShapeDtypeStruct(gathered.shape, gathered.dtype)], {0: 0},
                 len(moves) + len(then), plan)


_X, _Y, _C, _XY = (1, 0, 0), (0, 1, 0), (0, 0, 1), (1, 1, 0)


def _gather_second(gathered):
    def plan(ops, lands, sem):
        x, y, c = _coords()
        sibling = (x, y, 1 - c)
        chips = [(1 - x, y), (x, 1 - y), (1 - x, 1 - y)]

        def cp(k, block):
            slab = lands[0].at[_slab_index(block)]
            return pltpu.make_async_remote_copy(
                src_ref=slab, dst_ref=slab, send_sem=sem(0, k), recv_sem=sem(1, k),
                device_id=sibling, device_id_type=MESH)

        sends = [cp(k, (*chip, c)) for k, chip in enumerate(chips)]
        recvs = [cp(k, (*chip, 1 - c)) for k, chip in enumerate(chips)]
        return [s.start for s in sends], [s.wait_send for s in sends] + [r.wait_recv for r in recvs]

    return _Part([gathered], [jax.ShapeDtypeStruct(gathered.shape, gathered.dtype)], {0: 0}, 3, plan)


def _scatter_step(array, axis, minor=None, rows=None):
    minor = (axis == "c") if minor is None else minor
    pieces = array.shape[0] if minor else array.shape[1]
    lo, hi = (0, array.shape[2]) if rows is None else rows

    def plan(ops, lands, sem):
        coords = list(_coords())
        ai = _AXES.index(axis)
        mine = coords[ai]
        peer = list(coords)
        peer[ai] = 1 - mine
        cps = []
        for p in range(pieces):
            src = ops[0].at[p, 1 - mine, pl.ds(lo, hi - lo)] if minor else ops[0].at[1 - mine, p, pl.ds(lo, hi - lo)]
            cps.append(pltpu.make_async_remote_copy(
                src_ref=src, dst_ref=lands[0].at[p], send_sem=sem(0, p), recv_sem=sem(1, p),
                device_id=tuple(peer), device_id_type=MESH))
        return [cp.start for cp in cps], [cp.wait for cp in cps]

    return _Part([array], [jax.ShapeDtypeStruct((pieces, hi - lo, array.shape[3]), array.dtype)], {}, pieces, plan)


def _grid_edges(grid):
    first = last = None
    for ax, n in enumerate(grid):
        p = pl.program_id(ax)
        f, l = p == 0, p == n - 1
        first = f if first is None else jnp.logical_and(first, f)
        last = l if last is None else jnp.logical_and(last, l)
    return first, last


def _call(body, *, name, grid, in_specs, out_specs, out_shape, args, scratch_shapes=(), sem=None, carry=None):
    if carry is None:
        return pl.pallas_call(
            body, name=name, grid=grid, in_specs=list(in_specs), out_specs=list(out_specs),
            out_shape=list(out_shape), scratch_shapes=list(scratch_shapes), compiler_params=_cparams(sem),
        )(*args)
    n_in, n_out, n_scr = len(in_specs), len(out_specs), len(scratch_shapes)
    n_cin, n_cout = len(carry.operands), len(carry.landings)

    def wrapped(*refs):
        ins, cins = refs[:n_in], refs[n_in:n_in + n_cin]
        o0 = n_in + n_cin
        outs, couts = refs[o0:o0 + n_out], refs[o0 + n_out:o0 + n_out + n_cout]
        s0 = o0 + n_out + n_cout
        scr, sems = refs[s0:s0 + n_scr], refs[s0 + n_scr:]
        first, last = _grid_edges(grid)

        def plan():
            return carry.plan(cins, couts, lambda kind, k: sems[kind].at[k])

        def start_all():
            for start in plan()[0]:
                start()

        def wait_all():
            for wait in plan()[1]:
                wait()

        if grid:
            pl.when(first)(start_all)
            body(*ins, *outs, *scr)
            pl.when(last)(wait_all)
        else:
            start_all()
            body(*ins, *outs, *scr)
            wait_all()

    sem_arrays = [pltpu.SemaphoreType.DMA((carry.n_sems,))] * 3
    res = pl.pallas_call(
        wrapped, name=name, grid=grid,
        in_specs=[*in_specs, *[ANY] * n_cin], out_specs=[*out_specs, *[ANY] * n_cout],
        out_shape=[*out_shape, *carry.landings],
        scratch_shapes=[*scratch_shapes, *sem_arrays],
        input_output_aliases={n_in + i: n_out + j for i, j in carry.aliases.items()},
        compiler_params=_cparams(("arbitrary",) * len(grid) if grid else None, has_side_effects=True),
    )(*args, *carry.operands)
    return res[:n_out], res[n_out:]


MM_TILE = 1024
MM_K_TILE = 2048
MXU_COLS = 256
MM_VMEM_BUDGET = 50 * 1024 * 1024


def _matmul(name, a, b, dims, grid, a_spec, b_spec, out_shape, out_spec, epilogue,
            extras=(), extra_specs=(), prologue=None, carry=None):
    nk = grid[2]
    n_extra = len(extras)
    acc_shape = out_spec.block_shape[-2:]

    def lhs(a_ref):
        return a_ref[...] if prologue is None else prologue(a_ref[...])

    def body_one(a_ref, b_ref, *rest):
        epilogue(_dot(lhs(a_ref), b_ref[...], dims), rest[:n_extra], rest[n_extra:])

    def body_acc(a_ref, b_ref, *rest):
        acc = rest[-1]
        k = pl.program_id(2)
        part = _dot(lhs(a_ref), b_ref[...], dims)

        @pl.when(k == 0)
        def _():
            acc[...] = part

        @pl.when(k > 0)
        def _():
            acc[...] += part

        @pl.when(k == nk - 1)
        def _():
            epilogue(acc[...], rest[:n_extra], rest[n_extra:-1])

    res = _call(body_one if nk == 1 else body_acc, name=name, grid=grid,
                in_specs=[a_spec, b_spec, *extra_specs], out_specs=[out_spec], out_shape=[out_shape],
                args=(a, b, *extras), scratch_shapes=[] if nk == 1 else [pltpu.VMEM(acc_shape, F32)],
                sem=("parallel", "parallel", "arbitrary"), carry=carry)
    return res[0] if carry is None else (res[0][0], res[1])


def _store_as(acc, extra_refs, out_refs):
    out_refs[0][...] = acc.astype(out_refs[0].dtype)


def _square(u):
    return u * u


def mm_nn(name, a, b, out_dtype, tk=None, tm=MM_TILE, tn=MM_TILE, prologue=None, carry=None):
    (m, kk), n = a.shape, b.shape[1]
    tm, tn = _tile(m, tm), _tile(n, tn, mult=MXU_COLS)
    tk = kk if tk is None else _tile(kk, tk, mult=MXU_COLS)
    return _matmul(name, a, b, NN, (m // tm, n // tn, kk // tk),
                   pl.BlockSpec((tm, tk), lambda i, j, k: (i, k)),
                   pl.BlockSpec((tk, tn), lambda i, j, k: (k, j)),
                   jax.ShapeDtypeStruct((m, n), out_dtype),
                   pl.BlockSpec((tm, tn), lambda i, j, k: (i, j)), _store_as, prologue=prologue, carry=carry)


def mm_nt(name, a, b, out_dtype, epilogue=_store_as, extras=(), extra_specs=(), tn=MM_TILE, carry=None):
    (m, kk), n = a.shape, b.shape[0]
    tm, tn = _tile(m, MM_TILE), _tile(n, tn, mult=MXU_COLS)
    return _matmul(name, a, b, NT, (m // tm, n // tn, 1),
                   pl.BlockSpec((tm, kk), lambda i, j, k: (i, 0)),
                   pl.BlockSpec((tn, kk), lambda i, j, k: (j, 0)),
                   jax.ShapeDtypeStruct((m, n), out_dtype),
                   pl.BlockSpec((tm, tn), lambda i, j, k: (i, j)), epilogue,
                   extras=extras, extra_specs=extra_specs, carry=carry)


def _whole_k_fits(tm, tn, kk, out_dtype, prologue):
    operands = 2 * 2 * kk * (tm + tn)
    out = 2 * tm * tn * jnp.dtype(out_dtype).itemsize + 4 * tm * tn
    return operands + out + (2 * kk * tm if prologue is not None else 0) <= MM_VMEM_BUDGET


def mm_tn(name, a, b, out_dtype, prologue=None, carry=None):
    (kk, m), n = a.shape, b.shape[1]
    tm, tn = _tile(m, MM_TILE), _tile(n, MM_TILE)
    tk = kk if _whole_k_fits(tm, tn, kk, out_dtype, prologue) else _tile(kk, MM_K_TILE)
    return _matmul(name, a, b, TN, (m // tm, n // tn, kk // tk),
                   pl.BlockSpec((tk, tm), lambda i, j, k: (k, i)),
                   pl.BlockSpec((tk, tn), lambda i, j, k: (k, j)),
                   jax.ShapeDtypeStruct((m, n), out_dtype),
                   pl.BlockSpec((tm, tn), lambda i, j, k: (i, j)), _store_as, prologue=prologue, carry=carry)


def up_proj(h2, wup_slabs):
    (m, kk), (_, _, ns) = h2.shape, wup_slabs.shape
    tm, tn = _tile(m, MM_TILE), _tile(ns, MM_TILE)
    r = ns // tn
    n = N_DEV * ns

    def epi(acc, extra_refs, out_refs):
        out_refs[0][...] = jnp.maximum(acc, 0.0).astype(BF16)

    return _matmul("up_proj", h2, wup_slabs, NN, (m // tm, n // tn, 1),
                   pl.BlockSpec((tm, kk), lambda i, j, k: (i, 0)),
                   pl.BlockSpec((None, kk, tn), lambda i, j, k: (j // r, 0, j % r)),
                   jax.ShapeDtypeStruct((m, n), BF16),
                   pl.BlockSpec((tm, tn), lambda i, j, k: (i, j)), epi)


def down_proj(u, wdown):
    return mm_nn("down_proj", u, wdown, F32, tm=MM_TILE // 2, tn=MM_TILE // 2, prologue=_square)


def down_bwd_act(dy, wdown, u):
    tm, tn = _tile(dy.shape[0], MM_TILE), _tile(wdown.shape[0], MM_TILE)

    def epi(acc, extra_refs, out_refs):
        out_refs[0][...] = (acc * (2.0 * extra_refs[0][...].astype(F32))).astype(BF16)

    return mm_nt("down_bwd_act", dy, wdown, BF16, epilogue=epi, extras=(u,),
                 extra_specs=(pl.BlockSpec((tm, tn), lambda i, j, k: (i, j)),))


def down_wgrad(u, dy):
    return mm_tn("down_wgrad", u, dy, BF16, prologue=_square)


def up_bwd_x(du, wup_slabs, carry=None):
    (m, kk), (slabs, n, ns) = du.shape, wup_slabs.shape
    tm, tn = _tile(m, MM_TILE // 2), _tile(n, MM_TILE // 2, mult=MXU_COLS)

    def body(a_ref, b_ref, o_ref):
        acc = _dot(a_ref[:, :ns], b_ref[0], NT)
        for s in range(1, slabs):
            acc = acc + _dot(a_ref[:, s * ns:(s + 1) * ns], b_ref[s], NT)
        o_ref[...] = acc

    res = _call(body, name="up_bwd_x", grid=(m // tm, n // tn),
                in_specs=[pl.BlockSpec((tm, kk), lambda i, j: (i, 0)),
                          pl.BlockSpec((slabs, tn, ns), lambda i, j: (0, j, 0))],
                out_specs=[pl.BlockSpec((tm, tn), lambda i, j: (i, j))],
                out_shape=[jax.ShapeDtypeStruct((m, n), F32)], args=(du, wup_slabs),
                sem=("parallel", "parallel"), carry=carry)
    return res[0] if carry is None else (res[0][0], res[1])


def up_wgrad(h2, du, carry=None):
    (kk, m), n = h2.shape, du.shape[1]
    ns = n // N_DEV
    tm, tn = _tile(m, MM_TILE), _tile(ns, MM_TILE)
    tk = kk if _whole_k_fits(tm, tn, kk, BF16, None) else _tile(kk, MM_K_TILE)
    r = ns // tn
    return _matmul("up_wgrad", h2, du, TN, (m // tm, n // tn, kk // tk),
                   pl.BlockSpec((tk, tm), lambda i, j, k: (k, i)),
                   pl.BlockSpec((tk, tn), lambda i, j, k: (k, j)),
                   jax.ShapeDtypeStruct((N_DEV, m, ns), BF16),
                   pl.BlockSpec((None, tm, tn), lambda i, j, k: (j // r, i, j % r)), _store_as, carry=carry)


def _rstd(x):
    return lax.rsqrt(jnp.mean(x * x, axis=-1, keepdims=True) + EPS)


def _norm_bwd(x, g, dy):
    r = _rstd(x)
    xh = x * r
    dyg = dy * g
    dx = r * (dyg - xh * jnp.mean(dyg * xh, axis=-1, keepdims=True))
    return dx, jnp.sum(dy * xh, axis=0, keepdims=True)


def _row_spec(tr, d):
    return pl.BlockSpec((tr, d), lambda i: (i, 0))


def _vec_spec(d):
    return pl.BlockSpec((1, d), lambda i: (0, 0))


def _accum(ref, val):
    @pl.when(pl.program_id(0) == 0)
    def _():
        ref[...] = jnp.zeros_like(ref)

    ref[...] += val


def pre_norm(x, g, carry=None, tr=256):
    t, d = x.shape
    tr = _pick(t, tr)

    def body(x_ref, g_ref, h_ref):
        xx = x_ref[...]
        h_ref[...] = (xx * _rstd(xx) * g_ref[...]).astype(BF16)

    return _call(body, name="pre_norm", grid=(t // tr,),
                 in_specs=[_row_spec(tr, d), _vec_spec(d)], out_specs=[_row_spec(tr, d)],
                 out_shape=[jax.ShapeDtypeStruct((t, d), BF16)], args=(x, g), sem=("parallel",), carry=carry)


def mid_fwd(mixed, g_post, x, g_pre2, tr=512):
    t, d = x.shape
    tr = _pick(t, tr)

    def body(m_ref, gp_ref, x_ref, g2_ref, x1_ref, h2_ref):
        mm = m_ref[...]
        x1 = x_ref[...] + mm * _rstd(mm) * gp_ref[...]
        x1_ref[...] = x1
        h2_ref[...] = (x1 * _rstd(x1) * g2_ref[...]).astype(BF16)

    return _call(body, name="mid_fwd", grid=(t // tr,),
                 in_specs=[_row_spec(tr, d), _vec_spec(d), _row_spec(tr, d), _vec_spec(d)],
                 out_specs=[_row_spec(tr, d), _row_spec(tr, d)],
                 out_shape=[jax.ShapeDtypeStruct((t, d), F32), jax.ShapeDtypeStruct((t, d), BF16)],
                 args=(mixed, g_post, x, g_pre2), sem=("parallel",))


def loss_bwd(y, g_post2, x1, target, tr=512):
    t, d = y.shape
    tr = _pick(t, tr)

    def body(y_ref, g_ref, x1_ref, t_ref, sse_ref, dout_ref, dy_ref, dg_ref):
        yy = y_ref[...]
        g = g_ref[...]
        err = x1_ref[...] + yy * _rstd(yy) * g - t_ref[...]
        _accum(sse_ref, jnp.sum(jnp.sum(err * err, axis=1, keepdims=True), axis=0, keepdims=True))
        dout = err * (1.0 / d)
        dout_ref[...] = dout
        dy, dg = _norm_bwd(yy, g, dout)
        dy_ref[...] = dy.astype(BF16)
        _accum(dg_ref, dg)

    return _call(body, name="loss_bwd", grid=(t // tr,),
                 in_specs=[_row_spec(tr, d), _vec_spec(d), _row_spec(tr, d), _row_spec(tr, d)],
                 out_specs=[pl.BlockSpec((1, 1), lambda i: (0, 0)), _row_spec(tr, d), _row_spec(tr, d), _vec_spec(d)],
                 out_shape=[jax.ShapeDtypeStruct((1, 1), F32), jax.ShapeDtypeStruct((t, d), F32),
                            jax.ShapeDtypeStruct((t, d), BF16), jax.ShapeDtypeStruct((1, d), F32)],
                 args=(y, g_post2, x1, target), sem=("arbitrary",))


def mid_bwd(dh2, x1, g_pre2, dout, mixed, g_post, carry=None, tr=256):
    t, d = x1.shape
    tr = _pick(t, tr)

    def body(dh_ref, x1_ref, g2_ref, do_ref, m_ref, gp_ref, dx1_ref, dm_ref, dg2_ref, dgp_ref):
        d1, dg2 = _norm_bwd(x1_ref[...], g2_ref[...], dh_ref[...])
        dx1 = do_ref[...] + d1
        dx1_ref[...] = dx1
        dm, dgp = _norm_bwd(m_ref[...], gp_ref[...], dx1)
        dm_ref[...] = dm.astype(BF16)
        _accum(dg2_ref, dg2)
        _accum(dgp_ref, dgp)

    return _call(body, name="mid_bwd", grid=(t // tr,),
                 in_specs=[_row_spec(tr, d), _row_spec(tr, d), _vec_spec(d), _row_spec(tr, d), _row_spec(tr, d),
                           _vec_spec(d)],
                 out_specs=[_row_spec(tr, d), _row_spec(tr, d), _vec_spec(d), _vec_spec(d)],
                 out_shape=[jax.ShapeDtypeStruct((t, d), F32), jax.ShapeDtypeStruct((t, d), BF16),
                            jax.ShapeDtypeStruct((1, d), F32), jax.ShapeDtypeStruct((1, d), F32)],
                 args=(dh2, x1, g_pre2, dout, mixed, g_post), sem=("arbitrary",), carry=carry)


def first_bwd(dh1, x, g_pre, dx1, carry=None, tr=512):
    t, d = x.shape
    tr = _pick(t, tr)

    def body(dh_ref, x_ref, g_ref, dx1_ref, gx_ref, dg_ref):
        d0, dg = _norm_bwd(x_ref[...], g_ref[...], dh_ref[...])
        gx_ref[...] = dx1_ref[...] + d0
        _accum(dg_ref, dg)

    return _call(body, name="first_bwd", grid=(t // tr,),
                 in_specs=[_row_spec(tr, d), _row_spec(tr, d), _vec_spec(d), _row_spec(tr, d)],
                 out_specs=[_row_spec(tr, d), _vec_spec(d)],
                 out_shape=[jax.ShapeDtypeStruct((t, d), F32), jax.ShapeDtypeStruct((1, d), F32)],
                 args=(dh1, x, g_pre, dx1), sem=("arbitrary",), carry=carry)


def _attn_geometry(has_prev):
    r = lax.broadcasted_iota(jnp.int32, (BLOCK, 2 * BLOCK), 0)
    c = lax.broadcasted_iota(jnp.int32, (BLOCK, 2 * BLOCK), 1)
    dist = r + BLOCK - c
    valid = jnp.logical_and(jnp.logical_and(dist >= 0, dist < BLOCK), jnp.logical_or(c >= BLOCK, has_prev))
    return dist.astype(F32), valid


def _stack_pairs(x, g, pairs):
    base = g * pairs * LANES
    return jnp.concatenate([x[:, base + p * LANES:base + (p + 1) * LANES] for p in range(pairs)], axis=0)


def _unstack_pairs(xs, pairs):
    return jnp.concatenate([xs[p * BLOCK:(p + 1) * BLOCK, :] for p in range(pairs)], axis=1)


def _to_half(x, g, odd):
    lane = lax.broadcasted_iota(jnp.int32, x.shape, 1)
    y = x if (g == 1) == odd else pltpu.roll(x, HEAD_DIM, axis=1)
    return jnp.where((lane >= HEAD_DIM) == odd, y, 0.0)


def _from_halves(even, odd, g):
    lane = lax.broadcasted_iota(jnp.int32, even.shape, 1)
    if g == 0:
        return jnp.where(lane < HEAD_DIM, even + pltpu.roll(odd, HEAD_DIM, axis=1), 0.0)
    return jnp.where(lane >= HEAD_DIM, pltpu.roll(even, HEAD_DIM, axis=1) + odd, 0.0)


_PARITIES = [(g, odd) for g in range(N_KV_HEADS) for odd in (False, True)]


def _softmax_sink(s, sink_ref, g, odd, group, n_heads, geo):
    dist, valid = geo
    pairs = group // 2
    heads = [g * group + 2 * p + int(odd) for p in range(pairs)]
    bias = jnp.concatenate([(2.0 ** (-8.0 * (h + 1) / n_heads)) * dist for h in heads], axis=0)
    sink = jnp.concatenate([jnp.full((BLOCK, 1), sink_ref[0, h], F32) for h in heads], axis=0)
    s = jnp.where(jnp.concatenate([valid] * pairs, axis=0), s - bias, -jnp.inf)
    m = jnp.maximum(jnp.max(s, axis=-1, keepdims=True), sink)
    p = jnp.exp(s - m)
    p_sink = jnp.exp(sink - m)
    inv = 1.0 / (jnp.sum(p, axis=-1, keepdims=True) + p_sink)
    return p * inv, p_sink * inv


def attn_fwd(proj, sinks, gain, aw, carry=None):
    t = proj.shape[0]
    kw = N_KV_HEADS * HEAD_DIM
    n_heads = aw // HEAD_DIM
    group = n_heads // N_KV_HEADS
    pairs = group // 2
    assert kw == LANES and group % 2 == 0
    nb = t // BLOCK
    scale = HEAD_DIM ** -0.5

    def body(sink_ref, q_ref, k_ref, v_ref, g_ref, o_ref, on_ref):
        n = pl.program_id(0)
        cur = pl.multiple_of(n * BLOCK, BLOCK)
        prev = pl.multiple_of(jnp.maximum(n - 1, 0) * BLOCK, BLOCK)
        geo = _attn_geometry(n > 0)
        kcat = jnp.concatenate([k_ref[pl.ds(prev, BLOCK), :], k_ref[pl.ds(cur, BLOCK), :]], axis=0)
        vcat = jnp.concatenate([v_ref[pl.ds(prev, BLOCK), :], v_ref[pl.ds(cur, BLOCK), :]], axis=0)
        q = q_ref[...] * scale
        groups = []
        for g in range(N_KV_HEADS):
            qs = _stack_pairs(q, g, pairs)
            o_pairs = None
            for odd in (False, True):
                s = _dot(qs, _to_half(kcat, g, odd), NT)
                p = _softmax_sink(s, sink_ref, g, odd, group, n_heads, geo)[0]
                o_half = _dot(p, _to_half(vcat, g, odd), NN)
                o_pairs = o_half if o_pairs is None else o_pairs + o_half
            groups.append(_unstack_pairs(o_pairs, pairs))
        o = jnp.concatenate(groups, axis=1)
        o_ref[...] = o
        on_ref[...] = (o * _rstd(o) * g_ref[...]).astype(BF16)

    return _call(body, name="attn_fwd", grid=(nb,),
                 in_specs=[pl.BlockSpec(memory_space=pltpu.SMEM),
                           pl.BlockSpec((BLOCK, aw), lambda n: (n, 0)),
                           pl.BlockSpec((t, kw), lambda n: (0, aw // kw)),
                           pl.BlockSpec((t, kw), lambda n: (0, aw // kw + 1)),
                           pl.BlockSpec((1, aw), lambda n: (0, 0))],
                 out_specs=[pl.BlockSpec((BLOCK, aw), lambda n: (n, 0)), pl.BlockSpec((BLOCK, aw), lambda n: (n, 0))],
                 out_shape=[jax.ShapeDtypeStruct((t, aw), F32), jax.ShapeDtypeStruct((t, aw), BF16)],
                 args=(sinks, proj, proj, proj, gain), sem=("parallel",), carry=carry)


def attn_bwd(proj, sinks, gain, attn_o, dcat, aw, carry=None):
    t = proj.shape[0]
    kw = N_KV_HEADS * HEAD_DIM
    n_heads = aw // HEAD_DIM
    group = n_heads // N_KV_HEADS
    pairs = group // 2
    assert kw == LANES and group % 2 == 0
    nb = t // BLOCK
    scale = HEAD_DIM ** -0.5

    def body(sink_ref, q_ref, k_ref, v_ref, g_ref, o_ref, dn_ref, dq_ref, dk_ref, dv_ref, dsink_ref, dg_ref):
        n = pl.program_id(0)
        cur = pl.multiple_of(n * BLOCK, BLOCK)
        prev = pl.multiple_of(jnp.maximum(n - 1, 0) * BLOCK, BLOCK)
        geo = _attn_geometry(n > 0)

        @pl.when(n == 0)
        def _():
            dk_ref[...] = jnp.zeros_like(dk_ref)
            dv_ref[...] = jnp.zeros_like(dv_ref)
            dsink_ref[...] = jnp.zeros_like(dsink_ref)

        o = o_ref[...]
        do_all, dg = _norm_bwd(o, g_ref[...], dn_ref[...])
        _accum(dg_ref, dg)
        kcat = jnp.concatenate([k_ref[pl.ds(prev, BLOCK), :], k_ref[pl.ds(cur, BLOCK), :]], axis=0)
        vcat = jnp.concatenate([v_ref[pl.ds(prev, BLOCK), :], v_ref[pl.ds(cur, BLOCK), :]], axis=0)
        q = q_ref[...] * scale
        lane = lax.broadcasted_iota(jnp.int32, (1, LANES), 1)
        lane_s = lax.broadcasted_iota(jnp.int32, (pairs * BLOCK, LANES), 1)
        qs = [_stack_pairs(q, g, pairs) for g in range(N_KV_HEADS)]
        dos = [_stack_pairs(do_all, g, pairs) for g in range(N_KV_HEADS)]
        kxs = [_to_half(kcat, g, odd) for g, odd in _PARITIES]
        scores = [_dot(qs[g], kx, NT) for kx, (g, odd) in zip(kxs, _PARITIES)]
        dps = [_dot(dos[g], _to_half(vcat, g, odd), NT) for g, odd in _PARITIES]
        deltas = []
        for g in range(N_KV_HEADS):
            prod = dos[g] * _stack_pairs(o, g, pairs)
            delta_even = jnp.sum(jnp.where(lane_s < HEAD_DIM, prod, 0.0), axis=-1, keepdims=True)
            deltas += [delta_even, jnp.sum(prod, axis=-1, keepdims=True) - delta_even]
        dsink = jnp.zeros((1, LANES), F32)
        ps, dss = [], []
        for i, (g, odd) in enumerate(_PARITIES):
            p, p_sink = _softmax_sink(scores[i], sink_ref, g, odd, group, n_heads, geo)
            ps.append(p)
            dss.append(p * (dps[i] - deltas[i]))
            sink_rows = p_sink * deltas[i]
            for pr in range(pairs):
                h = g * group + 2 * pr + int(odd)
                dsink = dsink + jnp.where(
                    lane == h, -jnp.sum(sink_rows[pr * BLOCK:(pr + 1) * BLOCK], axis=0, keepdims=True), 0.0)
        dq_pairs = [_dot(ds, kx, NN) for ds, kx in zip(dss, kxs)]
        dk_halves = [_dot(ds, qs[g], TN) for ds, (g, odd) in zip(dss, _PARITIES)]
        dv_halves = [_dot(p, dos[g], TN) for p, (g, odd) in zip(ps, _PARITIES)]
        dq_ref[...] = jnp.concatenate(
            [_unstack_pairs((dq_pairs[2 * g] + dq_pairs[2 * g + 1]) * scale, pairs) for g in range(N_KV_HEADS)],
            axis=1).astype(BF16)
        dk_upd = _from_halves(dk_halves[0], dk_halves[1], 0) + _from_halves(dk_halves[2], dk_halves[3], 1)
        dv_upd = _from_halves(dv_halves[0], dv_halves[1], 0) + _from_halves(dv_halves[2], dv_halves[3], 1)
        dk_ref[pl.ds(prev, BLOCK), :] += dk_upd[:BLOCK]
        dv_ref[pl.ds(prev, BLOCK), :] += dv_upd[:BLOCK]
        dk_ref[pl.ds(cur, BLOCK), :] += dk_upd[BLOCK:]
        dv_ref[pl.ds(cur, BLOCK), :] += dv_upd[BLOCK:]
        dsink_ref[...] += dsink

    return _call(body, name="attn_bwd", grid=(nb,),
                 in_specs=[pl.BlockSpec(memory_space=pltpu.SMEM),
                           pl.BlockSpec((BLOCK, aw), lambda n: (n, 0)),
                           pl.BlockSpec((t, kw), lambda n: (0, aw // kw)),
                           pl.BlockSpec((t, kw), lambda n: (0, aw // kw + 1)),
                           pl.BlockSpec((1, aw), lambda n: (0, 0)),
                           pl.BlockSpec((BLOCK, aw), lambda n: (n, 0)),
                           pl.BlockSpec((BLOCK, aw), lambda n: (n, 0))],
                 out_specs=[pl.BlockSpec((BLOCK, aw), lambda n: (n, 0)),
                            pl.BlockSpec((t, kw), lambda n: (0, 0)), pl.BlockSpec((t, kw), lambda n: (0, 0)),
                            pl.BlockSpec((1, LANES), lambda n: (0, 0)), pl.BlockSpec((1, aw), lambda n: (0, 0))],
                 out_shape=[jax.ShapeDtypeStruct((t, aw), BF16), jax.ShapeDtypeStruct((t, kw), F32),
                            jax.ShapeDtypeStruct((t, kw), F32), jax.ShapeDtypeStruct((1, LANES), F32),
                            jax.ShapeDtypeStruct((1, aw), F32)],
                 args=(sinks, proj, proj, proj, gain, attn_o, dcat), sem=("arbitrary",), carry=carry)


def _sigmoid(x):
    return 0.5 * jnp.tanh(0.5 * x) + 0.5


def _chunk_geometry():
    row = lax.broadcasted_iota(jnp.int32, (CHUNK, CHUNK), 0)
    col = lax.broadcasted_iota(jnp.int32, (CHUNK, CHUNK), 1)
    return row, col


def _cumsum_rows(x, reverse=False):
    row, col = _chunk_geometry()
    tri = (col >= row) if reverse else (col <= row)
    return lax.dot_general(tri.astype(F32), x, ((NN), ((), ())), precision=HI, preferred_element_type=F32)


def _rep_sub(x4, sub):
    k = x4.shape[-1]
    return jnp.broadcast_to(x4[:, None, :], (CHUNK // sub, sub, k)).reshape(CHUNK, k)


def _gates(q_r, f_r, lb):
    sg = _sigmoid(f_r)
    f = lb + (1.0 - lb) * sg
    sq = _sigmoid(q_r)
    return sg, f, sq, q_r * sq


def _offdiag_terms(b, j, sub):
    c = b[j * sub + sub - 1:j * sub + sub, :]
    return jnp.exp(jnp.minimum(b - c, 0.0)), jnp.exp(jnp.minimum(c - b, 0.0))


def _store_heads(ref, x):
    for j in range(ref.shape[0]):
        ref[j] = x[:, _head(j)]


def _sub_rows(ref, r, sub):
    rows = [ref[j, pl.ds(r, CHUNK // sub, stride=sub), :] for j in range(ref.shape[0])]
    return _rep_sub(jnp.concatenate(rows, axis=1), sub)


def _diag_mask(sub):
    row, col = _chunk_geometry()
    return jnp.logical_and((row // sub) == (col // sub), row >= col)


HGRN_HEADS_PER_STEP = 8


def _wide(refs):
    return jnp.concatenate([r[...] for r in refs], axis=1)


def _head(j):
    return slice(j * RNN_HEAD_DIM, (j + 1) * RNN_HEAD_DIM)


def _cat_heads(parts, hs):
    return jnp.concatenate([p[:, hs] for p in parts], axis=1)


def _offdiag_factors(q, k, b, sub):
    rowi = lax.broadcasted_iota(jnp.int32, b.shape, 0)
    qs, ks, ers, ecs = [], [], [], []
    for j in range(CHUNK // sub - 1):
        e_row, e_col = _offdiag_terms(b, j, sub)
        e_row = jnp.where(rowi >= (j + 1) * sub, e_row, 0.0)
        e_col = jnp.where((rowi // sub) == j, e_col, 0.0)
        qs.append(q * e_row)
        ks.append(k * e_col)
        ers.append(e_row)
        ecs.append(e_col)
    return qs, ks, ers, ecs


def hgrn_fwd(proj, attn_n, lb, norm_gain, col0, rw, carry=None):
    t, aw = attn_n.shape
    nh = rw // RNN_HEAD_DIM
    nc = t // CHUNK
    kd = RNN_HEAD_DIM
    cb = col0 // kd
    sub = SUB_FWD
    nsub = CHUNK // sub
    hp = nh
    assert nh <= HGRN_HEADS_PER_STEP
    w = hp * kd

    def body(*refs):
        q_refs, f_refs, i_refs, g_refs = (refs[i * hp:(i + 1) * hp] for i in range(4))
        lb_ref, ng_ref, an_ref, cat_ref, o_ref, att_ref, st_ref, state, b_ref, k_ref = refs[4 * hp:]
        c = pl.program_id(1)

        @pl.when(c == 0)
        def _():
            state[...] = jnp.zeros_like(state)

        st_ref[...] = state[...]
        q_r, f_r, v, g_r = (_wide(rs) for rs in (q_refs, f_refs, i_refs, g_refs))
        _, f, _, q = _gates(q_r, f_r, lb_ref[...])
        k = 1.0 - f
        b = _cumsum_rows(jnp.log(f))
        _store_heads(b_ref, b)
        _store_heads(k_ref, k)
        qcat, kcat, _, _ = _offdiag_factors(q, k, b, sub)
        row, col = _chunk_geometry()
        same = (row // sub) == (col // sub)
        rloc = lax.broadcasted_iota(jnp.int32, (CHUNK, w), 0) % sub
        diag = [jnp.zeros((CHUNK, CHUNK), F32)] * hp
        for r in range(sub):
            bs = _sub_rows(b_ref, r, sub)
            ks = _sub_rows(k_ref, r, sub)
            prod = q * jnp.exp(jnp.where(rloc >= r, b - bs, -jnp.inf)) * ks
            place = jnp.logical_and((col % sub) == r, same)
            diag = [jnp.where(place, jnp.sum(prod[:, _head(j)], axis=-1, keepdims=True), diag[j]) for j in range(hp)]
        b_last = b[CHUNK - 1:CHUNK, :]
        qe = q * jnp.exp(b)
        kdec = k * jnp.exp(b_last - b)
        decay = jnp.exp(b_last)
        outs, normed, states = [], [], []
        for j in range(hp):
            hs = _head(j)
            att = diag[j] + _dot(_cat_heads(qcat, hs), _cat_heads(kcat, hs), NT)
            att_ref[j] = att
            sj = state[j]
            o = _dot(qe[:, hs], sj, NT) + _dot(att, v[:, hs], NN)
            outs.append(o)
            normed.append(o * _rstd(o))
            states.append(sj * decay[:, hs] + _dot(v[:, hs], kdec[:, hs], TN))
        for j in range(hp):
            state[j] = states[j]
        o_ref[...] = jnp.concatenate(outs, axis=1)
        gate = g_r * _sigmoid(g_r)
        cat_ref[:, :aw] = an_ref[...]
        cat_ref[:, aw:] = (jnp.concatenate(normed, axis=1) * jnp.tile(ng_ref[...], (1, hp)) * gate).astype(BF16)

    def col(kidx, j):
        return pl.BlockSpec((CHUNK, kd), lambda hg, c: (c, cb + kidx * nh + hg * hp + j))

    return _call(body, name="hgrn_fwd", grid=(1, nc),
                 in_specs=[col(kidx, j) for kidx in range(4) for j in range(hp)] +
                          [pl.BlockSpec((1, w), lambda hg, c: (0, hg)), pl.BlockSpec((1, kd), lambda hg, c: (0, 0)),
                           pl.BlockSpec((CHUNK, aw), lambda hg, c: (c, 0))],
                 out_specs=[pl.BlockSpec((CHUNK, aw + w), lambda hg, c: (c, 0)),
                            pl.BlockSpec((CHUNK, w), lambda hg, c: (c, hg)),
                            pl.BlockSpec((hp, CHUNK, CHUNK), lambda hg, c: (hg, c, 0)),
                            pl.BlockSpec((None, hp, kd, kd), lambda hg, c: (c, hg, 0, 0))],
                 out_shape=[jax.ShapeDtypeStruct((t, aw + rw), BF16), jax.ShapeDtypeStruct((t, rw), F32),
                            jax.ShapeDtypeStruct((nh, t, CHUNK), F32), jax.ShapeDtypeStruct((nc, nh, kd, kd), F32)],
                 args=(*([proj] * (4 * hp)), lb, norm_gain, attn_n),
                 scratch_shapes=[pltpu.VMEM((hp, kd, kd), F32), pltpu.VMEM((hp, CHUNK, kd), F32),
                                 pltpu.VMEM((hp, CHUNK, kd), F32)],
                 sem=("parallel", "arbitrary"), carry=carry)


def hgrn_bwd(proj, lb, norm_gain, o_all, att_all, st_all, dcat, dq_a, dk_a, dv_a, col0, rw, carry=None):
    t, iw = proj.shape
    aw, kw = dq_a.shape[1], dk_a.shape[1]
    nh = rw // RNN_HEAD_DIM
    nc = t // CHUNK
    kd = RNN_HEAD_DIM
    cb = col0 // kd
    sub = SUB_BWD
    nsub = CHUNK // sub
    dcb = (dcat.shape[1] - rw) // kd
    hp = nh
    assert nh <= HGRN_HEADS_PER_STEP and dcb % hp == 0 and col0 == aw + 2 * kw and iw == col0 + 4 * rw
    w = hp * kd

    def per_head(x, fn):
        return jnp.concatenate([jnp.broadcast_to(fn(x[:, _head(j)]), (CHUNK, kd)) for j in range(hp)], axis=1)

    def body(*refs):
        q_refs, f_refs, i_refs, g_refs = (refs[i * hp:(i + 1) * hp] for i in range(4))
        (lb_ref, ng_ref, o_ref, att_ref, st0_ref, st1_ref, d_ref, dqa_ref, dka_ref, dva_ref, dp_ref, dlb_ref, dng_ref,
         dstate, b_ref, k_ref, dks_ref) = refs[4 * hp:]
        ci = pl.program_id(1)

        @pl.when(ci == 0)
        def _():
            dstate[...] = jnp.zeros_like(dstate)
            dlb_ref[...] = jnp.zeros_like(dlb_ref)
            dng_ref[...] = jnp.zeros_like(dng_ref)

        lbv = lb_ref[...]
        q_r, f_r, v, g_r = (_wide(rs) for rs in (q_refs, f_refs, i_refs, g_refs))
        sg, f, sq, q = _gates(q_r, f_r, lbv)
        k = 1.0 - f
        b = _cumsum_rows(jnp.log(f))
        _store_heads(b_ref, b)
        _store_heads(k_ref, k)
        row, col = _chunk_geometry()

        o = o_ref[...]
        ng = jnp.tile(ng_ref[...], (1, hp))
        sgg = _sigmoid(g_r)
        gate = g_r * sgg
        d_rnn = d_ref[...]
        r = per_head(o, _rstd)
        oh = o * r
        dp_ref[:, :aw] = dqa_ref[...]
        dp_ref[:, aw:aw + kw] = dka_ref[...].astype(BF16)
        dp_ref[:, aw + kw:col0] = dva_ref[...].astype(BF16)
        dp_ref[:, col0 + 3 * rw:] = (d_rnn * oh * ng * (sgg * (1.0 + g_r * (1.0 - sgg)))).astype(BF16)
        d_on = d_rnn * gate
        dng_rows = jnp.sum(d_on * oh, axis=0, keepdims=True)
        dng = dng_rows[:, _head(0)]
        for j in range(1, hp):
            dng = dng + dng_rows[:, _head(j)]
        dng_ref[...] += dng
        dyg = d_on * ng
        do = r * (dyg - oh * per_head(dyg * oh, lambda x: jnp.mean(x, axis=-1, keepdims=True)))

        b_last = b[CHUNK - 1:CHUNK, :]
        eb = jnp.exp(b)
        tail = jnp.exp(b_last - b)
        kdec = k * tail
        decay = jnp.exp(b_last)
        qe = q * eb
        qcat, kcat, ers, ecs = _offdiag_factors(q, k, b, sub)
        diag_mask = _diag_mask(sub)
        dqs, dks, dvs, dads, gsums, dstates = [], [], [], [], [], []
        for j in range(hp):
            hs = _head(j)
            do_h, v_h, dst = do[:, hs], v[:, hs], dstate[j]
            da = jnp.where(row >= col, _dot(do_h, v_h, NT), 0.0)
            dads.append(jnp.where(diag_mask, da, 0.0))
            dq = _dot(do_h, st0_ref[j], NN) * eb[:, hs]
            dk = _dot(v_h, dst, NN) * tail[:, hs]
            dvs.append(_dot(att_ref[j], do_h, TN) + _dot(kdec[:, hs], dst, NT))
            rq = _dot(da, _cat_heads(kcat, hs), NN)
            rk = _dot(da, _cat_heads(qcat, hs), TN)
            for jj in range(nsub - 1):
                dq = dq + ers[jj][:, hs] * rq[:, _head(jj)]
                dk = dk + ecs[jj][:, hs] * rk[:, _head(jj)]
            dqs.append(dq)
            dks.append(dk)
            gsums.append(jnp.sum(dst * st1_ref[j], axis=0, keepdims=True))
            dstates.append(dst * decay[:, hs] + _dot(do_h, qe[:, hs], TN))
        for j in range(hp):
            dstate[j] = dstates[j]
        dq = jnp.concatenate(dqs, axis=1)
        dk = jnp.concatenate(dks, axis=1)
        rloc = lax.broadcasted_iota(jnp.int32, (CHUNK, w), 0) % sub
        for rr in range(sub):
            bs = _sub_rows(b_ref, rr, sub)
            ks = _sub_rows(k_ref, rr, sub)
            e = jnp.exp(jnp.where(rloc >= rr, b - bs, -jnp.inf))
            pick = (col % sub) == rr
            dacol = jnp.concatenate(
                [jnp.broadcast_to(jnp.sum(jnp.where(pick, dads[j], 0.0), axis=-1, keepdims=True), (CHUNK, kd))
                 for j in range(hp)], axis=1)
            wv = dacol * e
            dq = dq + wv * ks
            sums = jnp.sum((wv * q).reshape(nsub, sub, w), axis=1)
            for j in range(hp):
                dks_ref[j, pl.ds(rr, nsub, stride=sub), :] = sums[:, _head(j)]
        dk = dk + jnp.concatenate([dks_ref[j] for j in range(hp)], axis=1)

        dlf = _cumsum_rows(q * dq - k * dk, reverse=True) + jnp.concatenate(gsums, axis=1)
        dfv = dlf / f - dk
        dp_ref[:, col0 + rw:col0 + 2 * rw] = (dfv * (1.0 - lbv) * sg * (1.0 - sg)).astype(BF16)
        dlb_ref[...] += jnp.sum(dfv * (1.0 - sg), axis=0, keepdims=True)
        dp_ref[:, col0:col0 + rw] = (dq * (sq * (1.0 + q_r * (1.0 - sq)))).astype(BF16)
        dp_ref[:, col0 + 2 * rw:col0 + 3 * rw] = jnp.concatenate(dvs, axis=1).astype(BF16)

    def rev(c):
        return nc - 1 - c

    def col_in(kidx, j):
        return pl.BlockSpec((CHUNK, kd), lambda hg, c: (rev(c), cb + kidx * nh + hg * hp + j))

    def rows(width):
        return pl.BlockSpec((CHUNK, width), lambda hg, c: (rev(c), 0))

    return _call(body, name="hgrn_bwd", grid=(1, nc),
                 in_specs=[col_in(kidx, j) for kidx in range(4) for j in range(hp)] +
                          [pl.BlockSpec((1, w), lambda hg, c: (0, hg)), pl.BlockSpec((1, kd), lambda hg, c: (0, 0)),
                           rows(w),
                           pl.BlockSpec((hp, CHUNK, CHUNK), lambda hg, c: (hg, rev(c), 0)),
                           pl.BlockSpec((None, hp, kd, kd), lambda hg, c: (rev(c), hg, 0, 0)),
                           pl.BlockSpec((None, hp, kd, kd),
                                        lambda hg, c: (jnp.minimum(rev(c) + 1, nc - 1), hg, 0, 0)),
                           pl.BlockSpec((CHUNK, w), lambda hg, c: (rev(c), dcb // hp + hg)),
                           rows(aw), rows(kw), rows(kw)],
                 out_specs=[rows(iw),
                            pl.BlockSpec((1, w), lambda hg, c: (0, hg)),
                            pl.BlockSpec((None, 1, kd), lambda hg, c: (hg, 0, 0))],
                 out_shape=[jax.ShapeDtypeStruct((t, iw), BF16), jax.ShapeDtypeStruct((1, rw), F32),
                            jax.ShapeDtypeStruct((1, 1, kd), F32)],
                 args=(*([proj] * (4 * hp)), lb, norm_gain, o_all, att_all, st_all, st_all, dcat, dq_a, dk_a, dv_a),
                 scratch_shapes=[pltpu.VMEM((hp, kd, kd), F32), pltpu.VMEM((hp, CHUNK, kd), F32),
                                 pltpu.VMEM((hp, CHUNK, kd), F32), pltpu.VMEM((hp, CHUNK, kd), F32)],
                 sem=("parallel", "arbitrary"), carry=carry)


def comm_only(name, part):
    return _call(lambda: None, name=name, grid=(), in_specs=[], out_specs=[], out_shape=[], args=(), carry=part)[1]


ADD_BLOCK_ELEMS = 1 << 21
ADAMW_BLOCK_ELEMS = 1 << 19


def add_kept_half(name, kept, got, sel, minor, row0=0):
    pieces, rows, cols = got.shape
    tr = _tile(rows, max(16, ADD_BLOCK_ELEMS // cols), mult=16)
    assert row0 % tr == 0
    i0 = row0 // tr

    def body(sel_ref, k_ref, g_ref, o_ref):
        o_ref[...] = (k_ref[...].astype(F32) + g_ref[...].astype(F32)).astype(o_ref.dtype)

    kept_spec = (pl.BlockSpec((None, None, tr, cols), lambda p, i, s: (p, s[0], i + i0, 0)) if minor else
                 pl.BlockSpec((None, None, tr, cols), lambda p, i, s: (s[0], p, i + i0, 0)))
    return pl.pallas_call(
        body, name=name,
        grid_spec=pltpu.PrefetchScalarGridSpec(
            num_scalar_prefetch=1, grid=(pieces, rows // tr),
            in_specs=[kept_spec, pl.BlockSpec((None, tr, cols), lambda p, i, s: (p, i, 0))],
            out_specs=pl.BlockSpec((None, tr, cols), lambda p, i, s: (p, i, 0))),
        out_shape=jax.ShapeDtypeStruct(got.shape, got.dtype),
        compiler_params=_cparams(("parallel", "parallel")),
    )(sel, kept, got)


def _adamw(w, g, m, v):
    m = ADAM_B1 * m + (1.0 - ADAM_B1) * g
    v = ADAM_B2 * v + (1.0 - ADAM_B2) * (g * g)
    m_hat = m / (1.0 - ADAM_B1 ** ADAM_STEP)
    v_hat = v / (1.0 - ADAM_B2 ** ADAM_STEP)
    delta = -ADAM_LR * (m_hat / (jnp.sqrt(v_hat) + ADAM_EPS) + ADAM_WD * w)
    return delta, m, v


def add_adamw(name, kept, got, sel, w, m, v, row0=0, into=None):
    _, rows, cols = got.shape
    tr = _tile(rows, max(16, ADAMW_BLOCK_ELEMS // cols), mult=16)
    assert row0 % tr == 0
    i0 = row0 // tr
    n_into = 0 if into is None else len(into)

    def body(sel_ref, k_ref, g_ref, w_ref, m_ref, v_ref, *rest):
        go_ref, d_ref, mo_ref, vo_ref = rest[n_into:]
        g = k_ref[...].astype(F32) + g_ref[...].astype(F32)
        go_ref[...] = g
        d_ref[...], mo_ref[...], vo_ref[...] = _adamw(w_ref[...], g, m_ref[...], v_ref[...])

    shard_tile = pl.BlockSpec((tr, cols), lambda i, s: (i + i0, 0))
    return pl.pallas_call(
        body, name=name,
        grid_spec=pltpu.PrefetchScalarGridSpec(
            num_scalar_prefetch=1, grid=(rows // tr,),
            in_specs=[pl.BlockSpec((None, None, tr, cols), lambda i, s: (s[0], 0, i, 0)),
                      pl.BlockSpec((None, tr, cols), lambda i, s: (0, i, 0)), shard_tile, shard_tile, shard_tile,
                      *[ANY] * n_into],
            out_specs=[shard_tile] * 4),
        out_shape=[jax.ShapeDtypeStruct(w.shape, F32)] * 4,
        input_output_aliases={6 + k: k for k in range(n_into)},
        compiler_params=_cparams(("parallel",)),
    )(sel, kept, got, w, m, v, *(into or ()))


def small_allreduce_adamw(partial, scale, w, m, v):
    rows = partial.shape[0]

    def body(p_ref, s_ref, w_ref, m_ref, v_ref, g_ref, d_ref, mo_ref, vo_ref, slots, send_sems, recv_sems):
        x, y, c = _coords()
        my_slot = _slab_index((x, y, c))
        slots[my_slot] = p_ref[...]
        copies = []
        for mask in range(1, N_DEV):
            to = tuple(1 - v_ if (mask >> s_) & 1 else v_ for v_, s_ in ((x, 2), (y, 1), (c, 0)))
            copies.append(pltpu.make_async_remote_copy(
                src_ref=p_ref, dst_ref=slots.at[my_slot],
                send_sem=send_sems.at[mask - 1], recv_sem=recv_sems.at[mask - 1],
                device_id=to, device_id_type=MESH))
        for cp in copies:
            cp.start()
        for cp in copies:
            cp.wait()
        total = slots[0]
        for b in range(1, N_DEV):
            total = total + slots[b]
        g = total * s_ref[...]
        g_ref[...] = g
        d_ref[...], mo_ref[...], vo_ref[...] = _adamw(w_ref[...], g, m_ref[...], v_ref[...])

    vm = pl.BlockSpec(memory_space=pltpu.VMEM)
    return pl.pallas_call(
        body, name="small_allreduce_adamw",
        in_specs=[vm] * 5, out_specs=[vm] * 4,
        out_shape=[jax.ShapeDtypeStruct((rows, LANES), F32)] * 4,
        scratch_shapes=[pltpu.VMEM((N_DEV, rows, LANES), F32),
                        pltpu.SemaphoreType.DMA((N_DEV - 1,)), pltpu.SemaphoreType.DMA((N_DEV - 1,))],
        compiler_params=pltpu.CompilerParams(has_side_effects=True),
    )(partial, scale, w, m, v)


_SMALL = ("attn_sinks", "attn_out_gain", "rnn_lb_logits", "rnn_norm_gain", "mix_pre_gain", "mix_post_gain",
          "mlp_pre_gain", "mlp_post_gain")


def _pack(parts):
    rows = []
    for p in parts:
        flat = p.reshape(-1).astype(F32)
        pad = (-flat.shape[0]) % LANES
        rows.append(jnp.pad(flat, (0, pad)).reshape(-1, LANES))
    packed = jnp.concatenate(rows, axis=0)
    pad_rows = (-packed.shape[0]) % 8
    return jnp.pad(packed, ((0, pad_rows), (0, 0)))


def _unpack(packed, shapes):
    out, r = [], 0
    for s in shapes:
        size = math.prod(s)
        nrows = -(-size // LANES)
        out.append(packed[r:r + nrows].reshape(-1)[:size].reshape(s))
        r += nrows
    return out


class _Scatter:
    def __init__(self, tag, grad, sels, both_links=False):
        self.tag, self.sels, self.both = tag, sels, both_links
        self.shape = grad.shape[1:]
        self.half = self.shape[0] // 2
        self.cur = grad.reshape(4, 2, *self.shape)
        self.stage = 0

    def step(self):
        if self.stage == 0 or not self.both:
            return _scatter_step(self.cur, "cxy"[self.stage])
        if self.stage == 1:
            return _merge(_scatter_step(self.cur, "x", rows=(0, self.half)),
                          _scatter_step(self.cur, "y", minor=True, rows=(self.half, self.shape[0])))
        upper, lower = self.cur
        return _merge(_scatter_step(upper, "y"), _scatter_step(lower, "x"))

    def land(self, got, w=None, m=None, v=None):
        stage, tag, sels = self.stage, self.tag, self.sels
        self.stage += 1
        if stage == 0 or not self.both:
            axis = "cxy"[stage]
            name = "rs_add_%s_%s" % (axis, tag)
            if axis == "y":
                return add_adamw(name, self.cur, got, sels[axis], w, m, v)
            summed = add_kept_half(name, self.cur, got, sels[axis], minor=axis == "c")
            self.cur = summed.reshape(2, summed.shape[0] // 2, *self.shape)
            return None
        got_upper, got_lower = got
        if stage == 1:
            upper = add_kept_half("rs_add_x_%s_upper" % tag, self.cur, got_upper, sels["x"], minor=False)
            lower = add_kept_half("rs_add_y_%s_lower" % tag, self.cur, got_lower, sels["y"], minor=True,
                                  row0=self.half)
            self.cur = tuple(s.reshape(2, 1, *s.shape[1:]) for s in (upper, lower))
            return None
        upper, lower = self.cur
        out_upper = add_adamw("rs_add_y_%s_upper" % tag, upper, got_upper, sels["y"], w, m, v)
        return add_adamw("rs_add_x_%s_lower" % tag, lower, got_lower, sels["x"], w, m, v, row0=self.half,
                         into=out_upper)


def kernel(x, w_in, attn_sinks, attn_out_gain, rnn_lb_logits, rnn_norm_gain, w_out, mix_pre_gain, mix_post_gain, mlp_pre_gain, mlp_post_gain, w_up, w_down, loss_target, m_w_in, m_attn_sinks, m_attn_out_gain, m_rnn_lb_logits, m_rnn_norm_gain, m_w_out, m_mix_pre_gain, m_mix_post_gain, m_mlp_pre_gain, m_mlp_post_gain, m_w_up, m_w_down, v_w_in, v_attn_sinks, v_attn_out_gain, v_rnn_lb_logits, v_rnn_norm_gain, v_w_out, v_mix_pre_gain, v_mix_post_gain, v_mlp_pre_gain, v_mlp_post_gain, v_w_up, v_w_down):
    xs, target = x[0], loss_target[0]
    t, d = xs.shape
    aw = d // 2
    rw = d - aw
    col0 = aw + 2 * N_KV_HEADS * HEAD_DIM
    small_w = dict(attn_sinks=attn_sinks, attn_out_gain=attn_out_gain, rnn_lb_logits=rnn_lb_logits,
                   rnn_norm_gain=rnn_norm_gain, mix_pre_gain=mix_pre_gain, mix_post_gain=mix_post_gain,
                   mlp_pre_gain=mlp_pre_gain, mlp_post_gain=mlp_post_gain)
    small_m = dict(attn_sinks=m_attn_sinks, attn_out_gain=m_attn_out_gain, rnn_lb_logits=m_rnn_lb_logits,
                   rnn_norm_gain=m_rnn_norm_gain, mix_pre_gain=m_mix_pre_gain, mix_post_gain=m_mix_post_gain,
                   mlp_pre_gain=m_mlp_pre_gain, mlp_post_gain=m_mlp_post_gain)
    small_v = dict(attn_sinks=v_attn_sinks, attn_out_gain=v_attn_out_gain, rnn_lb_logits=v_rnn_lb_logits,
                   rnn_norm_gain=v_rnn_norm_gain, mix_pre_gain=v_mix_pre_gain, mix_post_gain=v_mix_post_gain,
                   mlp_pre_gain=v_mlp_pre_gain, mlp_post_gain=v_mlp_post_gain)
    cx, cy, cc = _coords()
    sels = {a: jnp.reshape(v_, (1,)).astype(jnp.int32) for a, v_ in (("x", cx), ("y", cy), ("c", cc))}

    w_in_t, m_in_t, v_in_t = w_in[0].T, m_w_in[0].T, v_w_in[0].T
    s_in, s_out, s_up, s_down = (w.astype(BF16) for w in (w_in_t, w_out[0], w_up[0], w_down[0]))
    probs = jax.nn.softmax(rnn_lb_logits.astype(F32), axis=0)
    lb = probs[0:1]

    (h1,), (wint_part,) = pre_norm(xs, mix_pre_gain, carry=_gather_first(s_in, diagonal=False))
    in_rows = s_in.shape[0]
    wint = comm_only("gather_rest_w_in", _pass_slabs(
        wint_part,
        [(_X, _Y, (0, in_rows // 2)), (_Y, _X, (in_rows // 2, in_rows)), (_X, _C, None), (_Y, _C, None)],
        then=[(_XY, _C, None)]))[0].reshape(-1, d)
    up_rows = s_up.shape[0]
    up_cut = up_rows * 9 // 16
    proj, (wup_part,) = mm_nt("in_proj", h1, wint, F32, tn=2 * MM_TILE,
                              carry=_gather_first(s_up, rows=(0, up_cut)))
    (attn_o, attn_n), (wup_half, wout_half) = attn_fwd(
        proj, attn_sinks, attn_out_gain, aw,
        carry=_merge(_gather_first(s_up, rows=(up_cut, up_rows), into=wup_part), _gather_first(s_out)))
    (cat, o_r, att, st), (wup, wout, wdown_half) = hgrn_fwd(
        proj, attn_n, lb, rnn_norm_gain, col0, rw,
        carry=_merge(_gather_second(wup_half), _gather_second(wout_half), _gather_first(s_down)))
    wout = wout.reshape(-1, d)
    mixed, (wdown,) = mm_nn("out_proj", cat, wout, F32, carry=_gather_second(wdown_half))
    wdown = wdown.reshape(-1, d)
    x1, h2 = mid_fwd(mixed, mix_post_gain, xs, mlp_pre_gain)
    u = up_proj(h2, wup)
    y = down_proj(u, wdown)
    sse, dout, dy, dg_mlppost = loss_bwd(y, mlp_post_gain, x1, target)

    du = down_bwd_act(dy, wdown, u)
    rs_down = _Scatter("down", down_wgrad(u, dy).reshape(N_DEV, -1, d), sels, both_links=True)
    dh2, (got,) = up_bwd_x(du, wup, carry=rs_down.step())
    rs_down.land(got)
    dwup, gots = up_wgrad(h2, du, carry=rs_down.step())
    rs_down.land(gots)
    rs_up = _Scatter("up", dwup, sels, both_links=True)
    (dx1, dmixed, dg_mlppre, dg_mixpost), (got,) = mid_bwd(dh2, x1, mlp_pre_gain, dout, mixed, mix_post_gain,
                                                          carry=rs_up.step())
    rs_up.land(got)
    dcat = mm_nt("out_bwd_x", dmixed, wout, F32)
    rs_out = _Scatter("out", mm_tn("out_wgrad", cat, dmixed, BF16).reshape(N_DEV, -1, d), sels)
    (dq_a, dk_a, dv_a, dsinks, daog), (*gots, got_o) = attn_bwd(
        proj, attn_sinks, attn_out_gain, attn_o, dcat, aw, carry=_merge(rs_down.step(), rs_out.step()))
    out_down = rs_down.land(gots, w_down[0], m_w_down[0], v_w_down[0])
    rs_out.land(got_o)
    (dproj, dlb, dng), (*gots, got_o) = hgrn_bwd(
        proj, lb, rnn_norm_gain, o_r, att, st, dcat, dq_a, dk_a, dv_a, col0, rw,
        carry=_merge(rs_up.step(), rs_out.step()))
    rs_up.land(gots)
    rs_out.land(got_o)
    dwin, (*gots, got_o) = mm_tn("in_wgrad", dproj, h1, BF16, carry=_merge(rs_up.step(), rs_out.step()))
    out_up = rs_up.land(gots, w_up[0], m_w_up[0], v_w_up[0])
    out_out = rs_out.land(got_o, w_out[0], m_w_out[0], v_w_out[0])
    rs_in = _Scatter("in", dwin.reshape(N_DEV, -1, d), sels, both_links=True)
    rs_in.land(comm_only("rs_exchange_c_in", rs_in.step())[0])
    dh1, gots = mm_nn("in_bwd_x", dproj, wint, F32, tm=MM_TILE // 2, carry=rs_in.step())
    rs_in.land(gots)
    (grad_x, dg_mixpre), gots = first_bwd(dh1, xs, mix_pre_gain, dx1, carry=rs_in.step())
    out_in = rs_in.land(gots, w_in_t, m_in_t, v_in_t)
    big_out = [out_in, out_out, out_up, out_down]

    n_heads = attn_sinks.shape[1]
    jac = probs[0] * probs[1]
    partial = _pack([sse, dsinks[0, :n_heads], daog, jnp.stack([dlb[0], dlb[0]]), jnp.sum(dng, axis=0),
                     dg_mixpre, dg_mixpost, dg_mlppre, dg_mlppost])
    ones = [jnp.ones(small_w[k].shape, F32) for k in _SMALL]
    ones[2] = jnp.stack([jac, -jac])
    scale = _pack([jnp.full((1,), 0.5 / d, F32)] + ones)
    zero = jnp.zeros((1,), F32)
    outs = small_allreduce_adamw(partial, scale, _pack([zero] + [small_w[k] for k in _SMALL]),
                                 _pack([zero] + [small_m[k] for k in _SMALL]),
                                 _pack([jnp.ones((1,), F32)] + [small_v[k] for k in _SMALL]))
    shapes = [(1,)] + [small_w[k].shape for k in _SMALL]
    sgrad, sdelta, snm, snv = (_unpack(o, shapes) for o in outs)
    loss = sgrad[0][0]

    def big(i, j):
        o = big_out[i][j]
        return (o.T if i == 0 else o)[None]

    def ordered(j, smalls):
        s = dict(zip(_SMALL, smalls[1:]))
        return [big(0, j), s["attn_sinks"], s["attn_out_gain"], s["rnn_lb_logits"], s["rnn_norm_gain"], big(1, j),
                s["mix_pre_gain"], s["mix_post_gain"], s["mlp_pre_gain"], s["mlp_post_gain"], big(2, j), big(3, j)]

    return (loss, grad_x[None], *ordered(0, sgrad), *ordered(1, sdelta), *ordered(2, snm), *ordered(3, snv))
```
